```python
import math
import jax, jax.numpy as jnp
from jax import lax
import numpy as np

D_MODEL = 2048
BATCH = 8
SEQ = 2048
DEPTH = 1

SGU_GROUP_DIM = 128
SGU_WIDTH = D_MODEL // 2
SGU_GROUPS = SGU_WIDTH // SGU_GROUP_DIM
CHUNK = 128
HEAD_DIM = 128
N_HEADS = (D_MODEL // 2) // HEAD_DIM
N_KV_HEADS = 2
GQA_GROUP = N_HEADS // N_KV_HEADS
ATT_WIDTH = N_HEADS * HEAD_DIM
KV_WIDTH = N_KV_HEADS * HEAD_DIM
WINDOW = 128
BLOCK = 128
REL_BUCKETS = 32
REL_MAX_DIST = 128
D_FF = ((8 * D_MODEL // 3 + 255) // 256) * 256
EPS = 1e-6
NEG = -1e30

IN_SPLITS = [SGU_WIDTH, SGU_WIDTH, ATT_WIDTH, KV_WIDTH, KV_WIDTH, D_MODEL, D_MODEL]
IN_COLS = int(sum(IN_SPLITS))
IN_OFFSETS = [int(o) for o in np.cumsum(IN_SPLITS)[:-1]]

kernel_name = "hybrid_sgu_swa_gated_encoder"


def rms_norm(x, g):
    xf = x.astype(jnp.float32)
    y = xf * lax.rsqrt(jnp.mean(xf * xf, axis=-1, keepdims=True) + EPS)
    return (y * g.astype(jnp.float32)).astype(x.dtype)


def t5_bucket(rel):
    nb = REL_BUCKETS // 2
    ret = jnp.where(rel > 0, nb, 0)
    n = jnp.abs(rel)
    max_exact = nb // 2
    nf = jnp.maximum(n, 1).astype(jnp.float32)
    large = max_exact + (jnp.log(nf / max_exact) / math.log(REL_MAX_DIST / max_exact)
                         * (nb - max_exact)).astype(jnp.int32)
    large = jnp.minimum(large, nb - 1)
    return ret + jnp.where(n < max_exact, n, large)


def band_structure(seq):
    nblk = seq // BLOCK
    qi = jnp.arange(BLOCK)[:, None]
    kj = jnp.arange(3 * BLOCK)[None, :]
    rel = kj - BLOCK - qi
    key_pos = jnp.arange(nblk)[:, None, None] * BLOCK + kj[None] - BLOCK
    valid = (jnp.abs(rel)[None] <= WINDOW) & (key_pos >= 0) & (key_pos < seq)
    return rel, valid


def band(t, nblk):
    tp = jnp.pad(t, ((0, 0), (BLOCK, BLOCK), (0, 0), (0, 0)))
    tp = tp.reshape(t.shape[0], nblk + 2, BLOCK, t.shape[2], t.shape[3])
    return jnp.concatenate([tp[:, :-2], tp[:, 1:-1], tp[:, 2:]], axis=2)


def windowed_gqa(q, k, v, rel_bias, sink):
    B, S = q.shape[0], q.shape[1]
    nb = S // BLOCK
    q = q.reshape(B, nb, BLOCK, N_KV_HEADS, GQA_GROUP, HEAD_DIM)
    kb = band(k.reshape(B, S, N_KV_HEADS, HEAD_DIM), nb)
    vb = band(v.reshape(B, S, N_KV_HEADS, HEAD_DIM), nb)
    s = jnp.einsum('bnqkgd,bnjkd->bnkgqj', q, kb).astype(jnp.float32) * (HEAD_DIM ** -0.5)
    rel, valid = band_structure(S)
    bias = rel_bias.astype(jnp.float32)[t5_bucket(rel)]
    bias = bias.transpose(2, 0, 1).reshape(N_KV_HEADS, GQA_GROUP, BLOCK, 3 * BLOCK)
    s = jnp.where(valid[None, :, None, None], s + bias, NEG)
    sink_logit = jnp.broadcast_to(
        sink.astype(jnp.float32).reshape(N_KV_HEADS, GQA_GROUP)[None, None, :, :, None, None],
        s.shape[:-1] + (1,))
    p = jax.nn.softmax(jnp.concatenate([s, sink_logit], axis=-1), axis=-1)[..., :-1]
    o = jnp.einsum('bnkgqj,bnjkd->bnqkgd', p.astype(vb.dtype), vb)
    return o.reshape(B, S, ATT_WIDTH)


def chunked_sgu(u, v, v_gain, w_s, b_s):
    B, S = u.shape[0], u.shape[1]
    nc = S // CHUNK
    v = rms_norm(v, v_gain).reshape(B, nc, CHUNK, SGU_GROUPS, SGU_GROUP_DIM)
    mixed = jnp.einsum('gpq,bcqge->bcpge', w_s, v) + b_s.T[:, :, None]
    return u * mixed.reshape(B, S, SGU_WIDTH)


def _fwd_setup_inputs(seed: int = 0) -> dict:
    key = jax.random.key(seed)
    ks = jax.random.split(key, 20)
    f32 = jnp.float32

    def nrm(k, shape, scale):
        return jax.random.normal(k, shape, f32) * scale

    return {
        "x": nrm(ks[0], (BATCH, SEQ, D_MODEL), 1.0),
        "w_in": nrm(ks[1], (DEPTH, D_MODEL, IN_COLS), D_MODEL ** -0.5),
        "norm_mix": 1.0 + nrm(ks[2], (DEPTH, D_MODEL), 0.05),
        "sgu_v_gain": 1.0 + nrm(ks[3], (DEPTH, SGU_WIDTH), 0.05),
        "sgu_w_s": nrm(ks[4], (DEPTH, SGU_GROUPS, CHUNK, CHUNK), 0.5 * CHUNK ** -0.5),
        "sgu_b_s": 1.0 + nrm(ks[5], (DEPTH, SGU_GROUPS, CHUNK), 0.1),
        "w_a_out": nrm(ks[6], (DEPTH, SGU_WIDTH, D_MODEL), SGU_WIDTH ** -0.5),
        "attn_sink": nrm(ks[7], (DEPTH, N_HEADS), 0.5),
        "rel_bias": nrm(ks[8], (REL_BUCKETS, N_HEADS), 0.5),
        "w_b_out": nrm(ks[9], (DEPTH, ATT_WIDTH, D_MODEL), ATT_WIDTH ** -0.5),
        "w_o": nrm(ks[10], (DEPTH, D_MODEL, D_MODEL), D_MODEL ** -0.5),
        "norm_ffn": 1.0 + nrm(ks[11], (DEPTH, D_MODEL), 0.05),
        "w_gate": nrm(ks[12], (DEPTH, D_MODEL, D_FF), D_MODEL ** -0.5),
        "w_up": nrm(ks[13], (DEPTH, D_MODEL, D_FF), D_MODEL ** -0.5),
        "w_down": nrm(ks[14], (DEPTH, D_FF, D_MODEL), D_FF ** -0.5),
        "norm_final": 1.0 + nrm(ks[15], (D_MODEL,), 0.05),
    }


def _fwd_reference(x, w_in, norm_mix, sgu_v_gain, sgu_w_s, sgu_b_s, w_a_out, attn_sink, rel_bias,
              w_b_out, w_o, norm_ffn, w_gate, w_up, w_down, norm_final):
    for l in range(DEPTH):
        h = rms_norm(x, norm_mix[l])
        z = h @ w_in[l]
        zu, zv, q, k, v, ga, gb = jnp.split(z, IN_OFFSETS, axis=-1)
        y_a = chunked_sgu(jax.nn.gelu(zu), jax.nn.gelu(zv), sgu_v_gain[l],
                          sgu_w_s[l], sgu_b_s[l]) @ w_a_out[l]
        y_b = windowed_gqa(q, k, v, rel_bias, attn_sink[l]) @ w_b_out[l]
        m = jax.nn.sigmoid(ga) * y_a + jax.nn.sigmoid(gb) * y_b
        x = x + m @ w_o[l]
        h = rms_norm(x, norm_ffn[l])
        x = x + (jax.nn.silu(h @ w_gate[l]) * (h @ w_up[l])) @ w_down[l]
    return rms_norm(x, norm_final)


import jax as _jax
import jax.numpy as _jnp

TWIN_FORMAT = 'train_step'
FWD_PARAMS = ['x', 'w_in', 'norm_mix', 'sgu_v_gain', 'sgu_w_s', 'sgu_b_s', 'w_a_out', 'attn_sink', 'rel_bias', 'w_b_out', 'w_o', 'norm_ffn', 'w_gate', 'w_up', 'w_down', 'norm_final']
TWIN_WEIGHTS = ['w_in', 'norm_mix', 'sgu_v_gain', 'sgu_w_s', 'sgu_b_s', 'w_a_out', 'attn_sink', 'rel_bias', 'w_b_out', 'w_o', 'norm_ffn', 'w_gate', 'w_up', 'w_down', 'norm_final']
TWIN_DIFF_INPUT = 'x'
TWIN_INPUTS = ['x', 'w_in', 'norm_mix', 'sgu_v_gain', 'sgu_w_s', 'sgu_b_s', 'w_a_out', 'attn_sink', 'rel_bias', 'w_b_out', 'w_o', 'norm_ffn', 'w_gate', 'w_up', 'w_down', 'norm_final', 'loss_target', 'm_w_in', 'm_norm_mix', 'm_sgu_v_gain', 'm_sgu_w_s', 'm_sgu_b_s', 'm_w_a_out', 'm_attn_sink', 'm_rel_bias', 'm_w_b_out', 'm_w_o', 'm_norm_ffn', 'm_w_gate', 'm_w_up', 'm_w_down', 'm_norm_final', 'v_w_in', 'v_norm_mix', 'v_sgu_v_gain', 'v_sgu_w_s', 'v_sgu_b_s', 'v_w_a_out', 'v_attn_sink', 'v_rel_bias', 'v_w_b_out', 'v_w_o', 'v_norm_ffn', 'v_w_gate', 'v_w_up', 'v_w_down', 'v_norm_final']
TWIN_OUTPUTS = ['loss', 'grad_x', 'grad_w_in', 'grad_norm_mix', 'grad_sgu_v_gain', 'grad_sgu_w_s', 'grad_sgu_b_s', 'grad_w_a_out', 'grad_attn_sink', 'grad_rel_bias', 'grad_w_b_out', 'grad_w_o', 'grad_norm_ffn', 'grad_w_gate', 'grad_w_up', 'grad_w_down', 'grad_norm_final', 'delta_w_in', 'delta_norm_mix', 'delta_sgu_v_gain', 'delta_sgu_w_s', 'delta_sgu_b_s', 'delta_w_a_out', 'delta_attn_sink', 'delta_rel_bias', 'delta_w_b_out', 'delta_w_o', 'delta_norm_ffn', 'delta_w_gate', 'delta_w_up', 'delta_w_down', 'delta_norm_final', 'new_m_w_in', 'new_m_norm_mix', 'new_m_sgu_v_gain', 'new_m_sgu_w_s', 'new_m_sgu_b_s', 'new_m_w_a_out', 'new_m_attn_sink', 'new_m_rel_bias', 'new_m_w_b_out', 'new_m_w_o', 'new_m_norm_ffn', 'new_m_w_gate', 'new_m_w_up', 'new_m_w_down', 'new_m_norm_final', 'new_v_w_in', 'new_v_norm_mix', 'new_v_sgu_v_gain', 'new_v_sgu_w_s', 'new_v_sgu_b_s', 'new_v_w_a_out', 'new_v_attn_sink', 'new_v_rel_bias', 'new_v_w_b_out', 'new_v_w_o', 'new_v_norm_ffn', 'new_v_w_gate', 'new_v_w_up', 'new_v_w_down', 'new_v_norm_final']
TWIN_LEAF_KINDS = {'loss': 'loss', 'grad_x': 'grad_x', 'grad_w_in': 'grad_w', 'grad_norm_mix': 'grad_w', 'grad_sgu_v_gain': 'grad_w', 'grad_sgu_w_s': 'grad_w', 'grad_sgu_b_s': 'grad_w', 'grad_w_a_out': 'grad_w', 'grad_attn_sink': 'grad_w', 'grad_rel_bias': 'grad_w', 'grad_w_b_out': 'grad_w', 'grad_w_o': 'grad_w', 'grad_norm_ffn': 'grad_w', 'grad_w_gate': 'grad_w', 'grad_w_up': 'grad_w', 'grad_w_down': 'grad_w', 'grad_norm_final': 'grad_w', 'delta_w_in': 'delta_w', 'delta_norm_mix': 'delta_w', 'delta_sgu_v_gain': 'delta_w', 'delta_sgu_w_s': 'delta_w', 'delta_sgu_b_s': 'delta_w', 'delta_w_a_out': 'delta_w', 'delta_attn_sink': 'delta_w', 'delta_rel_bias': 'delta_w', 'delta_w_b_out': 'delta_w', 'delta_w_o': 'delta_w', 'delta_norm_ffn': 'delta_w', 'delta_w_gate': 'delta_w', 'delta_w_up': 'delta_w', 'delta_w_down': 'delta_w', 'delta_norm_final': 'delta_w', 'new_m_w_in': 'new_m', 'new_m_norm_mix': 'new_m', 'new_m_sgu_v_gain': 'new_m', 'new_m_sgu_w_s': 'new_m', 'new_m_sgu_b_s': 'new_m', 'new_m_w_a_out': 'new_m', 'new_m_attn_sink': 'new_m', 'new_m_rel_bias': 'new_m', 'new_m_w_b_out': 'new_m', 'new_m_w_o': 'new_m', 'new_m_norm_ffn': 'new_m', 'new_m_w_gate': 'new_m', 'new_m_w_up': 'new_m', 'new_m_w_down': 'new_m', 'new_m_norm_final': 'new_m', 'new_v_w_in': 'new_v', 'new_v_norm_mix': 'new_v', 'new_v_sgu_v_gain': 'new_v', 'new_v_sgu_w_s': 'new_v', 'new_v_sgu_b_s': 'new_v', 'new_v_w_a_out': 'new_v', 'new_v_attn_sink': 'new_v', 'new_v_rel_bias': 'new_v', 'new_v_w_b_out': 'new_v', 'new_v_w_o': 'new_v', 'new_v_norm_ffn': 'new_v', 'new_v_w_gate': 'new_v', 'new_v_w_up': 'new_v', 'new_v_w_down': 'new_v', 'new_v_norm_final': 'new_v'}


def _forward(args):
    return _fwd_reference(*[args[k] for k in FWD_PARAMS])


def _output_shape():
    out = _jax.eval_shape(lambda: _forward(_fwd_setup_inputs(0)))
    return out.shape, out.dtype

N_MICROBATCH = 1
ADAM_LR = 0.001
ADAM_B1 = 0.9
ADAM_B2 = 0.999
ADAM_EPS = 1e-08
ADAM_WD = 0.01
ADAM_STEP = 10
PER_EXAMPLE_BATCH_AXIS = {'x': 0, 'loss_target': 0}
SHARED_INPUTS = []
_WEIGHT_DTYPES = {'w_in': _jnp.float32, 'norm_mix': _jnp.float32, 'sgu_v_gain': _jnp.float32, 'sgu_w_s': _jnp.float32, 'sgu_b_s': _jnp.float32, 'w_a_out': _jnp.float32, 'attn_sink': _jnp.float32, 'rel_bias': _jnp.float32, 'w_b_out': _jnp.float32, 'w_o': _jnp.float32, 'norm_ffn': _jnp.float32, 'w_gate': _jnp.float32, 'w_up': _jnp.float32, 'w_down': _jnp.float32, 'norm_final': _jnp.float32}
MOMENT_SCALE = {'w_in': 1.674886e-02, 'norm_mix': 3.304397e-02, 'sgu_v_gain': 1.638961e-02, 'sgu_w_s': 3.272604e-02, 'sgu_b_s': 3.304814e-02, 'w_a_out': 2.764525e-02, 'attn_sink': 2.313076e-04, 'rel_bias': 1.281201e-02, 'w_b_out': 4.619124e-03, 'w_o': 2.791234e-02, 'norm_ffn': 4.520643e-02, 'w_gate': 1.897629e-02, 'w_up': 1.844212e-02, 'w_down': 3.063385e-02, 'norm_final': 7.997139e+00}


def _to_microbatches(a, axis):
    t = _jnp.moveaxis(a, axis, 0)
    t = t.reshape((N_MICROBATCH, t.shape[0] // N_MICROBATCH) + t.shape[1:])
    return _jnp.moveaxis(t, 1, axis + 1)


def setup_inputs(seed: int = 0) -> dict:
    inp = _fwd_setup_inputs(seed)
    key = _jax.random.fold_in(_jax.random.key(seed), 7919)
    shape, _ = _output_shape()
    out = dict(inp)
    out["loss_target"] = _jax.random.normal(_jax.random.fold_in(key, 0), shape, _jnp.float32)
    for i, name in enumerate(TWIN_WEIGHTS):
        w = inp[name].astype(_jnp.float32)
        if MOMENT_SCALE is None:
            s = _jnp.sqrt(_jnp.mean(_jnp.square(w)) + 1e-30)
        else:
            s = MOMENT_SCALE[name]
        km, kv = _jax.random.split(_jax.random.fold_in(key, i + 1))
        out[name] = w
        out["m_" + name] = s * _jax.random.normal(km, w.shape, _jnp.float32)
        out["v_" + name] = (s * s) * _jax.random.uniform(kv, w.shape, _jnp.float32, 0.5, 1.5)
    if N_MICROBATCH > 1:
        for name, axis in PER_EXAMPLE_BATCH_AXIS.items():
            out[name] = _to_microbatches(out[name], axis)
    return {'x': out['x'], 'w_in': out['w_in'], 'norm_mix': out['norm_mix'], 'sgu_v_gain': out['sgu_v_gain'], 'sgu_w_s': out['sgu_w_s'], 'sgu_b_s': out['sgu_b_s'], 'w_a_out': out['w_a_out'], 'attn_sink': out['attn_sink'], 'rel_bias': out['rel_bias'], 'w_b_out': out['w_b_out'], 'w_o': out['w_o'], 'norm_ffn': out['norm_ffn'], 'w_gate': out['w_gate'], 'w_up': out['w_up'], 'w_down': out['w_down'], 'norm_final': out['norm_final'], 'loss_target': out['loss_target'], 'm_w_in': out['m_w_in'], 'm_norm_mix': out['m_norm_mix'], 'm_sgu_v_gain': out['m_sgu_v_gain'], 'm_sgu_w_s': out['m_sgu_w_s'], 'm_sgu_b_s': out['m_sgu_b_s'], 'm_w_a_out': out['m_w_a_out'], 'm_attn_sink': out['m_attn_sink'], 'm_rel_bias': out['m_rel_bias'], 'm_w_b_out': out['m_w_b_out'], 'm_w_o': out['m_w_o'], 'm_norm_ffn': out['m_norm_ffn'], 'm_w_gate': out['m_w_gate'], 'm_w_up': out['m_w_up'], 'm_w_down': out['m_w_down'], 'm_norm_final': out['m_norm_final'], 'v_w_in': out['v_w_in'], 'v_norm_mix': out['v_norm_mix'], 'v_sgu_v_gain': out['v_sgu_v_gain'], 'v_sgu_w_s': out['v_sgu_w_s'], 'v_sgu_b_s': out['v_sgu_b_s'], 'v_w_a_out': out['v_w_a_out'], 'v_attn_sink': out['v_attn_sink'], 'v_rel_bias': out['v_rel_bias'], 'v_w_b_out': out['v_w_b_out'], 'v_w_o': out['v_w_o'], 'v_norm_ffn': out['v_norm_ffn'], 'v_w_gate': out['v_w_gate'], 'v_w_up': out['v_w_up'], 'v_w_down': out['v_w_down'], 'v_norm_final': out['v_norm_final']}


def _loss(weights, diff, rest, loss_target):
    with _jax.named_scope("forward"):
        args = {**rest, TWIN_DIFF_INPUT: diff, **{k: w.astype(_WEIGHT_DTYPES[k]) for k, w in weights.items()}}
        y = _forward(args)
    with _jax.named_scope("loss_head"):
        err = _jnp.square(y.astype(_jnp.float32) - loss_target)
        return 0.5 * _jnp.sum(_jnp.mean(err, axis=-1)) if err.ndim else 0.5 * err


def _adamw(w, g, m, v):
    m = ADAM_B1 * m + (1.0 - ADAM_B1) * g
    v = ADAM_B2 * v + (1.0 - ADAM_B2) * _jnp.square(g)
    m_hat = m / (1.0 - ADAM_B1 ** ADAM_STEP)
    v_hat = v / (1.0 - ADAM_B2 ** ADAM_STEP)
    delta = -ADAM_LR * (m_hat / (_jnp.sqrt(v_hat) + ADAM_EPS) + ADAM_WD * w)
    return delta, m, v


def reference(x, w_in, norm_mix, sgu_v_gain, sgu_w_s, sgu_b_s, w_a_out, attn_sink, rel_bias, w_b_out, w_o, norm_ffn, w_gate, w_up, w_down, norm_final, loss_target, m_w_in, m_norm_mix, m_sgu_v_gain, m_sgu_w_s, m_sgu_b_s, m_w_a_out, m_attn_sink, m_rel_bias, m_w_b_out, m_w_o, m_norm_ffn, m_w_gate, m_w_up, m_w_down, m_norm_final, v_w_in, v_norm_mix, v_sgu_v_gain, v_sgu_w_s, v_sgu_b_s, v_w_a_out, v_attn_sink, v_rel_bias, v_w_b_out, v_w_o, v_norm_ffn, v_w_gate, v_w_up, v_w_down, v_norm_final):
    given = dict(x=x, w_in=w_in, norm_mix=norm_mix, sgu_v_gain=sgu_v_gain, sgu_w_s=sgu_w_s, sgu_b_s=sgu_b_s, w_a_out=w_a_out, attn_sink=attn_sink, rel_bias=rel_bias, w_b_out=w_b_out, w_o=w_o, norm_ffn=norm_ffn, w_gate=w_gate, w_up=w_up, w_down=w_down, norm_final=norm_final, loss_target=loss_target, m_w_in=m_w_in, m_norm_mix=m_norm_mix, m_sgu_v_gain=m_sgu_v_gain, m_sgu_w_s=m_sgu_w_s, m_sgu_b_s=m_sgu_b_s, m_w_a_out=m_w_a_out, m_attn_sink=m_attn_sink, m_rel_bias=m_rel_bias, m_w_b_out=m_w_b_out, m_w_o=m_w_o, m_norm_ffn=m_norm_ffn, m_w_gate=m_w_gate, m_w_up=m_w_up, m_w_down=m_w_down, m_norm_final=m_norm_final, v_w_in=v_w_in, v_norm_mix=v_norm_mix, v_sgu_v_gain=v_sgu_v_gain, v_sgu_w_s=v_sgu_w_s, v_sgu_b_s=v_sgu_b_s, v_w_a_out=v_w_a_out, v_attn_sink=v_attn_sink, v_rel_bias=v_rel_bias, v_w_b_out=v_w_b_out, v_w_o=v_w_o, v_norm_ffn=v_norm_ffn, v_w_gate=v_w_gate, v_w_up=v_w_up, v_w_down=v_w_down, v_norm_final=v_norm_final)
    weights = {n: given[n] for n in TWIN_WEIGHTS}
    shared = {n: given[n] for n in SHARED_INPUTS}
    per_example = {n: given[n] for n in ['x']}
    grad_fn = _jax.value_and_grad(_loss, argnums=(0, 1))

    def one_microbatch(ex, loss_target):
        ex = dict(ex)
        diff = ex.pop(TWIN_DIFF_INPUT)
        return grad_fn(weights, diff, {**shared, **ex}, loss_target)

    if N_MICROBATCH == 1:
        loss, (grad_w, grad_x) = one_microbatch(per_example, given["loss_target"])
    else:
        def body(carry, xs):
            loss_sum, grad_sum = carry
            l_k, (gw_k, gx_k) = one_microbatch(xs[0], xs[1])
            with _jax.named_scope("update"):
                return (loss_sum + l_k, _jax.tree.map(_jnp.add, grad_sum, gw_k)), gx_k

        init = (_jnp.zeros((), _jnp.float32), _jax.tree.map(_jnp.zeros_like, weights))
        (loss, grad_w), grad_x = _jax.lax.scan(body, init, (per_example, given["loss_target"]))
    with _jax.named_scope("update"):
        delta_w, new_m, new_v = {}, {}, {}
        for n in TWIN_WEIGHTS:
            delta_w[n], new_m[n], new_v[n] = _adamw(weights[n], grad_w[n], given["m_" + n], given["v_" + n])
    return (loss, grad_x, *[grad_w[n] for n in TWIN_WEIGHTS], *[delta_w[n] for n in TWIN_WEIGHTS],
            *[new_m[n] for n in TWIN_WEIGHTS], *[new_v[n] for n in TWIN_WEIGHTS])
```

```python
import functools
import math

import numpy as np
import jax
import jax.numpy as jnp
from jax import lax
from jax.experimental import pallas as pl
from jax.experimental.pallas import tpu as pltpu

F32 = jnp.float32
BF16 = jnp.bfloat16

EPS = 1e-6
NEG = -1e30
HEAD_DIM = 128
BLOCK = 128
N_KV_HEADS = 2
KV_WIDTH = N_KV_HEADS * HEAD_DIM
REL_BUCKETS = 32
REL_MAX_DIST = 128

ADAM_LR = 0.001
ADAM_B1 = 0.9
ADAM_B2 = 0.999
ADAM_EPS = 1e-08
ADAM_WD = 0.01
ADAM_STEP = 10

N_DEV = 8
LANES = 128
VMEM_LIMIT = 56 * 1024 * 1024
MESH = pl.DeviceIdType.MESH


def _cparams(*sem):
    return pltpu.CompilerParams(dimension_semantics=sem, vmem_limit_bytes=VMEM_LIMIT)


def _div(n, target, mult=LANES):
    best = None
    for d in range(mult, min(n, target) + 1, mult):
        if n % d == 0:
            best = d
    assert best is not None, (n, target, mult)
    return best


def _bucket_map():
    nb = REL_BUCKETS // 2
    qi = np.arange(BLOCK)[:, None]
    kj = np.arange(3 * BLOCK)[None, :]
    rel = kj - BLOCK - qi
    ret = np.where(rel > 0, nb, 0)
    n = np.abs(rel)
    max_exact = nb // 2
    nf = np.maximum(n, 1).astype(np.float32)
    large = max_exact + (np.log(nf / np.float32(max_exact)) / np.float32(math.log(REL_MAX_DIST / max_exact))
                         * np.float32(nb - max_exact)).astype(np.int32)
    large = np.minimum(large, nb - 1)
    return (ret + np.where(n < max_exact, n, large)).astype(np.int32)


_GELU_C = math.sqrt(2.0 / math.pi)
_GELU_A = 0.044715


def _gelu(x):
    t = jnp.tanh(_GELU_C * (x + _GELU_A * (x * x * x)))
    return 0.5 * x * (1.0 + t)


def _gelu_and_grad(x):
    x2 = x * x
    t = jnp.tanh(_GELU_C * (x + _GELU_A * (x2 * x)))
    g = 0.5 * x * (1.0 + t)
    dg = 0.5 * (1.0 + t) + 0.5 * x * (1.0 - t * t) * (_GELU_C * (1.0 + 3.0 * _GELU_A * x2))
    return g, dg


def _sigmoid(x):
    return 1.0 / (1.0 + jnp.exp(-x))


def _mm(a, b, *, name, ta=False, tb=False, add=None, out_dtype=F32, bm=1024, bn=1024, bk=None):
    if ta:
        K, M = a.shape
    else:
        M, K = a.shape
    N = b.shape[0] if tb else b.shape[1]
    assert (b.shape[1] if tb else b.shape[0]) == K
    bm = _div(M, bm)
    bn = _div(N, bn)
    bk = K if bk is None else _div(K, bk)
    nk = K // bk
    a_spec = pl.BlockSpec((bk, bm), lambda i, j, k: (k, i)) if ta else pl.BlockSpec((bm, bk), lambda i, j, k: (i, k))
    b_spec = pl.BlockSpec((bn, bk), lambda i, j, k: (j, k)) if tb else pl.BlockSpec((bk, bn), lambda i, j, k: (k, j))
    o_spec = pl.BlockSpec((bm, bn), lambda i, j, k: (i, j))
    dims = (((0 if ta else 1,), (1 if tb else 0,)), ((), ()))
    has_add = add is not None

    def body(*refs):
        if has_add:
            a_ref, b_ref, add_ref, o_ref, *scratch = refs
        else:
            a_ref, b_ref, o_ref, *scratch = refs
            add_ref = None
        p = lax.dot_general(a_ref[...].astype(BF16), b_ref[...].astype(BF16), dims, preferred_element_type=F32)
        if nk == 1:
            if has_add:
                p = p + add_ref[...]
            o_ref[...] = p.astype(out_dtype)
        else:
            acc = scratch[0]
            k = pl.program_id(2)

            @pl.when(k == 0)
            def _():
                acc[...] = p

            @pl.when(k > 0)
            def _():
                acc[...] += p

            @pl.when(k == nk - 1)
            def _():
                r = acc[...]
                if has_add:
                    r = r + add_ref[...]
                o_ref[...] = r.astype(out_dtype)

    in_specs = [a_spec, b_spec] + ([o_spec] if has_add else [])
    args = (a, b) + ((add,) if has_add else ())
    return pl.pallas_call(
        body, name=name, grid=(M // bm, N // bn, nk),
        in_specs=in_specs, out_specs=o_spec,
        out_shape=jax.ShapeDtypeStruct((M, N), out_dtype),
        scratch_shapes=[pltpu.VMEM((bm, bn), F32)] if nk > 1 else [],
        compiler_params=_cparams("parallel", "parallel", "arbitrary"),
    )(*args)


def _rms_fwd(x, g, *, name):
    T, D = x.shape
    tm = _div(T, 256, 8)

    def body(x_ref, g_ref, h_ref):
        xf = x_ref[...]
        r = lax.rsqrt(jnp.mean(xf * xf, axis=-1, keepdims=True) + EPS)
        h_ref[...] = ((xf * r) * g_ref[...]).astype(BF16)

    return pl.pallas_call(
        body, name=name, grid=(T // tm,),
        in_specs=[pl.BlockSpec((tm, D), lambda i: (i, 0)), pl.BlockSpec((1, D), lambda i: (0, 0))],
        out_specs=pl.BlockSpec((tm, D), lambda i: (i, 0)),
        out_shape=jax.ShapeDtypeStruct((T, D), BF16),
        compiler_params=_cparams("parallel"),
    )(x, g)


def _rms_bwd(x, g, dh, dres, *, name, want_bf16):
    T, D = x.shape
    tm = _div(T, 256, 8)

    def body(x_ref, g_ref, dh_ref, dres_ref, dx_ref, *rest):
        if want_bf16:
            dxb_ref, dg_ref = rest
        else:
            (dg_ref,) = rest
        xf = x_ref[...]
        r = lax.rsqrt(jnp.mean(xf * xf, axis=-1, keepdims=True) + EPS)
        xhat = xf * r
        dh_ = dh_ref[...]
        dy = dh_ * g_ref[...]
        dx = dres_ref[...] + r * (dy - xhat * jnp.mean(dy * xhat, axis=-1, keepdims=True))
        dx_ref[...] = dx
        if want_bf16:
            dxb_ref[...] = dx.astype(BF16)
        part = jnp.sum(dh_ * xhat, axis=0, keepdims=True)

        @pl.when(pl.program_id(0) == 0)
        def _():
            dg_ref[...] = part

        @pl.when(pl.program_id(0) > 0)
        def _():
            dg_ref[...] += part

    row = pl.BlockSpec((tm, D), lambda i: (i, 0))
    vec = pl.BlockSpec((1, D), lambda i: (0, 0))
    out_specs = [row] + ([row] if want_bf16 else []) + [vec]
    out_shape = ([jax.ShapeDtypeStruct((T, D), F32)] + ([jax.ShapeDtypeStruct((T, D), BF16)] if want_bf16 else [])
                 + [jax.ShapeDtypeStruct((1, D), F32)])
    return pl.pallas_call(
        body, name=name, grid=(T // tm,),
        in_specs=[row, vec, row, row], out_specs=out_specs, out_shape=out_shape,
        compiler_params=_cparams("arbitrary"),
    )(x, g, dh, dres)


def _loss_head(x, g, target, *, name):
    T, D = x.shape
    tm = _div(T, 256, 8)

    def body(x_ref, g_ref, t_ref, loss_ref, dx_ref, dxb_ref, dg_ref):
        xf = x_ref[...]
        r = lax.rsqrt(jnp.mean(xf * xf, axis=-1, keepdims=True) + EPS)
        xhat = xf * r
        gain = g_ref[...]
        err = xhat * gain - t_ref[...]
        lpart = 0.5 * jnp.sum(jnp.mean(err * err, axis=-1, keepdims=True), axis=0, keepdims=True)
        dh_ = err * (1.0 / D)
        dy = dh_ * gain
        dx = r * (dy - xhat * jnp.mean(dy * xhat, axis=-1, keepdims=True))
        dx_ref[...] = dx
        dxb_ref[...] = dx.astype(BF16)
        part = jnp.sum(dh_ * xhat, axis=0, keepdims=True)

        @pl.when(pl.program_id(0) == 0)
        def _():
            dg_ref[...] = part
            loss_ref[...] = jnp.broadcast_to(lpart, loss_ref.shape)

        @pl.when(pl.program_id(0) > 0)
        def _():
            dg_ref[...] += part
            loss_ref[...] += jnp.broadcast_to(lpart, loss_ref.shape)

    row = pl.BlockSpec((tm, D), lambda i: (i, 0))
    vec = pl.BlockSpec((1, D), lambda i: (0, 0))
    return pl.pallas_call(
        body, name=name, grid=(T // tm,),
        in_specs=[row, vec, row],
        out_specs=[pl.BlockSpec((8, LANES), lambda i: (0, 0)), row, row, vec],
        out_shape=[jax.ShapeDtypeStruct((8, LANES), F32), jax.ShapeDtypeStruct((T, D), F32),
                   jax.ShapeDtypeStruct((T, D), BF16), jax.ShapeDtypeStruct((1, D), F32)],
        compiler_params=_cparams("arbitrary"),
    )(x, g, target)


def _gate_cols(D):
    off_a = 3 * D // 2 + 2 * KV_WIDTH
    off_b = off_a + D
    cw = math.gcd(math.gcd(off_a, off_b), math.gcd(D, 512))
    return cw, off_a // cw, off_b // cw


def _merge_fwd(z, ya, yb, *, name):
    T, D = ya.shape
    cw, ba, bb = _gate_cols(D)
    tm = _div(T, 512, 8)

    def body(ga_ref, gb_ref, ya_ref, yb_ref, m_ref):
        m_ref[...] = (_sigmoid(ga_ref[...]) * ya_ref[...] + _sigmoid(gb_ref[...]) * yb_ref[...]).astype(BF16)

    blk = pl.BlockSpec((tm, cw), lambda i, j: (i, j))
    return pl.pallas_call(
        body, name=name, grid=(T // tm, D // cw),
        in_specs=[pl.BlockSpec((tm, cw), lambda i, j: (i, ba + j)), pl.BlockSpec((tm, cw), lambda i, j: (i, bb + j)), blk, blk],
        out_specs=blk, out_shape=jax.ShapeDtypeStruct((T, D), BF16),
        compiler_params=_cparams("parallel", "parallel"),
    )(z, z, ya, yb)


def _merge_bwd(z, ya, yb, dm, *, name):
    T, D = ya.shape
    cw, ba, bb = _gate_cols(D)
    tm = _div(T, 512, 8)

    def body(ga_ref, gb_ref, ya_ref, yb_ref, dm_ref, dya_ref, dyb_ref, dga_ref, dgb_ref):
        sa = _sigmoid(ga_ref[...])
        sb = _sigmoid(gb_ref[...])
        dm_ = dm_ref[...]
        dya_ref[...] = (dm_ * sa).astype(BF16)
        dyb_ref[...] = (dm_ * sb).astype(BF16)
        dga_ref[...] = (dm_ * ya_ref[...] * (sa * (1.0 - sa))).astype(BF16)
        dgb_ref[...] = (dm_ * yb_ref[...] * (sb * (1.0 - sb))).astype(BF16)

    blk = pl.BlockSpec((tm, cw), lambda i, j: (i, j))
    out = jax.ShapeDtypeStruct((T, D), BF16)
    return pl.pallas_call(
        body, name=name, grid=(T // tm, D // cw),
        in_specs=[pl.BlockSpec((tm, cw), lambda i, j: (i, ba + j)), pl.BlockSpec((tm, cw), lambda i, j: (i, bb + j)), blk, blk, blk],
        out_specs=[blk] * 4, out_shape=[out] * 4,
        compiler_params=_cparams("parallel", "parallel"),
    )(z, z, ya, yb, dm)


def _swiglu_fwd(gate, up, *, name):
    T, F = gate.shape
    tm, cw = _div(T, 512, 8), _div(F, 512)

    def body(g_ref, u_ref, act_ref):
        g = g_ref[...]
        act_ref[...] = (g * _sigmoid(g) * u_ref[...]).astype(BF16)

    blk = pl.BlockSpec((tm, cw), lambda i, j: (i, j))
    return pl.pallas_call(
        body, name=name, grid=(T // tm, F // cw), in_specs=[blk, blk], out_specs=blk,
        out_shape=jax.ShapeDtypeStruct((T, F), BF16), compiler_params=_cparams("parallel", "parallel"),
    )(gate, up)


def _swiglu_bwd(gate, up, dact, *, name):
    T, F = gate.shape
    tm, cw = _div(T, 512, 8), _div(F, 512)

    def body(g_ref, u_ref, d_ref, dg_ref, du_ref):
        g = g_ref[...]
        s = _sigmoid(g)
        d = d_ref[...]
        silu = g * s
        dg_ref[...] = (d * u_ref[...] * (s + silu * (1.0 - s))).astype(BF16)
        du_ref[...] = (d * silu).astype(BF16)

    blk = pl.BlockSpec((tm, cw), lambda i, j: (i, j))
    out = jax.ShapeDtypeStruct((T, F), BF16)
    return pl.pallas_call(
        body, name=name, grid=(T // tm, F // cw), in_specs=[blk, blk, blk], out_specs=[blk, blk],
        out_shape=[out, out], compiler_params=_cparams("parallel", "parallel"),
    )(gate, up, dact)


def _sgu_fwd(z, gain, ws_b, bs_t, *, name):
    T = z.shape[0]
    SW = gain.shape[1]
    G = SW // BLOCK

    def body(zu_ref, zv_ref, gain_ref, ws_ref, bs_ref, a_ref):
        u = _gelu(zu_ref[...])
        vg = _gelu(zv_ref[...])
        r = lax.rsqrt(jnp.mean(vg * vg, axis=-1, keepdims=True) + EPS)
        vn = ((vg * r) * gain_ref[...]).astype(BF16)
        for g in range(G):
            sl = slice(g * BLOCK, (g + 1) * BLOCK)
            mixed = jnp.dot(ws_ref[g], vn[:, sl], preferred_element_type=F32) + bs_ref[:, g:g + 1]
            a_ref[:, sl] = (u[:, sl] * mixed).astype(BF16)

    return pl.pallas_call(
        body, name=name, grid=(T // BLOCK,),
        in_specs=[pl.BlockSpec((BLOCK, SW), lambda c: (c, 0)), pl.BlockSpec((BLOCK, SW), lambda c: (c, 1)),
                  pl.BlockSpec((1, SW), lambda c: (0, 0)), pl.BlockSpec((G, BLOCK, BLOCK), lambda c: (0, 0, 0)),
                  pl.BlockSpec((BLOCK, G), lambda c: (0, 0))],
        out_specs=pl.BlockSpec((BLOCK, SW), lambda c: (c, 0)),
        out_shape=jax.ShapeDtypeStruct((T, SW), BF16),
        compiler_params=_cparams("parallel"),
    )(z, z, gain, ws_b, bs_t)


def _sgu_bwd(z, gain, ws_b, bs_t, da, *, name):
    T = z.shape[0]
    SW = gain.shape[1]
    G = SW // BLOCK

    def body(zu_ref, zv_ref, gain_ref, ws_ref, bs_ref, da_ref, dzu_ref, dzv_ref, dws_ref, dbs_ref, dgain_ref, dvn_ref):
        first = pl.program_id(0) == 0

        @pl.when(first)
        def _():
            dws_ref[...] = jnp.zeros_like(dws_ref)
            dbs_ref[...] = jnp.zeros_like(dbs_ref)
            dgain_ref[...] = jnp.zeros_like(dgain_ref)

        u, du = _gelu_and_grad(zu_ref[...])
        vg, dvg = _gelu_and_grad(zv_ref[...])
        r = lax.rsqrt(jnp.mean(vg * vg, axis=-1, keepdims=True) + EPS)
        xhat = vg * r
        gain_ = gain_ref[...]
        vn = (xhat * gain_).astype(BF16)
        da_ = da_ref[...]
        for g in range(G):
            sl = slice(g * BLOCK, (g + 1) * BLOCK)
            w = ws_ref[g]
            mixed = jnp.dot(w, vn[:, sl], preferred_element_type=F32) + bs_ref[:, g:g + 1]
            dmix = da_[:, sl] * u[:, sl]
            dzu_ref[:, sl] = (da_[:, sl] * mixed * du[:, sl]).astype(BF16)
            dmb = dmix.astype(BF16)
            dws_ref[g] += lax.dot_general(dmb, vn[:, sl], (((1,), (1,)), ((), ())), preferred_element_type=F32)
            dbs_ref[:, g:g + 1] += jnp.sum(dmix, axis=-1, keepdims=True)
            dvn_ref[:, sl] = lax.dot_general(w, dmb, (((0,), (0,)), ((), ())), preferred_element_type=F32)
        dvn = dvn_ref[...]
        dgain_ref[...] += jnp.sum(dvn * xhat, axis=0, keepdims=True)
        dy = dvn * gain_
        dv_ = r * (dy - xhat * jnp.mean(dy * xhat, axis=-1, keepdims=True))
        dzv_ref[...] = (dv_ * dvg).astype(BF16)

    row = pl.BlockSpec((BLOCK, SW), lambda c: (c, 0))
    return pl.pallas_call(
        body, name=name, grid=(T // BLOCK,),
        in_specs=[row, pl.BlockSpec((BLOCK, SW), lambda c: (c, 1)),
                  pl.BlockSpec((1, SW), lambda c: (0, 0)), pl.BlockSpec((G, BLOCK, BLOCK), lambda c: (0, 0, 0)),
                  pl.BlockSpec((BLOCK, G), lambda c: (0, 0)), row],
        out_specs=[row, row, pl.BlockSpec((G, BLOCK, BLOCK), lambda c: (0, 0, 0)),
                   pl.BlockSpec((BLOCK, G), lambda c: (0, 0)), pl.BlockSpec((1, SW), lambda c: (0, 0))],
        out_shape=[jax.ShapeDtypeStruct((T, SW), BF16), jax.ShapeDtypeStruct((T, SW), BF16),
                   jax.ShapeDtypeStruct((G, BLOCK, BLOCK), F32), jax.ShapeDtypeStruct((BLOCK, G), F32),
                   jax.ShapeDtypeStruct((1, SW), F32)],
        scratch_shapes=[pltpu.VMEM((BLOCK, SW), F32)],
        compiler_params=_cparams("arbitrary"),
    )(z, z, gain, ws_b, bs_t, da)


def _bias_table(rel_bias, bmap, *, name):
    H = rel_bias.shape[1]

    def body(rb_ref, bmap_ref, o_ref):
        bm_ = bmap_ref[...]
        for h in range(H):
            acc = jnp.zeros(bm_.shape, F32)
            for b in range(REL_BUCKETS):
                acc = jnp.where(bm_ == b, rb_ref[b, h], acc)
            o_ref[h] = acc

    return pl.pallas_call(
        body, name=name,
        in_specs=[pl.BlockSpec(memory_space=pltpu.SMEM), pl.BlockSpec(memory_space=pltpu.VMEM)],
        out_specs=pl.BlockSpec(memory_space=pltpu.VMEM),
        out_shape=jax.ShapeDtypeStruct((H, BLOCK, 3 * BLOCK), F32),
    )(rel_bias, bmap)


def _attn_probs(q_ref, kb, bias_ref, sink_ref, valid, h, group):
    kv = h // group
    qh = q_ref[:, h * HEAD_DIM:(h + 1) * HEAD_DIM].astype(BF16)
    s = lax.dot_general(qh, kb[:, kv * HEAD_DIM:(kv + 1) * HEAD_DIM], (((1,), (1,)), ((), ())),
                        preferred_element_type=F32)
    s = s * (HEAD_DIM ** -0.5) + bias_ref[h]
    s = jnp.where(valid, s, NEG)
    sink = sink_ref[0:1, h:h + 1]
    m = jnp.maximum(jnp.max(s, axis=-1, keepdims=True), sink)
    e = jnp.exp(s - m)
    es = jnp.exp(sink - m)
    inv = 1.0 / (jnp.sum(e, axis=-1, keepdims=True) + es)
    return e * inv, es * inv, qh


def _band_valid(n, T):
    row = lax.broadcasted_iota(jnp.int32, (BLOCK, 3 * BLOCK), 0)
    col = lax.broadcasted_iota(jnp.int32, (BLOCK, 3 * BLOCK), 1)
    rel = col - BLOCK - row
    key_pos = n * BLOCK + col - BLOCK
    return (jnp.abs(rel) <= BLOCK) & (key_pos >= 0) & (key_pos < T)


def _attn_fwd(z, kpad, vpad, bias, sink, *, name):
    T = z.shape[0]
    H = bias.shape[0]
    AW = H * HEAD_DIM
    group = H // N_KV_HEADS

    def body(q_ref, k_ref, v_ref, bias_ref, sink_ref, o_ref):
        n = pl.program_id(0)
        start = pl.multiple_of(n * BLOCK, BLOCK)
        kb = k_ref[pl.ds(start, 3 * BLOCK), :]
        vb = v_ref[pl.ds(start, 3 * BLOCK), :]
        valid = _band_valid(n, T)
        for h in range(H):
            kv = h // group
            p, _, _ = _attn_probs(q_ref, kb, bias_ref, sink_ref, valid, h, group)
            o = jnp.dot(p.astype(BF16), vb[:, kv * HEAD_DIM:(kv + 1) * HEAD_DIM], preferred_element_type=F32)
            o_ref[:, h * HEAD_DIM:(h + 1) * HEAD_DIM] = o.astype(BF16)

    full_kv = pl.BlockSpec((T + 2 * BLOCK, KV_WIDTH), lambda n: (0, 0))
    return pl.pallas_call(
        body, name=name, grid=(T // BLOCK,),
        in_specs=[pl.BlockSpec((BLOCK, AW), lambda n: (n, 2)), full_kv, full_kv,
                  pl.BlockSpec((H, BLOCK, 3 * BLOCK), lambda n: (0, 0, 0)), pl.BlockSpec((1, H), lambda n: (0, 0))],
        out_specs=pl.BlockSpec((BLOCK, AW), lambda n: (n, 0)),
        out_shape=jax.ShapeDtypeStruct((T, AW), BF16),
        compiler_params=_cparams("parallel"),
    )(z, kpad, vpad, bias, sink)


def _attn_bwd(z, kpad, vpad, bias, sink, do, *, name):
    T = z.shape[0]
    H = bias.shape[0]
    AW = H * HEAD_DIM
    group = H // N_KV_HEADS
    scale = HEAD_DIM ** -0.5

    def body(q_ref, k_ref, v_ref, bias_ref, sink_ref, do_ref, dq_ref, dk_ref, dv_ref, dbias_ref, dsink_ref):
        n = pl.program_id(0)

        @pl.when(n == 0)
        def _():
            dk_ref[...] = jnp.zeros_like(dk_ref)
            dv_ref[...] = jnp.zeros_like(dv_ref)
            dbias_ref[...] = jnp.zeros_like(dbias_ref)
            dsink_ref[...] = jnp.zeros_like(dsink_ref)

        start = pl.multiple_of(n * BLOCK, BLOCK)
        kb = k_ref[pl.ds(start, 3 * BLOCK), :]
        vb = v_ref[pl.ds(start, 3 * BLOCK), :]
        valid = _band_valid(n, T)
        for kv in range(N_KV_HEADS):
            ksl = slice(kv * HEAD_DIM, (kv + 1) * HEAD_DIM)
            dk_acc = jnp.zeros((3 * BLOCK, HEAD_DIM), F32)
            dv_acc = jnp.zeros((3 * BLOCK, HEAD_DIM), F32)
            for gi in range(group):
                h = kv * group + gi
                hsl = slice(h * HEAD_DIM, (h + 1) * HEAD_DIM)
                p, p_sink, qh = _attn_probs(q_ref, kb, bias_ref, sink_ref, valid, h, group)
                doh = do_ref[:, hsl]
                dp = lax.dot_general(doh, vb[:, ksl], (((1,), (1,)), ((), ())), preferred_element_type=F32)
                delta = jnp.sum(p * dp, axis=-1, keepdims=True)
                ds = p * (dp - delta)
                dbias_ref[h] += ds
                dsink_ref[:, h:h + 1] += -(p_sink * delta)
                dsb = ds.astype(BF16)
                dq = jnp.dot(dsb, kb[:, ksl], preferred_element_type=F32) * scale
                dq_ref[:, hsl] = dq.astype(BF16)
                dk_acc = dk_acc + lax.dot_general(dsb, qh, (((0,), (0,)), ((), ())), preferred_element_type=F32)
                dv_acc = dv_acc + lax.dot_general(p.astype(BF16), doh, (((0,), (0,)), ((), ())),
                                                  preferred_element_type=F32)
            dk_ref[pl.ds(start, 3 * BLOCK), ksl] += dk_acc * scale
            dv_ref[pl.ds(start, 3 * BLOCK), ksl] += dv_acc

    full_kv = pl.BlockSpec((T + 2 * BLOCK, KV_WIDTH), lambda n: (0, 0))
    bias_spec = pl.BlockSpec((H, BLOCK, 3 * BLOCK), lambda n: (0, 0, 0))
    row = pl.BlockSpec((BLOCK, AW), lambda n: (n, 0))
    return pl.pallas_call(
        body, name=name, grid=(T // BLOCK,),
        in_specs=[pl.BlockSpec((BLOCK, AW), lambda n: (n, 2)), full_kv, full_kv, bias_spec,
                  pl.BlockSpec((1, H), lambda n: (0, 0)), row],
        out_specs=[row, full_kv, full_kv, bias_spec, pl.BlockSpec((BLOCK, H), lambda n: (0, 0))],
        out_shape=[jax.ShapeDtypeStruct((T, AW), BF16),
                   jax.ShapeDtypeStruct((T + 2 * BLOCK, KV_WIDTH), F32), jax.ShapeDtypeStruct((T + 2 * BLOCK, KV_WIDTH), F32),
                   jax.ShapeDtypeStruct((H, BLOCK, 3 * BLOCK), F32), jax.ShapeDtypeStruct((BLOCK, H), F32)],
        compiler_params=_cparams("arbitrary"),
    )(z, kpad, vpad, bias, sink, do)


def _attn_small_grads(dbias, dsink_rows, bmap, *, name):
    H = dbias.shape[0]

    def body(dbias_ref, dsink_ref, bmap_ref, drel_ref, ds_ref):
        bm_ = bmap_ref[...]
        for h in range(H):
            d = dbias_ref[h]
            for b in range(REL_BUCKETS):
                drel_ref[b, h] = jnp.sum(jnp.where(bm_ == b, d, 0.0))
            ds_ref[0, h] = jnp.sum(dsink_ref[:, h:h + 1])

    vmem = pl.BlockSpec(memory_space=pltpu.VMEM)
    smem = pl.BlockSpec(memory_space=pltpu.SMEM)
    return pl.pallas_call(
        body, name=name, in_specs=[vmem, vmem, vmem], out_specs=[smem, smem],
        out_shape=[jax.ShapeDtypeStruct((REL_BUCKETS, H), F32), jax.ShapeDtypeStruct((1, H), F32)],
    )(dbias, dsink_rows, bmap)


def _local_step(x, target, w_in, norm_mix, v_gain, w_s, b_s, w_a, sink, rel_bias, w_b, w_o, norm_ffn,
                w_gate, w_up, w_down, norm_final):
    T, D = x.shape
    SW = D // 2
    off_k = D + SW
    ws_b = w_s.astype(BF16)
    bs_t = b_s.T
    bmap = jnp.asarray(_bucket_map())

    h = _rms_fwd(x, norm_mix, name="rms_mix")
    z = _mm(h, w_in, name="mm_z", bm=2048, bn=768)
    a = _sgu_fwd(z, v_gain, ws_b, bs_t, name="sgu_fwd")
    ya = _mm(a, w_a, name="mm_ya", bm=2048, bn=1024)
    pad = ((BLOCK, BLOCK), (0, 0))
    kpad = jnp.pad(z[:, off_k:off_k + KV_WIDTH].astype(BF16), pad)
    vpad = jnp.pad(z[:, off_k + KV_WIDTH:off_k + 2 * KV_WIDTH].astype(BF16), pad)
    bias = _bias_table(rel_bias, bmap, name="bias_table")
    o = _attn_fwd(z, kpad, vpad, bias, sink, name="attn_fwd")
    yb = _mm(o, w_b, name="mm_yb", bm=2048, bn=1024)
    m = _merge_fwd(z, ya, yb, name="merge_fwd")
    x1 = _mm(m, w_o, name="mm_x1", add=x, bm=2048, bn=512)
    h2 = _rms_fwd(x1, norm_ffn, name="rms_ffn")
    gate = _mm(h2, w_gate, name="mm_gate", bm=2048, bn=512)
    up = _mm(h2, w_up, name="mm_up", bm=2048, bn=512)
    act = _swiglu_fwd(gate, up, name="swiglu_fwd")
    x2 = _mm(act, w_down, name="mm_x2", add=x1, bm=1024, bn=1024, bk=2816)
    loss, dx2, dx2b, g_norm_final = _loss_head(x2, norm_final, target, name="loss_head")

    dact = _mm(dx2b, w_down, tb=True, name="mm_dact", bm=2048, bn=512)
    g_w_down = _mm(act, dx2b, ta=True, name="mm_gwdown", bm=512, bn=2048)
    dgate, dup = _swiglu_bwd(gate, up, dact, name="swiglu_bwd")
    dh2 = _mm(dgate, w_gate, tb=True, name="mm_dh2a", bm=1024, bn=1024, bk=2816)
    dh2 = _mm(dup, w_up, tb=True, add=dh2, name="mm_dh2b", bm=1024, bn=1024, bk=2816)
    g_w_gate = _mm(h2, dgate, ta=True, name="mm_gwgate", bm=2048, bn=512)
    g_w_up = _mm(h2, dup, ta=True, name="mm_gwup", bm=2048, bn=512)
    dx1, dx1b, g_norm_ffn = _rms_bwd(x1, norm_ffn, dh2, dx2, name="rms_ffn_bwd", want_bf16=True)

    dm = _mm(dx1b, w_o, tb=True, name="mm_dm", bm=2048, bn=512)
    g_w_o = _mm(m, dx1b, ta=True, name="mm_gwo", bm=2048, bn=512)
    dya, dyb, dga, dgb = _merge_bwd(z, ya, yb, dm, name="merge_bwd")
    da = _mm(dya, w_a, tb=True, name="mm_da", bm=2048, bn=512)
    g_w_a = _mm(a, dya, ta=True, name="mm_gwa", bm=1024, bn=1024)
    do = _mm(dyb, w_b, tb=True, out_dtype=BF16, name="mm_do", bm=2048, bn=512)
    g_w_b = _mm(o, dyb, ta=True, name="mm_gwb", bm=1024, bn=1024)
    dzu, dzv, g_w_s, g_b_s_t, g_v_gain = _sgu_bwd(z, v_gain, ws_b, bs_t, da, name="sgu_bwd")
    dq, dkp, dvp, dbias, dsink_rows = _attn_bwd(z, kpad, vpad, bias, sink, do, name="attn_bwd")
    g_rel_bias, g_sink = _attn_small_grads(dbias, dsink_rows, bmap, name="attn_small_grads")
    dz = jnp.concatenate([dzu, dzv, dq, dkp[BLOCK:BLOCK + T].astype(BF16), dvp[BLOCK:BLOCK + T].astype(BF16), dga, dgb], axis=1)
    dh = _mm(dz, w_in, tb=True, name="mm_dh", bm=1024, bn=1024, bk=2560)
    g_w_in = _mm(h, dz, ta=True, name="mm_gwin", bm=2048, bn=768)
    grad_x, g_norm_mix = _rms_bwd(x, norm_mix, dh, dx1, name="rms_mix_bwd", want_bf16=False)

    grads = dict(w_in=g_w_in, norm_mix=g_norm_mix, sgu_v_gain=g_v_gain, sgu_w_s=g_w_s, sgu_b_s=g_b_s_t.T,
                 w_a_out=g_w_a, attn_sink=g_sink, rel_bias=g_rel_bias, w_b_out=g_w_b, w_o=g_w_o,
                 norm_ffn=g_norm_ffn, w_gate=g_w_gate, w_up=g_w_up, w_down=g_w_down, norm_final=g_norm_final)
    return loss, grad_x, grads


def _position():
    return lax.axis_index("x"), lax.axis_index("y"), lax.axis_index("c")


def _other_chips(x, y):
    return [(1 - x, y), (x, 1 - y), (1 - x, 1 - y)]


def _slot(px, py, pc):
    return 4 * px + 2 * py + pc


_ANY = pl.BlockSpec(memory_space=pl.ANY)


def _all_gather(shards, *, name):
    n = len(shards)

    def body(*refs):
        ins, outs = refs[:n], refs[n:2 * n]
        send_sems, recv_sems, local_sems = refs[2 * n:]
        x, y, c = _position()
        me, sibling = (x, y, c), (x, y, 1 - c)
        chips = _other_chips(x, y)

        def copy(w, k, block, to, src=None):
            dst = outs[w].at[_slot(*block)]
            return pltpu.make_async_remote_copy(
                src_ref=dst if src is None else src, dst_ref=dst,
                send_sem=send_sems.at[w * 7 + k], recv_sem=recv_sems.at[w * 7 + k],
                device_id=to, device_id_type=MESH)

        local, started = [], []
        for w in range(n):
            mine = pltpu.make_async_copy(ins[w], outs[w].at[_slot(*me)], local_sems.at[w])
            mine.start()
            local.append(mine)
            first = [copy(w, 0, me, sibling, src=ins[w])]
            first += [copy(w, 1 + j, me, (*chip, c), src=ins[w]) for j, chip in enumerate(chips)]
            for cp in first:
                cp.start()
            started += first
        for w in range(n):
            for j, chip in enumerate(chips):
                copy(w, 1 + j, (*chip, c), me).wait_recv()
                fwd = copy(w, 4 + j, (*chip, c), sibling)
                fwd.start()
                started.append(fwd)
        for w in range(n):
            copy(w, 0, sibling, me).wait_recv()
            for j, chip in enumerate(chips):
                copy(w, 4 + j, (*chip, 1 - c), me).wait_recv()
        for cp in started:
            cp.wait_send()
        for cp in local:
            cp.wait()

    return pl.pallas_call(
        body, name=name,
        in_specs=[_ANY] * n, out_specs=[_ANY] * n,
        out_shape=[jax.ShapeDtypeStruct((N_DEV,) + s.shape, s.dtype) for s in shards],
        scratch_shapes=[pltpu.SemaphoreType.DMA((7 * n,)), pltpu.SemaphoreType.DMA((7 * n,)), pltpu.SemaphoreType.DMA((n,))],
    )(*shards)


def _rs_to_sibling(grads8, *, name):
    n = len(grads8)

    def body(*refs):
        ins, outs = refs[:n], refs[n:2 * n]
        send_sems, recv_sems = refs[2 * n:]
        x, y, c = _position()
        sibling = (x, y, 1 - c)
        copies = []
        for w in range(n):
            for p in range(4):
                cp = pltpu.make_async_remote_copy(
                    src_ref=ins[w].at[2 * p + (1 - c)], dst_ref=outs[w].at[p],
                    send_sem=send_sems.at[w * 4 + p], recv_sem=recv_sems.at[w * 4 + p],
                    device_id=sibling, device_id_type=MESH)
                cp.start()
                copies.append(cp)
        for cp in copies:
            cp.wait()

    return pl.pallas_call(
        body, name=name,
        in_specs=[_ANY] * n, out_specs=[_ANY] * n,
        out_shape=[jax.ShapeDtypeStruct((4,) + g.shape[1:], g.dtype) for g in grads8],
        scratch_shapes=[pltpu.SemaphoreType.DMA((4 * n,)), pltpu.SemaphoreType.DMA((4 * n,))],
    )(*grads8)


def _chip_sums(g8, from_sibling, pos, *, name):
    _, R, C = g8.shape
    tr = _div(R, 256, 8)

    def body(pos_ref, g_ref, s_ref, o_ref):
        o_ref[...] = (g_ref[...] + s_ref[...]).astype(BF16)

    grid_spec = pltpu.PrefetchScalarGridSpec(
        num_scalar_prefetch=1, grid=(4, R // tr),
        in_specs=[pl.BlockSpec((None, tr, C), lambda p, i, pos_ref: (2 * p + pos_ref[2], i, 0)),
                  pl.BlockSpec((None, tr, C), lambda p, i, pos_ref: (p, i, 0))],
        out_specs=pl.BlockSpec((None, tr, C), lambda p, i, pos_ref: (p, i, 0)))
    return pl.pallas_call(
        body, name=name, grid_spec=grid_spec,
        out_shape=jax.ShapeDtypeStruct((4, R, C), BF16),
        compiler_params=_cparams("parallel", "parallel"),
    )(pos, g8, from_sibling)


def _rs_to_chips(sums4, *, name):
    n = len(sums4)

    def body(*refs):
        ins, outs = refs[:n], refs[n:2 * n]
        send_sems, recv_sems = refs[2 * n:]
        x, y, c = _position()
        copies = []
        for w in range(n):
            for k, (px, py) in enumerate(_other_chips(x, y)):
                cp = pltpu.make_async_remote_copy(
                    src_ref=ins[w].at[2 * px + py], dst_ref=outs[w].at[k],
                    send_sem=send_sems.at[w * 3 + k], recv_sem=recv_sems.at[w * 3 + k],
                    device_id=(px, py, c), device_id_type=MESH)
                cp.start()
                copies.append(cp)
        for cp in copies:
            cp.wait()

    return pl.pallas_call(
        body, name=name,
        in_specs=[_ANY] * n, out_specs=[_ANY] * n,
        out_shape=[jax.ShapeDtypeStruct((3,) + s.shape[1:], s.dtype) for s in sums4],
        scratch_shapes=[pltpu.SemaphoreType.DMA((3 * n,)), pltpu.SemaphoreType.DMA((3 * n,))],
    )(*sums4)


def _small_all_reduce(packed, *, name):
    R, L = packed.shape

    def body(x_ref, sum_ref, gath_ref, send_sems, recv_sems, local_sem):
        x, y, c = _position()
        me, sibling = (x, y, c), (x, y, 1 - c)
        chips = _other_chips(x, y)

        def rows(px, py, pc):
            return gath_ref.at[pl.ds(_slot(px, py, pc) * R, R), :]

        def copy(k, block, to, src=None):
            return pltpu.make_async_remote_copy(
                src_ref=rows(*block) if src is None else src, dst_ref=rows(*block),
                send_sem=send_sems.at[k], recv_sem=recv_sems.at[k], device_id=to, device_id_type=MESH)

        mine = pltpu.make_async_copy(x_ref, rows(*me), local_sem)
        mine.start()
        first = [copy(0, me, sibling, src=x_ref)]
        first += [copy(1 + j, me, (*chip, c), src=x_ref) for j, chip in enumerate(chips)]
        for cp in first:
            cp.start()
        passed = [copy(4 + j, (*chip, c), sibling) for j, chip in enumerate(chips)]
        for j, chip in enumerate(chips):
            copy(1 + j, (*chip, c), me).wait_recv()
            passed[j].start()
        copy(0, sibling, me).wait_recv()
        for j, chip in enumerate(chips):
            copy(4 + j, (*chip, 1 - c), me).wait_recv()
        for cp in first + passed:
            cp.wait_send()
        mine.wait()
        acc = gath_ref[0:R, :]
        for d in range(1, N_DEV):
            acc = acc + gath_ref[d * R:(d + 1) * R, :]
        sum_ref[...] = acc

    vmem = pl.BlockSpec(memory_space=pltpu.VMEM)
    return pl.pallas_call(
        body, name=name, in_specs=[vmem], out_specs=vmem,
        out_shape=jax.ShapeDtypeStruct((R, L), F32),
        scratch_shapes=[pltpu.VMEM((N_DEV * R, L), F32), pltpu.SemaphoreType.DMA((7,)), pltpu.SemaphoreType.DMA((7,)),
                        pltpu.SemaphoreType.DMA],
        compiler_params=pltpu.CompilerParams(vmem_limit_bytes=VMEM_LIMIT),
    )(packed)


def _adamw_math(w, g, m, v):
    m = ADAM_B1 * m + (1.0 - ADAM_B1) * g
    v = ADAM_B2 * v + (1.0 - ADAM_B2) * (g * g)
    m_hat = m / (1.0 - ADAM_B1 ** ADAM_STEP)
    v_hat = v / (1.0 - ADAM_B2 ** ADAM_STEP)
    delta = -ADAM_LR * (m_hat / (jnp.sqrt(v_hat) + ADAM_EPS) + ADAM_WD * w)
    return delta, m, v


def _adamw_shard(w, m, v, g8, from_sibling, from_chips, pos, *, name):
    R, C = w.shape
    tr = _div(R, 256, 8)

    def body(pos_ref, w_ref, m_ref, v_ref, g_ref, s_ref, r_ref, go_ref, d_ref, mo_ref, vo_ref):
        g = g_ref[...] + s_ref[...]
        for k in range(3):
            g = g + r_ref[k].astype(F32)
        delta, m_, v_ = _adamw_math(w_ref[...], g, m_ref[...], v_ref[...])
        go_ref[...] = g
        d_ref[...] = delta
        mo_ref[...] = m_
        vo_ref[...] = v_

    blk = pl.BlockSpec((tr, C), lambda i, pos_ref: (i, 0))
    grid_spec = pltpu.PrefetchScalarGridSpec(
        num_scalar_prefetch=1, grid=(R // tr,),
        in_specs=[blk, blk, blk,
                  pl.BlockSpec((None, tr, C), lambda i, pos_ref: (pos_ref[0], i, 0)),
                  pl.BlockSpec((None, tr, C), lambda i, pos_ref: (pos_ref[1], i, 0)),
                  pl.BlockSpec((3, tr, C), lambda i, pos_ref: (0, i, 0))],
        out_specs=[blk] * 4)
    out = jax.ShapeDtypeStruct((R, C), F32)
    return pl.pallas_call(
        body, name=name, grid_spec=grid_spec, out_shape=[out] * 4,
        compiler_params=_cparams("parallel"),
    )(pos, w, m, v, g8, from_sibling, from_chips)


def _adamw_small(w, g, m, v, *, name):
    R, L = w.shape

    def body(w_ref, g_ref, m_ref, v_ref, d_ref, mo_ref, vo_ref):
        delta, m_, v_ = _adamw_math(w_ref[...], g_ref[...], m_ref[...], v_ref[...])
        d_ref[...] = delta
        mo_ref[...] = m_
        vo_ref[...] = v_

    vmem = pl.BlockSpec(memory_space=pltpu.VMEM)
    out = jax.ShapeDtypeStruct((R, L), F32)
    return pl.pallas_call(body, name=name, in_specs=[vmem] * 4, out_specs=[vmem] * 3, out_shape=[out] * 3)(w, g, m, v)


_TILE = 8 * LANES


def _pack(pieces):
    rows = []
    for p in pieces:
        flat = p.reshape(-1).astype(F32)
        padded = -(-flat.shape[0] // _TILE) * _TILE
        rows.append(jnp.pad(flat, (0, padded - flat.shape[0])).reshape(-1, LANES))
    return jnp.concatenate(rows, axis=0)


def _unpack(packed, like):
    out, r = [], 0
    for p in like:
        size = int(np.prod(p.shape)) if p.shape else 1
        nrows = -(-size // _TILE) * 8
        out.append(packed[r:r + nrows].reshape(-1)[:size].reshape(p.shape))
        r += nrows
    return out


_COL_SHARDED = ("w_in", "w_a_out", "w_b_out", "w_gate", "w_up")
_ROW_SHARDED = ("w_o", "w_down")
_BIG = ("w_in", "w_a_out", "w_b_out", "w_o", "w_gate", "w_up", "w_down")
_SMALL = ("norm_mix", "sgu_v_gain", "sgu_w_s", "sgu_b_s", "attn_sink", "rel_bias", "norm_ffn", "norm_final")
_ORDER = ("w_in", "norm_mix", "sgu_v_gain", "sgu_w_s", "sgu_b_s", "w_a_out", "attn_sink", "rel_bias", "w_b_out", "w_o",
          "norm_ffn", "w_gate", "w_up", "w_down", "norm_final")


def _whole(name, gathered):
    _, r, c = gathered.shape
    if name in _COL_SHARDED:
        return gathered.transpose(1, 0, 2).reshape(r, N_DEV * c)
    return gathered.reshape(N_DEV * r, c)


def _blocks(name, grad):
    if name in _COL_SHARDED:
        r, c = grad.shape
        return grad.reshape(r, N_DEV, c // N_DEV).transpose(1, 0, 2)
    r, c = grad.shape
    return grad.reshape(N_DEV, r // N_DEV, c)


def kernel(x, w_in, norm_mix, sgu_v_gain, sgu_w_s, sgu_b_s, w_a_out, attn_sink, rel_bias, w_b_out, w_o, norm_ffn, w_gate, w_up, w_down, norm_final, loss_target, m_w_in, m_norm_mix, m_sgu_v_gain, m_sgu_w_s, m_sgu_b_s, m_w_a_out, m_attn_sink, m_rel_bias, m_w_b_out, m_w_o, m_norm_ffn, m_w_gate, m_w_up, m_w_down, m_norm_final, v_w_in, v_norm_mix, v_sgu_v_gain, v_sgu_w_s, v_sgu_b_s, v_w_a_out, v_attn_sink, v_rel_bias, v_w_b_out, v_w_o, v_norm_ffn, v_w_gate, v_w_up, v_w_down, v_norm_final):
    w = dict(w_in=w_in, norm_mix=norm_mix, sgu_v_gain=sgu_v_gain, sgu_w_s=sgu_w_s, sgu_b_s=sgu_b_s, w_a_out=w_a_out,
             attn_sink=attn_sink, rel_bias=rel_bias, w_b_out=w_b_out, w_o=w_o, norm_ffn=norm_ffn, w_gate=w_gate,
             w_up=w_up, w_down=w_down, norm_final=norm_final)
    m = dict(w_in=m_w_in, norm_mix=m_norm_mix, sgu_v_gain=m_sgu_v_gain, sgu_w_s=m_sgu_w_s, sgu_b_s=m_sgu_b_s,
             w_a_out=m_w_a_out, attn_sink=m_attn_sink, rel_bias=m_rel_bias, w_b_out=m_w_b_out, w_o=m_w_o,
             norm_ffn=m_norm_ffn, w_gate=m_w_gate, w_up=m_w_up, w_down=m_w_down, norm_final=m_norm_final)
    v = dict(w_in=v_w_in, norm_mix=v_norm_mix, sgu_v_gain=v_sgu_v_gain, sgu_w_s=v_sgu_w_s, sgu_b_s=v_sgu_b_s,
             w_a_out=v_w_a_out, attn_sink=v_attn_sink, rel_bias=v_rel_bias, w_b_out=v_w_b_out, w_o=v_w_o,
             norm_ffn=v_norm_ffn, w_gate=v_w_gate, w_up=v_w_up, w_down=v_w_down, norm_final=v_norm_final)
    xc, yc, cc = _position()
    pos = jnp.stack([_slot(xc, yc, cc), 2 * xc + yc, cc]).astype(jnp.int32)

    shards = [w[n][0].astype(BF16) for n in _BIG]
    gathered = _all_gather(shards, name="all_gather_weights")
    full = {n: _whole(n, g) for n, g in zip(_BIG, gathered)}

    loss, grad_x, grads = _local_step(
        x[0], loss_target[0], full["w_in"], norm_mix, sgu_v_gain, sgu_w_s[0], sgu_b_s[0], full["w_a_out"], attn_sink,
        rel_bias, full["w_b_out"], full["w_o"], norm_ffn, full["w_gate"], full["w_up"], full["w_down"], norm_final[None])

    g8 = [_blocks(n, grads[n]) for n in _BIG]
    from_sibling = _rs_to_sibling(g8, name="rs_to_sibling")
    sums4 = [_chip_sums(g, s, pos, name="chip_sums_" + n) for n, g, s in zip(_BIG, g8, from_sibling)]
    from_chips = _rs_to_chips(sums4, name="rs_to_chips")
    small_like = [w[n] for n in _SMALL]
    small_w = _pack(small_like)
    packed = _pack([grads[n] for n in _SMALL] + [loss[0, 0]])
    summed = _small_all_reduce(packed, name="small_all_reduce")
    *small_grads, loss_sum = _unpack(summed, small_like + [jax.ShapeDtypeStruct((), F32)])

    out_g, out_d, out_m, out_v = {}, {}, {}, {}
    for i, n in enumerate(_BIG):
        g, d, m_, v_ = _adamw_shard(w[n][0], m[n][0], v[n][0], g8[i], from_sibling[i], from_chips[i], pos, name="adamw_" + n)
        out_g[n], out_d[n], out_m[n], out_v[n] = g[None], d[None], m_[None], v_[None]
    d_s, m_s, v_s = _adamw_small(small_w, summed[:small_w.shape[0]], _pack([m[n] for n in _SMALL]),
                                 _pack([v[n] for n in _SMALL]), name="adamw_small")
    for n, g, d, m_, v_ in zip(_SMALL, small_grads, _unpack(d_s, small_like), _unpack(m_s, small_like), _unpack(v_s, small_like)):
        out_g[n], out_d[n], out_m[n], out_v[n] = g, d, m_, v_

    return (loss_sum, grad_x[None], *[out_g[n] for n in _ORDER], *[out_d[n] for n in _ORDER],
            *[out_m[n] for n in _ORDER], *[out_v[n] for n in _ORDER])
```

```python
import functools
import math

import numpy as np
import jax
import jax.numpy as jnp
from jax import lax
from jax.experimental import pallas as pl
from jax.experimental.pallas import tpu as pltpu

F32 = jnp.float32
BF16 = jnp.bfloat16

EPS = 1e-6
NEG = -1e30
HEAD_DIM = 128
BLOCK = 128
N_KV_HEADS = 2
KV_WIDTH = N_KV_HEADS * HEAD_DIM
REL_BUCKETS = 32
REL_MAX_DIST = 128

ADAM_LR = 0.001
ADAM_B1 = 0.9
ADAM_B2 = 0.999
ADAM_EPS = 1e-08
ADAM_WD = 0.01
ADAM_STEP = 10

N_DEV = 8
LANES = 128
VMEM_LIMIT = 56 * 1024 * 1024
MESH = pl.DeviceIdType.MESH


def _cparams(*sem):
    return pltpu.CompilerParams(dimension_semantics=sem, vmem_limit_bytes=VMEM_LIMIT)


def _div(n, target, mult=LANES):
    best = None
    for d in range(mult, min(n, target) + 1, mult):
        if n % d == 0:
            best = d
    assert best is not None, (n, target, mult)
    return best


def _bucket_map():
    nb = REL_BUCKETS // 2
    qi = np.arange(BLOCK)[:, None]
    kj = np.arange(3 * BLOCK)[None, :]
    rel = kj - BLOCK - qi
    ret = np.where(rel > 0, nb, 0)
    n = np.abs(rel)
    max_exact = nb // 2
    nf = np.maximum(n, 1).astype(np.float32)
    large = max_exact + (np.log(nf / np.float32(max_exact)) / np.float32(math.log(REL_MAX_DIST / max_exact))
                         * np.float32(nb - max_exact)).astype(np.int32)
    large = np.minimum(large, nb - 1)
    return (ret + np.where(n < max_exact, n, large)).astype(np.int32)


_GELU_C = math.sqrt(2.0 / math.pi)
_GELU_A = 0.044715


def _gelu(x):
    t = jnp.tanh(_GELU_C * (x + _GELU_A * (x * x * x)))
    return 0.5 * x * (1.0 + t)


def _gelu_and_grad(x):
    x2 = x * x
    t = jnp.tanh(_GELU_C * (x + _GELU_A * (x2 * x)))
    g = 0.5 * x * (1.0 + t)
    dg = 0.5 * (1.0 + t) + 0.5 * x * (1.0 - t * t) * (_GELU_C * (1.0 + 3.0 * _GELU_A * x2))
    return g, dg


def _sigmoid(x):
    return 1.0 / (1.0 + jnp.exp(-x))


def _mm(a, b, *, name, ta=False, tb=False, add=None, out_dtype=F32, bm=1024, bn=1024, bk=None):
    if ta:
        K, M = a.shape
    else:
        M, K = a.shape
    N = b.shape[0] if tb else b.shape[1]
    assert (b.shape[1] if tb else b.shape[0]) == K
    bm = _div(M, bm)
    bn = _div(N, bn)
    bk = K if bk is None else _div(K, bk)
    nk = K // bk
    a_spec = pl.BlockSpec((bk, bm), lambda i, j, k: (k, i)) if ta else pl.BlockSpec((bm, bk), lambda i, j, k: (i, k))
    b_spec = pl.BlockSpec((bn, bk), lambda i, j, k: (j, k)) if tb else pl.BlockSpec((bk, bn), lambda i, j, k: (k, j))
    o_spec = pl.BlockSpec((bm, bn), lambda i, j, k: (i, j))
    dims = (((0 if ta else 1,), (1 if tb else 0,)), ((), ()))
    has_add = add is not None

    def body(*refs):
        if has_add:
            a_ref, b_ref, add_ref, o_ref, *scratch = refs
        else:
            a_ref, b_ref, o_ref, *scratch = refs
            add_ref = None
        p = lax.dot_general(a_ref[...].astype(BF16), b_ref[...].astype(BF16), dims, preferred_element_type=F32)
        if nk == 1:
            if has_add:
                p = p + add_ref[...]
            o_ref[...] = p.astype(out_dtype)
        else:
            acc = scratch[0]
            k = pl.program_id(2)

            @pl.when(k == 0)
            def _():
                acc[...] = p

            @pl.when(k > 0)
            def _():
                acc[...] += p

            @pl.when(k == nk - 1)
            def _():
                r = acc[...]
                if has_add:
                    r = r + add_ref[...]
                o_ref[...] = r.astype(out_dtype)

    in_specs = [a_spec, b_spec] + ([o_spec] if has_add else [])
    args = (a, b) + ((add,) if has_add else ())
    return pl.pallas_call(
        body, name=name, grid=(M // bm, N // bn, nk),
        in_specs=in_specs, out_specs=o_spec,
        out_shape=jax.ShapeDtypeStruct((M, N), out_dtype),
        scratch_shapes=[pltpu.VMEM((bm, bn), F32)] if nk > 1 else [],
        compiler_params=_cparams("parallel", "parallel", "arbitrary"),
    )(*args)


def _blocks_per_tile(c):
    nb = 1
    while (nb * c) % LANES or (nb * c < 1024 and nb < N_DEV):
        nb *= 2
    assert nb <= N_DEV and (nb * c) % LANES == 0, c
    return nb


def _mm_w8(a, w8, *, name, bm=1024):
    M, K = a.shape
    _, _, c = w8.shape
    nb = _blocks_per_tile(c)
    bm = _div(M, bm)

    def body(a_ref, w_ref, o_ref):
        a_ = a_ref[...]
        for t in range(nb):
            o_ref[:, t * c:(t + 1) * c] = jnp.dot(a_, w_ref[t], preferred_element_type=F32)

    return pl.pallas_call(
        body, name=name, grid=(M // bm, N_DEV // nb),
        in_specs=[pl.BlockSpec((bm, K), lambda i, j: (i, 0)), pl.BlockSpec((nb, K, c), lambda i, j: (j, 0, 0))],
        out_specs=pl.BlockSpec((bm, nb * c), lambda i, j: (i, j)),
        out_shape=jax.ShapeDtypeStruct((M, N_DEV * c), F32),
        compiler_params=_cparams("parallel", "parallel"),
    )(a, w8)


def _mm_w8t(dy, w8, *, name, add=None, out_dtype=F32, bm=1024, bn=1024):
    M = dy.shape[0]
    _, K, c = w8.shape
    nb = _blocks_per_tile(c)
    nk = N_DEV // nb
    bm, bn = _div(M, bm), _div(K, bn)
    has_add = add is not None
    dims = (((1,), (1,)), ((), ()))

    def body(*refs):
        if has_add:
            dy_ref, w_ref, add_ref, o_ref, acc = refs
        else:
            dy_ref, w_ref, o_ref, acc = refs
        p = lax.dot_general(dy_ref[:, 0:c], w_ref[0], dims, preferred_element_type=F32)
        for t in range(1, nb):
            p = p + lax.dot_general(dy_ref[:, t * c:(t + 1) * c], w_ref[t], dims, preferred_element_type=F32)
        k = pl.program_id(2)

        @pl.when(k == 0)
        def _():
            acc[...] = p

        @pl.when(k > 0)
        def _():
            acc[...] += p

        @pl.when(k == nk - 1)
        def _():
            r = acc[...]
            if has_add:
                r = r + add_ref[...]
            o_ref[...] = r.astype(out_dtype)

    o_spec = pl.BlockSpec((bm, bn), lambda i, j, k: (i, j))
    in_specs = [pl.BlockSpec((bm, nb * c), lambda i, j, k: (i, k)), pl.BlockSpec((nb, bn, c), lambda i, j, k: (k, j, 0))]
    return pl.pallas_call(
        body, name=name, grid=(M // bm, K // bn, nk),
        in_specs=in_specs + ([o_spec] if has_add else []), out_specs=o_spec,
        out_shape=jax.ShapeDtypeStruct((M, K), out_dtype),
        scratch_shapes=[pltpu.VMEM((bm, bn), F32)],
        compiler_params=_cparams("parallel", "parallel", "arbitrary"),
    )(*((dy, w8) + ((add,) if has_add else ())))


def _mm_gw8(x, dy, c, *, name, bk=1024):
    T, K = x.shape
    nb = _blocks_per_tile(c)
    bk = _div(K, bk)
    dims = (((0,), (0,)), ((), ()))

    def body(x_ref, dy_ref, o_ref):
        x_ = x_ref[...]
        for t in range(nb):
            o_ref[t] = lax.dot_general(x_, dy_ref[:, t * c:(t + 1) * c], dims, preferred_element_type=F32).astype(BF16)

    return pl.pallas_call(
        body, name=name, grid=(K // bk, N_DEV // nb),
        in_specs=[pl.BlockSpec((T, bk), lambda i, j: (0, i)), pl.BlockSpec((T, nb * c), lambda i, j: (0, j))],
        out_specs=pl.BlockSpec((nb, bk, c), lambda i, j: (j, i, 0)),
        out_shape=jax.ShapeDtypeStruct((N_DEV, K, c), BF16),
        compiler_params=_cparams("parallel", "parallel"),
    )(x, dy)


def _rms_fwd(x, g, *, name):
    T, D = x.shape
    tm = _div(T, 256, 8)

    def body(x_ref, g_ref, h_ref):
        xf = x_ref[...]
        r = lax.rsqrt(jnp.mean(xf * xf, axis=-1, keepdims=True) + EPS)
        h_ref[...] = ((xf * r) * g_ref[...]).astype(BF16)

    return pl.pallas_call(
        body, name=name, grid=(T // tm,),
        in_specs=[pl.BlockSpec((tm, D), lambda i: (i, 0)), pl.BlockSpec((1, D), lambda i: (0, 0))],
        out_specs=pl.BlockSpec((tm, D), lambda i: (i, 0)),
        out_shape=jax.ShapeDtypeStruct((T, D), BF16),
        compiler_params=_cparams("parallel"),
    )(x, g)


def _rms_bwd(x, g, dh, dres, *, name, want_bf16):
    T, D = x.shape
    tm = _div(T, 256, 8)

    def body(x_ref, g_ref, dh_ref, dres_ref, dx_ref, *rest):
        if want_bf16:
            dxb_ref, dg_ref = rest
        else:
            (dg_ref,) = rest
        xf = x_ref[...]
        r = lax.rsqrt(jnp.mean(xf * xf, axis=-1, keepdims=True) + EPS)
        xhat = xf * r
        dh_ = dh_ref[...]
        dy = dh_ * g_ref[...]
        dx = dres_ref[...] + r * (dy - xhat * jnp.mean(dy * xhat, axis=-1, keepdims=True))
        dx_ref[...] = dx
        if want_bf16:
            dxb_ref[...] = dx.astype(BF16)
        part = jnp.sum(dh_ * xhat, axis=0, keepdims=True)

        @pl.when(pl.program_id(0) == 0)
        def _():
            dg_ref[...] = part

        @pl.when(pl.program_id(0) > 0)
        def _():
            dg_ref[...] += part

    row = pl.BlockSpec((tm, D), lambda i: (i, 0))
    vec = pl.BlockSpec((1, D), lambda i: (0, 0))
    out_specs = [row] + ([row] if want_bf16 else []) + [vec]
    out_shape = ([jax.ShapeDtypeStruct((T, D), F32)] + ([jax.ShapeDtypeStruct((T, D), BF16)] if want_bf16 else [])
                 + [jax.ShapeDtypeStruct((1, D), F32)])
    return pl.pallas_call(
        body, name=name, grid=(T // tm,),
        in_specs=[row, vec, row, row], out_specs=out_specs, out_shape=out_shape,
        compiler_params=_cparams("arbitrary"),
    )(x, g, dh, dres)


def _loss_head(x, g, target, *, name):
    T, D = x.shape
    tm = _div(T, 256, 8)

    def body(x_ref, g_ref, t_ref, loss_ref, dx_ref, dxb_ref, dg_ref):
        xf = x_ref[...]
        r = lax.rsqrt(jnp.mean(xf * xf, axis=-1, keepdims=True) + EPS)
        xhat = xf * r
        gain = g_ref[...]
        err = xhat * gain - t_ref[...]
        lpart = 0.5 * jnp.sum(jnp.mean(err * err, axis=-1, keepdims=True), axis=0, keepdims=True)
        dh_ = err * (1.0 / D)
        dy = dh_ * gain
        dx = r * (dy - xhat * jnp.mean(dy * xhat, axis=-1, keepdims=True))
        dx_ref[...] = dx
        dxb_ref[...] = dx.astype(BF16)
        part = jnp.sum(dh_ * xhat, axis=0, keepdims=True)

        @pl.when(pl.program_id(0) == 0)
        def _():
            dg_ref[...] = part
            loss_ref[...] = jnp.broadcast_to(lpart, loss_ref.shape)

        @pl.when(pl.program_id(0) > 0)
        def _():
            dg_ref[...] += part
            loss_ref[...] += jnp.broadcast_to(lpart, loss_ref.shape)

    row = pl.BlockSpec((tm, D), lambda i: (i, 0))
    vec = pl.BlockSpec((1, D), lambda i: (0, 0))
    return pl.pallas_call(
        body, name=name, grid=(T // tm,),
        in_specs=[row, vec, row],
        out_specs=[pl.BlockSpec((8, LANES), lambda i: (0, 0)), row, row, vec],
        out_shape=[jax.ShapeDtypeStruct((8, LANES), F32), jax.ShapeDtypeStruct((T, D), F32),
                   jax.ShapeDtypeStruct((T, D), BF16), jax.ShapeDtypeStruct((1, D), F32)],
        compiler_params=_cparams("arbitrary"),
    )(x, g, target)


def _gate_cols(D):
    off_a = 3 * D // 2 + 2 * KV_WIDTH
    off_b = off_a + D
    cw = math.gcd(math.gcd(off_a, off_b), math.gcd(D, 512))
    return cw, off_a // cw, off_b // cw


def _merge_fwd(z, ya, yb, *, name):
    T, D = ya.shape
    cw, ba, bb = _gate_cols(D)
    tm = _div(T, 512, 8)

    def body(ga_ref, gb_ref, ya_ref, yb_ref, m_ref):
        m_ref[...] = (_sigmoid(ga_ref[...]) * ya_ref[...] + _sigmoid(gb_ref[...]) * yb_ref[...]).astype(BF16)

    blk = pl.BlockSpec((tm, cw), lambda i, j: (i, j))
    return pl.pallas_call(
        body, name=name, grid=(T // tm, D // cw),
        in_specs=[pl.BlockSpec((tm, cw), lambda i, j: (i, ba + j)), pl.BlockSpec((tm, cw), lambda i, j: (i, bb + j)), blk, blk],
        out_specs=blk, out_shape=jax.ShapeDtypeStruct((T, D), BF16),
        compiler_params=_cparams("parallel", "parallel"),
    )(z, z, ya, yb)


def _merge_bwd(z, ya, yb, dm, *, name):
    T, D = ya.shape
    cw, ba, bb = _gate_cols(D)
    tm = _div(T, 512, 8)

    def body(ga_ref, gb_ref, ya_ref, yb_ref, dm_ref, dya_ref, dyb_ref, dga_ref, dgb_ref):
        sa = _sigmoid(ga_ref[...])
        sb = _sigmoid(gb_ref[...])
        dm_ = dm_ref[...]
        dya_ref[...] = (dm_ * sa).astype(BF16)
        dyb_ref[...] = (dm_ * sb).astype(BF16)
        dga_ref[...] = (dm_ * ya_ref[...] * (sa * (1.0 - sa))).astype(BF16)
        dgb_ref[...] = (dm_ * yb_ref[...] * (sb * (1.0 - sb))).astype(BF16)

    blk = pl.BlockSpec((tm, cw), lambda i, j: (i, j))
    out = jax.ShapeDtypeStruct((T, D), BF16)
    return pl.pallas_call(
        body, name=name, grid=(T // tm, D // cw),
        in_specs=[pl.BlockSpec((tm, cw), lambda i, j: (i, ba + j)), pl.BlockSpec((tm, cw), lambda i, j: (i, bb + j)), blk, blk, blk],
        out_specs=[blk] * 4, out_shape=[out] * 4,
        compiler_params=_cparams("parallel", "parallel"),
    )(z, z, ya, yb, dm)


def _swiglu_fwd(gate, up, *, name):
    T, F = gate.shape
    tm, cw = _div(T, 512, 8), _div(F, 512)

    def body(g_ref, u_ref, act_ref):
        g = g_ref[...]
        act_ref[...] = (g * _sigmoid(g) * u_ref[...]).astype(BF16)

    blk = pl.BlockSpec((tm, cw), lambda i, j: (i, j))
    return pl.pallas_call(
        body, name=name, grid=(T // tm, F // cw), in_specs=[blk, blk], out_specs=blk,
        out_shape=jax.ShapeDtypeStruct((T, F), BF16), compiler_params=_cparams("parallel", "parallel"),
    )(gate, up)


def _swiglu_bwd(gate, up, dact, *, name):
    T, F = gate.shape
    tm, cw = _div(T, 512, 8), _div(F, 512)

    def body(g_ref, u_ref, d_ref, dg_ref, du_ref):
        g = g_ref[...]
        s = _sigmoid(g)
        d = d_ref[...]
        silu = g * s
        dg_ref[...] = (d * u_ref[...] * (s + silu * (1.0 - s))).astype(BF16)
        du_ref[...] = (d * silu).astype(BF16)

    blk = pl.BlockSpec((tm, cw), lambda i, j: (i, j))
    out = jax.ShapeDtypeStruct((T, F), BF16)
    return pl.pallas_call(
        body, name=name, grid=(T // tm, F // cw), in_specs=[blk, blk, blk], out_specs=[blk, blk],
        out_shape=[out, out], compiler_params=_cparams("parallel", "parallel"),
    )(gate, up, dact)


def _sgu_fwd(z, gain, ws_b, bs_t, *, name):
    T = z.shape[0]
    SW = gain.shape[1]
    G = SW // BLOCK

    def body(zu_ref, zv_ref, gain_ref, ws_ref, bs_ref, a_ref):
        u = _gelu(zu_ref[...])
        vg = _gelu(zv_ref[...])
        r = lax.rsqrt(jnp.mean(vg * vg, axis=-1, keepdims=True) + EPS)
        vn = ((vg * r) * gain_ref[...]).astype(BF16)
        for g in range(G):
            sl = slice(g * BLOCK, (g + 1) * BLOCK)
            mixed = jnp.dot(ws_ref[g], vn[:, sl], preferred_element_type=F32) + bs_ref[:, g:g + 1]
            a_ref[:, sl] = (u[:, sl] * mixed).astype(BF16)

    return pl.pallas_call(
        body, name=name, grid=(T // BLOCK,),
        in_specs=[pl.BlockSpec((BLOCK, SW), lambda c: (c, 0)), pl.BlockSpec((BLOCK, SW), lambda c: (c, 1)),
                  pl.BlockSpec((1, SW), lambda c: (0, 0)), pl.BlockSpec((G, BLOCK, BLOCK), lambda c: (0, 0, 0)),
                  pl.BlockSpec((BLOCK, G), lambda c: (0, 0))],
        out_specs=pl.BlockSpec((BLOCK, SW), lambda c: (c, 0)),
        out_shape=jax.ShapeDtypeStruct((T, SW), BF16),
        compiler_params=_cparams("parallel"),
    )(z, z, gain, ws_b, bs_t)


def _sgu_bwd(z, gain, ws_b, bs_t, da, *, name):
    T = z.shape[0]
    SW = gain.shape[1]
    G = SW // BLOCK

    def body(zu_ref, zv_ref, gain_ref, ws_ref, bs_ref, da_ref, dzu_ref, dzv_ref, dws_ref, dbs_ref, dgain_ref, dvn_ref):
        first = pl.program_id(0) == 0

        @pl.when(first)
        def _():
            dws_ref[...] = jnp.zeros_like(dws_ref)
            dbs_ref[...] = jnp.zeros_like(dbs_ref)
            dgain_ref[...] = jnp.zeros_like(dgain_ref)

        u, du = _gelu_and_grad(zu_ref[...])
        vg, dvg = _gelu_and_grad(zv_ref[...])
        r = lax.rsqrt(jnp.mean(vg * vg, axis=-1, keepdims=True) + EPS)
        xhat = vg * r
        gain_ = gain_ref[...]
        vn = (xhat * gain_).astype(BF16)
        da_ = da_ref[...]
        for g in range(G):
            sl = slice(g * BLOCK, (g + 1) * BLOCK)
            w = ws_ref[g]
            mixed = jnp.dot(w, vn[:, sl], preferred_element_type=F32) + bs_ref[:, g:g + 1]
            dmix = da_[:, sl] * u[:, sl]
            dzu_ref[:, sl] = (da_[:, sl] * mixed * du[:, sl]).astype(BF16)
            dmb = dmix.astype(BF16)
            dws_ref[g] += lax.dot_general(dmb, vn[:, sl], (((1,), (1,)), ((), ())), preferred_element_type=F32)
            dbs_ref[:, g:g + 1] += jnp.sum(dmix, axis=-1, keepdims=True)
            dvn_ref[:, sl] = lax.dot_general(w, dmb, (((0,), (0,)), ((), ())), preferred_element_type=F32)
        dvn = dvn_ref[...]
        dgain_ref[...] += jnp.sum(dvn * xhat, axis=0, keepdims=True)
        dy = dvn * gain_
        dv_ = r * (dy - xhat * jnp.mean(dy * xhat, axis=-1, keepdims=True))
        dzv_ref[...] = (dv_ * dvg).astype(BF16)

    row = pl.BlockSpec((BLOCK, SW), lambda c: (c, 0))
    return pl.pallas_call(
        body, name=name, grid=(T // BLOCK,),
        in_specs=[row, pl.BlockSpec((BLOCK, SW), lambda c: (c, 1)),
                  pl.BlockSpec((1, SW), lambda c: (0, 0)), pl.BlockSpec((G, BLOCK, BLOCK), lambda c: (0, 0, 0)),
                  pl.BlockSpec((BLOCK, G), lambda c: (0, 0)), row],
        out_specs=[row, row, pl.BlockSpec((G, BLOCK, BLOCK), lambda c: (0, 0, 0)),
                   pl.BlockSpec((BLOCK, G), lambda c: (0, 0)), pl.BlockSpec((1, SW), lambda c: (0, 0))],
        out_shape=[jax.ShapeDtypeStruct((T, SW), BF16), jax.ShapeDtypeStruct((T, SW), BF16),
                   jax.ShapeDtypeStruct((G, BLOCK, BLOCK), F32), jax.ShapeDtypeStruct((BLOCK, G), F32),
                   jax.ShapeDtypeStruct((1, SW), F32)],
        scratch_shapes=[pltpu.VMEM((BLOCK, SW), F32)],
        compiler_params=_cparams("arbitrary"),
    )(z, z, gain, ws_b, bs_t, da)


def _bias_table(rel_bias, bmap, *, name):
    H = rel_bias.shape[1]

    def body(rb_ref, bmap_ref, o_ref):
        bm_ = bmap_ref[...]
        for h in range(H):
            acc = jnp.zeros(bm_.shape, F32)
            for b in range(REL_BUCKETS):
                acc = jnp.where(bm_ == b, rb_ref[b, h], acc)
            o_ref[h] = acc

    return pl.pallas_call(
        body, name=name,
        in_specs=[pl.BlockSpec(memory_space=pltpu.SMEM), pl.BlockSpec(memory_space=pltpu.VMEM)],
        out_specs=pl.BlockSpec(memory_space=pltpu.VMEM),
        out_shape=jax.ShapeDtypeStruct((H, BLOCK, 3 * BLOCK), F32),
    )(rel_bias, bmap)


def _attn_probs(q_ref, kb, bias_ref, sink_ref, valid, h, group):
    kv = h // group
    qh = q_ref[:, h * HEAD_DIM:(h + 1) * HEAD_DIM].astype(BF16)
    s = lax.dot_general(qh, kb[:, kv * HEAD_DIM:(kv + 1) * HEAD_DIM], (((1,), (1,)), ((), ())),
                        preferred_element_type=F32)
    s = s * (HEAD_DIM ** -0.5) + bias_ref[h]
    s = jnp.where(valid, s, NEG)
    sink = sink_ref[0:1, h:h + 1]
    m = jnp.maximum(jnp.max(s, axis=-1, keepdims=True), sink)
    e = jnp.exp(s - m)
    es = jnp.exp(sink - m)
    inv = 1.0 / (jnp.sum(e, axis=-1, keepdims=True) + es)
    return e * inv, es * inv, qh


def _band_valid(n, T):
    row = lax.broadcasted_iota(jnp.int32, (BLOCK, 3 * BLOCK), 0)
    col = lax.broadcasted_iota(jnp.int32, (BLOCK, 3 * BLOCK), 1)
    rel = col - BLOCK - row
    key_pos = n * BLOCK + col - BLOCK
    return (jnp.abs(rel) <= BLOCK) & (key_pos >= 0) & (key_pos < T)


def _attn_fwd(z, kpad, vpad, bias, sink, *, name):
    T = z.shape[0]
    H = bias.shape[0]
    AW = H * HEAD_DIM
    group = H // N_KV_HEADS

    def body(q_ref, k_ref, v_ref, bias_ref, sink_ref, o_ref):
        n = pl.program_id(0)
        start = pl.multiple_of(n * BLOCK, BLOCK)
        kb = k_ref[pl.ds(start, 3 * BLOCK), :]
        vb = v_ref[pl.ds(start, 3 * BLOCK), :]
        valid = _band_valid(n, T)
        for h in range(H):
            kv = h // group
            p, _, _ = _attn_probs(q_ref, kb, bias_ref, sink_ref, valid, h, group)
            o = jnp.dot(p.astype(BF16), vb[:, kv * HEAD_DIM:(kv + 1) * HEAD_DIM], preferred_element_type=F32)
            o_ref[:, h * HEAD_DIM:(h + 1) * HEAD_DIM] = o.astype(BF16)

    full_kv = pl.BlockSpec((T + 2 * BLOCK, KV_WIDTH), lambda n: (0, 0))
    return pl.pallas_call(
        body, name=name, grid=(T // BLOCK,),
        in_specs=[pl.BlockSpec((BLOCK, AW), lambda n: (n, 2)), full_kv, full_kv,
                  pl.BlockSpec((H, BLOCK, 3 * BLOCK), lambda n: (0, 0, 0)), pl.BlockSpec((1, H), lambda n: (0, 0))],
        out_specs=pl.BlockSpec((BLOCK, AW), lambda n: (n, 0)),
        out_shape=jax.ShapeDtypeStruct((T, AW), BF16),
        compiler_params=_cparams("parallel"),
    )(z, kpad, vpad, bias, sink)


def _attn_bwd(z, kpad, vpad, bias, sink, do, *, name):
    T = z.shape[0]
    H = bias.shape[0]
    AW = H * HEAD_DIM
    group = H // N_KV_HEADS
    scale = HEAD_DIM ** -0.5

    def body(q_ref, k_ref, v_ref, bias_ref, sink_ref, do_ref, dq_ref, dk_ref, dv_ref, dbias_ref, dsink_ref):
        n = pl.program_id(0)

        @pl.when(n == 0)
        def _():
            dk_ref[...] = jnp.zeros_like(dk_ref)
            dv_ref[...] = jnp.zeros_like(dv_ref)
            dbias_ref[...] = jnp.zeros_like(dbias_ref)
            dsink_ref[...] = jnp.zeros_like(dsink_ref)

        start = pl.multiple_of(n * BLOCK, BLOCK)
        kb = k_ref[pl.ds(start, 3 * BLOCK), :]
        vb = v_ref[pl.ds(start, 3 * BLOCK), :]
        valid = _band_valid(n, T)
        for kv in range(N_KV_HEADS):
            ksl = slice(kv * HEAD_DIM, (kv + 1) * HEAD_DIM)
            dk_acc = jnp.zeros((3 * BLOCK, HEAD_DIM), F32)
            dv_acc = jnp.zeros((3 * BLOCK, HEAD_DIM), F32)
            for gi in range(group):
                h = kv * group + gi
                hsl = slice(h * HEAD_DIM, (h + 1) * HEAD_DIM)
                p, p_sink, qh = _attn_probs(q_ref, kb, bias_ref, sink_ref, valid, h, group)
                doh = do_ref[:, hsl]
                dp = lax.dot_general(doh, vb[:, ksl], (((1,), (1,)), ((), ())), preferred_element_type=F32)
                delta = jnp.sum(p * dp, axis=-1, keepdims=True)
                ds = p * (dp - delta)
                dbias_ref[h] += ds
                dsink_ref[:, h:h + 1] += -(p_sink * delta)
                dsb = ds.astype(BF16)
                dq = jnp.dot(dsb, kb[:, ksl], preferred_element_type=F32) * scale
                dq_ref[:, hsl] = dq.astype(BF16)
                dk_acc = dk_acc + lax.dot_general(dsb, qh, (((0,), (0,)), ((), ())), preferred_element_type=F32)
                dv_acc = dv_acc + lax.dot_general(p.astype(BF16), doh, (((0,), (0,)), ((), ())),
                                                  preferred_element_type=F32)
            dk_ref[pl.ds(start, 3 * BLOCK), ksl] += dk_acc * scale
            dv_ref[pl.ds(start, 3 * BLOCK), ksl] += dv_acc

    full_kv = pl.BlockSpec((T + 2 * BLOCK, KV_WIDTH), lambda n: (0, 0))
    bias_spec = pl.BlockSpec((H, BLOCK, 3 * BLOCK), lambda n: (0, 0, 0))
    row = pl.BlockSpec((BLOCK, AW), lambda n: (n, 0))
    return pl.pallas_call(
        body, name=name, grid=(T // BLOCK,),
        in_specs=[pl.BlockSpec((BLOCK, AW), lambda n: (n, 2)), full_kv, full_kv, bias_spec,
                  pl.BlockSpec((1, H), lambda n: (0, 0)), row],
        out_specs=[row, full_kv, full_kv, bias_spec, pl.BlockSpec((BLOCK, H), lambda n: (0, 0))],
        out_shape=[jax.ShapeDtypeStruct((T, AW), BF16),
                   jax.ShapeDtypeStruct((T + 2 * BLOCK, KV_WIDTH), F32), jax.ShapeDtypeStruct((T + 2 * BLOCK, KV_WIDTH), F32),
                   jax.ShapeDtypeStruct((H, BLOCK, 3 * BLOCK), F32), jax.ShapeDtypeStruct((BLOCK, H), F32)],
        compiler_params=_cparams("arbitrary"),
    )(z, kpad, vpad, bias, sink, do)


def _attn_small_grads(dbias, dsink_rows, bmap, *, name):
    H = dbias.shape[0]

    def body(dbias_ref, dsink_ref, bmap_ref, drel_ref, ds_ref):
        bm_ = bmap_ref[...]
        for h in range(H):
            d = dbias_ref[h]
            for b in range(REL_BUCKETS):
                drel_ref[b, h] = jnp.sum(jnp.where(bm_ == b, d, 0.0))
            ds_ref[0, h] = jnp.sum(dsink_ref[:, h:h + 1])

    vmem = pl.BlockSpec(memory_space=pltpu.VMEM)
    smem = pl.BlockSpec(memory_space=pltpu.SMEM)
    return pl.pallas_call(
        body, name=name, in_specs=[vmem, vmem, vmem], out_specs=[smem, smem],
        out_shape=[jax.ShapeDtypeStruct((REL_BUCKETS, H), F32), jax.ShapeDtypeStruct((1, H), F32)],
    )(dbias, dsink_rows, bmap)


def _local_step(x, target, w_in, norm_mix, v_gain, w_s, b_s, w_a, sink, rel_bias, w_b, w_o, norm_ffn,
                w_gate, w_up, w_down, norm_final):
    T, D = x.shape
    SW = D // 2
    off_k = D + SW
    ws_b = w_s.astype(BF16)
    bs_t = b_s.T
    bmap = jnp.asarray(_bucket_map())

    h = _rms_fwd(x, norm_mix, name="rms_mix")
    z = _mm_w8(h, w_in, name="mm_z")
    a = _sgu_fwd(z, v_gain, ws_b, bs_t, name="sgu_fwd")
    ya = _mm_w8(a, w_a, name="mm_ya", bm=2048)
    pad = ((BLOCK, BLOCK), (0, 0))
    kpad = jnp.pad(z[:, off_k:off_k + KV_WIDTH].astype(BF16), pad)
    vpad = jnp.pad(z[:, off_k + KV_WIDTH:off_k + 2 * KV_WIDTH].astype(BF16), pad)
    bias = _bias_table(rel_bias, bmap, name="bias_table")
    o = _attn_fwd(z, kpad, vpad, bias, sink, name="attn_fwd")
    yb = _mm_w8(o, w_b, name="mm_yb", bm=2048)
    m = _merge_fwd(z, ya, yb, name="merge_fwd")
    x1 = _mm(m, w_o, name="mm_x1", add=x, bm=2048, bn=512)
    h2 = _rms_fwd(x1, norm_ffn, name="rms_ffn")
    gate = _mm_w8(h2, w_gate, name="mm_gate")
    up = _mm_w8(h2, w_up, name="mm_up")
    act = _swiglu_fwd(gate, up, name="swiglu_fwd")
    x2 = _mm(act, w_down, name="mm_x2", add=x1, bm=1024, bn=1024, bk=2816)
    loss, dx2, dx2b, g_norm_final = _loss_head(x2, norm_final, target, name="loss_head")

    dact = _mm(dx2b, w_down, tb=True, name="mm_dact", bm=2048, bn=512)
    g_w_down = _mm(act, dx2b, ta=True, out_dtype=BF16, name="mm_gwdown", bm=512, bn=2048)
    dgate, dup = _swiglu_bwd(gate, up, dact, name="swiglu_bwd")
    dh2 = _mm_w8t(dgate, w_gate, name="mm_dh2a")
    dh2 = _mm_w8t(dup, w_up, add=dh2, name="mm_dh2b")
    g_w_gate = _mm_gw8(h2, dgate, w_gate.shape[2], name="mm_gwgate")
    g_w_up = _mm_gw8(h2, dup, w_up.shape[2], name="mm_gwup")
    dx1, dx1b, g_norm_ffn = _rms_bwd(x1, norm_ffn, dh2, dx2, name="rms_ffn_bwd", want_bf16=True)

    dm = _mm(dx1b, w_o, tb=True, name="mm_dm", bm=2048, bn=512)
    g_w_o = _mm(m, dx1b, ta=True, out_dtype=BF16, name="mm_gwo", bm=2048, bn=512)
    dya, dyb, dga, dgb = _merge_bwd(z, ya, yb, dm, name="merge_bwd")
    da = _mm_w8t(dya, w_a, name="mm_da", bm=2048, bn=512)
    g_w_a = _mm_gw8(a, dya, w_a.shape[2], name="mm_gwa")
    do = _mm_w8t(dyb, w_b, out_dtype=BF16, name="mm_do", bm=2048, bn=512)
    g_w_b = _mm_gw8(o, dyb, w_b.shape[2], name="mm_gwb")
    dzu, dzv, g_w_s, g_b_s_t, g_v_gain = _sgu_bwd(z, v_gain, ws_b, bs_t, da, name="sgu_bwd")
    dq, dkp, dvp, dbias, dsink_rows = _attn_bwd(z, kpad, vpad, bias, sink, do, name="attn_bwd")
    g_rel_bias, g_sink = _attn_small_grads(dbias, dsink_rows, bmap, name="attn_small_grads")
    dz = jnp.concatenate([dzu, dzv, dq, dkp[BLOCK:BLOCK + T].astype(BF16), dvp[BLOCK:BLOCK + T].astype(BF16), dga, dgb], axis=1)
    g_w_in = _mm_gw8(h, dz, w_in.shape[2], name="mm_gwin")
    dh = _mm_w8t(dz, w_in, name="mm_dh")
    grad_x, g_norm_mix = _rms_bwd(x, norm_mix, dh, dx1, name="rms_mix_bwd", want_bf16=False)

    grads = dict(w_in=g_w_in, norm_mix=g_norm_mix, sgu_v_gain=g_v_gain, sgu_w_s=g_w_s, sgu_b_s=g_b_s_t.T,
                 w_a_out=g_w_a, attn_sink=g_sink, rel_bias=g_rel_bias, w_b_out=g_w_b, w_o=g_w_o,
                 norm_ffn=g_norm_ffn, w_gate=g_w_gate, w_up=g_w_up, w_down=g_w_down, norm_final=g_norm_final)
    return loss, grad_x, grads


def _position():
    return lax.axis_index("x"), lax.axis_index("y"), lax.axis_index("c")


def _other_chips(x, y):
    return [(1 - x, y), (x, 1 - y), (1 - x, 1 - y)]


def _slot(px, py, pc):
    return 4 * px + 2 * py + pc


_ANY = pl.BlockSpec(memory_space=pl.ANY)


def _all_gather(shards, *, name):
    n = len(shards)

    def body(*refs):
        ins, outs = refs[:n], refs[n:2 * n]
        send_sems, recv_sems, local_sems = refs[2 * n:]
        x, y, c = _position()
        me, sibling = (x, y, c), (x, y, 1 - c)
        chips = _other_chips(x, y)

        def copy(w, k, block, to, src=None):
            dst = outs[w].at[_slot(*block)]
            return pltpu.make_async_remote_copy(
                src_ref=dst if src is None else src, dst_ref=dst,
                send_sem=send_sems.at[w * 7 + k], recv_sem=recv_sems.at[w * 7 + k],
                device_id=to, device_id_type=MESH)

        local, started = [], []
        for w in range(n):
            mine = pltpu.make_async_copy(ins[w], outs[w].at[_slot(*me)], local_sems.at[w])
            mine.start()
            local.append(mine)
            first = [copy(w, 0, me, sibling, src=ins[w])]
            first += [copy(w, 1 + j, me, (*chip, c), src=ins[w]) for j, chip in enumerate(chips)]
            for cp in first:
                cp.start()
            started += first
        for w in range(n):
            for j, chip in enumerate(chips):
                copy(w, 1 + j, (*chip, c), me).wait_recv()
                fwd = copy(w, 4 + j, (*chip, c), sibling)
                fwd.start()
                started.append(fwd)
        for w in range(n):
            copy(w, 0, sibling, me).wait_recv()
            for j, chip in enumerate(chips):
                copy(w, 4 + j, (*chip, 1 - c), me).wait_recv()
        for cp in started:
            cp.wait_send()
        for cp in local:
            cp.wait()

    return pl.pallas_call(
        body, name=name,
        in_specs=[_ANY] * n, out_specs=[_ANY] * n,
        out_shape=[jax.ShapeDtypeStruct((N_DEV,) + s.shape, s.dtype) for s in shards],
        scratch_shapes=[pltpu.SemaphoreType.DMA((7 * n,)), pltpu.SemaphoreType.DMA((7 * n,)), pltpu.SemaphoreType.DMA((n,))],
    )(*shards)


def _rs_to_sibling(grads8, *, name):
    n = len(grads8)

    def body(*refs):
        ins, outs = refs[:n], refs[n:2 * n]
        send_sems, recv_sems = refs[2 * n:]
        x, y, c = _position()
        sibling = (x, y, 1 - c)
        copies = []
        for w in range(n):
            for p in range(4):
                cp = pltpu.make_async_remote_copy(
                    src_ref=ins[w].at[2 * p + (1 - c)], dst_ref=outs[w].at[p],
                    send_sem=send_sems.at[w * 4 + p], recv_sem=recv_sems.at[w * 4 + p],
                    device_id=sibling, device_id_type=MESH)
                cp.start()
                copies.append(cp)
        for cp in copies:
            cp.wait()

    return pl.pallas_call(
        body, name=name,
        in_specs=[_ANY] * n, out_specs=[_ANY] * n,
        out_shape=[jax.ShapeDtypeStruct((4,) + g.shape[1:], g.dtype) for g in grads8],
        scratch_shapes=[pltpu.SemaphoreType.DMA((4 * n,)), pltpu.SemaphoreType.DMA((4 * n,))],
    )(*grads8)


def _chip_sums(g8, from_sibling, pos, *, name):
    _, R, C = g8.shape
    tr = _div(R, 512, 16)

    def body(pos_ref, g_ref, s_ref, o_ref):
        o_ref[...] = (g_ref[...].astype(F32) + s_ref[...].astype(F32)).astype(BF16)

    grid_spec = pltpu.PrefetchScalarGridSpec(
        num_scalar_prefetch=1, grid=(4, R // tr),
        in_specs=[pl.BlockSpec((None, tr, C), lambda p, i, pos_ref: (2 * p + pos_ref[2], i, 0)),
                  pl.BlockSpec((None, tr, C), lambda p, i, pos_ref: (p, i, 0))],
        out_specs=pl.BlockSpec((None, tr, C), lambda p, i, pos_ref: (p, i, 0)))
    return pl.pallas_call(
        body, name=name, grid_spec=grid_spec,
        out_shape=jax.ShapeDtypeStruct((4, R, C), BF16),
        compiler_params=_cparams("parallel", "parallel"),
    )(pos, g8, from_sibling)


def _rs_to_chips(sums4, *, name):
    n = len(sums4)

    def body(*refs):
        ins, outs = refs[:n], refs[n:2 * n]
        send_sems, recv_sems = refs[2 * n:]
        x, y, c = _position()
        copies = []
        for w in range(n):
            for k, (px, py) in enumerate(_other_chips(x, y)):
                cp = pltpu.make_async_remote_copy(
                    src_ref=ins[w].at[2 * px + py], dst_ref=outs[w].at[k],
                    send_sem=send_sems.at[w * 3 + k], recv_sem=recv_sems.at[w * 3 + k],
                    device_id=(px, py, c), device_id_type=MESH)
                cp.start()
                copies.append(cp)
        for cp in copies:
            cp.wait()

    return pl.pallas_call(
        body, name=name,
        in_specs=[_ANY] * n, out_specs=[_ANY] * n,
        out_shape=[jax.ShapeDtypeStruct((3,) + s.shape[1:], s.dtype) for s in sums4],
        scratch_shapes=[pltpu.SemaphoreType.DMA((3 * n,)), pltpu.SemaphoreType.DMA((3 * n,))],
    )(*sums4)


def _small_all_reduce(packed, *, name):
    R, L = packed.shape

    def body(x_ref, sum_ref, gath_ref, send_sems, recv_sems, local_sem):
        x, y, c = _position()
        me, sibling = (x, y, c), (x, y, 1 - c)
        chips = _other_chips(x, y)

        def rows(px, py, pc):
            return gath_ref.at[pl.ds(_slot(px, py, pc) * R, R), :]

        def copy(k, block, to, src=None):
            return pltpu.make_async_remote_copy(
                src_ref=rows(*block) if src is None else src, dst_ref=rows(*block),
                send_sem=send_sems.at[k], recv_sem=recv_sems.at[k], device_id=to, device_id_type=MESH)

        mine = pltpu.make_async_copy(x_ref, rows(*me), local_sem)
        mine.start()
        first = [copy(0, me, sibling, src=x_ref)]
        first += [copy(1 + j, me, (*chip, c), src=x_ref) for j, chip in enumerate(chips)]
        for cp in first:
            cp.start()
        passed = [copy(4 + j, (*chip, c), sibling) for j, chip in enumerate(chips)]
        for j, chip in enumerate(chips):
            copy(1 + j, (*chip, c), me).wait_recv()
            passed[j].start()
        copy(0, sibling, me).wait_recv()
        for j, chip in enumerate(chips):
            copy(4 + j, (*chip, 1 - c), me).wait_recv()
        for cp in first + passed:
            cp.wait_send()
        mine.wait()
        acc = gath_ref[0:R, :]
        for d in range(1, N_DEV):
            acc = acc + gath_ref[d * R:(d + 1) * R, :]
        sum_ref[...] = acc

    vmem = pl.BlockSpec(memory_space=pltpu.VMEM)
    return pl.pallas_call(
        body, name=name, in_specs=[vmem], out_specs=vmem,
        out_shape=jax.ShapeDtypeStruct((R, L), F32),
        scratch_shapes=[pltpu.VMEM((N_DEV * R, L), F32), pltpu.SemaphoreType.DMA((7,)), pltpu.SemaphoreType.DMA((7,)),
                        pltpu.SemaphoreType.DMA],
        compiler_params=pltpu.CompilerParams(vmem_limit_bytes=VMEM_LIMIT),
    )(packed)


def _adamw_math(w, g, m, v):
    m = ADAM_B1 * m + (1.0 - ADAM_B1) * g
    v = ADAM_B2 * v + (1.0 - ADAM_B2) * (g * g)
    m_hat = m / (1.0 - ADAM_B1 ** ADAM_STEP)
    v_hat = v / (1.0 - ADAM_B2 ** ADAM_STEP)
    delta = -ADAM_LR * (m_hat / (jnp.sqrt(v_hat) + ADAM_EPS) + ADAM_WD * w)
    return delta, m, v


def _adamw_shard(w, m, v, g8, from_sibling, from_chips, pos, *, name):
    R, C = w.shape
    tr = _div(R, 256, 16)

    def body(pos_ref, w_ref, m_ref, v_ref, g_ref, s_ref, r_ref, go_ref, d_ref, mo_ref, vo_ref):
        g = g_ref[...].astype(F32) + s_ref[...].astype(F32)
        for k in range(3):
            g = g + r_ref[k].astype(F32)
        delta, m_, v_ = _adamw_math(w_ref[...], g, m_ref[...], v_ref[...])
        go_ref[...] = g
        d_ref[...] = delta
        mo_ref[...] = m_
        vo_ref[...] = v_

    blk = pl.BlockSpec((tr, C), lambda i, pos_ref: (i, 0))
    grid_spec = pltpu.PrefetchScalarGridSpec(
        num_scalar_prefetch=1, grid=(R // tr,),
        in_specs=[blk, blk, blk,
                  pl.BlockSpec((None, tr, C), lambda i, pos_ref: (pos_ref[0], i, 0)),
                  pl.BlockSpec((None, tr, C), lambda i, pos_ref: (pos_ref[1], i, 0)),
                  pl.BlockSpec((3, tr, C), lambda i, pos_ref: (0, i, 0))],
        out_specs=[blk] * 4)
    out = jax.ShapeDtypeStruct((R, C), F32)
    return pl.pallas_call(
        body, name=name, grid_spec=grid_spec, out_shape=[out] * 4,
        compiler_params=_cparams("parallel"),
    )(pos, w, m, v, g8, from_sibling, from_chips)


def _adamw_small(w, g, m, v, *, name):
    R, L = w.shape

    def body(w_ref, g_ref, m_ref, v_ref, d_ref, mo_ref, vo_ref):
        delta, m_, v_ = _adamw_math(w_ref[...], g_ref[...], m_ref[...], v_ref[...])
        d_ref[...] = delta
        mo_ref[...] = m_
        vo_ref[...] = v_

    vmem = pl.BlockSpec(memory_space=pltpu.VMEM)
    out = jax.ShapeDtypeStruct((R, L), F32)
    return pl.pallas_call(body, name=name, in_specs=[vmem] * 4, out_specs=[vmem] * 3, out_shape=[out] * 3)(w, g, m, v)


_TILE = 8 * LANES


def _pack(pieces):
    rows = []
    for p in pieces:
        flat = p.reshape(-1).astype(F32)
        padded = -(-flat.shape[0] // _TILE) * _TILE
        rows.append(jnp.pad(flat, (0, padded - flat.shape[0])).reshape(-1, LANES))
    return jnp.concatenate(rows, axis=0)


def _unpack(packed, like):
    out, r = [], 0
    for p in like:
        size = int(np.prod(p.shape)) if p.shape else 1
        nrows = -(-size // _TILE) * 8
        out.append(packed[r:r + nrows].reshape(-1)[:size].reshape(p.shape))
        r += nrows
    return out


_COL_SHARDED = ("w_in", "w_a_out", "w_b_out", "w_gate", "w_up")
_ROW_SHARDED = ("w_o", "w_down")
_BIG = ("w_in", "w_a_out", "w_b_out", "w_o", "w_gate", "w_up", "w_down")
_SMALL = ("norm_mix", "sgu_v_gain", "sgu_w_s", "sgu_b_s", "attn_sink", "rel_bias", "norm_ffn", "norm_final")
_ORDER = ("w_in", "norm_mix", "sgu_v_gain", "sgu_w_s", "sgu_b_s", "w_a_out", "attn_sink", "rel_bias", "w_b_out", "w_o",
          "norm_ffn", "w_gate", "w_up", "w_down", "norm_final")


def _whole(name, gathered):
    _, r, c = gathered.shape
    return gathered if name in _COL_SHARDED else gathered.reshape(N_DEV * r, c)


def _blocks(name, grad):
    if name in _COL_SHARDED:
        return grad
    r, c = grad.shape
    return grad.reshape(N_DEV, r // N_DEV, c)


def kernel(x, w_in, norm_mix, sgu_v_gain, sgu_w_s, sgu_b_s, w_a_out, attn_sink, rel_bias, w_b_out, w_o, norm_ffn, w_gate, w_up, w_down, norm_final, loss_target, m_w_in, m_norm_mix, m_sgu_v_gain, m_sgu_w_s, m_sgu_b_s, m_w_a_out, m_attn_sink, m_rel_bias, m_w_b_out, m_w_o, m_norm_ffn, m_w_gate, m_w_up, m_w_down, m_norm_final, v_w_in, v_norm_mix, v_sgu_v_gain, v_sgu_w_s, v_sgu_b_s, v_w_a_out, v_attn_sink, v_rel_bias, v_w_b_out, v_w_o, v_norm_ffn, v_w_gate, v_w_up, v_w_down, v_norm_final):
    w = dict(w_in=w_in, norm_mix=norm_mix, sgu_v_gain=sgu_v_gain, sgu_w_s=sgu_w_s, sgu_b_s=sgu_b_s, w_a_out=w_a_out,
             attn_sink=attn_sink, rel_bias=rel_bias, w_b_out=w_b_out, w_o=w_o, norm_ffn=norm_ffn, w_gate=w_gate,
             w_up=w_up, w_down=w_down, norm_final=norm_final)
    m = dict(w_in=m_w_in, norm_mix=m_norm_mix, sgu_v_gain=m_sgu_v_gain, sgu_w_s=m_sgu_w_s, sgu_b_s=m_sgu_b_s,
             w_a_out=m_w_a_out, attn_sink=m_attn_sink, rel_bias=m_rel_bias, w_b_out=m_w_b_out, w_o=m_w_o,
             norm_ffn=m_norm_ffn, w_gate=m_w_gate, w_up=m_w_up, w_down=m_w_down, norm_final=m_norm_final)
    v = dict(w_in=v_w_in, norm_mix=v_norm_mix, sgu_v_gain=v_sgu_v_gain, sgu_w_s=v_sgu_w_s, sgu_b_s=v_sgu_b_s,
             w_a_out=v_w_a_out, attn_sink=v_attn_sink, rel_bias=v_rel_bias, w_b_out=v_w_b_out, w_o=v_w_o,
             norm_ffn=v_norm_ffn, w_gate=v_w_gate, w_up=v_w_up, w_down=v_w_down, norm_final=v_norm_final)
    xc, yc, cc = _position()
    pos = jnp.stack([_slot(xc, yc, cc), 2 * xc + yc, cc]).astype(jnp.int32)

    shards = [w[n][0].astype(BF16) for n in _BIG]
    gathered = _all_gather(shards, name="all_gather_weights")
    full = {n: _whole(n, g) for n, g in zip(_BIG, gathered)}

    loss, grad_x, grads = _local_step(
        x[0], loss_target[0], full["w_in"], norm_mix, sgu_v_gain, sgu_w_s[0], sgu_b_s[0], full["w_a_out"], attn_sink,
        rel_bias, full["w_b_out"], full["w_o"], norm_ffn, full["w_gate"], full["w_up"], full["w_down"], norm_final[None])

    g8 = [_blocks(n, grads[n]) for n in _BIG]
    from_sibling = _rs_to_sibling(g8, name="rs_to_sibling")
    sums4 = [_chip_sums(g, s, pos, name="chip_sums_" + n) for n, g, s in zip(_BIG, g8, from_sibling)]
    from_chips = _rs_to_chips(sums4, name="rs_to_chips")
    small_like = [w[n] for n in _SMALL]
    small_w = _pack(small_like)
    packed = _pack([grads[n] for n in _SMALL] + [loss[0, 0]])
    summed = _small_all_reduce(packed, name="small_all_reduce")
    *small_grads, loss_sum = _unpack(summed, small_like + [jax.ShapeDtypeStruct((), F32)])

    out_g, out_d, out_m, out_v = {}, {}, {}, {}
    for i, n in enumerate(_BIG):
        g, d, m_, v_ = _adamw_shard(w[n][0], m[n][0], v[n][0], g8[i], from_sibling[i], from_chips[i], pos, name="adamw_" + n)
        out_g[n], out_d[n], out_m[n], out_v[n] = g[None], d[None], m_[None], v_[None]
    d_s, m_s, v_s = _adamw_small(small_w, summed[:small_w.shape[0]], _pack([m[n] for n in _SMALL]),
                                 _pack([v[n] for n in _SMALL]), name="adamw_small")
    for n, g, d, m_, v_ in zip(_SMALL, small_grads, _unpack(d_s, small_like), _unpack(m_s, small_like), _unpack(v_s, small_like)):
        out_g[n], out_d[n], out_m[n], out_v[n] = g, d, m_, v_

    return (loss_sum, grad_x[None], *[out_g[n] for n in _ORDER], *[out_d[n] for n in _ORDER],
            *[out_m[n] for n in _ORDER], *[out_v[n] for n in _ORDER])
```

```python
import functools
import math

import numpy as np
import jax
import jax.numpy as jnp
from jax import lax
from jax.experimental import pallas as pl
from jax.experimental.pallas import tpu as pltpu

F32 = jnp.float32
BF16 = jnp.bfloat16

EPS = 1e-6
NEG = -1e30
HEAD_DIM = 128
BLOCK = 128
N_KV_HEADS = 2
KV_WIDTH = N_KV_HEADS * HEAD_DIM
REL_BUCKETS = 32
REL_MAX_DIST = 128

ADAM_LR = 0.001
ADAM_B1 = 0.9
ADAM_B2 = 0.999
ADAM_EPS = 1e-08
ADAM_WD = 0.01
ADAM_STEP = 10

N_DEV = 8
LANES = 128
VMEM_LIMIT = 56 * 1024 * 1024
MESH = pl.DeviceIdType.MESH


def _cparams(*sem):
    return pltpu.CompilerParams(dimension_semantics=sem, vmem_limit_bytes=VMEM_LIMIT)


def _div(n, target, mult=LANES):
    best = None
    for d in range(mult, min(n, target) + 1, mult):
        if n % d == 0:
            best = d
    assert best is not None, (n, target, mult)
    return best


_ANY = pl.BlockSpec(memory_space=pl.ANY)


def _ordered_after(body, n_inputs, in_specs, args, after):
    if after is None:
        return body, in_specs, args

    def wrapped(*refs):
        return body(*refs[:n_inputs], *refs[n_inputs + 1:])

    return wrapped, list(in_specs) + [_ANY], tuple(args) + (after,)


def _bucket_map():
    nb = REL_BUCKETS // 2
    qi = np.arange(BLOCK)[:, None]
    kj = np.arange(3 * BLOCK)[None, :]
    rel = kj - BLOCK - qi
    ret = np.where(rel > 0, nb, 0)
    n = np.abs(rel)
    max_exact = nb // 2
    nf = np.maximum(n, 1).astype(np.float32)
    large = max_exact + (np.log(nf / np.float32(max_exact)) / np.float32(math.log(REL_MAX_DIST / max_exact))
                         * np.float32(nb - max_exact)).astype(np.int32)
    large = np.minimum(large, nb - 1)
    return (ret + np.where(n < max_exact, n, large)).astype(np.int32)


_GELU_C = math.sqrt(2.0 / math.pi)
_GELU_A = 0.044715


def _gelu(x):
    t = jnp.tanh(_GELU_C * (x + _GELU_A * (x * x * x)))
    return 0.5 * x * (1.0 + t)


def _gelu_and_grad(x):
    x2 = x * x
    t = jnp.tanh(_GELU_C * (x + _GELU_A * (x2 * x)))
    g = 0.5 * x * (1.0 + t)
    dg = 0.5 * (1.0 + t) + 0.5 * x * (1.0 - t * t) * (_GELU_C * (1.0 + 3.0 * _GELU_A * x2))
    return g, dg


def _sigmoid(x):
    return 1.0 / (1.0 + jnp.exp(-x))


def _mm(a, b, *, name, ta=False, tb=False, add=None, out_dtype=F32, bm=1024, bn=1024, bk=None):
    if ta:
        K, M = a.shape
    else:
        M, K = a.shape
    N = b.shape[0] if tb else b.shape[1]
    assert (b.shape[1] if tb else b.shape[0]) == K
    bm = _div(M, bm)
    bn = _div(N, bn)
    bk = K if bk is None else _div(K, bk)
    nk = K // bk
    a_spec = pl.BlockSpec((bk, bm), lambda i, j, k: (k, i)) if ta else pl.BlockSpec((bm, bk), lambda i, j, k: (i, k))
    b_spec = pl.BlockSpec((bn, bk), lambda i, j, k: (j, k)) if tb else pl.BlockSpec((bk, bn), lambda i, j, k: (k, j))
    o_spec = pl.BlockSpec((bm, bn), lambda i, j, k: (i, j))
    dims = (((0 if ta else 1,), (1 if tb else 0,)), ((), ()))
    has_add = add is not None

    def body(*refs):
        if has_add:
            a_ref, b_ref, add_ref, o_ref, *scratch = refs
        else:
            a_ref, b_ref, o_ref, *scratch = refs
            add_ref = None
        p = lax.dot_general(a_ref[...].astype(BF16), b_ref[...].astype(BF16), dims, preferred_element_type=F32)
        if nk == 1:
            if has_add:
                p = p + add_ref[...]
            o_ref[...] = p.astype(out_dtype)
        else:
            acc = scratch[0]
            k = pl.program_id(2)

            @pl.when(k == 0)
            def _():
                acc[...] = p

            @pl.when(k > 0)
            def _():
                acc[...] += p

            @pl.when(k == nk - 1)
            def _():
                r = acc[...]
                if has_add:
                    r = r + add_ref[...]
                o_ref[...] = r.astype(out_dtype)

    in_specs = [a_spec, b_spec] + ([o_spec] if has_add else [])
    args = (a, b) + ((add,) if has_add else ())
    return pl.pallas_call(
        body, name=name, grid=(M // bm, N // bn, nk),
        in_specs=in_specs, out_specs=o_spec,
        out_shape=jax.ShapeDtypeStruct((M, N), out_dtype),
        scratch_shapes=[pltpu.VMEM((bm, bn), F32)] if nk > 1 else [],
        compiler_params=_cparams("parallel", "parallel", "arbitrary"),
    )(*args)


def _blocks_per_tile(c):
    nb = 1
    while (nb * c) % LANES or (nb * c < 1024 and nb < N_DEV):
        nb *= 2
    assert nb <= N_DEV and (nb * c) % LANES == 0, c
    return nb


def _mm_w8(a, w8, *, name, bm=1024):
    M, K = a.shape
    _, _, c = w8.shape
    nb = _blocks_per_tile(c)
    bm = _div(M, bm)

    def body(a_ref, w_ref, o_ref):
        a_ = a_ref[...]
        for t in range(nb):
            o_ref[:, t * c:(t + 1) * c] = jnp.dot(a_, w_ref[t], preferred_element_type=F32)

    return pl.pallas_call(
        body, name=name, grid=(M // bm, N_DEV // nb),
        in_specs=[pl.BlockSpec((bm, K), lambda i, j: (i, 0)), pl.BlockSpec((nb, K, c), lambda i, j: (j, 0, 0))],
        out_specs=pl.BlockSpec((bm, nb * c), lambda i, j: (i, j)),
        out_shape=jax.ShapeDtypeStruct((M, N_DEV * c), F32),
        compiler_params=_cparams("parallel", "parallel"),
    )(a, w8)


def _mm_w8t(dy, w8, *, name, add=None, out_dtype=F32, bm=1024, bn=1024, after=None):
    M = dy.shape[0]
    _, K, c = w8.shape
    nb = _blocks_per_tile(c)
    nk = N_DEV // nb
    bm, bn = _div(M, bm), _div(K, bn)
    has_add = add is not None
    dims = (((1,), (1,)), ((), ()))

    def body(*refs):
        if has_add:
            dy_ref, w_ref, add_ref, o_ref, acc = refs
        else:
            dy_ref, w_ref, o_ref, acc = refs
        p = lax.dot_general(dy_ref[:, 0:c], w_ref[0], dims, preferred_element_type=F32)
        for t in range(1, nb):
            p = p + lax.dot_general(dy_ref[:, t * c:(t + 1) * c], w_ref[t], dims, preferred_element_type=F32)
        k = pl.program_id(2)

        @pl.when(k == 0)
        def _():
            acc[...] = p

        @pl.when(k > 0)
        def _():
            acc[...] += p

        @pl.when(k == nk - 1)
        def _():
            r = acc[...]
            if has_add:
                r = r + add_ref[...]
            o_ref[...] = r.astype(out_dtype)

    o_spec = pl.BlockSpec((bm, bn), lambda i, j, k: (i, j))
    in_specs = [pl.BlockSpec((bm, nb * c), lambda i, j, k: (i, k)), pl.BlockSpec((nb, bn, c), lambda i, j, k: (k, j, 0))]
    in_specs += [o_spec] if has_add else []
    args = (dy, w8) + ((add,) if has_add else ())
    body, in_specs, args = _ordered_after(body, len(args), in_specs, args, after)
    return pl.pallas_call(
        body, name=name, grid=(M // bm, K // bn, nk),
        in_specs=in_specs, out_specs=o_spec,
        out_shape=jax.ShapeDtypeStruct((M, K), out_dtype),
        scratch_shapes=[pltpu.VMEM((bm, bn), F32)],
        compiler_params=_cparams("parallel", "parallel", "arbitrary"),
    )(*args)


def _mm_gw8(x, dy, c, *, name, bk=1024):
    T, K = x.shape
    nb = _blocks_per_tile(c)
    bk = _div(K, bk)
    dims = (((0,), (0,)), ((), ()))

    def body(x_ref, dy_ref, o_ref):
        x_ = x_ref[...]
        for t in range(nb):
            o_ref[t] = lax.dot_general(x_, dy_ref[:, t * c:(t + 1) * c], dims, preferred_element_type=F32).astype(BF16)

    return pl.pallas_call(
        body, name=name, grid=(K // bk, N_DEV // nb),
        in_specs=[pl.BlockSpec((T, bk), lambda i, j: (0, i)), pl.BlockSpec((T, nb * c), lambda i, j: (0, j))],
        out_specs=pl.BlockSpec((nb, bk, c), lambda i, j: (j, i, 0)),
        out_shape=jax.ShapeDtypeStruct((N_DEV, K, c), BF16),
        compiler_params=_cparams("parallel", "parallel"),
    )(x, dy)


def _rms_fwd(x, g, *, name):
    T, D = x.shape
    tm = _div(T, 256, 8)

    def body(x_ref, g_ref, h_ref):
        xf = x_ref[...]
        r = lax.rsqrt(jnp.mean(xf * xf, axis=-1, keepdims=True) + EPS)
        h_ref[...] = ((xf * r) * g_ref[...]).astype(BF16)

    return pl.pallas_call(
        body, name=name, grid=(T // tm,),
        in_specs=[pl.BlockSpec((tm, D), lambda i: (i, 0)), pl.BlockSpec((1, D), lambda i: (0, 0))],
        out_specs=pl.BlockSpec((tm, D), lambda i: (i, 0)),
        out_shape=jax.ShapeDtypeStruct((T, D), BF16),
        compiler_params=_cparams("parallel"),
    )(x, g)


def _rms_bwd(x, g, dh, dres, *, name, want_bf16, after=None):
    T, D = x.shape
    tm = _div(T, 256, 8)

    def body(x_ref, g_ref, dh_ref, dres_ref, dx_ref, *rest):
        if want_bf16:
            dxb_ref, dg_ref = rest
        else:
            (dg_ref,) = rest
        xf = x_ref[...]
        r = lax.rsqrt(jnp.mean(xf * xf, axis=-1, keepdims=True) + EPS)
        xhat = xf * r
        dh_ = dh_ref[...]
        dy = dh_ * g_ref[...]
        dx = dres_ref[...] + r * (dy - xhat * jnp.mean(dy * xhat, axis=-1, keepdims=True))
        dx_ref[...] = dx
        if want_bf16:
            dxb_ref[...] = dx.astype(BF16)
        part = jnp.sum(dh_ * xhat, axis=0, keepdims=True)

        @pl.when(pl.program_id(0) == 0)
        def _():
            dg_ref[...] = part

        @pl.when(pl.program_id(0) > 0)
        def _():
            dg_ref[...] += part

    row = pl.BlockSpec((tm, D), lambda i: (i, 0))
    vec = pl.BlockSpec((1, D), lambda i: (0, 0))
    out_specs = [row] + ([row] if want_bf16 else []) + [vec]
    out_shape = ([jax.ShapeDtypeStruct((T, D), F32)] + ([jax.ShapeDtypeStruct((T, D), BF16)] if want_bf16 else [])
                 + [jax.ShapeDtypeStruct((1, D), F32)])
    body, in_specs, args = _ordered_after(body, 4, [row, vec, row, row], (x, g, dh, dres), after)
    return pl.pallas_call(
        body, name=name, grid=(T // tm,),
        in_specs=in_specs, out_specs=out_specs, out_shape=out_shape,
        compiler_params=_cparams("arbitrary"),
    )(*args)


def _loss_head(x, g, target, *, name):
    T, D = x.shape
    tm = _div(T, 256, 8)

    def body(x_ref, g_ref, t_ref, loss_ref, dx_ref, dxb_ref, dg_ref):
        xf = x_ref[...]
        r = lax.rsqrt(jnp.mean(xf * xf, axis=-1, keepdims=True) + EPS)
        xhat = xf * r
        gain = g_ref[...]
        err = xhat * gain - t_ref[...]
        lpart = 0.5 * jnp.sum(jnp.mean(err * err, axis=-1, keepdims=True), axis=0, keepdims=True)
        dh_ = err * (1.0 / D)
        dy = dh_ * gain
        dx = r * (dy - xhat * jnp.mean(dy * xhat, axis=-1, keepdims=True))
        dx_ref[...] = dx
        dxb_ref[...] = dx.astype(BF16)
        part = jnp.sum(dh_ * xhat, axis=0, keepdims=True)

        @pl.when(pl.program_id(0) == 0)
        def _():
            dg_ref[...] = part
            loss_ref[...] = jnp.broadcast_to(lpart, loss_ref.shape)

        @pl.when(pl.program_id(0) > 0)
        def _():
            dg_ref[...] += part
            loss_ref[...] += jnp.broadcast_to(lpart, loss_ref.shape)

    row = pl.BlockSpec((tm, D), lambda i: (i, 0))
    vec = pl.BlockSpec((1, D), lambda i: (0, 0))
    return pl.pallas_call(
        body, name=name, grid=(T // tm,),
        in_specs=[row, vec, row],
        out_specs=[pl.BlockSpec((8, LANES), lambda i: (0, 0)), row, row, vec],
        out_shape=[jax.ShapeDtypeStruct((8, LANES), F32), jax.ShapeDtypeStruct((T, D), F32),
                   jax.ShapeDtypeStruct((T, D), BF16), jax.ShapeDtypeStruct((1, D), F32)],
        compiler_params=_cparams("arbitrary"),
    )(x, g, target)


def _gate_cols(D):
    off_a = 3 * D // 2 + 2 * KV_WIDTH
    off_b = off_a + D
    cw = math.gcd(math.gcd(off_a, off_b), math.gcd(D, 512))
    return cw, off_a // cw, off_b // cw


def _merge_fwd(z, ya, yb, *, name):
    T, D = ya.shape
    cw, ba, bb = _gate_cols(D)
    tm = _div(T, 512, 8)

    def body(ga_ref, gb_ref, ya_ref, yb_ref, m_ref):
        m_ref[...] = (_sigmoid(ga_ref[...]) * ya_ref[...] + _sigmoid(gb_ref[...]) * yb_ref[...]).astype(BF16)

    blk = pl.BlockSpec((tm, cw), lambda i, j: (i, j))
    return pl.pallas_call(
        body, name=name, grid=(T // tm, D // cw),
        in_specs=[pl.BlockSpec((tm, cw), lambda i, j: (i, ba + j)), pl.BlockSpec((tm, cw), lambda i, j: (i, bb + j)), blk, blk],
        out_specs=blk, out_shape=jax.ShapeDtypeStruct((T, D), BF16),
        compiler_params=_cparams("parallel", "parallel"),
    )(z, z, ya, yb)


def _merge_bwd(z, ya, yb, dm, *, name, after=None):
    T, D = ya.shape
    cw, ba, bb = _gate_cols(D)
    tm = _div(T, 512, 8)

    def body(ga_ref, gb_ref, ya_ref, yb_ref, dm_ref, dya_ref, dyb_ref, dga_ref, dgb_ref):
        sa = _sigmoid(ga_ref[...])
        sb = _sigmoid(gb_ref[...])
        dm_ = dm_ref[...]
        dya_ref[...] = (dm_ * sa).astype(BF16)
        dyb_ref[...] = (dm_ * sb).astype(BF16)
        dga_ref[...] = (dm_ * ya_ref[...] * (sa * (1.0 - sa))).astype(BF16)
        dgb_ref[...] = (dm_ * yb_ref[...] * (sb * (1.0 - sb))).astype(BF16)

    blk = pl.BlockSpec((tm, cw), lambda i, j: (i, j))
    out = jax.ShapeDtypeStruct((T, D), BF16)
    in_specs = [pl.BlockSpec((tm, cw), lambda i, j: (i, ba + j)), pl.BlockSpec((tm, cw), lambda i, j: (i, bb + j)), blk, blk, blk]
    body, in_specs, args = _ordered_after(body, 5, in_specs, (z, z, ya, yb, dm), after)
    return pl.pallas_call(
        body, name=name, grid=(T // tm, D // cw),
        in_specs=in_specs, out_specs=[blk] * 4, out_shape=[out] * 4,
        compiler_params=_cparams("parallel", "parallel"),
    )(*args)


def _swiglu_fwd(gate, up, *, name):
    T, F = gate.shape
    tm, cw = _div(T, 512, 8), _div(F, 512)

    def body(g_ref, u_ref, act_ref):
        g = g_ref[...]
        act_ref[...] = (g * _sigmoid(g) * u_ref[...]).astype(BF16)

    blk = pl.BlockSpec((tm, cw), lambda i, j: (i, j))
    return pl.pallas_call(
        body, name=name, grid=(T // tm, F // cw), in_specs=[blk, blk], out_specs=blk,
        out_shape=jax.ShapeDtypeStruct((T, F), BF16), compiler_params=_cparams("parallel", "parallel"),
    )(gate, up)


def _swiglu_bwd(gate, up, dact, *, name, after=None):
    T, F = gate.shape
    tm, cw = _div(T, 512, 8), _div(F, 512)

    def body(g_ref, u_ref, d_ref, dg_ref, du_ref):
        g = g_ref[...]
        s = _sigmoid(g)
        d = d_ref[...]
        silu = g * s
        dg_ref[...] = (d * u_ref[...] * (s + silu * (1.0 - s))).astype(BF16)
        du_ref[...] = (d * silu).astype(BF16)

    blk = pl.BlockSpec((tm, cw), lambda i, j: (i, j))
    out = jax.ShapeDtypeStruct((T, F), BF16)
    body, in_specs, args = _ordered_after(body, 3, [blk, blk, blk], (gate, up, dact), after)
    return pl.pallas_call(
        body, name=name, grid=(T // tm, F // cw), in_specs=in_specs, out_specs=[blk, blk],
        out_shape=[out, out], compiler_params=_cparams("parallel", "parallel"),
    )(*args)


def _sgu_fwd(z, gain, ws_b, bs_t, *, name):
    T = z.shape[0]
    SW = gain.shape[1]
    G = SW // BLOCK

    def body(zu_ref, zv_ref, gain_ref, ws_ref, bs_ref, a_ref):
        u = _gelu(zu_ref[...])
        vg = _gelu(zv_ref[...])
        r = lax.rsqrt(jnp.mean(vg * vg, axis=-1, keepdims=True) + EPS)
        vn = ((vg * r) * gain_ref[...]).astype(BF16)
        for g in range(G):
            sl = slice(g * BLOCK, (g + 1) * BLOCK)
            mixed = jnp.dot(ws_ref[g], vn[:, sl], preferred_element_type=F32) + bs_ref[:, g:g + 1]
            a_ref[:, sl] = (u[:, sl] * mixed).astype(BF16)

    return pl.pallas_call(
        body, name=name, grid=(T // BLOCK,),
        in_specs=[pl.BlockSpec((BLOCK, SW), lambda c: (c, 0)), pl.BlockSpec((BLOCK, SW), lambda c: (c, 1)),
                  pl.BlockSpec((1, SW), lambda c: (0, 0)), pl.BlockSpec((G, BLOCK, BLOCK), lambda c: (0, 0, 0)),
                  pl.BlockSpec((BLOCK, G), lambda c: (0, 0))],
        out_specs=pl.BlockSpec((BLOCK, SW), lambda c: (c, 0)),
        out_shape=jax.ShapeDtypeStruct((T, SW), BF16),
        compiler_params=_cparams("parallel"),
    )(z, z, gain, ws_b, bs_t)


def _sgu_bwd(z, gain, ws_b, bs_t, da, *, name, after=None):
    T = z.shape[0]
    SW = gain.shape[1]
    G = SW // BLOCK

    def body(zu_ref, zv_ref, gain_ref, ws_ref, bs_ref, da_ref, dzu_ref, dzv_ref, dws_ref, dbs_ref, dgain_ref, dvn_ref):
        first = pl.program_id(0) == 0

        @pl.when(first)
        def _():
            dws_ref[...] = jnp.zeros_like(dws_ref)
            dbs_ref[...] = jnp.zeros_like(dbs_ref)
            dgain_ref[...] = jnp.zeros_like(dgain_ref)

        u, du = _gelu_and_grad(zu_ref[...])
        vg, dvg = _gelu_and_grad(zv_ref[...])
        r = lax.rsqrt(jnp.mean(vg * vg, axis=-1, keepdims=True) + EPS)
        xhat = vg * r
        gain_ = gain_ref[...]
        vn = (xhat * gain_).astype(BF16)
        da_ = da_ref[...]
        for g in range(G):
            sl = slice(g * BLOCK, (g + 1) * BLOCK)
            w = ws_ref[g]
            mixed = jnp.dot(w, vn[:, sl], preferred_element_type=F32) + bs_ref[:, g:g + 1]
            dmix = da_[:, sl] * u[:, sl]
            dzu_ref[:, sl] = (da_[:, sl] * mixed * du[:, sl]).astype(BF16)
            dmb = dmix.astype(BF16)
            dws_ref[g] += lax.dot_general(dmb, vn[:, sl], (((1,), (1,)), ((), ())), preferred_element_type=F32)
            dbs_ref[:, g:g + 1] += jnp.sum(dmix, axis=-1, keepdims=True)
            dvn_ref[:, sl] = lax.dot_general(w, dmb, (((0,), (0,)), ((), ())), preferred_element_type=F32)
        dvn = dvn_ref[...]
        dgain_ref[...] += jnp.sum(dvn * xhat, axis=0, keepdims=True)
        dy = dvn * gain_
        dv_ = r * (dy - xhat * jnp.mean(dy * xhat, axis=-1, keepdims=True))
        dzv_ref[...] = (dv_ * dvg).astype(BF16)

    row = pl.BlockSpec((BLOCK, SW), lambda c: (c, 0))
    in_specs = [row, pl.BlockSpec((BLOCK, SW), lambda c: (c, 1)),
                pl.BlockSpec((1, SW), lambda c: (0, 0)), pl.BlockSpec((G, BLOCK, BLOCK), lambda c: (0, 0, 0)),
                pl.BlockSpec((BLOCK, G), lambda c: (0, 0)), row]
    body, in_specs, args = _ordered_after(body, 6, in_specs, (z, z, gain, ws_b, bs_t, da), after)
    return pl.pallas_call(
        body, name=name, grid=(T // BLOCK,),
        in_specs=in_specs,
        out_specs=[row, row, pl.BlockSpec((G, BLOCK, BLOCK), lambda c: (0, 0, 0)),
                   pl.BlockSpec((BLOCK, G), lambda c: (0, 0)), pl.BlockSpec((1, SW), lambda c: (0, 0))],
        out_shape=[jax.ShapeDtypeStruct((T, SW), BF16), jax.ShapeDtypeStruct((T, SW), BF16),
                   jax.ShapeDtypeStruct((G, BLOCK, BLOCK), F32), jax.ShapeDtypeStruct((BLOCK, G), F32),
                   jax.ShapeDtypeStruct((1, SW), F32)],
        scratch_shapes=[pltpu.VMEM((BLOCK, SW), F32)],
        compiler_params=_cparams("arbitrary"),
    )(*args)


def _bias_table(rel_bias, bmap, *, name):
    H = rel_bias.shape[1]

    def body(rb_ref, bmap_ref, o_ref):
        bm_ = bmap_ref[...]
        for h in range(H):
            acc = jnp.zeros(bm_.shape, F32)
            for b in range(REL_BUCKETS):
                acc = jnp.where(bm_ == b, rb_ref[b, h], acc)
            o_ref[h] = acc

    return pl.pallas_call(
        body, name=name,
        in_specs=[pl.BlockSpec(memory_space=pltpu.SMEM), pl.BlockSpec(memory_space=pltpu.VMEM)],
        out_specs=pl.BlockSpec(memory_space=pltpu.VMEM),
        out_shape=jax.ShapeDtypeStruct((H, BLOCK, 3 * BLOCK), F32),
    )(rel_bias, bmap)


def _attn_probs(q_ref, kb, bias_ref, sink_ref, valid, h, group):
    kv = h // group
    qh = q_ref[:, h * HEAD_DIM:(h + 1) * HEAD_DIM].astype(BF16)
    s = lax.dot_general(qh, kb[:, kv * HEAD_DIM:(kv + 1) * HEAD_DIM], (((1,), (1,)), ((), ())),
                        preferred_element_type=F32)
    s = s * (HEAD_DIM ** -0.5) + bias_ref[h]
    s = jnp.where(valid, s, NEG)
    sink = sink_ref[0:1, h:h + 1]
    m = jnp.maximum(jnp.max(s, axis=-1, keepdims=True), sink)
    e = jnp.exp(s - m)
    es = jnp.exp(sink - m)
    inv = 1.0 / (jnp.sum(e, axis=-1, keepdims=True) + es)
    return e * inv, es * inv, qh


def _band_valid(n, T):
    row = lax.broadcasted_iota(jnp.int32, (BLOCK, 3 * BLOCK), 0)
    col = lax.broadcasted_iota(jnp.int32, (BLOCK, 3 * BLOCK), 1)
    rel = col - BLOCK - row
    key_pos = n * BLOCK + col - BLOCK
    return (jnp.abs(rel) <= BLOCK) & (key_pos >= 0) & (key_pos < T)


def _attn_fwd(z, kpad, vpad, bias, sink, *, name):
    T = z.shape[0]
    H = bias.shape[0]
    AW = H * HEAD_DIM
    group = H // N_KV_HEADS

    def body(q_ref, k_ref, v_ref, bias_ref, sink_ref, o_ref):
        n = pl.program_id(0)
        start = pl.multiple_of(n * BLOCK, BLOCK)
        kb = k_ref[pl.ds(start, 3 * BLOCK), :]
        vb = v_ref[pl.ds(start, 3 * BLOCK), :]
        valid = _band_valid(n, T)
        for h in range(H):
            kv = h // group
            p, _, _ = _attn_probs(q_ref, kb, bias_ref, sink_ref, valid, h, group)
            o = jnp.dot(p.astype(BF16), vb[:, kv * HEAD_DIM:(kv + 1) * HEAD_DIM], preferred_element_type=F32)
            o_ref[:, h * HEAD_DIM:(h + 1) * HEAD_DIM] = o.astype(BF16)

    full_kv = pl.BlockSpec((T + 2 * BLOCK, KV_WIDTH), lambda n: (0, 0))
    return pl.pallas_call(
        body, name=name, grid=(T // BLOCK,),
        in_specs=[pl.BlockSpec((BLOCK, AW), lambda n: (n, 2)), full_kv, full_kv,
                  pl.BlockSpec((H, BLOCK, 3 * BLOCK), lambda n: (0, 0, 0)), pl.BlockSpec((1, H), lambda n: (0, 0))],
        out_specs=pl.BlockSpec((BLOCK, AW), lambda n: (n, 0)),
        out_shape=jax.ShapeDtypeStruct((T, AW), BF16),
        compiler_params=_cparams("parallel"),
    )(z, kpad, vpad, bias, sink)


def _attn_bwd(z, kpad, vpad, bias, sink, do, *, name):
    T = z.shape[0]
    H = bias.shape[0]
    AW = H * HEAD_DIM
    group = H // N_KV_HEADS
    scale = HEAD_DIM ** -0.5

    def body(q_ref, k_ref, v_ref, bias_ref, sink_ref, do_ref, dq_ref, dk_ref, dv_ref, dbias_ref, dsink_ref):
        n = pl.program_id(0)

        @pl.when(n == 0)
        def _():
            dk_ref[...] = jnp.zeros_like(dk_ref)
            dv_ref[...] = jnp.zeros_like(dv_ref)
            dbias_ref[...] = jnp.zeros_like(dbias_ref)
            dsink_ref[...] = jnp.zeros_like(dsink_ref)

        start = pl.multiple_of(n * BLOCK, BLOCK)
        kb = k_ref[pl.ds(start, 3 * BLOCK), :]
        vb = v_ref[pl.ds(start, 3 * BLOCK), :]
        valid = _band_valid(n, T)
        for kv in range(N_KV_HEADS):
            ksl = slice(kv * HEAD_DIM, (kv + 1) * HEAD_DIM)
            dk_acc = jnp.zeros((3 * BLOCK, HEAD_DIM), F32)
            dv_acc = jnp.zeros((3 * BLOCK, HEAD_DIM), F32)
            for gi in range(group):
                h = kv * group + gi
                hsl = slice(h * HEAD_DIM, (h + 1) * HEAD_DIM)
                p, p_sink, qh = _attn_probs(q_ref, kb, bias_ref, sink_ref, valid, h, group)
                doh = do_ref[:, hsl]
                dp = lax.dot_general(doh, vb[:, ksl], (((1,), (1,)), ((), ())), preferred_element_type=F32)
                delta = jnp.sum(p * dp, axis=-1, keepdims=True)
                ds = p * (dp - delta)
                dbias_ref[h] += ds
                dsink_ref[:, h:h + 1] += -(p_sink * delta)
                dsb = ds.astype(BF16)
                dq = jnp.dot(dsb, kb[:, ksl], preferred_element_type=F32) * scale
                dq_ref[:, hsl] = dq.astype(BF16)
                dk_acc = dk_acc + lax.dot_general(dsb, qh, (((0,), (0,)), ((), ())), preferred_element_type=F32)
                dv_acc = dv_acc + lax.dot_general(p.astype(BF16), doh, (((0,), (0,)), ((), ())),
                                                  preferred_element_type=F32)
            dk_ref[pl.ds(start, 3 * BLOCK), ksl] += dk_acc * scale
            dv_ref[pl.ds(start, 3 * BLOCK), ksl] += dv_acc

    full_kv = pl.BlockSpec((T + 2 * BLOCK, KV_WIDTH), lambda n: (0, 0))
    bias_spec = pl.BlockSpec((H, BLOCK, 3 * BLOCK), lambda n: (0, 0, 0))
    row = pl.BlockSpec((BLOCK, AW), lambda n: (n, 0))
    return pl.pallas_call(
        body, name=name, grid=(T // BLOCK,),
        in_specs=[pl.BlockSpec((BLOCK, AW), lambda n: (n, 2)), full_kv, full_kv, bias_spec,
                  pl.BlockSpec((1, H), lambda n: (0, 0)), row],
        out_specs=[row, full_kv, full_kv, bias_spec, pl.BlockSpec((BLOCK, H), lambda n: (0, 0))],
        out_shape=[jax.ShapeDtypeStruct((T, AW), BF16),
                   jax.ShapeDtypeStruct((T + 2 * BLOCK, KV_WIDTH), F32), jax.ShapeDtypeStruct((T + 2 * BLOCK, KV_WIDTH), F32),
                   jax.ShapeDtypeStruct((H, BLOCK, 3 * BLOCK), F32), jax.ShapeDtypeStruct((BLOCK, H), F32)],
        compiler_params=_cparams("arbitrary"),
    )(z, kpad, vpad, bias, sink, do)


def _attn_small_grads(dbias, dsink_rows, bmap, *, name):
    H = dbias.shape[0]

    def body(dbias_ref, dsink_ref, bmap_ref, drel_ref, ds_ref):
        bm_ = bmap_ref[...]
        for h in range(H):
            d = dbias_ref[h]
            for b in range(REL_BUCKETS):
                drel_ref[b, h] = jnp.sum(jnp.where(bm_ == b, d, 0.0))
            ds_ref[0, h] = jnp.sum(dsink_ref[:, h:h + 1])

    vmem = pl.BlockSpec(memory_space=pltpu.VMEM)
    smem = pl.BlockSpec(memory_space=pltpu.SMEM)
    return pl.pallas_call(
        body, name=name, in_specs=[vmem, vmem, vmem], out_specs=[smem, smem],
        out_shape=[jax.ShapeDtypeStruct((REL_BUCKETS, H), F32), jax.ShapeDtypeStruct((1, H), F32)],
    )(dbias, dsink_rows, bmap)


def _local_step(x, target, weight, emit, norm_mix, v_gain, w_s, b_s, sink, rel_bias, norm_ffn, norm_final):
    T, D = x.shape
    SW = D // 2
    off_k = D + SW
    ws_b = w_s.astype(BF16)
    bs_t = b_s.T
    bmap = jnp.asarray(_bucket_map())

    h = _rms_fwd(x, norm_mix, name="rms_mix")
    w_in = weight("w_in", h)
    z = _mm_w8(h, w_in, name="mm_z")
    a = _sgu_fwd(z, v_gain, ws_b, bs_t, name="sgu_fwd")
    w_a = weight("w_a_out", a)
    ya = _mm_w8(a, w_a, name="mm_ya", bm=2048)
    pad = ((BLOCK, BLOCK), (0, 0))
    kpad = jnp.pad(z[:, off_k:off_k + KV_WIDTH].astype(BF16), pad)
    vpad = jnp.pad(z[:, off_k + KV_WIDTH:off_k + 2 * KV_WIDTH].astype(BF16), pad)
    bias = _bias_table(rel_bias, bmap, name="bias_table")
    o = _attn_fwd(z, kpad, vpad, bias, sink, name="attn_fwd")
    w_b = weight("w_b_out", o)
    yb = _mm_w8(o, w_b, name="mm_yb", bm=2048)
    m = _merge_fwd(z, ya, yb, name="merge_fwd")
    w_o = weight("w_o", m)
    x1 = _mm(m, w_o, name="mm_x1", add=x, bm=2048, bn=512)
    h2 = _rms_fwd(x1, norm_ffn, name="rms_ffn")
    w_gate = weight("w_gate", h2)
    w_up = weight("w_up", h2)
    gate = _mm_w8(h2, w_gate, name="mm_gate")
    up = _mm_w8(h2, w_up, name="mm_up")
    act = _swiglu_fwd(gate, up, name="swiglu_fwd")
    w_down = weight("w_down", act)
    x2 = _mm(act, w_down, name="mm_x2", add=x1, bm=1024, bn=1024, bk=2816)
    loss, dx2, dx2b, g_norm_final = _loss_head(x2, norm_final, target, name="loss_head")

    g_w_down = _mm(act, dx2b, ta=True, out_dtype=BF16, name="mm_gwdown", bm=512, bn=2048)
    tok = emit(("w_down",), (g_w_down,))
    dact = _mm(dx2b, w_down, tb=True, name="mm_dact", bm=2048, bn=512)
    dgate, dup = _swiglu_bwd(gate, up, dact, name="swiglu_bwd", after=tok)
    g_w_gate = _mm_gw8(h2, dgate, w_gate.shape[2], name="mm_gwgate")
    g_w_up = _mm_gw8(h2, dup, w_up.shape[2], name="mm_gwup")
    tok = emit(("w_gate", "w_up"), (g_w_gate, g_w_up))
    dh2 = _mm_w8t(dgate, w_gate, name="mm_dh2a", after=tok)
    dh2 = _mm_w8t(dup, w_up, add=dh2, name="mm_dh2b")
    dx1, dx1b, g_norm_ffn = _rms_bwd(x1, norm_ffn, dh2, dx2, name="rms_ffn_bwd", want_bf16=True)

    g_w_o = _mm(m, dx1b, ta=True, out_dtype=BF16, name="mm_gwo", bm=2048, bn=512)
    tok = emit(("w_o",), (g_w_o,))
    dm = _mm(dx1b, w_o, tb=True, name="mm_dm", bm=2048, bn=512)
    dya, dyb, dga, dgb = _merge_bwd(z, ya, yb, dm, name="merge_bwd", after=tok)
    g_w_a = _mm_gw8(a, dya, w_a.shape[2], name="mm_gwa")
    g_w_b = _mm_gw8(o, dyb, w_b.shape[2], name="mm_gwb")
    tok = emit(("w_a_out", "w_b_out"), (g_w_a, g_w_b))
    da = _mm_w8t(dya, w_a, name="mm_da", bm=2048, bn=512, after=tok)
    do = _mm_w8t(dyb, w_b, out_dtype=BF16, name="mm_do", bm=2048, bn=512)
    dzu, dzv, g_w_s, g_b_s_t, g_v_gain = _sgu_bwd(z, v_gain, ws_b, bs_t, da, name="sgu_bwd")
    dq, dkp, dvp, dbias, dsink_rows = _attn_bwd(z, kpad, vpad, bias, sink, do, name="attn_bwd")
    g_rel_bias, g_sink = _attn_small_grads(dbias, dsink_rows, bmap, name="attn_small_grads")
    dz = jnp.concatenate([dzu, dzv, dq, dkp[BLOCK:BLOCK + T].astype(BF16), dvp[BLOCK:BLOCK + T].astype(BF16), dga, dgb], axis=1)
    g_w_in = _mm_gw8(h, dz, w_in.shape[2], name="mm_gwin")
    tok = emit(("w_in",), (g_w_in,))
    dh = _mm_w8t(dz, w_in, name="mm_dh", after=tok)
    grad_x, g_norm_mix = _rms_bwd(x, norm_mix, dh, dx1, name="rms_mix_bwd", want_bf16=False)

    small = dict(norm_mix=g_norm_mix, sgu_v_gain=g_v_gain, sgu_w_s=g_w_s, sgu_b_s=g_b_s_t.T, attn_sink=g_sink,
                 rel_bias=g_rel_bias, norm_ffn=g_norm_ffn, norm_final=g_norm_final)
    return loss, grad_x, small


def _position():
    return lax.axis_index("x"), lax.axis_index("y"), lax.axis_index("c")


def _other_chips(x, y):
    return [(1 - x, y), (x, 1 - y), (1 - x, 1 - y)]


def _slot(px, py, pc):
    return 4 * px + 2 * py + pc


_HBM = pl.BlockSpec(memory_space=pltpu.HBM)
_SEM = pl.BlockSpec(memory_space=pltpu.SEMAPHORE)
_DATAFLOW = pltpu.SideEffectType.DATAFLOW_SIDE_EFFECTING


def _in_hbm(a):
    return pltpu.with_memory_space_constraint(a, pltpu.HBM)


def _ag_copies(w, shard_ref, land_ref, send_sems, recv_sems):
    x, y, c = _position()
    mine = land_ref.at[_slot(x, y, c)]
    targets = [(px, py, c) for px, py in _other_chips(x, y)] + [(x, y, 1 - c)]
    return [pltpu.make_async_remote_copy(src_ref=shard_ref, dst_ref=mine, send_sem=send_sems.at[4 * w + k],
                                         recv_sem=recv_sems.at[4 * w + k], device_id=to, device_id_type=MESH)
            for k, to in enumerate(targets)]


def _ag_start(shards, groups, *, name):
    order = [i for g in groups for i in g]
    srcs = [shards[i] for i in order]
    lands = [lax.empty((N_DEV,) + s.shape, s.dtype) for s in srcs]
    n, ng = len(srcs), len(groups)
    sizes = [len(g) for g in groups]

    def body(*refs):
        shard_refs, land_refs = refs[:n], refs[n:2 * n]
        sems = refs[2 * n:2 * n + 2 * ng]
        i = 0
        for g in range(ng):
            for w in range(sizes[g]):
                for cp in _ag_copies(w, shard_refs[i], land_refs[i], sems[2 * g], sems[2 * g + 1]):
                    cp.start()
                i += 1

    sem_shapes = [pltpu.SemaphoreType.DMA((4 * k,)) for k in sizes for _ in range(2)]
    outs = pl.pallas_call(
        body, name=name,
        in_specs=[_HBM] * (2 * n),
        out_specs=tuple([_SEM] * (2 * ng) + [_HBM] * (2 * n)),
        out_shape=tuple(sem_shapes + [pltpu.HBM(a.shape, a.dtype) for a in srcs + lands]),
        input_output_aliases={i: 2 * ng + i for i in range(2 * n)},
        compiler_params=pltpu.CompilerParams(has_side_effects=_DATAFLOW),
    )(*[_in_hbm(a) for a in srcs + lands])
    sems, thru = outs[:2 * ng], outs[2 * ng:]
    result, i = [], 0
    for g in range(ng):
        k = sizes[g]
        result.append((sems[2 * g], sems[2 * g + 1], list(thru[i:i + k]), list(thru[n + i:n + i + k])))
        i += k
    return result


def _ag_wait(send_sems, recv_sems, shards, lands, after, *, name):
    n = len(shards)

    def body(*refs):
        shard_refs, land_refs = refs[:n], refs[n:2 * n]
        send_ref, recv_ref = refs[2 * n], refs[2 * n + 1]
        for w in range(n):
            for cp in _ag_copies(w, shard_refs[w], land_refs[w], send_ref, recv_ref):
                cp.wait_send()
                cp.wait_recv()

    outs = pl.pallas_call(
        body, name=name,
        in_specs=[_HBM] * (2 * n) + [_SEM, _SEM, _ANY],
        out_specs=tuple([_HBM] * (2 * n)),
        out_shape=tuple(pltpu.HBM(a.shape, a.dtype) for a in shards + lands),
        input_output_aliases={i: i for i in range(2 * n)},
        compiler_params=pltpu.CompilerParams(has_side_effects=_DATAFLOW),
    )(*shards, *lands, send_sems, recv_sems, after)
    return list(outs[n:])


def _ag_forward(shards, lands, *, name):
    n = len(shards)

    def body(*refs):
        shard_refs, in_refs, out_refs = refs[:n], refs[n:2 * n], refs[2 * n:3 * n]
        send_sems, recv_sems, local_sems = refs[3 * n:]
        x, y, c = _position()
        copies = []
        for w in range(n):
            mine = pltpu.make_async_copy(shard_refs[w], out_refs[w].at[_slot(x, y, c)], local_sems.at[w])
            mine.start()
            copies.append(mine)
            for k, (px, py) in enumerate(_other_chips(x, y)):
                cp = pltpu.make_async_remote_copy(
                    src_ref=in_refs[w].at[_slot(px, py, c)], dst_ref=out_refs[w].at[_slot(px, py, c)],
                    send_sem=send_sems.at[3 * w + k], recv_sem=recv_sems.at[3 * w + k],
                    device_id=(x, y, 1 - c), device_id_type=MESH)
                cp.start()
                copies.append(cp)
        for cp in copies:
            cp.wait()

    return pl.pallas_call(
        body, name=name,
        in_specs=[_ANY] * (2 * n), out_specs=[_ANY] * n,
        out_shape=[jax.ShapeDtypeStruct(a.shape, a.dtype) for a in lands],
        input_output_aliases={n + i: i for i in range(n)},
        scratch_shapes=[pltpu.SemaphoreType.DMA((3 * n,)), pltpu.SemaphoreType.DMA((3 * n,)), pltpu.SemaphoreType.DMA((n,))],
    )(*shards, *lands)


def _rs_to_sibling(grads8, *, name):
    n = len(grads8)

    def body(*refs):
        ins, outs = refs[:n], refs[n:2 * n]
        send_sems, recv_sems = refs[2 * n:]
        x, y, c = _position()
        sibling = (x, y, 1 - c)
        copies = []
        for w in range(n):
            for p in range(4):
                cp = pltpu.make_async_remote_copy(
                    src_ref=ins[w].at[2 * p + (1 - c)], dst_ref=outs[w].at[p],
                    send_sem=send_sems.at[w * 4 + p], recv_sem=recv_sems.at[w * 4 + p],
                    device_id=sibling, device_id_type=MESH)
                cp.start()
                copies.append(cp)
        for cp in copies:
            cp.wait()

    return pl.pallas_call(
        body, name=name,
        in_specs=[_ANY] * n, out_specs=[_ANY] * n,
        out_shape=[jax.ShapeDtypeStruct((4,) + g.shape[1:], g.dtype) for g in grads8],
        scratch_shapes=[pltpu.SemaphoreType.DMA((4 * n,)), pltpu.SemaphoreType.DMA((4 * n,))],
    )(*grads8)


def _chip_sums(g8, from_sibling, pos, *, name):
    _, R, C = g8.shape
    tr = _div(R, 512, 16)

    def body(pos_ref, g_ref, s_ref, o_ref):
        o_ref[...] = (g_ref[...].astype(F32) + s_ref[...].astype(F32)).astype(BF16)

    grid_spec = pltpu.PrefetchScalarGridSpec(
        num_scalar_prefetch=1, grid=(4, R // tr),
        in_specs=[pl.BlockSpec((None, tr, C), lambda p, i, pos_ref: (2 * p + pos_ref[2], i, 0)),
                  pl.BlockSpec((None, tr, C), lambda p, i, pos_ref: (p, i, 0))],
        out_specs=pl.BlockSpec((None, tr, C), lambda p, i, pos_ref: (p, i, 0)))
    return pl.pallas_call(
        body, name=name, grid_spec=grid_spec,
        out_shape=jax.ShapeDtypeStruct((4, R, C), BF16),
        compiler_params=_cparams("parallel", "parallel"),
    )(pos, g8, from_sibling)


def _chip_copies(w, sums_ref, land_ref, send_sems, recv_sems):
    x, y, c = _position()
    return [pltpu.make_async_remote_copy(src_ref=sums_ref.at[2 * px + py], dst_ref=land_ref.at[k],
                                         send_sem=send_sems.at[3 * w + k], recv_sem=recv_sems.at[3 * w + k],
                                         device_id=(px, py, c), device_id_type=MESH)
            for k, (px, py) in enumerate(_other_chips(x, y))]


def _rs_chips_start(sums4, *, name):
    n = len(sums4)
    lands = [lax.empty((3,) + s.shape[1:], s.dtype) for s in sums4]

    def body(*refs):
        sums_refs, land_refs = refs[:n], refs[n:2 * n]
        send_sems, recv_sems = refs[2 * n], refs[2 * n + 1]
        token = refs[-1]
        for w in range(n):
            for cp in _chip_copies(w, sums_refs[w], land_refs[w], send_sems, recv_sems):
                cp.start()
        token[...] = jnp.zeros_like(token)

    outs = pl.pallas_call(
        body, name=name,
        in_specs=[_HBM] * (2 * n),
        out_specs=tuple([_SEM, _SEM] + [_HBM] * (2 * n) + [pl.BlockSpec(memory_space=pltpu.VMEM)]),
        out_shape=tuple([pltpu.SemaphoreType.DMA((3 * n,)), pltpu.SemaphoreType.DMA((3 * n,))]
                        + [pltpu.HBM(a.shape, a.dtype) for a in sums4 + lands] + [jax.ShapeDtypeStruct((8, LANES), F32)]),
        input_output_aliases={i: 2 + i for i in range(2 * n)},
        compiler_params=pltpu.CompilerParams(has_side_effects=_DATAFLOW),
    )(*[_in_hbm(a) for a in sums4 + lands])
    return outs[0], outs[1], list(outs[2:2 + n]), list(outs[2 + n:2 + 2 * n]), outs[-1]


def _rs_chips_wait(send_sems, recv_sems, sums4, lands, after, *, name):
    n = len(sums4)

    def body(*refs):
        sums_refs, land_refs = refs[:n], refs[n:2 * n]
        send_ref, recv_ref = refs[2 * n], refs[2 * n + 1]
        for w in range(n):
            for cp in _chip_copies(w, sums_refs[w], land_refs[w], send_ref, recv_ref):
                cp.wait_send()
                cp.wait_recv()

    outs = pl.pallas_call(
        body, name=name,
        in_specs=[_HBM] * (2 * n) + [_SEM, _SEM, _ANY],
        out_specs=tuple([_HBM] * (2 * n)),
        out_shape=tuple(pltpu.HBM(a.shape, a.dtype) for a in sums4 + lands),
        input_output_aliases={i: i for i in range(2 * n)},
        compiler_params=pltpu.CompilerParams(has_side_effects=_DATAFLOW),
    )(*sums4, *lands, send_sems, recv_sems, after)
    return list(outs[n:])


def _small_all_reduce(packed, *, name):
    R, L = packed.shape

    def body(x_ref, sum_ref, gath_ref, send_sems, recv_sems, local_sem):
        x, y, c = _position()
        me, sibling = (x, y, c), (x, y, 1 - c)
        chips = _other_chips(x, y)

        def rows(px, py, pc):
            return gath_ref.at[pl.ds(_slot(px, py, pc) * R, R), :]

        def copy(k, block, to, src=None):
            return pltpu.make_async_remote_copy(
                src_ref=rows(*block) if src is None else src, dst_ref=rows(*block),
                send_sem=send_sems.at[k], recv_sem=recv_sems.at[k], device_id=to, device_id_type=MESH)

        mine = pltpu.make_async_copy(x_ref, rows(*me), local_sem)
        mine.start()
        first = [copy(0, me, sibling, src=x_ref)]
        first += [copy(1 + j, me, (*chip, c), src=x_ref) for j, chip in enumerate(chips)]
        for cp in first:
            cp.start()
        passed = [copy(4 + j, (*chip, c), sibling) for j, chip in enumerate(chips)]
        for j, chip in enumerate(chips):
            copy(1 + j, (*chip, c), me).wait_recv()
            passed[j].start()
        copy(0, sibling, me).wait_recv()
        for j, chip in enumerate(chips):
            copy(4 + j, (*chip, 1 - c), me).wait_recv()
        for cp in first + passed:
            cp.wait_send()
        mine.wait()
        acc = gath_ref[0:R, :]
        for d in range(1, N_DEV):
            acc = acc + gath_ref[d * R:(d + 1) * R, :]
        sum_ref[...] = acc

    vmem = pl.BlockSpec(memory_space=pltpu.VMEM)
    return pl.pallas_call(
        body, name=name, in_specs=[vmem], out_specs=vmem,
        out_shape=jax.ShapeDtypeStruct((R, L), F32),
        scratch_shapes=[pltpu.VMEM((N_DEV * R, L), F32), pltpu.SemaphoreType.DMA((7,)), pltpu.SemaphoreType.DMA((7,)),
                        pltpu.SemaphoreType.DMA],
        compiler_params=pltpu.CompilerParams(vmem_limit_bytes=VMEM_LIMIT),
    )(packed)


def _adamw_math(w, g, m, v):
    m = ADAM_B1 * m + (1.0 - ADAM_B1) * g
    v = ADAM_B2 * v + (1.0 - ADAM_B2) * (g * g)
    m_hat = m / (1.0 - ADAM_B1 ** ADAM_STEP)
    v_hat = v / (1.0 - ADAM_B2 ** ADAM_STEP)
    delta = -ADAM_LR * (m_hat / (jnp.sqrt(v_hat) + ADAM_EPS) + ADAM_WD * w)
    return delta, m, v


def _adamw_shard(w, m, v, g8, from_sibling, from_chips, pos, *, name):
    R, C = w.shape
    tr = _div(R, 256, 16)

    def body(pos_ref, w_ref, m_ref, v_ref, g_ref, s_ref, r_ref, go_ref, d_ref, mo_ref, vo_ref):
        g = g_ref[...].astype(F32) + s_ref[...].astype(F32)
        for k in range(3):
            g = g + r_ref[k].astype(F32)
        delta, m_, v_ = _adamw_math(w_ref[...], g, m_ref[...], v_ref[...])
        go_ref[...] = g
        d_ref[...] = delta
        mo_ref[...] = m_
        vo_ref[...] = v_

    blk = pl.BlockSpec((tr, C), lambda i, pos_ref: (i, 0))
    grid_spec = pltpu.PrefetchScalarGridSpec(
        num_scalar_prefetch=1, grid=(R // tr,),
        in_specs=[blk, blk, blk,
                  pl.BlockSpec((None, tr, C), lambda i, pos_ref: (pos_ref[0], i, 0)),
                  pl.BlockSpec((None, tr, C), lambda i, pos_ref: (pos_ref[1], i, 0)),
                  pl.BlockSpec((3, tr, C), lambda i, pos_ref: (0, i, 0))],
        out_specs=[blk] * 4)
    out = jax.ShapeDtypeStruct((R, C), F32)
    return pl.pallas_call(
        body, name=name, grid_spec=grid_spec, out_shape=[out] * 4,
        compiler_params=_cparams("parallel"),
    )(pos, w, m, v, g8, from_sibling, from_chips)


def _adamw_small(w, g, m, v, *, name):
    R, L = w.shape

    def body(w_ref, g_ref, m_ref, v_ref, d_ref, mo_ref, vo_ref):
        delta, m_, v_ = _adamw_math(w_ref[...], g_ref[...], m_ref[...], v_ref[...])
        d_ref[...] = delta
        mo_ref[...] = m_
        vo_ref[...] = v_

    vmem = pl.BlockSpec(memory_space=pltpu.VMEM)
    out = jax.ShapeDtypeStruct((R, L), F32)
    return pl.pallas_call(body, name=name, in_specs=[vmem] * 4, out_specs=[vmem] * 3, out_shape=[out] * 3)(w, g, m, v)


_TILE = 8 * LANES


def _pack(pieces):
    rows = []
    for p in pieces:
        flat = p.reshape(-1).astype(F32)
        padded = -(-flat.shape[0] // _TILE) * _TILE
        rows.append(jnp.pad(flat, (0, padded - flat.shape[0])).reshape(-1, LANES))
    return jnp.concatenate(rows, axis=0)


def _unpack(packed, like):
    out, r = [], 0
    for p in like:
        size = int(np.prod(p.shape)) if p.shape else 1
        nrows = -(-size // _TILE) * 8
        out.append(packed[r:r + nrows].reshape(-1)[:size].reshape(p.shape))
        r += nrows
    return out


_COL_SHARDED = ("w_in", "w_a_out", "w_b_out", "w_gate", "w_up")
_BIG = ("w_in", "w_a_out", "w_b_out", "w_o", "w_gate", "w_up", "w_down")
_GATHER_GROUPS = (("w_in",), ("w_a_out", "w_b_out", "w_o"), ("w_gate", "w_up"), ("w_down",))
_SMALL = ("norm_mix", "sgu_v_gain", "sgu_w_s", "sgu_b_s", "attn_sink", "rel_bias", "norm_ffn", "norm_final")
_ORDER = ("w_in", "norm_mix", "sgu_v_gain", "sgu_w_s", "sgu_b_s", "w_a_out", "attn_sink", "rel_bias", "w_b_out", "w_o",
          "norm_ffn", "w_gate", "w_up", "w_down", "norm_final")


def _whole(name, gathered):
    _, r, c = gathered.shape
    return gathered if name in _COL_SHARDED else gathered.reshape(N_DEV * r, c)


def _blocks(name, grad):
    if name in _COL_SHARDED:
        return grad
    r, c = grad.shape
    return grad.reshape(N_DEV, r // N_DEV, c)


def kernel(x, w_in, norm_mix, sgu_v_gain, sgu_w_s, sgu_b_s, w_a_out, attn_sink, rel_bias, w_b_out, w_o, norm_ffn, w_gate, w_up, w_down, norm_final, loss_target, m_w_in, m_norm_mix, m_sgu_v_gain, m_sgu_w_s, m_sgu_b_s, m_w_a_out, m_attn_sink, m_rel_bias, m_w_b_out, m_w_o, m_norm_ffn, m_w_gate, m_w_up, m_w_down, m_norm_final, v_w_in, v_norm_mix, v_sgu_v_gain, v_sgu_w_s, v_sgu_b_s, v_w_a_out, v_attn_sink, v_rel_bias, v_w_b_out, v_w_o, v_norm_ffn, v_w_gate, v_w_up, v_w_down, v_norm_final):
    w = dict(w_in=w_in, norm_mix=norm_mix, sgu_v_gain=sgu_v_gain, sgu_w_s=sgu_w_s, sgu_b_s=sgu_b_s, w_a_out=w_a_out,
             attn_sink=attn_sink, rel_bias=rel_bias, w_b_out=w_b_out, w_o=w_o, norm_ffn=norm_ffn, w_gate=w_gate,
             w_up=w_up, w_down=w_down, norm_final=norm_final)
    m = dict(w_in=m_w_in, norm_mix=m_norm_mix, sgu_v_gain=m_sgu_v_gain, sgu_w_s=m_sgu_w_s, sgu_b_s=m_sgu_b_s,
             w_a_out=m_w_a_out, attn_sink=m_attn_sink, rel_bias=m_rel_bias, w_b_out=m_w_b_out, w_o=m_w_o,
             norm_ffn=m_norm_ffn, w_gate=m_w_gate, w_up=m_w_up, w_down=m_w_down, norm_final=m_norm_final)
    v = dict(w_in=v_w_in, norm_mix=v_norm_mix, sgu_v_gain=v_sgu_v_gain, sgu_w_s=v_sgu_w_s, sgu_b_s=v_sgu_b_s,
             w_a_out=v_w_a_out, attn_sink=v_attn_sink, rel_bias=v_rel_bias, w_b_out=v_w_b_out, w_o=v_w_o,
             norm_ffn=v_norm_ffn, w_gate=v_w_gate, w_up=v_w_up, w_down=v_w_down, norm_final=v_norm_final)
    xc, yc, cc = _position()
    pos = jnp.stack([_slot(xc, yc, cc), 2 * xc + yc, cc]).astype(jnp.int32)

    shards = [w[n][0].astype(BF16) for n in _BIG]
    in_flight = _ag_start(shards, [[_BIG.index(n) for n in grp] for grp in _GATHER_GROUPS], name="ag_start")
    full = {}

    def weight(name, after):
        if name not in full:
            gi = next(i for i, grp in enumerate(_GATHER_GROUPS) if name in grp)
            send_sems, recv_sems, group_shards, lands = in_flight[gi]
            lands = _ag_wait(send_sems, recv_sems, group_shards, lands, after, name="ag_wait_%d" % gi)
            gathered = _ag_forward(group_shards, lands, name="ag_forward_%d" % gi)
            full.update({n: _whole(n, g) for n, g in zip(_GATHER_GROUPS[gi], gathered)})
        return full[name]

    reducing = {}

    def emit(names, grads):
        g8 = [_blocks(n, g) for n, g in zip(names, grads)]
        from_sibling = _rs_to_sibling(g8, name="rs_to_sibling_" + names[0])
        sums4 = [_chip_sums(g, s, pos, name="chip_sums_" + n) for n, g, s in zip(names, g8, from_sibling)]
        send_sems, recv_sems, sums4, lands, token = _rs_chips_start(sums4, name="rs_chips_start_" + names[0])
        reducing[names] = (g8, from_sibling, send_sems, recv_sems, sums4, lands)
        return token

    loss, grad_x, small_grads_local = _local_step(
        x[0], loss_target[0], weight, emit, norm_mix, sgu_v_gain, sgu_w_s[0], sgu_b_s[0], attn_sink, rel_bias, norm_ffn,
        norm_final[None])

    small_like = [w[n] for n in _SMALL]
    small_w = _pack(small_like)
    packed = _pack([small_grads_local[n] for n in _SMALL] + [loss[0, 0]])
    summed = _small_all_reduce(packed, name="small_all_reduce")
    *small_grads, loss_sum = _unpack(summed, small_like + [jax.ShapeDtypeStruct((), F32)])

    out_g, out_d, out_m, out_v = {}, {}, {}, {}
    after = summed
    for names, (g8, from_sibling, send_sems, recv_sems, sums4, lands) in reducing.items():
        from_chips = _rs_chips_wait(send_sems, recv_sems, sums4, lands, after, name="rs_chips_wait_" + names[0])
        for i, n in enumerate(names):
            g, d, m_, v_ = _adamw_shard(w[n][0], m[n][0], v[n][0], g8[i], from_sibling[i], from_chips[i], pos,
                                        name="adamw_" + n)
            out_g[n], out_d[n], out_m[n], out_v[n] = g[None], d[None], m_[None], v_[None]
            after = d
    d_s, m_s, v_s = _adamw_small(small_w, summed[:small_w.shape[0]], _pack([m[n] for n in _SMALL]),
                                 _pack([v[n] for n in _SMALL]), name="adamw_small")
    for n, g, d, m_, v_ in zip(_SMALL, small_grads, _unpack(d_s, small_like), _unpack(m_s, small_like), _unpack(v_s, small_like)):
        out_g[n], out_d[n], out_m[n], out_v[n] = g, d, m_, v_

    return (loss_sum, grad_x[None], *[out_g[n] for n in _ORDER], *[out_d[n] for n in _ORDER],
            *[out_m[n] for n in _ORDER], *[out_v[n] for n in _ORDER])
```

```python
import functools
import math

import numpy as np
import jax
import jax.numpy as jnp
from jax import lax
from jax.experimental import pallas as pl
from jax.experimental.pallas import tpu as pltpu

F32 = jnp.float32
BF16 = jnp.bfloat16

EPS = 1e-6
NEG = -1e30
HEAD_DIM = 128
BLOCK = 128
N_KV_HEADS = 2
KV_WIDTH = N_KV_HEADS * HEAD_DIM
REL_BUCKETS = 32
REL_MAX_DIST = 128

ADAM_LR = 0.001
ADAM_B1 = 0.9
ADAM_B2 = 0.999
ADAM_EPS = 1e-08
ADAM_WD = 0.01
ADAM_STEP = 10

N_DEV = 8
LANES = 128
VMEM_LIMIT = 56 * 1024 * 1024
MESH = pl.DeviceIdType.MESH


def _cparams(*sem):
    return pltpu.CompilerParams(dimension_semantics=sem, vmem_limit_bytes=VMEM_LIMIT)


def _div(n, target, mult=LANES):
    best = None
    for d in range(mult, min(n, target) + 1, mult):
        if n % d == 0:
            best = d
    assert best is not None, (n, target, mult)
    return best


_ANY = pl.BlockSpec(memory_space=pl.ANY)


def _ordered_after(body, n_inputs, in_specs, args, after):
    if after is None:
        return body, in_specs, args

    def wrapped(*refs):
        return body(*refs[:n_inputs], *refs[n_inputs + 1:])

    return wrapped, list(in_specs) + [_ANY], tuple(args) + (after,)


def _bucket_map():
    nb = REL_BUCKETS // 2
    qi = np.arange(BLOCK)[:, None]
    kj = np.arange(3 * BLOCK)[None, :]
    rel = kj - BLOCK - qi
    ret = np.where(rel > 0, nb, 0)
    n = np.abs(rel)
    max_exact = nb // 2
    nf = np.maximum(n, 1).astype(np.float32)
    large = max_exact + (np.log(nf / np.float32(max_exact)) / np.float32(math.log(REL_MAX_DIST / max_exact))
                         * np.float32(nb - max_exact)).astype(np.int32)
    large = np.minimum(large, nb - 1)
    return (ret + np.where(n < max_exact, n, large)).astype(np.int32)


_GELU_C = math.sqrt(2.0 / math.pi)
_GELU_A = 0.044715


def _gelu(x):
    t = jnp.tanh(_GELU_C * (x + _GELU_A * (x * x * x)))
    return 0.5 * x * (1.0 + t)


def _gelu_and_grad(x):
    x2 = x * x
    t = jnp.tanh(_GELU_C * (x + _GELU_A * (x2 * x)))
    g = 0.5 * x * (1.0 + t)
    dg = 0.5 * (1.0 + t) + 0.5 * x * (1.0 - t * t) * (_GELU_C * (1.0 + 3.0 * _GELU_A * x2))
    return g, dg


def _sigmoid(x):
    return 1.0 / (1.0 + jnp.exp(-x))


def _mm(a, b, *, name, ta=False, tb=False, add=None, out_dtype=F32, bm=1024, bn=1024, bk=None):
    if ta:
        K, M = a.shape
    else:
        M, K = a.shape
    N = b.shape[0] if tb else b.shape[1]
    assert (b.shape[1] if tb else b.shape[0]) == K
    bm = _div(M, bm)
    bn = _div(N, bn)
    bk = K if bk is None else _div(K, bk)
    nk = K // bk
    a_spec = pl.BlockSpec((bk, bm), lambda i, j, k: (k, i)) if ta else pl.BlockSpec((bm, bk), lambda i, j, k: (i, k))
    b_spec = pl.BlockSpec((bn, bk), lambda i, j, k: (j, k)) if tb else pl.BlockSpec((bk, bn), lambda i, j, k: (k, j))
    o_spec = pl.BlockSpec((bm, bn), lambda i, j, k: (i, j))
    dims = (((0 if ta else 1,), (1 if tb else 0,)), ((), ()))
    has_add = add is not None

    def body(*refs):
        if has_add:
            a_ref, b_ref, add_ref, o_ref, *scratch = refs
        else:
            a_ref, b_ref, o_ref, *scratch = refs
            add_ref = None
        p = lax.dot_general(a_ref[...].astype(BF16), b_ref[...].astype(BF16), dims, preferred_element_type=F32)
        if nk == 1:
            if has_add:
                p = p + add_ref[...]
            o_ref[...] = p.astype(out_dtype)
        else:
            acc = scratch[0]
            k = pl.program_id(2)

            @pl.when(k == 0)
            def _():
                acc[...] = p

            @pl.when(k > 0)
            def _():
                acc[...] += p

            @pl.when(k == nk - 1)
            def _():
                r = acc[...]
                if has_add:
                    r = r + add_ref[...]
                o_ref[...] = r.astype(out_dtype)

    in_specs = [a_spec, b_spec] + ([o_spec] if has_add else [])
    args = (a, b) + ((add,) if has_add else ())
    return pl.pallas_call(
        body, name=name, grid=(M // bm, N // bn, nk),
        in_specs=in_specs, out_specs=o_spec,
        out_shape=jax.ShapeDtypeStruct((M, N), out_dtype),
        scratch_shapes=[pltpu.VMEM((bm, bn), F32)] if nk > 1 else [],
        compiler_params=_cparams("parallel", "parallel", "arbitrary"),
    )(*args)


def _blocks_per_tile(c):
    nb = 1
    while (nb * c) % LANES or (nb * c < 1024 and nb < N_DEV):
        nb *= 2
    assert nb <= N_DEV and (nb * c) % LANES == 0, c
    return nb


def _mm_w8(a, w8, *, name, bm=1024):
    M, K = a.shape
    _, _, c = w8.shape
    nb = _blocks_per_tile(c)
    bm = _div(M, bm)

    def body(a_ref, w_ref, o_ref):
        a_ = a_ref[...]
        for t in range(nb):
            o_ref[:, t * c:(t + 1) * c] = jnp.dot(a_, w_ref[t], preferred_element_type=F32)

    return pl.pallas_call(
        body, name=name, grid=(M // bm, N_DEV // nb),
        in_specs=[pl.BlockSpec((bm, K), lambda i, j: (i, 0)), pl.BlockSpec((nb, K, c), lambda i, j: (j, 0, 0))],
        out_specs=pl.BlockSpec((bm, nb * c), lambda i, j: (i, j)),
        out_shape=jax.ShapeDtypeStruct((M, N_DEV * c), F32),
        compiler_params=_cparams("parallel", "parallel"),
    )(a, w8)


def _mm_w8t(dy, w8, *, name, add=None, out_dtype=F32, bm=1024, bn=1024, after=None):
    M = dy.shape[0]
    _, K, c = w8.shape
    nb = _blocks_per_tile(c)
    nk = N_DEV // nb
    bm, bn = _div(M, bm), _div(K, bn)
    has_add = add is not None
    dims = (((1,), (1,)), ((), ()))

    def body(*refs):
        if has_add:
            dy_ref, w_ref, add_ref, o_ref, acc = refs
        else:
            dy_ref, w_ref, o_ref, acc = refs
        p = lax.dot_general(dy_ref[:, 0:c], w_ref[0], dims, preferred_element_type=F32)
        for t in range(1, nb):
            p = p + lax.dot_general(dy_ref[:, t * c:(t + 1) * c], w_ref[t], dims, preferred_element_type=F32)
        k = pl.program_id(2)

        @pl.when(k == 0)
        def _():
            acc[...] = p

        @pl.when(k > 0)
        def _():
            acc[...] += p

        @pl.when(k == nk - 1)
        def _():
            r = acc[...]
            if has_add:
                r = r + add_ref[...]
            o_ref[...] = r.astype(out_dtype)

    o_spec = pl.BlockSpec((bm, bn), lambda i, j, k: (i, j))
    in_specs = [pl.BlockSpec((bm, nb * c), lambda i, j, k: (i, k)), pl.BlockSpec((nb, bn, c), lambda i, j, k: (k, j, 0))]
    in_specs += [o_spec] if has_add else []
    args = (dy, w8) + ((add,) if has_add else ())
    body, in_specs, args = _ordered_after(body, len(args), in_specs, args, after)
    return pl.pallas_call(
        body, name=name, grid=(M // bm, K // bn, nk),
        in_specs=in_specs, out_specs=o_spec,
        out_shape=jax.ShapeDtypeStruct((M, K), out_dtype),
        scratch_shapes=[pltpu.VMEM((bm, bn), F32)],
        compiler_params=_cparams("parallel", "parallel", "arbitrary"),
    )(*args)


def _mm_gw8(x, dy, c, *, name, bk=1024):
    T, K = x.shape
    nb = _blocks_per_tile(c)
    bk = _div(K, bk)
    dims = (((0,), (0,)), ((), ()))

    def body(x_ref, dy_ref, o_ref):
        x_ = x_ref[...]
        for t in range(nb):
            o_ref[t] = lax.dot_general(x_, dy_ref[:, t * c:(t + 1) * c], dims, preferred_element_type=F32).astype(BF16)

    return pl.pallas_call(
        body, name=name, grid=(K // bk, N_DEV // nb),
        in_specs=[pl.BlockSpec((T, bk), lambda i, j: (0, i)), pl.BlockSpec((T, nb * c), lambda i, j: (0, j))],
        out_specs=pl.BlockSpec((nb, bk, c), lambda i, j: (j, i, 0)),
        out_shape=jax.ShapeDtypeStruct((N_DEV, K, c), BF16),
        compiler_params=_cparams("parallel", "parallel"),
    )(x, dy)


def _rms_fwd(x, g, *, name):
    T, D = x.shape
    tm = _div(T, 256, 8)

    def body(x_ref, g_ref, h_ref):
        xf = x_ref[...]
        r = lax.rsqrt(jnp.mean(xf * xf, axis=-1, keepdims=True) + EPS)
        h_ref[...] = ((xf * r) * g_ref[...]).astype(BF16)

    return pl.pallas_call(
        body, name=name, grid=(T // tm,),
        in_specs=[pl.BlockSpec((tm, D), lambda i: (i, 0)), pl.BlockSpec((1, D), lambda i: (0, 0))],
        out_specs=pl.BlockSpec((tm, D), lambda i: (i, 0)),
        out_shape=jax.ShapeDtypeStruct((T, D), BF16),
        compiler_params=_cparams("parallel"),
    )(x, g)


def _rms_bwd(x, g, dh, dres, *, name, want_bf16, after=None):
    T, D = x.shape
    tm = _div(T, 256, 8)

    def body(x_ref, g_ref, dh_ref, dres_ref, dx_ref, *rest):
        if want_bf16:
            dxb_ref, dg_ref = rest
        else:
            (dg_ref,) = rest
        xf = x_ref[...]
        r = lax.rsqrt(jnp.mean(xf * xf, axis=-1, keepdims=True) + EPS)
        xhat = xf * r
        dh_ = dh_ref[...]
        dy = dh_ * g_ref[...]
        dx = dres_ref[...] + r * (dy - xhat * jnp.mean(dy * xhat, axis=-1, keepdims=True))
        dx_ref[...] = dx
        if want_bf16:
            dxb_ref[...] = dx.astype(BF16)
        part = jnp.sum(dh_ * xhat, axis=0, keepdims=True)

        @pl.when(pl.program_id(0) == 0)
        def _():
            dg_ref[...] = part

        @pl.when(pl.program_id(0) > 0)
        def _():
            dg_ref[...] += part

    row = pl.BlockSpec((tm, D), lambda i: (i, 0))
    vec = pl.BlockSpec((1, D), lambda i: (0, 0))
    out_specs = [row] + ([row] if want_bf16 else []) + [vec]
    out_shape = ([jax.ShapeDtypeStruct((T, D), F32)] + ([jax.ShapeDtypeStruct((T, D), BF16)] if want_bf16 else [])
                 + [jax.ShapeDtypeStruct((1, D), F32)])
    body, in_specs, args = _ordered_after(body, 4, [row, vec, row, row], (x, g, dh, dres), after)
    return pl.pallas_call(
        body, name=name, grid=(T // tm,),
        in_specs=in_specs, out_specs=out_specs, out_shape=out_shape,
        compiler_params=_cparams("arbitrary"),
    )(*args)


def _loss_head(x, g, target, *, name):
    T, D = x.shape
    tm = _div(T, 256, 8)

    def body(x_ref, g_ref, t_ref, loss_ref, dx_ref, dxb_ref, dg_ref):
        xf = x_ref[...]
        r = lax.rsqrt(jnp.mean(xf * xf, axis=-1, keepdims=True) + EPS)
        xhat = xf * r
        gain = g_ref[...]
        err = xhat * gain - t_ref[...]
        lpart = 0.5 * jnp.sum(jnp.mean(err * err, axis=-1, keepdims=True), axis=0, keepdims=True)
        dh_ = err * (1.0 / D)
        dy = dh_ * gain
        dx = r * (dy - xhat * jnp.mean(dy * xhat, axis=-1, keepdims=True))
        dx_ref[...] = dx
        dxb_ref[...] = dx.astype(BF16)
        part = jnp.sum(dh_ * xhat, axis=0, keepdims=True)

        @pl.when(pl.program_id(0) == 0)
        def _():
            dg_ref[...] = part
            loss_ref[...] = jnp.broadcast_to(lpart, loss_ref.shape)

        @pl.when(pl.program_id(0) > 0)
        def _():
            dg_ref[...] += part
            loss_ref[...] += jnp.broadcast_to(lpart, loss_ref.shape)

    row = pl.BlockSpec((tm, D), lambda i: (i, 0))
    vec = pl.BlockSpec((1, D), lambda i: (0, 0))
    return pl.pallas_call(
        body, name=name, grid=(T // tm,),
        in_specs=[row, vec, row],
        out_specs=[pl.BlockSpec((8, LANES), lambda i: (0, 0)), row, row, vec],
        out_shape=[jax.ShapeDtypeStruct((8, LANES), F32), jax.ShapeDtypeStruct((T, D), F32),
                   jax.ShapeDtypeStruct((T, D), BF16), jax.ShapeDtypeStruct((1, D), F32)],
        compiler_params=_cparams("arbitrary"),
    )(x, g, target)


def _gate_cols(D):
    off_a = 3 * D // 2 + 2 * KV_WIDTH
    off_b = off_a + D
    cw = math.gcd(math.gcd(off_a, off_b), math.gcd(D, 512))
    return cw, off_a // cw, off_b // cw


def _merge_fwd(z, ya, yb, *, name):
    T, D = ya.shape
    cw, ba, bb = _gate_cols(D)
    tm = _div(T, 512, 8)

    def body(ga_ref, gb_ref, ya_ref, yb_ref, m_ref):
        m_ref[...] = (_sigmoid(ga_ref[...]) * ya_ref[...] + _sigmoid(gb_ref[...]) * yb_ref[...]).astype(BF16)

    blk = pl.BlockSpec((tm, cw), lambda i, j: (i, j))
    return pl.pallas_call(
        body, name=name, grid=(T // tm, D // cw),
        in_specs=[pl.BlockSpec((tm, cw), lambda i, j: (i, ba + j)), pl.BlockSpec((tm, cw), lambda i, j: (i, bb + j)), blk, blk],
        out_specs=blk, out_shape=jax.ShapeDtypeStruct((T, D), BF16),
        compiler_params=_cparams("parallel", "parallel"),
    )(z, z, ya, yb)


def _merge_bwd(z, ya, yb, dm, *, name, after=None):
    T, D = ya.shape
    cw, ba, bb = _gate_cols(D)
    tm = _div(T, 512, 8)

    def body(ga_ref, gb_ref, ya_ref, yb_ref, dm_ref, dya_ref, dyb_ref, dga_ref, dgb_ref):
        sa = _sigmoid(ga_ref[...])
        sb = _sigmoid(gb_ref[...])
        dm_ = dm_ref[...]
        dya_ref[...] = (dm_ * sa).astype(BF16)
        dyb_ref[...] = (dm_ * sb).astype(BF16)
        dga_ref[...] = (dm_ * ya_ref[...] * (sa * (1.0 - sa))).astype(BF16)
        dgb_ref[...] = (dm_ * yb_ref[...] * (sb * (1.0 - sb))).astype(BF16)

    blk = pl.BlockSpec((tm, cw), lambda i, j: (i, j))
    out = jax.ShapeDtypeStruct((T, D), BF16)
    in_specs = [pl.BlockSpec((tm, cw), lambda i, j: (i, ba + j)), pl.BlockSpec((tm, cw), lambda i, j: (i, bb + j)), blk, blk, blk]
    body, in_specs, args = _ordered_after(body, 5, in_specs, (z, z, ya, yb, dm), after)
    return pl.pallas_call(
        body, name=name, grid=(T // tm, D // cw),
        in_specs=in_specs, out_specs=[blk] * 4, out_shape=[out] * 4,
        compiler_params=_cparams("parallel", "parallel"),
    )(*args)


def _swiglu_fwd(gate, up, *, name):
    T, F = gate.shape
    tm, cw = _div(T, 512, 8), _div(F, 512)

    def body(g_ref, u_ref, act_ref):
        g = g_ref[...]
        act_ref[...] = (g * _sigmoid(g) * u_ref[...]).astype(BF16)

    blk = pl.BlockSpec((tm, cw), lambda i, j: (i, j))
    return pl.pallas_call(
        body, name=name, grid=(T // tm, F // cw), in_specs=[blk, blk], out_specs=blk,
        out_shape=jax.ShapeDtypeStruct((T, F), BF16), compiler_params=_cparams("parallel", "parallel"),
    )(gate, up)


def _swiglu_bwd(gate, up, dact, *, name, after=None):
    T, F = gate.shape
    tm, cw = _div(T, 512, 8), _div(F, 512)

    def body(g_ref, u_ref, d_ref, dg_ref, du_ref):
        g = g_ref[...]
        s = _sigmoid(g)
        d = d_ref[...]
        silu = g * s
        dg_ref[...] = (d * u_ref[...] * (s + silu * (1.0 - s))).astype(BF16)
        du_ref[...] = (d * silu).astype(BF16)

    blk = pl.BlockSpec((tm, cw), lambda i, j: (i, j))
    out = jax.ShapeDtypeStruct((T, F), BF16)
    body, in_specs, args = _ordered_after(body, 3, [blk, blk, blk], (gate, up, dact), after)
    return pl.pallas_call(
        body, name=name, grid=(T // tm, F // cw), in_specs=in_specs, out_specs=[blk, blk],
        out_shape=[out, out], compiler_params=_cparams("parallel", "parallel"),
    )(*args)


def _sgu_fwd(z, gain, ws_b, bs_t, *, name):
    T = z.shape[0]
    SW = gain.shape[1]
    G = SW // BLOCK

    def body(zu_ref, zv_ref, gain_ref, ws_ref, bs_ref, a_ref):
        u = _gelu(zu_ref[...])
        vg = _gelu(zv_ref[...])
        r = lax.rsqrt(jnp.mean(vg * vg, axis=-1, keepdims=True) + EPS)
        vn = ((vg * r) * gain_ref[...]).astype(BF16)
        for g in range(G):
            sl = slice(g * BLOCK, (g + 1) * BLOCK)
            mixed = jnp.dot(ws_ref[g], vn[:, sl], preferred_element_type=F32) + bs_ref[:, g:g + 1]
            a_ref[:, sl] = (u[:, sl] * mixed).astype(BF16)

    return pl.pallas_call(
        body, name=name, grid=(T // BLOCK,),
        in_specs=[pl.BlockSpec((BLOCK, SW), lambda c: (c, 0)), pl.BlockSpec((BLOCK, SW), lambda c: (c, 1)),
                  pl.BlockSpec((1, SW), lambda c: (0, 0)), pl.BlockSpec((G, BLOCK, BLOCK), lambda c: (0, 0, 0)),
                  pl.BlockSpec((BLOCK, G), lambda c: (0, 0))],
        out_specs=pl.BlockSpec((BLOCK, SW), lambda c: (c, 0)),
        out_shape=jax.ShapeDtypeStruct((T, SW), BF16),
        compiler_params=_cparams("parallel"),
    )(z, z, gain, ws_b, bs_t)


def _sgu_bwd(z, gain, ws_b, bs_t, da, *, name, after=None):
    T = z.shape[0]
    SW = gain.shape[1]
    G = SW // BLOCK

    def body(zu_ref, zv_ref, gain_ref, ws_ref, bs_ref, da_ref, dzu_ref, dzv_ref, dws_ref, dbs_ref, dgain_ref, dvn_ref):
        first = pl.program_id(0) == 0

        @pl.when(first)
        def _():
            dws_ref[...] = jnp.zeros_like(dws_ref)
            dbs_ref[...] = jnp.zeros_like(dbs_ref)
            dgain_ref[...] = jnp.zeros_like(dgain_ref)

        u, du = _gelu_and_grad(zu_ref[...])
        vg, dvg = _gelu_and_grad(zv_ref[...])
        r = lax.rsqrt(jnp.mean(vg * vg, axis=-1, keepdims=True) + EPS)
        xhat = vg * r
        gain_ = gain_ref[...]
        vn = (xhat * gain_).astype(BF16)
        da_ = da_ref[...]
        for g in range(G):
            sl = slice(g * BLOCK, (g + 1) * BLOCK)
            w = ws_ref[g]
            mixed = jnp.dot(w, vn[:, sl], preferred_element_type=F32) + bs_ref[:, g:g + 1]
            dmix = da_[:, sl] * u[:, sl]
            dzu_ref[:, sl] = (da_[:, sl] * mixed * du[:, sl]).astype(BF16)
            dmb = dmix.astype(BF16)
            dws_ref[g] += lax.dot_general(dmb, vn[:, sl], (((1,), (1,)), ((), ())), preferred_element_type=F32)
            dbs_ref[:, g:g + 1] += jnp.sum(dmix, axis=-1, keepdims=True)
            dvn_ref[:, sl] = lax.dot_general(w, dmb, (((0,), (0,)), ((), ())), preferred_element_type=F32)
        dvn = dvn_ref[...]
        dgain_ref[...] += jnp.sum(dvn * xhat, axis=0, keepdims=True)
        dy = dvn * gain_
        dv_ = r * (dy - xhat * jnp.mean(dy * xhat, axis=-1, keepdims=True))
        dzv_ref[...] = (dv_ * dvg).astype(BF16)

    row = pl.BlockSpec((BLOCK, SW), lambda c: (c, 0))
    in_specs = [row, pl.BlockSpec((BLOCK, SW), lambda c: (c, 1)),
                pl.BlockSpec((1, SW), lambda c: (0, 0)), pl.BlockSpec((G, BLOCK, BLOCK), lambda c: (0, 0, 0)),
                pl.BlockSpec((BLOCK, G), lambda c: (0, 0)), row]
    body, in_specs, args = _ordered_after(body, 6, in_specs, (z, z, gain, ws_b, bs_t, da), after)
    return pl.pallas_call(
        body, name=name, grid=(T // BLOCK,),
        in_specs=in_specs,
        out_specs=[row, row, pl.BlockSpec((G, BLOCK, BLOCK), lambda c: (0, 0, 0)),
                   pl.BlockSpec((BLOCK, G), lambda c: (0, 0)), pl.BlockSpec((1, SW), lambda c: (0, 0))],
        out_shape=[jax.ShapeDtypeStruct((T, SW), BF16), jax.ShapeDtypeStruct((T, SW), BF16),
                   jax.ShapeDtypeStruct((G, BLOCK, BLOCK), F32), jax.ShapeDtypeStruct((BLOCK, G), F32),
                   jax.ShapeDtypeStruct((1, SW), F32)],
        scratch_shapes=[pltpu.VMEM((BLOCK, SW), F32)],
        compiler_params=_cparams("arbitrary"),
    )(*args)


def _bias_table(rel_bias, bmap, *, name):
    H = rel_bias.shape[1]

    def body(rb_ref, bmap_ref, o_ref):
        bm_ = bmap_ref[...]
        for h in range(H):
            acc = jnp.zeros(bm_.shape, F32)
            for b in range(REL_BUCKETS):
                acc = jnp.where(bm_ == b, rb_ref[b, h], acc)
            o_ref[h] = acc

    return pl.pallas_call(
        body, name=name,
        in_specs=[pl.BlockSpec(memory_space=pltpu.SMEM), pl.BlockSpec(memory_space=pltpu.VMEM)],
        out_specs=pl.BlockSpec(memory_space=pltpu.VMEM),
        out_shape=jax.ShapeDtypeStruct((H, BLOCK, 3 * BLOCK), F32),
    )(rel_bias, bmap)


def _attn_probs(q_ref, kb, bias_ref, sink_ref, valid, h, group):
    kv = h // group
    qh = q_ref[:, h * HEAD_DIM:(h + 1) * HEAD_DIM].astype(BF16)
    s = lax.dot_general(qh, kb[:, kv * HEAD_DIM:(kv + 1) * HEAD_DIM], (((1,), (1,)), ((), ())),
                        preferred_element_type=F32)
    s = s * (HEAD_DIM ** -0.5) + bias_ref[h]
    s = jnp.where(valid, s, NEG)
    sink = sink_ref[0:1, h:h + 1]
    m = jnp.maximum(jnp.max(s, axis=-1, keepdims=True), sink)
    e = jnp.exp(s - m)
    es = jnp.exp(sink - m)
    inv = 1.0 / (jnp.sum(e, axis=-1, keepdims=True) + es)
    return e * inv, es * inv, qh


def _band_valid(n, T):
    row = lax.broadcasted_iota(jnp.int32, (BLOCK, 3 * BLOCK), 0)
    col = lax.broadcasted_iota(jnp.int32, (BLOCK, 3 * BLOCK), 1)
    rel = col - BLOCK - row
    key_pos = n * BLOCK + col - BLOCK
    return (jnp.abs(rel) <= BLOCK) & (key_pos >= 0) & (key_pos < T)


def _attn_fwd(z, kpad, vpad, bias, sink, *, name):
    T = z.shape[0]
    H = bias.shape[0]
    AW = H * HEAD_DIM
    group = H // N_KV_HEADS

    def body(q_ref, k_ref, v_ref, bias_ref, sink_ref, o_ref):
        n = pl.program_id(0)
        start = pl.multiple_of(n * BLOCK, BLOCK)
        kb = k_ref[pl.ds(start, 3 * BLOCK), :]
        vb = v_ref[pl.ds(start, 3 * BLOCK), :]
        valid = _band_valid(n, T)
        for h in range(H):
            kv = h // group
            p, _, _ = _attn_probs(q_ref, kb, bias_ref, sink_ref, valid, h, group)
            o = jnp.dot(p.astype(BF16), vb[:, kv * HEAD_DIM:(kv + 1) * HEAD_DIM], preferred_element_type=F32)
            o_ref[:, h * HEAD_DIM:(h + 1) * HEAD_DIM] = o.astype(BF16)

    full_kv = pl.BlockSpec((T + 2 * BLOCK, KV_WIDTH), lambda n: (0, 0))
    return pl.pallas_call(
        body, name=name, grid=(T // BLOCK,),
        in_specs=[pl.BlockSpec((BLOCK, AW), lambda n: (n, 2)), full_kv, full_kv,
                  pl.BlockSpec((H, BLOCK, 3 * BLOCK), lambda n: (0, 0, 0)), pl.BlockSpec((1, H), lambda n: (0, 0))],
        out_specs=pl.BlockSpec((BLOCK, AW), lambda n: (n, 0)),
        out_shape=jax.ShapeDtypeStruct((T, AW), BF16),
        compiler_params=_cparams("parallel"),
    )(z, kpad, vpad, bias, sink)


def _attn_bwd(z, kpad, vpad, bias, sink, do, *, name):
    T = z.shape[0]
    H = bias.shape[0]
    AW = H * HEAD_DIM
    group = H // N_KV_HEADS
    scale = HEAD_DIM ** -0.5

    def body(q_ref, k_ref, v_ref, bias_ref, sink_ref, do_ref, dq_ref, dk_ref, dv_ref, dbias_ref, dsink_ref):
        n = pl.program_id(0)

        @pl.when(n == 0)
        def _():
            dk_ref[...] = jnp.zeros_like(dk_ref)
            dv_ref[...] = jnp.zeros_like(dv_ref)
            dbias_ref[...] = jnp.zeros_like(dbias_ref)
            dsink_ref[...] = jnp.zeros_like(dsink_ref)

        start = pl.multiple_of(n * BLOCK, BLOCK)
        kb = k_ref[pl.ds(start, 3 * BLOCK), :]
        vb = v_ref[pl.ds(start, 3 * BLOCK), :]
        valid = _band_valid(n, T)
        for kv in range(N_KV_HEADS):
            ksl = slice(kv * HEAD_DIM, (kv + 1) * HEAD_DIM)
            dk_acc = jnp.zeros((3 * BLOCK, HEAD_DIM), F32)
            dv_acc = jnp.zeros((3 * BLOCK, HEAD_DIM), F32)
            for gi in range(group):
                h = kv * group + gi
                hsl = slice(h * HEAD_DIM, (h + 1) * HEAD_DIM)
                p, p_sink, qh = _attn_probs(q_ref, kb, bias_ref, sink_ref, valid, h, group)
                doh = do_ref[:, hsl]
                dp = lax.dot_general(doh, vb[:, ksl], (((1,), (1,)), ((), ())), preferred_element_type=F32)
                delta = jnp.sum(p * dp, axis=-1, keepdims=True)
                ds = p * (dp - delta)
                dbias_ref[h] += ds
                dsink_ref[:, h:h + 1] += -(p_sink * delta)
                dsb = ds.astype(BF16)
                dq = jnp.dot(dsb, kb[:, ksl], preferred_element_type=F32) * scale
                dq_ref[:, hsl] = dq.astype(BF16)
                dk_acc = dk_acc + lax.dot_general(dsb, qh, (((0,), (0,)), ((), ())), preferred_element_type=F32)
                dv_acc = dv_acc + lax.dot_general(p.astype(BF16), doh, (((0,), (0,)), ((), ())),
                                                  preferred_element_type=F32)
            dk_ref[pl.ds(start, 3 * BLOCK), ksl] += dk_acc * scale
            dv_ref[pl.ds(start, 3 * BLOCK), ksl] += dv_acc

    full_kv = pl.BlockSpec((T + 2 * BLOCK, KV_WIDTH), lambda n: (0, 0))
    bias_spec = pl.BlockSpec((H, BLOCK, 3 * BLOCK), lambda n: (0, 0, 0))
    row = pl.BlockSpec((BLOCK, AW), lambda n: (n, 0))
    return pl.pallas_call(
        body, name=name, grid=(T // BLOCK,),
        in_specs=[pl.BlockSpec((BLOCK, AW), lambda n: (n, 2)), full_kv, full_kv, bias_spec,
                  pl.BlockSpec((1, H), lambda n: (0, 0)), row],
        out_specs=[row, full_kv, full_kv, bias_spec, pl.BlockSpec((BLOCK, H), lambda n: (0, 0))],
        out_shape=[jax.ShapeDtypeStruct((T, AW), BF16),
                   jax.ShapeDtypeStruct((T + 2 * BLOCK, KV_WIDTH), F32), jax.ShapeDtypeStruct((T + 2 * BLOCK, KV_WIDTH), F32),
                   jax.ShapeDtypeStruct((H, BLOCK, 3 * BLOCK), F32), jax.ShapeDtypeStruct((BLOCK, H), F32)],
        compiler_params=_cparams("arbitrary"),
    )(z, kpad, vpad, bias, sink, do)


def _attn_small_grads(dbias, dsink_rows, bmap, *, name):
    H = dbias.shape[0]

    def body(dbias_ref, dsink_ref, bmap_ref, drel_ref, ds_ref):
        bm_ = bmap_ref[...]
        for h in range(H):
            d = dbias_ref[h]
            for b in range(REL_BUCKETS):
                drel_ref[b, h] = jnp.sum(jnp.where(bm_ == b, d, 0.0))
            ds_ref[0, h] = jnp.sum(dsink_ref[:, h:h + 1])

    vmem = pl.BlockSpec(memory_space=pltpu.VMEM)
    smem = pl.BlockSpec(memory_space=pltpu.SMEM)
    return pl.pallas_call(
        body, name=name, in_specs=[vmem, vmem, vmem], out_specs=[smem, smem],
        out_shape=[jax.ShapeDtypeStruct((REL_BUCKETS, H), F32), jax.ShapeDtypeStruct((1, H), F32)],
    )(dbias, dsink_rows, bmap)


def _local_step(x, target, weight, emit, norm_mix, v_gain, w_s, b_s, sink, rel_bias, norm_ffn, norm_final):
    T, D = x.shape
    SW = D // 2
    off_k = D + SW
    ws_b = w_s.astype(BF16)
    bs_t = b_s.T
    bmap = jnp.asarray(_bucket_map())

    h = _rms_fwd(x, norm_mix, name="rms_mix")
    w_in = weight("w_in", h)
    z = _mm_w8(h, w_in, name="mm_z")
    a = _sgu_fwd(z, v_gain, ws_b, bs_t, name="sgu_fwd")
    w_a = weight("w_a_out", a)
    ya = _mm_w8(a, w_a, name="mm_ya", bm=2048)
    pad = ((BLOCK, BLOCK), (0, 0))
    kpad = jnp.pad(z[:, off_k:off_k + KV_WIDTH].astype(BF16), pad)
    vpad = jnp.pad(z[:, off_k + KV_WIDTH:off_k + 2 * KV_WIDTH].astype(BF16), pad)
    bias = _bias_table(rel_bias, bmap, name="bias_table")
    o = _attn_fwd(z, kpad, vpad, bias, sink, name="attn_fwd")
    w_b = weight("w_b_out", o)
    yb = _mm_w8(o, w_b, name="mm_yb", bm=2048)
    m = _merge_fwd(z, ya, yb, name="merge_fwd")
    w_o = weight("w_o", m)
    x1 = _mm(m, w_o, name="mm_x1", add=x, bm=2048, bn=512)
    h2 = _rms_fwd(x1, norm_ffn, name="rms_ffn")
    w_gate = weight("w_gate", h2)
    w_up = weight("w_up", h2)
    gate = _mm_w8(h2, w_gate, name="mm_gate")
    up = _mm_w8(h2, w_up, name="mm_up")
    act = _swiglu_fwd(gate, up, name="swiglu_fwd")
    w_down = weight("w_down", act)
    x2 = _mm(act, w_down, name="mm_x2", add=x1, bm=1024, bn=1024, bk=2816)
    loss, dx2, dx2b, g_norm_final = _loss_head(x2, norm_final, target, name="loss_head")

    g_w_down = _mm(act, dx2b, ta=True, out_dtype=BF16, name="mm_gwdown", bm=512, bn=2048)
    tok = emit(("w_down",), (g_w_down,))
    dact = _mm(dx2b, w_down, tb=True, name="mm_dact", bm=2048, bn=512)
    dgate, dup = _swiglu_bwd(gate, up, dact, name="swiglu_bwd", after=tok)
    g_w_gate = _mm_gw8(h2, dgate, w_gate.shape[2], name="mm_gwgate")
    g_w_up = _mm_gw8(h2, dup, w_up.shape[2], name="mm_gwup")
    tok = emit(("w_gate", "w_up"), (g_w_gate, g_w_up))
    dh2 = _mm_w8t(dgate, w_gate, name="mm_dh2a", after=tok)
    dh2 = _mm_w8t(dup, w_up, add=dh2, name="mm_dh2b")
    dx1, dx1b, g_norm_ffn = _rms_bwd(x1, norm_ffn, dh2, dx2, name="rms_ffn_bwd", want_bf16=True)

    g_w_o = _mm(m, dx1b, ta=True, out_dtype=BF16, name="mm_gwo", bm=2048, bn=512)
    tok = emit(("w_o",), (g_w_o,))
    dm = _mm(dx1b, w_o, tb=True, name="mm_dm", bm=2048, bn=512)
    dya, dyb, dga, dgb = _merge_bwd(z, ya, yb, dm, name="merge_bwd", after=tok)
    g_w_a = _mm_gw8(a, dya, w_a.shape[2], name="mm_gwa")
    g_w_b = _mm_gw8(o, dyb, w_b.shape[2], name="mm_gwb")
    tok = emit(("w_a_out", "w_b_out"), (g_w_a, g_w_b))
    da = _mm_w8t(dya, w_a, name="mm_da", bm=2048, bn=512, after=tok)
    do = _mm_w8t(dyb, w_b, out_dtype=BF16, name="mm_do", bm=2048, bn=512)
    dzu, dzv, g_w_s, g_b_s_t, g_v_gain = _sgu_bwd(z, v_gain, ws_b, bs_t, da, name="sgu_bwd")
    dq, dkp, dvp, dbias, dsink_rows = _attn_bwd(z, kpad, vpad, bias, sink, do, name="attn_bwd")
    g_rel_bias, g_sink = _attn_small_grads(dbias, dsink_rows, bmap, name="attn_small_grads")
    dz = jnp.concatenate([dzu, dzv, dq, dkp[BLOCK:BLOCK + T].astype(BF16), dvp[BLOCK:BLOCK + T].astype(BF16), dga, dgb], axis=1)
    g_w_in = _mm_gw8(h, dz, w_in.shape[2], name="mm_gwin")
    tok = emit(("w_in",), (g_w_in,))
    dh = _mm_w8t(dz, w_in, name="mm_dh", after=tok)
    grad_x, g_norm_mix = _rms_bwd(x, norm_mix, dh, dx1, name="rms_mix_bwd", want_bf16=False)

    small = dict(norm_mix=g_norm_mix, sgu_v_gain=g_v_gain, sgu_w_s=g_w_s, sgu_b_s=g_b_s_t.T, attn_sink=g_sink,
                 rel_bias=g_rel_bias, norm_ffn=g_norm_ffn, norm_final=g_norm_final)
    return loss, grad_x, small


def _position():
    return lax.axis_index("x"), lax.axis_index("y"), lax.axis_index("c")


def _other_chips(x, y):
    return [(1 - x, y), (x, 1 - y), (1 - x, 1 - y)]


def _slot(px, py, pc):
    return 4 * px + 2 * py + pc


_HBM = pl.BlockSpec(memory_space=pltpu.HBM)
_SEM = pl.BlockSpec(memory_space=pltpu.SEMAPHORE)
_DATAFLOW = pltpu.SideEffectType.DATAFLOW_SIDE_EFFECTING


def _in_hbm(a):
    return pltpu.with_memory_space_constraint(a, pltpu.HBM)


def _own_slot(shard, pos, *, name):
    R, C = shard.shape
    tr = _div(R, 256, 16)

    def body(pos_ref, w_ref, o_ref):
        o_ref[...] = w_ref[...].astype(BF16)

    grid_spec = pltpu.PrefetchScalarGridSpec(
        num_scalar_prefetch=1, grid=(R // tr,),
        in_specs=[pl.BlockSpec((tr, C), lambda i, pos_ref: (i, 0))],
        out_specs=pl.BlockSpec((None, tr, C), lambda i, pos_ref: (pos_ref[0], i, 0)))
    return pl.pallas_call(
        body, name=name, grid_spec=grid_spec,
        out_shape=jax.ShapeDtypeStruct((N_DEV, R, C), BF16),
        compiler_params=_cparams("parallel"),
    )(pos, shard)


def _ag_copies(w, land_ref, send_sems, recv_sems):
    x, y, c = _position()
    mine = land_ref.at[_slot(x, y, c)]
    targets = [(px, py, c) for px, py in _other_chips(x, y)] + [(x, y, 1 - c)]
    return [pltpu.make_async_remote_copy(src_ref=mine, dst_ref=mine, send_sem=send_sems.at[4 * w + k],
                                         recv_sem=recv_sems.at[4 * w + k], device_id=to, device_id_type=MESH)
            for k, to in enumerate(targets)]


def _ag_start(buffers, groups, *, name):
    lands = [buffers[i] for g in groups for i in g]
    n, ng = len(lands), len(groups)
    sizes = [len(g) for g in groups]

    def body(*refs):
        land_refs = refs[:n]
        sems = refs[n:n + 2 * ng]
        i = 0
        for g in range(ng):
            for w in range(sizes[g]):
                for cp in _ag_copies(w, land_refs[i], sems[2 * g], sems[2 * g + 1]):
                    cp.start()
                i += 1

    sem_shapes = [pltpu.SemaphoreType.DMA((4 * k,)) for k in sizes for _ in range(2)]
    outs = pl.pallas_call(
        body, name=name,
        in_specs=[_HBM] * n,
        out_specs=tuple([_SEM] * (2 * ng) + [_HBM] * n),
        out_shape=tuple(sem_shapes + [pltpu.HBM(a.shape, a.dtype) for a in lands]),
        input_output_aliases={i: 2 * ng + i for i in range(n)},
        compiler_params=pltpu.CompilerParams(has_side_effects=_DATAFLOW),
    )(*[_in_hbm(a) for a in lands])
    sems, thru = outs[:2 * ng], outs[2 * ng:]
    result, i = [], 0
    for g in range(ng):
        k = sizes[g]
        result.append((sems[2 * g], sems[2 * g + 1], list(thru[i:i + k])))
        i += k
    return result


def _ag_wait(send_sems, recv_sems, lands, after, *, name):
    n = len(lands)

    def body(*refs):
        land_refs = refs[:n]
        send_ref, recv_ref = refs[n], refs[n + 1]
        for w in range(n):
            for cp in _ag_copies(w, land_refs[w], send_ref, recv_ref):
                cp.wait_send()
                cp.wait_recv()

    outs = pl.pallas_call(
        body, name=name,
        in_specs=[_HBM] * n + [_SEM, _SEM, _ANY],
        out_specs=tuple([_HBM] * n),
        out_shape=tuple(pltpu.HBM(a.shape, a.dtype) for a in lands),
        input_output_aliases={i: i for i in range(n)},
        compiler_params=pltpu.CompilerParams(has_side_effects=_DATAFLOW),
    )(*lands, send_sems, recv_sems, after)
    return list(outs)


def _ag_forward(lands, *, name):
    n = len(lands)

    def body(*refs):
        in_refs, out_refs = refs[:n], refs[n:2 * n]
        send_sems, recv_sems = refs[2 * n:]
        x, y, c = _position()
        copies = []
        for w in range(n):
            for k, (px, py) in enumerate(_other_chips(x, y)):
                cp = pltpu.make_async_remote_copy(
                    src_ref=in_refs[w].at[_slot(px, py, c)], dst_ref=out_refs[w].at[_slot(px, py, c)],
                    send_sem=send_sems.at[3 * w + k], recv_sem=recv_sems.at[3 * w + k],
                    device_id=(x, y, 1 - c), device_id_type=MESH)
                cp.start()
                copies.append(cp)
        for cp in copies:
            cp.wait()

    return pl.pallas_call(
        body, name=name,
        in_specs=[_ANY] * n, out_specs=[_ANY] * n,
        out_shape=[jax.ShapeDtypeStruct(a.shape, a.dtype) for a in lands],
        input_output_aliases={i: i for i in range(n)},
        scratch_shapes=[pltpu.SemaphoreType.DMA((3 * n,)), pltpu.SemaphoreType.DMA((3 * n,))],
    )(*lands)


def _rs_to_sibling(grads8, *, name):
    n = len(grads8)

    def body(*refs):
        ins, outs = refs[:n], refs[n:2 * n]
        send_sems, recv_sems = refs[2 * n:]
        x, y, c = _position()
        sibling = (x, y, 1 - c)
        copies = []
        for w in range(n):
            for p in range(4):
                cp = pltpu.make_async_remote_copy(
                    src_ref=ins[w].at[2 * p + (1 - c)], dst_ref=outs[w].at[p],
                    send_sem=send_sems.at[w * 4 + p], recv_sem=recv_sems.at[w * 4 + p],
                    device_id=sibling, device_id_type=MESH)
                cp.start()
                copies.append(cp)
        for cp in copies:
            cp.wait()

    return pl.pallas_call(
        body, name=name,
        in_specs=[_ANY] * n, out_specs=[_ANY] * n,
        out_shape=[jax.ShapeDtypeStruct((4,) + g.shape[1:], g.dtype) for g in grads8],
        scratch_shapes=[pltpu.SemaphoreType.DMA((4 * n,)), pltpu.SemaphoreType.DMA((4 * n,))],
    )(*grads8)


def _chip_sums(g8, from_sibling, pos, *, name):
    _, R, C = g8.shape
    tr = _div(R, 512, 16)

    def body(pos_ref, g_ref, s_ref, o_ref):
        o_ref[...] = (g_ref[...].astype(F32) + s_ref[...].astype(F32)).astype(BF16)

    grid_spec = pltpu.PrefetchScalarGridSpec(
        num_scalar_prefetch=1, grid=(4, R // tr),
        in_specs=[pl.BlockSpec((None, tr, C), lambda p, i, pos_ref: (2 * p + pos_ref[2], i, 0)),
                  pl.BlockSpec((None, tr, C), lambda p, i, pos_ref: (p, i, 0))],
        out_specs=pl.BlockSpec((None, tr, C), lambda p, i, pos_ref: (p, i, 0)))
    return pl.pallas_call(
        body, name=name, grid_spec=grid_spec,
        out_shape=jax.ShapeDtypeStruct((4, R, C), BF16),
        compiler_params=_cparams("parallel", "parallel"),
    )(pos, g8, from_sibling)


def _chip_copies(w, sums_ref, land_ref, send_sems, recv_sems):
    x, y, c = _position()
    return [pltpu.make_async_remote_copy(src_ref=sums_ref.at[2 * px + py], dst_ref=land_ref.at[k],
                                         send_sem=send_sems.at[3 * w + k], recv_sem=recv_sems.at[3 * w + k],
                                         device_id=(px, py, c), device_id_type=MESH)
            for k, (px, py) in enumerate(_other_chips(x, y))]


def _rs_chips_start(sums4, *, name):
    n = len(sums4)
    lands = [lax.empty((3,) + s.shape[1:], s.dtype) for s in sums4]

    def body(*refs):
        sums_refs, land_refs = refs[:n], refs[n:2 * n]
        send_sems, recv_sems = refs[2 * n], refs[2 * n + 1]
        token = refs[-1]
        for w in range(n):
            for cp in _chip_copies(w, sums_refs[w], land_refs[w], send_sems, recv_sems):
                cp.start()
        token[...] = jnp.zeros_like(token)

    outs = pl.pallas_call(
        body, name=name,
        in_specs=[_HBM] * (2 * n),
        out_specs=tuple([_SEM, _SEM] + [_HBM] * (2 * n) + [pl.BlockSpec(memory_space=pltpu.VMEM)]),
        out_shape=tuple([pltpu.SemaphoreType.DMA((3 * n,)), pltpu.SemaphoreType.DMA((3 * n,))]
                        + [pltpu.HBM(a.shape, a.dtype) for a in sums4 + lands] + [jax.ShapeDtypeStruct((8, LANES), F32)]),
        input_output_aliases={i: 2 + i for i in range(2 * n)},
        compiler_params=pltpu.CompilerParams(has_side_effects=_DATAFLOW),
    )(*[_in_hbm(a) for a in sums4 + lands])
    return outs[0], outs[1], list(outs[2:2 + n]), list(outs[2 + n:2 + 2 * n]), outs[-1]


def _rs_chips_wait(send_sems, recv_sems, sums4, lands, after, *, name):
    n = len(sums4)

    def body(*refs):
        sums_refs, land_refs = refs[:n], refs[n:2 * n]
        send_ref, recv_ref = refs[2 * n], refs[2 * n + 1]
        for w in range(n):
            for cp in _chip_copies(w, sums_refs[w], land_refs[w], send_ref, recv_ref):
                cp.wait_send()
                cp.wait_recv()

    outs = pl.pallas_call(
        body, name=name,
        in_specs=[_HBM] * (2 * n) + [_SEM, _SEM, _ANY],
        out_specs=tuple([_HBM] * (2 * n)),
        out_shape=tuple(pltpu.HBM(a.shape, a.dtype) for a in sums4 + lands),
        input_output_aliases={i: i for i in range(2 * n)},
        compiler_params=pltpu.CompilerParams(has_side_effects=_DATAFLOW),
    )(*sums4, *lands, send_sems, recv_sems, after)
    return list(outs[n:])


def _small_all_reduce(packed, after, *, name):
    R, L = packed.shape

    def body(x_ref, sum_ref, gath_ref, send_sems, recv_sems, local_sem):
        x, y, c = _position()
        me, sibling = (x, y, c), (x, y, 1 - c)
        chips = _other_chips(x, y)

        def rows(px, py, pc):
            return gath_ref.at[pl.ds(_slot(px, py, pc) * R, R), :]

        def copy(k, block, to, src=None):
            return pltpu.make_async_remote_copy(
                src_ref=rows(*block) if src is None else src, dst_ref=rows(*block),
                send_sem=send_sems.at[k], recv_sem=recv_sems.at[k], device_id=to, device_id_type=MESH)

        mine = pltpu.make_async_copy(x_ref, rows(*me), local_sem)
        mine.start()
        first = [copy(0, me, sibling, src=x_ref)]
        first += [copy(1 + j, me, (*chip, c), src=x_ref) for j, chip in enumerate(chips)]
        for cp in first:
            cp.start()
        passed = [copy(4 + j, (*chip, c), sibling) for j, chip in enumerate(chips)]
        for j, chip in enumerate(chips):
            copy(1 + j, (*chip, c), me).wait_recv()
            passed[j].start()
        copy(0, sibling, me).wait_recv()
        for j, chip in enumerate(chips):
            copy(4 + j, (*chip, 1 - c), me).wait_recv()
        for cp in first + passed:
            cp.wait_send()
        mine.wait()
        acc = gath_ref[0:R, :]
        for d in range(1, N_DEV):
            acc = acc + gath_ref[d * R:(d + 1) * R, :]
        sum_ref[...] = acc

    vmem = pl.BlockSpec(memory_space=pltpu.VMEM)
    body, in_specs, args = _ordered_after(body, 1, [vmem], (packed,), after)
    return pl.pallas_call(
        body, name=name, in_specs=in_specs, out_specs=vmem,
        out_shape=jax.ShapeDtypeStruct((R, L), F32),
        scratch_shapes=[pltpu.VMEM((N_DEV * R, L), F32), pltpu.SemaphoreType.DMA((7,)), pltpu.SemaphoreType.DMA((7,)),
                        pltpu.SemaphoreType.DMA],
        compiler_params=pltpu.CompilerParams(vmem_limit_bytes=VMEM_LIMIT),
    )(*args)


def _adamw_math(w, g, m, v):
    m = ADAM_B1 * m + (1.0 - ADAM_B1) * g
    v = ADAM_B2 * v + (1.0 - ADAM_B2) * (g * g)
    m_hat = m / (1.0 - ADAM_B1 ** ADAM_STEP)
    v_hat = v / (1.0 - ADAM_B2 ** ADAM_STEP)
    delta = -ADAM_LR * (m_hat / (jnp.sqrt(v_hat) + ADAM_EPS) + ADAM_WD * w)
    return delta, m, v


def _adamw_shard(w, m, v, g8, from_sibling, from_chips, pos, *, name):
    R, C = w.shape
    tr = _div(R, 256, 16)

    def body(pos_ref, w_ref, m_ref, v_ref, g_ref, s_ref, r_ref, go_ref, d_ref, mo_ref, vo_ref):
        g = g_ref[...].astype(F32) + s_ref[...].astype(F32)
        for k in range(3):
            g = g + r_ref[k].astype(F32)
        delta, m_, v_ = _adamw_math(w_ref[...], g, m_ref[...], v_ref[...])
        go_ref[...] = g
        d_ref[...] = delta
        mo_ref[...] = m_
        vo_ref[...] = v_

    blk = pl.BlockSpec((tr, C), lambda i, pos_ref: (i, 0))
    grid_spec = pltpu.PrefetchScalarGridSpec(
        num_scalar_prefetch=1, grid=(R // tr,),
        in_specs=[blk, blk, blk,
                  pl.BlockSpec((None, tr, C), lambda i, pos_ref: (pos_ref[0], i, 0)),
                  pl.BlockSpec((None, tr, C), lambda i, pos_ref: (pos_ref[1], i, 0)),
                  pl.BlockSpec((3, tr, C), lambda i, pos_ref: (0, i, 0))],
        out_specs=[blk] * 4)
    out = jax.ShapeDtypeStruct((R, C), F32)
    return pl.pallas_call(
        body, name=name, grid_spec=grid_spec, out_shape=[out] * 4,
        compiler_params=_cparams("parallel"),
    )(pos, w, m, v, g8, from_sibling, from_chips)


def _adamw_small(w, g, m, v, *, name):
    R, L = w.shape

    def body(w_ref, g_ref, m_ref, v_ref, d_ref, mo_ref, vo_ref):
        delta, m_, v_ = _adamw_math(w_ref[...], g_ref[...], m_ref[...], v_ref[...])
        d_ref[...] = delta
        mo_ref[...] = m_
        vo_ref[...] = v_

    vmem = pl.BlockSpec(memory_space=pltpu.VMEM)
    out = jax.ShapeDtypeStruct((R, L), F32)
    return pl.pallas_call(body, name=name, in_specs=[vmem] * 4, out_specs=[vmem] * 3, out_shape=[out] * 3)(w, g, m, v)


_TILE = 8 * LANES


def _pack(pieces):
    rows = []
    for p in pieces:
        flat = p.reshape(-1).astype(F32)
        padded = -(-flat.shape[0] // _TILE) * _TILE
        rows.append(jnp.pad(flat, (0, padded - flat.shape[0])).reshape(-1, LANES))
    return jnp.concatenate(rows, axis=0)


def _unpack(packed, like):
    out, r = [], 0
    for p in like:
        size = int(np.prod(p.shape)) if p.shape else 1
        nrows = -(-size // _TILE) * 8
        out.append(packed[r:r + nrows].reshape(-1)[:size].reshape(p.shape))
        r += nrows
    return out


_COL_SHARDED = ("w_in", "w_a_out", "w_b_out", "w_gate", "w_up")
_BIG = ("w_in", "w_a_out", "w_b_out", "w_o", "w_gate", "w_up", "w_down")
_GATHER_GROUPS = (("w_in",), ("w_a_out", "w_b_out", "w_o"), ("w_gate", "w_up"), ("w_down",))
_SMALL = ("norm_mix", "sgu_v_gain", "sgu_w_s", "sgu_b_s", "attn_sink", "rel_bias", "norm_ffn", "norm_final")
_ORDER = ("w_in", "norm_mix", "sgu_v_gain", "sgu_w_s", "sgu_b_s", "w_a_out", "attn_sink", "rel_bias", "w_b_out", "w_o",
          "norm_ffn", "w_gate", "w_up", "w_down", "norm_final")


def _whole(name, gathered):
    _, r, c = gathered.shape
    return gathered if name in _COL_SHARDED else gathered.reshape(N_DEV * r, c)


def _blocks(name, grad):
    if name in _COL_SHARDED:
        return grad
    r, c = grad.shape
    return grad.reshape(N_DEV, r // N_DEV, c)


def kernel(x, w_in, norm_mix, sgu_v_gain, sgu_w_s, sgu_b_s, w_a_out, attn_sink, rel_bias, w_b_out, w_o, norm_ffn, w_gate, w_up, w_down, norm_final, loss_target, m_w_in, m_norm_mix, m_sgu_v_gain, m_sgu_w_s, m_sgu_b_s, m_w_a_out, m_attn_sink, m_rel_bias, m_w_b_out, m_w_o, m_norm_ffn, m_w_gate, m_w_up, m_w_down, m_norm_final, v_w_in, v_norm_mix, v_sgu_v_gain, v_sgu_w_s, v_sgu_b_s, v_w_a_out, v_attn_sink, v_rel_bias, v_w_b_out, v_w_o, v_norm_ffn, v_w_gate, v_w_up, v_w_down, v_norm_final):
    w = dict(w_in=w_in, norm_mix=norm_mix, sgu_v_gain=sgu_v_gain, sgu_w_s=sgu_w_s, sgu_b_s=sgu_b_s, w_a_out=w_a_out,
             attn_sink=attn_sink, rel_bias=rel_bias, w_b_out=w_b_out, w_o=w_o, norm_ffn=norm_ffn, w_gate=w_gate,
             w_up=w_up, w_down=w_down, norm_final=norm_final)
    m = dict(w_in=m_w_in, norm_mix=m_norm_mix, sgu_v_gain=m_sgu_v_gain, sgu_w_s=m_sgu_w_s, sgu_b_s=m_sgu_b_s,
             w_a_out=m_w_a_out, attn_sink=m_attn_sink, rel_bias=m_rel_bias, w_b_out=m_w_b_out, w_o=m_w_o,
             norm_ffn=m_norm_ffn, w_gate=m_w_gate, w_up=m_w_up, w_down=m_w_down, norm_final=m_norm_final)
    v = dict(w_in=v_w_in, norm_mix=v_norm_mix, sgu_v_gain=v_sgu_v_gain, sgu_w_s=v_sgu_w_s, sgu_b_s=v_sgu_b_s,
             w_a_out=v_w_a_out, attn_sink=v_attn_sink, rel_bias=v_rel_bias, w_b_out=v_w_b_out, w_o=v_w_o,
             norm_ffn=v_norm_ffn, w_gate=v_w_gate, w_up=v_w_up, w_down=v_w_down, norm_final=v_norm_final)
    xc, yc, cc = _position()
    pos = jnp.stack([_slot(xc, yc, cc), 2 * xc + yc, cc]).astype(jnp.int32)

    buffers = [_own_slot(w[n][0], pos, name="own_slot_" + n) for n in _BIG]
    in_flight = _ag_start(buffers, [[_BIG.index(n) for n in grp] for grp in _GATHER_GROUPS], name="ag_start")
    full = {}

    def weight(name, after):
        if name not in full:
            gi = next(i for i, grp in enumerate(_GATHER_GROUPS) if name in grp)
            send_sems, recv_sems, lands = in_flight[gi]
            lands = _ag_wait(send_sems, recv_sems, lands, after, name="ag_wait_%d" % gi)
            gathered = _ag_forward(lands, name="ag_forward_%d" % gi)
            full.update({n: _whole(n, g) for n, g in zip(_GATHER_GROUPS[gi], gathered)})
        return full[name]

    reducing = {}

    def emit(names, grads):
        g8 = [_blocks(n, g) for n, g in zip(names, grads)]
        from_sibling = _rs_to_sibling(g8, name="rs_to_sibling_" + names[0])
        sums4 = [_chip_sums(g, s, pos, name="chip_sums_" + n) for n, g, s in zip(names, g8, from_sibling)]
        send_sems, recv_sems, sums4, lands, token = _rs_chips_start(sums4, name="rs_chips_start_" + names[0])
        reducing[names] = (g8, from_sibling, send_sems, recv_sems, sums4, lands)
        return token

    loss, grad_x, small_grads_local = _local_step(
        x[0], loss_target[0], weight, emit, norm_mix, sgu_v_gain, sgu_w_s[0], sgu_b_s[0], attn_sink, rel_bias, norm_ffn,
        norm_final[None])

    out_g, out_d, out_m, out_v = {}, {}, {}, {}
    small_like = [w[n] for n in _SMALL]
    small_w = _pack(small_like)
    packed = _pack([small_grads_local[n] for n in _SMALL] + [loss[0, 0]])
    after = grad_x
    for gi, (names, (g8, from_sibling, send_sems, recv_sems, sums4, lands)) in enumerate(reducing.items()):
        if gi == len(reducing) - 1:
            summed = _small_all_reduce(packed, after, name="small_all_reduce")
            after = summed
        from_chips = _rs_chips_wait(send_sems, recv_sems, sums4, lands, after, name="rs_chips_wait_" + names[0])
        for i, n in enumerate(names):
            g, d, m_, v_ = _adamw_shard(w[n][0], m[n][0], v[n][0], g8[i], from_sibling[i], from_chips[i], pos,
                                        name="adamw_" + n)
            out_g[n], out_d[n], out_m[n], out_v[n] = g[None], d[None], m_[None], v_[None]
            after = d
    *small_grads, loss_sum = _unpack(summed, small_like + [jax.ShapeDtypeStruct((), F32)])
    d_s, m_s, v_s = _adamw_small(small_w, summed[:small_w.shape[0]], _pack([m[n] for n in _SMALL]),
                                 _pack([v[n] for n in _SMALL]), name="adamw_small")
    for n, g, d, m_, v_ in zip(_SMALL, small_grads, _unpack(d_s, small_like), _unpack(m_s, small_like), _unpack(v_s, small_like)):
        out_g[n], out_d[n], out_m[n], out_v[n] = g, d, m_, v_

    return (loss_sum, grad_x[None], *[out_g[n] for n in _ORDER], *[out_d[n] for n in _ORDER],
            *[out_m[n] for n in _ORDER], *[out_v[n] for n in _ORDER])
```

```python
import functools
import math

import numpy as np
import jax
import jax.numpy as jnp
from jax import lax
from jax.experimental import pallas as pl
from jax.experimental.pallas import tpu as pltpu

F32 = jnp.float32
BF16 = jnp.bfloat16

EPS = 1e-6
NEG = -1e30
HEAD_DIM = 128
BLOCK = 128
N_KV_HEADS = 2
KV_WIDTH = N_KV_HEADS * HEAD_DIM
REL_BUCKETS = 32
REL_MAX_DIST = 128

ADAM_LR = 0.001
ADAM_B1 = 0.9
ADAM_B2 = 0.999
ADAM_EPS = 1e-08
ADAM_WD = 0.01
ADAM_STEP = 10

N_DEV = 8
LANES = 128
VMEM_LIMIT = 56 * 1024 * 1024
MESH = pl.DeviceIdType.MESH


def _cparams(*sem):
    return pltpu.CompilerParams(dimension_semantics=sem, vmem_limit_bytes=VMEM_LIMIT)


def _div(n, target, mult=LANES):
    best = None
    for d in range(mult, min(n, target) + 1, mult):
        if n % d == 0:
            best = d
    assert best is not None, (n, target, mult)
    return best


_ANY = pl.BlockSpec(memory_space=pl.ANY)


def _ordered_after(body, n_inputs, in_specs, args, after):
    if after is None:
        return body, in_specs, args

    def wrapped(*refs):
        return body(*refs[:n_inputs], *refs[n_inputs + 1:])

    return wrapped, list(in_specs) + [_ANY], tuple(args) + (after,)


def _bucket_map():
    nb = REL_BUCKETS // 2
    qi = np.arange(BLOCK)[:, None]
    kj = np.arange(3 * BLOCK)[None, :]
    rel = kj - BLOCK - qi
    ret = np.where(rel > 0, nb, 0)
    n = np.abs(rel)
    max_exact = nb // 2
    nf = np.maximum(n, 1).astype(np.float32)
    large = max_exact + (np.log(nf / np.float32(max_exact)) / np.float32(math.log(REL_MAX_DIST / max_exact))
                         * np.float32(nb - max_exact)).astype(np.int32)
    large = np.minimum(large, nb - 1)
    return (ret + np.where(n < max_exact, n, large)).astype(np.int32)


_GELU_C = math.sqrt(2.0 / math.pi)
_GELU_A = 0.044715


def _gelu(x):
    t = jnp.tanh(_GELU_C * (x + _GELU_A * (x * x * x)))
    return 0.5 * x * (1.0 + t)


def _gelu_and_grad(x):
    x2 = x * x
    t = jnp.tanh(_GELU_C * (x + _GELU_A * (x2 * x)))
    g = 0.5 * x * (1.0 + t)
    dg = 0.5 * (1.0 + t) + 0.5 * x * (1.0 - t * t) * (_GELU_C * (1.0 + 3.0 * _GELU_A * x2))
    return g, dg


def _sigmoid(x):
    return 1.0 / (1.0 + jnp.exp(-x))


def _mm(a, b, *, name, ta=False, tb=False, add=None, out_dtype=F32, bm=1024, bn=1024, bk=None, after=None):
    if ta:
        K, M = a.shape
    else:
        M, K = a.shape
    N = b.shape[0] if tb else b.shape[1]
    assert (b.shape[1] if tb else b.shape[0]) == K
    bm = _div(M, bm)
    bn = _div(N, bn)
    bk = K if bk is None else _div(K, bk)
    nk = K // bk
    a_spec = pl.BlockSpec((bk, bm), lambda i, j, k: (k, i)) if ta else pl.BlockSpec((bm, bk), lambda i, j, k: (i, k))
    b_spec = pl.BlockSpec((bn, bk), lambda i, j, k: (j, k)) if tb else pl.BlockSpec((bk, bn), lambda i, j, k: (k, j))
    o_spec = pl.BlockSpec((bm, bn), lambda i, j, k: (i, j))
    dims = (((0 if ta else 1,), (1 if tb else 0,)), ((), ()))
    has_add = add is not None

    def body(*refs):
        if has_add:
            a_ref, b_ref, add_ref, o_ref, *scratch = refs
        else:
            a_ref, b_ref, o_ref, *scratch = refs
            add_ref = None
        p = lax.dot_general(a_ref[...].astype(BF16), b_ref[...].astype(BF16), dims, preferred_element_type=F32)
        if nk == 1:
            if has_add:
                p = p + add_ref[...]
            o_ref[...] = p.astype(out_dtype)
        else:
            acc = scratch[0]
            k = pl.program_id(2)

            @pl.when(k == 0)
            def _():
                acc[...] = p

            @pl.when(k > 0)
            def _():
                acc[...] += p

            @pl.when(k == nk - 1)
            def _():
                r = acc[...]
                if has_add:
                    r = r + add_ref[...]
                o_ref[...] = r.astype(out_dtype)

    in_specs = [a_spec, b_spec] + ([o_spec] if has_add else [])
    args = (a, b) + ((add,) if has_add else ())
    body, in_specs, args = _ordered_after(body, len(args), in_specs, args, after)
    return pl.pallas_call(
        body, name=name, grid=(M // bm, N // bn, nk),
        in_specs=in_specs, out_specs=o_spec,
        out_shape=jax.ShapeDtypeStruct((M, N), out_dtype),
        scratch_shapes=[pltpu.VMEM((bm, bn), F32)] if nk > 1 else [],
        compiler_params=_cparams("parallel", "parallel", "arbitrary"),
    )(*args)


def _blocks_per_tile(c):
    nb = 1
    while (nb * c) % LANES or (nb * c < 1024 and nb < N_DEV):
        nb *= 2
    assert nb <= N_DEV and (nb * c) % LANES == 0, c
    return nb


def _mm_w8(a, w8, *, name, bm=1024):
    M, K = a.shape
    _, _, c = w8.shape
    nb = _blocks_per_tile(c)
    bm = _div(M, bm)

    def body(a_ref, w_ref, o_ref):
        a_ = a_ref[...]
        for t in range(nb):
            o_ref[:, t * c:(t + 1) * c] = jnp.dot(a_, w_ref[t], preferred_element_type=F32)

    return pl.pallas_call(
        body, name=name, grid=(M // bm, N_DEV // nb),
        in_specs=[pl.BlockSpec((bm, K), lambda i, j: (i, 0)), pl.BlockSpec((nb, K, c), lambda i, j: (j, 0, 0))],
        out_specs=pl.BlockSpec((bm, nb * c), lambda i, j: (i, j)),
        out_shape=jax.ShapeDtypeStruct((M, N_DEV * c), F32),
        compiler_params=_cparams("parallel", "parallel"),
    )(a, w8)


def _mm_w8t(dy, w8, *, name, add=None, out_dtype=F32, bm=1024, bn=1024, after=None, lead=None):
    M = dy.shape[-2]
    _, K, c = w8.shape
    nb = _blocks_per_tile(c)
    nk = N_DEV // nb
    bm, bn = _div(M, bm), _div(K, bn)
    has_add = add is not None
    dims = (((1,), (1,)), ((), ()))

    def body(*refs):
        if has_add:
            dy_ref, w_ref, add_ref, o_ref, acc = refs
        else:
            dy_ref, w_ref, o_ref, acc = refs
        p = lax.dot_general(dy_ref[:, 0:c], w_ref[0], dims, preferred_element_type=F32)
        for t in range(1, nb):
            p = p + lax.dot_general(dy_ref[:, t * c:(t + 1) * c], w_ref[t], dims, preferred_element_type=F32)
        k = pl.program_id(2)

        @pl.when(k == 0)
        def _():
            acc[...] = p

        @pl.when(k > 0)
        def _():
            acc[...] += p

        @pl.when(k == nk - 1)
        def _():
            r = acc[...]
            if has_add:
                r = r + add_ref[...]
            o_ref[...] = r.astype(out_dtype)

    o_spec = pl.BlockSpec((bm, bn), lambda i, j, k: (i, j))
    dy_spec = (pl.BlockSpec((bm, nb * c), lambda i, j, k: (i, k)) if lead is None
               else pl.BlockSpec((None, bm, nb * c), lambda i, j, k: (lead, i, k)))
    in_specs = [dy_spec, pl.BlockSpec((nb, bn, c), lambda i, j, k: (k, j, 0))]
    in_specs += [o_spec] if has_add else []
    args = (dy, w8) + ((add,) if has_add else ())
    body, in_specs, args = _ordered_after(body, len(args), in_specs, args, after)
    return pl.pallas_call(
        body, name=name, grid=(M // bm, K // bn, nk),
        in_specs=in_specs, out_specs=o_spec,
        out_shape=jax.ShapeDtypeStruct((M, K), out_dtype),
        scratch_shapes=[pltpu.VMEM((bm, bn), F32)],
        compiler_params=_cparams("parallel", "parallel", "arbitrary"),
    )(*args)


def _mm_gw8(x, dy, c, *, name, bk=1024, lead=None):
    T, K = x.shape
    nb = _blocks_per_tile(c)
    bk = _div(K, bk)
    dims = (((0,), (0,)), ((), ()))

    def body(x_ref, dy_ref, o_ref):
        x_ = x_ref[...]
        for t in range(nb):
            o_ref[t] = lax.dot_general(x_, dy_ref[:, t * c:(t + 1) * c], dims, preferred_element_type=F32).astype(BF16)

    dy_spec = (pl.BlockSpec((T, nb * c), lambda i, j: (0, j)) if lead is None
               else pl.BlockSpec((None, T, nb * c), lambda i, j: (lead, 0, j)))
    return pl.pallas_call(
        body, name=name, grid=(K // bk, N_DEV // nb),
        in_specs=[pl.BlockSpec((T, bk), lambda i, j: (0, i)), dy_spec],
        out_specs=pl.BlockSpec((nb, bk, c), lambda i, j: (j, i, 0)),
        out_shape=jax.ShapeDtypeStruct((N_DEV, K, c), BF16),
        compiler_params=_cparams("parallel", "parallel"),
    )(x, dy)


def _rms_fwd(x, g, *, name):
    T, D = x.shape
    tm = _div(T, 256, 8)

    def body(x_ref, g_ref, h_ref):
        xf = x_ref[...]
        r = lax.rsqrt(jnp.mean(xf * xf, axis=-1, keepdims=True) + EPS)
        h_ref[...] = ((xf * r) * g_ref[...]).astype(BF16)

    return pl.pallas_call(
        body, name=name, grid=(T // tm,),
        in_specs=[pl.BlockSpec((tm, D), lambda i: (i, 0)), pl.BlockSpec((1, D), lambda i: (0, 0))],
        out_specs=pl.BlockSpec((tm, D), lambda i: (i, 0)),
        out_shape=jax.ShapeDtypeStruct((T, D), BF16),
        compiler_params=_cparams("parallel"),
    )(x, g)


def _rms_bwd(x, g, dh, dres, *, name, want_bf16, after=None):
    T, D = x.shape
    tm = _div(T, 256, 8)

    def body(x_ref, g_ref, dh_ref, dres_ref, dx_ref, *rest):
        if want_bf16:
            dxb_ref, dg_ref = rest
        else:
            (dg_ref,) = rest
        xf = x_ref[...]
        r = lax.rsqrt(jnp.mean(xf * xf, axis=-1, keepdims=True) + EPS)
        xhat = xf * r
        dh_ = dh_ref[...]
        dy = dh_ * g_ref[...]
        dx = dres_ref[...] + r * (dy - xhat * jnp.mean(dy * xhat, axis=-1, keepdims=True))
        dx_ref[...] = dx
        if want_bf16:
            dxb_ref[...] = dx.astype(BF16)
        part = jnp.sum(dh_ * xhat, axis=0, keepdims=True)

        @pl.when(pl.program_id(0) == 0)
        def _():
            dg_ref[...] = part

        @pl.when(pl.program_id(0) > 0)
        def _():
            dg_ref[...] += part

    row = pl.BlockSpec((tm, D), lambda i: (i, 0))
    vec = pl.BlockSpec((1, D), lambda i: (0, 0))
    out_specs = [row] + ([row] if want_bf16 else []) + [vec]
    out_shape = ([jax.ShapeDtypeStruct((T, D), F32)] + ([jax.ShapeDtypeStruct((T, D), BF16)] if want_bf16 else [])
                 + [jax.ShapeDtypeStruct((1, D), F32)])
    body, in_specs, args = _ordered_after(body, 4, [row, vec, row, row], (x, g, dh, dres), after)
    return pl.pallas_call(
        body, name=name, grid=(T // tm,),
        in_specs=in_specs, out_specs=out_specs, out_shape=out_shape,
        compiler_params=_cparams("arbitrary"),
    )(*args)


def _loss_head(x, g, target, *, name):
    T, D = x.shape
    tm = _div(T, 256, 8)

    def body(x_ref, g_ref, t_ref, loss_ref, dx_ref, dxb_ref, dg_ref):
        xf = x_ref[...]
        r = lax.rsqrt(jnp.mean(xf * xf, axis=-1, keepdims=True) + EPS)
        xhat = xf * r
        gain = g_ref[...]
        err = xhat * gain - t_ref[...]
        lpart = 0.5 * jnp.sum(jnp.mean(err * err, axis=-1, keepdims=True), axis=0, keepdims=True)
        dh_ = err * (1.0 / D)
        dy = dh_ * gain
        dx = r * (dy - xhat * jnp.mean(dy * xhat, axis=-1, keepdims=True))
        dx_ref[...] = dx
        dxb_ref[...] = dx.astype(BF16)
        part = jnp.sum(dh_ * xhat, axis=0, keepdims=True)

        @pl.when(pl.program_id(0) == 0)
        def _():
            dg_ref[...] = part
            loss_ref[...] = jnp.broadcast_to(lpart, loss_ref.shape)

        @pl.when(pl.program_id(0) > 0)
        def _():
            dg_ref[...] += part
            loss_ref[...] += jnp.broadcast_to(lpart, loss_ref.shape)

    row = pl.BlockSpec((tm, D), lambda i: (i, 0))
    vec = pl.BlockSpec((1, D), lambda i: (0, 0))
    return pl.pallas_call(
        body, name=name, grid=(T // tm,),
        in_specs=[row, vec, row],
        out_specs=[pl.BlockSpec((8, LANES), lambda i: (0, 0)), row, row, vec],
        out_shape=[jax.ShapeDtypeStruct((8, LANES), F32), jax.ShapeDtypeStruct((T, D), F32),
                   jax.ShapeDtypeStruct((T, D), BF16), jax.ShapeDtypeStruct((1, D), F32)],
        compiler_params=_cparams("arbitrary"),
    )(x, g, target)


def _gate_cols(D):
    off_a = 3 * D // 2 + 2 * KV_WIDTH
    off_b = off_a + D
    cw = math.gcd(math.gcd(off_a, off_b), math.gcd(D, 512))
    return cw, off_a // cw, off_b // cw


def _merge_fwd(z, ya, yb, *, name):
    T, D = ya.shape
    cw, ba, bb = _gate_cols(D)
    tm = _div(T, 512, 8)

    def body(ga_ref, gb_ref, ya_ref, yb_ref, m_ref):
        m_ref[...] = (_sigmoid(ga_ref[...]) * ya_ref[...] + _sigmoid(gb_ref[...]) * yb_ref[...]).astype(BF16)

    blk = pl.BlockSpec((tm, cw), lambda i, j: (i, j))
    return pl.pallas_call(
        body, name=name, grid=(T // tm, D // cw),
        in_specs=[pl.BlockSpec((tm, cw), lambda i, j: (i, ba + j)), pl.BlockSpec((tm, cw), lambda i, j: (i, bb + j)), blk, blk],
        out_specs=blk, out_shape=jax.ShapeDtypeStruct((T, D), BF16),
        compiler_params=_cparams("parallel", "parallel"),
    )(z, z, ya, yb)


def _merge_bwd(z, ya, yb, dm, *, name, after=None):
    T, D = ya.shape
    cw, ba, bb = _gate_cols(D)
    nj = D // cw
    assert bb == ba + nj
    tm = _div(T, 512, 8)

    def body(g_ref, ya_ref, yb_ref, dm_ref, dy_ref, dz_ref):
        sig = _sigmoid(g_ref[...])
        dm_ = dm_ref[...]
        y = jnp.where(pl.program_id(1) == 0, ya_ref[...], yb_ref[...])
        dy_ref[...] = (dm_ * sig).astype(BF16)
        dz_ref[...] = (dm_ * y * (sig * (1.0 - sig))).astype(BF16)

    in_specs = [pl.BlockSpec((tm, cw), lambda i, s, j: (i, ba + s * nj + j)),
                pl.BlockSpec((tm, cw), lambda i, s, j: (i, j * (1 - s))),
                pl.BlockSpec((tm, cw), lambda i, s, j: (i, j * s)),
                pl.BlockSpec((tm, cw), lambda i, s, j: (i, j))]
    body, in_specs, args = _ordered_after(body, 4, in_specs, (z, ya, yb, dm), after)
    return pl.pallas_call(
        body, name=name, grid=(T // tm, 2, nj),
        in_specs=in_specs,
        out_specs=[pl.BlockSpec((None, tm, cw), lambda i, s, j: (s, i, j)),
                   pl.BlockSpec((tm, cw), lambda i, s, j: (i, ba + s * nj + j))],
        out_shape=[jax.ShapeDtypeStruct((2, T, D), BF16), jax.ShapeDtypeStruct(z.shape, BF16)],
        compiler_params=_cparams("parallel", "arbitrary", "arbitrary"),
    )(*args)


def _swiglu_fwd(gate, up, *, name):
    T, F = gate.shape
    tm, cw = _div(T, 512, 8), _div(F, 512)

    def body(g_ref, u_ref, act_ref):
        g = g_ref[...]
        act_ref[...] = (g * _sigmoid(g) * u_ref[...]).astype(BF16)

    blk = pl.BlockSpec((tm, cw), lambda i, j: (i, j))
    return pl.pallas_call(
        body, name=name, grid=(T // tm, F // cw), in_specs=[blk, blk], out_specs=blk,
        out_shape=jax.ShapeDtypeStruct((T, F), BF16), compiler_params=_cparams("parallel", "parallel"),
    )(gate, up)


def _swiglu_bwd(gate, up, dact, *, name, after=None):
    T, F = gate.shape
    tm, cw = _div(T, 512, 8), _div(F, 512)

    def body(g_ref, u_ref, d_ref, dg_ref, du_ref):
        g = g_ref[...]
        s = _sigmoid(g)
        d = d_ref[...]
        silu = g * s
        dg_ref[...] = (d * u_ref[...] * (s + silu * (1.0 - s))).astype(BF16)
        du_ref[...] = (d * silu).astype(BF16)

    blk = pl.BlockSpec((tm, cw), lambda i, j: (i, j))
    out = jax.ShapeDtypeStruct((T, F), BF16)
    body, in_specs, args = _ordered_after(body, 3, [blk, blk, blk], (gate, up, dact), after)
    return pl.pallas_call(
        body, name=name, grid=(T // tm, F // cw), in_specs=in_specs, out_specs=[blk, blk],
        out_shape=[out, out], compiler_params=_cparams("parallel", "parallel"),
    )(*args)


def _sgu_fwd(z, gain, ws_b, bs_t, *, name):
    T = z.shape[0]
    SW = gain.shape[1]
    G = SW // BLOCK

    def body(zu_ref, zv_ref, gain_ref, ws_ref, bs_ref, a_ref):
        u = _gelu(zu_ref[...])
        vg = _gelu(zv_ref[...])
        r = lax.rsqrt(jnp.mean(vg * vg, axis=-1, keepdims=True) + EPS)
        vn = ((vg * r) * gain_ref[...]).astype(BF16)
        for g in range(G):
            sl = slice(g * BLOCK, (g + 1) * BLOCK)
            mixed = jnp.dot(ws_ref[g], vn[:, sl], preferred_element_type=F32) + bs_ref[:, g:g + 1]
            a_ref[:, sl] = (u[:, sl] * mixed).astype(BF16)

    return pl.pallas_call(
        body, name=name, grid=(T // BLOCK,),
        in_specs=[pl.BlockSpec((BLOCK, SW), lambda c: (c, 0)), pl.BlockSpec((BLOCK, SW), lambda c: (c, 1)),
                  pl.BlockSpec((1, SW), lambda c: (0, 0)), pl.BlockSpec((G, BLOCK, BLOCK), lambda c: (0, 0, 0)),
                  pl.BlockSpec((BLOCK, G), lambda c: (0, 0))],
        out_specs=pl.BlockSpec((BLOCK, SW), lambda c: (c, 0)),
        out_shape=jax.ShapeDtypeStruct((T, SW), BF16),
        compiler_params=_cparams("parallel"),
    )(z, z, gain, ws_b, bs_t)


def _sgu_bwd(z, gain, ws_b, bs_t, da, dz, *, name):
    T = z.shape[0]
    SW = gain.shape[1]
    G = SW // BLOCK

    def body(zu_ref, zv_ref, gain_ref, ws_ref, bs_ref, da_ref, dz_in_ref, dz_ref, dws_ref, dbs_ref, dgain_ref, dvn_ref):
        first = pl.program_id(0) == 0

        @pl.when(first)
        def _():
            dws_ref[...] = jnp.zeros_like(dws_ref)
            dbs_ref[...] = jnp.zeros_like(dbs_ref)
            dgain_ref[...] = jnp.zeros_like(dgain_ref)

        u, du = _gelu_and_grad(zu_ref[...])
        vg, dvg = _gelu_and_grad(zv_ref[...])
        r = lax.rsqrt(jnp.mean(vg * vg, axis=-1, keepdims=True) + EPS)
        xhat = vg * r
        gain_ = gain_ref[...]
        vn = (xhat * gain_).astype(BF16)
        da_ = da_ref[...]
        for g in range(G):
            sl = slice(g * BLOCK, (g + 1) * BLOCK)
            w = ws_ref[g]
            mixed = jnp.dot(w, vn[:, sl], preferred_element_type=F32) + bs_ref[:, g:g + 1]
            dmix = da_[:, sl] * u[:, sl]
            dz_ref[:, sl] = (da_[:, sl] * mixed * du[:, sl]).astype(BF16)
            dmb = dmix.astype(BF16)
            dws_ref[g] += lax.dot_general(dmb, vn[:, sl], (((1,), (1,)), ((), ())), preferred_element_type=F32)
            dbs_ref[:, g:g + 1] += jnp.sum(dmix, axis=-1, keepdims=True)
            dvn_ref[:, sl] = lax.dot_general(w, dmb, (((0,), (0,)), ((), ())), preferred_element_type=F32)
        dvn = dvn_ref[...]
        dgain_ref[...] += jnp.sum(dvn * xhat, axis=0, keepdims=True)
        dy = dvn * gain_
        dv_ = r * (dy - xhat * jnp.mean(dy * xhat, axis=-1, keepdims=True))
        dz_ref[:, SW:] = (dv_ * dvg).astype(BF16)

    row = pl.BlockSpec((BLOCK, SW), lambda c: (c, 0))
    return pl.pallas_call(
        body, name=name, grid=(T // BLOCK,),
        in_specs=[row, pl.BlockSpec((BLOCK, SW), lambda c: (c, 1)),
                  pl.BlockSpec((1, SW), lambda c: (0, 0)), pl.BlockSpec((G, BLOCK, BLOCK), lambda c: (0, 0, 0)),
                  pl.BlockSpec((BLOCK, G), lambda c: (0, 0)), row, _ANY],
        out_specs=[pl.BlockSpec((BLOCK, 2 * SW), lambda c: (c, 0)), pl.BlockSpec((G, BLOCK, BLOCK), lambda c: (0, 0, 0)),
                   pl.BlockSpec((BLOCK, G), lambda c: (0, 0)), pl.BlockSpec((1, SW), lambda c: (0, 0))],
        out_shape=[jax.ShapeDtypeStruct(dz.shape, dz.dtype),
                   jax.ShapeDtypeStruct((G, BLOCK, BLOCK), F32), jax.ShapeDtypeStruct((BLOCK, G), F32),
                   jax.ShapeDtypeStruct((1, SW), F32)],
        input_output_aliases={6: 0},
        scratch_shapes=[pltpu.VMEM((BLOCK, SW), F32)],
        compiler_params=_cparams("arbitrary"),
    )(z, z, gain, ws_b, bs_t, da, dz)


def _bias_table(rel_bias, bmap, *, name):
    H = rel_bias.shape[1]

    def body(rb_ref, bmap_ref, o_ref):
        bm_ = bmap_ref[...]
        for h in range(H):
            acc = jnp.zeros(bm_.shape, F32)
            for b in range(REL_BUCKETS):
                acc = jnp.where(bm_ == b, rb_ref[b, h], acc)
            o_ref[h] = acc

    return pl.pallas_call(
        body, name=name,
        in_specs=[pl.BlockSpec(memory_space=pltpu.SMEM), pl.BlockSpec(memory_space=pltpu.VMEM)],
        out_specs=pl.BlockSpec(memory_space=pltpu.VMEM),
        out_shape=jax.ShapeDtypeStruct((H, BLOCK, 3 * BLOCK), F32),
    )(rel_bias, bmap)


def _attn_probs(q_ref, kb, bias_ref, sink_ref, valid, h, group):
    kv = h // group
    qh = q_ref[:, h * HEAD_DIM:(h + 1) * HEAD_DIM].astype(BF16)
    s = lax.dot_general(qh, kb[:, kv * HEAD_DIM:(kv + 1) * HEAD_DIM], (((1,), (1,)), ((), ())),
                        preferred_element_type=F32)
    s = s * (HEAD_DIM ** -0.5) + bias_ref[h]
    s = jnp.where(valid, s, NEG)
    sink = sink_ref[0:1, h:h + 1]
    m = jnp.maximum(jnp.max(s, axis=-1, keepdims=True), sink)
    e = jnp.exp(s - m)
    es = jnp.exp(sink - m)
    inv = 1.0 / (jnp.sum(e, axis=-1, keepdims=True) + es)
    return e * inv, es * inv, qh


def _band_valid(n, T):
    row = lax.broadcasted_iota(jnp.int32, (BLOCK, 3 * BLOCK), 0)
    col = lax.broadcasted_iota(jnp.int32, (BLOCK, 3 * BLOCK), 1)
    rel = col - BLOCK - row
    key_pos = n * BLOCK + col - BLOCK
    return (jnp.abs(rel) <= BLOCK) & (key_pos >= 0) & (key_pos < T)


def _attn_fwd(z, kpad, vpad, bias, sink, *, name):
    T = z.shape[0]
    H = bias.shape[0]
    AW = H * HEAD_DIM
    group = H // N_KV_HEADS

    def body(q_ref, k_ref, v_ref, bias_ref, sink_ref, o_ref):
        n = pl.program_id(0)
        start = pl.multiple_of(n * BLOCK, BLOCK)
        kb = k_ref[pl.ds(start, 3 * BLOCK), :]
        vb = v_ref[pl.ds(start, 3 * BLOCK), :]
        valid = _band_valid(n, T)
        for h in range(H):
            kv = h // group
            p, _, _ = _attn_probs(q_ref, kb, bias_ref, sink_ref, valid, h, group)
            o = jnp.dot(p.astype(BF16), vb[:, kv * HEAD_DIM:(kv + 1) * HEAD_DIM], preferred_element_type=F32)
            o_ref[:, h * HEAD_DIM:(h + 1) * HEAD_DIM] = o.astype(BF16)

    full_kv = pl.BlockSpec((T + 2 * BLOCK, KV_WIDTH), lambda n: (0, 0))
    return pl.pallas_call(
        body, name=name, grid=(T // BLOCK,),
        in_specs=[pl.BlockSpec((BLOCK, AW), lambda n: (n, 2)), full_kv, full_kv,
                  pl.BlockSpec((H, BLOCK, 3 * BLOCK), lambda n: (0, 0, 0)), pl.BlockSpec((1, H), lambda n: (0, 0))],
        out_specs=pl.BlockSpec((BLOCK, AW), lambda n: (n, 0)),
        out_shape=jax.ShapeDtypeStruct((T, AW), BF16),
        compiler_params=_cparams("parallel"),
    )(z, kpad, vpad, bias, sink)


def _attn_bwd(z, kpad, vpad, bias, sink, do, dz, *, name):
    T = z.shape[0]
    H = bias.shape[0]
    AW = H * HEAD_DIM
    group = H // N_KV_HEADS
    scale = HEAD_DIM ** -0.5

    def body(q_ref, k_ref, v_ref, bias_ref, sink_ref, do_ref, dz_in_ref, dq_ref, dk_ref, dv_ref, dbias_ref, dsink_ref):
        n = pl.program_id(0)

        @pl.when(n == 0)
        def _():
            dk_ref[...] = jnp.zeros_like(dk_ref)
            dv_ref[...] = jnp.zeros_like(dv_ref)
            dbias_ref[...] = jnp.zeros_like(dbias_ref)
            dsink_ref[...] = jnp.zeros_like(dsink_ref)

        start = pl.multiple_of(n * BLOCK, BLOCK)
        kb = k_ref[pl.ds(start, 3 * BLOCK), :]
        vb = v_ref[pl.ds(start, 3 * BLOCK), :]
        valid = _band_valid(n, T)
        for kv in range(N_KV_HEADS):
            ksl = slice(kv * HEAD_DIM, (kv + 1) * HEAD_DIM)
            dk_acc = jnp.zeros((3 * BLOCK, HEAD_DIM), F32)
            dv_acc = jnp.zeros((3 * BLOCK, HEAD_DIM), F32)
            for gi in range(group):
                h = kv * group + gi
                hsl = slice(h * HEAD_DIM, (h + 1) * HEAD_DIM)
                p, p_sink, qh = _attn_probs(q_ref, kb, bias_ref, sink_ref, valid, h, group)
                doh = do_ref[:, hsl]
                dp = lax.dot_general(doh, vb[:, ksl], (((1,), (1,)), ((), ())), preferred_element_type=F32)
                delta = jnp.sum(p * dp, axis=-1, keepdims=True)
                ds = p * (dp - delta)
                dbias_ref[h] += ds
                dsink_ref[:, h:h + 1] += -(p_sink * delta)
                dsb = ds.astype(BF16)
                dq = jnp.dot(dsb, kb[:, ksl], preferred_element_type=F32) * scale
                dq_ref[:, hsl] = dq.astype(BF16)
                dk_acc = dk_acc + lax.dot_general(dsb, qh, (((0,), (0,)), ((), ())), preferred_element_type=F32)
                dv_acc = dv_acc + lax.dot_general(p.astype(BF16), doh, (((0,), (0,)), ((), ())),
                                                  preferred_element_type=F32)
            dk_ref[pl.ds(start, 3 * BLOCK), ksl] += dk_acc * scale
            dv_ref[pl.ds(start, 3 * BLOCK), ksl] += dv_acc

    full_kv = pl.BlockSpec((T + 2 * BLOCK, KV_WIDTH), lambda n: (0, 0))
    bias_spec = pl.BlockSpec((H, BLOCK, 3 * BLOCK), lambda n: (0, 0, 0))
    row = pl.BlockSpec((BLOCK, AW), lambda n: (n, 0))
    q_cols = pl.BlockSpec((BLOCK, AW), lambda n: (n, 2))
    return pl.pallas_call(
        body, name=name, grid=(T // BLOCK,),
        in_specs=[q_cols, full_kv, full_kv, bias_spec, pl.BlockSpec((1, H), lambda n: (0, 0)), row, _ANY],
        out_specs=[q_cols, full_kv, full_kv, bias_spec, pl.BlockSpec((BLOCK, H), lambda n: (0, 0))],
        out_shape=[jax.ShapeDtypeStruct(dz.shape, dz.dtype),
                   jax.ShapeDtypeStruct((T + 2 * BLOCK, KV_WIDTH), F32), jax.ShapeDtypeStruct((T + 2 * BLOCK, KV_WIDTH), F32),
                   jax.ShapeDtypeStruct((H, BLOCK, 3 * BLOCK), F32), jax.ShapeDtypeStruct((BLOCK, H), F32)],
        input_output_aliases={6: 0},
        compiler_params=_cparams("arbitrary"),
    )(z, kpad, vpad, bias, sink, do, dz)


def _dkv_into(dkp, dvp, dz, *, name):
    T = dz.shape[0]
    D = (dz.shape[1] - 2 * KV_WIDTH) * 2 // 7
    col = (D + D // 2) // (2 * KV_WIDTH)
    assert col * 2 * KV_WIDTH == D + D // 2

    def body(dk_ref, dv_ref, dz_in_ref, o_ref):
        o_ref[:, :KV_WIDTH] = dk_ref[...].astype(BF16)
        o_ref[:, KV_WIDTH:] = dv_ref[...].astype(BF16)

    kv = pl.BlockSpec((BLOCK, KV_WIDTH), lambda n: (n + 1, 0))
    return pl.pallas_call(
        body, name=name, grid=(T // BLOCK,),
        in_specs=[kv, kv, _ANY], out_specs=pl.BlockSpec((BLOCK, 2 * KV_WIDTH), lambda n: (n, col)),
        out_shape=jax.ShapeDtypeStruct(dz.shape, dz.dtype), input_output_aliases={2: 0},
        compiler_params=_cparams("parallel"),
    )(dkp, dvp, dz)


def _kv_pad(z, *, name):
    T = z.shape[0]
    D = (z.shape[1] - 2 * KV_WIDTH) * 2 // 7
    kcol = (D + D // 2) // KV_WIDTH
    nb = T // BLOCK

    def body(k_ref, v_ref, ko_ref, vo_ref):
        b = pl.program_id(0)
        inside = (b >= 1) & (b <= nb)
        ko_ref[...] = jnp.where(inside, k_ref[...], 0.0).astype(BF16)
        vo_ref[...] = jnp.where(inside, v_ref[...], 0.0).astype(BF16)

    out = jax.ShapeDtypeStruct((T + 2 * BLOCK, KV_WIDTH), BF16)
    o_spec = pl.BlockSpec((BLOCK, KV_WIDTH), lambda b: (b, 0))
    return pl.pallas_call(
        body, name=name, grid=(nb + 2,),
        in_specs=[pl.BlockSpec((BLOCK, KV_WIDTH), lambda b: (jnp.clip(b - 1, 0, nb - 1), kcol)),
                  pl.BlockSpec((BLOCK, KV_WIDTH), lambda b: (jnp.clip(b - 1, 0, nb - 1), kcol + 1))],
        out_specs=[o_spec, o_spec], out_shape=[out, out],
        compiler_params=_cparams("parallel"),
    )(z, z)


def _attn_small_grads(dbias, dsink_rows, bmap, *, name):
    H = dbias.shape[0]

    def body(dbias_ref, dsink_ref, bmap_ref, drel_ref, ds_ref):
        bm_ = bmap_ref[...]
        for h in range(H):
            d = dbias_ref[h]
            for b in range(REL_BUCKETS):
                drel_ref[b, h] = jnp.sum(jnp.where(bm_ == b, d, 0.0))
            ds_ref[0, h] = jnp.sum(dsink_ref[:, h:h + 1])

    vmem = pl.BlockSpec(memory_space=pltpu.VMEM)
    smem = pl.BlockSpec(memory_space=pltpu.SMEM)
    return pl.pallas_call(
        body, name=name, in_specs=[vmem, vmem, vmem], out_specs=[smem, smem],
        out_shape=[jax.ShapeDtypeStruct((REL_BUCKETS, H), F32), jax.ShapeDtypeStruct((1, H), F32)],
    )(dbias, dsink_rows, bmap)


def _local_step(x, target, weight, emit, flush, norm_mix, v_gain, w_s, b_s, sink, rel_bias, norm_ffn, norm_final):
    T, D = x.shape
    ws_b = w_s.astype(BF16)
    bs_t = b_s.T
    bmap = jnp.asarray(_bucket_map())

    h = _rms_fwd(x, norm_mix, name="rms_mix")
    w_in = weight("w_in", h)
    z = _mm_w8(h, w_in, name="mm_z")
    a = _sgu_fwd(z, v_gain, ws_b, bs_t, name="sgu_fwd")
    w_a = weight("w_a_out", a)
    ya = _mm_w8(a, w_a, name="mm_ya", bm=2048)
    kpad, vpad = _kv_pad(z, name="kv_pad")
    bias = _bias_table(rel_bias, bmap, name="bias_table")
    o = _attn_fwd(z, kpad, vpad, bias, sink, name="attn_fwd")
    w_b = weight("w_b_out", o)
    yb = _mm_w8(o, w_b, name="mm_yb", bm=2048)
    m = _merge_fwd(z, ya, yb, name="merge_fwd")
    w_o = weight("w_o", m)
    x1 = _mm(m, w_o, name="mm_x1", add=x, bm=2048, bn=512)
    h2 = _rms_fwd(x1, norm_ffn, name="rms_ffn")
    w_gate = weight("w_gate", h2)
    w_up = weight("w_up", h2)
    gate = _mm_w8(h2, w_gate, name="mm_gate")
    up = _mm_w8(h2, w_up, name="mm_up")
    act = _swiglu_fwd(gate, up, name="swiglu_fwd")
    w_down = weight("w_down", act)
    x2 = _mm(act, w_down, name="mm_x2", add=x1, bm=1024, bn=1024, bk=2816)
    loss, dx2, dx2b, g_norm_final = _loss_head(x2, norm_final, target, name="loss_head")

    g_w_down = _mm(act, dx2b, ta=True, out_dtype=BF16, name="mm_gwdown", bm=512, bn=2048)
    tok = emit(("w_down",), (g_w_down,))
    dact = _mm(dx2b, w_down, tb=True, name="mm_dact", bm=2048, bn=512, after=tok)
    tok = flush(dact)
    dgate, dup = _swiglu_bwd(gate, up, dact, name="swiglu_bwd", after=tok)
    g_w_gate = _mm_gw8(h2, dgate, w_gate.shape[2], name="mm_gwgate")
    g_w_up = _mm_gw8(h2, dup, w_up.shape[2], name="mm_gwup")
    tok = emit(("w_gate", "w_up"), (g_w_gate, g_w_up))
    dh2 = _mm_w8t(dgate, w_gate, name="mm_dh2a", after=tok)
    tok = flush(dh2)
    dh2 = _mm_w8t(dup, w_up, add=dh2, name="mm_dh2b", after=tok)
    dx1, dx1b, g_norm_ffn = _rms_bwd(x1, norm_ffn, dh2, dx2, name="rms_ffn_bwd", want_bf16=True)

    g_w_o = _mm(m, dx1b, ta=True, out_dtype=BF16, name="mm_gwo", bm=2048, bn=512)
    tok = emit(("w_o",), (g_w_o,))
    dm = _mm(dx1b, w_o, tb=True, name="mm_dm", bm=2048, bn=512, after=tok)
    tok = flush(dm)
    dy, dz = _merge_bwd(z, ya, yb, dm, name="merge_bwd", after=tok)
    g_w_a = _mm_gw8(a, dy, w_a.shape[2], name="mm_gwa", lead=0)
    g_w_b = _mm_gw8(o, dy, w_b.shape[2], name="mm_gwb", lead=1)
    tok = emit(("w_a_out", "w_b_out"), (g_w_a, g_w_b))
    da = _mm_w8t(dy, w_a, name="mm_da", bm=2048, bn=512, after=tok, lead=0)
    tok = flush(da)
    do = _mm_w8t(dy, w_b, out_dtype=BF16, name="mm_do", bm=2048, bn=512, after=tok, lead=1)
    dz, g_w_s, g_b_s_t, g_v_gain = _sgu_bwd(z, v_gain, ws_b, bs_t, da, dz, name="sgu_bwd")
    dz, dkp, dvp, dbias, dsink_rows = _attn_bwd(z, kpad, vpad, bias, sink, do, dz, name="attn_bwd")
    dz = _dkv_into(dkp, dvp, dz, name="dkv_into_dz")
    g_rel_bias, g_sink = _attn_small_grads(dbias, dsink_rows, bmap, name="attn_small_grads")
    g_w_in = _mm_gw8(h, dz, w_in.shape[2], name="mm_gwin")
    tok = emit(("w_in",), (g_w_in,))
    dh = _mm_w8t(dz, w_in, name="mm_dh", after=tok)
    tok = flush(dh)
    grad_x, g_norm_mix = _rms_bwd(x, norm_mix, dh, dx1, name="rms_mix_bwd", want_bf16=False, after=tok)

    small = dict(norm_mix=g_norm_mix, sgu_v_gain=g_v_gain, sgu_w_s=g_w_s, sgu_b_s=g_b_s_t.T, attn_sink=g_sink,
                 rel_bias=g_rel_bias, norm_ffn=g_norm_ffn, norm_final=g_norm_final)
    return loss, grad_x, small


def _position():
    return lax.axis_index("x"), lax.axis_index("y"), lax.axis_index("c")


def _other_chips(x, y):
    return [(1 - x, y), (x, 1 - y), (1 - x, 1 - y)]


def _slot(px, py, pc):
    return 4 * px + 2 * py + pc


_HBM = pl.BlockSpec(memory_space=pltpu.HBM)
_SEM = pl.BlockSpec(memory_space=pltpu.SEMAPHORE)
_DATAFLOW = pltpu.SideEffectType.DATAFLOW_SIDE_EFFECTING


def _in_hbm(a):
    return pltpu.with_memory_space_constraint(a, pltpu.HBM)


def _own_slot(shard, pos, *, name, after=None):
    R, C = shard.shape
    tr = _div(R, 256, 16)

    def body(pos_ref, w_ref, o_ref):
        o_ref[...] = w_ref[...].astype(BF16)

    body, in_specs, args = _ordered_after(body, 2, [pl.BlockSpec((tr, C), lambda i, pos_ref: (i, 0))], (pos, shard), after)
    grid_spec = pltpu.PrefetchScalarGridSpec(
        num_scalar_prefetch=1, grid=(R // tr,), in_specs=in_specs,
        out_specs=pl.BlockSpec((None, tr, C), lambda i, pos_ref: (pos_ref[0], i, 0)))
    return pl.pallas_call(
        body, name=name, grid_spec=grid_spec,
        out_shape=jax.ShapeDtypeStruct((N_DEV, R, C), BF16),
        compiler_params=_cparams("parallel"),
    )(*args)


def _ag_copies(w, land_ref, send_sems, recv_sems):
    x, y, c = _position()
    mine = land_ref.at[_slot(x, y, c)]
    targets = [(px, py, c) for px, py in _other_chips(x, y)] + [(x, y, 1 - c)]
    return [pltpu.make_async_remote_copy(src_ref=mine, dst_ref=mine, send_sem=send_sems.at[4 * w + k],
                                         recv_sem=recv_sems.at[4 * w + k], device_id=to, device_id_type=MESH)
            for k, to in enumerate(targets)]


def _ag_start(buffers, groups, *, name):
    lands = [buffers[i] for g in groups for i in g]
    n, ng = len(lands), len(groups)
    sizes = [len(g) for g in groups]

    def body(*refs):
        land_refs = refs[:n]
        sems = refs[n:n + 2 * ng]
        token = refs[-1]
        i = 0
        for g in range(ng):
            for w in range(sizes[g]):
                for cp in _ag_copies(w, land_refs[i], sems[2 * g], sems[2 * g + 1]):
                    cp.start()
                i += 1
        token[...] = jnp.zeros_like(token)

    sem_shapes = [pltpu.SemaphoreType.DMA((4 * k,)) for k in sizes for _ in range(2)]
    outs = pl.pallas_call(
        body, name=name,
        in_specs=[_HBM] * n,
        out_specs=tuple([_SEM] * (2 * ng) + [_HBM] * n + [pl.BlockSpec(memory_space=pltpu.VMEM)]),
        out_shape=tuple(sem_shapes + [pltpu.HBM(a.shape, a.dtype) for a in lands] + [jax.ShapeDtypeStruct((8, LANES), F32)]),
        input_output_aliases={i: 2 * ng + i for i in range(n)},
        compiler_params=pltpu.CompilerParams(has_side_effects=_DATAFLOW),
    )(*[_in_hbm(a) for a in lands])
    sems, thru = outs[:2 * ng], outs[2 * ng:2 * ng + n]
    result, i = [], 0
    for g in range(ng):
        k = sizes[g]
        result.append((sems[2 * g], sems[2 * g + 1], list(thru[i:i + k])))
        i += k
    return result, outs[-1]


def _ag_wait(send_sems, recv_sems, lands, after, *, name):
    n = len(lands)

    def body(*refs):
        land_refs = refs[:n]
        send_ref, recv_ref = refs[n], refs[n + 1]
        for w in range(n):
            for cp in _ag_copies(w, land_refs[w], send_ref, recv_ref):
                cp.wait_send()
                cp.wait_recv()

    outs = pl.pallas_call(
        body, name=name,
        in_specs=[_HBM] * n + [_SEM, _SEM, _ANY],
        out_specs=tuple([_HBM] * n),
        out_shape=tuple(pltpu.HBM(a.shape, a.dtype) for a in lands),
        input_output_aliases={i: i for i in range(n)},
        compiler_params=pltpu.CompilerParams(has_side_effects=_DATAFLOW),
    )(*lands, send_sems, recv_sems, after)
    return list(outs)


def _ag_forward(lands, *, name):
    n = len(lands)

    def body(*refs):
        in_refs, out_refs = refs[:n], refs[n:2 * n]
        send_sems, recv_sems = refs[2 * n:]
        x, y, c = _position()
        copies = []
        for w in range(n):
            for k, (px, py) in enumerate(_other_chips(x, y)):
                cp = pltpu.make_async_remote_copy(
                    src_ref=in_refs[w].at[_slot(px, py, c)], dst_ref=out_refs[w].at[_slot(px, py, c)],
                    send_sem=send_sems.at[3 * w + k], recv_sem=recv_sems.at[3 * w + k],
                    device_id=(x, y, 1 - c), device_id_type=MESH)
                cp.start()
                copies.append(cp)
        for cp in copies:
            cp.wait()

    return pl.pallas_call(
        body, name=name,
        in_specs=[_ANY] * n, out_specs=[_ANY] * n,
        out_shape=[jax.ShapeDtypeStruct(a.shape, a.dtype) for a in lands],
        input_output_aliases={i: i for i in range(n)},
        scratch_shapes=[pltpu.SemaphoreType.DMA((3 * n,)), pltpu.SemaphoreType.DMA((3 * n,))],
    )(*lands)


def _sibling_copies(w, g8_ref, land_ref, send_sems, recv_sems):
    x, y, c = _position()
    return [pltpu.make_async_remote_copy(src_ref=g8_ref.at[2 * p + (1 - c)], dst_ref=land_ref.at[p],
                                         send_sem=send_sems.at[4 * w + p], recv_sem=recv_sems.at[4 * w + p],
                                         device_id=(x, y, 1 - c), device_id_type=MESH)
            for p in range(4)]


def _chip_copies(w, sums_ref, land_ref, send_sems, recv_sems):
    x, y, c = _position()
    return [pltpu.make_async_remote_copy(src_ref=sums_ref.at[2 * px + py], dst_ref=land_ref.at[k],
                                         send_sem=send_sems.at[3 * w + k], recv_sem=recv_sems.at[3 * w + k],
                                         device_id=(px, py, c), device_id_type=MESH)
            for k, (px, py) in enumerate(_other_chips(x, y))]


def _copies_start(copies, per_weight, srcs, *, name):
    n = len(srcs)
    lands = [lax.empty((per_weight,) + s.shape[1:], s.dtype) for s in srcs]

    def body(*refs):
        src_refs, land_refs = refs[:n], refs[n:2 * n]
        send_sems, recv_sems = refs[2 * n], refs[2 * n + 1]
        token = refs[-1]
        for w in range(n):
            for cp in copies(w, src_refs[w], land_refs[w], send_sems, recv_sems):
                cp.start()
        token[...] = jnp.zeros_like(token)

    outs = pl.pallas_call(
        body, name=name,
        in_specs=[_HBM] * (2 * n),
        out_specs=tuple([_SEM, _SEM] + [_HBM] * (2 * n) + [pl.BlockSpec(memory_space=pltpu.VMEM)]),
        out_shape=tuple([pltpu.SemaphoreType.DMA((per_weight * n,)), pltpu.SemaphoreType.DMA((per_weight * n,))]
                        + [pltpu.HBM(a.shape, a.dtype) for a in srcs + lands] + [jax.ShapeDtypeStruct((8, LANES), F32)]),
        input_output_aliases={i: 2 + i for i in range(2 * n)},
        compiler_params=pltpu.CompilerParams(has_side_effects=_DATAFLOW),
    )(*[_in_hbm(a) for a in srcs + lands])
    return outs[0], outs[1], list(outs[2:2 + n]), list(outs[2 + n:2 + 2 * n]), outs[-1]


def _copies_wait(copies, send_sems, recv_sems, srcs, lands, after, *, name):
    n = len(srcs)

    def body(*refs):
        src_refs, land_refs = refs[:n], refs[n:2 * n]
        send_ref, recv_ref = refs[2 * n], refs[2 * n + 1]
        for w in range(n):
            for cp in copies(w, src_refs[w], land_refs[w], send_ref, recv_ref):
                cp.wait_send()
                cp.wait_recv()

    outs = pl.pallas_call(
        body, name=name,
        in_specs=[_HBM] * (2 * n) + [_SEM, _SEM, _ANY],
        out_specs=tuple([_HBM] * (2 * n)),
        out_shape=tuple(pltpu.HBM(a.shape, a.dtype) for a in srcs + lands),
        input_output_aliases={i: i for i in range(2 * n)},
        compiler_params=pltpu.CompilerParams(has_side_effects=_DATAFLOW),
    )(*srcs, *lands, send_sems, recv_sems, after)
    return list(outs[:n]), list(outs[n:])


def _chip_sums(g8, from_sibling, pos, *, name):
    _, R, C = g8.shape
    tr = _div(R, 512, 16)

    def body(pos_ref, g_ref, s_ref, o_ref):
        o_ref[...] = (g_ref[...].astype(F32) + s_ref[...].astype(F32)).astype(BF16)

    grid_spec = pltpu.PrefetchScalarGridSpec(
        num_scalar_prefetch=1, grid=(4, R // tr),
        in_specs=[pl.BlockSpec((None, tr, C), lambda p, i, pos_ref: (2 * p + pos_ref[2], i, 0)),
                  pl.BlockSpec((None, tr, C), lambda p, i, pos_ref: (p, i, 0))],
        out_specs=pl.BlockSpec((None, tr, C), lambda p, i, pos_ref: (p, i, 0)))
    return pl.pallas_call(
        body, name=name, grid_spec=grid_spec,
        out_shape=jax.ShapeDtypeStruct((4, R, C), BF16),
        compiler_params=_cparams("parallel", "parallel"),
    )(pos, g8, from_sibling)


def _small_all_reduce(packed, after, *, name):
    R, L = packed.shape

    def body(x_ref, sum_ref, gath_ref, send_sems, recv_sems, local_sem):
        x, y, c = _position()
        me, sibling = (x, y, c), (x, y, 1 - c)
        chips = _other_chips(x, y)

        def rows(px, py, pc):
            return gath_ref.at[pl.ds(_slot(px, py, pc) * R, R), :]

        def copy(k, block, to, src=None):
            return pltpu.make_async_remote_copy(
                src_ref=rows(*block) if src is None else src, dst_ref=rows(*block),
                send_sem=send_sems.at[k], recv_sem=recv_sems.at[k], device_id=to, device_id_type=MESH)

        mine = pltpu.make_async_copy(x_ref, rows(*me), local_sem)
        mine.start()
        first = [copy(0, me, sibling, src=x_ref)]
        first += [copy(1 + j, me, (*chip, c), src=x_ref) for j, chip in enumerate(chips)]
        for cp in first:
            cp.start()
        passed = [copy(4 + j, (*chip, c), sibling) for j, chip in enumerate(chips)]
        for j, chip in enumerate(chips):
            copy(1 + j, (*chip, c), me).wait_recv()
            passed[j].start()
        copy(0, sibling, me).wait_recv()
        for j, chip in enumerate(chips):
            copy(4 + j, (*chip, 1 - c), me).wait_recv()
        for cp in first + passed:
            cp.wait_send()
        mine.wait()
        acc = gath_ref[0:R, :]
        for d in range(1, N_DEV):
            acc = acc + gath_ref[d * R:(d + 1) * R, :]
        sum_ref[...] = acc

    vmem = pl.BlockSpec(memory_space=pltpu.VMEM)
    body, in_specs, args = _ordered_after(body, 1, [vmem], (packed,), after)
    return pl.pallas_call(
        body, name=name, in_specs=in_specs, out_specs=vmem,
        out_shape=jax.ShapeDtypeStruct((R, L), F32),
        scratch_shapes=[pltpu.VMEM((N_DEV * R, L), F32), pltpu.SemaphoreType.DMA((7,)), pltpu.SemaphoreType.DMA((7,)),
                        pltpu.SemaphoreType.DMA],
        compiler_params=pltpu.CompilerParams(vmem_limit_bytes=VMEM_LIMIT),
    )(*args)


def _adamw_math(w, g, m, v):
    m = ADAM_B1 * m + (1.0 - ADAM_B1) * g
    v = ADAM_B2 * v + (1.0 - ADAM_B2) * (g * g)
    m_hat = m / (1.0 - ADAM_B1 ** ADAM_STEP)
    v_hat = v / (1.0 - ADAM_B2 ** ADAM_STEP)
    delta = -ADAM_LR * (m_hat / (jnp.sqrt(v_hat) + ADAM_EPS) + ADAM_WD * w)
    return delta, m, v


def _adamw_shard(w, m, v, g8, from_sibling, from_chips, pos, *, name):
    R, C = w.shape
    tr = _div(R, 256, 16)

    def body(pos_ref, w_ref, m_ref, v_ref, g_ref, s_ref, r_ref, go_ref, d_ref, mo_ref, vo_ref):
        g = g_ref[...].astype(F32) + s_ref[...].astype(F32)
        for k in range(3):
            g = g + r_ref[k].astype(F32)
        delta, m_, v_ = _adamw_math(w_ref[...], g, m_ref[...], v_ref[...])
        go_ref[...] = g
        d_ref[...] = delta
        mo_ref[...] = m_
        vo_ref[...] = v_

    blk = pl.BlockSpec((tr, C), lambda i, pos_ref: (i, 0))
    grid_spec = pltpu.PrefetchScalarGridSpec(
        num_scalar_prefetch=1, grid=(R // tr,),
        in_specs=[blk, blk, blk,
                  pl.BlockSpec((None, tr, C), lambda i, pos_ref: (pos_ref[0], i, 0)),
                  pl.BlockSpec((None, tr, C), lambda i, pos_ref: (pos_ref[1], i, 0)),
                  pl.BlockSpec((3, tr, C), lambda i, pos_ref: (0, i, 0))],
        out_specs=[blk] * 4)
    out = jax.ShapeDtypeStruct((R, C), F32)
    return pl.pallas_call(
        body, name=name, grid_spec=grid_spec, out_shape=[out] * 4,
        compiler_params=_cparams("parallel"),
    )(pos, w, m, v, g8, from_sibling, from_chips)


def _adamw_small(w, g, m, v, *, name):
    R, L = w.shape

    def body(w_ref, g_ref, m_ref, v_ref, d_ref, mo_ref, vo_ref):
        delta, m_, v_ = _adamw_math(w_ref[...], g_ref[...], m_ref[...], v_ref[...])
        d_ref[...] = delta
        mo_ref[...] = m_
        vo_ref[...] = v_

    vmem = pl.BlockSpec(memory_space=pltpu.VMEM)
    out = jax.ShapeDtypeStruct((R, L), F32)
    return pl.pallas_call(body, name=name, in_specs=[vmem] * 4, out_specs=[vmem] * 3, out_shape=[out] * 3)(w, g, m, v)


_TILE = 8 * LANES


def _pack(pieces):
    rows = []
    for p in pieces:
        flat = p.reshape(-1).astype(F32)
        padded = -(-flat.shape[0] // _TILE) * _TILE
        rows.append(jnp.pad(flat, (0, padded - flat.shape[0])).reshape(-1, LANES))
    return jnp.concatenate(rows, axis=0)


def _unpack(packed, like):
    out, r = [], 0
    for p in like:
        size = int(np.prod(p.shape)) if p.shape else 1
        nrows = -(-size // _TILE) * 8
        out.append(packed[r:r + nrows].reshape(-1)[:size].reshape(p.shape))
        r += nrows
    return out


_COL_SHARDED = ("w_in", "w_a_out", "w_b_out", "w_gate", "w_up")
_BIG = ("w_in", "w_a_out", "w_b_out", "w_o", "w_gate", "w_up", "w_down")
_GATHER_GROUPS = (("w_in",), ("w_a_out", "w_b_out", "w_o"), ("w_gate", "w_up"), ("w_down",))
_SMALL = ("norm_mix", "sgu_v_gain", "sgu_w_s", "sgu_b_s", "attn_sink", "rel_bias", "norm_ffn", "norm_final")
_ORDER = ("w_in", "norm_mix", "sgu_v_gain", "sgu_w_s", "sgu_b_s", "w_a_out", "attn_sink", "rel_bias", "w_b_out", "w_o",
          "norm_ffn", "w_gate", "w_up", "w_down", "norm_final")


def _whole(name, gathered):
    _, r, c = gathered.shape
    return gathered if name in _COL_SHARDED else gathered.reshape(N_DEV * r, c)


def _blocks(name, grad):
    if name in _COL_SHARDED:
        return grad
    r, c = grad.shape
    return grad.reshape(N_DEV, r // N_DEV, c)


def kernel(x, w_in, norm_mix, sgu_v_gain, sgu_w_s, sgu_b_s, w_a_out, attn_sink, rel_bias, w_b_out, w_o, norm_ffn, w_gate, w_up, w_down, norm_final, loss_target, m_w_in, m_norm_mix, m_sgu_v_gain, m_sgu_w_s, m_sgu_b_s, m_w_a_out, m_attn_sink, m_rel_bias, m_w_b_out, m_w_o, m_norm_ffn, m_w_gate, m_w_up, m_w_down, m_norm_final, v_w_in, v_norm_mix, v_sgu_v_gain, v_sgu_w_s, v_sgu_b_s, v_w_a_out, v_attn_sink, v_rel_bias, v_w_b_out, v_w_o, v_norm_ffn, v_w_gate, v_w_up, v_w_down, v_norm_final):
    w = dict(w_in=w_in, norm_mix=norm_mix, sgu_v_gain=sgu_v_gain, sgu_w_s=sgu_w_s, sgu_b_s=sgu_b_s, w_a_out=w_a_out,
             attn_sink=attn_sink, rel_bias=rel_bias, w_b_out=w_b_out, w_o=w_o, norm_ffn=norm_ffn, w_gate=w_gate,
             w_up=w_up, w_down=w_down, norm_final=norm_final)
    m = dict(w_in=m_w_in, norm_mix=m_norm_mix, sgu_v_gain=m_sgu_v_gain, sgu_w_s=m_sgu_w_s, sgu_b_s=m_sgu_b_s,
             w_a_out=m_w_a_out, attn_sink=m_attn_sink, rel_bias=m_rel_bias, w_b_out=m_w_b_out, w_o=m_w_o,
             norm_ffn=m_norm_ffn, w_gate=m_w_gate, w_up=m_w_up, w_down=m_w_down, norm_final=m_norm_final)
    v = dict(w_in=v_w_in, norm_mix=v_norm_mix, sgu_v_gain=v_sgu_v_gain, sgu_w_s=v_sgu_w_s, sgu_b_s=v_sgu_b_s,
             w_a_out=v_w_a_out, attn_sink=v_attn_sink, rel_bias=v_rel_bias, w_b_out=v_w_b_out, w_o=v_w_o,
             norm_ffn=v_norm_ffn, w_gate=v_w_gate, w_up=v_w_up, w_down=v_w_down, norm_final=v_norm_final)
    xc, yc, cc = _position()
    pos = jnp.stack([_slot(xc, yc, cc), 2 * xc + yc, cc]).astype(jnp.int32)

    first, rest = _GATHER_GROUPS[0], _GATHER_GROUPS[1:]
    first_flight, token = _ag_start([_own_slot(w[n][0], pos, name="own_slot_" + n) for n in first],
                                    [list(range(len(first)))], name="ag_start_first")
    later = [n for grp in rest for n in grp]
    buffers = [_own_slot(w[n][0], pos, name="own_slot_" + n, after=token) for n in later]
    rest_flight, _ = _ag_start(buffers, [[later.index(n) for n in grp] for grp in rest], name="ag_start_rest")
    in_flight = first_flight + rest_flight
    full = {}

    def weight(name, after):
        if name not in full:
            gi = next(i for i, grp in enumerate(_GATHER_GROUPS) if name in grp)
            send_sems, recv_sems, lands = in_flight[gi]
            lands = _ag_wait(send_sems, recv_sems, lands, after, name="ag_wait_%d" % gi)
            gathered = _ag_forward(lands, name="ag_forward_%d" % gi)
            full.update({n: _whole(n, g) for n, g in zip(_GATHER_GROUPS[gi], gathered)})
        return full[name]

    to_sibling, reducing = [], {}

    def emit(names, grads):
        g8 = [_blocks(n, g) for n, g in zip(names, grads)]
        send_sems, recv_sems, g8, lands, token = _copies_start(_sibling_copies, 4, g8, name="rs_sibling_start_" + names[0])
        to_sibling.append((names, send_sems, recv_sems, g8, lands))
        return token

    def flush(after):
        names, send_sems, recv_sems, g8, lands = to_sibling.pop()
        g8, from_sibling = _copies_wait(_sibling_copies, send_sems, recv_sems, g8, lands, after,
                                        name="rs_sibling_wait_" + names[0])
        sums4 = [_chip_sums(g, s, pos, name="chip_sums_" + n) for n, g, s in zip(names, g8, from_sibling)]
        send_sems, recv_sems, sums4, lands, token = _copies_start(_chip_copies, 3, sums4, name="rs_chips_start_" + names[0])
        reducing[names] = (g8, from_sibling, send_sems, recv_sems, sums4, lands)
        return token

    loss, grad_x, small_grads_local = _local_step(
        x[0], loss_target[0], weight, emit, flush, norm_mix, sgu_v_gain, sgu_w_s[0], sgu_b_s[0], attn_sink, rel_bias,
        norm_ffn, norm_final[None])

    out_g, out_d, out_m, out_v = {}, {}, {}, {}
    small_like = [w[n] for n in _SMALL]
    small_w = _pack(small_like)
    packed = _pack([small_grads_local[n] for n in _SMALL] + [loss[0, 0]])
    after = grad_x
    for gi, (names, (g8, from_sibling, send_sems, recv_sems, sums4, lands)) in enumerate(reducing.items()):
        if gi == len(reducing) - 1:
            summed = _small_all_reduce(packed, after, name="small_all_reduce")
            after = summed
        _, from_chips = _copies_wait(_chip_copies, send_sems, recv_sems, sums4, lands, after,
                                     name="rs_chips_wait_" + names[0])
        for i, n in enumerate(names):
            g, d, m_, v_ = _adamw_shard(w[n][0], m[n][0], v[n][0], g8[i], from_sibling[i], from_chips[i], pos,
                                        name="adamw_" + n)
            out_g[n], out_d[n], out_m[n], out_v[n] = g[None], d[None], m_[None], v_[None]
            after = d
    *small_grads, loss_sum = _unpack(summed, small_like + [jax.ShapeDtypeStruct((), F32)])
    d_s, m_s, v_s = _adamw_small(small_w, summed[:small_w.shape[0]], _pack([m[n] for n in _SMALL]),
                                 _pack([v[n] for n in _SMALL]), name="adamw_small")
    for n, g, d, m_, v_ in zip(_SMALL, small_grads, _unpack(d_s, small_like), _unpack(m_s, small_like), _unpack(v_s, small_like)):
        out_g[n], out_d[n], out_m[n], out_v[n] = g, d, m_, v_

    return (loss_sum, grad_x[None], *[out_g[n] for n in _ORDER], *[out_d[n] for n in _ORDER],
            *[out_m[n] for n in _ORDER], *[out_v[n] for n in _ORDER])
```

```python
import functools
import math

import numpy as np
import jax
import jax.numpy as jnp
from jax import lax
from jax.experimental import pallas as pl
from jax.experimental.pallas import tpu as pltpu

F32 = jnp.float32
BF16 = jnp.bfloat16

EPS = 1e-6
NEG = -1e30
HEAD_DIM = 128
BLOCK = 128
N_KV_HEADS = 2
KV_WIDTH = N_KV_HEADS * HEAD_DIM
REL_BUCKETS = 32
REL_MAX_DIST = 128

ADAM_LR = 0.001
ADAM_B1 = 0.9
ADAM_B2 = 0.999
ADAM_EPS = 1e-08
ADAM_WD = 0.01
ADAM_STEP = 10

N_DEV = 8
LANES = 128
VMEM_LIMIT = 56 * 1024 * 1024
MESH = pl.DeviceIdType.MESH


def _cparams(*sem):
    return pltpu.CompilerParams(dimension_semantics=sem, vmem_limit_bytes=VMEM_LIMIT)


def _div(n, target, mult=LANES):
    best = None
    for d in range(mult, min(n, target) + 1, mult):
        if n % d == 0:
            best = d
    assert best is not None, (n, target, mult)
    return best


_ANY = pl.BlockSpec(memory_space=pl.ANY)


def _ordered_after(body, n_inputs, in_specs, args, after):
    if after is None:
        return body, in_specs, args

    def wrapped(*refs):
        return body(*refs[:n_inputs], *refs[n_inputs + 1:])

    return wrapped, list(in_specs) + [_ANY], tuple(args) + (after,)


def _bucket_map():
    nb = REL_BUCKETS // 2
    qi = np.arange(BLOCK)[:, None]
    kj = np.arange(3 * BLOCK)[None, :]
    rel = kj - BLOCK - qi
    ret = np.where(rel > 0, nb, 0)
    n = np.abs(rel)
    max_exact = nb // 2
    nf = np.maximum(n, 1).astype(np.float32)
    large = max_exact + (np.log(nf / np.float32(max_exact)) / np.float32(math.log(REL_MAX_DIST / max_exact))
                         * np.float32(nb - max_exact)).astype(np.int32)
    large = np.minimum(large, nb - 1)
    return (ret + np.where(n < max_exact, n, large)).astype(np.int32)


_GELU_C = math.sqrt(2.0 / math.pi)
_GELU_A = 0.044715


def _gelu(x):
    t = jnp.tanh(_GELU_C * (x + _GELU_A * (x * x * x)))
    return 0.5 * x * (1.0 + t)


def _gelu_and_grad(x):
    x2 = x * x
    t = jnp.tanh(_GELU_C * (x + _GELU_A * (x2 * x)))
    g = 0.5 * x * (1.0 + t)
    dg = 0.5 * (1.0 + t) + 0.5 * x * (1.0 - t * t) * (_GELU_C * (1.0 + 3.0 * _GELU_A * x2))
    return g, dg


def _sigmoid(x):
    return 1.0 / (1.0 + jnp.exp(-x))


def _mm(a, b, *, name, ta=False, tb=False, add=None, out_dtype=F32, bm=1024, bn=1024, bk=None, after=None):
    if ta:
        K, M = a.shape
    else:
        M, K = a.shape
    N = b.shape[0] if tb else b.shape[1]
    assert (b.shape[1] if tb else b.shape[0]) == K
    bm = _div(M, bm)
    bn = _div(N, bn)
    bk = K if bk is None else _div(K, bk)
    nk = K // bk
    a_spec = pl.BlockSpec((bk, bm), lambda i, j, k: (k, i)) if ta else pl.BlockSpec((bm, bk), lambda i, j, k: (i, k))
    b_spec = pl.BlockSpec((bn, bk), lambda i, j, k: (j, k)) if tb else pl.BlockSpec((bk, bn), lambda i, j, k: (k, j))
    o_spec = pl.BlockSpec((bm, bn), lambda i, j, k: (i, j))
    dims = (((0 if ta else 1,), (1 if tb else 0,)), ((), ()))
    has_add = add is not None

    def body(*refs):
        if has_add:
            a_ref, b_ref, add_ref, o_ref, *scratch = refs
        else:
            a_ref, b_ref, o_ref, *scratch = refs
            add_ref = None
        p = lax.dot_general(a_ref[...].astype(BF16), b_ref[...].astype(BF16), dims, preferred_element_type=F32)
        if nk == 1:
            if has_add:
                p = p + add_ref[...]
            o_ref[...] = p.astype(out_dtype)
        else:
            acc = scratch[0]
            k = pl.program_id(2)

            @pl.when(k == 0)
            def _():
                acc[...] = p

            @pl.when(k > 0)
            def _():
                acc[...] += p

            @pl.when(k == nk - 1)
            def _():
                r = acc[...]
                if has_add:
                    r = r + add_ref[...]
                o_ref[...] = r.astype(out_dtype)

    in_specs = [a_spec, b_spec] + ([o_spec] if has_add else [])
    args = (a, b) + ((add,) if has_add else ())
    body, in_specs, args = _ordered_after(body, len(args), in_specs, args, after)
    return pl.pallas_call(
        body, name=name, grid=(M // bm, N // bn, nk),
        in_specs=in_specs, out_specs=o_spec,
        out_shape=jax.ShapeDtypeStruct((M, N), out_dtype),
        scratch_shapes=[pltpu.VMEM((bm, bn), F32)] if nk > 1 else [],
        compiler_params=_cparams("parallel", "parallel", "arbitrary"),
    )(*args)


def _blocks_per_tile(c):
    nb = 1
    while (nb * c) % LANES or (nb * c < 1024 and nb < N_DEV):
        nb *= 2
    assert nb <= N_DEV and (nb * c) % LANES == 0, c
    return nb


def _mm_w8(a, w8, *, name, bm=1024):
    M, K = a.shape
    _, _, c = w8.shape
    nb = _blocks_per_tile(c)
    bm = _div(M, bm)

    def body(a_ref, w_ref, o_ref):
        a_ = a_ref[...]
        for t in range(nb):
            o_ref[:, t * c:(t + 1) * c] = jnp.dot(a_, w_ref[t], preferred_element_type=F32)

    return pl.pallas_call(
        body, name=name, grid=(M // bm, N_DEV // nb),
        in_specs=[pl.BlockSpec((bm, K), lambda i, j: (i, 0)), pl.BlockSpec((nb, K, c), lambda i, j: (j, 0, 0))],
        out_specs=pl.BlockSpec((bm, nb * c), lambda i, j: (i, j)),
        out_shape=jax.ShapeDtypeStruct((M, N_DEV * c), F32),
        compiler_params=_cparams("parallel", "parallel"),
    )(a, w8)


def _mm_w8t(dy, w8, *, name, add=None, out_dtype=F32, bm=1024, bn=1024, after=None, lead=None):
    M = dy.shape[-2]
    _, K, c = w8.shape
    nb = _blocks_per_tile(c)
    nk = N_DEV // nb
    bm, bn = _div(M, bm), _div(K, bn)
    has_add = add is not None
    dims = (((1,), (1,)), ((), ()))

    def body(*refs):
        if has_add:
            dy_ref, w_ref, add_ref, o_ref, acc = refs
        else:
            dy_ref, w_ref, o_ref, acc = refs
        p = lax.dot_general(dy_ref[:, 0:c], w_ref[0], dims, preferred_element_type=F32)
        for t in range(1, nb):
            p = p + lax.dot_general(dy_ref[:, t * c:(t + 1) * c], w_ref[t], dims, preferred_element_type=F32)
        k = pl.program_id(2)

        @pl.when(k == 0)
        def _():
            acc[...] = p

        @pl.when(k > 0)
        def _():
            acc[...] += p

        @pl.when(k == nk - 1)
        def _():
            r = acc[...]
            if has_add:
                r = r + add_ref[...]
            o_ref[...] = r.astype(out_dtype)

    o_spec = pl.BlockSpec((bm, bn), lambda i, j, k: (i, j))
    dy_spec = (pl.BlockSpec((bm, nb * c), lambda i, j, k: (i, k)) if lead is None
               else pl.BlockSpec((None, bm, nb * c), lambda i, j, k: (lead, i, k)))
    in_specs = [dy_spec, pl.BlockSpec((nb, bn, c), lambda i, j, k: (k, j, 0))]
    in_specs += [o_spec] if has_add else []
    args = (dy, w8) + ((add,) if has_add else ())
    body, in_specs, args = _ordered_after(body, len(args), in_specs, args, after)
    return pl.pallas_call(
        body, name=name, grid=(M // bm, K // bn, nk),
        in_specs=in_specs, out_specs=o_spec,
        out_shape=jax.ShapeDtypeStruct((M, K), out_dtype),
        scratch_shapes=[pltpu.VMEM((bm, bn), F32)],
        compiler_params=_cparams("parallel", "parallel", "arbitrary"),
    )(*args)


def _mm_gw8(x, dy, c, *, name, bk=1024, lead=None):
    T, K = x.shape
    nb = _blocks_per_tile(c)
    bk = _div(K, bk)
    dims = (((0,), (0,)), ((), ()))

    def body(x_ref, dy_ref, o_ref):
        x_ = x_ref[...]
        for t in range(nb):
            o_ref[t] = lax.dot_general(x_, dy_ref[:, t * c:(t + 1) * c], dims, preferred_element_type=F32).astype(BF16)

    dy_spec = (pl.BlockSpec((T, nb * c), lambda i, j: (0, j)) if lead is None
               else pl.BlockSpec((None, T, nb * c), lambda i, j: (lead, 0, j)))
    return pl.pallas_call(
        body, name=name, grid=(K // bk, N_DEV // nb),
        in_specs=[pl.BlockSpec((T, bk), lambda i, j: (0, i)), dy_spec],
        out_specs=pl.BlockSpec((nb, bk, c), lambda i, j: (j, i, 0)),
        out_shape=jax.ShapeDtypeStruct((N_DEV, K, c), BF16),
        compiler_params=_cparams("parallel", "parallel"),
    )(x, dy)


def _rms_fwd(x, g, *, name):
    T, D = x.shape
    tm = _div(T, 256, 8)

    def body(x_ref, g_ref, h_ref):
        xf = x_ref[...]
        r = lax.rsqrt(jnp.mean(xf * xf, axis=-1, keepdims=True) + EPS)
        h_ref[...] = ((xf * r) * g_ref[...]).astype(BF16)

    return pl.pallas_call(
        body, name=name, grid=(T // tm,),
        in_specs=[pl.BlockSpec((tm, D), lambda i: (i, 0)), pl.BlockSpec((1, D), lambda i: (0, 0))],
        out_specs=pl.BlockSpec((tm, D), lambda i: (i, 0)),
        out_shape=jax.ShapeDtypeStruct((T, D), BF16),
        compiler_params=_cparams("parallel"),
    )(x, g)


def _rms_bwd(x, g, dh, dres, *, name, want_bf16, after=None):
    T, D = x.shape
    tm = _div(T, 256, 8)

    def body(x_ref, g_ref, dh_ref, dres_ref, dx_ref, *rest):
        if want_bf16:
            dxb_ref, dg_ref = rest
        else:
            (dg_ref,) = rest
        xf = x_ref[...]
        r = lax.rsqrt(jnp.mean(xf * xf, axis=-1, keepdims=True) + EPS)
        xhat = xf * r
        dh_ = dh_ref[...]
        dy = dh_ * g_ref[...]
        dx = dres_ref[...] + r * (dy - xhat * jnp.mean(dy * xhat, axis=-1, keepdims=True))
        dx_ref[...] = dx
        if want_bf16:
            dxb_ref[...] = dx.astype(BF16)
        part = jnp.sum(dh_ * xhat, axis=0, keepdims=True)

        @pl.when(pl.program_id(0) == 0)
        def _():
            dg_ref[...] = part

        @pl.when(pl.program_id(0) > 0)
        def _():
            dg_ref[...] += part

    row = pl.BlockSpec((tm, D), lambda i: (i, 0))
    vec = pl.BlockSpec((1, D), lambda i: (0, 0))
    out_specs = [row] + ([row] if want_bf16 else []) + [vec]
    out_shape = ([jax.ShapeDtypeStruct((T, D), F32)] + ([jax.ShapeDtypeStruct((T, D), BF16)] if want_bf16 else [])
                 + [jax.ShapeDtypeStruct((1, D), F32)])
    body, in_specs, args = _ordered_after(body, 4, [row, vec, row, row], (x, g, dh, dres), after)
    return pl.pallas_call(
        body, name=name, grid=(T // tm,),
        in_specs=in_specs, out_specs=out_specs, out_shape=out_shape,
        compiler_params=_cparams("arbitrary"),
    )(*args)


def _loss_head(x, g, target, *, name):
    T, D = x.shape
    tm = _div(T, 256, 8)

    def body(x_ref, g_ref, t_ref, loss_ref, dx_ref, dxb_ref, dg_ref):
        xf = x_ref[...]
        r = lax.rsqrt(jnp.mean(xf * xf, axis=-1, keepdims=True) + EPS)
        xhat = xf * r
        gain = g_ref[...]
        err = xhat * gain - t_ref[...]
        lpart = 0.5 * jnp.sum(jnp.mean(err * err, axis=-1, keepdims=True), axis=0, keepdims=True)
        dh_ = err * (1.0 / D)
        dy = dh_ * gain
        dx = r * (dy - xhat * jnp.mean(dy * xhat, axis=-1, keepdims=True))
        dx_ref[...] = dx
        dxb_ref[...] = dx.astype(BF16)
        part = jnp.sum(dh_ * xhat, axis=0, keepdims=True)

        @pl.when(pl.program_id(0) == 0)
        def _():
            dg_ref[...] = part
            loss_ref[...] = jnp.broadcast_to(lpart, loss_ref.shape)

        @pl.when(pl.program_id(0) > 0)
        def _():
            dg_ref[...] += part
            loss_ref[...] += jnp.broadcast_to(lpart, loss_ref.shape)

    row = pl.BlockSpec((tm, D), lambda i: (i, 0))
    vec = pl.BlockSpec((1, D), lambda i: (0, 0))
    return pl.pallas_call(
        body, name=name, grid=(T // tm,),
        in_specs=[row, vec, row],
        out_specs=[pl.BlockSpec((8, LANES), lambda i: (0, 0)), row, row, vec],
        out_shape=[jax.ShapeDtypeStruct((8, LANES), F32), jax.ShapeDtypeStruct((T, D), F32),
                   jax.ShapeDtypeStruct((T, D), BF16), jax.ShapeDtypeStruct((1, D), F32)],
        compiler_params=_cparams("arbitrary"),
    )(x, g, target)


def _gate_cols(D):
    off_a = 3 * D // 2 + 2 * KV_WIDTH
    off_b = off_a + D
    cw = math.gcd(math.gcd(off_a, off_b), math.gcd(D, 512))
    return cw, off_a // cw, off_b // cw


def _merge_fwd(z, ya, yb, *, name):
    T, D = ya.shape
    cw, ba, bb = _gate_cols(D)
    tm = _div(T, 512, 8)

    def body(ga_ref, gb_ref, ya_ref, yb_ref, m_ref):
        m_ref[...] = (_sigmoid(ga_ref[...]) * ya_ref[...] + _sigmoid(gb_ref[...]) * yb_ref[...]).astype(BF16)

    blk = pl.BlockSpec((tm, cw), lambda i, j: (i, j))
    return pl.pallas_call(
        body, name=name, grid=(T // tm, D // cw),
        in_specs=[pl.BlockSpec((tm, cw), lambda i, j: (i, ba + j)), pl.BlockSpec((tm, cw), lambda i, j: (i, bb + j)), blk, blk],
        out_specs=blk, out_shape=jax.ShapeDtypeStruct((T, D), BF16),
        compiler_params=_cparams("parallel", "parallel"),
    )(z, z, ya, yb)


def _merge_bwd(z, ya, yb, dm, *, name, after=None):
    T, D = ya.shape
    cw, ba, bb = _gate_cols(D)
    nj = D // cw
    assert bb == ba + nj
    tm = _div(T, 512, 8)

    def body(g_ref, ya_ref, yb_ref, dm_ref, dy_ref, dz_ref):
        sig = _sigmoid(g_ref[...])
        dm_ = dm_ref[...]
        y = jnp.where(pl.program_id(1) == 0, ya_ref[...], yb_ref[...])
        dy_ref[...] = (dm_ * sig).astype(BF16)
        dz_ref[...] = (dm_ * y * (sig * (1.0 - sig))).astype(BF16)

    in_specs = [pl.BlockSpec((tm, cw), lambda i, s, j: (i, ba + s * nj + j)),
                pl.BlockSpec((tm, cw), lambda i, s, j: (i, j * (1 - s))),
                pl.BlockSpec((tm, cw), lambda i, s, j: (i, j * s)),
                pl.BlockSpec((tm, cw), lambda i, s, j: (i, j))]
    body, in_specs, args = _ordered_after(body, 4, in_specs, (z, ya, yb, dm), after)
    return pl.pallas_call(
        body, name=name, grid=(T // tm, 2, nj),
        in_specs=in_specs,
        out_specs=[pl.BlockSpec((None, tm, cw), lambda i, s, j: (s, i, j)),
                   pl.BlockSpec((tm, cw), lambda i, s, j: (i, ba + s * nj + j))],
        out_shape=[jax.ShapeDtypeStruct((2, T, D), BF16), jax.ShapeDtypeStruct(z.shape, BF16)],
        compiler_params=_cparams("parallel", "arbitrary", "arbitrary"),
    )(*args)


def _swiglu_fwd(gate, up, *, name):
    T, F = gate.shape
    tm, cw = _div(T, 512, 8), _div(F, 512)

    def body(g_ref, u_ref, act_ref):
        g = g_ref[...]
        act_ref[...] = (g * _sigmoid(g) * u_ref[...]).astype(BF16)

    blk = pl.BlockSpec((tm, cw), lambda i, j: (i, j))
    return pl.pallas_call(
        body, name=name, grid=(T // tm, F // cw), in_specs=[blk, blk], out_specs=blk,
        out_shape=jax.ShapeDtypeStruct((T, F), BF16), compiler_params=_cparams("parallel", "parallel"),
    )(gate, up)


def _swiglu_bwd(gate, up, dact, *, name, after=None):
    T, F = gate.shape
    tm, cw = _div(T, 512, 8), _div(F, 512)

    def body(g_ref, u_ref, d_ref, dg_ref, du_ref):
        g = g_ref[...]
        s = _sigmoid(g)
        d = d_ref[...]
        silu = g * s
        dg_ref[...] = (d * u_ref[...] * (s + silu * (1.0 - s))).astype(BF16)
        du_ref[...] = (d * silu).astype(BF16)

    blk = pl.BlockSpec((tm, cw), lambda i, j: (i, j))
    out = jax.ShapeDtypeStruct((T, F), BF16)
    body, in_specs, args = _ordered_after(body, 3, [blk, blk, blk], (gate, up, dact), after)
    return pl.pallas_call(
        body, name=name, grid=(T // tm, F // cw), in_specs=in_specs, out_specs=[blk, blk],
        out_shape=[out, out], compiler_params=_cparams("parallel", "parallel"),
    )(*args)


def _sgu_fwd(z, gain, ws_b, bs_t, *, name):
    T = z.shape[0]
    SW = gain.shape[1]
    G = SW // BLOCK

    def body(zu_ref, zv_ref, gain_ref, ws_ref, bs_ref, a_ref):
        u = _gelu(zu_ref[...])
        vg = _gelu(zv_ref[...])
        r = lax.rsqrt(jnp.mean(vg * vg, axis=-1, keepdims=True) + EPS)
        vn = ((vg * r) * gain_ref[...]).astype(BF16)
        for g in range(G):
            sl = slice(g * BLOCK, (g + 1) * BLOCK)
            mixed = jnp.dot(ws_ref[g], vn[:, sl], preferred_element_type=F32) + bs_ref[:, g:g + 1]
            a_ref[:, sl] = (u[:, sl] * mixed).astype(BF16)

    return pl.pallas_call(
        body, name=name, grid=(T // BLOCK,),
        in_specs=[pl.BlockSpec((BLOCK, SW), lambda c: (c, 0)), pl.BlockSpec((BLOCK, SW), lambda c: (c, 1)),
                  pl.BlockSpec((1, SW), lambda c: (0, 0)), pl.BlockSpec((G, BLOCK, BLOCK), lambda c: (0, 0, 0)),
                  pl.BlockSpec((BLOCK, G), lambda c: (0, 0))],
        out_specs=pl.BlockSpec((BLOCK, SW), lambda c: (c, 0)),
        out_shape=jax.ShapeDtypeStruct((T, SW), BF16),
        compiler_params=_cparams("parallel"),
    )(z, z, gain, ws_b, bs_t)


def _sgu_bwd(z, gain, ws_b, bs_t, da, dz, *, name):
    T = z.shape[0]
    SW = gain.shape[1]
    G = SW // BLOCK

    def body(zu_ref, zv_ref, gain_ref, ws_ref, bs_ref, da_ref, dz_in_ref, dz_ref, dws_ref, dbs_ref, dgain_ref, dvn_ref):
        first = pl.program_id(0) == 0

        @pl.when(first)
        def _():
            dws_ref[...] = jnp.zeros_like(dws_ref)
            dbs_ref[...] = jnp.zeros_like(dbs_ref)
            dgain_ref[...] = jnp.zeros_like(dgain_ref)

        u, du = _gelu_and_grad(zu_ref[...])
        vg, dvg = _gelu_and_grad(zv_ref[...])
        r = lax.rsqrt(jnp.mean(vg * vg, axis=-1, keepdims=True) + EPS)
        xhat = vg * r
        gain_ = gain_ref[...]
        vn = (xhat * gain_).astype(BF16)
        da_ = da_ref[...]
        for g in range(G):
            sl = slice(g * BLOCK, (g + 1) * BLOCK)
            w = ws_ref[g]
            mixed = jnp.dot(w, vn[:, sl], preferred_element_type=F32) + bs_ref[:, g:g + 1]
            dmix = da_[:, sl] * u[:, sl]
            dz_ref[:, sl] = (da_[:, sl] * mixed * du[:, sl]).astype(BF16)
            dmb = dmix.astype(BF16)
            dws_ref[g] += lax.dot_general(dmb, vn[:, sl], (((1,), (1,)), ((), ())), preferred_element_type=F32)
            dbs_ref[:, g:g + 1] += jnp.sum(dmix, axis=-1, keepdims=True)
            dvn_ref[:, sl] = lax.dot_general(w, dmb, (((0,), (0,)), ((), ())), preferred_element_type=F32)
        dvn = dvn_ref[...]
        dgain_ref[...] += jnp.sum(dvn * xhat, axis=0, keepdims=True)
        dy = dvn * gain_
        dv_ = r * (dy - xhat * jnp.mean(dy * xhat, axis=-1, keepdims=True))
        dz_ref[:, SW:] = (dv_ * dvg).astype(BF16)

    row = pl.BlockSpec((BLOCK, SW), lambda c: (c, 0))
    return pl.pallas_call(
        body, name=name, grid=(T // BLOCK,),
        in_specs=[row, pl.BlockSpec((BLOCK, SW), lambda c: (c, 1)),
                  pl.BlockSpec((1, SW), lambda c: (0, 0)), pl.BlockSpec((G, BLOCK, BLOCK), lambda c: (0, 0, 0)),
                  pl.BlockSpec((BLOCK, G), lambda c: (0, 0)), row, _ANY],
        out_specs=[pl.BlockSpec((BLOCK, 2 * SW), lambda c: (c, 0)), pl.BlockSpec((G, BLOCK, BLOCK), lambda c: (0, 0, 0)),
                   pl.BlockSpec((BLOCK, G), lambda c: (0, 0)), pl.BlockSpec((1, SW), lambda c: (0, 0))],
        out_shape=[jax.ShapeDtypeStruct(dz.shape, dz.dtype),
                   jax.ShapeDtypeStruct((G, BLOCK, BLOCK), F32), jax.ShapeDtypeStruct((BLOCK, G), F32),
                   jax.ShapeDtypeStruct((1, SW), F32)],
        input_output_aliases={6: 0},
        scratch_shapes=[pltpu.VMEM((BLOCK, SW), F32)],
        compiler_params=_cparams("arbitrary"),
    )(z, z, gain, ws_b, bs_t, da, dz)


def _bias_table(rel_bias, bmap, *, name):
    H = rel_bias.shape[1]

    def body(rb_ref, bmap_ref, o_ref):
        bm_ = bmap_ref[...]
        for h in range(H):
            acc = jnp.zeros(bm_.shape, F32)
            for b in range(REL_BUCKETS):
                acc = jnp.where(bm_ == b, rb_ref[b, h], acc)
            o_ref[h] = acc

    return pl.pallas_call(
        body, name=name,
        in_specs=[pl.BlockSpec(memory_space=pltpu.SMEM), pl.BlockSpec(memory_space=pltpu.VMEM)],
        out_specs=pl.BlockSpec(memory_space=pltpu.VMEM),
        out_shape=jax.ShapeDtypeStruct((H, BLOCK, 3 * BLOCK), F32),
    )(rel_bias, bmap)


def _attn_probs(q_ref, kb, bias_ref, sink_ref, valid, h, group):
    kv = h // group
    qh = q_ref[:, h * HEAD_DIM:(h + 1) * HEAD_DIM].astype(BF16)
    s = lax.dot_general(qh, kb[:, kv * HEAD_DIM:(kv + 1) * HEAD_DIM], (((1,), (1,)), ((), ())),
                        preferred_element_type=F32)
    s = s * (HEAD_DIM ** -0.5) + bias_ref[h]
    s = jnp.where(valid, s, NEG)
    sink = sink_ref[0:1, h:h + 1]
    m = jnp.maximum(jnp.max(s, axis=-1, keepdims=True), sink)
    e = jnp.exp(s - m)
    es = jnp.exp(sink - m)
    inv = 1.0 / (jnp.sum(e, axis=-1, keepdims=True) + es)
    return e * inv, es * inv, qh


def _band_valid(n, T):
    row = lax.broadcasted_iota(jnp.int32, (BLOCK, 3 * BLOCK), 0)
    col = lax.broadcasted_iota(jnp.int32, (BLOCK, 3 * BLOCK), 1)
    rel = col - BLOCK - row
    key_pos = n * BLOCK + col - BLOCK
    return (jnp.abs(rel) <= BLOCK) & (key_pos >= 0) & (key_pos < T)


def _attn_fwd(z, kpad, vpad, bias, sink, *, name):
    T = z.shape[0]
    H = bias.shape[0]
    AW = H * HEAD_DIM
    group = H // N_KV_HEADS

    def body(q_ref, k_ref, v_ref, bias_ref, sink_ref, o_ref):
        n = pl.program_id(0)
        start = pl.multiple_of(n * BLOCK, BLOCK)
        kb = k_ref[pl.ds(start, 3 * BLOCK), :]
        vb = v_ref[pl.ds(start, 3 * BLOCK), :]
        valid = _band_valid(n, T)
        for h in range(H):
            kv = h // group
            p, _, _ = _attn_probs(q_ref, kb, bias_ref, sink_ref, valid, h, group)
            o = jnp.dot(p.astype(BF16), vb[:, kv * HEAD_DIM:(kv + 1) * HEAD_DIM], preferred_element_type=F32)
            o_ref[:, h * HEAD_DIM:(h + 1) * HEAD_DIM] = o.astype(BF16)

    full_kv = pl.BlockSpec((T + 2 * BLOCK, KV_WIDTH), lambda n: (0, 0))
    return pl.pallas_call(
        body, name=name, grid=(T // BLOCK,),
        in_specs=[pl.BlockSpec((BLOCK, AW), lambda n: (n, 2)), full_kv, full_kv,
                  pl.BlockSpec((H, BLOCK, 3 * BLOCK), lambda n: (0, 0, 0)), pl.BlockSpec((1, H), lambda n: (0, 0))],
        out_specs=pl.BlockSpec((BLOCK, AW), lambda n: (n, 0)),
        out_shape=jax.ShapeDtypeStruct((T, AW), BF16),
        compiler_params=_cparams("parallel"),
    )(z, kpad, vpad, bias, sink)


def _attn_bwd(z, kpad, vpad, bias, sink, do, dz, *, name):
    T = z.shape[0]
    H = bias.shape[0]
    AW = H * HEAD_DIM
    group = H // N_KV_HEADS
    scale = HEAD_DIM ** -0.5

    def body(q_ref, k_ref, v_ref, bias_ref, sink_ref, do_ref, dz_in_ref, dq_ref, dk_ref, dv_ref, dbias_ref, dsink_ref):
        n = pl.program_id(0)

        @pl.when(n == 0)
        def _():
            dk_ref[...] = jnp.zeros_like(dk_ref)
            dv_ref[...] = jnp.zeros_like(dv_ref)
            dbias_ref[...] = jnp.zeros_like(dbias_ref)
            dsink_ref[...] = jnp.zeros_like(dsink_ref)

        start = pl.multiple_of(n * BLOCK, BLOCK)
        kb = k_ref[pl.ds(start, 3 * BLOCK), :]
        vb = v_ref[pl.ds(start, 3 * BLOCK), :]
        valid = _band_valid(n, T)
        for kv in range(N_KV_HEADS):
            ksl = slice(kv * HEAD_DIM, (kv + 1) * HEAD_DIM)
            dk_acc = jnp.zeros((3 * BLOCK, HEAD_DIM), F32)
            dv_acc = jnp.zeros((3 * BLOCK, HEAD_DIM), F32)
            for gi in range(group):
                h = kv * group + gi
                hsl = slice(h * HEAD_DIM, (h + 1) * HEAD_DIM)
                p, p_sink, qh = _attn_probs(q_ref, kb, bias_ref, sink_ref, valid, h, group)
                doh = do_ref[:, hsl]
                dp = lax.dot_general(doh, vb[:, ksl], (((1,), (1,)), ((), ())), preferred_element_type=F32)
                delta = jnp.sum(p * dp, axis=-1, keepdims=True)
                ds = p * (dp - delta)
                dbias_ref[h] += ds
                dsink_ref[:, h:h + 1] += -(p_sink * delta)
                dsb = ds.astype(BF16)
                dq = jnp.dot(dsb, kb[:, ksl], preferred_element_type=F32) * scale
                dq_ref[:, hsl] = dq.astype(BF16)
                dk_acc = dk_acc + lax.dot_general(dsb, qh, (((0,), (0,)), ((), ())), preferred_element_type=F32)
                dv_acc = dv_acc + lax.dot_general(p.astype(BF16), doh, (((0,), (0,)), ((), ())),
                                                  preferred_element_type=F32)
            dk_ref[pl.ds(start, 3 * BLOCK), ksl] += dk_acc * scale
            dv_ref[pl.ds(start, 3 * BLOCK), ksl] += dv_acc

    full_kv = pl.BlockSpec((T + 2 * BLOCK, KV_WIDTH), lambda n: (0, 0))
    bias_spec = pl.BlockSpec((H, BLOCK, 3 * BLOCK), lambda n: (0, 0, 0))
    row = pl.BlockSpec((BLOCK, AW), lambda n: (n, 0))
    q_cols = pl.BlockSpec((BLOCK, AW), lambda n: (n, 2))
    return pl.pallas_call(
        body, name=name, grid=(T // BLOCK,),
        in_specs=[q_cols, full_kv, full_kv, bias_spec, pl.BlockSpec((1, H), lambda n: (0, 0)), row, _ANY],
        out_specs=[q_cols, full_kv, full_kv, bias_spec, pl.BlockSpec((BLOCK, H), lambda n: (0, 0))],
        out_shape=[jax.ShapeDtypeStruct(dz.shape, dz.dtype),
                   jax.ShapeDtypeStruct((T + 2 * BLOCK, KV_WIDTH), F32), jax.ShapeDtypeStruct((T + 2 * BLOCK, KV_WIDTH), F32),
                   jax.ShapeDtypeStruct((H, BLOCK, 3 * BLOCK), F32), jax.ShapeDtypeStruct((BLOCK, H), F32)],
        input_output_aliases={6: 0},
        compiler_params=_cparams("arbitrary"),
    )(z, kpad, vpad, bias, sink, do, dz)


def _dkv_into(dkp, dvp, dz, *, name):
    T = dz.shape[0]
    D = (dz.shape[1] - 2 * KV_WIDTH) * 2 // 7
    col = (D + D // 2) // (2 * KV_WIDTH)
    assert col * 2 * KV_WIDTH == D + D // 2

    def body(dk_ref, dv_ref, dz_in_ref, o_ref):
        o_ref[:, :KV_WIDTH] = dk_ref[...].astype(BF16)
        o_ref[:, KV_WIDTH:] = dv_ref[...].astype(BF16)

    kv = pl.BlockSpec((BLOCK, KV_WIDTH), lambda n: (n + 1, 0))
    return pl.pallas_call(
        body, name=name, grid=(T // BLOCK,),
        in_specs=[kv, kv, _ANY], out_specs=pl.BlockSpec((BLOCK, 2 * KV_WIDTH), lambda n: (n, col)),
        out_shape=jax.ShapeDtypeStruct(dz.shape, dz.dtype), input_output_aliases={2: 0},
        compiler_params=_cparams("parallel"),
    )(dkp, dvp, dz)


def _kv_pad(z, *, name):
    T = z.shape[0]
    D = (z.shape[1] - 2 * KV_WIDTH) * 2 // 7
    kcol = (D + D // 2) // KV_WIDTH
    nb = T // BLOCK

    def body(k_ref, v_ref, ko_ref, vo_ref):
        b = pl.program_id(0)
        inside = (b >= 1) & (b <= nb)
        ko_ref[...] = jnp.where(inside, k_ref[...], 0.0).astype(BF16)
        vo_ref[...] = jnp.where(inside, v_ref[...], 0.0).astype(BF16)

    out = jax.ShapeDtypeStruct((T + 2 * BLOCK, KV_WIDTH), BF16)
    o_spec = pl.BlockSpec((BLOCK, KV_WIDTH), lambda b: (b, 0))
    return pl.pallas_call(
        body, name=name, grid=(nb + 2,),
        in_specs=[pl.BlockSpec((BLOCK, KV_WIDTH), lambda b: (jnp.clip(b - 1, 0, nb - 1), kcol)),
                  pl.BlockSpec((BLOCK, KV_WIDTH), lambda b: (jnp.clip(b - 1, 0, nb - 1), kcol + 1))],
        out_specs=[o_spec, o_spec], out_shape=[out, out],
        compiler_params=_cparams("parallel"),
    )(z, z)


def _attn_small_grads(dbias, dsink_rows, bmap, *, name):
    H = dbias.shape[0]

    def body(dbias_ref, dsink_ref, bmap_ref, drel_ref, ds_ref):
        bm_ = bmap_ref[...]
        for h in range(H):
            d = dbias_ref[h]
            for b in range(REL_BUCKETS):
                drel_ref[b, h] = jnp.sum(jnp.where(bm_ == b, d, 0.0))
            ds_ref[0, h] = jnp.sum(dsink_ref[:, h:h + 1])

    vmem = pl.BlockSpec(memory_space=pltpu.VMEM)
    smem = pl.BlockSpec(memory_space=pltpu.SMEM)
    return pl.pallas_call(
        body, name=name, in_specs=[vmem, vmem, vmem], out_specs=[smem, smem],
        out_shape=[jax.ShapeDtypeStruct((REL_BUCKETS, H), F32), jax.ShapeDtypeStruct((1, H), F32)],
    )(dbias, dsink_rows, bmap)


def _local_step(x, target, weight, emit, flush, norm_mix, v_gain, w_s, b_s, sink, rel_bias, norm_ffn, norm_final):
    T, D = x.shape
    ws_b = w_s.astype(BF16)
    bs_t = b_s.T
    bmap = jnp.asarray(_bucket_map())

    h = _rms_fwd(x, norm_mix, name="rms_mix")
    w_in = weight("w_in", h)
    z = _mm(h, w_in, tb=True, name="mm_z", bm=2048, bn=768)
    a = _sgu_fwd(z, v_gain, ws_b, bs_t, name="sgu_fwd")
    w_a = weight("w_a_out", a)
    ya = _mm_w8(a, w_a, name="mm_ya", bm=2048)
    kpad, vpad = _kv_pad(z, name="kv_pad")
    bias = _bias_table(rel_bias, bmap, name="bias_table")
    o = _attn_fwd(z, kpad, vpad, bias, sink, name="attn_fwd")
    w_b = weight("w_b_out", o)
    yb = _mm_w8(o, w_b, name="mm_yb", bm=2048)
    m = _merge_fwd(z, ya, yb, name="merge_fwd")
    w_o = weight("w_o", m)
    x1 = _mm(m, w_o, name="mm_x1", add=x, bm=2048, bn=512)
    h2 = _rms_fwd(x1, norm_ffn, name="rms_ffn")
    w_gate = weight("w_gate", h2)
    w_up = weight("w_up", h2)
    gate = _mm(h2, w_gate, tb=True, name="mm_gate", bm=2048, bn=512)
    up = _mm(h2, w_up, tb=True, name="mm_up", bm=2048, bn=512)
    act = _swiglu_fwd(gate, up, name="swiglu_fwd")
    w_down = weight("w_down", act)
    x2 = _mm(act, w_down, name="mm_x2", add=x1, bm=1024, bn=1024, bk=2816)
    loss, dx2, dx2b, g_norm_final = _loss_head(x2, norm_final, target, name="loss_head")

    g_w_down = _mm(act, dx2b, ta=True, out_dtype=BF16, name="mm_gwdown", bm=512, bn=2048)
    tok = emit(("w_down",), (g_w_down,))
    dact = _mm(dx2b, w_down, tb=True, name="mm_dact", bm=2048, bn=512, after=tok)
    tok = flush(dact)
    dgate, dup = _swiglu_bwd(gate, up, dact, name="swiglu_bwd", after=tok)
    g_w_gate = _mm(dgate, h2, ta=True, out_dtype=BF16, name="mm_gwgate", bm=512, bn=2048)
    g_w_up = _mm(dup, h2, ta=True, out_dtype=BF16, name="mm_gwup", bm=512, bn=2048)
    tok = emit(("w_gate", "w_up"), (g_w_gate, g_w_up))
    dh2 = _mm(dgate, w_gate, name="mm_dh2a", bm=1024, bn=1024, bk=2816, after=tok)
    tok = flush(dh2)
    dh2 = _mm(dup, w_up, add=dh2, name="mm_dh2b", bm=1024, bn=1024, bk=2816, after=tok)
    dx1, dx1b, g_norm_ffn = _rms_bwd(x1, norm_ffn, dh2, dx2, name="rms_ffn_bwd", want_bf16=True)

    g_w_o = _mm(m, dx1b, ta=True, out_dtype=BF16, name="mm_gwo", bm=2048, bn=512)
    tok = emit(("w_o",), (g_w_o,))
    dm = _mm(dx1b, w_o, tb=True, name="mm_dm", bm=2048, bn=512, after=tok)
    tok = flush(dm)
    dy, dz = _merge_bwd(z, ya, yb, dm, name="merge_bwd", after=tok)
    g_w_a = _mm_gw8(a, dy, w_a.shape[2], name="mm_gwa", lead=0)
    g_w_b = _mm_gw8(o, dy, w_b.shape[2], name="mm_gwb", lead=1)
    tok = emit(("w_a_out", "w_b_out"), (g_w_a, g_w_b))
    da = _mm_w8t(dy, w_a, name="mm_da", bm=2048, bn=512, after=tok, lead=0)
    tok = flush(da)
    do = _mm_w8t(dy, w_b, out_dtype=BF16, name="mm_do", bm=2048, bn=512, after=tok, lead=1)
    dz, g_w_s, g_b_s_t, g_v_gain = _sgu_bwd(z, v_gain, ws_b, bs_t, da, dz, name="sgu_bwd")
    dz, dkp, dvp, dbias, dsink_rows = _attn_bwd(z, kpad, vpad, bias, sink, do, dz, name="attn_bwd")
    dz = _dkv_into(dkp, dvp, dz, name="dkv_into_dz")
    g_rel_bias, g_sink = _attn_small_grads(dbias, dsink_rows, bmap, name="attn_small_grads")
    g_w_in = _mm(dz, h, ta=True, out_dtype=BF16, name="mm_gwin", bm=768, bn=2048)
    tok = emit(("w_in",), (g_w_in,))
    dh = _mm(dz, w_in, name="mm_dh", bm=1024, bn=1024, bk=2560, after=tok)
    tok = flush(dh)
    grad_x, g_norm_mix = _rms_bwd(x, norm_mix, dh, dx1, name="rms_mix_bwd", want_bf16=False, after=tok)

    small = dict(norm_mix=g_norm_mix, sgu_v_gain=g_v_gain, sgu_w_s=g_w_s, sgu_b_s=g_b_s_t.T, attn_sink=g_sink,
                 rel_bias=g_rel_bias, norm_ffn=g_norm_ffn, norm_final=g_norm_final)
    return loss, grad_x, small


def _position():
    return lax.axis_index("x"), lax.axis_index("y"), lax.axis_index("c")


def _other_chips(x, y):
    return [(1 - x, y), (x, 1 - y), (1 - x, 1 - y)]


def _slot(px, py, pc):
    return 4 * px + 2 * py + pc


_HBM = pl.BlockSpec(memory_space=pltpu.HBM)
_SEM = pl.BlockSpec(memory_space=pltpu.SEMAPHORE)
_DATAFLOW = pltpu.SideEffectType.DATAFLOW_SIDE_EFFECTING


def _in_hbm(a):
    return pltpu.with_memory_space_constraint(a, pltpu.HBM)


def _own_slot(shard, pos, *, name, after=None):
    R, C = shard.shape
    tr = _div(R, 256, 16)

    def body(pos_ref, w_ref, o_ref):
        o_ref[...] = w_ref[...].astype(BF16)

    body, in_specs, args = _ordered_after(body, 2, [pl.BlockSpec((tr, C), lambda i, pos_ref: (i, 0))], (pos, shard), after)
    grid_spec = pltpu.PrefetchScalarGridSpec(
        num_scalar_prefetch=1, grid=(R // tr,), in_specs=in_specs,
        out_specs=pl.BlockSpec((None, tr, C), lambda i, pos_ref: (pos_ref[0], i, 0)))
    return pl.pallas_call(
        body, name=name, grid_spec=grid_spec,
        out_shape=jax.ShapeDtypeStruct((N_DEV, R, C), BF16),
        compiler_params=_cparams("parallel"),
    )(*args)


def _ag_copies(w, land_ref, send_sems, recv_sems):
    x, y, c = _position()
    mine = land_ref.at[_slot(x, y, c)]
    targets = [(px, py, c) for px, py in _other_chips(x, y)] + [(x, y, 1 - c)]
    return [pltpu.make_async_remote_copy(src_ref=mine, dst_ref=mine, send_sem=send_sems.at[4 * w + k],
                                         recv_sem=recv_sems.at[4 * w + k], device_id=to, device_id_type=MESH)
            for k, to in enumerate(targets)]


def _ag_start(buffers, groups, *, name):
    lands = [buffers[i] for g in groups for i in g]
    n, ng = len(lands), len(groups)
    sizes = [len(g) for g in groups]

    def body(*refs):
        land_refs = refs[:n]
        sems = refs[n:n + 2 * ng]
        token = refs[-1]
        i = 0
        for g in range(ng):
            for w in range(sizes[g]):
                for cp in _ag_copies(w, land_refs[i], sems[2 * g], sems[2 * g + 1]):
                    cp.start()
                i += 1
        token[...] = jnp.zeros_like(token)

    sem_shapes = [pltpu.SemaphoreType.DMA((4 * k,)) for k in sizes for _ in range(2)]
    outs = pl.pallas_call(
        body, name=name,
        in_specs=[_HBM] * n,
        out_specs=tuple([_SEM] * (2 * ng) + [_HBM] * n + [pl.BlockSpec(memory_space=pltpu.VMEM)]),
        out_shape=tuple(sem_shapes + [pltpu.HBM(a.shape, a.dtype) for a in lands] + [jax.ShapeDtypeStruct((8, LANES), F32)]),
        input_output_aliases={i: 2 * ng + i for i in range(n)},
        compiler_params=pltpu.CompilerParams(has_side_effects=_DATAFLOW),
    )(*[_in_hbm(a) for a in lands])
    sems, thru = outs[:2 * ng], outs[2 * ng:2 * ng + n]
    result, i = [], 0
    for g in range(ng):
        k = sizes[g]
        result.append((sems[2 * g], sems[2 * g + 1], list(thru[i:i + k])))
        i += k
    return result, outs[-1]


def _ag_wait(send_sems, recv_sems, lands, after, *, name):
    n = len(lands)

    def body(*refs):
        land_refs = refs[:n]
        send_ref, recv_ref = refs[n], refs[n + 1]
        token = refs[-1]
        for w in range(n):
            for cp in _ag_copies(w, land_refs[w], send_ref, recv_ref):
                cp.wait_send()
                cp.wait_recv()
        token[...] = jnp.zeros_like(token)

    outs = pl.pallas_call(
        body, name=name,
        in_specs=[_HBM] * n + [_SEM, _SEM, _ANY],
        out_specs=tuple([_HBM] * n + [pl.BlockSpec(memory_space=pltpu.VMEM)]),
        out_shape=tuple([pltpu.HBM(a.shape, a.dtype) for a in lands] + [jax.ShapeDtypeStruct((8, LANES), F32)]),
        input_output_aliases={i: i for i in range(n)},
        compiler_params=pltpu.CompilerParams(has_side_effects=_DATAFLOW),
    )(*lands, send_sems, recv_sems, after)
    return list(outs[:n]), outs[n]


def _ag_forward(lands, *, name):
    n = len(lands)

    def body(*refs):
        in_refs, out_refs = refs[:n], refs[n:2 * n]
        send_sems, recv_sems = refs[2 * n:]
        x, y, c = _position()
        copies = []
        for w in range(n):
            for k, (px, py) in enumerate(_other_chips(x, y)):
                cp = pltpu.make_async_remote_copy(
                    src_ref=in_refs[w].at[_slot(px, py, c)], dst_ref=out_refs[w].at[_slot(px, py, c)],
                    send_sem=send_sems.at[3 * w + k], recv_sem=recv_sems.at[3 * w + k],
                    device_id=(x, y, 1 - c), device_id_type=MESH)
                cp.start()
                copies.append(cp)
        for cp in copies:
            cp.wait()

    return pl.pallas_call(
        body, name=name,
        in_specs=[_ANY] * n, out_specs=[_ANY] * n,
        out_shape=[jax.ShapeDtypeStruct(a.shape, a.dtype) for a in lands],
        input_output_aliases={i: i for i in range(n)},
        scratch_shapes=[pltpu.SemaphoreType.DMA((3 * n,)), pltpu.SemaphoreType.DMA((3 * n,))],
    )(*lands)


def _sibling_copies(w, g8_ref, land_ref, send_sems, recv_sems):
    x, y, c = _position()
    return [pltpu.make_async_remote_copy(src_ref=g8_ref.at[2 * p + (1 - c)], dst_ref=land_ref.at[p],
                                         send_sem=send_sems.at[4 * w + p], recv_sem=recv_sems.at[4 * w + p],
                                         device_id=(x, y, 1 - c), device_id_type=MESH)
            for p in range(4)]


def _chip_copies(w, sums_ref, land_ref, send_sems, recv_sems):
    x, y, c = _position()
    return [pltpu.make_async_remote_copy(src_ref=sums_ref.at[2 * px + py], dst_ref=land_ref.at[k],
                                         send_sem=send_sems.at[3 * w + k], recv_sem=recv_sems.at[3 * w + k],
                                         device_id=(px, py, c), device_id_type=MESH)
            for k, (px, py) in enumerate(_other_chips(x, y))]


def _copies_start(copies, per_weight, srcs, *, name):
    n = len(srcs)
    lands = [lax.empty((per_weight,) + s.shape[1:], s.dtype) for s in srcs]

    def body(*refs):
        src_refs, land_refs = refs[:n], refs[n:2 * n]
        send_sems, recv_sems = refs[2 * n], refs[2 * n + 1]
        token = refs[-1]
        for w in range(n):
            for cp in copies(w, src_refs[w], land_refs[w], send_sems, recv_sems):
                cp.start()
        token[...] = jnp.zeros_like(token)

    outs = pl.pallas_call(
        body, name=name,
        in_specs=[_HBM] * (2 * n),
        out_specs=tuple([_SEM, _SEM] + [_HBM] * (2 * n) + [pl.BlockSpec(memory_space=pltpu.VMEM)]),
        out_shape=tuple([pltpu.SemaphoreType.DMA((per_weight * n,)), pltpu.SemaphoreType.DMA((per_weight * n,))]
                        + [pltpu.HBM(a.shape, a.dtype) for a in srcs + lands] + [jax.ShapeDtypeStruct((8, LANES), F32)]),
        input_output_aliases={i: 2 + i for i in range(2 * n)},
        compiler_params=pltpu.CompilerParams(has_side_effects=_DATAFLOW),
    )(*[_in_hbm(a) for a in srcs + lands])
    return outs[0], outs[1], list(outs[2:2 + n]), list(outs[2 + n:2 + 2 * n]), outs[-1]


def _copies_wait(copies, send_sems, recv_sems, srcs, lands, after, *, name):
    n = len(srcs)

    def body(*refs):
        src_refs, land_refs = refs[:n], refs[n:2 * n]
        send_ref, recv_ref = refs[2 * n], refs[2 * n + 1]
        for w in range(n):
            for cp in copies(w, src_refs[w], land_refs[w], send_ref, recv_ref):
                cp.wait_send()
                cp.wait_recv()

    outs = pl.pallas_call(
        body, name=name,
        in_specs=[_HBM] * (2 * n) + [_SEM, _SEM, _ANY],
        out_specs=tuple([_HBM] * (2 * n)),
        out_shape=tuple(pltpu.HBM(a.shape, a.dtype) for a in srcs + lands),
        input_output_aliases={i: i for i in range(2 * n)},
        compiler_params=pltpu.CompilerParams(has_side_effects=_DATAFLOW),
    )(*srcs, *lands, send_sems, recv_sems, after)
    return list(outs[:n]), list(outs[n:])


def _chip_sums(g8, from_sibling, pos, *, name):
    _, R, C = g8.shape
    tr = _div(R, 512, 16)

    def body(pos_ref, g_ref, s_ref, o_ref):
        o_ref[...] = (g_ref[...].astype(F32) + s_ref[...].astype(F32)).astype(BF16)

    grid_spec = pltpu.PrefetchScalarGridSpec(
        num_scalar_prefetch=1, grid=(4, R // tr),
        in_specs=[pl.BlockSpec((None, tr, C), lambda p, i, pos_ref: (2 * p + pos_ref[2], i, 0)),
                  pl.BlockSpec((None, tr, C), lambda p, i, pos_ref: (p, i, 0))],
        out_specs=pl.BlockSpec((None, tr, C), lambda p, i, pos_ref: (p, i, 0)))
    return pl.pallas_call(
        body, name=name, grid_spec=grid_spec,
        out_shape=jax.ShapeDtypeStruct((4, R, C), BF16),
        compiler_params=_cparams("parallel", "parallel"),
    )(pos, g8, from_sibling)


def _small_all_reduce(packed, after, *, name):
    R, L = packed.shape

    def body(x_ref, sum_ref, gath_ref, send_sems, recv_sems, local_sem):
        x, y, c = _position()
        me, sibling = (x, y, c), (x, y, 1 - c)
        chips = _other_chips(x, y)

        def rows(px, py, pc):
            return gath_ref.at[pl.ds(_slot(px, py, pc) * R, R), :]

        def copy(k, block, to, src=None):
            return pltpu.make_async_remote_copy(
                src_ref=rows(*block) if src is None else src, dst_ref=rows(*block),
                send_sem=send_sems.at[k], recv_sem=recv_sems.at[k], device_id=to, device_id_type=MESH)

        mine = pltpu.make_async_copy(x_ref, rows(*me), local_sem)
        mine.start()
        first = [copy(0, me, sibling, src=x_ref)]
        first += [copy(1 + j, me, (*chip, c), src=x_ref) for j, chip in enumerate(chips)]
        for cp in first:
            cp.start()
        passed = [copy(4 + j, (*chip, c), sibling) for j, chip in enumerate(chips)]
        for j, chip in enumerate(chips):
            copy(1 + j, (*chip, c), me).wait_recv()
            passed[j].start()
        copy(0, sibling, me).wait_recv()
        for j, chip in enumerate(chips):
            copy(4 + j, (*chip, 1 - c), me).wait_recv()
        for cp in first + passed:
            cp.wait_send()
        mine.wait()
        acc = gath_ref[0:R, :]
        for d in range(1, N_DEV):
            acc = acc + gath_ref[d * R:(d + 1) * R, :]
        sum_ref[...] = acc

    vmem = pl.BlockSpec(memory_space=pltpu.VMEM)
    body, in_specs, args = _ordered_after(body, 1, [vmem], (packed,), after)
    return pl.pallas_call(
        body, name=name, in_specs=in_specs, out_specs=vmem,
        out_shape=jax.ShapeDtypeStruct((R, L), F32),
        scratch_shapes=[pltpu.VMEM((N_DEV * R, L), F32), pltpu.SemaphoreType.DMA((7,)), pltpu.SemaphoreType.DMA((7,)),
                        pltpu.SemaphoreType.DMA],
        compiler_params=pltpu.CompilerParams(vmem_limit_bytes=VMEM_LIMIT),
    )(*args)


def _adamw_math(w, g, m, v):
    m = ADAM_B1 * m + (1.0 - ADAM_B1) * g
    v = ADAM_B2 * v + (1.0 - ADAM_B2) * (g * g)
    m_hat = m / (1.0 - ADAM_B1 ** ADAM_STEP)
    v_hat = v / (1.0 - ADAM_B2 ** ADAM_STEP)
    delta = -ADAM_LR * (m_hat / (jnp.sqrt(v_hat) + ADAM_EPS) + ADAM_WD * w)
    return delta, m, v


def _adamw_shard(w, m, v, g8, from_sibling, from_chips, pos, *, name):
    R, C = w.shape
    tr = _div(R, 256, 16)

    def body(pos_ref, w_ref, m_ref, v_ref, g_ref, s_ref, r_ref, go_ref, d_ref, mo_ref, vo_ref):
        g = g_ref[...].astype(F32) + s_ref[...].astype(F32)
        for k in range(3):
            g = g + r_ref[k].astype(F32)
        delta, m_, v_ = _adamw_math(w_ref[...], g, m_ref[...], v_ref[...])
        go_ref[...] = g
        d_ref[...] = delta
        mo_ref[...] = m_
        vo_ref[...] = v_

    blk = pl.BlockSpec((tr, C), lambda i, pos_ref: (i, 0))
    grid_spec = pltpu.PrefetchScalarGridSpec(
        num_scalar_prefetch=1, grid=(R // tr,),
        in_specs=[blk, blk, blk,
                  pl.BlockSpec((None, tr, C), lambda i, pos_ref: (pos_ref[0], i, 0)),
                  pl.BlockSpec((None, tr, C), lambda i, pos_ref: (pos_ref[1], i, 0)),
                  pl.BlockSpec((3, tr, C), lambda i, pos_ref: (0, i, 0))],
        out_specs=[blk] * 4)
    out = jax.ShapeDtypeStruct((R, C), F32)
    return pl.pallas_call(
        body, name=name, grid_spec=grid_spec, out_shape=[out] * 4,
        compiler_params=_cparams("parallel"),
    )(pos, w, m, v, g8, from_sibling, from_chips)


def _adamw_small(w, g, m, v, *, name):
    R, L = w.shape

    def body(w_ref, g_ref, m_ref, v_ref, d_ref, mo_ref, vo_ref):
        delta, m_, v_ = _adamw_math(w_ref[...], g_ref[...], m_ref[...], v_ref[...])
        d_ref[...] = delta
        mo_ref[...] = m_
        vo_ref[...] = v_

    vmem = pl.BlockSpec(memory_space=pltpu.VMEM)
    out = jax.ShapeDtypeStruct((R, L), F32)
    return pl.pallas_call(body, name=name, in_specs=[vmem] * 4, out_specs=[vmem] * 3, out_shape=[out] * 3)(w, g, m, v)


_TILE = 8 * LANES


def _pack(pieces):
    rows = []
    for p in pieces:
        flat = p.reshape(-1).astype(F32)
        padded = -(-flat.shape[0] // _TILE) * _TILE
        rows.append(jnp.pad(flat, (0, padded - flat.shape[0])).reshape(-1, LANES))
    return jnp.concatenate(rows, axis=0)


def _unpack(packed, like):
    out, r = [], 0
    for p in like:
        size = int(np.prod(p.shape)) if p.shape else 1
        nrows = -(-size // _TILE) * 8
        out.append(packed[r:r + nrows].reshape(-1)[:size].reshape(p.shape))
        r += nrows
    return out


_BIG = ("w_in", "w_a_out", "w_b_out", "w_o", "w_gate", "w_up", "w_down")
_TRANSPOSED = ("w_in", "w_gate", "w_up")
_COL_SHARDED = ("w_a_out", "w_b_out")
_GATHER_GROUPS = (("w_in",), ("w_a_out", "w_b_out", "w_o"), ("w_gate", "w_up"), ("w_down",))
_START_AFTER_WAIT = {0: (1, 2), 2: (3,)}
_SMALL = ("norm_mix", "sgu_v_gain", "sgu_w_s", "sgu_b_s", "attn_sink", "rel_bias", "norm_ffn", "norm_final")
_ORDER = ("w_in", "norm_mix", "sgu_v_gain", "sgu_w_s", "sgu_b_s", "w_a_out", "attn_sink", "rel_bias", "w_b_out", "w_o",
          "norm_ffn", "w_gate", "w_up", "w_down", "norm_final")


def _shard(name, a):
    return jnp.swapaxes(a, 1, 2)[0] if name in _TRANSPOSED else a[0]


def _unshard(name, a):
    return jnp.swapaxes(a[None], 1, 2) if name in _TRANSPOSED else a[None]


def _whole(name, gathered):
    _, r, c = gathered.shape
    return gathered if name in _COL_SHARDED else gathered.reshape(N_DEV * r, c)


def _blocks(name, grad):
    if name in _COL_SHARDED:
        return grad
    r, c = grad.shape
    return grad.reshape(N_DEV, r // N_DEV, c)


def kernel(x, w_in, norm_mix, sgu_v_gain, sgu_w_s, sgu_b_s, w_a_out, attn_sink, rel_bias, w_b_out, w_o, norm_ffn, w_gate, w_up, w_down, norm_final, loss_target, m_w_in, m_norm_mix, m_sgu_v_gain, m_sgu_w_s, m_sgu_b_s, m_w_a_out, m_attn_sink, m_rel_bias, m_w_b_out, m_w_o, m_norm_ffn, m_w_gate, m_w_up, m_w_down, m_norm_final, v_w_in, v_norm_mix, v_sgu_v_gain, v_sgu_w_s, v_sgu_b_s, v_w_a_out, v_attn_sink, v_rel_bias, v_w_b_out, v_w_o, v_norm_ffn, v_w_gate, v_w_up, v_w_down, v_norm_final):
    w = dict(w_in=w_in, norm_mix=norm_mix, sgu_v_gain=sgu_v_gain, sgu_w_s=sgu_w_s, sgu_b_s=sgu_b_s, w_a_out=w_a_out,
             attn_sink=attn_sink, rel_bias=rel_bias, w_b_out=w_b_out, w_o=w_o, norm_ffn=norm_ffn, w_gate=w_gate,
             w_up=w_up, w_down=w_down, norm_final=norm_final)
    m = dict(w_in=m_w_in, norm_mix=m_norm_mix, sgu_v_gain=m_sgu_v_gain, sgu_w_s=m_sgu_w_s, sgu_b_s=m_sgu_b_s,
             w_a_out=m_w_a_out, attn_sink=m_attn_sink, rel_bias=m_rel_bias, w_b_out=m_w_b_out, w_o=m_w_o,
             norm_ffn=m_norm_ffn, w_gate=m_w_gate, w_up=m_w_up, w_down=m_w_down, norm_final=m_norm_final)
    v = dict(w_in=v_w_in, norm_mix=v_norm_mix, sgu_v_gain=v_sgu_v_gain, sgu_w_s=v_sgu_w_s, sgu_b_s=v_sgu_b_s,
             w_a_out=v_w_a_out, attn_sink=v_attn_sink, rel_bias=v_rel_bias, w_b_out=v_w_b_out, w_o=v_w_o,
             norm_ffn=v_norm_ffn, w_gate=v_w_gate, w_up=v_w_up, w_down=v_w_down, norm_final=v_norm_final)
    xc, yc, cc = _position()
    pos = jnp.stack([_slot(xc, yc, cc), 2 * xc + yc, cc]).astype(jnp.int32)

    in_flight, full = {}, {}

    def start_gather(groups, after):
        names = [n for gi in groups for n in _GATHER_GROUPS[gi]]
        buffers = [_own_slot(_shard(n, w[n]), pos, name="own_slot_" + n, after=after) for n in names]
        flights, _ = _ag_start(buffers, [[names.index(n) for n in _GATHER_GROUPS[gi]] for gi in groups],
                               name="ag_start_%d" % groups[0])
        in_flight.update(zip(groups, flights))

    def weight(name, after):
        if name not in full:
            gi = next(i for i, grp in enumerate(_GATHER_GROUPS) if name in grp)
            send_sems, recv_sems, lands = in_flight[gi]
            lands, token = _ag_wait(send_sems, recv_sems, lands, after, name="ag_wait_%d" % gi)
            if gi in _START_AFTER_WAIT:
                start_gather(_START_AFTER_WAIT[gi], token)
            gathered = _ag_forward(lands, name="ag_forward_%d" % gi)
            full.update({n: _whole(n, g) for n, g in zip(_GATHER_GROUPS[gi], gathered)})
        return full[name]

    start_gather((0,), None)

    to_sibling, reducing = [], {}

    def emit(names, grads):
        g8 = [_blocks(n, g) for n, g in zip(names, grads)]
        send_sems, recv_sems, g8, lands, token = _copies_start(_sibling_copies, 4, g8, name="rs_sibling_start_" + names[0])
        to_sibling.append((names, send_sems, recv_sems, g8, lands))
        return token

    def flush(after):
        names, send_sems, recv_sems, g8, lands = to_sibling.pop()
        g8, from_sibling = _copies_wait(_sibling_copies, send_sems, recv_sems, g8, lands, after,
                                        name="rs_sibling_wait_" + names[0])
        sums4 = [_chip_sums(g, s, pos, name="chip_sums_" + n) for n, g, s in zip(names, g8, from_sibling)]
        send_sems, recv_sems, sums4, lands, token = _copies_start(_chip_copies, 3, sums4, name="rs_chips_start_" + names[0])
        reducing[names] = (g8, from_sibling, send_sems, recv_sems, sums4, lands)
        return token

    loss, grad_x, small_grads_local = _local_step(
        x[0], loss_target[0], weight, emit, flush, norm_mix, sgu_v_gain, sgu_w_s[0], sgu_b_s[0], attn_sink, rel_bias,
        norm_ffn, norm_final[None])

    out_g, out_d, out_m, out_v = {}, {}, {}, {}
    small_like = [w[n] for n in _SMALL]
    small_w = _pack(small_like)
    packed = _pack([small_grads_local[n] for n in _SMALL] + [loss[0, 0]])
    after = grad_x
    for gi, (names, (g8, from_sibling, send_sems, recv_sems, sums4, lands)) in enumerate(reducing.items()):
        if gi == len(reducing) - 1:
            summed = _small_all_reduce(packed, after, name="small_all_reduce")
            after = summed
        _, from_chips = _copies_wait(_chip_copies, send_sems, recv_sems, sums4, lands, after,
                                     name="rs_chips_wait_" + names[0])
        for i, n in enumerate(names):
            g, d, m_, v_ = _adamw_shard(_shard(n, w[n]), _shard(n, m[n]), _shard(n, v[n]), g8[i], from_sibling[i],
                                        from_chips[i], pos, name="adamw_" + n)
            out_g[n], out_d[n], out_m[n], out_v[n] = (_unshard(n, o) for o in (g, d, m_, v_))
            after = d
    *small_grads, loss_sum = _unpack(summed, small_like + [jax.ShapeDtypeStruct((), F32)])
    d_s, m_s, v_s = _adamw_small(small_w, summed[:small_w.shape[0]], _pack([m[n] for n in _SMALL]),
                                 _pack([v[n] for n in _SMALL]), name="adamw_small")
    for n, g, d, m_, v_ in zip(_SMALL, small_grads, _unpack(d_s, small_like), _unpack(m_s, small_like), _unpack(v_s, small_like)):
        out_g[n], out_d[n], out_m[n], out_v[n] = g, d, m_, v_

    return (loss_sum, grad_x[None], *[out_g[n] for n in _ORDER], *[out_d[n] for n in _ORDER],
            *[out_m[n] for n in _ORDER], *[out_v[n] for n in _ORDER])
```

```python
import functools
import math

import numpy as np
import jax
import jax.numpy as jnp
from jax import lax
from jax.experimental import pallas as pl
from jax.experimental.pallas import tpu as pltpu

F32 = jnp.float32
BF16 = jnp.bfloat16

EPS = 1e-6
NEG = -1e30
HEAD_DIM = 128
BLOCK = 128
N_KV_HEADS = 2
KV_WIDTH = N_KV_HEADS * HEAD_DIM
REL_BUCKETS = 32
REL_MAX_DIST = 128

ADAM_LR = 0.001
ADAM_B1 = 0.9
ADAM_B2 = 0.999
ADAM_EPS = 1e-08
ADAM_WD = 0.01
ADAM_STEP = 10

N_DEV = 8
LANES = 128
VMEM_LIMIT = 56 * 1024 * 1024
MESH = pl.DeviceIdType.MESH


def _cparams(*sem):
    return pltpu.CompilerParams(dimension_semantics=sem, vmem_limit_bytes=VMEM_LIMIT)


def _div(n, target, mult=LANES):
    best = None
    for d in range(mult, min(n, target) + 1, mult):
        if n % d == 0:
            best = d
    assert best is not None, (n, target, mult)
    return best


_ANY = pl.BlockSpec(memory_space=pl.ANY)


def _ordered_after(body, n_inputs, in_specs, args, after):
    if after is None:
        return body, in_specs, args

    def wrapped(*refs):
        return body(*refs[:n_inputs], *refs[n_inputs + 1:])

    return wrapped, list(in_specs) + [_ANY], tuple(args) + (after,)


def _bucket_map():
    nb = REL_BUCKETS // 2
    qi = np.arange(BLOCK)[:, None]
    kj = np.arange(3 * BLOCK)[None, :]
    rel = kj - BLOCK - qi
    ret = np.where(rel > 0, nb, 0)
    n = np.abs(rel)
    max_exact = nb // 2
    nf = np.maximum(n, 1).astype(np.float32)
    large = max_exact + (np.log(nf / np.float32(max_exact)) / np.float32(math.log(REL_MAX_DIST / max_exact))
                         * np.float32(nb - max_exact)).astype(np.int32)
    large = np.minimum(large, nb - 1)
    return (ret + np.where(n < max_exact, n, large)).astype(np.int32)


_GELU_C = math.sqrt(2.0 / math.pi)
_GELU_A = 0.044715


def _gelu(x):
    t = jnp.tanh(_GELU_C * (x + _GELU_A * (x * x * x)))
    return 0.5 * x * (1.0 + t)


def _gelu_and_grad(x):
    x2 = x * x
    t = jnp.tanh(_GELU_C * (x + _GELU_A * (x2 * x)))
    g = 0.5 * x * (1.0 + t)
    dg = 0.5 * (1.0 + t) + 0.5 * x * (1.0 - t * t) * (_GELU_C * (1.0 + 3.0 * _GELU_A * x2))
    return g, dg


def _sigmoid(x):
    return 1.0 / (1.0 + jnp.exp(-x))


def _mm(a, b, *, name, ta=False, tb=False, add=None, out_dtype=F32, bm=1024, bn=1024, bk=None, after=None):
    if ta:
        K, M = a.shape
    else:
        M, K = a.shape
    N = b.shape[0] if tb else b.shape[1]
    assert (b.shape[1] if tb else b.shape[0]) == K
    bm = _div(M, bm)
    bn = _div(N, bn)
    bk = K if bk is None else _div(K, bk)
    nk = K // bk
    a_spec = pl.BlockSpec((bk, bm), lambda i, j, k: (k, i)) if ta else pl.BlockSpec((bm, bk), lambda i, j, k: (i, k))
    b_spec = pl.BlockSpec((bn, bk), lambda i, j, k: (j, k)) if tb else pl.BlockSpec((bk, bn), lambda i, j, k: (k, j))
    o_spec = pl.BlockSpec((bm, bn), lambda i, j, k: (i, j))
    dims = (((0 if ta else 1,), (1 if tb else 0,)), ((), ()))
    has_add = add is not None

    def body(*refs):
        if has_add:
            a_ref, b_ref, add_ref, o_ref, *scratch = refs
        else:
            a_ref, b_ref, o_ref, *scratch = refs
            add_ref = None
        p = lax.dot_general(a_ref[...].astype(BF16), b_ref[...].astype(BF16), dims, preferred_element_type=F32)
        if nk == 1:
            if has_add:
                p = p + add_ref[...]
            o_ref[...] = p.astype(out_dtype)
        else:
            acc = scratch[0]
            k = pl.program_id(2)

            @pl.when(k == 0)
            def _():
                acc[...] = p

            @pl.when(k > 0)
            def _():
                acc[...] += p

            @pl.when(k == nk - 1)
            def _():
                r = acc[...]
                if has_add:
                    r = r + add_ref[...]
                o_ref[...] = r.astype(out_dtype)

    in_specs = [a_spec, b_spec] + ([o_spec] if has_add else [])
    args = (a, b) + ((add,) if has_add else ())
    body, in_specs, args = _ordered_after(body, len(args), in_specs, args, after)
    return pl.pallas_call(
        body, name=name, grid=(M // bm, N // bn, nk),
        in_specs=in_specs, out_specs=o_spec,
        out_shape=jax.ShapeDtypeStruct((M, N), out_dtype),
        scratch_shapes=[pltpu.VMEM((bm, bn), F32)] if nk > 1 else [],
        compiler_params=_cparams("parallel", "parallel", "arbitrary"),
    )(*args)


def _blocks_per_tile(c):
    nb = 1
    while (nb * c) % LANES or (nb * c < 1024 and nb < N_DEV):
        nb *= 2
    assert nb <= N_DEV and (nb * c) % LANES == 0, c
    return nb


def _mm_w8(a, w8, *, name, bm=1024):
    M, K = a.shape
    _, _, c = w8.shape
    nb = _blocks_per_tile(c)
    bm = _div(M, bm)

    def body(a_ref, w_ref, o_ref):
        a_ = a_ref[...]
        for t in range(nb):
            o_ref[:, t * c:(t + 1) * c] = jnp.dot(a_, w_ref[t], preferred_element_type=F32)

    return pl.pallas_call(
        body, name=name, grid=(M // bm, N_DEV // nb),
        in_specs=[pl.BlockSpec((bm, K), lambda i, j: (i, 0)), pl.BlockSpec((nb, K, c), lambda i, j: (j, 0, 0))],
        out_specs=pl.BlockSpec((bm, nb * c), lambda i, j: (i, j)),
        out_shape=jax.ShapeDtypeStruct((M, N_DEV * c), F32),
        compiler_params=_cparams("parallel", "parallel"),
    )(a, w8)


def _mm_w8t(dy, w8, *, name, add=None, out_dtype=F32, bm=1024, bn=1024, after=None, lead=None):
    M = dy.shape[-2]
    _, K, c = w8.shape
    nb = _blocks_per_tile(c)
    nk = N_DEV // nb
    bm, bn = _div(M, bm), _div(K, bn)
    has_add = add is not None
    dims = (((1,), (1,)), ((), ()))

    def body(*refs):
        if has_add:
            dy_ref, w_ref, add_ref, o_ref, acc = refs
        else:
            dy_ref, w_ref, o_ref, acc = refs
        p = lax.dot_general(dy_ref[:, 0:c], w_ref[0], dims, preferred_element_type=F32)
        for t in range(1, nb):
            p = p + lax.dot_general(dy_ref[:, t * c:(t + 1) * c], w_ref[t], dims, preferred_element_type=F32)
        k = pl.program_id(2)

        @pl.when(k == 0)
        def _():
            acc[...] = p

        @pl.when(k > 0)
        def _():
            acc[...] += p

        @pl.when(k == nk - 1)
        def _():
            r = acc[...]
            if has_add:
                r = r + add_ref[...]
            o_ref[...] = r.astype(out_dtype)

    o_spec = pl.BlockSpec((bm, bn), lambda i, j, k: (i, j))
    dy_spec = (pl.BlockSpec((bm, nb * c), lambda i, j, k: (i, k)) if lead is None
               else pl.BlockSpec((None, bm, nb * c), lambda i, j, k: (lead, i, k)))
    in_specs = [dy_spec, pl.BlockSpec((nb, bn, c), lambda i, j, k: (k, j, 0))]
    in_specs += [o_spec] if has_add else []
    args = (dy, w8) + ((add,) if has_add else ())
    body, in_specs, args = _ordered_after(body, len(args), in_specs, args, after)
    return pl.pallas_call(
        body, name=name, grid=(M // bm, K // bn, nk),
        in_specs=in_specs, out_specs=o_spec,
        out_shape=jax.ShapeDtypeStruct((M, K), out_dtype),
        scratch_shapes=[pltpu.VMEM((bm, bn), F32)],
        compiler_params=_cparams("parallel", "parallel", "arbitrary"),
    )(*args)


def _mm_gw8(x, dy, c, *, name, bk=1024, lead=None):
    T, K = x.shape
    nb = _blocks_per_tile(c)
    bk = _div(K, bk)
    dims = (((0,), (0,)), ((), ()))

    def body(x_ref, dy_ref, o_ref):
        x_ = x_ref[...]
        for t in range(nb):
            o_ref[t] = lax.dot_general(x_, dy_ref[:, t * c:(t + 1) * c], dims, preferred_element_type=F32).astype(BF16)

    dy_spec = (pl.BlockSpec((T, nb * c), lambda i, j: (0, j)) if lead is None
               else pl.BlockSpec((None, T, nb * c), lambda i, j: (lead, 0, j)))
    return pl.pallas_call(
        body, name=name, grid=(K // bk, N_DEV // nb),
        in_specs=[pl.BlockSpec((T, bk), lambda i, j: (0, i)), dy_spec],
        out_specs=pl.BlockSpec((nb, bk, c), lambda i, j: (j, i, 0)),
        out_shape=jax.ShapeDtypeStruct((N_DEV, K, c), BF16),
        compiler_params=_cparams("parallel", "parallel"),
    )(x, dy)


def _rms_fwd(x, g, *, name):
    T, D = x.shape
    tm = _div(T, 256, 8)

    def body(x_ref, g_ref, h_ref):
        xf = x_ref[...]
        r = lax.rsqrt(jnp.mean(xf * xf, axis=-1, keepdims=True) + EPS)
        h_ref[...] = ((xf * r) * g_ref[...]).astype(BF16)

    return pl.pallas_call(
        body, name=name, grid=(T // tm,),
        in_specs=[pl.BlockSpec((tm, D), lambda i: (i, 0)), pl.BlockSpec((1, D), lambda i: (0, 0))],
        out_specs=pl.BlockSpec((tm, D), lambda i: (i, 0)),
        out_shape=jax.ShapeDtypeStruct((T, D), BF16),
        compiler_params=_cparams("parallel"),
    )(x, g)


def _rms_bwd(x, g, dh, dres, *, name, want_bf16, after=None):
    T, D = x.shape
    tm = _div(T, 256, 8)

    def body(x_ref, g_ref, dh_ref, dres_ref, dx_ref, *rest):
        if want_bf16:
            dxb_ref, dg_ref = rest
        else:
            (dg_ref,) = rest
        xf = x_ref[...]
        r = lax.rsqrt(jnp.mean(xf * xf, axis=-1, keepdims=True) + EPS)
        xhat = xf * r
        dh_ = dh_ref[...]
        dy = dh_ * g_ref[...]
        dx = dres_ref[...] + r * (dy - xhat * jnp.mean(dy * xhat, axis=-1, keepdims=True))
        dx_ref[...] = dx
        if want_bf16:
            dxb_ref[...] = dx.astype(BF16)
        part = jnp.sum(dh_ * xhat, axis=0, keepdims=True)

        @pl.when(pl.program_id(0) == 0)
        def _():
            dg_ref[...] = part

        @pl.when(pl.program_id(0) > 0)
        def _():
            dg_ref[...] += part

    row = pl.BlockSpec((tm, D), lambda i: (i, 0))
    vec = pl.BlockSpec((1, D), lambda i: (0, 0))
    out_specs = [row] + ([row] if want_bf16 else []) + [vec]
    out_shape = ([jax.ShapeDtypeStruct((T, D), F32)] + ([jax.ShapeDtypeStruct((T, D), BF16)] if want_bf16 else [])
                 + [jax.ShapeDtypeStruct((1, D), F32)])
    body, in_specs, args = _ordered_after(body, 4, [row, vec, row, row], (x, g, dh, dres), after)
    return pl.pallas_call(
        body, name=name, grid=(T // tm,),
        in_specs=in_specs, out_specs=out_specs, out_shape=out_shape,
        compiler_params=_cparams("arbitrary"),
    )(*args)


def _loss_head(x, g, target, *, name):
    T, D = x.shape
    tm = _div(T, 256, 8)

    def body(x_ref, g_ref, t_ref, loss_ref, dx_ref, dxb_ref, dg_ref):
        xf = x_ref[...]
        r = lax.rsqrt(jnp.mean(xf * xf, axis=-1, keepdims=True) + EPS)
        xhat = xf * r
        gain = g_ref[...]
        err = xhat * gain - t_ref[...]
        lpart = 0.5 * jnp.sum(jnp.mean(err * err, axis=-1, keepdims=True), axis=0, keepdims=True)
        dh_ = err * (1.0 / D)
        dy = dh_ * gain
        dx = r * (dy - xhat * jnp.mean(dy * xhat, axis=-1, keepdims=True))
        dx_ref[...] = dx
        dxb_ref[...] = dx.astype(BF16)
        part = jnp.sum(dh_ * xhat, axis=0, keepdims=True)

        @pl.when(pl.program_id(0) == 0)
        def _():
            dg_ref[...] = part
            loss_ref[...] = jnp.broadcast_to(lpart, loss_ref.shape)

        @pl.when(pl.program_id(0) > 0)
        def _():
            dg_ref[...] += part
            loss_ref[...] += jnp.broadcast_to(lpart, loss_ref.shape)

    row = pl.BlockSpec((tm, D), lambda i: (i, 0))
    vec = pl.BlockSpec((1, D), lambda i: (0, 0))
    return pl.pallas_call(
        body, name=name, grid=(T // tm,),
        in_specs=[row, vec, row],
        out_specs=[pl.BlockSpec((8, LANES), lambda i: (0, 0)), row, row, vec],
        out_shape=[jax.ShapeDtypeStruct((8, LANES), F32), jax.ShapeDtypeStruct((T, D), F32),
                   jax.ShapeDtypeStruct((T, D), BF16), jax.ShapeDtypeStruct((1, D), F32)],
        compiler_params=_cparams("arbitrary"),
    )(x, g, target)


def _gate_cols(D):
    off_a = 3 * D // 2 + 2 * KV_WIDTH
    off_b = off_a + D
    cw = math.gcd(math.gcd(off_a, off_b), math.gcd(D, 512))
    return cw, off_a // cw, off_b // cw


def _merge_fwd(z, ya, yb, *, name):
    T, D = ya.shape
    cw, ba, bb = _gate_cols(D)
    tm = _div(T, 512, 8)

    def body(ga_ref, gb_ref, ya_ref, yb_ref, m_ref):
        m_ref[...] = (_sigmoid(ga_ref[...]) * ya_ref[...] + _sigmoid(gb_ref[...]) * yb_ref[...]).astype(BF16)

    blk = pl.BlockSpec((tm, cw), lambda i, j: (i, j))
    return pl.pallas_call(
        body, name=name, grid=(T // tm, D // cw),
        in_specs=[pl.BlockSpec((tm, cw), lambda i, j: (i, ba + j)), pl.BlockSpec((tm, cw), lambda i, j: (i, bb + j)), blk, blk],
        out_specs=blk, out_shape=jax.ShapeDtypeStruct((T, D), BF16),
        compiler_params=_cparams("parallel", "parallel"),
    )(z, z, ya, yb)


def _merge_bwd(z, ya, yb, dm, *, name, after=None):
    T, D = ya.shape
    cw, ba, bb = _gate_cols(D)
    nj = D // cw
    assert bb == ba + nj
    tm = _div(T, 512, 8)

    def body(g_ref, ya_ref, yb_ref, dm_ref, dy_ref, dz_ref):
        sig = _sigmoid(g_ref[...])
        dm_ = dm_ref[...]
        y = jnp.where(pl.program_id(1) == 0, ya_ref[...], yb_ref[...])
        dy_ref[...] = (dm_ * sig).astype(BF16)
        dz_ref[...] = (dm_ * y * (sig * (1.0 - sig))).astype(BF16)

    in_specs = [pl.BlockSpec((tm, cw), lambda i, s, j: (i, ba + s * nj + j)),
                pl.BlockSpec((tm, cw), lambda i, s, j: (i, j * (1 - s))),
                pl.BlockSpec((tm, cw), lambda i, s, j: (i, j * s)),
                pl.BlockSpec((tm, cw), lambda i, s, j: (i, j))]
    body, in_specs, args = _ordered_after(body, 4, in_specs, (z, ya, yb, dm), after)
    return pl.pallas_call(
        body, name=name, grid=(T // tm, 2, nj),
        in_specs=in_specs,
        out_specs=[pl.BlockSpec((None, tm, cw), lambda i, s, j: (s, i, j)),
                   pl.BlockSpec((tm, cw), lambda i, s, j: (i, ba + s * nj + j))],
        out_shape=[jax.ShapeDtypeStruct((2, T, D), BF16), jax.ShapeDtypeStruct(z.shape, BF16)],
        compiler_params=_cparams("parallel", "arbitrary", "arbitrary"),
    )(*args)


def _swiglu_mm_fwd(h, wg_t, wu_t, *, name, bm=1024, bn=512):
    T, D = h.shape
    F = wg_t.shape[0]
    bm, bn = _div(T, bm), _div(F, bn)
    dims = (((1,), (1,)), ((), ()))

    def body(h_ref, wg_ref, wu_ref, g_ref, u_ref, act_ref):
        h_ = h_ref[...]
        g = lax.dot_general(h_, wg_ref[...], dims, preferred_element_type=F32)
        u = lax.dot_general(h_, wu_ref[...], dims, preferred_element_type=F32)
        g_ref[...] = g
        u_ref[...] = u
        act_ref[...] = (g * _sigmoid(g) * u).astype(BF16)

    w_spec = pl.BlockSpec((bn, D), lambda i, j: (j, 0))
    o_spec = pl.BlockSpec((bm, bn), lambda i, j: (i, j))
    return pl.pallas_call(
        body, name=name, grid=(T // bm, F // bn),
        in_specs=[pl.BlockSpec((bm, D), lambda i, j: (i, 0)), w_spec, w_spec], out_specs=[o_spec] * 3,
        out_shape=[jax.ShapeDtypeStruct((T, F), F32), jax.ShapeDtypeStruct((T, F), F32), jax.ShapeDtypeStruct((T, F), BF16)],
        compiler_params=_cparams("parallel", "parallel"),
    )(h, wg_t, wu_t)


def _swiglu_mm_bwd(dx, w_down, gate, up, *, name, bm=1024, bn=512, after=None):
    T, D = dx.shape
    F = w_down.shape[0]
    bm, bn = _div(T, bm), _div(F, bn)
    dims = (((1,), (1,)), ((), ()))

    def body(dx_ref, w_ref, g_ref, u_ref, dg_ref, du_ref):
        d = lax.dot_general(dx_ref[...], w_ref[...], dims, preferred_element_type=F32)
        g = g_ref[...]
        s = _sigmoid(g)
        silu = g * s
        dg_ref[...] = (d * u_ref[...] * (s + silu * (1.0 - s))).astype(BF16)
        du_ref[...] = (d * silu).astype(BF16)

    o_spec = pl.BlockSpec((bm, bn), lambda i, j: (i, j))
    in_specs = [pl.BlockSpec((bm, D), lambda i, j: (i, 0)), pl.BlockSpec((bn, D), lambda i, j: (j, 0)), o_spec, o_spec]
    body, in_specs, args = _ordered_after(body, 4, in_specs, (dx, w_down, gate, up), after)
    out = jax.ShapeDtypeStruct((T, F), BF16)
    return pl.pallas_call(
        body, name=name, grid=(T // bm, F // bn), in_specs=in_specs, out_specs=[o_spec, o_spec], out_shape=[out, out],
        compiler_params=_cparams("parallel", "parallel"),
    )(*args)


def _sgu_fwd(z, gain, ws_b, bs_t, *, name):
    T = z.shape[0]
    SW = gain.shape[1]
    G = SW // BLOCK

    def body(zu_ref, zv_ref, gain_ref, ws_ref, bs_ref, a_ref):
        u = _gelu(zu_ref[...])
        vg = _gelu(zv_ref[...])
        r = lax.rsqrt(jnp.mean(vg * vg, axis=-1, keepdims=True) + EPS)
        vn = ((vg * r) * gain_ref[...]).astype(BF16)
        for g in range(G):
            sl = slice(g * BLOCK, (g + 1) * BLOCK)
            mixed = jnp.dot(ws_ref[g], vn[:, sl], preferred_element_type=F32) + bs_ref[:, g:g + 1]
            a_ref[:, sl] = (u[:, sl] * mixed).astype(BF16)

    return pl.pallas_call(
        body, name=name, grid=(T // BLOCK,),
        in_specs=[pl.BlockSpec((BLOCK, SW), lambda c: (c, 0)), pl.BlockSpec((BLOCK, SW), lambda c: (c, 1)),
                  pl.BlockSpec((1, SW), lambda c: (0, 0)), pl.BlockSpec((G, BLOCK, BLOCK), lambda c: (0, 0, 0)),
                  pl.BlockSpec((BLOCK, G), lambda c: (0, 0))],
        out_specs=pl.BlockSpec((BLOCK, SW), lambda c: (c, 0)),
        out_shape=jax.ShapeDtypeStruct((T, SW), BF16),
        compiler_params=_cparams("parallel"),
    )(z, z, gain, ws_b, bs_t)


def _sgu_bwd(z, gain, ws_b, bs_t, da, dz, *, name):
    T = z.shape[0]
    SW = gain.shape[1]
    G = SW // BLOCK

    def body(zu_ref, zv_ref, gain_ref, ws_ref, bs_ref, da_ref, dz_in_ref, dz_ref, dws_ref, dbs_ref, dgain_ref, dvn_ref):
        first = pl.program_id(0) == 0

        @pl.when(first)
        def _():
            dws_ref[...] = jnp.zeros_like(dws_ref)
            dbs_ref[...] = jnp.zeros_like(dbs_ref)
            dgain_ref[...] = jnp.zeros_like(dgain_ref)

        u, du = _gelu_and_grad(zu_ref[...])
        vg, dvg = _gelu_and_grad(zv_ref[...])
        r = lax.rsqrt(jnp.mean(vg * vg, axis=-1, keepdims=True) + EPS)
        xhat = vg * r
        gain_ = gain_ref[...]
        vn = (xhat * gain_).astype(BF16)
        da_ = da_ref[...]
        for g in range(G):
            sl = slice(g * BLOCK, (g + 1) * BLOCK)
            w = ws_ref[g]
            mixed = jnp.dot(w, vn[:, sl], preferred_element_type=F32) + bs_ref[:, g:g + 1]
            dmix = da_[:, sl] * u[:, sl]
            dz_ref[:, sl] = (da_[:, sl] * mixed * du[:, sl]).astype(BF16)
            dmb = dmix.astype(BF16)
            dws_ref[g] += lax.dot_general(dmb, vn[:, sl], (((1,), (1,)), ((), ())), preferred_element_type=F32)
            dbs_ref[:, g:g + 1] += jnp.sum(dmix, axis=-1, keepdims=True)
            dvn_ref[:, sl] = lax.dot_general(w, dmb, (((0,), (0,)), ((), ())), preferred_element_type=F32)
        dvn = dvn_ref[...]
        dgain_ref[...] += jnp.sum(dvn * xhat, axis=0, keepdims=True)
        dy = dvn * gain_
        dv_ = r * (dy - xhat * jnp.mean(dy * xhat, axis=-1, keepdims=True))
        dz_ref[:, SW:] = (dv_ * dvg).astype(BF16)

    row = pl.BlockSpec((BLOCK, SW), lambda c: (c, 0))
    return pl.pallas_call(
        body, name=name, grid=(T // BLOCK,),
        in_specs=[row, pl.BlockSpec((BLOCK, SW), lambda c: (c, 1)),
                  pl.BlockSpec((1, SW), lambda c: (0, 0)), pl.BlockSpec((G, BLOCK, BLOCK), lambda c: (0, 0, 0)),
                  pl.BlockSpec((BLOCK, G), lambda c: (0, 0)), row, _ANY],
        out_specs=[pl.BlockSpec((BLOCK, 2 * SW), lambda c: (c, 0)), pl.BlockSpec((G, BLOCK, BLOCK), lambda c: (0, 0, 0)),
                   pl.BlockSpec((BLOCK, G), lambda c: (0, 0)), pl.BlockSpec((1, SW), lambda c: (0, 0))],
        out_shape=[jax.ShapeDtypeStruct(dz.shape, dz.dtype),
                   jax.ShapeDtypeStruct((G, BLOCK, BLOCK), F32), jax.ShapeDtypeStruct((BLOCK, G), F32),
                   jax.ShapeDtypeStruct((1, SW), F32)],
        input_output_aliases={6: 0},
        scratch_shapes=[pltpu.VMEM((BLOCK, SW), F32)],
        compiler_params=_cparams("arbitrary"),
    )(z, z, gain, ws_b, bs_t, da, dz)


def _bias_table(rel_bias, bmap, *, name):
    H = rel_bias.shape[1]

    def body(rb_ref, bmap_ref, o_ref):
        bm_ = bmap_ref[...]
        for h in range(H):
            acc = jnp.zeros(bm_.shape, F32)
            for b in range(REL_BUCKETS):
                acc = jnp.where(bm_ == b, rb_ref[b, h], acc)
            o_ref[h] = acc

    return pl.pallas_call(
        body, name=name,
        in_specs=[pl.BlockSpec(memory_space=pltpu.SMEM), pl.BlockSpec(memory_space=pltpu.VMEM)],
        out_specs=pl.BlockSpec(memory_space=pltpu.VMEM),
        out_shape=jax.ShapeDtypeStruct((H, BLOCK, 3 * BLOCK), F32),
    )(rel_bias, bmap)


def _attn_probs(q_ref, kb, bias_ref, sink_ref, valid, h, group):
    kv = h // group
    qh = q_ref[:, h * HEAD_DIM:(h + 1) * HEAD_DIM].astype(BF16)
    s = lax.dot_general(qh, kb[:, kv * HEAD_DIM:(kv + 1) * HEAD_DIM], (((1,), (1,)), ((), ())),
                        preferred_element_type=F32)
    s = s * (HEAD_DIM ** -0.5) + bias_ref[h]
    s = jnp.where(valid, s, NEG)
    sink = sink_ref[0:1, h:h + 1]
    m = jnp.maximum(jnp.max(s, axis=-1, keepdims=True), sink)
    e = jnp.exp(s - m)
    es = jnp.exp(sink - m)
    inv = 1.0 / (jnp.sum(e, axis=-1, keepdims=True) + es)
    return e * inv, es * inv, qh


def _band_valid(n, T):
    row = lax.broadcasted_iota(jnp.int32, (BLOCK, 3 * BLOCK), 0)
    col = lax.broadcasted_iota(jnp.int32, (BLOCK, 3 * BLOCK), 1)
    rel = col - BLOCK - row
    key_pos = n * BLOCK + col - BLOCK
    return (jnp.abs(rel) <= BLOCK) & (key_pos >= 0) & (key_pos < T)


def _attn_fwd(z, kpad, vpad, bias, sink, *, name):
    T = z.shape[0]
    H = bias.shape[0]
    AW = H * HEAD_DIM
    group = H // N_KV_HEADS

    def body(q_ref, k_ref, v_ref, bias_ref, sink_ref, o_ref):
        n = pl.program_id(0)
        start = pl.multiple_of(n * BLOCK, BLOCK)
        kb = k_ref[pl.ds(start, 3 * BLOCK), :]
        vb = v_ref[pl.ds(start, 3 * BLOCK), :]
        valid = _band_valid(n, T)
        for h in range(H):
            kv = h // group
            p, _, _ = _attn_probs(q_ref, kb, bias_ref, sink_ref, valid, h, group)
            o = jnp.dot(p.astype(BF16), vb[:, kv * HEAD_DIM:(kv + 1) * HEAD_DIM], preferred_element_type=F32)
            o_ref[:, h * HEAD_DIM:(h + 1) * HEAD_DIM] = o.astype(BF16)

    full_kv = pl.BlockSpec((T + 2 * BLOCK, KV_WIDTH), lambda n: (0, 0))
    return pl.pallas_call(
        body, name=name, grid=(T // BLOCK,),
        in_specs=[pl.BlockSpec((BLOCK, AW), lambda n: (n, 2)), full_kv, full_kv,
                  pl.BlockSpec((H, BLOCK, 3 * BLOCK), lambda n: (0, 0, 0)), pl.BlockSpec((1, H), lambda n: (0, 0))],
        out_specs=pl.BlockSpec((BLOCK, AW), lambda n: (n, 0)),
        out_shape=jax.ShapeDtypeStruct((T, AW), BF16),
        compiler_params=_cparams("parallel"),
    )(z, kpad, vpad, bias, sink)


def _attn_bwd(z, kpad, vpad, bias, sink, do, dz, *, name):
    T = z.shape[0]
    H = bias.shape[0]
    AW = H * HEAD_DIM
    group = H // N_KV_HEADS
    scale = HEAD_DIM ** -0.5

    def body(q_ref, k_ref, v_ref, bias_ref, sink_ref, do_ref, dz_in_ref, dq_ref, dk_ref, dv_ref, dbias_ref, dsink_ref):
        n = pl.program_id(0)

        @pl.when(n == 0)
        def _():
            dk_ref[...] = jnp.zeros_like(dk_ref)
            dv_ref[...] = jnp.zeros_like(dv_ref)
            dbias_ref[...] = jnp.zeros_like(dbias_ref)
            dsink_ref[...] = jnp.zeros_like(dsink_ref)

        start = pl.multiple_of(n * BLOCK, BLOCK)
        kb = k_ref[pl.ds(start, 3 * BLOCK), :]
        vb = v_ref[pl.ds(start, 3 * BLOCK), :]
        valid = _band_valid(n, T)
        for kv in range(N_KV_HEADS):
            ksl = slice(kv * HEAD_DIM, (kv + 1) * HEAD_DIM)
            dk_acc = jnp.zeros((3 * BLOCK, HEAD_DIM), F32)
            dv_acc = jnp.zeros((3 * BLOCK, HEAD_DIM), F32)
            for gi in range(group):
                h = kv * group + gi
                hsl = slice(h * HEAD_DIM, (h + 1) * HEAD_DIM)
                p, p_sink, qh = _attn_probs(q_ref, kb, bias_ref, sink_ref, valid, h, group)
                doh = do_ref[:, hsl]
                dp = lax.dot_general(doh, vb[:, ksl], (((1,), (1,)), ((), ())), preferred_element_type=F32)
                delta = jnp.sum(p * dp, axis=-1, keepdims=True)
                ds = p * (dp - delta)
                dbias_ref[h] += ds
                dsink_ref[:, h:h + 1] += -(p_sink * delta)
                dsb = ds.astype(BF16)
                dq = jnp.dot(dsb, kb[:, ksl], preferred_element_type=F32) * scale
                dq_ref[:, hsl] = dq.astype(BF16)
                dk_acc = dk_acc + lax.dot_general(dsb, qh, (((0,), (0,)), ((), ())), preferred_element_type=F32)
                dv_acc = dv_acc + lax.dot_general(p.astype(BF16), doh, (((0,), (0,)), ((), ())),
                                                  preferred_element_type=F32)
            dk_ref[pl.ds(start, 3 * BLOCK), ksl] += dk_acc * scale
            dv_ref[pl.ds(start, 3 * BLOCK), ksl] += dv_acc

    full_kv = pl.BlockSpec((T + 2 * BLOCK, KV_WIDTH), lambda n: (0, 0))
    bias_spec = pl.BlockSpec((H, BLOCK, 3 * BLOCK), lambda n: (0, 0, 0))
    row = pl.BlockSpec((BLOCK, AW), lambda n: (n, 0))
    q_cols = pl.BlockSpec((BLOCK, AW), lambda n: (n, 2))
    return pl.pallas_call(
        body, name=name, grid=(T // BLOCK,),
        in_specs=[q_cols, full_kv, full_kv, bias_spec, pl.BlockSpec((1, H), lambda n: (0, 0)), row, _ANY],
        out_specs=[q_cols, full_kv, full_kv, bias_spec, pl.BlockSpec((BLOCK, H), lambda n: (0, 0))],
        out_shape=[jax.ShapeDtypeStruct(dz.shape, dz.dtype),
                   jax.ShapeDtypeStruct((T + 2 * BLOCK, KV_WIDTH), F32), jax.ShapeDtypeStruct((T + 2 * BLOCK, KV_WIDTH), F32),
                   jax.ShapeDtypeStruct((H, BLOCK, 3 * BLOCK), F32), jax.ShapeDtypeStruct((BLOCK, H), F32)],
        input_output_aliases={6: 0},
        compiler_params=_cparams("arbitrary"),
    )(z, kpad, vpad, bias, sink, do, dz)


def _dkv_into(dkp, dvp, dz, *, name):
    T = dz.shape[0]
    D = (dz.shape[1] - 2 * KV_WIDTH) * 2 // 7
    col = (D + D // 2) // (2 * KV_WIDTH)
    assert col * 2 * KV_WIDTH == D + D // 2

    def body(dk_ref, dv_ref, dz_in_ref, o_ref):
        o_ref[:, :KV_WIDTH] = dk_ref[...].astype(BF16)
        o_ref[:, KV_WIDTH:] = dv_ref[...].astype(BF16)

    kv = pl.BlockSpec((BLOCK, KV_WIDTH), lambda n: (n + 1, 0))
    return pl.pallas_call(
        body, name=name, grid=(T // BLOCK,),
        in_specs=[kv, kv, _ANY], out_specs=pl.BlockSpec((BLOCK, 2 * KV_WIDTH), lambda n: (n, col)),
        out_shape=jax.ShapeDtypeStruct(dz.shape, dz.dtype), input_output_aliases={2: 0},
        compiler_params=_cparams("parallel"),
    )(dkp, dvp, dz)


def _kv_pad(z, *, name):
    T = z.shape[0]
    D = (z.shape[1] - 2 * KV_WIDTH) * 2 // 7
    kcol = (D + D // 2) // KV_WIDTH
    nb = T // BLOCK

    def body(k_ref, v_ref, ko_ref, vo_ref):
        b = pl.program_id(0)
        inside = (b >= 1) & (b <= nb)
        ko_ref[...] = jnp.where(inside, k_ref[...], 0.0).astype(BF16)
        vo_ref[...] = jnp.where(inside, v_ref[...], 0.0).astype(BF16)

    out = jax.ShapeDtypeStruct((T + 2 * BLOCK, KV_WIDTH), BF16)
    o_spec = pl.BlockSpec((BLOCK, KV_WIDTH), lambda b: (b, 0))
    return pl.pallas_call(
        body, name=name, grid=(nb + 2,),
        in_specs=[pl.BlockSpec((BLOCK, KV_WIDTH), lambda b: (jnp.clip(b - 1, 0, nb - 1), kcol)),
                  pl.BlockSpec((BLOCK, KV_WIDTH), lambda b: (jnp.clip(b - 1, 0, nb - 1), kcol + 1))],
        out_specs=[o_spec, o_spec], out_shape=[out, out],
        compiler_params=_cparams("parallel"),
    )(z, z)


def _attn_small_grads(dbias, dsink_rows, bmap, *, name):
    H = dbias.shape[0]

    def body(dbias_ref, dsink_ref, bmap_ref, drel_ref, ds_ref):
        bm_ = bmap_ref[...]
        for h in range(H):
            d = dbias_ref[h]
            for b in range(REL_BUCKETS):
                drel_ref[b, h] = jnp.sum(jnp.where(bm_ == b, d, 0.0))
            ds_ref[0, h] = jnp.sum(dsink_ref[:, h:h + 1])

    vmem = pl.BlockSpec(memory_space=pltpu.VMEM)
    smem = pl.BlockSpec(memory_space=pltpu.SMEM)
    return pl.pallas_call(
        body, name=name, in_specs=[vmem, vmem, vmem], out_specs=[smem, smem],
        out_shape=[jax.ShapeDtypeStruct((REL_BUCKETS, H), F32), jax.ShapeDtypeStruct((1, H), F32)],
    )(dbias, dsink_rows, bmap)


def _local_step(x, target, weight, emit, flush, norm_mix, v_gain, w_s, b_s, sink, rel_bias, norm_ffn, norm_final):
    T, D = x.shape
    ws_b = w_s.astype(BF16)
    bs_t = b_s.T
    bmap = jnp.asarray(_bucket_map())

    h = _rms_fwd(x, norm_mix, name="rms_mix")
    w_in = weight("w_in", h)
    z = _mm(h, w_in, tb=True, name="mm_z", bm=2048, bn=768)
    a = _sgu_fwd(z, v_gain, ws_b, bs_t, name="sgu_fwd")
    w_a = weight("w_a_out", a)
    ya = _mm_w8(a, w_a, name="mm_ya", bm=2048)
    kpad, vpad = _kv_pad(z, name="kv_pad")
    bias = _bias_table(rel_bias, bmap, name="bias_table")
    o = _attn_fwd(z, kpad, vpad, bias, sink, name="attn_fwd")
    w_b = weight("w_b_out", o)
    yb = _mm_w8(o, w_b, name="mm_yb", bm=2048)
    m = _merge_fwd(z, ya, yb, name="merge_fwd")
    w_o = weight("w_o", m)
    x1 = _mm(m, w_o, name="mm_x1", add=x, bm=2048, bn=512)
    h2 = _rms_fwd(x1, norm_ffn, name="rms_ffn")
    w_gate = weight("w_gate", h2)
    w_up = weight("w_up", h2)
    gate, up, act = _swiglu_mm_fwd(h2, w_gate, w_up, name="mm_gate_up")
    w_down = weight("w_down", act)
    x2 = _mm(act, w_down, name="mm_x2", add=x1, bm=1024, bn=1024, bk=2816)
    loss, dx2, dx2b, g_norm_final = _loss_head(x2, norm_final, target, name="loss_head")

    g_w_down = _mm(act, dx2b, ta=True, out_dtype=BF16, name="mm_gwdown", bm=512, bn=2048)
    tok = emit(("w_down",), (g_w_down,))
    dgate, dup = _swiglu_mm_bwd(dx2b, w_down, gate, up, name="mm_dact_swiglu", after=tok)
    tok = flush(dgate)
    g_w_gate = _mm(dgate, h2, ta=True, out_dtype=BF16, name="mm_gwgate", bm=512, bn=2048)
    g_w_up = _mm(dup, h2, ta=True, out_dtype=BF16, name="mm_gwup", bm=512, bn=2048)
    tok = emit(("w_gate", "w_up"), (g_w_gate, g_w_up))
    dh2 = _mm(dgate, w_gate, name="mm_dh2a", bm=1024, bn=1024, bk=2816, after=tok)
    tok = flush(dh2)
    dh2 = _mm(dup, w_up, add=dh2, name="mm_dh2b", bm=1024, bn=1024, bk=2816, after=tok)
    dx1, dx1b, g_norm_ffn = _rms_bwd(x1, norm_ffn, dh2, dx2, name="rms_ffn_bwd", want_bf16=True)

    g_w_o = _mm(m, dx1b, ta=True, out_dtype=BF16, name="mm_gwo", bm=2048, bn=512)
    tok = emit(("w_o",), (g_w_o,))
    dm = _mm(dx1b, w_o, tb=True, name="mm_dm", bm=2048, bn=512, after=tok)
    tok = flush(dm)
    dy, dz = _merge_bwd(z, ya, yb, dm, name="merge_bwd", after=tok)
    g_w_a = _mm_gw8(a, dy, w_a.shape[2], name="mm_gwa", lead=0)
    g_w_b = _mm_gw8(o, dy, w_b.shape[2], name="mm_gwb", lead=1)
    tok = emit(("w_a_out", "w_b_out"), (g_w_a, g_w_b))
    da = _mm_w8t(dy, w_a, name="mm_da", bm=2048, bn=512, after=tok, lead=0)
    tok = flush(da)
    do = _mm_w8t(dy, w_b, out_dtype=BF16, name="mm_do", bm=2048, bn=512, after=tok, lead=1)
    dz, g_w_s, g_b_s_t, g_v_gain = _sgu_bwd(z, v_gain, ws_b, bs_t, da, dz, name="sgu_bwd")
    dz, dkp, dvp, dbias, dsink_rows = _attn_bwd(z, kpad, vpad, bias, sink, do, dz, name="attn_bwd")
    dz = _dkv_into(dkp, dvp, dz, name="dkv_into_dz")
    g_rel_bias, g_sink = _attn_small_grads(dbias, dsink_rows, bmap, name="attn_small_grads")
    g_w_in = _mm(dz, h, ta=True, out_dtype=BF16, name="mm_gwin", bm=768, bn=2048)
    tok = emit(("w_in",), (g_w_in,))
    dh = _mm(dz, w_in, name="mm_dh", bm=1024, bn=1024, bk=2560, after=tok)
    tok = flush(dh)
    grad_x, g_norm_mix = _rms_bwd(x, norm_mix, dh, dx1, name="rms_mix_bwd", want_bf16=False, after=tok)

    small = dict(norm_mix=g_norm_mix, sgu_v_gain=g_v_gain, sgu_w_s=g_w_s, sgu_b_s=g_b_s_t.T, attn_sink=g_sink,
                 rel_bias=g_rel_bias, norm_ffn=g_norm_ffn, norm_final=g_norm_final)
    return loss, grad_x, small


def _position():
    return lax.axis_index("x"), lax.axis_index("y"), lax.axis_index("c")


def _other_chips(x, y):
    return [(1 - x, y), (x, 1 - y), (1 - x, 1 - y)]


def _slot(px, py, pc):
    return 4 * px + 2 * py + pc


_HBM = pl.BlockSpec(memory_space=pltpu.HBM)
_SEM = pl.BlockSpec(memory_space=pltpu.SEMAPHORE)
_DATAFLOW = pltpu.SideEffectType.DATAFLOW_SIDE_EFFECTING


def _in_hbm(a):
    return pltpu.with_memory_space_constraint(a, pltpu.HBM)


def _own_slot(shard, pos, *, name, after=None):
    R, C = shard.shape
    tr = _div(R, 256, 16)

    def body(pos_ref, w_ref, o_ref):
        o_ref[...] = w_ref[...].astype(BF16)

    body, in_specs, args = _ordered_after(body, 2, [pl.BlockSpec((tr, C), lambda i, pos_ref: (i, 0))], (pos, shard), after)
    grid_spec = pltpu.PrefetchScalarGridSpec(
        num_scalar_prefetch=1, grid=(R // tr,), in_specs=in_specs,
        out_specs=pl.BlockSpec((None, tr, C), lambda i, pos_ref: (pos_ref[0], i, 0)))
    return pl.pallas_call(
        body, name=name, grid_spec=grid_spec,
        out_shape=jax.ShapeDtypeStruct((N_DEV, R, C), BF16),
        compiler_params=_cparams("parallel"),
    )(*args)


def _ag_copies(w, land_ref, send_sems, recv_sems):
    x, y, c = _position()
    mine = land_ref.at[_slot(x, y, c)]
    targets = [(px, py, c) for px, py in _other_chips(x, y)] + [(x, y, 1 - c)]
    return [pltpu.make_async_remote_copy(src_ref=mine, dst_ref=mine, send_sem=send_sems.at[4 * w + k],
                                         recv_sem=recv_sems.at[4 * w + k], device_id=to, device_id_type=MESH)
            for k, to in enumerate(targets)]


def _ag_start(buffers, groups, *, name):
    lands = [buffers[i] for g in groups for i in g]
    n, ng = len(lands), len(groups)
    sizes = [len(g) for g in groups]

    def body(*refs):
        land_refs = refs[:n]
        sems = refs[n:n + 2 * ng]
        token = refs[-1]
        i = 0
        for g in range(ng):
            for w in range(sizes[g]):
                for cp in _ag_copies(w, land_refs[i], sems[2 * g], sems[2 * g + 1]):
                    cp.start()
                i += 1
        token[...] = jnp.zeros_like(token)

    sem_shapes = [pltpu.SemaphoreType.DMA((4 * k,)) for k in sizes for _ in range(2)]
    outs = pl.pallas_call(
        body, name=name,
        in_specs=[_HBM] * n,
        out_specs=tuple([_SEM] * (2 * ng) + [_HBM] * n + [pl.BlockSpec(memory_space=pltpu.VMEM)]),
        out_shape=tuple(sem_shapes + [pltpu.HBM(a.shape, a.dtype) for a in lands] + [jax.ShapeDtypeStruct((8, LANES), F32)]),
        input_output_aliases={i: 2 * ng + i for i in range(n)},
        compiler_params=pltpu.CompilerParams(has_side_effects=_DATAFLOW),
    )(*[_in_hbm(a) for a in lands])
    sems, thru = outs[:2 * ng], outs[2 * ng:2 * ng + n]
    result, i = [], 0
    for g in range(ng):
        k = sizes[g]
        result.append((sems[2 * g], sems[2 * g + 1], list(thru[i:i + k])))
        i += k
    return result, outs[-1]


def _ag_wait(send_sems, recv_sems, lands, after, *, name):
    n = len(lands)

    def body(*refs):
        land_refs = refs[:n]
        send_ref, recv_ref = refs[n], refs[n + 1]
        token = refs[-1]
        for w in range(n):
            for cp in _ag_copies(w, land_refs[w], send_ref, recv_ref):
                cp.wait_send()
                cp.wait_recv()
        token[...] = jnp.zeros_like(token)

    outs = pl.pallas_call(
        body, name=name,
        in_specs=[_HBM] * n + [_SEM, _SEM, _ANY],
        out_specs=tuple([_HBM] * n + [pl.BlockSpec(memory_space=pltpu.VMEM)]),
        out_shape=tuple([pltpu.HBM(a.shape, a.dtype) for a in lands] + [jax.ShapeDtypeStruct((8, LANES), F32)]),
        input_output_aliases={i: i for i in range(n)},
        compiler_params=pltpu.CompilerParams(has_side_effects=_DATAFLOW),
    )(*lands, send_sems, recv_sems, after)
    return list(outs[:n]), outs[n]


def _ag_forward(lands, *, name, after=None):
    n = len(lands)

    def body(*refs):
        in_refs, out_refs = refs[:n], refs[n:2 * n]
        send_sems, recv_sems = refs[2 * n:]
        x, y, c = _position()
        copies = []
        for w in range(n):
            for k, (px, py) in enumerate(_other_chips(x, y)):
                cp = pltpu.make_async_remote_copy(
                    src_ref=in_refs[w].at[_slot(px, py, c)], dst_ref=out_refs[w].at[_slot(px, py, c)],
                    send_sem=send_sems.at[3 * w + k], recv_sem=recv_sems.at[3 * w + k],
                    device_id=(x, y, 1 - c), device_id_type=MESH)
                cp.start()
                copies.append(cp)
        for cp in copies:
            cp.wait()

    body, in_specs, args = _ordered_after(body, n, [_ANY] * n, tuple(lands), after)
    return pl.pallas_call(
        body, name=name,
        in_specs=in_specs, out_specs=[_ANY] * n,
        out_shape=[jax.ShapeDtypeStruct(a.shape, a.dtype) for a in lands],
        input_output_aliases={i: i for i in range(n)},
        scratch_shapes=[pltpu.SemaphoreType.DMA((3 * n,)), pltpu.SemaphoreType.DMA((3 * n,))],
    )(*args)


def _sibling_copies(w, g8_ref, land_ref, send_sems, recv_sems):
    x, y, c = _position()
    return [pltpu.make_async_remote_copy(src_ref=g8_ref.at[2 * p + (1 - c)], dst_ref=land_ref.at[p],
                                         send_sem=send_sems.at[4 * w + p], recv_sem=recv_sems.at[4 * w + p],
                                         device_id=(x, y, 1 - c), device_id_type=MESH)
            for p in range(4)]


def _chip_copies(w, sums_ref, land_ref, send_sems, recv_sems):
    x, y, c = _position()
    return [pltpu.make_async_remote_copy(src_ref=sums_ref.at[2 * px + py], dst_ref=land_ref.at[k],
                                         send_sem=send_sems.at[3 * w + k], recv_sem=recv_sems.at[3 * w + k],
                                         device_id=(px, py, c), device_id_type=MESH)
            for k, (px, py) in enumerate(_other_chips(x, y))]


def _copies_start(copies, per_weight, srcs, *, name):
    n = len(srcs)
    lands = [lax.empty((per_weight,) + s.shape[1:], s.dtype) for s in srcs]

    def body(*refs):
        src_refs, land_refs = refs[:n], refs[n:2 * n]
        send_sems, recv_sems = refs[2 * n], refs[2 * n + 1]
        token = refs[-1]
        for w in range(n):
            for cp in copies(w, src_refs[w], land_refs[w], send_sems, recv_sems):
                cp.start()
        token[...] = jnp.zeros_like(token)

    outs = pl.pallas_call(
        body, name=name,
        in_specs=[_HBM] * (2 * n),
        out_specs=tuple([_SEM, _SEM] + [_HBM] * (2 * n) + [pl.BlockSpec(memory_space=pltpu.VMEM)]),
        out_shape=tuple([pltpu.SemaphoreType.DMA((per_weight * n,)), pltpu.SemaphoreType.DMA((per_weight * n,))]
                        + [pltpu.HBM(a.shape, a.dtype) for a in srcs + lands] + [jax.ShapeDtypeStruct((8, LANES), F32)]),
        input_output_aliases={i: 2 + i for i in range(2 * n)},
        compiler_params=pltpu.CompilerParams(has_side_effects=_DATAFLOW),
    )(*[_in_hbm(a) for a in srcs + lands])
    return outs[0], outs[1], list(outs[2:2 + n]), list(outs[2 + n:2 + 2 * n]), outs[-1]


def _copies_wait(copies, send_sems, recv_sems, srcs, lands, after, *, name):
    n = len(srcs)

    def body(*refs):
        src_refs, land_refs = refs[:n], refs[n:2 * n]
        send_ref, recv_ref = refs[2 * n], refs[2 * n + 1]
        for w in range(n):
            for cp in copies(w, src_refs[w], land_refs[w], send_ref, recv_ref):
                cp.wait_send()
                cp.wait_recv()

    outs = pl.pallas_call(
        body, name=name,
        in_specs=[_HBM] * (2 * n) + [_SEM, _SEM, _ANY],
        out_specs=tuple([_HBM] * (2 * n)),
        out_shape=tuple(pltpu.HBM(a.shape, a.dtype) for a in srcs + lands),
        input_output_aliases={i: i for i in range(2 * n)},
        compiler_params=pltpu.CompilerParams(has_side_effects=_DATAFLOW),
    )(*srcs, *lands, send_sems, recv_sems, after)
    return list(outs[:n]), list(outs[n:])


def _chip_sums(g8, from_sibling, pos, *, name):
    _, R, C = g8.shape
    tr = _div(R, 512, 16)

    def body(pos_ref, g_ref, s_ref, o_ref):
        o_ref[...] = (g_ref[...].astype(F32) + s_ref[...].astype(F32)).astype(BF16)

    grid_spec = pltpu.PrefetchScalarGridSpec(
        num_scalar_prefetch=1, grid=(4, R // tr),
        in_specs=[pl.BlockSpec((None, tr, C), lambda p, i, pos_ref: (2 * p + pos_ref[2], i, 0)),
                  pl.BlockSpec((None, tr, C), lambda p, i, pos_ref: (p, i, 0))],
        out_specs=pl.BlockSpec((None, tr, C), lambda p, i, pos_ref: (p, i, 0)))
    return pl.pallas_call(
        body, name=name, grid_spec=grid_spec,
        out_shape=jax.ShapeDtypeStruct((4, R, C), BF16),
        compiler_params=_cparams("parallel", "parallel"),
    )(pos, g8, from_sibling)


def _small_all_reduce(packed, after, *, name):
    R, L = packed.shape

    def body(x_ref, sum_ref, gath_ref, send_sems, recv_sems, local_sem):
        x, y, c = _position()
        me, sibling = (x, y, c), (x, y, 1 - c)
        chips = _other_chips(x, y)

        def rows(px, py, pc):
            return gath_ref.at[pl.ds(_slot(px, py, pc) * R, R), :]

        def copy(k, block, to, src=None):
            return pltpu.make_async_remote_copy(
                src_ref=rows(*block) if src is None else src, dst_ref=rows(*block),
                send_sem=send_sems.at[k], recv_sem=recv_sems.at[k], device_id=to, device_id_type=MESH)

        mine = pltpu.make_async_copy(x_ref, rows(*me), local_sem)
        mine.start()
        first = [copy(0, me, sibling, src=x_ref)]
        first += [copy(1 + j, me, (*chip, c), src=x_ref) for j, chip in enumerate(chips)]
        for cp in first:
            cp.start()
        passed = [copy(4 + j, (*chip, c), sibling) for j, chip in enumerate(chips)]
        for j, chip in enumerate(chips):
            copy(1 + j, (*chip, c), me).wait_recv()
            passed[j].start()
        copy(0, sibling, me).wait_recv()
        for j, chip in enumerate(chips):
            copy(4 + j, (*chip, 1 - c), me).wait_recv()
        for cp in first + passed:
            cp.wait_send()
        mine.wait()
        acc = gath_ref[0:R, :]
        for d in range(1, N_DEV):
            acc = acc + gath_ref[d * R:(d + 1) * R, :]
        sum_ref[...] = acc

    vmem = pl.BlockSpec(memory_space=pltpu.VMEM)
    body, in_specs, args = _ordered_after(body, 1, [vmem], (packed,), after)
    return pl.pallas_call(
        body, name=name, in_specs=in_specs, out_specs=vmem,
        out_shape=jax.ShapeDtypeStruct((R, L), F32),
        scratch_shapes=[pltpu.VMEM((N_DEV * R, L), F32), pltpu.SemaphoreType.DMA((7,)), pltpu.SemaphoreType.DMA((7,)),
                        pltpu.SemaphoreType.DMA],
        compiler_params=pltpu.CompilerParams(vmem_limit_bytes=VMEM_LIMIT),
    )(*args)


def _adamw_math(w, g, m, v):
    m = ADAM_B1 * m + (1.0 - ADAM_B1) * g
    v = ADAM_B2 * v + (1.0 - ADAM_B2) * (g * g)
    m_hat = m / (1.0 - ADAM_B1 ** ADAM_STEP)
    v_hat = v / (1.0 - ADAM_B2 ** ADAM_STEP)
    delta = -ADAM_LR * (m_hat / (jnp.sqrt(v_hat) + ADAM_EPS) + ADAM_WD * w)
    return delta, m, v


def _adamw_shard(w, m, v, g8, from_sibling, from_chips, pos, *, name):
    R, C = w.shape
    tr = _div(R, 256, 16)

    def body(pos_ref, w_ref, m_ref, v_ref, g_ref, s_ref, r_ref, go_ref, d_ref, mo_ref, vo_ref):
        g = g_ref[...].astype(F32) + s_ref[...].astype(F32)
        for k in range(3):
            g = g + r_ref[k].astype(F32)
        delta, m_, v_ = _adamw_math(w_ref[...], g, m_ref[...], v_ref[...])
        go_ref[...] = g
        d_ref[...] = delta
        mo_ref[...] = m_
        vo_ref[...] = v_

    blk = pl.BlockSpec((tr, C), lambda i, pos_ref: (i, 0))
    grid_spec = pltpu.PrefetchScalarGridSpec(
        num_scalar_prefetch=1, grid=(R // tr,),
        in_specs=[blk, blk, blk,
                  pl.BlockSpec((None, tr, C), lambda i, pos_ref: (pos_ref[0], i, 0)),
                  pl.BlockSpec((None, tr, C), lambda i, pos_ref: (pos_ref[1], i, 0)),
                  pl.BlockSpec((3, tr, C), lambda i, pos_ref: (0, i, 0))],
        out_specs=[blk] * 4)
    out = jax.ShapeDtypeStruct((R, C), F32)
    return pl.pallas_call(
        body, name=name, grid_spec=grid_spec, out_shape=[out] * 4,
        compiler_params=_cparams("parallel"),
    )(pos, w, m, v, g8, from_sibling, from_chips)


def _adamw_small(w, g, m, v, *, name):
    R, L = w.shape

    def body(w_ref, g_ref, m_ref, v_ref, d_ref, mo_ref, vo_ref):
        delta, m_, v_ = _adamw_math(w_ref[...], g_ref[...], m_ref[...], v_ref[...])
        d_ref[...] = delta
        mo_ref[...] = m_
        vo_ref[...] = v_

    vmem = pl.BlockSpec(memory_space=pltpu.VMEM)
    out = jax.ShapeDtypeStruct((R, L), F32)
    return pl.pallas_call(body, name=name, in_specs=[vmem] * 4, out_specs=[vmem] * 3, out_shape=[out] * 3)(w, g, m, v)


_TILE = 8 * LANES


def _pack(pieces):
    rows = []
    for p in pieces:
        flat = p.reshape(-1).astype(F32)
        padded = -(-flat.shape[0] // _TILE) * _TILE
        rows.append(jnp.pad(flat, (0, padded - flat.shape[0])).reshape(-1, LANES))
    return jnp.concatenate(rows, axis=0)


def _unpack(packed, like):
    out, r = [], 0
    for p in like:
        size = int(np.prod(p.shape)) if p.shape else 1
        nrows = -(-size // _TILE) * 8
        out.append(packed[r:r + nrows].reshape(-1)[:size].reshape(p.shape))
        r += nrows
    return out


_BIG = ("w_in", "w_a_out", "w_b_out", "w_o", "w_gate", "w_up", "w_down")
_TRANSPOSED = ("w_in", "w_gate", "w_up")
_COL_SHARDED = ("w_a_out", "w_b_out")
_GATHER_GROUPS = (("w_in",), ("w_a_out", "w_b_out", "w_o"), ("w_gate", "w_up"), ("w_down",))
_START_AFTER_WAIT = {0: (1, 2), 2: (3,)}
_SMALL = ("norm_mix", "sgu_v_gain", "sgu_w_s", "sgu_b_s", "attn_sink", "rel_bias", "norm_ffn", "norm_final")
_ORDER = ("w_in", "norm_mix", "sgu_v_gain", "sgu_w_s", "sgu_b_s", "w_a_out", "attn_sink", "rel_bias", "w_b_out", "w_o",
          "norm_ffn", "w_gate", "w_up", "w_down", "norm_final")


def _shard(name, a):
    return jnp.swapaxes(a, 1, 2)[0] if name in _TRANSPOSED else a[0]


def _unshard(name, a):
    return jnp.swapaxes(a[None], 1, 2) if name in _TRANSPOSED else a[None]


def _whole(name, gathered):
    _, r, c = gathered.shape
    return gathered if name in _COL_SHARDED else gathered.reshape(N_DEV * r, c)


def _blocks(name, grad):
    if name in _COL_SHARDED:
        return grad
    r, c = grad.shape
    return grad.reshape(N_DEV, r // N_DEV, c)


def kernel(x, w_in, norm_mix, sgu_v_gain, sgu_w_s, sgu_b_s, w_a_out, attn_sink, rel_bias, w_b_out, w_o, norm_ffn, w_gate, w_up, w_down, norm_final, loss_target, m_w_in, m_norm_mix, m_sgu_v_gain, m_sgu_w_s, m_sgu_b_s, m_w_a_out, m_attn_sink, m_rel_bias, m_w_b_out, m_w_o, m_norm_ffn, m_w_gate, m_w_up, m_w_down, m_norm_final, v_w_in, v_norm_mix, v_sgu_v_gain, v_sgu_w_s, v_sgu_b_s, v_w_a_out, v_attn_sink, v_rel_bias, v_w_b_out, v_w_o, v_norm_ffn, v_w_gate, v_w_up, v_w_down, v_norm_final):
    w = dict(w_in=w_in, norm_mix=norm_mix, sgu_v_gain=sgu_v_gain, sgu_w_s=sgu_w_s, sgu_b_s=sgu_b_s, w_a_out=w_a_out,
             attn_sink=attn_sink, rel_bias=rel_bias, w_b_out=w_b_out, w_o=w_o, norm_ffn=norm_ffn, w_gate=w_gate,
             w_up=w_up, w_down=w_down, norm_final=norm_final)
    m = dict(w_in=m_w_in, norm_mix=m_norm_mix, sgu_v_gain=m_sgu_v_gain, sgu_w_s=m_sgu_w_s, sgu_b_s=m_sgu_b_s,
             w_a_out=m_w_a_out, attn_sink=m_attn_sink, rel_bias=m_rel_bias, w_b_out=m_w_b_out, w_o=m_w_o,
             norm_ffn=m_norm_ffn, w_gate=m_w_gate, w_up=m_w_up, w_down=m_w_down, norm_final=m_norm_final)
    v = dict(w_in=v_w_in, norm_mix=v_norm_mix, sgu_v_gain=v_sgu_v_gain, sgu_w_s=v_sgu_w_s, sgu_b_s=v_sgu_b_s,
             w_a_out=v_w_a_out, attn_sink=v_attn_sink, rel_bias=v_rel_bias, w_b_out=v_w_b_out, w_o=v_w_o,
             norm_ffn=v_norm_ffn, w_gate=v_w_gate, w_up=v_w_up, w_down=v_w_down, norm_final=v_norm_final)
    xc, yc, cc = _position()
    pos = jnp.stack([_slot(xc, yc, cc), 2 * xc + yc, cc]).astype(jnp.int32)

    in_flight, full = {}, {}

    def start_gather(groups, after):
        names = [n for gi in groups for n in _GATHER_GROUPS[gi]]
        buffers = [_own_slot(_shard(n, w[n]), pos, name="own_slot_" + n, after=after) for n in names]
        flights, token = _ag_start(buffers, [[names.index(n) for n in _GATHER_GROUPS[gi]] for gi in groups],
                                   name="ag_start_%d" % groups[0])
        in_flight.update(zip(groups, flights))
        return token

    def weight(name, after):
        if name not in full:
            gi = next(i for i, grp in enumerate(_GATHER_GROUPS) if name in grp)
            send_sems, recv_sems, lands = in_flight[gi]
            lands, token = _ag_wait(send_sems, recv_sems, lands, after, name="ag_wait_%d" % gi)
            started = start_gather(_START_AFTER_WAIT[gi], token) if gi in _START_AFTER_WAIT else None
            gathered = _ag_forward(lands, name="ag_forward_%d" % gi, after=started)
            full.update({n: _whole(n, g) for n, g in zip(_GATHER_GROUPS[gi], gathered)})
        return full[name]

    start_gather((0,), None)

    to_sibling, reducing = [], {}

    def emit(names, grads):
        g8 = [_blocks(n, g) for n, g in zip(names, grads)]
        send_sems, recv_sems, g8, lands, token = _copies_start(_sibling_copies, 4, g8, name="rs_sibling_start_" + names[0])
        to_sibling.append((names, send_sems, recv_sems, g8, lands))
        return token

    def flush(after):
        names, send_sems, recv_sems, g8, lands = to_sibling.pop()
        g8, from_sibling = _copies_wait(_sibling_copies, send_sems, recv_sems, g8, lands, after,
                                        name="rs_sibling_wait_" + names[0])
        sums4 = [_chip_sums(g, s, pos, name="chip_sums_" + n) for n, g, s in zip(names, g8, from_sibling)]
        send_sems, recv_sems, sums4, lands, token = _copies_start(_chip_copies, 3, sums4, name="rs_chips_start_" + names[0])
        reducing[names] = (g8, from_sibling, send_sems, recv_sems, sums4, lands)
        return token

    loss, grad_x, small_grads_local = _local_step(
        x[0], loss_target[0], weight, emit, flush, norm_mix, sgu_v_gain, sgu_w_s[0], sgu_b_s[0], attn_sink, rel_bias,
        norm_ffn, norm_final[None])

    out_g, out_d, out_m, out_v = {}, {}, {}, {}
    small_like = [w[n] for n in _SMALL]
    small_w = _pack(small_like)
    packed = _pack([small_grads_local[n] for n in _SMALL] + [loss[0, 0]])
    after = grad_x
    for gi, (names, (g8, from_sibling, send_sems, recv_sems, sums4, lands)) in enumerate(reducing.items()):
        _, from_chips = _copies_wait(_chip_copies, send_sems, recv_sems, sums4, lands, after,
                                     name="rs_chips_wait_" + names[0])
        if gi == len(reducing) - 1:
            summed = _small_all_reduce(packed, from_chips[0], name="small_all_reduce")
        for i, n in enumerate(names):
            g, d, m_, v_ = _adamw_shard(_shard(n, w[n]), _shard(n, m[n]), _shard(n, v[n]), g8[i], from_sibling[i],
                                        from_chips[i], pos, name="adamw_" + n)
            out_g[n], out_d[n], out_m[n], out_v[n] = (_unshard(n, o) for o in (g, d, m_, v_))
            after = d
    *small_grads, loss_sum = _unpack(summed, small_like + [jax.ShapeDtypeStruct((), F32)])
    d_s, m_s, v_s = _adamw_small(small_w, summed[:small_w.shape[0]], _pack([m[n] for n in _SMALL]),
                                 _pack([v[n] for n in _SMALL]), name="adamw_small")
    for n, g, d, m_, v_ in zip(_SMALL, small_grads, _unpack(d_s, small_like), _unpack(m_s, small_like), _unpack(v_s, small_like)):
        out_g[n], out_d[n], out_m[n], out_v[n] = g, d, m_, v_

    return (loss_sum, grad_x[None], *[out_g[n] for n in _ORDER], *[out_d[n] for n in _ORDER],
            *[out_m[n] for n in _ORDER], *[out_v[n] for n in _ORDER])
```

```python
import functools
import math

import numpy as np
import jax
import jax.numpy as jnp
from jax import lax
from jax.experimental import pallas as pl
from jax.experimental.pallas import tpu as pltpu

F32 = jnp.float32
BF16 = jnp.bfloat16

EPS = 1e-6
NEG = -1e30
HEAD_DIM = 128
BLOCK = 128
N_KV_HEADS = 2
KV_WIDTH = N_KV_HEADS * HEAD_DIM
REL_BUCKETS = 32
REL_MAX_DIST = 128

ADAM_LR = 0.001
ADAM_B1 = 0.9
ADAM_B2 = 0.999
ADAM_EPS = 1e-08
ADAM_WD = 0.01
ADAM_STEP = 10

N_DEV = 8
LANES = 128
VMEM_LIMIT = 56 * 1024 * 1024
MESH = pl.DeviceIdType.MESH


def _cparams(*sem):
    return pltpu.CompilerParams(dimension_semantics=sem, vmem_limit_bytes=VMEM_LIMIT)


def _div(n, target, mult=LANES):
    best = None
    for d in range(mult, min(n, target) + 1, mult):
        if n % d == 0:
            best = d
    assert best is not None, (n, target, mult)
    return best


_ANY = pl.BlockSpec(memory_space=pl.ANY)


def _ordered_after(body, n_inputs, in_specs, args, after):
    if after is None:
        return body, in_specs, args

    def wrapped(*refs):
        return body(*refs[:n_inputs], *refs[n_inputs + 1:])

    return wrapped, list(in_specs) + [_ANY], tuple(args) + (after,)


def _bucket_map():
    nb = REL_BUCKETS // 2
    qi = np.arange(BLOCK)[:, None]
    kj = np.arange(3 * BLOCK)[None, :]
    rel = kj - BLOCK - qi
    ret = np.where(rel > 0, nb, 0)
    n = np.abs(rel)
    max_exact = nb // 2
    nf = np.maximum(n, 1).astype(np.float32)
    large = max_exact + (np.log(nf / np.float32(max_exact)) / np.float32(math.log(REL_MAX_DIST / max_exact))
                         * np.float32(nb - max_exact)).astype(np.int32)
    large = np.minimum(large, nb - 1)
    return (ret + np.where(n < max_exact, n, large)).astype(np.int32)


_GELU_C = math.sqrt(2.0 / math.pi)
_GELU_A = 0.044715


def _gelu(x):
    t = jnp.tanh(_GELU_C * (x + _GELU_A * (x * x * x)))
    return 0.5 * x * (1.0 + t)


def _gelu_and_grad(x):
    x2 = x * x
    t = jnp.tanh(_GELU_C * (x + _GELU_A * (x2 * x)))
    g = 0.5 * x * (1.0 + t)
    dg = 0.5 * (1.0 + t) + 0.5 * x * (1.0 - t * t) * (_GELU_C * (1.0 + 3.0 * _GELU_A * x2))
    return g, dg


def _sigmoid(x):
    return 1.0 / (1.0 + jnp.exp(-x))


def _mm(a, b, *, name, ta=False, tb=False, add=None, out_dtype=F32, bm=1024, bn=1024, bk=None, after=None):
    if ta:
        K, M = a.shape
    else:
        M, K = a.shape
    N = b.shape[0] if tb else b.shape[1]
    assert (b.shape[1] if tb else b.shape[0]) == K
    bm = _div(M, bm)
    bn = _div(N, bn)
    bk = K if bk is None else _div(K, bk)
    nk = K // bk
    a_spec = pl.BlockSpec((bk, bm), lambda i, j, k: (k, i)) if ta else pl.BlockSpec((bm, bk), lambda i, j, k: (i, k))
    b_spec = pl.BlockSpec((bn, bk), lambda i, j, k: (j, k)) if tb else pl.BlockSpec((bk, bn), lambda i, j, k: (k, j))
    o_spec = pl.BlockSpec((bm, bn), lambda i, j, k: (i, j))
    dims = (((0 if ta else 1,), (1 if tb else 0,)), ((), ()))
    has_add = add is not None

    def body(*refs):
        if has_add:
            a_ref, b_ref, add_ref, o_ref, *scratch = refs
        else:
            a_ref, b_ref, o_ref, *scratch = refs
            add_ref = None
        p = lax.dot_general(a_ref[...].astype(BF16), b_ref[...].astype(BF16), dims, preferred_element_type=F32)
        if nk == 1:
            if has_add:
                p = p + add_ref[...]
            o_ref[...] = p.astype(out_dtype)
        else:
            acc = scratch[0]
            k = pl.program_id(2)

            @pl.when(k == 0)
            def _():
                acc[...] = p

            @pl.when(k > 0)
            def _():
                acc[...] += p

            @pl.when(k == nk - 1)
            def _():
                r = acc[...]
                if has_add:
                    r = r + add_ref[...]
                o_ref[...] = r.astype(out_dtype)

    in_specs = [a_spec, b_spec] + ([o_spec] if has_add else [])
    args = (a, b) + ((add,) if has_add else ())
    body, in_specs, args = _ordered_after(body, len(args), in_specs, args, after)
    return pl.pallas_call(
        body, name=name, grid=(M // bm, N // bn, nk),
        in_specs=in_specs, out_specs=o_spec,
        out_shape=jax.ShapeDtypeStruct((M, N), out_dtype),
        scratch_shapes=[pltpu.VMEM((bm, bn), F32)] if nk > 1 else [],
        compiler_params=_cparams("parallel", "parallel", "arbitrary"),
    )(*args)


def _blocks_per_tile(c):
    nb = 1
    while (nb * c) % LANES or (nb * c < 1024 and nb < N_DEV):
        nb *= 2
    assert nb <= N_DEV and (nb * c) % LANES == 0, c
    return nb


def _mm_w8(a, w8, *, name, bm=1024):
    M, K = a.shape
    _, _, c = w8.shape
    nb = _blocks_per_tile(c)
    bm = _div(M, bm)

    def body(a_ref, w_ref, o_ref):
        a_ = a_ref[...]
        for t in range(nb):
            o_ref[:, t * c:(t + 1) * c] = jnp.dot(a_, w_ref[t], preferred_element_type=F32)

    return pl.pallas_call(
        body, name=name, grid=(M // bm, N_DEV // nb),
        in_specs=[pl.BlockSpec((bm, K), lambda i, j: (i, 0)), pl.BlockSpec((nb, K, c), lambda i, j: (j, 0, 0))],
        out_specs=pl.BlockSpec((bm, nb * c), lambda i, j: (i, j)),
        out_shape=jax.ShapeDtypeStruct((M, N_DEV * c), F32),
        compiler_params=_cparams("parallel", "parallel"),
    )(a, w8)


def _mm_w8t(dy, w8, *, name, add=None, out_dtype=F32, bm=1024, bn=1024, after=None, lead=None):
    M = dy.shape[-2]
    _, K, c = w8.shape
    nb = _blocks_per_tile(c)
    nk = N_DEV // nb
    bm, bn = _div(M, bm), _div(K, bn)
    has_add = add is not None
    dims = (((1,), (1,)), ((), ()))

    def body(*refs):
        if has_add:
            dy_ref, w_ref, add_ref, o_ref, acc = refs
        else:
            dy_ref, w_ref, o_ref, acc = refs
        p = lax.dot_general(dy_ref[:, 0:c], w_ref[0], dims, preferred_element_type=F32)
        for t in range(1, nb):
            p = p + lax.dot_general(dy_ref[:, t * c:(t + 1) * c], w_ref[t], dims, preferred_element_type=F32)
        k = pl.program_id(2)

        @pl.when(k == 0)
        def _():
            acc[...] = p

        @pl.when(k > 0)
        def _():
            acc[...] += p

        @pl.when(k == nk - 1)
        def _():
            r = acc[...]
            if has_add:
                r = r + add_ref[...]
            o_ref[...] = r.astype(out_dtype)

    o_spec = pl.BlockSpec((bm, bn), lambda i, j, k: (i, j))
    dy_spec = (pl.BlockSpec((bm, nb * c), lambda i, j, k: (i, k)) if lead is None
               else pl.BlockSpec((None, bm, nb * c), lambda i, j, k: (lead, i, k)))
    in_specs = [dy_spec, pl.BlockSpec((nb, bn, c), lambda i, j, k: (k, j, 0))]
    in_specs += [o_spec] if has_add else []
    args = (dy, w8) + ((add,) if has_add else ())
    body, in_specs, args = _ordered_after(body, len(args), in_specs, args, after)
    return pl.pallas_call(
        body, name=name, grid=(M // bm, K // bn, nk),
        in_specs=in_specs, out_specs=o_spec,
        out_shape=jax.ShapeDtypeStruct((M, K), out_dtype),
        scratch_shapes=[pltpu.VMEM((bm, bn), F32)],
        compiler_params=_cparams("parallel", "parallel", "arbitrary"),
    )(*args)


def _mm_gw8(x, dy, c, *, name, bk=1024, lead=None):
    T, K = x.shape
    nb = _blocks_per_tile(c)
    bk = _div(K, bk)
    dims = (((0,), (0,)), ((), ()))

    def body(x_ref, dy_ref, o_ref):
        x_ = x_ref[...]
        for t in range(nb):
            o_ref[t] = lax.dot_general(x_, dy_ref[:, t * c:(t + 1) * c], dims, preferred_element_type=F32).astype(BF16)

    dy_spec = (pl.BlockSpec((T, nb * c), lambda i, j: (0, j)) if lead is None
               else pl.BlockSpec((None, T, nb * c), lambda i, j: (lead, 0, j)))
    return pl.pallas_call(
        body, name=name, grid=(K // bk, N_DEV // nb),
        in_specs=[pl.BlockSpec((T, bk), lambda i, j: (0, i)), dy_spec],
        out_specs=pl.BlockSpec((nb, bk, c), lambda i, j: (j, i, 0)),
        out_shape=jax.ShapeDtypeStruct((N_DEV, K, c), BF16),
        compiler_params=_cparams("parallel", "parallel"),
    )(x, dy)


def _rms_fwd(x, g, *, name):
    T, D = x.shape
    tm = _div(T, 256, 8)

    def body(x_ref, g_ref, h_ref):
        xf = x_ref[...]
        r = lax.rsqrt(jnp.mean(xf * xf, axis=-1, keepdims=True) + EPS)
        h_ref[...] = ((xf * r) * g_ref[...]).astype(BF16)

    return pl.pallas_call(
        body, name=name, grid=(T // tm,),
        in_specs=[pl.BlockSpec((tm, D), lambda i: (i, 0)), pl.BlockSpec((1, D), lambda i: (0, 0))],
        out_specs=pl.BlockSpec((tm, D), lambda i: (i, 0)),
        out_shape=jax.ShapeDtypeStruct((T, D), BF16),
        compiler_params=_cparams("parallel"),
    )(x, g)


def _rms_bwd(x, g, dh, dres, *, name, want_bf16, after=None):
    T, D = x.shape
    tm = _div(T, 256, 8)

    def body(x_ref, g_ref, dh_ref, dres_ref, dx_ref, *rest):
        if want_bf16:
            dxb_ref, dg_ref = rest
        else:
            (dg_ref,) = rest
        xf = x_ref[...]
        r = lax.rsqrt(jnp.mean(xf * xf, axis=-1, keepdims=True) + EPS)
        xhat = xf * r
        dh_ = dh_ref[...]
        dy = dh_ * g_ref[...]
        dx = dres_ref[...] + r * (dy - xhat * jnp.mean(dy * xhat, axis=-1, keepdims=True))
        dx_ref[...] = dx
        if want_bf16:
            dxb_ref[...] = dx.astype(BF16)
        part = jnp.sum(dh_ * xhat, axis=0, keepdims=True)

        @pl.when(pl.program_id(0) == 0)
        def _():
            dg_ref[...] = part

        @pl.when(pl.program_id(0) > 0)
        def _():
            dg_ref[...] += part

    row = pl.BlockSpec((tm, D), lambda i: (i, 0))
    vec = pl.BlockSpec((1, D), lambda i: (0, 0))
    out_specs = [row] + ([row] if want_bf16 else []) + [vec]
    out_shape = ([jax.ShapeDtypeStruct((T, D), F32)] + ([jax.ShapeDtypeStruct((T, D), BF16)] if want_bf16 else [])
                 + [jax.ShapeDtypeStruct((1, D), F32)])
    body, in_specs, args = _ordered_after(body, 4, [row, vec, row, row], (x, g, dh, dres), after)
    return pl.pallas_call(
        body, name=name, grid=(T // tm,),
        in_specs=in_specs, out_specs=out_specs, out_shape=out_shape,
        compiler_params=_cparams("arbitrary"),
    )(*args)


def _loss_head(x, g, target, *, name):
    T, D = x.shape
    tm = _div(T, 256, 8)

    def body(x_ref, g_ref, t_ref, loss_ref, dx_ref, dxb_ref, dg_ref):
        xf = x_ref[...]
        r = lax.rsqrt(jnp.mean(xf * xf, axis=-1, keepdims=True) + EPS)
        xhat = xf * r
        gain = g_ref[...]
        err = xhat * gain - t_ref[...]
        lpart = 0.5 * jnp.sum(jnp.mean(err * err, axis=-1, keepdims=True), axis=0, keepdims=True)
        dh_ = err * (1.0 / D)
        dy = dh_ * gain
        dx = r * (dy - xhat * jnp.mean(dy * xhat, axis=-1, keepdims=True))
        dx_ref[...] = dx
        dxb_ref[...] = dx.astype(BF16)
        part = jnp.sum(dh_ * xhat, axis=0, keepdims=True)

        @pl.when(pl.program_id(0) == 0)
        def _():
            dg_ref[...] = part
            loss_ref[...] = jnp.broadcast_to(lpart, loss_ref.shape)

        @pl.when(pl.program_id(0) > 0)
        def _():
            dg_ref[...] += part
            loss_ref[...] += jnp.broadcast_to(lpart, loss_ref.shape)

    row = pl.BlockSpec((tm, D), lambda i: (i, 0))
    vec = pl.BlockSpec((1, D), lambda i: (0, 0))
    return pl.pallas_call(
        body, name=name, grid=(T // tm,),
        in_specs=[row, vec, row],
        out_specs=[pl.BlockSpec((8, LANES), lambda i: (0, 0)), row, row, vec],
        out_shape=[jax.ShapeDtypeStruct((8, LANES), F32), jax.ShapeDtypeStruct((T, D), F32),
                   jax.ShapeDtypeStruct((T, D), BF16), jax.ShapeDtypeStruct((1, D), F32)],
        compiler_params=_cparams("arbitrary"),
    )(x, g, target)


def _gate_cols(D):
    off_a = 3 * D // 2 + 2 * KV_WIDTH
    off_b = off_a + D
    cw = math.gcd(math.gcd(off_a, off_b), math.gcd(D, 512))
    return cw, off_a // cw, off_b // cw


def _merge_fwd(z, ya, yb, *, name):
    T, D = ya.shape
    cw, ba, bb = _gate_cols(D)
    tm = _div(T, 512, 8)

    def body(ga_ref, gb_ref, ya_ref, yb_ref, m_ref):
        m_ref[...] = (_sigmoid(ga_ref[...]) * ya_ref[...] + _sigmoid(gb_ref[...]) * yb_ref[...]).astype(BF16)

    blk = pl.BlockSpec((tm, cw), lambda i, j: (i, j))
    return pl.pallas_call(
        body, name=name, grid=(T // tm, D // cw),
        in_specs=[pl.BlockSpec((tm, cw), lambda i, j: (i, ba + j)), pl.BlockSpec((tm, cw), lambda i, j: (i, bb + j)), blk, blk],
        out_specs=blk, out_shape=jax.ShapeDtypeStruct((T, D), BF16),
        compiler_params=_cparams("parallel", "parallel"),
    )(z, z, ya, yb)


def _merge_bwd(z, ya, yb, dm, *, name, after=None):
    T, D = ya.shape
    cw, ba, bb = _gate_cols(D)
    nj = D // cw
    assert bb == ba + nj
    tm = _div(T, 512, 8)

    def body(g_ref, ya_ref, yb_ref, dm_ref, dy_ref, dz_ref):
        sig = _sigmoid(g_ref[...])
        dm_ = dm_ref[...]
        y = jnp.where(pl.program_id(1) == 0, ya_ref[...], yb_ref[...])
        dy_ref[...] = (dm_ * sig).astype(BF16)
        dz_ref[...] = (dm_ * y * (sig * (1.0 - sig))).astype(BF16)

    in_specs = [pl.BlockSpec((tm, cw), lambda i, s, j: (i, ba + s * nj + j)),
                pl.BlockSpec((tm, cw), lambda i, s, j: (i, j * (1 - s))),
                pl.BlockSpec((tm, cw), lambda i, s, j: (i, j * s)),
                pl.BlockSpec((tm, cw), lambda i, s, j: (i, j))]
    body, in_specs, args = _ordered_after(body, 4, in_specs, (z, ya, yb, dm), after)
    return pl.pallas_call(
        body, name=name, grid=(T // tm, 2, nj),
        in_specs=in_specs,
        out_specs=[pl.BlockSpec((None, tm, cw), lambda i, s, j: (s, i, j)),
                   pl.BlockSpec((tm, cw), lambda i, s, j: (i, ba + s * nj + j))],
        out_shape=[jax.ShapeDtypeStruct((2, T, D), BF16), jax.ShapeDtypeStruct(z.shape, BF16)],
        compiler_params=_cparams("parallel", "arbitrary", "arbitrary"),
    )(*args)


def _swiglu_mm_fwd(h, wg_t, wu_t, *, name, bm=1024, bn=512):
    T, D = h.shape
    F = wg_t.shape[0]
    bm, bn = _div(T, bm), _div(F, bn)
    dims = (((1,), (1,)), ((), ()))

    def body(h_ref, wg_ref, wu_ref, g_ref, u_ref, act_ref):
        h_ = h_ref[...]
        g = lax.dot_general(h_, wg_ref[...], dims, preferred_element_type=F32)
        u = lax.dot_general(h_, wu_ref[...], dims, preferred_element_type=F32)
        g_ref[...] = g
        u_ref[...] = u
        act_ref[...] = (g * _sigmoid(g) * u).astype(BF16)

    w_spec = pl.BlockSpec((bn, D), lambda i, j: (j, 0))
    o_spec = pl.BlockSpec((bm, bn), lambda i, j: (i, j))
    return pl.pallas_call(
        body, name=name, grid=(T // bm, F // bn),
        in_specs=[pl.BlockSpec((bm, D), lambda i, j: (i, 0)), w_spec, w_spec], out_specs=[o_spec] * 3,
        out_shape=[jax.ShapeDtypeStruct((T, F), F32), jax.ShapeDtypeStruct((T, F), F32), jax.ShapeDtypeStruct((T, F), BF16)],
        compiler_params=_cparams("parallel", "parallel"),
    )(h, wg_t, wu_t)


def _swiglu_mm_bwd(dx, w_down, gate, up, *, name, bm=1024, bn=512, after=None):
    T, D = dx.shape
    F = w_down.shape[0]
    bm, bn = _div(T, bm), _div(F, bn)
    dims = (((1,), (1,)), ((), ()))

    def body(dx_ref, w_ref, g_ref, u_ref, dg_ref, du_ref):
        d = lax.dot_general(dx_ref[...], w_ref[...], dims, preferred_element_type=F32)
        g = g_ref[...]
        s = _sigmoid(g)
        silu = g * s
        dg_ref[...] = (d * u_ref[...] * (s + silu * (1.0 - s))).astype(BF16)
        du_ref[...] = (d * silu).astype(BF16)

    o_spec = pl.BlockSpec((bm, bn), lambda i, j: (i, j))
    in_specs = [pl.BlockSpec((bm, D), lambda i, j: (i, 0)), pl.BlockSpec((bn, D), lambda i, j: (j, 0)), o_spec, o_spec]
    body, in_specs, args = _ordered_after(body, 4, in_specs, (dx, w_down, gate, up), after)
    out = jax.ShapeDtypeStruct((T, F), BF16)
    return pl.pallas_call(
        body, name=name, grid=(T // bm, F // bn), in_specs=in_specs, out_specs=[o_spec, o_spec], out_shape=[out, out],
        compiler_params=_cparams("parallel", "parallel"),
    )(*args)


def _sgu_fwd(z, gain, ws_b, bs_t, *, name):
    T = z.shape[0]
    SW = gain.shape[1]
    G = SW // BLOCK

    def body(zu_ref, zv_ref, gain_ref, ws_ref, bs_ref, a_ref):
        u = _gelu(zu_ref[...])
        vg = _gelu(zv_ref[...])
        r = lax.rsqrt(jnp.mean(vg * vg, axis=-1, keepdims=True) + EPS)
        vn = ((vg * r) * gain_ref[...]).astype(BF16)
        for g in range(G):
            sl = slice(g * BLOCK, (g + 1) * BLOCK)
            mixed = jnp.dot(ws_ref[g], vn[:, sl], preferred_element_type=F32) + bs_ref[:, g:g + 1]
            a_ref[:, sl] = (u[:, sl] * mixed).astype(BF16)

    return pl.pallas_call(
        body, name=name, grid=(T // BLOCK,),
        in_specs=[pl.BlockSpec((BLOCK, SW), lambda c: (c, 0)), pl.BlockSpec((BLOCK, SW), lambda c: (c, 1)),
                  pl.BlockSpec((1, SW), lambda c: (0, 0)), pl.BlockSpec((G, BLOCK, BLOCK), lambda c: (0, 0, 0)),
                  pl.BlockSpec((BLOCK, G), lambda c: (0, 0))],
        out_specs=pl.BlockSpec((BLOCK, SW), lambda c: (c, 0)),
        out_shape=jax.ShapeDtypeStruct((T, SW), BF16),
        compiler_params=_cparams("parallel"),
    )(z, z, gain, ws_b, bs_t)


def _sgu_bwd(z, gain, ws_b, bs_t, da, dz, *, name):
    T = z.shape[0]
    SW = gain.shape[1]
    G = SW // BLOCK

    def body(zu_ref, zv_ref, gain_ref, ws_ref, bs_ref, da_ref, dz_in_ref, dz_ref, dws_ref, dbs_ref, dgain_ref, dvn_ref):
        first = pl.program_id(0) == 0

        @pl.when(first)
        def _():
            dws_ref[...] = jnp.zeros_like(dws_ref)
            dbs_ref[...] = jnp.zeros_like(dbs_ref)
            dgain_ref[...] = jnp.zeros_like(dgain_ref)

        u, du = _gelu_and_grad(zu_ref[...])
        vg, dvg = _gelu_and_grad(zv_ref[...])
        r = lax.rsqrt(jnp.mean(vg * vg, axis=-1, keepdims=True) + EPS)
        xhat = vg * r
        gain_ = gain_ref[...]
        vn = (xhat * gain_).astype(BF16)
        da_ = da_ref[...]
        for g in range(G):
            sl = slice(g * BLOCK, (g + 1) * BLOCK)
            w = ws_ref[g]
            mixed = jnp.dot(w, vn[:, sl], preferred_element_type=F32) + bs_ref[:, g:g + 1]
            dmix = da_[:, sl] * u[:, sl]
            dz_ref[:, sl] = (da_[:, sl] * mixed * du[:, sl]).astype(BF16)
            dmb = dmix.astype(BF16)
            dws_ref[g] += lax.dot_general(dmb, vn[:, sl], (((1,), (1,)), ((), ())), preferred_element_type=F32)
            dbs_ref[:, g:g + 1] += jnp.sum(dmix, axis=-1, keepdims=True)
            dvn_ref[:, sl] = lax.dot_general(w, dmb, (((0,), (0,)), ((), ())), preferred_element_type=F32)
        dvn = dvn_ref[...]
        dgain_ref[...] += jnp.sum(dvn * xhat, axis=0, keepdims=True)
        dy = dvn * gain_
        dv_ = r * (dy - xhat * jnp.mean(dy * xhat, axis=-1, keepdims=True))
        dz_ref[:, SW:] = (dv_ * dvg).astype(BF16)

    row = pl.BlockSpec((BLOCK, SW), lambda c: (c, 0))
    return pl.pallas_call(
        body, name=name, grid=(T // BLOCK,),
        in_specs=[row, pl.BlockSpec((BLOCK, SW), lambda c: (c, 1)),
                  pl.BlockSpec((1, SW), lambda c: (0, 0)), pl.BlockSpec((G, BLOCK, BLOCK), lambda c: (0, 0, 0)),
                  pl.BlockSpec((BLOCK, G), lambda c: (0, 0)), row, _ANY],
        out_specs=[pl.BlockSpec((BLOCK, 2 * SW), lambda c: (c, 0)), pl.BlockSpec((G, BLOCK, BLOCK), lambda c: (0, 0, 0)),
                   pl.BlockSpec((BLOCK, G), lambda c: (0, 0)), pl.BlockSpec((1, SW), lambda c: (0, 0))],
        out_shape=[jax.ShapeDtypeStruct(dz.shape, dz.dtype),
                   jax.ShapeDtypeStruct((G, BLOCK, BLOCK), F32), jax.ShapeDtypeStruct((BLOCK, G), F32),
                   jax.ShapeDtypeStruct((1, SW), F32)],
        input_output_aliases={6: 0},
        scratch_shapes=[pltpu.VMEM((BLOCK, SW), F32)],
        compiler_params=_cparams("arbitrary"),
    )(z, z, gain, ws_b, bs_t, da, dz)


def _bias_table(rel_bias, bmap, *, name):
    H = rel_bias.shape[1]

    def body(rb_ref, bmap_ref, o_ref):
        bm_ = bmap_ref[...]
        for h in range(H):
            acc = jnp.zeros(bm_.shape, F32)
            for b in range(REL_BUCKETS):
                acc = jnp.where(bm_ == b, rb_ref[b, h], acc)
            o_ref[h] = acc

    return pl.pallas_call(
        body, name=name,
        in_specs=[pl.BlockSpec(memory_space=pltpu.SMEM), pl.BlockSpec(memory_space=pltpu.VMEM)],
        out_specs=pl.BlockSpec(memory_space=pltpu.VMEM),
        out_shape=jax.ShapeDtypeStruct((H, BLOCK, 3 * BLOCK), F32),
    )(rel_bias, bmap)


def _attn_probs(q_ref, kb, bias_ref, sink_ref, valid, h, group):
    kv = h // group
    qh = q_ref[:, h * HEAD_DIM:(h + 1) * HEAD_DIM].astype(BF16)
    s = lax.dot_general(qh, kb[:, kv * HEAD_DIM:(kv + 1) * HEAD_DIM], (((1,), (1,)), ((), ())),
                        preferred_element_type=F32)
    s = s * (HEAD_DIM ** -0.5) + bias_ref[h]
    s = jnp.where(valid, s, NEG)
    sink = sink_ref[0:1, h:h + 1]
    m = jnp.maximum(jnp.max(s, axis=-1, keepdims=True), sink)
    e = jnp.exp(s - m)
    es = jnp.exp(sink - m)
    inv = 1.0 / (jnp.sum(e, axis=-1, keepdims=True) + es)
    return e * inv, es * inv, qh


def _band_valid(n, T):
    row = lax.broadcasted_iota(jnp.int32, (BLOCK, 3 * BLOCK), 0)
    col = lax.broadcasted_iota(jnp.int32, (BLOCK, 3 * BLOCK), 1)
    rel = col - BLOCK - row
    key_pos = n * BLOCK + col - BLOCK
    return (jnp.abs(rel) <= BLOCK) & (key_pos >= 0) & (key_pos < T)


def _attn_fwd(z, kpad, vpad, bias, sink, *, name):
    T = z.shape[0]
    H = bias.shape[0]
    AW = H * HEAD_DIM
    group = H // N_KV_HEADS

    def body(q_ref, k_ref, v_ref, bias_ref, sink_ref, o_ref):
        n = pl.program_id(0)
        start = pl.multiple_of(n * BLOCK, BLOCK)
        kb = k_ref[pl.ds(start, 3 * BLOCK), :]
        vb = v_ref[pl.ds(start, 3 * BLOCK), :]
        valid = _band_valid(n, T)
        for h in range(H):
            kv = h // group
            p, _, _ = _attn_probs(q_ref, kb, bias_ref, sink_ref, valid, h, group)
            o = jnp.dot(p.astype(BF16), vb[:, kv * HEAD_DIM:(kv + 1) * HEAD_DIM], preferred_element_type=F32)
            o_ref[:, h * HEAD_DIM:(h + 1) * HEAD_DIM] = o.astype(BF16)

    full_kv = pl.BlockSpec((T + 2 * BLOCK, KV_WIDTH), lambda n: (0, 0))
    return pl.pallas_call(
        body, name=name, grid=(T // BLOCK,),
        in_specs=[pl.BlockSpec((BLOCK, AW), lambda n: (n, 2)), full_kv, full_kv,
                  pl.BlockSpec((H, BLOCK, 3 * BLOCK), lambda n: (0, 0, 0)), pl.BlockSpec((1, H), lambda n: (0, 0))],
        out_specs=pl.BlockSpec((BLOCK, AW), lambda n: (n, 0)),
        out_shape=jax.ShapeDtypeStruct((T, AW), BF16),
        compiler_params=_cparams("parallel"),
    )(z, kpad, vpad, bias, sink)


def _attn_bwd(z, kpad, vpad, bias, sink, do, dz, *, name):
    T = z.shape[0]
    H = bias.shape[0]
    AW = H * HEAD_DIM
    group = H // N_KV_HEADS
    scale = HEAD_DIM ** -0.5

    def body(q_ref, k_ref, v_ref, bias_ref, sink_ref, do_ref, dz_in_ref, dq_ref, dk_ref, dv_ref, dbias_ref, dsink_ref):
        n = pl.program_id(0)

        @pl.when(n == 0)
        def _():
            dk_ref[...] = jnp.zeros_like(dk_ref)
            dv_ref[...] = jnp.zeros_like(dv_ref)
            dbias_ref[...] = jnp.zeros_like(dbias_ref)
            dsink_ref[...] = jnp.zeros_like(dsink_ref)

        start = pl.multiple_of(n * BLOCK, BLOCK)
        kb = k_ref[pl.ds(start, 3 * BLOCK), :]
        vb = v_ref[pl.ds(start, 3 * BLOCK), :]
        valid = _band_valid(n, T)
        for kv in range(N_KV_HEADS):
            ksl = slice(kv * HEAD_DIM, (kv + 1) * HEAD_DIM)
            dk_acc = jnp.zeros((3 * BLOCK, HEAD_DIM), F32)
            dv_acc = jnp.zeros((3 * BLOCK, HEAD_DIM), F32)
            for gi in range(group):
                h = kv * group + gi
                hsl = slice(h * HEAD_DIM, (h + 1) * HEAD_DIM)
                p, p_sink, qh = _attn_probs(q_ref, kb, bias_ref, sink_ref, valid, h, group)
                doh = do_ref[:, hsl]
                dp = lax.dot_general(doh, vb[:, ksl], (((1,), (1,)), ((), ())), preferred_element_type=F32)
                delta = jnp.sum(p * dp, axis=-1, keepdims=True)
                ds = p * (dp - delta)
                dbias_ref[h] += ds
                dsink_ref[:, h:h + 1] += -(p_sink * delta)
                dsb = ds.astype(BF16)
                dq = jnp.dot(dsb, kb[:, ksl], preferred_element_type=F32) * scale
                dq_ref[:, hsl] = dq.astype(BF16)
                dk_acc = dk_acc + lax.dot_general(dsb, qh, (((0,), (0,)), ((), ())), preferred_element_type=F32)
                dv_acc = dv_acc + lax.dot_general(p.astype(BF16), doh, (((0,), (0,)), ((), ())),
                                                  preferred_element_type=F32)
            dk_ref[pl.ds(start, 3 * BLOCK), ksl] += dk_acc * scale
            dv_ref[pl.ds(start, 3 * BLOCK), ksl] += dv_acc

    full_kv = pl.BlockSpec((T + 2 * BLOCK, KV_WIDTH), lambda n: (0, 0))
    bias_spec = pl.BlockSpec((H, BLOCK, 3 * BLOCK), lambda n: (0, 0, 0))
    row = pl.BlockSpec((BLOCK, AW), lambda n: (n, 0))
    q_cols = pl.BlockSpec((BLOCK, AW), lambda n: (n, 2))
    return pl.pallas_call(
        body, name=name, grid=(T // BLOCK,),
        in_specs=[q_cols, full_kv, full_kv, bias_spec, pl.BlockSpec((1, H), lambda n: (0, 0)), row, _ANY],
        out_specs=[q_cols, full_kv, full_kv, bias_spec, pl.BlockSpec((BLOCK, H), lambda n: (0, 0))],
        out_shape=[jax.ShapeDtypeStruct(dz.shape, dz.dtype),
                   jax.ShapeDtypeStruct((T + 2 * BLOCK, KV_WIDTH), F32), jax.ShapeDtypeStruct((T + 2 * BLOCK, KV_WIDTH), F32),
                   jax.ShapeDtypeStruct((H, BLOCK, 3 * BLOCK), F32), jax.ShapeDtypeStruct((BLOCK, H), F32)],
        input_output_aliases={6: 0},
        compiler_params=_cparams("arbitrary"),
    )(z, kpad, vpad, bias, sink, do, dz)


def _dkv_into(dkp, dvp, dz, *, name):
    T = dz.shape[0]
    D = (dz.shape[1] - 2 * KV_WIDTH) * 2 // 7
    col = (D + D // 2) // (2 * KV_WIDTH)
    assert col * 2 * KV_WIDTH == D + D // 2

    def body(dk_ref, dv_ref, dz_in_ref, o_ref):
        o_ref[:, :KV_WIDTH] = dk_ref[...].astype(BF16)
        o_ref[:, KV_WIDTH:] = dv_ref[...].astype(BF16)

    kv = pl.BlockSpec((BLOCK, KV_WIDTH), lambda n: (n + 1, 0))
    return pl.pallas_call(
        body, name=name, grid=(T // BLOCK,),
        in_specs=[kv, kv, _ANY], out_specs=pl.BlockSpec((BLOCK, 2 * KV_WIDTH), lambda n: (n, col)),
        out_shape=jax.ShapeDtypeStruct(dz.shape, dz.dtype), input_output_aliases={2: 0},
        compiler_params=_cparams("parallel"),
    )(dkp, dvp, dz)


def _kv_pad(z, *, name):
    T = z.shape[0]
    D = (z.shape[1] - 2 * KV_WIDTH) * 2 // 7
    kcol = (D + D // 2) // KV_WIDTH
    nb = T // BLOCK

    def body(k_ref, v_ref, ko_ref, vo_ref):
        b = pl.program_id(0)
        inside = (b >= 1) & (b <= nb)
        ko_ref[...] = jnp.where(inside, k_ref[...], 0.0).astype(BF16)
        vo_ref[...] = jnp.where(inside, v_ref[...], 0.0).astype(BF16)

    out = jax.ShapeDtypeStruct((T + 2 * BLOCK, KV_WIDTH), BF16)
    o_spec = pl.BlockSpec((BLOCK, KV_WIDTH), lambda b: (b, 0))
    return pl.pallas_call(
        body, name=name, grid=(nb + 2,),
        in_specs=[pl.BlockSpec((BLOCK, KV_WIDTH), lambda b: (jnp.clip(b - 1, 0, nb - 1), kcol)),
                  pl.BlockSpec((BLOCK, KV_WIDTH), lambda b: (jnp.clip(b - 1, 0, nb - 1), kcol + 1))],
        out_specs=[o_spec, o_spec], out_shape=[out, out],
        compiler_params=_cparams("parallel"),
    )(z, z)


def _attn_small_grads(dbias, dsink_rows, bmap, *, name):
    H = dbias.shape[0]

    def body(dbias_ref, dsink_ref, bmap_ref, drel_ref, ds_ref):
        bm_ = bmap_ref[...]
        for h in range(H):
            d = dbias_ref[h]
            for b in range(REL_BUCKETS):
                drel_ref[b, h] = jnp.sum(jnp.where(bm_ == b, d, 0.0))
            ds_ref[0, h] = jnp.sum(dsink_ref[:, h:h + 1])

    vmem = pl.BlockSpec(memory_space=pltpu.VMEM)
    smem = pl.BlockSpec(memory_space=pltpu.SMEM)
    return pl.pallas_call(
        body, name=name, in_specs=[vmem, vmem, vmem], out_specs=[smem, smem],
        out_shape=[jax.ShapeDtypeStruct((REL_BUCKETS, H), F32), jax.ShapeDtypeStruct((1, H), F32)],
    )(dbias, dsink_rows, bmap)


def _local_step(x, target, weight, emit, flush, norm_mix, v_gain, w_s, b_s, sink, rel_bias, norm_ffn, norm_final):
    T, D = x.shape
    ws_b = w_s.astype(BF16)
    bs_t = b_s.T
    bmap = jnp.asarray(_bucket_map())

    h = _rms_fwd(x, norm_mix, name="rms_mix")
    w_in = weight("w_in", h)
    z = _mm(h, w_in, tb=True, name="mm_z", bm=2048, bn=768)
    a = _sgu_fwd(z, v_gain, ws_b, bs_t, name="sgu_fwd")
    w_a = weight("w_a_out", a)
    ya = _mm_w8(a, w_a, name="mm_ya", bm=2048)
    kpad, vpad = _kv_pad(z, name="kv_pad")
    bias = _bias_table(rel_bias, bmap, name="bias_table")
    o = _attn_fwd(z, kpad, vpad, bias, sink, name="attn_fwd")
    w_b = weight("w_b_out", o)
    yb = _mm_w8(o, w_b, name="mm_yb", bm=2048)
    m = _merge_fwd(z, ya, yb, name="merge_fwd")
    w_o = weight("w_o", m)
    x1 = _mm(m, w_o, name="mm_x1", add=x, bm=2048, bn=512)
    h2 = _rms_fwd(x1, norm_ffn, name="rms_ffn")
    w_gate = weight("w_gate", h2)
    w_up = weight("w_up", h2)
    gate, up, act = _swiglu_mm_fwd(h2, w_gate, w_up, name="mm_gate_up")
    w_down = weight("w_down", act)
    x2 = _mm(act, w_down, name="mm_x2", add=x1, bm=1024, bn=1024, bk=2816)
    loss, dx2, dx2b, g_norm_final = _loss_head(x2, norm_final, target, name="loss_head")

    g_w_down = _mm(act, dx2b, ta=True, out_dtype=BF16, name="mm_gwdown", bm=512, bn=2048)
    tok = emit(("w_down",), (g_w_down,))
    dgate, dup = _swiglu_mm_bwd(dx2b, w_down, gate, up, name="mm_dact_swiglu", after=tok)
    tok = flush(dgate)
    g_w_gate = _mm(dgate, h2, ta=True, out_dtype=BF16, name="mm_gwgate", bm=512, bn=2048, after=tok)
    g_w_up = _mm(dup, h2, ta=True, out_dtype=BF16, name="mm_gwup", bm=512, bn=2048)
    tok = emit(("w_gate", "w_up"), (g_w_gate, g_w_up))
    dh2 = _mm(dgate, w_gate, name="mm_dh2a", bm=1024, bn=1024, bk=2816, after=tok)
    tok = flush(dh2)
    dh2 = _mm(dup, w_up, add=dh2, name="mm_dh2b", bm=1024, bn=1024, bk=2816, after=tok)
    dx1, dx1b, g_norm_ffn = _rms_bwd(x1, norm_ffn, dh2, dx2, name="rms_ffn_bwd", want_bf16=True)

    g_w_o = _mm(m, dx1b, ta=True, out_dtype=BF16, name="mm_gwo", bm=2048, bn=512)
    tok = emit(("w_o",), (g_w_o,))
    dm = _mm(dx1b, w_o, tb=True, name="mm_dm", bm=2048, bn=512, after=tok)
    tok = flush(dm)
    dy, dz = _merge_bwd(z, ya, yb, dm, name="merge_bwd", after=tok)
    g_w_a = _mm_gw8(a, dy, w_a.shape[2], name="mm_gwa", lead=0)
    g_w_b = _mm_gw8(o, dy, w_b.shape[2], name="mm_gwb", lead=1)
    tok = emit(("w_a_out", "w_b_out"), (g_w_a, g_w_b))
    da = _mm_w8t(dy, w_a, name="mm_da", bm=2048, bn=512, after=tok, lead=0)
    tok = flush(da)
    do = _mm_w8t(dy, w_b, out_dtype=BF16, name="mm_do", bm=2048, bn=512, after=tok, lead=1)
    dz, g_w_s, g_b_s_t, g_v_gain = _sgu_bwd(z, v_gain, ws_b, bs_t, da, dz, name="sgu_bwd")
    dz, dkp, dvp, dbias, dsink_rows = _attn_bwd(z, kpad, vpad, bias, sink, do, dz, name="attn_bwd")
    dz = _dkv_into(dkp, dvp, dz, name="dkv_into_dz")
    g_rel_bias, g_sink = _attn_small_grads(dbias, dsink_rows, bmap, name="attn_small_grads")
    g_w_in = _mm(dz, h, ta=True, out_dtype=BF16, name="mm_gwin", bm=768, bn=2048)
    tok = emit(("w_in",), (g_w_in,))
    dh = _mm(dz, w_in, name="mm_dh", bm=1024, bn=1024, bk=2560, after=tok)
    tok = flush(dh)
    grad_x, g_norm_mix = _rms_bwd(x, norm_mix, dh, dx1, name="rms_mix_bwd", want_bf16=False, after=tok)

    small = dict(norm_mix=g_norm_mix, sgu_v_gain=g_v_gain, sgu_w_s=g_w_s, sgu_b_s=g_b_s_t.T, attn_sink=g_sink,
                 rel_bias=g_rel_bias, norm_ffn=g_norm_ffn, norm_final=g_norm_final)
    return loss, grad_x, small


def _position():
    return lax.axis_index("x"), lax.axis_index("y"), lax.axis_index("c")


def _other_chips(x, y):
    return [(1 - x, y), (x, 1 - y), (1 - x, 1 - y)]


def _slot(px, py, pc):
    return 4 * px + 2 * py + pc


_HBM = pl.BlockSpec(memory_space=pltpu.HBM)
_SEM = pl.BlockSpec(memory_space=pltpu.SEMAPHORE)
_DATAFLOW = pltpu.SideEffectType.DATAFLOW_SIDE_EFFECTING


def _in_hbm(a):
    return pltpu.with_memory_space_constraint(a, pltpu.HBM)


def _own_slot(shard, pos, *, name, after=None):
    R, C = shard.shape
    tr = _div(R, 256, 16)

    def body(pos_ref, w_ref, o_ref):
        o_ref[...] = w_ref[...].astype(BF16)

    body, in_specs, args = _ordered_after(body, 2, [pl.BlockSpec((tr, C), lambda i, pos_ref: (i, 0))], (pos, shard), after)
    grid_spec = pltpu.PrefetchScalarGridSpec(
        num_scalar_prefetch=1, grid=(R // tr,), in_specs=in_specs,
        out_specs=pl.BlockSpec((None, tr, C), lambda i, pos_ref: (pos_ref[0], i, 0)))
    return pl.pallas_call(
        body, name=name, grid_spec=grid_spec,
        out_shape=jax.ShapeDtypeStruct((N_DEV, R, C), BF16),
        compiler_params=_cparams("parallel"),
    )(*args)


def _ag_copies(w, land_ref, send_sems, recv_sems):
    x, y, c = _position()
    mine = land_ref.at[_slot(x, y, c)]
    targets = [(px, py, c) for px, py in _other_chips(x, y)] + [(x, y, 1 - c)]
    return [pltpu.make_async_remote_copy(src_ref=mine, dst_ref=mine, send_sem=send_sems.at[4 * w + k],
                                         recv_sem=recv_sems.at[4 * w + k], device_id=to, device_id_type=MESH)
            for k, to in enumerate(targets)]


def _ag_start(buffers, groups, *, name):
    lands = [buffers[i] for g in groups for i in g]
    n, ng = len(lands), len(groups)
    sizes = [len(g) for g in groups]

    def body(*refs):
        land_refs = refs[:n]
        sems = refs[n:n + 2 * ng]
        token = refs[-1]
        i = 0
        for g in range(ng):
            for w in range(sizes[g]):
                for cp in _ag_copies(w, land_refs[i], sems[2 * g], sems[2 * g + 1]):
                    cp.start()
                i += 1
        token[...] = jnp.zeros_like(token)

    sem_shapes = [pltpu.SemaphoreType.DMA((4 * k,)) for k in sizes for _ in range(2)]
    outs = pl.pallas_call(
        body, name=name,
        in_specs=[_HBM] * n,
        out_specs=tuple([_SEM] * (2 * ng) + [_HBM] * n + [pl.BlockSpec(memory_space=pltpu.VMEM)]),
        out_shape=tuple(sem_shapes + [pltpu.HBM(a.shape, a.dtype) for a in lands] + [jax.ShapeDtypeStruct((8, LANES), F32)]),
        input_output_aliases={i: 2 * ng + i for i in range(n)},
        compiler_params=pltpu.CompilerParams(has_side_effects=_DATAFLOW),
    )(*[_in_hbm(a) for a in lands])
    sems, thru = outs[:2 * ng], outs[2 * ng:2 * ng + n]
    result, i = [], 0
    for g in range(ng):
        k = sizes[g]
        result.append((sems[2 * g], sems[2 * g + 1], list(thru[i:i + k])))
        i += k
    return result, outs[-1]


def _ag_wait(send_sems, recv_sems, lands, after, *, name):
    n = len(lands)

    def body(*refs):
        land_refs = refs[:n]
        send_ref, recv_ref = refs[n], refs[n + 1]
        token = refs[-1]
        for w in range(n):
            for cp in _ag_copies(w, land_refs[w], send_ref, recv_ref):
                cp.wait_send()
                cp.wait_recv()
        token[...] = jnp.zeros_like(token)

    outs = pl.pallas_call(
        body, name=name,
        in_specs=[_HBM] * n + [_SEM, _SEM, _ANY],
        out_specs=tuple([_HBM] * n + [pl.BlockSpec(memory_space=pltpu.VMEM)]),
        out_shape=tuple([pltpu.HBM(a.shape, a.dtype) for a in lands] + [jax.ShapeDtypeStruct((8, LANES), F32)]),
        input_output_aliases={i: i for i in range(n)},
        compiler_params=pltpu.CompilerParams(has_side_effects=_DATAFLOW),
    )(*lands, send_sems, recv_sems, after)
    return list(outs[:n]), outs[n]


def _ag_forward(lands, *, name, after=None):
    n = len(lands)

    def body(*refs):
        in_refs, out_refs = refs[:n], refs[n:2 * n]
        send_sems, recv_sems = refs[2 * n:]
        x, y, c = _position()
        copies = []
        for w in range(n):
            for k, (px, py) in enumerate(_other_chips(x, y)):
                cp = pltpu.make_async_remote_copy(
                    src_ref=in_refs[w].at[_slot(px, py, c)], dst_ref=out_refs[w].at[_slot(px, py, c)],
                    send_sem=send_sems.at[3 * w + k], recv_sem=recv_sems.at[3 * w + k],
                    device_id=(x, y, 1 - c), device_id_type=MESH)
                cp.start()
                copies.append(cp)
        for cp in copies:
            cp.wait()

    body, in_specs, args = _ordered_after(body, n, [_ANY] * n, tuple(lands), after)
    return pl.pallas_call(
        body, name=name,
        in_specs=in_specs, out_specs=[_ANY] * n,
        out_shape=[jax.ShapeDtypeStruct(a.shape, a.dtype) for a in lands],
        input_output_aliases={i: i for i in range(n)},
        scratch_shapes=[pltpu.SemaphoreType.DMA((3 * n,)), pltpu.SemaphoreType.DMA((3 * n,))],
    )(*args)


def _sibling_copies(w, g8_ref, land_ref, send_sems, recv_sems):
    x, y, c = _position()
    return [pltpu.make_async_remote_copy(src_ref=g8_ref.at[2 * p + (1 - c)], dst_ref=land_ref.at[p],
                                         send_sem=send_sems.at[4 * w + p], recv_sem=recv_sems.at[4 * w + p],
                                         device_id=(x, y, 1 - c), device_id_type=MESH)
            for p in range(4)]


def _chip_copies(w, sums_ref, land_ref, send_sems, recv_sems):
    x, y, c = _position()
    return [pltpu.make_async_remote_copy(src_ref=sums_ref.at[2 * px + py], dst_ref=land_ref.at[k],
                                         send_sem=send_sems.at[3 * w + k], recv_sem=recv_sems.at[3 * w + k],
                                         device_id=(px, py, c), device_id_type=MESH)
            for k, (px, py) in enumerate(_other_chips(x, y))]


def _copies_start(copies, per_weight, srcs, *, name):
    n = len(srcs)
    lands = [lax.empty((per_weight,) + s.shape[1:], s.dtype) for s in srcs]

    def body(*refs):
        src_refs, land_refs = refs[:n], refs[n:2 * n]
        send_sems, recv_sems = refs[2 * n], refs[2 * n + 1]
        token = refs[-1]
        for w in range(n):
            for cp in copies(w, src_refs[w], land_refs[w], send_sems, recv_sems):
                cp.start()
        token[...] = jnp.zeros_like(token)

    outs = pl.pallas_call(
        body, name=name,
        in_specs=[_HBM] * (2 * n),
        out_specs=tuple([_SEM, _SEM] + [_HBM] * (2 * n) + [pl.BlockSpec(memory_space=pltpu.VMEM)]),
        out_shape=tuple([pltpu.SemaphoreType.DMA((per_weight * n,)), pltpu.SemaphoreType.DMA((per_weight * n,))]
                        + [pltpu.HBM(a.shape, a.dtype) for a in srcs + lands] + [jax.ShapeDtypeStruct((8, LANES), F32)]),
        input_output_aliases={i: 2 + i for i in range(2 * n)},
        compiler_params=pltpu.CompilerParams(has_side_effects=_DATAFLOW),
    )(*[_in_hbm(a) for a in srcs + lands])
    return outs[0], outs[1], list(outs[2:2 + n]), list(outs[2 + n:2 + 2 * n]), outs[-1]


def _copies_wait(copies, send_sems, recv_sems, srcs, lands, after, *, name):
    n = len(srcs)

    def body(*refs):
        src_refs, land_refs = refs[:n], refs[n:2 * n]
        send_ref, recv_ref = refs[2 * n], refs[2 * n + 1]
        for w in range(n):
            for cp in copies(w, src_refs[w], land_refs[w], send_ref, recv_ref):
                cp.wait_send()
                cp.wait_recv()

    outs = pl.pallas_call(
        body, name=name,
        in_specs=[_HBM] * (2 * n) + [_SEM, _SEM, _ANY],
        out_specs=tuple([_HBM] * (2 * n)),
        out_shape=tuple(pltpu.HBM(a.shape, a.dtype) for a in srcs + lands),
        input_output_aliases={i: i for i in range(2 * n)},
        compiler_params=pltpu.CompilerParams(has_side_effects=_DATAFLOW),
    )(*srcs, *lands, send_sems, recv_sems, after)
    return list(outs[:n]), list(outs[n:])


def _chip_sums(g8, from_sibling, pos, *, name):
    _, R, C = g8.shape
    tr = _div(R, 512, 16)

    def body(pos_ref, g_ref, s_ref, o_ref):
        o_ref[...] = (g_ref[...].astype(F32) + s_ref[...].astype(F32)).astype(BF16)

    grid_spec = pltpu.PrefetchScalarGridSpec(
        num_scalar_prefetch=1, grid=(4, R // tr),
        in_specs=[pl.BlockSpec((None, tr, C), lambda p, i, pos_ref: (2 * p + pos_ref[2], i, 0)),
                  pl.BlockSpec((None, tr, C), lambda p, i, pos_ref: (p, i, 0))],
        out_specs=pl.BlockSpec((None, tr, C), lambda p, i, pos_ref: (p, i, 0)))
    return pl.pallas_call(
        body, name=name, grid_spec=grid_spec,
        out_shape=jax.ShapeDtypeStruct((4, R, C), BF16),
        compiler_params=_cparams("parallel", "parallel"),
    )(pos, g8, from_sibling)


def _small_all_reduce(packed, after, *, name):
    R, L = packed.shape

    def body(x_ref, sum_ref, gath_ref, send_sems, recv_sems, local_sem):
        x, y, c = _position()
        me, sibling = (x, y, c), (x, y, 1 - c)
        chips = _other_chips(x, y)

        def rows(px, py, pc):
            return gath_ref.at[pl.ds(_slot(px, py, pc) * R, R), :]

        def copy(k, block, to, src=None):
            return pltpu.make_async_remote_copy(
                src_ref=rows(*block) if src is None else src, dst_ref=rows(*block),
                send_sem=send_sems.at[k], recv_sem=recv_sems.at[k], device_id=to, device_id_type=MESH)

        mine = pltpu.make_async_copy(x_ref, rows(*me), local_sem)
        mine.start()
        first = [copy(0, me, sibling, src=x_ref)]
        first += [copy(1 + j, me, (*chip, c), src=x_ref) for j, chip in enumerate(chips)]
        for cp in first:
            cp.start()
        passed = [copy(4 + j, (*chip, c), sibling) for j, chip in enumerate(chips)]
        for j, chip in enumerate(chips):
            copy(1 + j, (*chip, c), me).wait_recv()
            passed[j].start()
        copy(0, sibling, me).wait_recv()
        for j, chip in enumerate(chips):
            copy(4 + j, (*chip, 1 - c), me).wait_recv()
        for cp in first + passed:
            cp.wait_send()
        mine.wait()
        acc = gath_ref[0:R, :]
        for d in range(1, N_DEV):
            acc = acc + gath_ref[d * R:(d + 1) * R, :]
        sum_ref[...] = acc

    vmem = pl.BlockSpec(memory_space=pltpu.VMEM)
    body, in_specs, args = _ordered_after(body, 1, [vmem], (packed,), after)
    return pl.pallas_call(
        body, name=name, in_specs=in_specs, out_specs=vmem,
        out_shape=jax.ShapeDtypeStruct((R, L), F32),
        scratch_shapes=[pltpu.VMEM((N_DEV * R, L), F32), pltpu.SemaphoreType.DMA((7,)), pltpu.SemaphoreType.DMA((7,)),
                        pltpu.SemaphoreType.DMA],
        compiler_params=pltpu.CompilerParams(vmem_limit_bytes=VMEM_LIMIT),
    )(*args)


def _adamw_math(w, g, m, v):
    m = ADAM_B1 * m + (1.0 - ADAM_B1) * g
    v = ADAM_B2 * v + (1.0 - ADAM_B2) * (g * g)
    m_hat = m / (1.0 - ADAM_B1 ** ADAM_STEP)
    v_hat = v / (1.0 - ADAM_B2 ** ADAM_STEP)
    delta = -ADAM_LR * (m_hat / (jnp.sqrt(v_hat) + ADAM_EPS) + ADAM_WD * w)
    return delta, m, v


def _adamw_shard(w, m, v, g8, from_sibling, from_chips, pos, *, name):
    R, C = w.shape
    tr = _div(R, 256, 16)

    def body(pos_ref, w_ref, m_ref, v_ref, g_ref, s_ref, r_ref, go_ref, d_ref, mo_ref, vo_ref):
        g = g_ref[...].astype(F32) + s_ref[...].astype(F32)
        for k in range(3):
            g = g + r_ref[k].astype(F32)
        delta, m_, v_ = _adamw_math(w_ref[...], g, m_ref[...], v_ref[...])
        go_ref[...] = g
        d_ref[...] = delta
        mo_ref[...] = m_
        vo_ref[...] = v_

    blk = pl.BlockSpec((tr, C), lambda i, pos_ref: (i, 0))
    grid_spec = pltpu.PrefetchScalarGridSpec(
        num_scalar_prefetch=1, grid=(R // tr,),
        in_specs=[blk, blk, blk,
                  pl.BlockSpec((None, tr, C), lambda i, pos_ref: (pos_ref[0], i, 0)),
                  pl.BlockSpec((None, tr, C), lambda i, pos_ref: (pos_ref[1], i, 0)),
                  pl.BlockSpec((3, tr, C), lambda i, pos_ref: (0, i, 0))],
        out_specs=[blk] * 4)
    out = jax.ShapeDtypeStruct((R, C), F32)
    return pl.pallas_call(
        body, name=name, grid_spec=grid_spec, out_shape=[out] * 4,
        compiler_params=_cparams("parallel"),
    )(pos, w, m, v, g8, from_sibling, from_chips)


def _adamw_small(w, g, m, v, *, name):
    R, L = w.shape

    def body(w_ref, g_ref, m_ref, v_ref, d_ref, mo_ref, vo_ref):
        delta, m_, v_ = _adamw_math(w_ref[...], g_ref[...], m_ref[...], v_ref[...])
        d_ref[...] = delta
        mo_ref[...] = m_
        vo_ref[...] = v_

    vmem = pl.BlockSpec(memory_space=pltpu.VMEM)
    out = jax.ShapeDtypeStruct((R, L), F32)
    return pl.pallas_call(body, name=name, in_specs=[vmem] * 4, out_specs=[vmem] * 3, out_shape=[out] * 3)(w, g, m, v)


_TILE = 8 * LANES


def _pack(pieces):
    rows = []
    for p in pieces:
        flat = p.reshape(-1).astype(F32)
        padded = -(-flat.shape[0] // _TILE) * _TILE
        rows.append(jnp.pad(flat, (0, padded - flat.shape[0])).reshape(-1, LANES))
    return jnp.concatenate(rows, axis=0)


def _unpack(packed, like):
    out, r = [], 0
    for p in like:
        size = int(np.prod(p.shape)) if p.shape else 1
        nrows = -(-size // _TILE) * 8
        out.append(packed[r:r + nrows].reshape(-1)[:size].reshape(p.shape))
        r += nrows
    return out


_BIG = ("w_in", "w_a_out", "w_b_out", "w_o", "w_gate", "w_up", "w_down")
_TRANSPOSED = ("w_in", "w_gate", "w_up")
_COL_SHARDED = ("w_a_out", "w_b_out")
_GATHER_GROUPS = (("w_in",), ("w_a_out", "w_b_out", "w_o"), ("w_gate", "w_up"), ("w_down",))
_START_AFTER_WAIT = {0: (1, 2), 2: (3,)}
_SMALL = ("norm_mix", "sgu_v_gain", "sgu_w_s", "sgu_b_s", "attn_sink", "rel_bias", "norm_ffn", "norm_final")
_ORDER = ("w_in", "norm_mix", "sgu_v_gain", "sgu_w_s", "sgu_b_s", "w_a_out", "attn_sink", "rel_bias", "w_b_out", "w_o",
          "norm_ffn", "w_gate", "w_up", "w_down", "norm_final")


def _shard(name, a):
    return jnp.swapaxes(a, 1, 2)[0] if name in _TRANSPOSED else a[0]


def _unshard(name, a):
    return jnp.swapaxes(a[None], 1, 2) if name in _TRANSPOSED else a[None]


def _whole(name, gathered):
    _, r, c = gathered.shape
    return gathered if name in _COL_SHARDED else gathered.reshape(N_DEV * r, c)


def _blocks(name, grad):
    if name in _COL_SHARDED:
        return grad
    r, c = grad.shape
    return grad.reshape(N_DEV, r // N_DEV, c)


def kernel(x, w_in, norm_mix, sgu_v_gain, sgu_w_s, sgu_b_s, w_a_out, attn_sink, rel_bias, w_b_out, w_o, norm_ffn, w_gate, w_up, w_down, norm_final, loss_target, m_w_in, m_norm_mix, m_sgu_v_gain, m_sgu_w_s, m_sgu_b_s, m_w_a_out, m_attn_sink, m_rel_bias, m_w_b_out, m_w_o, m_norm_ffn, m_w_gate, m_w_up, m_w_down, m_norm_final, v_w_in, v_norm_mix, v_sgu_v_gain, v_sgu_w_s, v_sgu_b_s, v_w_a_out, v_attn_sink, v_rel_bias, v_w_b_out, v_w_o, v_norm_ffn, v_w_gate, v_w_up, v_w_down, v_norm_final):
    w = dict(w_in=w_in, norm_mix=norm_mix, sgu_v_gain=sgu_v_gain, sgu_w_s=sgu_w_s, sgu_b_s=sgu_b_s, w_a_out=w_a_out,
             attn_sink=attn_sink, rel_bias=rel_bias, w_b_out=w_b_out, w_o=w_o, norm_ffn=norm_ffn, w_gate=w_gate,
             w_up=w_up, w_down=w_down, norm_final=norm_final)
    m = dict(w_in=m_w_in, norm_mix=m_norm_mix, sgu_v_gain=m_sgu_v_gain, sgu_w_s=m_sgu_w_s, sgu_b_s=m_sgu_b_s,
             w_a_out=m_w_a_out, attn_sink=m_attn_sink, rel_bias=m_rel_bias, w_b_out=m_w_b_out, w_o=m_w_o,
             norm_ffn=m_norm_ffn, w_gate=m_w_gate, w_up=m_w_up, w_down=m_w_down, norm_final=m_norm_final)
    v = dict(w_in=v_w_in, norm_mix=v_norm_mix, sgu_v_gain=v_sgu_v_gain, sgu_w_s=v_sgu_w_s, sgu_b_s=v_sgu_b_s,
             w_a_out=v_w_a_out, attn_sink=v_attn_sink, rel_bias=v_rel_bias, w_b_out=v_w_b_out, w_o=v_w_o,
             norm_ffn=v_norm_ffn, w_gate=v_w_gate, w_up=v_w_up, w_down=v_w_down, norm_final=v_norm_final)
    xc, yc, cc = _position()
    pos = jnp.stack([_slot(xc, yc, cc), 2 * xc + yc, cc]).astype(jnp.int32)

    in_flight, full = {}, {}

    def start_gather(groups, after):
        names = [n for gi in groups for n in _GATHER_GROUPS[gi]]
        buffers = [_own_slot(_shard(n, w[n]), pos, name="own_slot_" + n, after=after) for n in names]
        flights, token = _ag_start(buffers, [[names.index(n) for n in _GATHER_GROUPS[gi]] for gi in groups],
                                   name="ag_start_%d" % groups[0])
        in_flight.update(zip(groups, flights))
        return token

    def weight(name, after):
        if name not in full:
            gi = next(i for i, grp in enumerate(_GATHER_GROUPS) if name in grp)
            send_sems, recv_sems, lands = in_flight[gi]
            lands, token = _ag_wait(send_sems, recv_sems, lands, after, name="ag_wait_%d" % gi)
            started = start_gather(_START_AFTER_WAIT[gi], token) if gi in _START_AFTER_WAIT else None
            gathered = _ag_forward(lands, name="ag_forward_%d" % gi, after=started)
            full.update({n: _whole(n, g) for n, g in zip(_GATHER_GROUPS[gi], gathered)})
        return full[name]

    start_gather((0,), None)

    to_sibling, reducing = [], {}

    def emit(names, grads):
        g8 = [_blocks(n, g) for n, g in zip(names, grads)]
        send_sems, recv_sems, g8, lands, token = _copies_start(_sibling_copies, 4, g8, name="rs_sibling_start_" + names[0])
        to_sibling.append((names, send_sems, recv_sems, g8, lands))
        return token

    def flush(after):
        names, send_sems, recv_sems, g8, lands = to_sibling.pop()
        g8, from_sibling = _copies_wait(_sibling_copies, send_sems, recv_sems, g8, lands, after,
                                        name="rs_sibling_wait_" + names[0])
        sums4 = [_chip_sums(g, s, pos, name="chip_sums_" + n) for n, g, s in zip(names, g8, from_sibling)]
        send_sems, recv_sems, sums4, lands, token = _copies_start(_chip_copies, 3, sums4, name="rs_chips_start_" + names[0])
        reducing[names] = (g8, from_sibling, send_sems, recv_sems, sums4, lands)
        return token

    loss, grad_x, small_grads_local = _local_step(
        x[0], loss_target[0], weight, emit, flush, norm_mix, sgu_v_gain, sgu_w_s[0], sgu_b_s[0], attn_sink, rel_bias,
        norm_ffn, norm_final[None])

    out_g, out_d, out_m, out_v = {}, {}, {}, {}
    small_like = [w[n] for n in _SMALL]
    small_w = _pack(small_like)
    packed = _pack([small_grads_local[n] for n in _SMALL] + [loss[0, 0]])
    after = grad_x
    for gi, (names, (g8, from_sibling, send_sems, recv_sems, sums4, lands)) in enumerate(reducing.items()):
        _, from_chips = _copies_wait(_chip_copies, send_sems, recv_sems, sums4, lands, after,
                                     name="rs_chips_wait_" + names[0])
        if gi == len(reducing) - 1:
            summed = _small_all_reduce(packed, from_chips[0], name="small_all_reduce")
        for i, n in enumerate(names):
            g, d, m_, v_ = _adamw_shard(_shard(n, w[n]), _shard(n, m[n]), _shard(n, v[n]), g8[i], from_sibling[i],
                                        from_chips[i], pos, name="adamw_" + n)
            out_g[n], out_d[n], out_m[n], out_v[n] = (_unshard(n, o) for o in (g, d, m_, v_))
            after = d
    *small_grads, loss_sum = _unpack(summed, small_like + [jax.ShapeDtypeStruct((), F32)])
    d_s, m_s, v_s = _adamw_small(small_w, summed[:small_w.shape[0]], _pack([m[n] for n in _SMALL]),
                                 _pack([v[n] for n in _SMALL]), name="adamw_small")
    for n, g, d, m_, v_ in zip(_SMALL, small_grads, _unpack(d_s, small_like), _unpack(m_s, small_like), _unpack(v_s, small_like)):
        out_g[n], out_d[n], out_m[n], out_v[n] = g, d, m_, v_

    return (loss_sum, grad_x[None], *[out_g[n] for n in _ORDER], *[out_d[n] for n in _ORDER],
            *[out_m[n] for n in _ORDER], *[out_v[n] for n in _ORDER])
```

```python
import functools
import math

import numpy as np
import jax
import jax.numpy as jnp
from jax import lax
from jax.experimental import pallas as pl
from jax.experimental.pallas import tpu as pltpu

F32 = jnp.float32
BF16 = jnp.bfloat16

EPS = 1e-6
NEG = -1e30
HEAD_DIM = 128
BLOCK = 128
N_KV_HEADS = 2
KV_WIDTH = N_KV_HEADS * HEAD_DIM
REL_BUCKETS = 32
REL_MAX_DIST = 128

ADAM_LR = 0.001
ADAM_B1 = 0.9
ADAM_B2 = 0.999
ADAM_EPS = 1e-08
ADAM_WD = 0.01
ADAM_STEP = 10

N_DEV = 8
LANES = 128
VMEM_LIMIT = 56 * 1024 * 1024
MESH = pl.DeviceIdType.MESH


def _cparams(*sem):
    return pltpu.CompilerParams(dimension_semantics=sem, vmem_limit_bytes=VMEM_LIMIT)


def _div(n, target, mult=LANES):
    best = None
    for d in range(mult, min(n, target) + 1, mult):
        if n % d == 0:
            best = d
    assert best is not None, (n, target, mult)
    return best


_ANY = pl.BlockSpec(memory_space=pl.ANY)


def _ordered_after(body, n_inputs, in_specs, args, after):
    if after is None:
        return body, in_specs, args

    def wrapped(*refs):
        return body(*refs[:n_inputs], *refs[n_inputs + 1:])

    return wrapped, list(in_specs) + [_ANY], tuple(args) + (after,)


def _bucket_map():
    nb = REL_BUCKETS // 2
    qi = np.arange(BLOCK)[:, None]
    kj = np.arange(3 * BLOCK)[None, :]
    rel = kj - BLOCK - qi
    ret = np.where(rel > 0, nb, 0)
    n = np.abs(rel)
    max_exact = nb // 2
    nf = np.maximum(n, 1).astype(np.float32)
    large = max_exact + (np.log(nf / np.float32(max_exact)) / np.float32(math.log(REL_MAX_DIST / max_exact))
                         * np.float32(nb - max_exact)).astype(np.int32)
    large = np.minimum(large, nb - 1)
    return (ret + np.where(n < max_exact, n, large)).astype(np.int32)


_GELU_C = math.sqrt(2.0 / math.pi)
_GELU_A = 0.044715


def _gelu(x):
    t = jnp.tanh(_GELU_C * (x + _GELU_A * (x * x * x)))
    return 0.5 * x * (1.0 + t)


def _gelu_and_grad(x):
    x2 = x * x
    t = jnp.tanh(_GELU_C * (x + _GELU_A * (x2 * x)))
    g = 0.5 * x * (1.0 + t)
    dg = 0.5 * (1.0 + t) + 0.5 * x * (1.0 - t * t) * (_GELU_C * (1.0 + 3.0 * _GELU_A * x2))
    return g, dg


def _sigmoid(x):
    return 1.0 / (1.0 + jnp.exp(-x))


def _mm(a, b, *, name, ta=False, tb=False, add=None, out_dtype=F32, bm=1024, bn=1024, bk=None, after=None):
    if ta:
        K, M = a.shape
    else:
        M, K = a.shape
    N = b.shape[0] if tb else b.shape[1]
    assert (b.shape[1] if tb else b.shape[0]) == K
    bm = _div(M, bm)
    bn = _div(N, bn)
    bk = K if bk is None else _div(K, bk)
    nk = K // bk
    a_spec = pl.BlockSpec((bk, bm), lambda i, j, k: (k, i)) if ta else pl.BlockSpec((bm, bk), lambda i, j, k: (i, k))
    b_spec = pl.BlockSpec((bn, bk), lambda i, j, k: (j, k)) if tb else pl.BlockSpec((bk, bn), lambda i, j, k: (k, j))
    o_spec = pl.BlockSpec((bm, bn), lambda i, j, k: (i, j))
    dims = (((0 if ta else 1,), (1 if tb else 0,)), ((), ()))
    has_add = add is not None

    def body(*refs):
        if has_add:
            a_ref, b_ref, add_ref, o_ref, *scratch = refs
        else:
            a_ref, b_ref, o_ref, *scratch = refs
            add_ref = None
        p = lax.dot_general(a_ref[...].astype(BF16), b_ref[...].astype(BF16), dims, preferred_element_type=F32)
        if nk == 1:
            if has_add:
                p = p + add_ref[...]
            o_ref[...] = p.astype(out_dtype)
        else:
            acc = scratch[0]
            k = pl.program_id(2)

            @pl.when(k == 0)
            def _():
                acc[...] = p

            @pl.when(k > 0)
            def _():
                acc[...] += p

            @pl.when(k == nk - 1)
            def _():
                r = acc[...]
                if has_add:
                    r = r + add_ref[...]
                o_ref[...] = r.astype(out_dtype)

    in_specs = [a_spec, b_spec] + ([o_spec] if has_add else [])
    args = (a, b) + ((add,) if has_add else ())
    body, in_specs, args = _ordered_after(body, len(args), in_specs, args, after)
    return pl.pallas_call(
        body, name=name, grid=(M // bm, N // bn, nk),
        in_specs=in_specs, out_specs=o_spec,
        out_shape=jax.ShapeDtypeStruct((M, N), out_dtype),
        scratch_shapes=[pltpu.VMEM((bm, bn), F32)] if nk > 1 else [],
        compiler_params=_cparams("parallel", "parallel", "arbitrary"),
    )(*args)


def _blocks_per_tile(c):
    nb = 1
    while (nb * c) % LANES or (nb * c < 1024 and nb < N_DEV):
        nb *= 2
    assert nb <= N_DEV and (nb * c) % LANES == 0, c
    return nb


def _mm_w8(a, w8, *, name, bm=1024):
    M, K = a.shape
    _, _, c = w8.shape
    nb = _blocks_per_tile(c)
    bm = _div(M, bm)

    def body(a_ref, w_ref, o_ref):
        a_ = a_ref[...]
        for t in range(nb):
            o_ref[:, t * c:(t + 1) * c] = jnp.dot(a_, w_ref[t], preferred_element_type=F32)

    return pl.pallas_call(
        body, name=name, grid=(M // bm, N_DEV // nb),
        in_specs=[pl.BlockSpec((bm, K), lambda i, j: (i, 0)), pl.BlockSpec((nb, K, c), lambda i, j: (j, 0, 0))],
        out_specs=pl.BlockSpec((bm, nb * c), lambda i, j: (i, j)),
        out_shape=jax.ShapeDtypeStruct((M, N_DEV * c), F32),
        compiler_params=_cparams("parallel", "parallel"),
    )(a, w8)


def _mm_w8t(dy, w8, *, name, add=None, out_dtype=F32, bm=1024, bn=1024, after=None, lead=None):
    M = dy.shape[-2]
    _, K, c = w8.shape
    nb = _blocks_per_tile(c)
    nk = N_DEV // nb
    bm, bn = _div(M, bm), _div(K, bn)
    has_add = add is not None
    dims = (((1,), (1,)), ((), ()))

    def body(*refs):
        if has_add:
            dy_ref, w_ref, add_ref, o_ref, acc = refs
        else:
            dy_ref, w_ref, o_ref, acc = refs
        p = lax.dot_general(dy_ref[:, 0:c], w_ref[0], dims, preferred_element_type=F32)
        for t in range(1, nb):
            p = p + lax.dot_general(dy_ref[:, t * c:(t + 1) * c], w_ref[t], dims, preferred_element_type=F32)
        k = pl.program_id(2)

        @pl.when(k == 0)
        def _():
            acc[...] = p

        @pl.when(k > 0)
        def _():
            acc[...] += p

        @pl.when(k == nk - 1)
        def _():
            r = acc[...]
            if has_add:
                r = r + add_ref[...]
            o_ref[...] = r.astype(out_dtype)

    o_spec = pl.BlockSpec((bm, bn), lambda i, j, k: (i, j))
    dy_spec = (pl.BlockSpec((bm, nb * c), lambda i, j, k: (i, k)) if lead is None
               else pl.BlockSpec((None, bm, nb * c), lambda i, j, k: (lead, i, k)))
    in_specs = [dy_spec, pl.BlockSpec((nb, bn, c), lambda i, j, k: (k, j, 0))]
    in_specs += [o_spec] if has_add else []
    args = (dy, w8) + ((add,) if has_add else ())
    body, in_specs, args = _ordered_after(body, len(args), in_specs, args, after)
    return pl.pallas_call(
        body, name=name, grid=(M // bm, K // bn, nk),
        in_specs=in_specs, out_specs=o_spec,
        out_shape=jax.ShapeDtypeStruct((M, K), out_dtype),
        scratch_shapes=[pltpu.VMEM((bm, bn), F32)],
        compiler_params=_cparams("parallel", "parallel", "arbitrary"),
    )(*args)


def _mm_gw8(x, dy, c, *, name, bk=1024, lead=None):
    T, K = x.shape
    nb = _blocks_per_tile(c)
    bk = _div(K, bk)
    dims = (((0,), (0,)), ((), ()))

    def body(x_ref, dy_ref, o_ref):
        x_ = x_ref[...]
        for t in range(nb):
            o_ref[t] = lax.dot_general(x_, dy_ref[:, t * c:(t + 1) * c], dims, preferred_element_type=F32).astype(BF16)

    dy_spec = (pl.BlockSpec((T, nb * c), lambda i, j: (0, j)) if lead is None
               else pl.BlockSpec((None, T, nb * c), lambda i, j: (lead, 0, j)))
    return pl.pallas_call(
        body, name=name, grid=(K // bk, N_DEV // nb),
        in_specs=[pl.BlockSpec((T, bk), lambda i, j: (0, i)), dy_spec],
        out_specs=pl.BlockSpec((nb, bk, c), lambda i, j: (j, i, 0)),
        out_shape=jax.ShapeDtypeStruct((N_DEV, K, c), BF16),
        compiler_params=_cparams("parallel", "parallel"),
    )(x, dy)


def _rms_fwd(x, g, *, name):
    T, D = x.shape
    tm = _div(T, 256, 8)

    def body(x_ref, g_ref, h_ref):
        xf = x_ref[...]
        r = lax.rsqrt(jnp.mean(xf * xf, axis=-1, keepdims=True) + EPS)
        h_ref[...] = ((xf * r) * g_ref[...]).astype(BF16)

    return pl.pallas_call(
        body, name=name, grid=(T // tm,),
        in_specs=[pl.BlockSpec((tm, D), lambda i: (i, 0)), pl.BlockSpec((1, D), lambda i: (0, 0))],
        out_specs=pl.BlockSpec((tm, D), lambda i: (i, 0)),
        out_shape=jax.ShapeDtypeStruct((T, D), BF16),
        compiler_params=_cparams("parallel"),
    )(x, g)


def _rms_bwd(x, g, dh, dres, *, name, want_bf16, after=None):
    T, D = x.shape
    tm = _div(T, 256, 8)

    def body(x_ref, g_ref, dh_ref, dres_ref, dx_ref, *rest):
        if want_bf16:
            dxb_ref, dg_ref = rest
        else:
            (dg_ref,) = rest
        xf = x_ref[...]
        r = lax.rsqrt(jnp.mean(xf * xf, axis=-1, keepdims=True) + EPS)
        xhat = xf * r
        dh_ = dh_ref[...]
        dy = dh_ * g_ref[...]
        dx = dres_ref[...] + r * (dy - xhat * jnp.mean(dy * xhat, axis=-1, keepdims=True))
        dx_ref[...] = dx
        if want_bf16:
            dxb_ref[...] = dx.astype(BF16)
        part = jnp.sum(dh_ * xhat, axis=0, keepdims=True)

        @pl.when(pl.program_id(0) == 0)
        def _():
            dg_ref[...] = part

        @pl.when(pl.program_id(0) > 0)
        def _():
            dg_ref[...] += part

    row = pl.BlockSpec((tm, D), lambda i: (i, 0))
    vec = pl.BlockSpec((1, D), lambda i: (0, 0))
    out_specs = [row] + ([row] if want_bf16 else []) + [vec]
    out_shape = ([jax.ShapeDtypeStruct((T, D), F32)] + ([jax.ShapeDtypeStruct((T, D), BF16)] if want_bf16 else [])
                 + [jax.ShapeDtypeStruct((1, D), F32)])
    body, in_specs, args = _ordered_after(body, 4, [row, vec, row, row], (x, g, dh, dres), after)
    return pl.pallas_call(
        body, name=name, grid=(T // tm,),
        in_specs=in_specs, out_specs=out_specs, out_shape=out_shape,
        compiler_params=_cparams("arbitrary"),
    )(*args)


def _loss_head(x, g, target, *, name):
    T, D = x.shape
    tm = _div(T, 256, 8)

    def body(x_ref, g_ref, t_ref, loss_ref, dx_ref, dxb_ref, dg_ref):
        xf = x_ref[...]
        r = lax.rsqrt(jnp.mean(xf * xf, axis=-1, keepdims=True) + EPS)
        xhat = xf * r
        gain = g_ref[...]
        err = xhat * gain - t_ref[...]
        lpart = 0.5 * jnp.sum(jnp.mean(err * err, axis=-1, keepdims=True), axis=0, keepdims=True)
        dh_ = err * (1.0 / D)
        dy = dh_ * gain
        dx = r * (dy - xhat * jnp.mean(dy * xhat, axis=-1, keepdims=True))
        dx_ref[...] = dx
        dxb_ref[...] = dx.astype(BF16)
        part = jnp.sum(dh_ * xhat, axis=0, keepdims=True)

        @pl.when(pl.program_id(0) == 0)
        def _():
            dg_ref[...] = part
            loss_ref[...] = jnp.broadcast_to(lpart, loss_ref.shape)

        @pl.when(pl.program_id(0) > 0)
        def _():
            dg_ref[...] += part
            loss_ref[...] += jnp.broadcast_to(lpart, loss_ref.shape)

    row = pl.BlockSpec((tm, D), lambda i: (i, 0))
    vec = pl.BlockSpec((1, D), lambda i: (0, 0))
    return pl.pallas_call(
        body, name=name, grid=(T // tm,),
        in_specs=[row, vec, row],
        out_specs=[pl.BlockSpec((8, LANES), lambda i: (0, 0)), row, row, vec],
        out_shape=[jax.ShapeDtypeStruct((8, LANES), F32), jax.ShapeDtypeStruct((T, D), F32),
                   jax.ShapeDtypeStruct((T, D), BF16), jax.ShapeDtypeStruct((1, D), F32)],
        compiler_params=_cparams("arbitrary"),
    )(x, g, target)


def _gate_cols(D):
    off_a = 3 * D // 2 + 2 * KV_WIDTH
    off_b = off_a + D
    cw = math.gcd(math.gcd(off_a, off_b), math.gcd(D, 512))
    return cw, off_a // cw, off_b // cw


def _merge_fwd(z, ya, yb, *, name):
    T, D = ya.shape
    cw, ba, bb = _gate_cols(D)
    tm = _div(T, 512, 8)

    def body(ga_ref, gb_ref, ya_ref, yb_ref, m_ref):
        m_ref[...] = (_sigmoid(ga_ref[...]) * ya_ref[...] + _sigmoid(gb_ref[...]) * yb_ref[...]).astype(BF16)

    blk = pl.BlockSpec((tm, cw), lambda i, j: (i, j))
    return pl.pallas_call(
        body, name=name, grid=(T // tm, D // cw),
        in_specs=[pl.BlockSpec((tm, cw), lambda i, j: (i, ba + j)), pl.BlockSpec((tm, cw), lambda i, j: (i, bb + j)), blk, blk],
        out_specs=blk, out_shape=jax.ShapeDtypeStruct((T, D), BF16),
        compiler_params=_cparams("parallel", "parallel"),
    )(z, z, ya, yb)


def _merge_bwd(z, ya, yb, dm, *, name, after=None):
    T, D = ya.shape
    cw, ba, bb = _gate_cols(D)
    nj = D // cw
    assert bb == ba + nj
    tm = _div(T, 512, 8)

    def body(g_ref, ya_ref, yb_ref, dm_ref, dy_ref, dz_ref):
        sig = _sigmoid(g_ref[...])
        dm_ = dm_ref[...]
        y = jnp.where(pl.program_id(1) == 0, ya_ref[...], yb_ref[...])
        dy_ref[...] = (dm_ * sig).astype(BF16)
        dz_ref[...] = (dm_ * y * (sig * (1.0 - sig))).astype(BF16)

    in_specs = [pl.BlockSpec((tm, cw), lambda i, s, j: (i, ba + s * nj + j)),
                pl.BlockSpec((tm, cw), lambda i, s, j: (i, j * (1 - s))),
                pl.BlockSpec((tm, cw), lambda i, s, j: (i, j * s)),
                pl.BlockSpec((tm, cw), lambda i, s, j: (i, j))]
    body, in_specs, args = _ordered_after(body, 4, in_specs, (z, ya, yb, dm), after)
    return pl.pallas_call(
        body, name=name, grid=(T // tm, 2, nj),
        in_specs=in_specs,
        out_specs=[pl.BlockSpec((None, tm, cw), lambda i, s, j: (s, i, j)),
                   pl.BlockSpec((tm, cw), lambda i, s, j: (i, ba + s * nj + j))],
        out_shape=[jax.ShapeDtypeStruct((2, T, D), BF16), jax.ShapeDtypeStruct(z.shape, BF16)],
        compiler_params=_cparams("parallel", "arbitrary", "arbitrary"),
    )(*args)


def _swiglu_mm_fwd(h, wg_t, wu_t, *, name, bm=1024, bn=512):
    T, D = h.shape
    F = wg_t.shape[0]
    bm, bn = _div(T, bm), _div(F, bn)
    dims = (((1,), (1,)), ((), ()))

    def body(h_ref, wg_ref, wu_ref, g_ref, u_ref, act_ref):
        h_ = h_ref[...]
        g = lax.dot_general(h_, wg_ref[...], dims, preferred_element_type=F32)
        u = lax.dot_general(h_, wu_ref[...], dims, preferred_element_type=F32)
        g_ref[...] = g
        u_ref[...] = u
        act_ref[...] = (g * _sigmoid(g) * u).astype(BF16)

    w_spec = pl.BlockSpec((bn, D), lambda i, j: (j, 0))
    o_spec = pl.BlockSpec((bm, bn), lambda i, j: (i, j))
    return pl.pallas_call(
        body, name=name, grid=(T // bm, F // bn),
        in_specs=[pl.BlockSpec((bm, D), lambda i, j: (i, 0)), w_spec, w_spec], out_specs=[o_spec] * 3,
        out_shape=[jax.ShapeDtypeStruct((T, F), F32), jax.ShapeDtypeStruct((T, F), F32), jax.ShapeDtypeStruct((T, F), BF16)],
        compiler_params=_cparams("parallel", "parallel"),
    )(h, wg_t, wu_t)


def _swiglu_mm_bwd(dx, w_down, gate, up, *, name, bm=1024, bn=512, after=None):
    T, D = dx.shape
    F = w_down.shape[0]
    bm, bn = _div(T, bm), _div(F, bn)
    dims = (((1,), (1,)), ((), ()))

    def body(dx_ref, w_ref, g_ref, u_ref, dg_ref, du_ref):
        d = lax.dot_general(dx_ref[...], w_ref[...], dims, preferred_element_type=F32)
        g = g_ref[...]
        s = _sigmoid(g)
        silu = g * s
        dg_ref[...] = (d * u_ref[...] * (s + silu * (1.0 - s))).astype(BF16)
        du_ref[...] = (d * silu).astype(BF16)

    o_spec = pl.BlockSpec((bm, bn), lambda i, j: (i, j))
    in_specs = [pl.BlockSpec((bm, D), lambda i, j: (i, 0)), pl.BlockSpec((bn, D), lambda i, j: (j, 0)), o_spec, o_spec]
    body, in_specs, args = _ordered_after(body, 4, in_specs, (dx, w_down, gate, up), after)
    out = jax.ShapeDtypeStruct((T, F), BF16)
    return pl.pallas_call(
        body, name=name, grid=(T // bm, F // bn), in_specs=in_specs, out_specs=[o_spec, o_spec], out_shape=[out, out],
        compiler_params=_cparams("parallel", "parallel"),
    )(*args)


def _sgu_fwd(z, gain, ws_b, bs_t, *, name):
    T = z.shape[0]
    SW = gain.shape[1]
    G = SW // BLOCK

    def body(zu_ref, zv_ref, gain_ref, ws_ref, bs_ref, a_ref):
        u = _gelu(zu_ref[...])
        vg = _gelu(zv_ref[...])
        r = lax.rsqrt(jnp.mean(vg * vg, axis=-1, keepdims=True) + EPS)
        vn = ((vg * r) * gain_ref[...]).astype(BF16)
        for g in range(G):
            sl = slice(g * BLOCK, (g + 1) * BLOCK)
            mixed = jnp.dot(ws_ref[g], vn[:, sl], preferred_element_type=F32) + bs_ref[:, g:g + 1]
            a_ref[:, sl] = (u[:, sl] * mixed).astype(BF16)

    return pl.pallas_call(
        body, name=name, grid=(T // BLOCK,),
        in_specs=[pl.BlockSpec((BLOCK, SW), lambda c: (c, 0)), pl.BlockSpec((BLOCK, SW), lambda c: (c, 1)),
                  pl.BlockSpec((1, SW), lambda c: (0, 0)), pl.BlockSpec((G, BLOCK, BLOCK), lambda c: (0, 0, 0)),
                  pl.BlockSpec((BLOCK, G), lambda c: (0, 0))],
        out_specs=pl.BlockSpec((BLOCK, SW), lambda c: (c, 0)),
        out_shape=jax.ShapeDtypeStruct((T, SW), BF16),
        compiler_params=_cparams("parallel"),
    )(z, z, gain, ws_b, bs_t)


def _sgu_bwd(z, gain, ws_b, bs_t, da, dz, *, name):
    T = z.shape[0]
    SW = gain.shape[1]
    G = SW // BLOCK

    def body(zu_ref, zv_ref, gain_ref, ws_ref, bs_ref, da_ref, dz_in_ref, dz_ref, dws_ref, dbs_ref, dgain_ref, dvn_ref):
        first = pl.program_id(0) == 0

        @pl.when(first)
        def _():
            dws_ref[...] = jnp.zeros_like(dws_ref)
            dbs_ref[...] = jnp.zeros_like(dbs_ref)
            dgain_ref[...] = jnp.zeros_like(dgain_ref)

        u, du = _gelu_and_grad(zu_ref[...])
        vg, dvg = _gelu_and_grad(zv_ref[...])
        r = lax.rsqrt(jnp.mean(vg * vg, axis=-1, keepdims=True) + EPS)
        xhat = vg * r
        gain_ = gain_ref[...]
        vn = (xhat * gain_).astype(BF16)
        da_ = da_ref[...]
        for g in range(G):
            sl = slice(g * BLOCK, (g + 1) * BLOCK)
            w = ws_ref[g]
            mixed = jnp.dot(w, vn[:, sl], preferred_element_type=F32) + bs_ref[:, g:g + 1]
            dmix = da_[:, sl] * u[:, sl]
            dz_ref[:, sl] = (da_[:, sl] * mixed * du[:, sl]).astype(BF16)
            dmb = dmix.astype(BF16)
            dws_ref[g] += lax.dot_general(dmb, vn[:, sl], (((1,), (1,)), ((), ())), preferred_element_type=F32)
            dbs_ref[:, g:g + 1] += jnp.sum(dmix, axis=-1, keepdims=True)
            dvn_ref[:, sl] = lax.dot_general(w, dmb, (((0,), (0,)), ((), ())), preferred_element_type=F32)
        dvn = dvn_ref[...]
        dgain_ref[...] += jnp.sum(dvn * xhat, axis=0, keepdims=True)
        dy = dvn * gain_
        dv_ = r * (dy - xhat * jnp.mean(dy * xhat, axis=-1, keepdims=True))
        dz_ref[:, SW:] = (dv_ * dvg).astype(BF16)

    row = pl.BlockSpec((BLOCK, SW), lambda c: (c, 0))
    return pl.pallas_call(
        body, name=name, grid=(T // BLOCK,),
        in_specs=[row, pl.BlockSpec((BLOCK, SW), lambda c: (c, 1)),
                  pl.BlockSpec((1, SW), lambda c: (0, 0)), pl.BlockSpec((G, BLOCK, BLOCK), lambda c: (0, 0, 0)),
                  pl.BlockSpec((BLOCK, G), lambda c: (0, 0)), row, _ANY],
        out_specs=[pl.BlockSpec((BLOCK, 2 * SW), lambda c: (c, 0)), pl.BlockSpec((G, BLOCK, BLOCK), lambda c: (0, 0, 0)),
                   pl.BlockSpec((BLOCK, G), lambda c: (0, 0)), pl.BlockSpec((1, SW), lambda c: (0, 0))],
        out_shape=[jax.ShapeDtypeStruct(dz.shape, dz.dtype),
                   jax.ShapeDtypeStruct((G, BLOCK, BLOCK), F32), jax.ShapeDtypeStruct((BLOCK, G), F32),
                   jax.ShapeDtypeStruct((1, SW), F32)],
        input_output_aliases={6: 0},
        scratch_shapes=[pltpu.VMEM((BLOCK, SW), F32)],
        compiler_params=_cparams("arbitrary"),
    )(z, z, gain, ws_b, bs_t, da, dz)


def _bias_table(rel_bias, bmap, *, name):
    H = rel_bias.shape[1]

    def body(rb_ref, bmap_ref, o_ref):
        bm_ = bmap_ref[...]
        for h in range(H):
            acc = jnp.zeros(bm_.shape, F32)
            for b in range(REL_BUCKETS):
                acc = jnp.where(bm_ == b, rb_ref[b, h], acc)
            o_ref[h] = acc

    return pl.pallas_call(
        body, name=name,
        in_specs=[pl.BlockSpec(memory_space=pltpu.SMEM), pl.BlockSpec(memory_space=pltpu.VMEM)],
        out_specs=pl.BlockSpec(memory_space=pltpu.VMEM),
        out_shape=jax.ShapeDtypeStruct((H, BLOCK, 3 * BLOCK), F32),
    )(rel_bias, bmap)


def _attn_probs(q_ref, kb, bias_ref, sink_ref, s_ref, n, T, group):
    H = s_ref.shape[0]
    for h in range(H):
        kv = h // group
        qh = q_ref[:, h * HEAD_DIM:(h + 1) * HEAD_DIM].astype(BF16)
        s_ref[h] = lax.dot_general(qh, kb[:, kv * HEAD_DIM:(kv + 1) * HEAD_DIM], (((1,), (1,)), ((), ())),
                                   preferred_element_type=F32)
    row = lax.broadcasted_iota(jnp.int32, (BLOCK, 3 * BLOCK), 0)
    col = lax.broadcasted_iota(jnp.int32, (BLOCK, 3 * BLOCK), 1)
    key_pos = n * BLOCK + col - BLOCK
    valid = (jnp.abs(col - BLOCK - row) <= BLOCK) & (key_pos >= 0) & (key_pos < T)
    s = s_ref[...] * (HEAD_DIM ** -0.5) + bias_ref[...]
    s = jnp.where(valid[None], s, NEG)
    sink = sink_ref[...]
    m = jnp.maximum(jnp.max(s, axis=-1, keepdims=True), sink)
    e = jnp.exp(s - m)
    es = jnp.exp(sink - m)
    inv = 1.0 / (jnp.sum(e, axis=-1, keepdims=True) + es)
    return e * inv, es * inv


def _attn_fwd(z, kpad, vpad, bias, sink, *, name):
    T = z.shape[0]
    H = bias.shape[0]
    AW = H * HEAD_DIM
    group = H // N_KV_HEADS

    def body(q_ref, k_ref, v_ref, bias_ref, sink_ref, o_ref, s_ref, p_ref):
        n = pl.program_id(0)
        start = pl.multiple_of(n * BLOCK, BLOCK)
        kb = k_ref[pl.ds(start, 3 * BLOCK), :]
        vb = v_ref[pl.ds(start, 3 * BLOCK), :]
        p, _ = _attn_probs(q_ref, kb, bias_ref, sink_ref, s_ref, n, T, group)
        p_ref[...] = p.astype(BF16)
        for h in range(H):
            kv = h // group
            o = jnp.dot(p_ref[h], vb[:, kv * HEAD_DIM:(kv + 1) * HEAD_DIM], preferred_element_type=F32)
            o_ref[:, h * HEAD_DIM:(h + 1) * HEAD_DIM] = o.astype(BF16)

    full_kv = pl.BlockSpec((T + 2 * BLOCK, KV_WIDTH), lambda n: (0, 0))
    return pl.pallas_call(
        body, name=name, grid=(T // BLOCK,),
        in_specs=[pl.BlockSpec((BLOCK, AW), lambda n: (n, 2)), full_kv, full_kv,
                  pl.BlockSpec((H, BLOCK, 3 * BLOCK), lambda n: (0, 0, 0)), pl.BlockSpec((H, 1, 1), lambda n: (0, 0, 0))],
        out_specs=pl.BlockSpec((BLOCK, AW), lambda n: (n, 0)),
        out_shape=jax.ShapeDtypeStruct((T, AW), BF16),
        scratch_shapes=[pltpu.VMEM((H, BLOCK, 3 * BLOCK), F32), pltpu.VMEM((H, BLOCK, 3 * BLOCK), BF16)],
        compiler_params=_cparams("parallel"),
    )(z, kpad, vpad, bias, sink)


def _attn_bwd(z, kpad, vpad, bias, sink, do, dz, *, name):
    T = z.shape[0]
    H = bias.shape[0]
    AW = H * HEAD_DIM
    group = H // N_KV_HEADS
    scale = HEAD_DIM ** -0.5

    def body(q_ref, k_ref, v_ref, bias_ref, sink_ref, do_ref, dz_in_ref, dq_ref, dk_ref, dv_ref, dbias_ref, dsink_ref,
             s_ref, dp_ref, p_ref, ds_ref):
        n = pl.program_id(0)

        @pl.when(n == 0)
        def _():
            dk_ref[...] = jnp.zeros_like(dk_ref)
            dv_ref[...] = jnp.zeros_like(dv_ref)
            dbias_ref[...] = jnp.zeros_like(dbias_ref)
            dsink_ref[...] = jnp.zeros_like(dsink_ref)

        start = pl.multiple_of(n * BLOCK, BLOCK)
        kb = k_ref[pl.ds(start, 3 * BLOCK), :]
        vb = v_ref[pl.ds(start, 3 * BLOCK), :]
        p, p_sink = _attn_probs(q_ref, kb, bias_ref, sink_ref, s_ref, n, T, group)
        s_ref[...] = p
        p_ref[...] = p.astype(BF16)
        for h in range(H):
            kv = h // group
            dp_ref[h] = lax.dot_general(do_ref[:, h * HEAD_DIM:(h + 1) * HEAD_DIM], vb[:, kv * HEAD_DIM:(kv + 1) * HEAD_DIM],
                                        (((1,), (1,)), ((), ())), preferred_element_type=F32)
        p = s_ref[...]
        dp = dp_ref[...]
        delta = jnp.sum(p * dp, axis=-1, keepdims=True)
        ds = p * (dp - delta)
        dbias_ref[...] += ds
        dsink_ref[...] += -(p_sink * delta)
        ds_ref[...] = ds.astype(BF16)
        for kv in range(N_KV_HEADS):
            ksl = slice(kv * HEAD_DIM, (kv + 1) * HEAD_DIM)
            dk_acc = jnp.zeros((3 * BLOCK, HEAD_DIM), F32)
            dv_acc = jnp.zeros((3 * BLOCK, HEAD_DIM), F32)
            for gi in range(group):
                h = kv * group + gi
                hsl = slice(h * HEAD_DIM, (h + 1) * HEAD_DIM)
                dsb = ds_ref[h]
                dq = jnp.dot(dsb, kb[:, ksl], preferred_element_type=F32) * scale
                dq_ref[:, hsl] = dq.astype(BF16)
                dk_acc = dk_acc + lax.dot_general(dsb, q_ref[:, hsl].astype(BF16), (((0,), (0,)), ((), ())),
                                                  preferred_element_type=F32)
                dv_acc = dv_acc + lax.dot_general(p_ref[h], do_ref[:, hsl], (((0,), (0,)), ((), ())),
                                                  preferred_element_type=F32)
            dk_ref[pl.ds(start, 3 * BLOCK), ksl] += dk_acc * scale
            dv_ref[pl.ds(start, 3 * BLOCK), ksl] += dv_acc

    full_kv = pl.BlockSpec((T + 2 * BLOCK, KV_WIDTH), lambda n: (0, 0))
    bias_spec = pl.BlockSpec((H, BLOCK, 3 * BLOCK), lambda n: (0, 0, 0))
    row = pl.BlockSpec((BLOCK, AW), lambda n: (n, 0))
    q_cols = pl.BlockSpec((BLOCK, AW), lambda n: (n, 2))
    band = (H, BLOCK, 3 * BLOCK)
    return pl.pallas_call(
        body, name=name, grid=(T // BLOCK,),
        in_specs=[q_cols, full_kv, full_kv, bias_spec, pl.BlockSpec((H, 1, 1), lambda n: (0, 0, 0)), row, _ANY],
        out_specs=[q_cols, full_kv, full_kv, bias_spec, pl.BlockSpec((H, BLOCK, 1), lambda n: (0, 0, 0))],
        out_shape=[jax.ShapeDtypeStruct(dz.shape, dz.dtype),
                   jax.ShapeDtypeStruct((T + 2 * BLOCK, KV_WIDTH), F32), jax.ShapeDtypeStruct((T + 2 * BLOCK, KV_WIDTH), F32),
                   jax.ShapeDtypeStruct(band, F32), jax.ShapeDtypeStruct((H, BLOCK, 1), F32)],
        input_output_aliases={6: 0},
        scratch_shapes=[pltpu.VMEM(band, F32), pltpu.VMEM(band, F32), pltpu.VMEM(band, BF16), pltpu.VMEM(band, BF16)],
        compiler_params=_cparams("arbitrary"),
    )(z, kpad, vpad, bias, sink, do, dz)


def _dkv_into(dkp, dvp, dz, *, name):
    T = dz.shape[0]
    D = (dz.shape[1] - 2 * KV_WIDTH) * 2 // 7
    col = (D + D // 2) // (2 * KV_WIDTH)
    assert col * 2 * KV_WIDTH == D + D // 2

    def body(dk_ref, dv_ref, dz_in_ref, o_ref):
        o_ref[:, :KV_WIDTH] = dk_ref[...].astype(BF16)
        o_ref[:, KV_WIDTH:] = dv_ref[...].astype(BF16)

    kv = pl.BlockSpec((BLOCK, KV_WIDTH), lambda n: (n + 1, 0))
    return pl.pallas_call(
        body, name=name, grid=(T // BLOCK,),
        in_specs=[kv, kv, _ANY], out_specs=pl.BlockSpec((BLOCK, 2 * KV_WIDTH), lambda n: (n, col)),
        out_shape=jax.ShapeDtypeStruct(dz.shape, dz.dtype), input_output_aliases={2: 0},
        compiler_params=_cparams("parallel"),
    )(dkp, dvp, dz)


def _kv_pad(z, *, name):
    T = z.shape[0]
    D = (z.shape[1] - 2 * KV_WIDTH) * 2 // 7
    kcol = (D + D // 2) // KV_WIDTH
    nb = T // BLOCK

    def body(k_ref, v_ref, ko_ref, vo_ref):
        b = pl.program_id(0)
        inside = (b >= 1) & (b <= nb)
        ko_ref[...] = jnp.where(inside, k_ref[...], 0.0).astype(BF16)
        vo_ref[...] = jnp.where(inside, v_ref[...], 0.0).astype(BF16)

    out = jax.ShapeDtypeStruct((T + 2 * BLOCK, KV_WIDTH), BF16)
    o_spec = pl.BlockSpec((BLOCK, KV_WIDTH), lambda b: (b, 0))
    return pl.pallas_call(
        body, name=name, grid=(nb + 2,),
        in_specs=[pl.BlockSpec((BLOCK, KV_WIDTH), lambda b: (jnp.clip(b - 1, 0, nb - 1), kcol)),
                  pl.BlockSpec((BLOCK, KV_WIDTH), lambda b: (jnp.clip(b - 1, 0, nb - 1), kcol + 1))],
        out_specs=[o_spec, o_spec], out_shape=[out, out],
        compiler_params=_cparams("parallel"),
    )(z, z)


def _attn_small_grads(dbias, dsink_rows, bmap, *, name):
    H = dbias.shape[0]

    def body(dbias_ref, dsink_ref, bmap_ref, drel_ref, ds_ref):
        bm_ = bmap_ref[...]
        for h in range(H):
            d = dbias_ref[h]
            for b in range(REL_BUCKETS):
                drel_ref[b, h] = jnp.sum(jnp.where(bm_ == b, d, 0.0))
            ds_ref[0, h] = jnp.sum(dsink_ref[h])

    vmem = pl.BlockSpec(memory_space=pltpu.VMEM)
    smem = pl.BlockSpec(memory_space=pltpu.SMEM)
    return pl.pallas_call(
        body, name=name, in_specs=[vmem, vmem, vmem], out_specs=[smem, smem],
        out_shape=[jax.ShapeDtypeStruct((REL_BUCKETS, H), F32), jax.ShapeDtypeStruct((1, H), F32)],
    )(dbias, dsink_rows, bmap)


def _local_step(x, target, weight, emit, flush, norm_mix, v_gain, w_s, b_s, sink, rel_bias, norm_ffn, norm_final):
    T, D = x.shape
    ws_b = w_s.astype(BF16)
    bs_t = b_s.T
    bmap = jnp.asarray(_bucket_map())
    sink = sink.reshape(-1, 1, 1)

    h = _rms_fwd(x, norm_mix, name="rms_mix")
    w_in = weight("w_in", h)
    z = _mm(h, w_in, tb=True, name="mm_z", bm=2048, bn=768)
    a = _sgu_fwd(z, v_gain, ws_b, bs_t, name="sgu_fwd")
    w_a = weight("w_a_out", a)
    ya = _mm_w8(a, w_a, name="mm_ya", bm=2048)
    kpad, vpad = _kv_pad(z, name="kv_pad")
    bias = _bias_table(rel_bias, bmap, name="bias_table")
    o = _attn_fwd(z, kpad, vpad, bias, sink, name="attn_fwd")
    w_b = weight("w_b_out", o)
    yb = _mm_w8(o, w_b, name="mm_yb", bm=2048)
    m = _merge_fwd(z, ya, yb, name="merge_fwd")
    w_o = weight("w_o", m)
    x1 = _mm(m, w_o, name="mm_x1", add=x, bm=2048, bn=512)
    h2 = _rms_fwd(x1, norm_ffn, name="rms_ffn")
    w_gate = weight("w_gate", h2)
    w_up = weight("w_up", h2)
    gate, up, act = _swiglu_mm_fwd(h2, w_gate, w_up, name="mm_gate_up")
    w_down = weight("w_down", act)
    x2 = _mm(act, w_down, name="mm_x2", add=x1, bm=1024, bn=1024, bk=2816)
    loss, dx2, dx2b, g_norm_final = _loss_head(x2, norm_final, target, name="loss_head")

    g_w_down = _mm(act, dx2b, ta=True, out_dtype=BF16, name="mm_gwdown", bm=512, bn=2048)
    tok = emit(("w_down",), (g_w_down,))
    dgate, dup = _swiglu_mm_bwd(dx2b, w_down, gate, up, name="mm_dact_swiglu", after=tok)
    tok = flush(dgate)
    g_w_gate = _mm(dgate, h2, ta=True, out_dtype=BF16, name="mm_gwgate", bm=512, bn=2048, after=tok)
    g_w_up = _mm(dup, h2, ta=True, out_dtype=BF16, name="mm_gwup", bm=512, bn=2048)
    tok = emit(("w_gate", "w_up"), (g_w_gate, g_w_up))
    dh2 = _mm(dgate, w_gate, name="mm_dh2a", bm=1024, bn=1024, bk=2816, after=tok)
    tok = flush(dh2)
    dh2 = _mm(dup, w_up, add=dh2, name="mm_dh2b", bm=1024, bn=1024, bk=2816, after=tok)
    dx1, dx1b, g_norm_ffn = _rms_bwd(x1, norm_ffn, dh2, dx2, name="rms_ffn_bwd", want_bf16=True)

    g_w_o = _mm(m, dx1b, ta=True, out_dtype=BF16, name="mm_gwo", bm=2048, bn=512)
    tok = emit(("w_o",), (g_w_o,))
    dm = _mm(dx1b, w_o, tb=True, name="mm_dm", bm=2048, bn=512, after=tok)
    tok = flush(dm)
    dy, dz = _merge_bwd(z, ya, yb, dm, name="merge_bwd", after=tok)
    g_w_a = _mm_gw8(a, dy, w_a.shape[2], name="mm_gwa", lead=0)
    g_w_b = _mm_gw8(o, dy, w_b.shape[2], name="mm_gwb", lead=1)
    tok = emit(("w_a_out", "w_b_out"), (g_w_a, g_w_b))
    da = _mm_w8t(dy, w_a, name="mm_da", bm=2048, bn=512, after=tok, lead=0)
    tok = flush(da)
    do = _mm_w8t(dy, w_b, out_dtype=BF16, name="mm_do", bm=2048, bn=512, after=tok, lead=1)
    dz, g_w_s, g_b_s_t, g_v_gain = _sgu_bwd(z, v_gain, ws_b, bs_t, da, dz, name="sgu_bwd")
    dz, dkp, dvp, dbias, dsink_rows = _attn_bwd(z, kpad, vpad, bias, sink, do, dz, name="attn_bwd")
    dz = _dkv_into(dkp, dvp, dz, name="dkv_into_dz")
    g_rel_bias, g_sink = _attn_small_grads(dbias, dsink_rows, bmap, name="attn_small_grads")
    g_w_in = _mm(dz, h, ta=True, out_dtype=BF16, name="mm_gwin", bm=768, bn=2048)
    tok = emit(("w_in",), (g_w_in,))
    dh = _mm(dz, w_in, name="mm_dh", bm=1024, bn=1024, bk=2560, after=tok)
    tok = flush(dh)
    grad_x, g_norm_mix = _rms_bwd(x, norm_mix, dh, dx1, name="rms_mix_bwd", want_bf16=False, after=tok)

    small = dict(norm_mix=g_norm_mix, sgu_v_gain=g_v_gain, sgu_w_s=g_w_s, sgu_b_s=g_b_s_t.T, attn_sink=g_sink,
                 rel_bias=g_rel_bias, norm_ffn=g_norm_ffn, norm_final=g_norm_final)
    return loss, grad_x, small


def _position():
    return lax.axis_index("x"), lax.axis_index("y"), lax.axis_index("c")


def _other_chips(x, y):
    return [(1 - x, y), (x, 1 - y), (1 - x, 1 - y)]


def _slot(px, py, pc):
    return 4 * px + 2 * py + pc


_HBM = pl.BlockSpec(memory_space=pltpu.HBM)
_SEM = pl.BlockSpec(memory_space=pltpu.SEMAPHORE)
_DATAFLOW = pltpu.SideEffectType.DATAFLOW_SIDE_EFFECTING


def _in_hbm(a):
    return pltpu.with_memory_space_constraint(a, pltpu.HBM)


def _own_slot(shard, pos, *, name, after=None):
    R, C = shard.shape
    tr = _div(R, 256, 16)

    def body(pos_ref, w_ref, o_ref):
        o_ref[...] = w_ref[...].astype(BF16)

    body, in_specs, args = _ordered_after(body, 2, [pl.BlockSpec((tr, C), lambda i, pos_ref: (i, 0))], (pos, shard), after)
    grid_spec = pltpu.PrefetchScalarGridSpec(
        num_scalar_prefetch=1, grid=(R // tr,), in_specs=in_specs,
        out_specs=pl.BlockSpec((None, tr, C), lambda i, pos_ref: (pos_ref[0], i, 0)))
    return pl.pallas_call(
        body, name=name, grid_spec=grid_spec,
        out_shape=jax.ShapeDtypeStruct((N_DEV, R, C), BF16),
        compiler_params=_cparams("parallel"),
    )(*args)


def _ag_copies(w, land_ref, send_sems, recv_sems):
    x, y, c = _position()
    mine = land_ref.at[_slot(x, y, c)]
    targets = [(px, py, c) for px, py in _other_chips(x, y)] + [(x, y, 1 - c)]
    return [pltpu.make_async_remote_copy(src_ref=mine, dst_ref=mine, send_sem=send_sems.at[4 * w + k],
                                         recv_sem=recv_sems.at[4 * w + k], device_id=to, device_id_type=MESH)
            for k, to in enumerate(targets)]


def _ag_start(buffers, groups, *, name):
    lands = [buffers[i] for g in groups for i in g]
    n, ng = len(lands), len(groups)
    sizes = [len(g) for g in groups]

    def body(*refs):
        land_refs = refs[:n]
        sems = refs[n:n + 2 * ng]
        token = refs[-1]
        i = 0
        for g in range(ng):
            for w in range(sizes[g]):
                for cp in _ag_copies(w, land_refs[i], sems[2 * g], sems[2 * g + 1]):
                    cp.start()
                i += 1
        token[...] = jnp.zeros_like(token)

    sem_shapes = [pltpu.SemaphoreType.DMA((4 * k,)) for k in sizes for _ in range(2)]
    outs = pl.pallas_call(
        body, name=name,
        in_specs=[_HBM] * n,
        out_specs=tuple([_SEM] * (2 * ng) + [_HBM] * n + [pl.BlockSpec(memory_space=pltpu.VMEM)]),
        out_shape=tuple(sem_shapes + [pltpu.HBM(a.shape, a.dtype) for a in lands] + [jax.ShapeDtypeStruct((8, LANES), F32)]),
        input_output_aliases={i: 2 * ng + i for i in range(n)},
        compiler_params=pltpu.CompilerParams(has_side_effects=_DATAFLOW),
    )(*[_in_hbm(a) for a in lands])
    sems, thru = outs[:2 * ng], outs[2 * ng:2 * ng + n]
    result, i = [], 0
    for g in range(ng):
        k = sizes[g]
        result.append((sems[2 * g], sems[2 * g + 1], list(thru[i:i + k])))
        i += k
    return result, outs[-1]


def _ag_wait(send_sems, recv_sems, lands, after, *, name):
    n = len(lands)

    def body(*refs):
        land_refs = refs[:n]
        send_ref, recv_ref = refs[n], refs[n + 1]
        token = refs[-1]
        for w in range(n):
            for cp in _ag_copies(w, land_refs[w], send_ref, recv_ref):
                cp.wait_send()
                cp.wait_recv()
        token[...] = jnp.zeros_like(token)

    outs = pl.pallas_call(
        body, name=name,
        in_specs=[_HBM] * n + [_SEM, _SEM, _ANY],
        out_specs=tuple([_HBM] * n + [pl.BlockSpec(memory_space=pltpu.VMEM)]),
        out_shape=tuple([pltpu.HBM(a.shape, a.dtype) for a in lands] + [jax.ShapeDtypeStruct((8, LANES), F32)]),
        input_output_aliases={i: i for i in range(n)},
        compiler_params=pltpu.CompilerParams(has_side_effects=_DATAFLOW),
    )(*lands, send_sems, recv_sems, after)
    return list(outs[:n]), outs[n]


def _ag_forward(lands, *, name, after=None):
    n = len(lands)

    def body(*refs):
        in_refs, out_refs = refs[:n], refs[n:2 * n]
        send_sems, recv_sems = refs[2 * n:]
        x, y, c = _position()
        copies = []
        for w in range(n):
            for k, (px, py) in enumerate(_other_chips(x, y)):
                cp = pltpu.make_async_remote_copy(
                    src_ref=in_refs[w].at[_slot(px, py, c)], dst_ref=out_refs[w].at[_slot(px, py, c)],
                    send_sem=send_sems.at[3 * w + k], recv_sem=recv_sems.at[3 * w + k],
                    device_id=(x, y, 1 - c), device_id_type=MESH)
                cp.start()
                copies.append(cp)
        for cp in copies:
            cp.wait()

    body, in_specs, args = _ordered_after(body, n, [_ANY] * n, tuple(lands), after)
    return pl.pallas_call(
        body, name=name,
        in_specs=in_specs, out_specs=[_ANY] * n,
        out_shape=[jax.ShapeDtypeStruct(a.shape, a.dtype) for a in lands],
        input_output_aliases={i: i for i in range(n)},
        scratch_shapes=[pltpu.SemaphoreType.DMA((3 * n,)), pltpu.SemaphoreType.DMA((3 * n,))],
    )(*args)


def _sibling_copies(w, g8_ref, land_ref, send_sems, recv_sems):
    x, y, c = _position()
    return [pltpu.make_async_remote_copy(src_ref=g8_ref.at[2 * p + (1 - c)], dst_ref=land_ref.at[p],
                                         send_sem=send_sems.at[4 * w + p], recv_sem=recv_sems.at[4 * w + p],
                                         device_id=(x, y, 1 - c), device_id_type=MESH)
            for p in range(4)]


def _chip_copies(w, sums_ref, land_ref, send_sems, recv_sems):
    x, y, c = _position()
    return [pltpu.make_async_remote_copy(src_ref=sums_ref.at[2 * px + py], dst_ref=land_ref.at[k],
                                         send_sem=send_sems.at[3 * w + k], recv_sem=recv_sems.at[3 * w + k],
                                         device_id=(px, py, c), device_id_type=MESH)
            for k, (px, py) in enumerate(_other_chips(x, y))]


def _copies_start(copies, per_weight, srcs, *, name):
    n = len(srcs)
    lands = [lax.empty((per_weight,) + s.shape[1:], s.dtype) for s in srcs]

    def body(*refs):
        src_refs, land_refs = refs[:n], refs[n:2 * n]
        send_sems, recv_sems = refs[2 * n], refs[2 * n + 1]
        token = refs[-1]
        for w in range(n):
            for cp in copies(w, src_refs[w], land_refs[w], send_sems, recv_sems):
                cp.start()
        token[...] = jnp.zeros_like(token)

    outs = pl.pallas_call(
        body, name=name,
        in_specs=[_HBM] * (2 * n),
        out_specs=tuple([_SEM, _SEM] + [_HBM] * (2 * n) + [pl.BlockSpec(memory_space=pltpu.VMEM)]),
        out_shape=tuple([pltpu.SemaphoreType.DMA((per_weight * n,)), pltpu.SemaphoreType.DMA((per_weight * n,))]
                        + [pltpu.HBM(a.shape, a.dtype) for a in srcs + lands] + [jax.ShapeDtypeStruct((8, LANES), F32)]),
        input_output_aliases={i: 2 + i for i in range(2 * n)},
        compiler_params=pltpu.CompilerParams(has_side_effects=_DATAFLOW),
    )(*[_in_hbm(a) for a in srcs + lands])
    return outs[0], outs[1], list(outs[2:2 + n]), list(outs[2 + n:2 + 2 * n]), outs[-1]


def _copies_wait(copies, send_sems, recv_sems, srcs, lands, after, *, name):
    n = len(srcs)

    def body(*refs):
        src_refs, land_refs = refs[:n], refs[n:2 * n]
        send_ref, recv_ref = refs[2 * n], refs[2 * n + 1]
        for w in range(n):
            for cp in copies(w, src_refs[w], land_refs[w], send_ref, recv_ref):
                cp.wait_send()
                cp.wait_recv()

    outs = pl.pallas_call(
        body, name=name,
        in_specs=[_HBM] * (2 * n) + [_SEM, _SEM, _ANY],
        out_specs=tuple([_HBM] * (2 * n)),
        out_shape=tuple(pltpu.HBM(a.shape, a.dtype) for a in srcs + lands),
        input_output_aliases={i: i for i in range(2 * n)},
        compiler_params=pltpu.CompilerParams(has_side_effects=_DATAFLOW),
    )(*srcs, *lands, send_sems, recv_sems, after)
    return list(outs[:n]), list(outs[n:])


def _chip_sums(g8, from_sibling, pos, *, name):
    _, R, C = g8.shape
    tr = _div(R, 512, 16)

    def body(pos_ref, g_ref, s_ref, o_ref):
        o_ref[...] = (g_ref[...].astype(F32) + s_ref[...].astype(F32)).astype(BF16)

    grid_spec = pltpu.PrefetchScalarGridSpec(
        num_scalar_prefetch=1, grid=(4, R // tr),
        in_specs=[pl.BlockSpec((None, tr, C), lambda p, i, pos_ref: (2 * p + pos_ref[2], i, 0)),
                  pl.BlockSpec((None, tr, C), lambda p, i, pos_ref: (p, i, 0))],
        out_specs=pl.BlockSpec((None, tr, C), lambda p, i, pos_ref: (p, i, 0)))
    return pl.pallas_call(
        body, name=name, grid_spec=grid_spec,
        out_shape=jax.ShapeDtypeStruct((4, R, C), BF16),
        compiler_params=_cparams("parallel", "parallel"),
    )(pos, g8, from_sibling)


def _small_all_reduce(packed, after, *, name):
    R, L = packed.shape

    def body(x_ref, sum_ref, gath_ref, send_sems, recv_sems, local_sem):
        x, y, c = _position()
        me, sibling = (x, y, c), (x, y, 1 - c)
        chips = _other_chips(x, y)

        def rows(px, py, pc):
            return gath_ref.at[pl.ds(_slot(px, py, pc) * R, R), :]

        def copy(k, block, to, src=None):
            return pltpu.make_async_remote_copy(
                src_ref=rows(*block) if src is None else src, dst_ref=rows(*block),
                send_sem=send_sems.at[k], recv_sem=recv_sems.at[k], device_id=to, device_id_type=MESH)

        mine = pltpu.make_async_copy(x_ref, rows(*me), local_sem)
        mine.start()
        first = [copy(0, me, sibling, src=x_ref)]
        first += [copy(1 + j, me, (*chip, c), src=x_ref) for j, chip in enumerate(chips)]
        for cp in first:
            cp.start()
        passed = [copy(4 + j, (*chip, c), sibling) for j, chip in enumerate(chips)]
        for j, chip in enumerate(chips):
            copy(1 + j, (*chip, c), me).wait_recv()
            passed[j].start()
        copy(0, sibling, me).wait_recv()
        for j, chip in enumerate(chips):
            copy(4 + j, (*chip, 1 - c), me).wait_recv()
        for cp in first + passed:
            cp.wait_send()
        mine.wait()
        acc = gath_ref[0:R, :]
        for d in range(1, N_DEV):
            acc = acc + gath_ref[d * R:(d + 1) * R, :]
        sum_ref[...] = acc

    vmem = pl.BlockSpec(memory_space=pltpu.VMEM)
    body, in_specs, args = _ordered_after(body, 1, [vmem], (packed,), after)
    return pl.pallas_call(
        body, name=name, in_specs=in_specs, out_specs=vmem,
        out_shape=jax.ShapeDtypeStruct((R, L), F32),
        scratch_shapes=[pltpu.VMEM((N_DEV * R, L), F32), pltpu.SemaphoreType.DMA((7,)), pltpu.SemaphoreType.DMA((7,)),
                        pltpu.SemaphoreType.DMA],
        compiler_params=pltpu.CompilerParams(vmem_limit_bytes=VMEM_LIMIT),
    )(*args)


def _adamw_math(w, g, m, v):
    m = ADAM_B1 * m + (1.0 - ADAM_B1) * g
    v = ADAM_B2 * v + (1.0 - ADAM_B2) * (g * g)
    m_hat = m / (1.0 - ADAM_B1 ** ADAM_STEP)
    v_hat = v / (1.0 - ADAM_B2 ** ADAM_STEP)
    delta = -ADAM_LR * (m_hat / (jnp.sqrt(v_hat) + ADAM_EPS) + ADAM_WD * w)
    return delta, m, v


def _adamw_shard(w, m, v, g8, from_sibling, from_chips, pos, *, name):
    R, C = w.shape
    tr = _div(R, 256, 16)

    def body(pos_ref, w_ref, m_ref, v_ref, g_ref, s_ref, r_ref, go_ref, d_ref, mo_ref, vo_ref):
        g = g_ref[...].astype(F32) + s_ref[...].astype(F32)
        for k in range(3):
            g = g + r_ref[k].astype(F32)
        delta, m_, v_ = _adamw_math(w_ref[...], g, m_ref[...], v_ref[...])
        go_ref[...] = g
        d_ref[...] = delta
        mo_ref[...] = m_
        vo_ref[...] = v_

    blk = pl.BlockSpec((tr, C), lambda i, pos_ref: (i, 0))
    grid_spec = pltpu.PrefetchScalarGridSpec(
        num_scalar_prefetch=1, grid=(R // tr,),
        in_specs=[blk, blk, blk,
                  pl.BlockSpec((None, tr, C), lambda i, pos_ref: (pos_ref[0], i, 0)),
                  pl.BlockSpec((None, tr, C), lambda i, pos_ref: (pos_ref[1], i, 0)),
                  pl.BlockSpec((3, tr, C), lambda i, pos_ref: (0, i, 0))],
        out_specs=[blk] * 4)
    out = jax.ShapeDtypeStruct((R, C), F32)
    return pl.pallas_call(
        body, name=name, grid_spec=grid_spec, out_shape=[out] * 4,
        compiler_params=_cparams("parallel"),
    )(pos, w, m, v, g8, from_sibling, from_chips)


def _adamw_small(w, g, m, v, *, name):
    R, L = w.shape

    def body(w_ref, g_ref, m_ref, v_ref, d_ref, mo_ref, vo_ref):
        delta, m_, v_ = _adamw_math(w_ref[...], g_ref[...], m_ref[...], v_ref[...])
        d_ref[...] = delta
        mo_ref[...] = m_
        vo_ref[...] = v_

    vmem = pl.BlockSpec(memory_space=pltpu.VMEM)
    out = jax.ShapeDtypeStruct((R, L), F32)
    return pl.pallas_call(body, name=name, in_specs=[vmem] * 4, out_specs=[vmem] * 3, out_shape=[out] * 3)(w, g, m, v)


_TILE = 8 * LANES


def _pack(pieces):
    rows = []
    for p in pieces:
        flat = p.reshape(-1).astype(F32)
        padded = -(-flat.shape[0] // _TILE) * _TILE
        rows.append(jnp.pad(flat, (0, padded - flat.shape[0])).reshape(-1, LANES))
    return jnp.concatenate(rows, axis=0)


def _unpack(packed, like):
    out, r = [], 0
    for p in like:
        size = int(np.prod(p.shape)) if p.shape else 1
        nrows = -(-size // _TILE) * 8
        out.append(packed[r:r + nrows].reshape(-1)[:size].reshape(p.shape))
        r += nrows
    return out


_BIG = ("w_in", "w_a_out", "w_b_out", "w_o", "w_gate", "w_up", "w_down")
_TRANSPOSED = ("w_in", "w_gate", "w_up")
_COL_SHARDED = ("w_a_out", "w_b_out")
_GATHER_GROUPS = (("w_in",), ("w_a_out", "w_b_out", "w_o"), ("w_gate", "w_up"), ("w_down",))
_START_AFTER_WAIT = {0: (1, 2), 2: (3,)}
_SMALL = ("norm_mix", "sgu_v_gain", "sgu_w_s", "sgu_b_s", "attn_sink", "rel_bias", "norm_ffn", "norm_final")
_ORDER = ("w_in", "norm_mix", "sgu_v_gain", "sgu_w_s", "sgu_b_s", "w_a_out", "attn_sink", "rel_bias", "w_b_out", "w_o",
          "norm_ffn", "w_gate", "w_up", "w_down", "norm_final")


def _shard(name, a):
    return jnp.swapaxes(a, 1, 2)[0] if name in _TRANSPOSED else a[0]


def _unshard(name, a):
    return jnp.swapaxes(a[None], 1, 2) if name in _TRANSPOSED else a[None]


def _whole(name, gathered):
    _, r, c = gathered.shape
    return gathered if name in _COL_SHARDED else gathered.reshape(N_DEV * r, c)


def _blocks(name, grad):
    if name in _COL_SHARDED:
        return grad
    r, c = grad.shape
    return grad.reshape(N_DEV, r // N_DEV, c)


def kernel(x, w_in, norm_mix, sgu_v_gain, sgu_w_s, sgu_b_s, w_a_out, attn_sink, rel_bias, w_b_out, w_o, norm_ffn, w_gate, w_up, w_down, norm_final, loss_target, m_w_in, m_norm_mix, m_sgu_v_gain, m_sgu_w_s, m_sgu_b_s, m_w_a_out, m_attn_sink, m_rel_bias, m_w_b_out, m_w_o, m_norm_ffn, m_w_gate, m_w_up, m_w_down, m_norm_final, v_w_in, v_norm_mix, v_sgu_v_gain, v_sgu_w_s, v_sgu_b_s, v_w_a_out, v_attn_sink, v_rel_bias, v_w_b_out, v_w_o, v_norm_ffn, v_w_gate, v_w_up, v_w_down, v_norm_final):
    w = dict(w_in=w_in, norm_mix=norm_mix, sgu_v_gain=sgu_v_gain, sgu_w_s=sgu_w_s, sgu_b_s=sgu_b_s, w_a_out=w_a_out,
             attn_sink=attn_sink, rel_bias=rel_bias, w_b_out=w_b_out, w_o=w_o, norm_ffn=norm_ffn, w_gate=w_gate,
             w_up=w_up, w_down=w_down, norm_final=norm_final)
    m = dict(w_in=m_w_in, norm_mix=m_norm_mix, sgu_v_gain=m_sgu_v_gain, sgu_w_s=m_sgu_w_s, sgu_b_s=m_sgu_b_s,
             w_a_out=m_w_a_out, attn_sink=m_attn_sink, rel_bias=m_rel_bias, w_b_out=m_w_b_out, w_o=m_w_o,
             norm_ffn=m_norm_ffn, w_gate=m_w_gate, w_up=m_w_up, w_down=m_w_down, norm_final=m_norm_final)
    v = dict(w_in=v_w_in, norm_mix=v_norm_mix, sgu_v_gain=v_sgu_v_gain, sgu_w_s=v_sgu_w_s, sgu_b_s=v_sgu_b_s,
             w_a_out=v_w_a_out, attn_sink=v_attn_sink, rel_bias=v_rel_bias, w_b_out=v_w_b_out, w_o=v_w_o,
             norm_ffn=v_norm_ffn, w_gate=v_w_gate, w_up=v_w_up, w_down=v_w_down, norm_final=v_norm_final)
    xc, yc, cc = _position()
    pos = jnp.stack([_slot(xc, yc, cc), 2 * xc + yc, cc]).astype(jnp.int32)

    in_flight, full = {}, {}

    def start_gather(groups, after):
        names = [n for gi in groups for n in _GATHER_GROUPS[gi]]
        buffers = [_own_slot(_shard(n, w[n]), pos, name="own_slot_" + n, after=after) for n in names]
        flights, token = _ag_start(buffers, [[names.index(n) for n in _GATHER_GROUPS[gi]] for gi in groups],
                                   name="ag_start_%d" % groups[0])
        in_flight.update(zip(groups, flights))
        return token

    def weight(name, after):
        if name not in full:
            gi = next(i for i, grp in enumerate(_GATHER_GROUPS) if name in grp)
            send_sems, recv_sems, lands = in_flight[gi]
            lands, token = _ag_wait(send_sems, recv_sems, lands, after, name="ag_wait_%d" % gi)
            started = start_gather(_START_AFTER_WAIT[gi], token) if gi in _START_AFTER_WAIT else None
            gathered = _ag_forward(lands, name="ag_forward_%d" % gi, after=started)
            full.update({n: _whole(n, g) for n, g in zip(_GATHER_GROUPS[gi], gathered)})
        return full[name]

    start_gather((0,), None)

    to_sibling, reducing = [], {}

    def emit(names, grads):
        g8 = [_blocks(n, g) for n, g in zip(names, grads)]
        send_sems, recv_sems, g8, lands, token = _copies_start(_sibling_copies, 4, g8, name="rs_sibling_start_" + names[0])
        to_sibling.append((names, send_sems, recv_sems, g8, lands))
        return token

    def flush(after):
        names, send_sems, recv_sems, g8, lands = to_sibling.pop()
        g8, from_sibling = _copies_wait(_sibling_copies, send_sems, recv_sems, g8, lands, after,
                                        name="rs_sibling_wait_" + names[0])
        sums4 = [_chip_sums(g, s, pos, name="chip_sums_" + n) for n, g, s in zip(names, g8, from_sibling)]
        send_sems, recv_sems, sums4, lands, token = _copies_start(_chip_copies, 3, sums4, name="rs_chips_start_" + names[0])
        reducing[names] = (g8, from_sibling, send_sems, recv_sems, sums4, lands)
        return token

    loss, grad_x, small_grads_local = _local_step(
        x[0], loss_target[0], weight, emit, flush, norm_mix, sgu_v_gain, sgu_w_s[0], sgu_b_s[0], attn_sink, rel_bias,
        norm_ffn, norm_final[None])

    out_g, out_d, out_m, out_v = {}, {}, {}, {}
    small_like = [w[n] for n in _SMALL]
    small_w = _pack(small_like)
    packed = _pack([small_grads_local[n] for n in _SMALL] + [loss[0, 0]])
    after = grad_x
    for gi, (names, (g8, from_sibling, send_sems, recv_sems, sums4, lands)) in enumerate(reducing.items()):
        _, from_chips = _copies_wait(_chip_copies, send_sems, recv_sems, sums4, lands, after,
                                     name="rs_chips_wait_" + names[0])
        if gi == len(reducing) - 1:
            summed = _small_all_reduce(packed, from_chips[0], name="small_all_reduce")
        for i, n in enumerate(names):
            g, d, m_, v_ = _adamw_shard(_shard(n, w[n]), _shard(n, m[n]), _shard(n, v[n]), g8[i], from_sibling[i],
                                        from_chips[i], pos, name="adamw_" + n)
            out_g[n], out_d[n], out_m[n], out_v[n] = (_unshard(n, o) for o in (g, d, m_, v_))
            after = d
    *small_grads, loss_sum = _unpack(summed, small_like + [jax.ShapeDtypeStruct((), F32)])
    d_s, m_s, v_s = _adamw_small(small_w, summed[:small_w.shape[0]], _pack([m[n] for n in _SMALL]),
                                 _pack([v[n] for n in _SMALL]), name="adamw_small")
    for n, g, d, m_, v_ in zip(_SMALL, small_grads, _unpack(d_s, small_like), _unpack(m_s, small_like), _unpack(v_s, small_like)):
        out_g[n], out_d[n], out_m[n], out_v[n] = g, d, m_, v_

    return (loss_sum, grad_x[None], *[out_g[n] for n in _ORDER], *[out_d[n] for n in _ORDER],
            *[out_m[n] for n in _ORDER], *[out_v[n] for n in _ORDER])
```

```python
import functools
import math

import numpy as np
import jax
import jax.numpy as jnp
from jax import lax
from jax.experimental import pallas as pl
from jax.experimental.pallas import tpu as pltpu

F32 = jnp.float32
BF16 = jnp.bfloat16

EPS = 1e-6
NEG = -1e30
HEAD_DIM = 128
BLOCK = 128
N_KV_HEADS = 2
KV_WIDTH = N_KV_HEADS * HEAD_DIM
REL_BUCKETS = 32
REL_MAX_DIST = 128

ADAM_LR = 0.001
ADAM_B1 = 0.9
ADAM_B2 = 0.999
ADAM_EPS = 1e-08
ADAM_WD = 0.01
ADAM_STEP = 10

N_DEV = 8
LANES = 128
VMEM_LIMIT = 56 * 1024 * 1024
MESH = pl.DeviceIdType.MESH


def _cparams(*sem):
    return pltpu.CompilerParams(dimension_semantics=sem, vmem_limit_bytes=VMEM_LIMIT)


def _div(n, target, mult=LANES):
    best = None
    for d in range(mult, min(n, target) + 1, mult):
        if n % d == 0:
            best = d
    assert best is not None, (n, target, mult)
    return best


_ANY = pl.BlockSpec(memory_space=pl.ANY)


def _ordered_after(body, n_inputs, in_specs, args, after):
    if after is None:
        return body, in_specs, args

    def wrapped(*refs):
        return body(*refs[:n_inputs], *refs[n_inputs + 1:])

    return wrapped, list(in_specs) + [_ANY], tuple(args) + (after,)


def _bucket_map():
    nb = REL_BUCKETS // 2
    qi = np.arange(BLOCK)[:, None]
    kj = np.arange(3 * BLOCK)[None, :]
    rel = kj - BLOCK - qi
    ret = np.where(rel > 0, nb, 0)
    n = np.abs(rel)
    max_exact = nb // 2
    nf = np.maximum(n, 1).astype(np.float32)
    large = max_exact + (np.log(nf / np.float32(max_exact)) / np.float32(math.log(REL_MAX_DIST / max_exact))
                         * np.float32(nb - max_exact)).astype(np.int32)
    large = np.minimum(large, nb - 1)
    return (ret + np.where(n < max_exact, n, large)).astype(np.int32)


_GELU_C = math.sqrt(2.0 / math.pi)
_GELU_A = 0.044715


def _gelu(x):
    t = jnp.tanh(_GELU_C * (x + _GELU_A * (x * x * x)))
    return 0.5 * x * (1.0 + t)


def _gelu_and_grad(x):
    x2 = x * x
    t = jnp.tanh(_GELU_C * (x + _GELU_A * (x2 * x)))
    g = 0.5 * x * (1.0 + t)
    dg = 0.5 * (1.0 + t) + 0.5 * x * (1.0 - t * t) * (_GELU_C * (1.0 + 3.0 * _GELU_A * x2))
    return g, dg


def _sigmoid(x):
    return 1.0 / (1.0 + jnp.exp(-x))


def _mm(a, b, *, name, ta=False, tb=False, add=None, out_dtype=F32, bm=1024, bn=1024, bk=None, after=None,
        row_blocks=None, into=None):
    if ta:
        K, M = a.shape
    else:
        M, K = a.shape
    N = b.shape[0] if tb else b.shape[1]
    assert (b.shape[1] if tb else b.shape[0]) == K
    bm = _div(M, bm)
    bn = _div(N, bn)
    bk = K if bk is None else _div(K, bk)
    nk = K // bk
    i0, ni = (0, M // bm) if row_blocks is None else row_blocks
    a_spec = (pl.BlockSpec((bk, bm), lambda i, j, k: (k, i + i0)) if ta
              else pl.BlockSpec((bm, bk), lambda i, j, k: (i + i0, k)))
    b_spec = pl.BlockSpec((bn, bk), lambda i, j, k: (j, k)) if tb else pl.BlockSpec((bk, bn), lambda i, j, k: (k, j))
    o_spec = pl.BlockSpec((bm, bn), lambda i, j, k: (i + i0, j))
    dims = (((0 if ta else 1,), (1 if tb else 0,)), ((), ()))
    has_add = add is not None

    def body(*refs):
        if has_add:
            a_ref, b_ref, add_ref, o_ref, *scratch = refs
        else:
            a_ref, b_ref, o_ref, *scratch = refs
            add_ref = None
        p = lax.dot_general(a_ref[...].astype(BF16), b_ref[...].astype(BF16), dims, preferred_element_type=F32)
        if nk == 1:
            if has_add:
                p = p + add_ref[...]
            o_ref[...] = p.astype(out_dtype)
        else:
            acc = scratch[0]
            k = pl.program_id(2)

            @pl.when(k == 0)
            def _():
                acc[...] = p

            @pl.when(k > 0)
            def _():
                acc[...] += p

            @pl.when(k == nk - 1)
            def _():
                r = acc[...]
                if has_add:
                    r = r + add_ref[...]
                o_ref[...] = r.astype(out_dtype)

    in_specs = [a_spec, b_spec] + ([o_spec] if has_add else [])
    args = (a, b) + ((add,) if has_add else ())
    aliases = {}
    if into is not None:
        body, in_specs, args = _ordered_after(body, len(args), in_specs, args, into)
        aliases = {len(args) - 1: 0}
    body, in_specs, args = _ordered_after(body, len(args), in_specs, args, after)
    return pl.pallas_call(
        body, name=name, grid=(ni, N // bn, nk),
        in_specs=in_specs, out_specs=o_spec,
        out_shape=jax.ShapeDtypeStruct((M, N), out_dtype),
        input_output_aliases=aliases,
        scratch_shapes=[pltpu.VMEM((bm, bn), F32)] if nk > 1 else [],
        compiler_params=_cparams("parallel", "parallel", "arbitrary"),
    )(*args)


def _blocks_per_tile(c):
    nb = 1
    while (nb * c) % LANES or (nb * c < 1024 and nb < N_DEV):
        nb *= 2
    assert nb <= N_DEV and (nb * c) % LANES == 0, c
    return nb


def _mm_w8(a, w8, *, name, bm=1024):
    M, K = a.shape
    _, _, c = w8.shape
    nb = _blocks_per_tile(c)
    bm = _div(M, bm)

    def body(a_ref, w_ref, o_ref):
        a_ = a_ref[...]
        for t in range(nb):
            o_ref[:, t * c:(t + 1) * c] = jnp.dot(a_, w_ref[t], preferred_element_type=F32)

    return pl.pallas_call(
        body, name=name, grid=(M // bm, N_DEV // nb),
        in_specs=[pl.BlockSpec((bm, K), lambda i, j: (i, 0)), pl.BlockSpec((nb, K, c), lambda i, j: (j, 0, 0))],
        out_specs=pl.BlockSpec((bm, nb * c), lambda i, j: (i, j)),
        out_shape=jax.ShapeDtypeStruct((M, N_DEV * c), F32),
        compiler_params=_cparams("parallel", "parallel"),
    )(a, w8)


def _mm_w8t(dy, w8, *, name, add=None, out_dtype=F32, bm=1024, bn=1024, after=None, lead=None):
    M = dy.shape[-2]
    _, K, c = w8.shape
    nb = _blocks_per_tile(c)
    nk = N_DEV // nb
    bm, bn = _div(M, bm), _div(K, bn)
    has_add = add is not None
    dims = (((1,), (1,)), ((), ()))

    def body(*refs):
        if has_add:
            dy_ref, w_ref, add_ref, o_ref, acc = refs
        else:
            dy_ref, w_ref, o_ref, acc = refs
        p = lax.dot_general(dy_ref[:, 0:c], w_ref[0], dims, preferred_element_type=F32)
        for t in range(1, nb):
            p = p + lax.dot_general(dy_ref[:, t * c:(t + 1) * c], w_ref[t], dims, preferred_element_type=F32)
        k = pl.program_id(2)

        @pl.when(k == 0)
        def _():
            acc[...] = p

        @pl.when(k > 0)
        def _():
            acc[...] += p

        @pl.when(k == nk - 1)
        def _():
            r = acc[...]
            if has_add:
                r = r + add_ref[...]
            o_ref[...] = r.astype(out_dtype)

    o_spec = pl.BlockSpec((bm, bn), lambda i, j, k: (i, j))
    dy_spec = (pl.BlockSpec((bm, nb * c), lambda i, j, k: (i, k)) if lead is None
               else pl.BlockSpec((None, bm, nb * c), lambda i, j, k: (lead, i, k)))
    in_specs = [dy_spec, pl.BlockSpec((nb, bn, c), lambda i, j, k: (k, j, 0))]
    in_specs += [o_spec] if has_add else []
    args = (dy, w8) + ((add,) if has_add else ())
    body, in_specs, args = _ordered_after(body, len(args), in_specs, args, after)
    return pl.pallas_call(
        body, name=name, grid=(M // bm, K // bn, nk),
        in_specs=in_specs, out_specs=o_spec,
        out_shape=jax.ShapeDtypeStruct((M, K), out_dtype),
        scratch_shapes=[pltpu.VMEM((bm, bn), F32)],
        compiler_params=_cparams("parallel", "parallel", "arbitrary"),
    )(*args)


def _mm_gw8(x, dy, c, *, name, bk=1024, lead=None):
    T, K = x.shape
    nb = _blocks_per_tile(c)
    bk = _div(K, bk)
    dims = (((0,), (0,)), ((), ()))

    def body(x_ref, dy_ref, o_ref):
        x_ = x_ref[...]
        for t in range(nb):
            o_ref[t] = lax.dot_general(x_, dy_ref[:, t * c:(t + 1) * c], dims, preferred_element_type=F32).astype(BF16)

    dy_spec = (pl.BlockSpec((T, nb * c), lambda i, j: (0, j)) if lead is None
               else pl.BlockSpec((None, T, nb * c), lambda i, j: (lead, 0, j)))
    return pl.pallas_call(
        body, name=name, grid=(K // bk, N_DEV // nb),
        in_specs=[pl.BlockSpec((T, bk), lambda i, j: (0, i)), dy_spec],
        out_specs=pl.BlockSpec((nb, bk, c), lambda i, j: (j, i, 0)),
        out_shape=jax.ShapeDtypeStruct((N_DEV, K, c), BF16),
        compiler_params=_cparams("parallel", "parallel"),
    )(x, dy)


def _rms_fwd(x, g, *, name):
    T, D = x.shape
    tm = _div(T, 256, 8)

    def body(x_ref, g_ref, h_ref):
        xf = x_ref[...]
        r = lax.rsqrt(jnp.mean(xf * xf, axis=-1, keepdims=True) + EPS)
        h_ref[...] = ((xf * r) * g_ref[...]).astype(BF16)

    return pl.pallas_call(
        body, name=name, grid=(T // tm,),
        in_specs=[pl.BlockSpec((tm, D), lambda i: (i, 0)), pl.BlockSpec((1, D), lambda i: (0, 0))],
        out_specs=pl.BlockSpec((tm, D), lambda i: (i, 0)),
        out_shape=jax.ShapeDtypeStruct((T, D), BF16),
        compiler_params=_cparams("parallel"),
    )(x, g)


def _rms_bwd(x, g, dh, dres, *, name, want_bf16, after=None):
    T, D = x.shape
    tm = _div(T, 256, 8)

    def body(x_ref, g_ref, dh_ref, dres_ref, dx_ref, *rest):
        if want_bf16:
            dxb_ref, dg_ref = rest
        else:
            (dg_ref,) = rest
        xf = x_ref[...]
        r = lax.rsqrt(jnp.mean(xf * xf, axis=-1, keepdims=True) + EPS)
        xhat = xf * r
        dh_ = dh_ref[...]
        dy = dh_ * g_ref[...]
        dx = dres_ref[...] + r * (dy - xhat * jnp.mean(dy * xhat, axis=-1, keepdims=True))
        dx_ref[...] = dx
        if want_bf16:
            dxb_ref[...] = dx.astype(BF16)
        part = jnp.sum(dh_ * xhat, axis=0, keepdims=True)

        @pl.when(pl.program_id(0) == 0)
        def _():
            dg_ref[...] = part

        @pl.when(pl.program_id(0) > 0)
        def _():
            dg_ref[...] += part

    row = pl.BlockSpec((tm, D), lambda i: (i, 0))
    vec = pl.BlockSpec((1, D), lambda i: (0, 0))
    out_specs = [row] + ([row] if want_bf16 else []) + [vec]
    out_shape = ([jax.ShapeDtypeStruct((T, D), F32)] + ([jax.ShapeDtypeStruct((T, D), BF16)] if want_bf16 else [])
                 + [jax.ShapeDtypeStruct((1, D), F32)])
    body, in_specs, args = _ordered_after(body, 4, [row, vec, row, row], (x, g, dh, dres), after)
    return pl.pallas_call(
        body, name=name, grid=(T // tm,),
        in_specs=in_specs, out_specs=out_specs, out_shape=out_shape,
        compiler_params=_cparams("arbitrary"),
    )(*args)


def _loss_head(x, g, target, *, name):
    T, D = x.shape
    tm = _div(T, 256, 8)

    def body(x_ref, g_ref, t_ref, loss_ref, dx_ref, dxb_ref, dg_ref):
        xf = x_ref[...]
        r = lax.rsqrt(jnp.mean(xf * xf, axis=-1, keepdims=True) + EPS)
        xhat = xf * r
        gain = g_ref[...]
        err = xhat * gain - t_ref[...]
        lpart = 0.5 * jnp.sum(jnp.mean(err * err, axis=-1, keepdims=True), axis=0, keepdims=True)
        dh_ = err * (1.0 / D)
        dy = dh_ * gain
        dx = r * (dy - xhat * jnp.mean(dy * xhat, axis=-1, keepdims=True))
        dx_ref[...] = dx
        dxb_ref[...] = dx.astype(BF16)
        part = jnp.sum(dh_ * xhat, axis=0, keepdims=True)

        @pl.when(pl.program_id(0) == 0)
        def _():
            dg_ref[...] = part
            loss_ref[...] = jnp.broadcast_to(lpart, loss_ref.shape)

        @pl.when(pl.program_id(0) > 0)
        def _():
            dg_ref[...] += part
            loss_ref[...] += jnp.broadcast_to(lpart, loss_ref.shape)

    row = pl.BlockSpec((tm, D), lambda i: (i, 0))
    vec = pl.BlockSpec((1, D), lambda i: (0, 0))
    return pl.pallas_call(
        body, name=name, grid=(T // tm,),
        in_specs=[row, vec, row],
        out_specs=[pl.BlockSpec((8, LANES), lambda i: (0, 0)), row, row, vec],
        out_shape=[jax.ShapeDtypeStruct((8, LANES), F32), jax.ShapeDtypeStruct((T, D), F32),
                   jax.ShapeDtypeStruct((T, D), BF16), jax.ShapeDtypeStruct((1, D), F32)],
        compiler_params=_cparams("arbitrary"),
    )(x, g, target)


def _gate_cols(D):
    off_a = 3 * D // 2 + 2 * KV_WIDTH
    off_b = off_a + D
    cw = math.gcd(math.gcd(off_a, off_b), math.gcd(D, 512))
    return cw, off_a // cw, off_b // cw


def _merge_fwd(z, ya, yb, *, name):
    T, D = ya.shape
    cw, ba, bb = _gate_cols(D)
    tm = _div(T, 512, 8)

    def body(ga_ref, gb_ref, ya_ref, yb_ref, m_ref):
        m_ref[...] = (_sigmoid(ga_ref[...]) * ya_ref[...] + _sigmoid(gb_ref[...]) * yb_ref[...]).astype(BF16)

    blk = pl.BlockSpec((tm, cw), lambda i, j: (i, j))
    return pl.pallas_call(
        body, name=name, grid=(T // tm, D // cw),
        in_specs=[pl.BlockSpec((tm, cw), lambda i, j: (i, ba + j)), pl.BlockSpec((tm, cw), lambda i, j: (i, bb + j)), blk, blk],
        out_specs=blk, out_shape=jax.ShapeDtypeStruct((T, D), BF16),
        compiler_params=_cparams("parallel", "parallel"),
    )(z, z, ya, yb)


def _merge_bwd(z, ya, yb, dm, *, name, after=None):
    T, D = ya.shape
    cw, ba, bb = _gate_cols(D)
    nj = D // cw
    assert bb == ba + nj
    tm = _div(T, 512, 8)

    def body(g_ref, ya_ref, yb_ref, dm_ref, dy_ref, dz_ref):
        sig = _sigmoid(g_ref[...])
        dm_ = dm_ref[...]
        y = jnp.where(pl.program_id(1) == 0, ya_ref[...], yb_ref[...])
        dy_ref[...] = (dm_ * sig).astype(BF16)
        dz_ref[...] = (dm_ * y * (sig * (1.0 - sig))).astype(BF16)

    in_specs = [pl.BlockSpec((tm, cw), lambda i, s, j: (i, ba + s * nj + j)),
                pl.BlockSpec((tm, cw), lambda i, s, j: (i, j * (1 - s))),
                pl.BlockSpec((tm, cw), lambda i, s, j: (i, j * s)),
                pl.BlockSpec((tm, cw), lambda i, s, j: (i, j))]
    body, in_specs, args = _ordered_after(body, 4, in_specs, (z, ya, yb, dm), after)
    return pl.pallas_call(
        body, name=name, grid=(T // tm, 2, nj),
        in_specs=in_specs,
        out_specs=[pl.BlockSpec((None, tm, cw), lambda i, s, j: (s, i, j)),
                   pl.BlockSpec((tm, cw), lambda i, s, j: (i, ba + s * nj + j))],
        out_shape=[jax.ShapeDtypeStruct((2, T, D), BF16), jax.ShapeDtypeStruct(z.shape, BF16)],
        compiler_params=_cparams("parallel", "arbitrary", "arbitrary"),
    )(*args)


def _swiglu_mm_fwd(h, wg_t, wu_t, *, name, bm=1024, bn=512):
    T, D = h.shape
    F = wg_t.shape[0]
    bm, bn = _div(T, bm), _div(F, bn)
    dims = (((1,), (1,)), ((), ()))

    def body(h_ref, wg_ref, wu_ref, g_ref, u_ref, act_ref):
        h_ = h_ref[...]
        g = lax.dot_general(h_, wg_ref[...], dims, preferred_element_type=F32)
        u = lax.dot_general(h_, wu_ref[...], dims, preferred_element_type=F32)
        g_ref[...] = g
        u_ref[...] = u
        act_ref[...] = (g * _sigmoid(g) * u).astype(BF16)

    w_spec = pl.BlockSpec((bn, D), lambda i, j: (j, 0))
    o_spec = pl.BlockSpec((bm, bn), lambda i, j: (i, j))
    return pl.pallas_call(
        body, name=name, grid=(T // bm, F // bn),
        in_specs=[pl.BlockSpec((bm, D), lambda i, j: (i, 0)), w_spec, w_spec], out_specs=[o_spec] * 3,
        out_shape=[jax.ShapeDtypeStruct((T, F), F32), jax.ShapeDtypeStruct((T, F), F32), jax.ShapeDtypeStruct((T, F), BF16)],
        compiler_params=_cparams("parallel", "parallel"),
    )(h, wg_t, wu_t)


def _swiglu_mm_bwd(dx, w_down, gate, up, *, name, bm=1024, bn=512, after=None):
    T, D = dx.shape
    F = w_down.shape[0]
    bm, bn = _div(T, bm), _div(F, bn)
    dims = (((1,), (1,)), ((), ()))

    def body(dx_ref, w_ref, g_ref, u_ref, dg_ref, du_ref):
        d = lax.dot_general(dx_ref[...], w_ref[...], dims, preferred_element_type=F32)
        g = g_ref[...]
        s = _sigmoid(g)
        silu = g * s
        dg_ref[...] = (d * u_ref[...] * (s + silu * (1.0 - s))).astype(BF16)
        du_ref[...] = (d * silu).astype(BF16)

    o_spec = pl.BlockSpec((bm, bn), lambda i, j: (i, j))
    in_specs = [pl.BlockSpec((bm, D), lambda i, j: (i, 0)), pl.BlockSpec((bn, D), lambda i, j: (j, 0)), o_spec, o_spec]
    body, in_specs, args = _ordered_after(body, 4, in_specs, (dx, w_down, gate, up), after)
    out = jax.ShapeDtypeStruct((T, F), BF16)
    return pl.pallas_call(
        body, name=name, grid=(T // bm, F // bn), in_specs=in_specs, out_specs=[o_spec, o_spec], out_shape=[out, out],
        compiler_params=_cparams("parallel", "parallel"),
    )(*args)


def _sgu_fwd(z, gain, ws_b, bs_t, *, name):
    T = z.shape[0]
    SW = gain.shape[1]
    G = SW // BLOCK

    def body(zu_ref, zv_ref, gain_ref, ws_ref, bs_ref, a_ref):
        u = _gelu(zu_ref[...])
        vg = _gelu(zv_ref[...])
        r = lax.rsqrt(jnp.mean(vg * vg, axis=-1, keepdims=True) + EPS)
        vn = ((vg * r) * gain_ref[...]).astype(BF16)
        for g in range(G):
            sl = slice(g * BLOCK, (g + 1) * BLOCK)
            mixed = jnp.dot(ws_ref[g], vn[:, sl], preferred_element_type=F32) + bs_ref[:, g:g + 1]
            a_ref[:, sl] = (u[:, sl] * mixed).astype(BF16)

    return pl.pallas_call(
        body, name=name, grid=(T // BLOCK,),
        in_specs=[pl.BlockSpec((BLOCK, SW), lambda c: (c, 0)), pl.BlockSpec((BLOCK, SW), lambda c: (c, 1)),
                  pl.BlockSpec((1, SW), lambda c: (0, 0)), pl.BlockSpec((G, BLOCK, BLOCK), lambda c: (0, 0, 0)),
                  pl.BlockSpec((BLOCK, G), lambda c: (0, 0))],
        out_specs=pl.BlockSpec((BLOCK, SW), lambda c: (c, 0)),
        out_shape=jax.ShapeDtypeStruct((T, SW), BF16),
        compiler_params=_cparams("parallel"),
    )(z, z, gain, ws_b, bs_t)


def _sgu_bwd(z, gain, ws_b, bs_t, da, dz, *, name):
    T = z.shape[0]
    SW = gain.shape[1]
    G = SW // BLOCK

    def body(zu_ref, zv_ref, gain_ref, ws_ref, bs_ref, da_ref, dz_in_ref, dz_ref, dws_ref, dbs_ref, dgain_ref, dvn_ref):
        first = pl.program_id(0) == 0

        @pl.when(first)
        def _():
            dws_ref[...] = jnp.zeros_like(dws_ref)
            dbs_ref[...] = jnp.zeros_like(dbs_ref)
            dgain_ref[...] = jnp.zeros_like(dgain_ref)

        u, du = _gelu_and_grad(zu_ref[...])
        vg, dvg = _gelu_and_grad(zv_ref[...])
        r = lax.rsqrt(jnp.mean(vg * vg, axis=-1, keepdims=True) + EPS)
        xhat = vg * r
        gain_ = gain_ref[...]
        vn = (xhat * gain_).astype(BF16)
        da_ = da_ref[...]
        for g in range(G):
            sl = slice(g * BLOCK, (g + 1) * BLOCK)
            w = ws_ref[g]
            mixed = jnp.dot(w, vn[:, sl], preferred_element_type=F32) + bs_ref[:, g:g + 1]
            dmix = da_[:, sl] * u[:, sl]
            dz_ref[:, sl] = (da_[:, sl] * mixed * du[:, sl]).astype(BF16)
            dmb = dmix.astype(BF16)
            dws_ref[g] += lax.dot_general(dmb, vn[:, sl], (((1,), (1,)), ((), ())), preferred_element_type=F32)
            dbs_ref[:, g:g + 1] += jnp.sum(dmix, axis=-1, keepdims=True)
            dvn_ref[:, sl] = lax.dot_general(w, dmb, (((0,), (0,)), ((), ())), preferred_element_type=F32)
        dvn = dvn_ref[...]
        dgain_ref[...] += jnp.sum(dvn * xhat, axis=0, keepdims=True)
        dy = dvn * gain_
        dv_ = r * (dy - xhat * jnp.mean(dy * xhat, axis=-1, keepdims=True))
        dz_ref[:, SW:] = (dv_ * dvg).astype(BF16)

    row = pl.BlockSpec((BLOCK, SW), lambda c: (c, 0))
    return pl.pallas_call(
        body, name=name, grid=(T // BLOCK,),
        in_specs=[row, pl.BlockSpec((BLOCK, SW), lambda c: (c, 1)),
                  pl.BlockSpec((1, SW), lambda c: (0, 0)), pl.BlockSpec((G, BLOCK, BLOCK), lambda c: (0, 0, 0)),
                  pl.BlockSpec((BLOCK, G), lambda c: (0, 0)), row, _ANY],
        out_specs=[pl.BlockSpec((BLOCK, 2 * SW), lambda c: (c, 0)), pl.BlockSpec((G, BLOCK, BLOCK), lambda c: (0, 0, 0)),
                   pl.BlockSpec((BLOCK, G), lambda c: (0, 0)), pl.BlockSpec((1, SW), lambda c: (0, 0))],
        out_shape=[jax.ShapeDtypeStruct(dz.shape, dz.dtype),
                   jax.ShapeDtypeStruct((G, BLOCK, BLOCK), F32), jax.ShapeDtypeStruct((BLOCK, G), F32),
                   jax.ShapeDtypeStruct((1, SW), F32)],
        input_output_aliases={6: 0},
        scratch_shapes=[pltpu.VMEM((BLOCK, SW), F32)],
        compiler_params=_cparams("arbitrary"),
    )(z, z, gain, ws_b, bs_t, da, dz)


def _bias_table(rel_bias, bmap, *, name):
    H = rel_bias.shape[1]

    def body(rb_ref, bmap_ref, o_ref):
        bm_ = bmap_ref[...]
        for h in range(H):
            acc = jnp.zeros(bm_.shape, F32)
            for b in range(REL_BUCKETS):
                acc = jnp.where(bm_ == b, rb_ref[b, h], acc)
            o_ref[h] = acc

    return pl.pallas_call(
        body, name=name,
        in_specs=[pl.BlockSpec(memory_space=pltpu.SMEM), pl.BlockSpec(memory_space=pltpu.VMEM)],
        out_specs=pl.BlockSpec(memory_space=pltpu.VMEM),
        out_shape=jax.ShapeDtypeStruct((H, BLOCK, 3 * BLOCK), F32),
    )(rel_bias, bmap)


def _attn_probs(q_ref, kb, bias_ref, sink_ref, s_ref, n, T, group):
    H = s_ref.shape[0]
    for h in range(H):
        kv = h // group
        qh = q_ref[:, h * HEAD_DIM:(h + 1) * HEAD_DIM].astype(BF16)
        s_ref[h] = lax.dot_general(qh, kb[:, kv * HEAD_DIM:(kv + 1) * HEAD_DIM], (((1,), (1,)), ((), ())),
                                   preferred_element_type=F32)
    row = lax.broadcasted_iota(jnp.int32, (BLOCK, 3 * BLOCK), 0)
    col = lax.broadcasted_iota(jnp.int32, (BLOCK, 3 * BLOCK), 1)
    key_pos = n * BLOCK + col - BLOCK
    valid = (jnp.abs(col - BLOCK - row) <= BLOCK) & (key_pos >= 0) & (key_pos < T)
    s = s_ref[...] * (HEAD_DIM ** -0.5) + bias_ref[...]
    s = jnp.where(valid[None], s, NEG)
    sink = sink_ref[...]
    m = jnp.maximum(jnp.max(s, axis=-1, keepdims=True), sink)
    e = jnp.exp(s - m)
    es = jnp.exp(sink - m)
    inv = 1.0 / (jnp.sum(e, axis=-1, keepdims=True) + es)
    return e * inv, es * inv


def _attn_fwd(z, kpad, vpad, bias, sink, *, name):
    T = z.shape[0]
    H = bias.shape[0]
    AW = H * HEAD_DIM
    group = H // N_KV_HEADS

    def body(q_ref, k_ref, v_ref, bias_ref, sink_ref, o_ref, s_ref, p_ref):
        n = pl.program_id(0)
        start = pl.multiple_of(n * BLOCK, BLOCK)
        kb = k_ref[pl.ds(start, 3 * BLOCK), :]
        vb = v_ref[pl.ds(start, 3 * BLOCK), :]
        p, _ = _attn_probs(q_ref, kb, bias_ref, sink_ref, s_ref, n, T, group)
        p_ref[...] = p.astype(BF16)
        for h in range(H):
            kv = h // group
            o = jnp.dot(p_ref[h], vb[:, kv * HEAD_DIM:(kv + 1) * HEAD_DIM], preferred_element_type=F32)
            o_ref[:, h * HEAD_DIM:(h + 1) * HEAD_DIM] = o.astype(BF16)

    full_kv = pl.BlockSpec((T + 2 * BLOCK, KV_WIDTH), lambda n: (0, 0))
    return pl.pallas_call(
        body, name=name, grid=(T // BLOCK,),
        in_specs=[pl.BlockSpec((BLOCK, AW), lambda n: (n, 2)), full_kv, full_kv,
                  pl.BlockSpec((H, BLOCK, 3 * BLOCK), lambda n: (0, 0, 0)), pl.BlockSpec((H, 1, 1), lambda n: (0, 0, 0))],
        out_specs=pl.BlockSpec((BLOCK, AW), lambda n: (n, 0)),
        out_shape=jax.ShapeDtypeStruct((T, AW), BF16),
        scratch_shapes=[pltpu.VMEM((H, BLOCK, 3 * BLOCK), F32), pltpu.VMEM((H, BLOCK, 3 * BLOCK), BF16)],
        compiler_params=_cparams("parallel"),
    )(z, kpad, vpad, bias, sink)


def _attn_bwd(z, kpad, vpad, bias, sink, do, dz, *, name):
    T = z.shape[0]
    H = bias.shape[0]
    AW = H * HEAD_DIM
    group = H // N_KV_HEADS
    scale = HEAD_DIM ** -0.5

    def body(q_ref, k_ref, v_ref, bias_ref, sink_ref, do_ref, dz_in_ref, dq_ref, dk_ref, dv_ref, dbias_ref, dsink_ref,
             s_ref, dp_ref, p_ref, ds_ref):
        n = pl.program_id(0)

        @pl.when(n == 0)
        def _():
            dk_ref[...] = jnp.zeros_like(dk_ref)
            dv_ref[...] = jnp.zeros_like(dv_ref)
            dbias_ref[...] = jnp.zeros_like(dbias_ref)
            dsink_ref[...] = jnp.zeros_like(dsink_ref)

        start = pl.multiple_of(n * BLOCK, BLOCK)
        kb = k_ref[pl.ds(start, 3 * BLOCK), :]
        vb = v_ref[pl.ds(start, 3 * BLOCK), :]
        p, p_sink = _attn_probs(q_ref, kb, bias_ref, sink_ref, s_ref, n, T, group)
        s_ref[...] = p
        p_ref[...] = p.astype(BF16)
        for h in range(H):
            kv = h // group
            dp_ref[h] = lax.dot_general(do_ref[:, h * HEAD_DIM:(h + 1) * HEAD_DIM], vb[:, kv * HEAD_DIM:(kv + 1) * HEAD_DIM],
                                        (((1,), (1,)), ((), ())), preferred_element_type=F32)
        p = s_ref[...]
        dp = dp_ref[...]
        delta = jnp.sum(p * dp, axis=-1, keepdims=True)
        ds = p * (dp - delta)
        dbias_ref[...] += ds
        dsink_ref[...] += -(p_sink * delta)
        ds_ref[...] = ds.astype(BF16)
        for kv in range(N_KV_HEADS):
            ksl = slice(kv * HEAD_DIM, (kv + 1) * HEAD_DIM)
            dk_acc = jnp.zeros((3 * BLOCK, HEAD_DIM), F32)
            dv_acc = jnp.zeros((3 * BLOCK, HEAD_DIM), F32)
            for gi in range(group):
                h = kv * group + gi
                hsl = slice(h * HEAD_DIM, (h + 1) * HEAD_DIM)
                dsb = ds_ref[h]
                dq = jnp.dot(dsb, kb[:, ksl], preferred_element_type=F32) * scale
                dq_ref[:, hsl] = dq.astype(BF16)
                dk_acc = dk_acc + lax.dot_general(dsb, q_ref[:, hsl].astype(BF16), (((0,), (0,)), ((), ())),
                                                  preferred_element_type=F32)
                dv_acc = dv_acc + lax.dot_general(p_ref[h], do_ref[:, hsl], (((0,), (0,)), ((), ())),
                                                  preferred_element_type=F32)
            dk_ref[pl.ds(start, 3 * BLOCK), ksl] += dk_acc * scale
            dv_ref[pl.ds(start, 3 * BLOCK), ksl] += dv_acc

    full_kv = pl.BlockSpec((T + 2 * BLOCK, KV_WIDTH), lambda n: (0, 0))
    bias_spec = pl.BlockSpec((H, BLOCK, 3 * BLOCK), lambda n: (0, 0, 0))
    row = pl.BlockSpec((BLOCK, AW), lambda n: (n, 0))
    q_cols = pl.BlockSpec((BLOCK, AW), lambda n: (n, 2))
    band = (H, BLOCK, 3 * BLOCK)
    return pl.pallas_call(
        body, name=name, grid=(T // BLOCK,),
        in_specs=[q_cols, full_kv, full_kv, bias_spec, pl.BlockSpec((H, 1, 1), lambda n: (0, 0, 0)), row, _ANY],
        out_specs=[q_cols, full_kv, full_kv, bias_spec, pl.BlockSpec((H, BLOCK, 1), lambda n: (0, 0, 0))],
        out_shape=[jax.ShapeDtypeStruct(dz.shape, dz.dtype),
                   jax.ShapeDtypeStruct((T + 2 * BLOCK, KV_WIDTH), F32), jax.ShapeDtypeStruct((T + 2 * BLOCK, KV_WIDTH), F32),
                   jax.ShapeDtypeStruct(band, F32), jax.ShapeDtypeStruct((H, BLOCK, 1), F32)],
        input_output_aliases={6: 0},
        scratch_shapes=[pltpu.VMEM(band, F32), pltpu.VMEM(band, F32), pltpu.VMEM(band, BF16), pltpu.VMEM(band, BF16)],
        compiler_params=_cparams("arbitrary"),
    )(z, kpad, vpad, bias, sink, do, dz)


def _dkv_into(dkp, dvp, dz, *, name):
    T = dz.shape[0]
    D = (dz.shape[1] - 2 * KV_WIDTH) * 2 // 7
    col = (D + D // 2) // (2 * KV_WIDTH)
    assert col * 2 * KV_WIDTH == D + D // 2

    def body(dk_ref, dv_ref, dz_in_ref, o_ref):
        o_ref[:, :KV_WIDTH] = dk_ref[...].astype(BF16)
        o_ref[:, KV_WIDTH:] = dv_ref[...].astype(BF16)

    kv = pl.BlockSpec((BLOCK, KV_WIDTH), lambda n: (n + 1, 0))
    return pl.pallas_call(
        body, name=name, grid=(T // BLOCK,),
        in_specs=[kv, kv, _ANY], out_specs=pl.BlockSpec((BLOCK, 2 * KV_WIDTH), lambda n: (n, col)),
        out_shape=jax.ShapeDtypeStruct(dz.shape, dz.dtype), input_output_aliases={2: 0},
        compiler_params=_cparams("parallel"),
    )(dkp, dvp, dz)


def _kv_pad(z, *, name):
    T = z.shape[0]
    D = (z.shape[1] - 2 * KV_WIDTH) * 2 // 7
    kcol = (D + D // 2) // KV_WIDTH
    nb = T // BLOCK

    def body(k_ref, v_ref, ko_ref, vo_ref):
        b = pl.program_id(0)
        inside = (b >= 1) & (b <= nb)
        ko_ref[...] = jnp.where(inside, k_ref[...], 0.0).astype(BF16)
        vo_ref[...] = jnp.where(inside, v_ref[...], 0.0).astype(BF16)

    out = jax.ShapeDtypeStruct((T + 2 * BLOCK, KV_WIDTH), BF16)
    o_spec = pl.BlockSpec((BLOCK, KV_WIDTH), lambda b: (b, 0))
    return pl.pallas_call(
        body, name=name, grid=(nb + 2,),
        in_specs=[pl.BlockSpec((BLOCK, KV_WIDTH), lambda b: (jnp.clip(b - 1, 0, nb - 1), kcol)),
                  pl.BlockSpec((BLOCK, KV_WIDTH), lambda b: (jnp.clip(b - 1, 0, nb - 1), kcol + 1))],
        out_specs=[o_spec, o_spec], out_shape=[out, out],
        compiler_params=_cparams("parallel"),
    )(z, z)


def _attn_small_grads(dbias, dsink_rows, bmap, after, *, name):
    H = dbias.shape[0]

    def body(dbias_ref, dsink_ref, bmap_ref, drel_ref, ds_ref):
        bm_ = bmap_ref[...]
        for h in range(H):
            d = dbias_ref[h]
            for b in range(REL_BUCKETS):
                drel_ref[b, h] = jnp.sum(jnp.where(bm_ == b, d, 0.0))
            ds_ref[0, h] = jnp.sum(dsink_ref[h])

    vmem = pl.BlockSpec(memory_space=pltpu.VMEM)
    smem = pl.BlockSpec(memory_space=pltpu.SMEM)
    body, in_specs, args = _ordered_after(body, 3, [vmem, vmem, vmem], (dbias, dsink_rows, bmap), after)
    return pl.pallas_call(
        body, name=name, in_specs=in_specs, out_specs=[smem, smem],
        out_shape=[jax.ShapeDtypeStruct((REL_BUCKETS, H), F32), jax.ShapeDtypeStruct((1, H), F32)],
    )(*args)


def _local_step(x, target, weight, emit, flush, norm_mix, v_gain, w_s, b_s, sink, rel_bias, norm_ffn, norm_final):
    T, D = x.shape
    ws_b = w_s.astype(BF16)
    bs_t = b_s.T
    bmap = jnp.asarray(_bucket_map())
    sink = sink.reshape(-1, 1, 1)

    h = _rms_fwd(x, norm_mix, name="rms_mix")
    w_in = weight("w_in", h)
    z = _mm(h, w_in, tb=True, name="mm_z", bm=2048, bn=768)
    a = _sgu_fwd(z, v_gain, ws_b, bs_t, name="sgu_fwd")
    w_a = weight("w_a_out", a)
    ya = _mm_w8(a, w_a, name="mm_ya", bm=2048)
    kpad, vpad = _kv_pad(z, name="kv_pad")
    bias = _bias_table(rel_bias, bmap, name="bias_table")
    o = _attn_fwd(z, kpad, vpad, bias, sink, name="attn_fwd")
    w_b = weight("w_b_out", o)
    yb = _mm_w8(o, w_b, name="mm_yb", bm=2048)
    m = _merge_fwd(z, ya, yb, name="merge_fwd")
    w_o = weight("w_o", m)
    x1 = _mm(m, w_o, name="mm_x1", add=x, bm=2048, bn=512)
    h2 = _rms_fwd(x1, norm_ffn, name="rms_ffn")
    w_gate = weight("w_gate", h2)
    w_up = weight("w_up", h2)
    gate, up, act = _swiglu_mm_fwd(h2, w_gate, w_up, name="mm_gate_up")
    w_down = weight("w_down", act)
    x2 = _mm(act, w_down, name="mm_x2", add=x1, bm=1024, bn=1024, bk=2816)
    loss, dx2, dx2b, g_norm_final = _loss_head(x2, norm_final, target, name="loss_head")

    g_w_down = _mm(act, dx2b, ta=True, out_dtype=BF16, name="mm_gwdown", bm=512, bn=2048)
    tok = emit(("w_down",), (g_w_down,))
    dgate, dup = _swiglu_mm_bwd(dx2b, w_down, gate, up, name="mm_dact_swiglu", after=tok)
    tok = flush(dgate)
    g_w_gate = _mm(dgate, h2, ta=True, out_dtype=BF16, name="mm_gwgate", bm=512, bn=2048, after=tok)
    g_w_up = _mm(dup, h2, ta=True, out_dtype=BF16, name="mm_gwup", bm=512, bn=2048)
    tok = emit(("w_gate", "w_up"), (g_w_gate, g_w_up))
    dh2 = _mm(dgate, w_gate, name="mm_dh2a", bm=1024, bn=1024, bk=2816, after=tok)
    tok = flush(dh2)
    dh2 = _mm(dup, w_up, add=dh2, name="mm_dh2b", bm=1024, bn=1024, bk=2816, after=tok)
    dx1, dx1b, g_norm_ffn = _rms_bwd(x1, norm_ffn, dh2, dx2, name="rms_ffn_bwd", want_bf16=True)

    g_w_o = _mm(m, dx1b, ta=True, out_dtype=BF16, name="mm_gwo", bm=2048, bn=512)
    tok = emit(("w_o",), (g_w_o,))
    dm = _mm(dx1b, w_o, tb=True, name="mm_dm", bm=2048, bn=512, after=tok)
    tok = flush(dm)
    dy, dz = _merge_bwd(z, ya, yb, dm, name="merge_bwd", after=tok)
    g_w_a = _mm_gw8(a, dy, w_a.shape[2], name="mm_gwa", lead=0)
    g_w_b = _mm_gw8(o, dy, w_b.shape[2], name="mm_gwb", lead=1)
    tok = emit(("w_a_out", "w_b_out"), (g_w_a, g_w_b))
    da = _mm_w8t(dy, w_a, name="mm_da", bm=2048, bn=512, after=tok, lead=0)
    tok = flush(da)
    do = _mm_w8t(dy, w_b, out_dtype=BF16, name="mm_do", bm=2048, bn=512, after=tok, lead=1)
    dz, g_w_s, g_b_s_t, g_v_gain = _sgu_bwd(z, v_gain, ws_b, bs_t, da, dz, name="sgu_bwd")
    dz, dkp, dvp, dbias, dsink_rows = _attn_bwd(z, kpad, vpad, bias, sink, do, dz, name="attn_bwd")
    dz = _dkv_into(dkp, dvp, dz, name="dkv_into_dz")
    g_w_in = _mm(dz, h, ta=True, out_dtype=BF16, name="mm_gwin", bm=768, bn=2048)
    tok = emit(("w_in",), (g_w_in,))
    half = dict(bm=T // 2, bn=1024, bk=2560)
    dh = _mm(dz, w_in, name="mm_dh_top", row_blocks=(0, 1), after=tok, **half)
    tok = flush(dh)
    dh = _mm(dz, w_in, name="mm_dh_bottom", row_blocks=(1, 1), into=dh, after=tok, **half)
    g_rel_bias, g_sink = _attn_small_grads(dbias, dsink_rows, bmap, dh, name="attn_small_grads")
    grad_x, g_norm_mix = _rms_bwd(x, norm_mix, dh, dx1, name="rms_mix_bwd", want_bf16=False)

    small = dict(norm_mix=g_norm_mix, sgu_v_gain=g_v_gain, sgu_w_s=g_w_s, sgu_b_s=g_b_s_t.T, attn_sink=g_sink,
                 rel_bias=g_rel_bias, norm_ffn=g_norm_ffn, norm_final=g_norm_final)
    return loss, grad_x, small


def _position():
    return lax.axis_index("x"), lax.axis_index("y"), lax.axis_index("c")


def _other_chips(x, y):
    return [(1 - x, y), (x, 1 - y), (1 - x, 1 - y)]


def _slot(px, py, pc):
    return 4 * px + 2 * py + pc


_HBM = pl.BlockSpec(memory_space=pltpu.HBM)
_SEM = pl.BlockSpec(memory_space=pltpu.SEMAPHORE)
_DATAFLOW = pltpu.SideEffectType.DATAFLOW_SIDE_EFFECTING


def _in_hbm(a):
    return pltpu.with_memory_space_constraint(a, pltpu.HBM)


def _own_slot(shard, pos, *, name, after=None):
    R, C = shard.shape
    tr = _div(R, 256, 16)

    def body(pos_ref, w_ref, o_ref):
        o_ref[...] = w_ref[...].astype(BF16)

    body, in_specs, args = _ordered_after(body, 2, [pl.BlockSpec((tr, C), lambda i, pos_ref: (i, 0))], (pos, shard), after)
    grid_spec = pltpu.PrefetchScalarGridSpec(
        num_scalar_prefetch=1, grid=(R // tr,), in_specs=in_specs,
        out_specs=pl.BlockSpec((None, tr, C), lambda i, pos_ref: (pos_ref[0], i, 0)))
    return pl.pallas_call(
        body, name=name, grid_spec=grid_spec,
        out_shape=jax.ShapeDtypeStruct((N_DEV, R, C), BF16),
        compiler_params=_cparams("parallel"),
    )(*args)


def _ag_copies(w, land_ref, send_sems, recv_sems):
    x, y, c = _position()
    mine = land_ref.at[_slot(x, y, c)]
    targets = [(px, py, c) for px, py in _other_chips(x, y)] + [(x, y, 1 - c)]
    return [pltpu.make_async_remote_copy(src_ref=mine, dst_ref=mine, send_sem=send_sems.at[4 * w + k],
                                         recv_sem=recv_sems.at[4 * w + k], device_id=to, device_id_type=MESH)
            for k, to in enumerate(targets)]


def _ag_start(buffers, groups, *, name):
    lands = [buffers[i] for g in groups for i in g]
    n, ng = len(lands), len(groups)
    sizes = [len(g) for g in groups]

    def body(*refs):
        land_refs = refs[:n]
        sems = refs[n:n + 2 * ng]
        token = refs[-1]
        i = 0
        for g in range(ng):
            for w in range(sizes[g]):
                for cp in _ag_copies(w, land_refs[i], sems[2 * g], sems[2 * g + 1]):
                    cp.start()
                i += 1
        token[...] = jnp.zeros_like(token)

    sem_shapes = [pltpu.SemaphoreType.DMA((4 * k,)) for k in sizes for _ in range(2)]
    outs = pl.pallas_call(
        body, name=name,
        in_specs=[_HBM] * n,
        out_specs=tuple([_SEM] * (2 * ng) + [_HBM] * n + [pl.BlockSpec(memory_space=pltpu.VMEM)]),
        out_shape=tuple(sem_shapes + [pltpu.HBM(a.shape, a.dtype) for a in lands] + [jax.ShapeDtypeStruct((8, LANES), F32)]),
        input_output_aliases={i: 2 * ng + i for i in range(n)},
        compiler_params=pltpu.CompilerParams(has_side_effects=_DATAFLOW),
    )(*[_in_hbm(a) for a in lands])
    sems, thru = outs[:2 * ng], outs[2 * ng:2 * ng + n]
    result, i = [], 0
    for g in range(ng):
        k = sizes[g]
        result.append((sems[2 * g], sems[2 * g + 1], list(thru[i:i + k])))
        i += k
    return result, outs[-1]


def _ag_wait(send_sems, recv_sems, lands, after, *, name):
    n = len(lands)

    def body(*refs):
        land_refs = refs[:n]
        send_ref, recv_ref = refs[n], refs[n + 1]
        token = refs[-1]
        for w in range(n):
            for cp in _ag_copies(w, land_refs[w], send_ref, recv_ref):
                cp.wait_send()
                cp.wait_recv()
        token[...] = jnp.zeros_like(token)

    outs = pl.pallas_call(
        body, name=name,
        in_specs=[_HBM] * n + [_SEM, _SEM, _ANY],
        out_specs=tuple([_HBM] * n + [pl.BlockSpec(memory_space=pltpu.VMEM)]),
        out_shape=tuple([pltpu.HBM(a.shape, a.dtype) for a in lands] + [jax.ShapeDtypeStruct((8, LANES), F32)]),
        input_output_aliases={i: i for i in range(n)},
        compiler_params=pltpu.CompilerParams(has_side_effects=_DATAFLOW),
    )(*lands, send_sems, recv_sems, after)
    return list(outs[:n]), outs[n]


def _ag_forward(lands, *, name, after=None):
    n = len(lands)

    def body(*refs):
        in_refs, out_refs = refs[:n], refs[n:2 * n]
        send_sems, recv_sems = refs[2 * n:]
        x, y, c = _position()
        copies = []
        for w in range(n):
            for k, (px, py) in enumerate(_other_chips(x, y)):
                cp = pltpu.make_async_remote_copy(
                    src_ref=in_refs[w].at[_slot(px, py, c)], dst_ref=out_refs[w].at[_slot(px, py, c)],
                    send_sem=send_sems.at[3 * w + k], recv_sem=recv_sems.at[3 * w + k],
                    device_id=(x, y, 1 - c), device_id_type=MESH)
                cp.start()
                copies.append(cp)
        for cp in copies:
            cp.wait()

    body, in_specs, args = _ordered_after(body, n, [_ANY] * n, tuple(lands), after)
    return pl.pallas_call(
        body, name=name,
        in_specs=in_specs, out_specs=[_ANY] * n,
        out_shape=[jax.ShapeDtypeStruct(a.shape, a.dtype) for a in lands],
        input_output_aliases={i: i for i in range(n)},
        scratch_shapes=[pltpu.SemaphoreType.DMA((3 * n,)), pltpu.SemaphoreType.DMA((3 * n,))],
    )(*args)


def _sibling_copies(w, g8_ref, land_ref, send_sems, recv_sems):
    x, y, c = _position()
    return [pltpu.make_async_remote_copy(src_ref=g8_ref.at[2 * p + (1 - c)], dst_ref=land_ref.at[p],
                                         send_sem=send_sems.at[4 * w + p], recv_sem=recv_sems.at[4 * w + p],
                                         device_id=(x, y, 1 - c), device_id_type=MESH)
            for p in range(4)]


def _chip_copies(w, sums_ref, land_ref, send_sems, recv_sems):
    x, y, c = _position()
    return [pltpu.make_async_remote_copy(src_ref=sums_ref.at[2 * px + py], dst_ref=land_ref.at[k],
                                         send_sem=send_sems.at[3 * w + k], recv_sem=recv_sems.at[3 * w + k],
                                         device_id=(px, py, c), device_id_type=MESH)
            for k, (px, py) in enumerate(_other_chips(x, y))]


def _copies_start(copies, per_weight, srcs, *, name):
    n = len(srcs)
    lands = [lax.empty((per_weight,) + s.shape[1:], s.dtype) for s in srcs]

    def body(*refs):
        src_refs, land_refs = refs[:n], refs[n:2 * n]
        send_sems, recv_sems = refs[2 * n], refs[2 * n + 1]
        token = refs[-1]
        for w in range(n):
            for cp in copies(w, src_refs[w], land_refs[w], send_sems, recv_sems):
                cp.start()
        token[...] = jnp.zeros_like(token)

    outs = pl.pallas_call(
        body, name=name,
        in_specs=[_HBM] * (2 * n),
        out_specs=tuple([_SEM, _SEM] + [_HBM] * (2 * n) + [pl.BlockSpec(memory_space=pltpu.VMEM)]),
        out_shape=tuple([pltpu.SemaphoreType.DMA((per_weight * n,)), pltpu.SemaphoreType.DMA((per_weight * n,))]
                        + [pltpu.HBM(a.shape, a.dtype) for a in srcs + lands] + [jax.ShapeDtypeStruct((8, LANES), F32)]),
        input_output_aliases={i: 2 + i for i in range(2 * n)},
        compiler_params=pltpu.CompilerParams(has_side_effects=_DATAFLOW),
    )(*[_in_hbm(a) for a in srcs + lands])
    return outs[0], outs[1], list(outs[2:2 + n]), list(outs[2 + n:2 + 2 * n]), outs[-1]


def _copies_wait(copies, send_sems, recv_sems, srcs, lands, after, *, name):
    n = len(srcs)

    def body(*refs):
        src_refs, land_refs = refs[:n], refs[n:2 * n]
        send_ref, recv_ref = refs[2 * n], refs[2 * n + 1]
        for w in range(n):
            for cp in copies(w, src_refs[w], land_refs[w], send_ref, recv_ref):
                cp.wait_send()
                cp.wait_recv()

    outs = pl.pallas_call(
        body, name=name,
        in_specs=[_HBM] * (2 * n) + [_SEM, _SEM, _ANY],
        out_specs=tuple([_HBM] * (2 * n)),
        out_shape=tuple(pltpu.HBM(a.shape, a.dtype) for a in srcs + lands),
        input_output_aliases={i: i for i in range(2 * n)},
        compiler_params=pltpu.CompilerParams(has_side_effects=_DATAFLOW),
    )(*srcs, *lands, send_sems, recv_sems, after)
    return list(outs[:n]), list(outs[n:])


def _chip_sums(g8, from_sibling, pos, *, name):
    _, R, C = g8.shape
    tr = _div(R, 512, 16)

    def body(pos_ref, g_ref, s_ref, o_ref):
        o_ref[...] = (g_ref[...].astype(F32) + s_ref[...].astype(F32)).astype(BF16)

    def chip(k, pos_ref):
        return jnp.where(k >= pos_ref[1], k + 1, k)

    grid_spec = pltpu.PrefetchScalarGridSpec(
        num_scalar_prefetch=1, grid=(3, R // tr),
        in_specs=[pl.BlockSpec((None, tr, C), lambda k, i, pos_ref: (2 * chip(k, pos_ref) + pos_ref[2], i, 0)),
                  pl.BlockSpec((None, tr, C), lambda k, i, pos_ref: (chip(k, pos_ref), i, 0))],
        out_specs=pl.BlockSpec((None, tr, C), lambda k, i, pos_ref: (chip(k, pos_ref), i, 0)))
    return pl.pallas_call(
        body, name=name, grid_spec=grid_spec,
        out_shape=jax.ShapeDtypeStruct((4, R, C), BF16),
        compiler_params=_cparams("parallel", "parallel"),
    )(pos, g8, from_sibling)


def _small_all_reduce(packed, after, *, name):
    R, L = packed.shape

    def body(x_ref, sum_ref, gath_ref, send_sems, recv_sems, local_sem):
        x, y, c = _position()
        me, sibling = (x, y, c), (x, y, 1 - c)
        chips = _other_chips(x, y)

        def rows(px, py, pc):
            return gath_ref.at[pl.ds(_slot(px, py, pc) * R, R), :]

        def copy(k, block, to, src=None):
            return pltpu.make_async_remote_copy(
                src_ref=rows(*block) if src is None else src, dst_ref=rows(*block),
                send_sem=send_sems.at[k], recv_sem=recv_sems.at[k], device_id=to, device_id_type=MESH)

        mine = pltpu.make_async_copy(x_ref, rows(*me), local_sem)
        mine.start()
        first = [copy(0, me, sibling, src=x_ref)]
        first += [copy(1 + j, me, (*chip, c), src=x_ref) for j, chip in enumerate(chips)]
        for cp in first:
            cp.start()
        passed = [copy(4 + j, (*chip, c), sibling) for j, chip in enumerate(chips)]
        for j, chip in enumerate(chips):
            copy(1 + j, (*chip, c), me).wait_recv()
            passed[j].start()
        copy(0, sibling, me).wait_recv()
        for j, chip in enumerate(chips):
            copy(4 + j, (*chip, 1 - c), me).wait_recv()
        for cp in first + passed:
            cp.wait_send()
        mine.wait()
        acc = gath_ref[0:R, :]
        for d in range(1, N_DEV):
            acc = acc + gath_ref[d * R:(d + 1) * R, :]
        sum_ref[...] = acc

    vmem = pl.BlockSpec(memory_space=pltpu.VMEM)
    body, in_specs, args = _ordered_after(body, 1, [vmem], (packed,), after)
    return pl.pallas_call(
        body, name=name, in_specs=in_specs, out_specs=vmem,
        out_shape=jax.ShapeDtypeStruct((R, L), F32),
        scratch_shapes=[pltpu.VMEM((N_DEV * R, L), F32), pltpu.SemaphoreType.DMA((7,)), pltpu.SemaphoreType.DMA((7,)),
                        pltpu.SemaphoreType.DMA],
        compiler_params=pltpu.CompilerParams(vmem_limit_bytes=VMEM_LIMIT),
    )(*args)


def _adamw_math(w, g, m, v):
    m = ADAM_B1 * m + (1.0 - ADAM_B1) * g
    v = ADAM_B2 * v + (1.0 - ADAM_B2) * (g * g)
    m_hat = m / (1.0 - ADAM_B1 ** ADAM_STEP)
    v_hat = v / (1.0 - ADAM_B2 ** ADAM_STEP)
    delta = -ADAM_LR * (m_hat / (jnp.sqrt(v_hat) + ADAM_EPS) + ADAM_WD * w)
    return delta, m, v


def _adamw_shard(w, m, v, g8, from_sibling, from_chips, pos, *, name):
    R, C = w.shape
    tr = _div(R, 256, 16)

    def body(pos_ref, w_ref, m_ref, v_ref, g_ref, s_ref, r_ref, go_ref, d_ref, mo_ref, vo_ref):
        g = g_ref[...].astype(F32) + s_ref[...].astype(F32)
        for k in range(3):
            g = g + r_ref[k].astype(F32)
        delta, m_, v_ = _adamw_math(w_ref[...], g, m_ref[...], v_ref[...])
        go_ref[...] = g
        d_ref[...] = delta
        mo_ref[...] = m_
        vo_ref[...] = v_

    blk = pl.BlockSpec((tr, C), lambda i, pos_ref: (i, 0))
    grid_spec = pltpu.PrefetchScalarGridSpec(
        num_scalar_prefetch=1, grid=(R // tr,),
        in_specs=[blk, blk, blk,
                  pl.BlockSpec((None, tr, C), lambda i, pos_ref: (pos_ref[0], i, 0)),
                  pl.BlockSpec((None, tr, C), lambda i, pos_ref: (pos_ref[1], i, 0)),
                  pl.BlockSpec((3, tr, C), lambda i, pos_ref: (0, i, 0))],
        out_specs=[blk] * 4)
    out = jax.ShapeDtypeStruct((R, C), F32)
    return pl.pallas_call(
        body, name=name, grid_spec=grid_spec, out_shape=[out] * 4,
        compiler_params=_cparams("parallel"),
    )(pos, w, m, v, g8, from_sibling, from_chips)


def _adamw_small(w, g, m, v, *, name):
    R, L = w.shape

    def body(w_ref, g_ref, m_ref, v_ref, d_ref, mo_ref, vo_ref):
        delta, m_, v_ = _adamw_math(w_ref[...], g_ref[...], m_ref[...], v_ref[...])
        d_ref[...] = delta
        mo_ref[...] = m_
        vo_ref[...] = v_

    vmem = pl.BlockSpec(memory_space=pltpu.VMEM)
    out = jax.ShapeDtypeStruct((R, L), F32)
    return pl.pallas_call(body, name=name, in_specs=[vmem] * 4, out_specs=[vmem] * 3, out_shape=[out] * 3)(w, g, m, v)


_TILE = 8 * LANES


def _pack(pieces):
    rows = []
    for p in pieces:
        flat = p.reshape(-1).astype(F32)
        padded = -(-flat.shape[0] // _TILE) * _TILE
        rows.append(jnp.pad(flat, (0, padded - flat.shape[0])).reshape(-1, LANES))
    return jnp.concatenate(rows, axis=0)


def _unpack(packed, like):
    out, r = [], 0
    for p in like:
        size = int(np.prod(p.shape)) if p.shape else 1
        nrows = -(-size // _TILE) * 8
        out.append(packed[r:r + nrows].reshape(-1)[:size].reshape(p.shape))
        r += nrows
    return out


_BIG = ("w_in", "w_a_out", "w_b_out", "w_o", "w_gate", "w_up", "w_down")
_TRANSPOSED = ("w_in", "w_gate", "w_up")
_COL_SHARDED = ("w_a_out", "w_b_out")
_GATHER_GROUPS = (("w_in",), ("w_a_out", "w_b_out", "w_o"), ("w_gate", "w_up"), ("w_down",))
_START_AFTER_WAIT = {0: (1, 2), 2: (3,)}
_SMALL = ("norm_mix", "sgu_v_gain", "sgu_w_s", "sgu_b_s", "attn_sink", "rel_bias", "norm_ffn", "norm_final")
_ORDER = ("w_in", "norm_mix", "sgu_v_gain", "sgu_w_s", "sgu_b_s", "w_a_out", "attn_sink", "rel_bias", "w_b_out", "w_o",
          "norm_ffn", "w_gate", "w_up", "w_down", "norm_final")


def _shard(name, a):
    return jnp.swapaxes(a, 1, 2)[0] if name in _TRANSPOSED else a[0]


def _unshard(name, a):
    return jnp.swapaxes(a[None], 1, 2) if name in _TRANSPOSED else a[None]


def _whole(name, gathered):
    _, r, c = gathered.shape
    return gathered if name in _COL_SHARDED else gathered.reshape(N_DEV * r, c)


def _blocks(name, grad):
    if name in _COL_SHARDED:
        return grad
    r, c = grad.shape
    return grad.reshape(N_DEV, r // N_DEV, c)


def kernel(x, w_in, norm_mix, sgu_v_gain, sgu_w_s, sgu_b_s, w_a_out, attn_sink, rel_bias, w_b_out, w_o, norm_ffn, w_gate, w_up, w_down, norm_final, loss_target, m_w_in, m_norm_mix, m_sgu_v_gain, m_sgu_w_s, m_sgu_b_s, m_w_a_out, m_attn_sink, m_rel_bias, m_w_b_out, m_w_o, m_norm_ffn, m_w_gate, m_w_up, m_w_down, m_norm_final, v_w_in, v_norm_mix, v_sgu_v_gain, v_sgu_w_s, v_sgu_b_s, v_w_a_out, v_attn_sink, v_rel_bias, v_w_b_out, v_w_o, v_norm_ffn, v_w_gate, v_w_up, v_w_down, v_norm_final):
    w = dict(w_in=w_in, norm_mix=norm_mix, sgu_v_gain=sgu_v_gain, sgu_w_s=sgu_w_s, sgu_b_s=sgu_b_s, w_a_out=w_a_out,
             attn_sink=attn_sink, rel_bias=rel_bias, w_b_out=w_b_out, w_o=w_o, norm_ffn=norm_ffn, w_gate=w_gate,
             w_up=w_up, w_down=w_down, norm_final=norm_final)
    m = dict(w_in=m_w_in, norm_mix=m_norm_mix, sgu_v_gain=m_sgu_v_gain, sgu_w_s=m_sgu_w_s, sgu_b_s=m_sgu_b_s,
             w_a_out=m_w_a_out, attn_sink=m_attn_sink, rel_bias=m_rel_bias, w_b_out=m_w_b_out, w_o=m_w_o,
             norm_ffn=m_norm_ffn, w_gate=m_w_gate, w_up=m_w_up, w_down=m_w_down, norm_final=m_norm_final)
    v = dict(w_in=v_w_in, norm_mix=v_norm_mix, sgu_v_gain=v_sgu_v_gain, sgu_w_s=v_sgu_w_s, sgu_b_s=v_sgu_b_s,
             w_a_out=v_w_a_out, attn_sink=v_attn_sink, rel_bias=v_rel_bias, w_b_out=v_w_b_out, w_o=v_w_o,
             norm_ffn=v_norm_ffn, w_gate=v_w_gate, w_up=v_w_up, w_down=v_w_down, norm_final=v_norm_final)
    xc, yc, cc = _position()
    pos = jnp.stack([_slot(xc, yc, cc), 2 * xc + yc, cc]).astype(jnp.int32)

    in_flight, full = {}, {}

    def start_gather(groups, after):
        names = [n for gi in groups for n in _GATHER_GROUPS[gi]]
        buffers = [_own_slot(_shard(n, w[n]), pos, name="own_slot_" + n, after=after) for n in names]
        flights, token = _ag_start(buffers, [[names.index(n) for n in _GATHER_GROUPS[gi]] for gi in groups],
                                   name="ag_start_%d" % groups[0])
        in_flight.update(zip(groups, flights))
        return token

    def weight(name, after):
        if name not in full:
            gi = next(i for i, grp in enumerate(_GATHER_GROUPS) if name in grp)
            send_sems, recv_sems, lands = in_flight[gi]
            lands, token = _ag_wait(send_sems, recv_sems, lands, after, name="ag_wait_%d" % gi)
            started = start_gather(_START_AFTER_WAIT[gi], token) if gi in _START_AFTER_WAIT else None
            gathered = _ag_forward(lands, name="ag_forward_%d" % gi, after=started)
            full.update({n: _whole(n, g) for n, g in zip(_GATHER_GROUPS[gi], gathered)})
        return full[name]

    start_gather((0,), None)

    to_sibling, reducing = [], {}

    def emit(names, grads):
        g8 = [_blocks(n, g) for n, g in zip(names, grads)]
        send_sems, recv_sems, g8, lands, token = _copies_start(_sibling_copies, 4, g8, name="rs_sibling_start_" + names[0])
        to_sibling.append((names, send_sems, recv_sems, g8, lands))
        return token

    def flush(after):
        names, send_sems, recv_sems, g8, lands = to_sibling.pop()
        g8, from_sibling = _copies_wait(_sibling_copies, send_sems, recv_sems, g8, lands, after,
                                        name="rs_sibling_wait_" + names[0])
        sums4 = [_chip_sums(g, s, pos, name="chip_sums_" + n) for n, g, s in zip(names, g8, from_sibling)]
        send_sems, recv_sems, sums4, lands, token = _copies_start(_chip_copies, 3, sums4, name="rs_chips_start_" + names[0])
        reducing[names] = (g8, from_sibling, send_sems, recv_sems, sums4, lands)
        return token

    loss, grad_x, small_grads_local = _local_step(
        x[0], loss_target[0], weight, emit, flush, norm_mix, sgu_v_gain, sgu_w_s[0], sgu_b_s[0], attn_sink, rel_bias,
        norm_ffn, norm_final[None])

    out_g, out_d, out_m, out_v = {}, {}, {}, {}
    small_like = [w[n] for n in _SMALL]
    small_w = _pack(small_like)
    packed = _pack([small_grads_local[n] for n in _SMALL] + [loss[0, 0]])
    after = grad_x
    for gi, (names, (g8, from_sibling, send_sems, recv_sems, sums4, lands)) in enumerate(reducing.items()):
        if gi == len(reducing) - 1:
            summed = _small_all_reduce(packed, after, name="small_all_reduce")
            after = summed
        _, from_chips = _copies_wait(_chip_copies, send_sems, recv_sems, sums4, lands, after,
                                     name="rs_chips_wait_" + names[0])
        for i, n in enumerate(names):
            g, d, m_, v_ = _adamw_shard(_shard(n, w[n]), _shard(n, m[n]), _shard(n, v[n]), g8[i], from_sibling[i],
                                        from_chips[i], pos, name="adamw_" + n)
            out_g[n], out_d[n], out_m[n], out_v[n] = (_unshard(n, o) for o in (g, d, m_, v_))
            after = d
    *small_grads, loss_sum = _unpack(summed, small_like + [jax.ShapeDtypeStruct((), F32)])
    d_s, m_s, v_s = _adamw_small(small_w, summed[:small_w.shape[0]], _pack([m[n] for n in _SMALL]),
                                 _pack([v[n] for n in _SMALL]), name="adamw_small")
    for n, g, d, m_, v_ in zip(_SMALL, small_grads, _unpack(d_s, small_like), _unpack(m_s, small_like), _unpack(v_s, small_like)):
        out_g[n], out_d[n], out_m[n], out_v[n] = g, d, m_, v_

    return (loss_sum, grad_x[None], *[out_g[n] for n in _ORDER], *[out_d[n] for n in _ORDER],
            *[out_m[n] for n in _ORDER], *[out_v[n] for n in _ORDER])
```

```python
import functools
import math

import numpy as np
import jax
import jax.numpy as jnp
from jax import lax
from jax.experimental import pallas as pl
from jax.experimental.pallas import tpu as pltpu

F32 = jnp.float32
BF16 = jnp.bfloat16

EPS = 1e-6
NEG = -1e30
HEAD_DIM = 128
BLOCK = 128
N_KV_HEADS = 2
KV_WIDTH = N_KV_HEADS * HEAD_DIM
REL_BUCKETS = 32
REL_MAX_DIST = 128

ADAM_LR = 0.001
ADAM_B1 = 0.9
ADAM_B2 = 0.999
ADAM_EPS = 1e-08
ADAM_WD = 0.01
ADAM_STEP = 10

N_DEV = 8
LANES = 128
VMEM_LIMIT = 56 * 1024 * 1024
MESH = pl.DeviceIdType.MESH


def _cparams(*sem):
    return pltpu.CompilerParams(dimension_semantics=sem, vmem_limit_bytes=VMEM_LIMIT)


def _div(n, target, mult=LANES):
    best = None
    for d in range(mult, min(n, target) + 1, mult):
        if n % d == 0:
            best = d
    assert best is not None, (n, target, mult)
    return best


_ANY = pl.BlockSpec(memory_space=pl.ANY)


def _ordered_after(body, n_inputs, in_specs, args, after):
    if after is None:
        return body, in_specs, args

    def wrapped(*refs):
        return body(*refs[:n_inputs], *refs[n_inputs + 1:])

    return wrapped, list(in_specs) + [_ANY], tuple(args) + (after,)


def _bucket_map():
    nb = REL_BUCKETS // 2
    qi = np.arange(BLOCK)[:, None]
    kj = np.arange(3 * BLOCK)[None, :]
    rel = kj - BLOCK - qi
    ret = np.where(rel > 0, nb, 0)
    n = np.abs(rel)
    max_exact = nb // 2
    nf = np.maximum(n, 1).astype(np.float32)
    large = max_exact + (np.log(nf / np.float32(max_exact)) / np.float32(math.log(REL_MAX_DIST / max_exact))
                         * np.float32(nb - max_exact)).astype(np.int32)
    large = np.minimum(large, nb - 1)
    return (ret + np.where(n < max_exact, n, large)).astype(np.int32)


_GELU_C = math.sqrt(2.0 / math.pi)
_GELU_A = 0.044715


def _gelu(x):
    t = jnp.tanh(_GELU_C * (x + _GELU_A * (x * x * x)))
    return 0.5 * x * (1.0 + t)


def _gelu_and_grad(x):
    x2 = x * x
    t = jnp.tanh(_GELU_C * (x + _GELU_A * (x2 * x)))
    g = 0.5 * x * (1.0 + t)
    dg = 0.5 * (1.0 + t) + 0.5 * x * (1.0 - t * t) * (_GELU_C * (1.0 + 3.0 * _GELU_A * x2))
    return g, dg


def _sigmoid(x):
    return 1.0 / (1.0 + jnp.exp(-x))


def _mm(a, b, *, name, ta=False, tb=False, add=None, out_dtype=F32, bm=1024, bn=1024, bk=None, after=None,
        row_blocks=None, into=None):
    if ta:
        K, M = a.shape
    else:
        M, K = a.shape
    N = b.shape[0] if tb else b.shape[1]
    assert (b.shape[1] if tb else b.shape[0]) == K
    bm = _div(M, bm)
    bn = _div(N, bn)
    bk = K if bk is None else _div(K, bk)
    nk = K // bk
    i0, ni = (0, M // bm) if row_blocks is None else row_blocks
    a_spec = (pl.BlockSpec((bk, bm), lambda i, j, k: (k, i + i0)) if ta
              else pl.BlockSpec((bm, bk), lambda i, j, k: (i + i0, k)))
    b_spec = pl.BlockSpec((bn, bk), lambda i, j, k: (j, k)) if tb else pl.BlockSpec((bk, bn), lambda i, j, k: (k, j))
    o_spec = pl.BlockSpec((bm, bn), lambda i, j, k: (i + i0, j))
    dims = (((0 if ta else 1,), (1 if tb else 0,)), ((), ()))
    has_add = add is not None

    def body(*refs):
        if has_add:
            a_ref, b_ref, add_ref, o_ref, *scratch = refs
        else:
            a_ref, b_ref, o_ref, *scratch = refs
            add_ref = None
        p = lax.dot_general(a_ref[...].astype(BF16), b_ref[...].astype(BF16), dims, preferred_element_type=F32)
        if nk == 1:
            if has_add:
                p = p + add_ref[...]
            o_ref[...] = p.astype(out_dtype)
        else:
            acc = scratch[0]
            k = pl.program_id(2)

            @pl.when(k == 0)
            def _():
                acc[...] = p

            @pl.when(k > 0)
            def _():
                acc[...] += p

            @pl.when(k == nk - 1)
            def _():
                r = acc[...]
                if has_add:
                    r = r + add_ref[...]
                o_ref[...] = r.astype(out_dtype)

    in_specs = [a_spec, b_spec] + ([o_spec] if has_add else [])
    args = (a, b) + ((add,) if has_add else ())
    aliases = {}
    if into is not None:
        body, in_specs, args = _ordered_after(body, len(args), in_specs, args, into)
        aliases = {len(args) - 1: 0}
    body, in_specs, args = _ordered_after(body, len(args), in_specs, args, after)
    return pl.pallas_call(
        body, name=name, grid=(ni, N // bn, nk),
        in_specs=in_specs, out_specs=o_spec,
        out_shape=jax.ShapeDtypeStruct((M, N), out_dtype),
        input_output_aliases=aliases,
        scratch_shapes=[pltpu.VMEM((bm, bn), F32)] if nk > 1 else [],
        compiler_params=_cparams("parallel", "parallel", "arbitrary"),
    )(*args)


def _blocks_per_tile(c):
    nb = 1
    while (nb * c) % LANES or (nb * c < 1024 and nb < N_DEV):
        nb *= 2
    assert nb <= N_DEV and (nb * c) % LANES == 0, c
    return nb


def _mm_w8(a, w8, *, name, bm=1024, out_dtype=F32):
    M, K = a.shape
    _, _, c = w8.shape
    nb = _blocks_per_tile(c)
    bm = _div(M, bm)

    def body(a_ref, w_ref, o_ref):
        a_ = a_ref[...]
        for t in range(nb):
            o_ref[:, t * c:(t + 1) * c] = jnp.dot(a_, w_ref[t], preferred_element_type=F32).astype(out_dtype)

    return pl.pallas_call(
        body, name=name, grid=(M // bm, N_DEV // nb),
        in_specs=[pl.BlockSpec((bm, K), lambda i, j: (i, 0)), pl.BlockSpec((nb, K, c), lambda i, j: (j, 0, 0))],
        out_specs=pl.BlockSpec((bm, nb * c), lambda i, j: (i, j)),
        out_shape=jax.ShapeDtypeStruct((M, N_DEV * c), out_dtype),
        compiler_params=_cparams("parallel", "parallel"),
    )(a, w8)


def _mm_w8t(dy, w8, *, name, add=None, out_dtype=F32, bm=1024, bn=1024, after=None, lead=None):
    M = dy.shape[-2]
    _, K, c = w8.shape
    nb = _blocks_per_tile(c)
    nk = N_DEV // nb
    bm, bn = _div(M, bm), _div(K, bn)
    has_add = add is not None
    dims = (((1,), (1,)), ((), ()))

    def body(*refs):
        if has_add:
            dy_ref, w_ref, add_ref, o_ref, acc = refs
        else:
            dy_ref, w_ref, o_ref, acc = refs
        p = lax.dot_general(dy_ref[:, 0:c], w_ref[0], dims, preferred_element_type=F32)
        for t in range(1, nb):
            p = p + lax.dot_general(dy_ref[:, t * c:(t + 1) * c], w_ref[t], dims, preferred_element_type=F32)
        k = pl.program_id(2)

        @pl.when(k == 0)
        def _():
            acc[...] = p

        @pl.when(k > 0)
        def _():
            acc[...] += p

        @pl.when(k == nk - 1)
        def _():
            r = acc[...]
            if has_add:
                r = r + add_ref[...]
            o_ref[...] = r.astype(out_dtype)

    o_spec = pl.BlockSpec((bm, bn), lambda i, j, k: (i, j))
    dy_spec = (pl.BlockSpec((bm, nb * c), lambda i, j, k: (i, k)) if lead is None
               else pl.BlockSpec((None, bm, nb * c), lambda i, j, k: (lead, i, k)))
    in_specs = [dy_spec, pl.BlockSpec((nb, bn, c), lambda i, j, k: (k, j, 0))]
    in_specs += [o_spec] if has_add else []
    args = (dy, w8) + ((add,) if has_add else ())
    body, in_specs, args = _ordered_after(body, len(args), in_specs, args, after)
    return pl.pallas_call(
        body, name=name, grid=(M // bm, K // bn, nk),
        in_specs=in_specs, out_specs=o_spec,
        out_shape=jax.ShapeDtypeStruct((M, K), out_dtype),
        scratch_shapes=[pltpu.VMEM((bm, bn), F32)],
        compiler_params=_cparams("parallel", "parallel", "arbitrary"),
    )(*args)


def _mm_gw8(x, dy, c, *, name, bk=1024, lead=None):
    T, K = x.shape
    nb = _blocks_per_tile(c)
    bk = _div(K, bk)
    dims = (((0,), (0,)), ((), ()))

    def body(x_ref, dy_ref, o_ref):
        x_ = x_ref[...]
        for t in range(nb):
            o_ref[t] = lax.dot_general(x_, dy_ref[:, t * c:(t + 1) * c], dims, preferred_element_type=F32).astype(BF16)

    dy_spec = (pl.BlockSpec((T, nb * c), lambda i, j: (0, j)) if lead is None
               else pl.BlockSpec((None, T, nb * c), lambda i, j: (lead, 0, j)))
    return pl.pallas_call(
        body, name=name, grid=(K // bk, N_DEV // nb),
        in_specs=[pl.BlockSpec((T, bk), lambda i, j: (0, i)), dy_spec],
        out_specs=pl.BlockSpec((nb, bk, c), lambda i, j: (j, i, 0)),
        out_shape=jax.ShapeDtypeStruct((N_DEV, K, c), BF16),
        compiler_params=_cparams("parallel", "parallel"),
    )(x, dy)


def _rms_fwd(x, g, *, name):
    T, D = x.shape
    tm = _div(T, 256, 8)

    def body(x_ref, g_ref, h_ref):
        xf = x_ref[...]
        r = lax.rsqrt(jnp.mean(xf * xf, axis=-1, keepdims=True) + EPS)
        h_ref[...] = ((xf * r) * g_ref[...]).astype(BF16)

    return pl.pallas_call(
        body, name=name, grid=(T // tm,),
        in_specs=[pl.BlockSpec((tm, D), lambda i: (i, 0)), pl.BlockSpec((1, D), lambda i: (0, 0))],
        out_specs=pl.BlockSpec((tm, D), lambda i: (i, 0)),
        out_shape=jax.ShapeDtypeStruct((T, D), BF16),
        compiler_params=_cparams("parallel"),
    )(x, g)


def _rms_bwd(x, g, dh, dres, *, name, want_bf16, after=None):
    T, D = x.shape
    tm = _div(T, 256, 8)

    def body(x_ref, g_ref, dh_ref, dres_ref, dx_ref, *rest):
        if want_bf16:
            dxb_ref, dg_ref = rest
        else:
            (dg_ref,) = rest
        xf = x_ref[...]
        r = lax.rsqrt(jnp.mean(xf * xf, axis=-1, keepdims=True) + EPS)
        xhat = xf * r
        dh_ = dh_ref[...]
        dy = dh_ * g_ref[...]
        dx = dres_ref[...] + r * (dy - xhat * jnp.mean(dy * xhat, axis=-1, keepdims=True))
        dx_ref[...] = dx
        if want_bf16:
            dxb_ref[...] = dx.astype(BF16)
        part = jnp.sum(dh_ * xhat, axis=0, keepdims=True)

        @pl.when(pl.program_id(0) == 0)
        def _():
            dg_ref[...] = part

        @pl.when(pl.program_id(0) > 0)
        def _():
            dg_ref[...] += part

    row = pl.BlockSpec((tm, D), lambda i: (i, 0))
    vec = pl.BlockSpec((1, D), lambda i: (0, 0))
    out_specs = [row] + ([row] if want_bf16 else []) + [vec]
    out_shape = ([jax.ShapeDtypeStruct((T, D), F32)] + ([jax.ShapeDtypeStruct((T, D), BF16)] if want_bf16 else [])
                 + [jax.ShapeDtypeStruct((1, D), F32)])
    body, in_specs, args = _ordered_after(body, 4, [row, vec, row, row], (x, g, dh, dres), after)
    return pl.pallas_call(
        body, name=name, grid=(T // tm,),
        in_specs=in_specs, out_specs=out_specs, out_shape=out_shape,
        compiler_params=_cparams("arbitrary"),
    )(*args)


def _loss_head(x, g, target, *, name):
    T, D = x.shape
    tm = _div(T, 256, 8)

    def body(x_ref, g_ref, t_ref, loss_ref, dx_ref, dxb_ref, dg_ref):
        xf = x_ref[...]
        r = lax.rsqrt(jnp.mean(xf * xf, axis=-1, keepdims=True) + EPS)
        xhat = xf * r
        gain = g_ref[...]
        err = xhat * gain - t_ref[...]
        lpart = 0.5 * jnp.sum(jnp.mean(err * err, axis=-1, keepdims=True), axis=0, keepdims=True)
        dh_ = err * (1.0 / D)
        dy = dh_ * gain
        dx = r * (dy - xhat * jnp.mean(dy * xhat, axis=-1, keepdims=True))
        dx_ref[...] = dx
        dxb_ref[...] = dx.astype(BF16)
        part = jnp.sum(dh_ * xhat, axis=0, keepdims=True)

        @pl.when(pl.program_id(0) == 0)
        def _():
            dg_ref[...] = part
            loss_ref[...] = jnp.broadcast_to(lpart, loss_ref.shape)

        @pl.when(pl.program_id(0) > 0)
        def _():
            dg_ref[...] += part
            loss_ref[...] += jnp.broadcast_to(lpart, loss_ref.shape)

    row = pl.BlockSpec((tm, D), lambda i: (i, 0))
    vec = pl.BlockSpec((1, D), lambda i: (0, 0))
    return pl.pallas_call(
        body, name=name, grid=(T // tm,),
        in_specs=[row, vec, row],
        out_specs=[pl.BlockSpec((8, LANES), lambda i: (0, 0)), row, row, vec],
        out_shape=[jax.ShapeDtypeStruct((8, LANES), F32), jax.ShapeDtypeStruct((T, D), F32),
                   jax.ShapeDtypeStruct((T, D), BF16), jax.ShapeDtypeStruct((1, D), F32)],
        compiler_params=_cparams("arbitrary"),
    )(x, g, target)


def _gate_cols(D):
    off_a = 3 * D // 2 + 2 * KV_WIDTH
    off_b = off_a + D
    cw = math.gcd(math.gcd(off_a, off_b), math.gcd(D, 512))
    return cw, off_a // cw, off_b // cw


def _merge_fwd(z, ya, yb, *, name):
    T, D = ya.shape
    cw, ba, bb = _gate_cols(D)
    tm = _div(T, 512, 8)

    def body(ga_ref, gb_ref, ya_ref, yb_ref, m_ref):
        m_ref[...] = (_sigmoid(ga_ref[...]) * ya_ref[...] + _sigmoid(gb_ref[...]) * yb_ref[...]).astype(BF16)

    blk = pl.BlockSpec((tm, cw), lambda i, j: (i, j))
    return pl.pallas_call(
        body, name=name, grid=(T // tm, D // cw),
        in_specs=[pl.BlockSpec((tm, cw), lambda i, j: (i, ba + j)), pl.BlockSpec((tm, cw), lambda i, j: (i, bb + j)), blk, blk],
        out_specs=blk, out_shape=jax.ShapeDtypeStruct((T, D), BF16),
        compiler_params=_cparams("parallel", "parallel"),
    )(z, z, ya, yb)


def _merge_bwd(z, ya, yb, dm, *, name, after=None):
    T, D = ya.shape
    cw, ba, bb = _gate_cols(D)
    nj = D // cw
    assert bb == ba + nj
    tm = _div(T, 512, 8)

    def body(g_ref, ya_ref, yb_ref, dm_ref, dy_ref, dz_ref):
        sig = _sigmoid(g_ref[...])
        dm_ = dm_ref[...]
        y = jnp.where(pl.program_id(1) == 0, ya_ref[...], yb_ref[...])
        dy_ref[...] = (dm_ * sig).astype(BF16)
        dz_ref[...] = (dm_ * y * (sig * (1.0 - sig))).astype(BF16)

    in_specs = [pl.BlockSpec((tm, cw), lambda i, s, j: (i, ba + s * nj + j)),
                pl.BlockSpec((tm, cw), lambda i, s, j: (i, j * (1 - s))),
                pl.BlockSpec((tm, cw), lambda i, s, j: (i, j * s)),
                pl.BlockSpec((tm, cw), lambda i, s, j: (i, j))]
    body, in_specs, args = _ordered_after(body, 4, in_specs, (z, ya, yb, dm), after)
    return pl.pallas_call(
        body, name=name, grid=(T // tm, 2, nj),
        in_specs=in_specs,
        out_specs=[pl.BlockSpec((None, tm, cw), lambda i, s, j: (s, i, j)),
                   pl.BlockSpec((tm, cw), lambda i, s, j: (i, ba + s * nj + j))],
        out_shape=[jax.ShapeDtypeStruct((2, T, D), BF16), jax.ShapeDtypeStruct(z.shape, BF16)],
        compiler_params=_cparams("parallel", "arbitrary", "arbitrary"),
    )(*args)


def _swiglu_mm_fwd(h, wg_t, wu_t, *, name, bm=1024, bn=512):
    T, D = h.shape
    F = wg_t.shape[0]
    bm, bn = _div(T, bm), _div(F, bn)
    dims = (((1,), (1,)), ((), ()))

    def body(h_ref, wg_ref, wu_ref, g_ref, u_ref, act_ref):
        h_ = h_ref[...]
        g = lax.dot_general(h_, wg_ref[...], dims, preferred_element_type=F32)
        u = lax.dot_general(h_, wu_ref[...], dims, preferred_element_type=F32)
        g_ref[...] = g.astype(BF16)
        u_ref[...] = u.astype(BF16)
        act_ref[...] = (g * _sigmoid(g) * u).astype(BF16)

    w_spec = pl.BlockSpec((bn, D), lambda i, j: (j, 0))
    o_spec = pl.BlockSpec((bm, bn), lambda i, j: (i, j))
    return pl.pallas_call(
        body, name=name, grid=(T // bm, F // bn),
        in_specs=[pl.BlockSpec((bm, D), lambda i, j: (i, 0)), w_spec, w_spec], out_specs=[o_spec] * 3,
        out_shape=[jax.ShapeDtypeStruct((T, F), BF16)] * 3,
        compiler_params=_cparams("parallel", "parallel"),
    )(h, wg_t, wu_t)


def _swiglu_mm_bwd(dx, w_down, gate, up, *, name, bm=2048, bn=512, after=None):
    T, D = dx.shape
    F = w_down.shape[0]
    bm, bn = _div(T, bm), _div(F, bn)
    dims = (((1,), (1,)), ((), ()))

    def body(dx_ref, w_ref, g_ref, u_ref, dg_ref, du_ref):
        d = lax.dot_general(dx_ref[...], w_ref[...], dims, preferred_element_type=F32)
        g = g_ref[...].astype(F32)
        s = _sigmoid(g)
        silu = g * s
        dg_ref[...] = (d * u_ref[...].astype(F32) * (s + silu * (1.0 - s))).astype(BF16)
        du_ref[...] = (d * silu).astype(BF16)

    o_spec = pl.BlockSpec((bm, bn), lambda i, j: (i, j))
    in_specs = [pl.BlockSpec((bm, D), lambda i, j: (i, 0)), pl.BlockSpec((bn, D), lambda i, j: (j, 0)), o_spec, o_spec]
    body, in_specs, args = _ordered_after(body, 4, in_specs, (dx, w_down, gate, up), after)
    out = jax.ShapeDtypeStruct((T, F), BF16)
    return pl.pallas_call(
        body, name=name, grid=(T // bm, F // bn), in_specs=in_specs, out_specs=[o_spec, o_spec], out_shape=[out, out],
        compiler_params=_cparams("parallel", "parallel"),
    )(*args)


def _sgu_fwd(z, gain, ws_b, bs_t, *, name):
    T = z.shape[0]
    SW = gain.shape[1]
    G = SW // BLOCK

    def body(zu_ref, zv_ref, gain_ref, ws_ref, bs_ref, a_ref):
        u = _gelu(zu_ref[...])
        vg = _gelu(zv_ref[...])
        r = lax.rsqrt(jnp.mean(vg * vg, axis=-1, keepdims=True) + EPS)
        vn = ((vg * r) * gain_ref[...]).astype(BF16)
        for g in range(G):
            sl = slice(g * BLOCK, (g + 1) * BLOCK)
            mixed = jnp.dot(ws_ref[g], vn[:, sl], preferred_element_type=F32) + bs_ref[:, g:g + 1]
            a_ref[:, sl] = (u[:, sl] * mixed).astype(BF16)

    return pl.pallas_call(
        body, name=name, grid=(T // BLOCK,),
        in_specs=[pl.BlockSpec((BLOCK, SW), lambda c: (c, 0)), pl.BlockSpec((BLOCK, SW), lambda c: (c, 1)),
                  pl.BlockSpec((1, SW), lambda c: (0, 0)), pl.BlockSpec((G, BLOCK, BLOCK), lambda c: (0, 0, 0)),
                  pl.BlockSpec((BLOCK, G), lambda c: (0, 0))],
        out_specs=pl.BlockSpec((BLOCK, SW), lambda c: (c, 0)),
        out_shape=jax.ShapeDtypeStruct((T, SW), BF16),
        compiler_params=_cparams("parallel"),
    )(z, z, gain, ws_b, bs_t)


def _sgu_bwd(z, gain, ws_b, bs_t, da, dz, *, name):
    T = z.shape[0]
    SW = gain.shape[1]
    G = SW // BLOCK

    def body(zu_ref, zv_ref, gain_ref, ws_ref, bs_ref, da_ref, dz_in_ref, dz_ref, dws_ref, dbs_ref, dgain_ref, dvn_ref):
        first = pl.program_id(0) == 0

        @pl.when(first)
        def _():
            dws_ref[...] = jnp.zeros_like(dws_ref)
            dbs_ref[...] = jnp.zeros_like(dbs_ref)
            dgain_ref[...] = jnp.zeros_like(dgain_ref)

        u, du = _gelu_and_grad(zu_ref[...])
        vg, dvg = _gelu_and_grad(zv_ref[...])
        r = lax.rsqrt(jnp.mean(vg * vg, axis=-1, keepdims=True) + EPS)
        xhat = vg * r
        gain_ = gain_ref[...]
        vn = (xhat * gain_).astype(BF16)
        da_ = da_ref[...]
        for g in range(G):
            sl = slice(g * BLOCK, (g + 1) * BLOCK)
            w = ws_ref[g]
            mixed = jnp.dot(w, vn[:, sl], preferred_element_type=F32) + bs_ref[:, g:g + 1]
            dmix = da_[:, sl] * u[:, sl]
            dz_ref[:, sl] = (da_[:, sl] * mixed * du[:, sl]).astype(BF16)
            dmb = dmix.astype(BF16)
            dws_ref[g] += lax.dot_general(dmb, vn[:, sl], (((1,), (1,)), ((), ())), preferred_element_type=F32)
            dbs_ref[:, g:g + 1] += jnp.sum(dmix, axis=-1, keepdims=True)
            dvn_ref[:, sl] = lax.dot_general(w, dmb, (((0,), (0,)), ((), ())), preferred_element_type=F32)
        dvn = dvn_ref[...]
        dgain_ref[...] += jnp.sum(dvn * xhat, axis=0, keepdims=True)
        dy = dvn * gain_
        dv_ = r * (dy - xhat * jnp.mean(dy * xhat, axis=-1, keepdims=True))
        dz_ref[:, SW:] = (dv_ * dvg).astype(BF16)

    row = pl.BlockSpec((BLOCK, SW), lambda c: (c, 0))
    return pl.pallas_call(
        body, name=name, grid=(T // BLOCK,),
        in_specs=[row, pl.BlockSpec((BLOCK, SW), lambda c: (c, 1)),
                  pl.BlockSpec((1, SW), lambda c: (0, 0)), pl.BlockSpec((G, BLOCK, BLOCK), lambda c: (0, 0, 0)),
                  pl.BlockSpec((BLOCK, G), lambda c: (0, 0)), row, _ANY],
        out_specs=[pl.BlockSpec((BLOCK, 2 * SW), lambda c: (c, 0)), pl.BlockSpec((G, BLOCK, BLOCK), lambda c: (0, 0, 0)),
                   pl.BlockSpec((BLOCK, G), lambda c: (0, 0)), pl.BlockSpec((1, SW), lambda c: (0, 0))],
        out_shape=[jax.ShapeDtypeStruct(dz.shape, dz.dtype),
                   jax.ShapeDtypeStruct((G, BLOCK, BLOCK), F32), jax.ShapeDtypeStruct((BLOCK, G), F32),
                   jax.ShapeDtypeStruct((1, SW), F32)],
        input_output_aliases={6: 0},
        scratch_shapes=[pltpu.VMEM((BLOCK, SW), F32)],
        compiler_params=_cparams("arbitrary"),
    )(z, z, gain, ws_b, bs_t, da, dz)


def _bias_table(rel_bias, bmap, *, name):
    H = rel_bias.shape[1]

    def body(rb_ref, bmap_ref, o_ref):
        bm_ = bmap_ref[...]
        for h in range(H):
            acc = jnp.zeros(bm_.shape, F32)
            for b in range(REL_BUCKETS):
                acc = jnp.where(bm_ == b, rb_ref[b, h], acc)
            o_ref[h] = acc

    return pl.pallas_call(
        body, name=name,
        in_specs=[pl.BlockSpec(memory_space=pltpu.SMEM), pl.BlockSpec(memory_space=pltpu.VMEM)],
        out_specs=pl.BlockSpec(memory_space=pltpu.VMEM),
        out_shape=jax.ShapeDtypeStruct((H, BLOCK, 3 * BLOCK), F32),
    )(rel_bias, bmap)


def _attn_probs(q_ref, kb, bias_ref, sink_ref, s_ref, n, T, group):
    H = s_ref.shape[0]
    for h in range(H):
        kv = h // group
        qh = q_ref[:, h * HEAD_DIM:(h + 1) * HEAD_DIM].astype(BF16)
        s_ref[h] = lax.dot_general(qh, kb[:, kv * HEAD_DIM:(kv + 1) * HEAD_DIM], (((1,), (1,)), ((), ())),
                                   preferred_element_type=F32)
    row = lax.broadcasted_iota(jnp.int32, (BLOCK, 3 * BLOCK), 0)
    col = lax.broadcasted_iota(jnp.int32, (BLOCK, 3 * BLOCK), 1)
    key_pos = n * BLOCK + col - BLOCK
    valid = (jnp.abs(col - BLOCK - row) <= BLOCK) & (key_pos >= 0) & (key_pos < T)
    s = s_ref[...] * (HEAD_DIM ** -0.5) + bias_ref[...]
    s = jnp.where(valid[None], s, NEG)
    sink = sink_ref[...]
    m = jnp.maximum(jnp.max(s, axis=-1, keepdims=True), sink)
    e = jnp.exp(s - m)
    es = jnp.exp(sink - m)
    inv = 1.0 / (jnp.sum(e, axis=-1, keepdims=True) + es)
    return e * inv, es * inv


def _attn_fwd(z, kpad, vpad, bias, sink, *, name):
    T = z.shape[0]
    H = bias.shape[0]
    AW = H * HEAD_DIM
    group = H // N_KV_HEADS

    def body(q_ref, k_ref, v_ref, bias_ref, sink_ref, o_ref, s_ref, p_ref):
        n = pl.program_id(0)
        start = pl.multiple_of(n * BLOCK, BLOCK)
        kb = k_ref[pl.ds(start, 3 * BLOCK), :]
        vb = v_ref[pl.ds(start, 3 * BLOCK), :]
        p, _ = _attn_probs(q_ref, kb, bias_ref, sink_ref, s_ref, n, T, group)
        p_ref[...] = p.astype(BF16)
        for h in range(H):
            kv = h // group
            o = jnp.dot(p_ref[h], vb[:, kv * HEAD_DIM:(kv + 1) * HEAD_DIM], preferred_element_type=F32)
            o_ref[:, h * HEAD_DIM:(h + 1) * HEAD_DIM] = o.astype(BF16)

    full_kv = pl.BlockSpec((T + 2 * BLOCK, KV_WIDTH), lambda n: (0, 0))
    return pl.pallas_call(
        body, name=name, grid=(T // BLOCK,),
        in_specs=[pl.BlockSpec((BLOCK, AW), lambda n: (n, 2)), full_kv, full_kv,
                  pl.BlockSpec((H, BLOCK, 3 * BLOCK), lambda n: (0, 0, 0)), pl.BlockSpec((H, 1, 1), lambda n: (0, 0, 0))],
        out_specs=pl.BlockSpec((BLOCK, AW), lambda n: (n, 0)),
        out_shape=jax.ShapeDtypeStruct((T, AW), BF16),
        scratch_shapes=[pltpu.VMEM((H, BLOCK, 3 * BLOCK), F32), pltpu.VMEM((H, BLOCK, 3 * BLOCK), BF16)],
        compiler_params=_cparams("parallel"),
    )(z, kpad, vpad, bias, sink)


def _attn_bwd(z, kpad, vpad, bias, sink, do, dz, *, name):
    T = z.shape[0]
    H = bias.shape[0]
    AW = H * HEAD_DIM
    group = H // N_KV_HEADS
    scale = HEAD_DIM ** -0.5

    def body(q_ref, k_ref, v_ref, bias_ref, sink_ref, do_ref, dz_in_ref, dq_ref, dk_ref, dv_ref, dbias_ref, dsink_ref,
             s_ref, dp_ref, p_ref, ds_ref):
        n = pl.program_id(0)

        @pl.when(n == 0)
        def _():
            dk_ref[...] = jnp.zeros_like(dk_ref)
            dv_ref[...] = jnp.zeros_like(dv_ref)
            dbias_ref[...] = jnp.zeros_like(dbias_ref)
            dsink_ref[...] = jnp.zeros_like(dsink_ref)

        start = pl.multiple_of(n * BLOCK, BLOCK)
        kb = k_ref[pl.ds(start, 3 * BLOCK), :]
        vb = v_ref[pl.ds(start, 3 * BLOCK), :]
        p, p_sink = _attn_probs(q_ref, kb, bias_ref, sink_ref, s_ref, n, T, group)
        s_ref[...] = p
        p_ref[...] = p.astype(BF16)
        for h in range(H):
            kv = h // group
            dp_ref[h] = lax.dot_general(do_ref[:, h * HEAD_DIM:(h + 1) * HEAD_DIM], vb[:, kv * HEAD_DIM:(kv + 1) * HEAD_DIM],
                                        (((1,), (1,)), ((), ())), preferred_element_type=F32)
        p = s_ref[...]
        dp = dp_ref[...]
        delta = jnp.sum(p * dp, axis=-1, keepdims=True)
        ds = p * (dp - delta)
        dbias_ref[...] += ds
        dsink_ref[...] += -(p_sink * delta)
        ds_ref[...] = ds.astype(BF16)
        for kv in range(N_KV_HEADS):
            ksl = slice(kv * HEAD_DIM, (kv + 1) * HEAD_DIM)
            dk_acc = jnp.zeros((3 * BLOCK, HEAD_DIM), F32)
            dv_acc = jnp.zeros((3 * BLOCK, HEAD_DIM), F32)
            for gi in range(group):
                h = kv * group + gi
                hsl = slice(h * HEAD_DIM, (h + 1) * HEAD_DIM)
                dsb = ds_ref[h]
                dq = jnp.dot(dsb, kb[:, ksl], preferred_element_type=F32) * scale
                dq_ref[:, hsl] = dq.astype(BF16)
                dk_acc = dk_acc + lax.dot_general(dsb, q_ref[:, hsl].astype(BF16), (((0,), (0,)), ((), ())),
                                                  preferred_element_type=F32)
                dv_acc = dv_acc + lax.dot_general(p_ref[h], do_ref[:, hsl], (((0,), (0,)), ((), ())),
                                                  preferred_element_type=F32)
            dk_ref[pl.ds(start, 3 * BLOCK), ksl] += dk_acc * scale
            dv_ref[pl.ds(start, 3 * BLOCK), ksl] += dv_acc

    full_kv = pl.BlockSpec((T + 2 * BLOCK, KV_WIDTH), lambda n: (0, 0))
    bias_spec = pl.BlockSpec((H, BLOCK, 3 * BLOCK), lambda n: (0, 0, 0))
    row = pl.BlockSpec((BLOCK, AW), lambda n: (n, 0))
    q_cols = pl.BlockSpec((BLOCK, AW), lambda n: (n, 2))
    band = (H, BLOCK, 3 * BLOCK)
    return pl.pallas_call(
        body, name=name, grid=(T // BLOCK,),
        in_specs=[q_cols, full_kv, full_kv, bias_spec, pl.BlockSpec((H, 1, 1), lambda n: (0, 0, 0)), row, _ANY],
        out_specs=[q_cols, full_kv, full_kv, bias_spec, pl.BlockSpec((H, BLOCK, 1), lambda n: (0, 0, 0))],
        out_shape=[jax.ShapeDtypeStruct(dz.shape, dz.dtype),
                   jax.ShapeDtypeStruct((T + 2 * BLOCK, KV_WIDTH), F32), jax.ShapeDtypeStruct((T + 2 * BLOCK, KV_WIDTH), F32),
                   jax.ShapeDtypeStruct(band, F32), jax.ShapeDtypeStruct((H, BLOCK, 1), F32)],
        input_output_aliases={6: 0},
        scratch_shapes=[pltpu.VMEM(band, F32), pltpu.VMEM(band, F32), pltpu.VMEM(band, BF16), pltpu.VMEM(band, BF16)],
        compiler_params=_cparams("arbitrary"),
    )(z, kpad, vpad, bias, sink, do, dz)


def _dkv_into(dkp, dvp, dz, *, name):
    T = dz.shape[0]
    D = (dz.shape[1] - 2 * KV_WIDTH) * 2 // 7
    col = (D + D // 2) // (2 * KV_WIDTH)
    assert col * 2 * KV_WIDTH == D + D // 2

    def body(dk_ref, dv_ref, dz_in_ref, o_ref):
        o_ref[:, :KV_WIDTH] = dk_ref[...].astype(BF16)
        o_ref[:, KV_WIDTH:] = dv_ref[...].astype(BF16)

    kv = pl.BlockSpec((BLOCK, KV_WIDTH), lambda n: (n + 1, 0))
    return pl.pallas_call(
        body, name=name, grid=(T // BLOCK,),
        in_specs=[kv, kv, _ANY], out_specs=pl.BlockSpec((BLOCK, 2 * KV_WIDTH), lambda n: (n, col)),
        out_shape=jax.ShapeDtypeStruct(dz.shape, dz.dtype), input_output_aliases={2: 0},
        compiler_params=_cparams("parallel"),
    )(dkp, dvp, dz)


def _kv_pad(z, *, name):
    T = z.shape[0]
    D = (z.shape[1] - 2 * KV_WIDTH) * 2 // 7
    kcol = (D + D // 2) // KV_WIDTH
    nb = T // BLOCK

    def body(k_ref, v_ref, ko_ref, vo_ref):
        b = pl.program_id(0)
        inside = (b >= 1) & (b <= nb)
        ko_ref[...] = jnp.where(inside, k_ref[...], 0.0).astype(BF16)
        vo_ref[...] = jnp.where(inside, v_ref[...], 0.0).astype(BF16)

    out = jax.ShapeDtypeStruct((T + 2 * BLOCK, KV_WIDTH), BF16)
    o_spec = pl.BlockSpec((BLOCK, KV_WIDTH), lambda b: (b, 0))
    return pl.pallas_call(
        body, name=name, grid=(nb + 2,),
        in_specs=[pl.BlockSpec((BLOCK, KV_WIDTH), lambda b: (jnp.clip(b - 1, 0, nb - 1), kcol)),
                  pl.BlockSpec((BLOCK, KV_WIDTH), lambda b: (jnp.clip(b - 1, 0, nb - 1), kcol + 1))],
        out_specs=[o_spec, o_spec], out_shape=[out, out],
        compiler_params=_cparams("parallel"),
    )(z, z)


def _attn_small_grads(dbias, dsink_rows, bmap, after, *, name):
    H = dbias.shape[0]

    def body(dbias_ref, dsink_ref, bmap_ref, drel_ref, ds_ref):
        bm_ = bmap_ref[...]
        for h in range(H):
            d = dbias_ref[h]
            for b in range(REL_BUCKETS):
                drel_ref[b, h] = jnp.sum(jnp.where(bm_ == b, d, 0.0))
            ds_ref[0, h] = jnp.sum(dsink_ref[h])

    vmem = pl.BlockSpec(memory_space=pltpu.VMEM)
    smem = pl.BlockSpec(memory_space=pltpu.SMEM)
    body, in_specs, args = _ordered_after(body, 3, [vmem, vmem, vmem], (dbias, dsink_rows, bmap), after)
    return pl.pallas_call(
        body, name=name, in_specs=in_specs, out_specs=[smem, smem],
        out_shape=[jax.ShapeDtypeStruct((REL_BUCKETS, H), F32), jax.ShapeDtypeStruct((1, H), F32)],
    )(*args)


def _local_step(x, target, weight, emit, flush, norm_mix, v_gain, w_s, b_s, sink, rel_bias, norm_ffn, norm_final):
    T, D = x.shape
    ws_b = w_s.astype(BF16)
    bs_t = b_s.T
    bmap = jnp.asarray(_bucket_map())
    sink = sink.reshape(-1, 1, 1)

    h = _rms_fwd(x, norm_mix, name="rms_mix")
    w_in = weight("w_in", h)
    z = _mm(h, w_in, tb=True, name="mm_z", bm=2048, bn=768)
    a = _sgu_fwd(z, v_gain, ws_b, bs_t, name="sgu_fwd")
    w_a = weight("w_a_out", a)
    ya = _mm_w8(a, w_a, name="mm_ya", bm=2048, out_dtype=BF16)
    kpad, vpad = _kv_pad(z, name="kv_pad")
    bias = _bias_table(rel_bias, bmap, name="bias_table")
    o = _attn_fwd(z, kpad, vpad, bias, sink, name="attn_fwd")
    w_b = weight("w_b_out", o)
    yb = _mm_w8(o, w_b, name="mm_yb", bm=2048, out_dtype=BF16)
    m = _merge_fwd(z, ya, yb, name="merge_fwd")
    w_o = weight("w_o", m)
    x1 = _mm(m, w_o, name="mm_x1", add=x, bm=2048, bn=512)
    h2 = _rms_fwd(x1, norm_ffn, name="rms_ffn")
    w_gate = weight("w_gate", h2)
    w_up = weight("w_up", h2)
    gate, up, act = _swiglu_mm_fwd(h2, w_gate, w_up, name="mm_gate_up")
    w_down = weight("w_down", act)
    x2 = _mm(act, w_down, name="mm_x2", add=x1, bm=1024, bn=1024, bk=2816)
    loss, dx2, dx2b, g_norm_final = _loss_head(x2, norm_final, target, name="loss_head")

    g_w_down = _mm(act, dx2b, ta=True, out_dtype=BF16, name="mm_gwdown", bm=512, bn=2048)
    tok = emit(("w_down",), (g_w_down,))
    dgate, dup = _swiglu_mm_bwd(dx2b, w_down, gate, up, name="mm_dact_swiglu", after=tok)
    tok = flush(dgate)
    g_w_gate = _mm(dgate, h2, ta=True, out_dtype=BF16, name="mm_gwgate", bm=512, bn=2048, after=tok)
    g_w_up = _mm(dup, h2, ta=True, out_dtype=BF16, name="mm_gwup", bm=512, bn=2048)
    tok = emit(("w_gate", "w_up"), (g_w_gate, g_w_up))
    dh2 = _mm(dgate, w_gate, name="mm_dh2a", bm=1024, bn=1024, bk=2816, after=tok)
    tok = flush(dh2)
    dh2 = _mm(dup, w_up, add=dh2, name="mm_dh2b", bm=1024, bn=1024, bk=2816, after=tok)
    dx1, dx1b, g_norm_ffn = _rms_bwd(x1, norm_ffn, dh2, dx2, name="rms_ffn_bwd", want_bf16=True)

    g_w_o = _mm(m, dx1b, ta=True, out_dtype=BF16, name="mm_gwo", bm=2048, bn=512)
    tok = emit(("w_o",), (g_w_o,))
    dm = _mm(dx1b, w_o, tb=True, name="mm_dm", bm=2048, bn=512, after=tok)
    tok = flush(dm)
    dy, dz = _merge_bwd(z, ya, yb, dm, name="merge_bwd", after=tok)
    g_w_a = _mm_gw8(a, dy, w_a.shape[2], name="mm_gwa", lead=0)
    g_w_b = _mm_gw8(o, dy, w_b.shape[2], name="mm_gwb", lead=1)
    tok = emit(("w_a_out", "w_b_out"), (g_w_a, g_w_b))
    da = _mm_w8t(dy, w_a, name="mm_da", bm=2048, bn=512, after=tok, lead=0)
    tok = flush(da)
    do = _mm_w8t(dy, w_b, out_dtype=BF16, name="mm_do", bm=2048, bn=512, after=tok, lead=1)
    dz, g_w_s, g_b_s_t, g_v_gain = _sgu_bwd(z, v_gain, ws_b, bs_t, da, dz, name="sgu_bwd")
    dz, dkp, dvp, dbias, dsink_rows = _attn_bwd(z, kpad, vpad, bias, sink, do, dz, name="attn_bwd")
    dz = _dkv_into(dkp, dvp, dz, name="dkv_into_dz")
    g_w_in = _mm(dz, h, ta=True, out_dtype=BF16, name="mm_gwin", bm=768, bn=2048)
    tok = emit(("w_in",), (g_w_in,))
    half = dict(bm=T // 2, bn=1024, bk=2560)
    dh = _mm(dz, w_in, name="mm_dh_top", row_blocks=(0, 1), after=tok, **half)
    tok = flush(dh)
    dh = _mm(dz, w_in, name="mm_dh_bottom", row_blocks=(1, 1), into=dh, after=tok, **half)
    g_rel_bias, g_sink = _attn_small_grads(dbias, dsink_rows, bmap, dh, name="attn_small_grads")
    grad_x, g_norm_mix = _rms_bwd(x, norm_mix, dh, dx1, name="rms_mix_bwd", want_bf16=False)

    small = dict(norm_mix=g_norm_mix, sgu_v_gain=g_v_gain, sgu_w_s=g_w_s, sgu_b_s=g_b_s_t.T, attn_sink=g_sink,
                 rel_bias=g_rel_bias, norm_ffn=g_norm_ffn, norm_final=g_norm_final)
    return loss, grad_x, small


def _position():
    return lax.axis_index("x"), lax.axis_index("y"), lax.axis_index("c")


def _other_chips(x, y):
    return [(1 - x, y), (x, 1 - y), (1 - x, 1 - y)]


def _slot(px, py, pc):
    return 4 * px + 2 * py + pc


_HBM = pl.BlockSpec(memory_space=pltpu.HBM)
_SEM = pl.BlockSpec(memory_space=pltpu.SEMAPHORE)
_DATAFLOW = pltpu.SideEffectType.DATAFLOW_SIDE_EFFECTING


def _in_hbm(a):
    return pltpu.with_memory_space_constraint(a, pltpu.HBM)


def _own_slot(shard, pos, *, name, after=None):
    R, C = shard.shape
    tr = _div(R, 256, 16)

    def body(pos_ref, w_ref, o_ref):
        o_ref[...] = w_ref[...].astype(BF16)

    body, in_specs, args = _ordered_after(body, 2, [pl.BlockSpec((tr, C), lambda i, pos_ref: (i, 0))], (pos, shard), after)
    grid_spec = pltpu.PrefetchScalarGridSpec(
        num_scalar_prefetch=1, grid=(R // tr,), in_specs=in_specs,
        out_specs=pl.BlockSpec((None, tr, C), lambda i, pos_ref: (pos_ref[0], i, 0)))
    return pl.pallas_call(
        body, name=name, grid_spec=grid_spec,
        out_shape=jax.ShapeDtypeStruct((N_DEV, R, C), BF16),
        compiler_params=_cparams("parallel"),
    )(*args)


def _ag_copies(w, land_ref, send_sems, recv_sems):
    x, y, c = _position()
    mine = land_ref.at[_slot(x, y, c)]
    targets = [(px, py, c) for px, py in _other_chips(x, y)] + [(x, y, 1 - c)]
    return [pltpu.make_async_remote_copy(src_ref=mine, dst_ref=mine, send_sem=send_sems.at[4 * w + k],
                                         recv_sem=recv_sems.at[4 * w + k], device_id=to, device_id_type=MESH)
            for k, to in enumerate(targets)]


def _ag_start(buffers, groups, *, name):
    lands = [buffers[i] for g in groups for i in g]
    n, ng = len(lands), len(groups)
    sizes = [len(g) for g in groups]

    def body(*refs):
        land_refs = refs[:n]
        sems = refs[n:n + 2 * ng]
        token = refs[-1]
        i = 0
        for g in range(ng):
            for w in range(sizes[g]):
                for cp in _ag_copies(w, land_refs[i], sems[2 * g], sems[2 * g + 1]):
                    cp.start()
                i += 1
        token[...] = jnp.zeros_like(token)

    sem_shapes = [pltpu.SemaphoreType.DMA((4 * k,)) for k in sizes for _ in range(2)]
    outs = pl.pallas_call(
        body, name=name,
        in_specs=[_HBM] * n,
        out_specs=tuple([_SEM] * (2 * ng) + [_HBM] * n + [pl.BlockSpec(memory_space=pltpu.VMEM)]),
        out_shape=tuple(sem_shapes + [pltpu.HBM(a.shape, a.dtype) for a in lands] + [jax.ShapeDtypeStruct((8, LANES), F32)]),
        input_output_aliases={i: 2 * ng + i for i in range(n)},
        compiler_params=pltpu.CompilerParams(has_side_effects=_DATAFLOW),
    )(*[_in_hbm(a) for a in lands])
    sems, thru = outs[:2 * ng], outs[2 * ng:2 * ng + n]
    result, i = [], 0
    for g in range(ng):
        k = sizes[g]
        result.append((sems[2 * g], sems[2 * g + 1], list(thru[i:i + k])))
        i += k
    return result, outs[-1]


def _ag_wait(send_sems, recv_sems, lands, after, *, name):
    n = len(lands)

    def body(*refs):
        land_refs = refs[:n]
        send_ref, recv_ref = refs[n], refs[n + 1]
        token = refs[-1]
        for w in range(n):
            for cp in _ag_copies(w, land_refs[w], send_ref, recv_ref):
                cp.wait_send()
                cp.wait_recv()
        token[...] = jnp.zeros_like(token)

    outs = pl.pallas_call(
        body, name=name,
        in_specs=[_HBM] * n + [_SEM, _SEM, _ANY],
        out_specs=tuple([_HBM] * n + [pl.BlockSpec(memory_space=pltpu.VMEM)]),
        out_shape=tuple([pltpu.HBM(a.shape, a.dtype) for a in lands] + [jax.ShapeDtypeStruct((8, LANES), F32)]),
        input_output_aliases={i: i for i in range(n)},
        compiler_params=pltpu.CompilerParams(has_side_effects=_DATAFLOW),
    )(*lands, send_sems, recv_sems, after)
    return list(outs[:n]), outs[n]


def _ag_forward(lands, *, name, after=None):
    n = len(lands)

    def body(*refs):
        in_refs, out_refs = refs[:n], refs[n:2 * n]
        send_sems, recv_sems = refs[2 * n:]
        x, y, c = _position()
        copies = []
        for w in range(n):
            for k, (px, py) in enumerate(_other_chips(x, y)):
                cp = pltpu.make_async_remote_copy(
                    src_ref=in_refs[w].at[_slot(px, py, c)], dst_ref=out_refs[w].at[_slot(px, py, c)],
                    send_sem=send_sems.at[3 * w + k], recv_sem=recv_sems.at[3 * w + k],
                    device_id=(x, y, 1 - c), device_id_type=MESH)
                cp.start()
                copies.append(cp)
        for cp in copies:
            cp.wait()

    body, in_specs, args = _ordered_after(body, n, [_ANY] * n, tuple(lands), after)
    return pl.pallas_call(
        body, name=name,
        in_specs=in_specs, out_specs=[_ANY] * n,
        out_shape=[jax.ShapeDtypeStruct(a.shape, a.dtype) for a in lands],
        input_output_aliases={i: i for i in range(n)},
        scratch_shapes=[pltpu.SemaphoreType.DMA((3 * n,)), pltpu.SemaphoreType.DMA((3 * n,))],
    )(*args)


def _sibling_copies(w, g8_ref, land_ref, send_sems, recv_sems):
    x, y, c = _position()
    return [pltpu.make_async_remote_copy(src_ref=g8_ref.at[2 * p + (1 - c)], dst_ref=land_ref.at[p],
                                         send_sem=send_sems.at[4 * w + p], recv_sem=recv_sems.at[4 * w + p],
                                         device_id=(x, y, 1 - c), device_id_type=MESH)
            for p in range(4)]


def _chip_copies(w, sums_ref, land_ref, send_sems, recv_sems):
    x, y, c = _position()
    return [pltpu.make_async_remote_copy(src_ref=sums_ref.at[2 * px + py], dst_ref=land_ref.at[k],
                                         send_sem=send_sems.at[3 * w + k], recv_sem=recv_sems.at[3 * w + k],
                                         device_id=(px, py, c), device_id_type=MESH)
            for k, (px, py) in enumerate(_other_chips(x, y))]


def _copies_start(copies, per_weight, srcs, *, name):
    n = len(srcs)
    lands = [lax.empty((per_weight,) + s.shape[1:], s.dtype) for s in srcs]

    def body(*refs):
        src_refs, land_refs = refs[:n], refs[n:2 * n]
        send_sems, recv_sems = refs[2 * n], refs[2 * n + 1]
        token = refs[-1]
        for w in range(n):
            for cp in copies(w, src_refs[w], land_refs[w], send_sems, recv_sems):
                cp.start()
        token[...] = jnp.zeros_like(token)

    outs = pl.pallas_call(
        body, name=name,
        in_specs=[_HBM] * (2 * n),
        out_specs=tuple([_SEM, _SEM] + [_HBM] * (2 * n) + [pl.BlockSpec(memory_space=pltpu.VMEM)]),
        out_shape=tuple([pltpu.SemaphoreType.DMA((per_weight * n,)), pltpu.SemaphoreType.DMA((per_weight * n,))]
                        + [pltpu.HBM(a.shape, a.dtype) for a in srcs + lands] + [jax.ShapeDtypeStruct((8, LANES), F32)]),
        input_output_aliases={i: 2 + i for i in range(2 * n)},
        compiler_params=pltpu.CompilerParams(has_side_effects=_DATAFLOW),
    )(*[_in_hbm(a) for a in srcs + lands])
    return outs[0], outs[1], list(outs[2:2 + n]), list(outs[2 + n:2 + 2 * n]), outs[-1]


def _copies_wait(copies, send_sems, recv_sems, srcs, lands, after, *, name):
    n = len(srcs)

    def body(*refs):
        src_refs, land_refs = refs[:n], refs[n:2 * n]
        send_ref, recv_ref = refs[2 * n], refs[2 * n + 1]
        for w in range(n):
            for cp in copies(w, src_refs[w], land_refs[w], send_ref, recv_ref):
                cp.wait_send()
                cp.wait_recv()

    outs = pl.pallas_call(
        body, name=name,
        in_specs=[_HBM] * (2 * n) + [_SEM, _SEM, _ANY],
        out_specs=tuple([_HBM] * (2 * n)),
        out_shape=tuple(pltpu.HBM(a.shape, a.dtype) for a in srcs + lands),
        input_output_aliases={i: i for i in range(2 * n)},
        compiler_params=pltpu.CompilerParams(has_side_effects=_DATAFLOW),
    )(*srcs, *lands, send_sems, recv_sems, after)
    return list(outs[:n]), list(outs[n:])


def _chip_sums(g8, from_sibling, pos, *, name):
    _, R, C = g8.shape
    tr = _div(R, 512, 16)

    def body(pos_ref, g_ref, s_ref, o_ref):
        o_ref[...] = (g_ref[...].astype(F32) + s_ref[...].astype(F32)).astype(BF16)

    def chip(k, pos_ref):
        return jnp.where(k >= pos_ref[1], k + 1, k)

    grid_spec = pltpu.PrefetchScalarGridSpec(
        num_scalar_prefetch=1, grid=(3, R // tr),
        in_specs=[pl.BlockSpec((None, tr, C), lambda k, i, pos_ref: (2 * chip(k, pos_ref) + pos_ref[2], i, 0)),
                  pl.BlockSpec((None, tr, C), lambda k, i, pos_ref: (chip(k, pos_ref), i, 0))],
        out_specs=pl.BlockSpec((None, tr, C), lambda k, i, pos_ref: (chip(k, pos_ref), i, 0)))
    return pl.pallas_call(
        body, name=name, grid_spec=grid_spec,
        out_shape=jax.ShapeDtypeStruct((4, R, C), BF16),
        compiler_params=_cparams("parallel", "parallel"),
    )(pos, g8, from_sibling)


def _small_all_reduce(packed, after, *, name):
    R, L = packed.shape

    def body(x_ref, sum_ref, gath_ref, send_sems, recv_sems, local_sem):
        x, y, c = _position()
        me, sibling = (x, y, c), (x, y, 1 - c)
        chips = _other_chips(x, y)

        def rows(px, py, pc):
            return gath_ref.at[pl.ds(_slot(px, py, pc) * R, R), :]

        def copy(k, block, to, src=None):
            return pltpu.make_async_remote_copy(
                src_ref=rows(*block) if src is None else src, dst_ref=rows(*block),
                send_sem=send_sems.at[k], recv_sem=recv_sems.at[k], device_id=to, device_id_type=MESH)

        mine = pltpu.make_async_copy(x_ref, rows(*me), local_sem)
        mine.start()
        first = [copy(0, me, sibling, src=x_ref)]
        first += [copy(1 + j, me, (*chip, c), src=x_ref) for j, chip in enumerate(chips)]
        for cp in first:
            cp.start()
        passed = [copy(4 + j, (*chip, c), sibling) for j, chip in enumerate(chips)]
        for j, chip in enumerate(chips):
            copy(1 + j, (*chip, c), me).wait_recv()
            passed[j].start()
        copy(0, sibling, me).wait_recv()
        for j, chip in enumerate(chips):
            copy(4 + j, (*chip, 1 - c), me).wait_recv()
        for cp in first + passed:
            cp.wait_send()
        mine.wait()
        acc = gath_ref[0:R, :]
        for d in range(1, N_DEV):
            acc = acc + gath_ref[d * R:(d + 1) * R, :]
        sum_ref[...] = acc

    vmem = pl.BlockSpec(memory_space=pltpu.VMEM)
    body, in_specs, args = _ordered_after(body, 1, [vmem], (packed,), after)
    return pl.pallas_call(
        body, name=name, in_specs=in_specs, out_specs=vmem,
        out_shape=jax.ShapeDtypeStruct((R, L), F32),
        scratch_shapes=[pltpu.VMEM((N_DEV * R, L), F32), pltpu.SemaphoreType.DMA((7,)), pltpu.SemaphoreType.DMA((7,)),
                        pltpu.SemaphoreType.DMA],
        compiler_params=pltpu.CompilerParams(vmem_limit_bytes=VMEM_LIMIT),
    )(*args)


def _adamw_math(w, g, m, v):
    m = ADAM_B1 * m + (1.0 - ADAM_B1) * g
    v = ADAM_B2 * v + (1.0 - ADAM_B2) * (g * g)
    m_hat = m / (1.0 - ADAM_B1 ** ADAM_STEP)
    v_hat = v / (1.0 - ADAM_B2 ** ADAM_STEP)
    delta = -ADAM_LR * (m_hat / (jnp.sqrt(v_hat) + ADAM_EPS) + ADAM_WD * w)
    return delta, m, v


def _adamw_shard(w, m, v, g8, from_sibling, from_chips, pos, *, name):
    R, C = w.shape
    tr = _div(R, 256, 16)

    def body(pos_ref, w_ref, m_ref, v_ref, g_ref, s_ref, r_ref, go_ref, d_ref, mo_ref, vo_ref):
        g = g_ref[...].astype(F32) + s_ref[...].astype(F32)
        for k in range(3):
            g = g + r_ref[k].astype(F32)
        delta, m_, v_ = _adamw_math(w_ref[...], g, m_ref[...], v_ref[...])
        go_ref[...] = g
        d_ref[...] = delta
        mo_ref[...] = m_
        vo_ref[...] = v_

    blk = pl.BlockSpec((tr, C), lambda i, pos_ref: (i, 0))
    grid_spec = pltpu.PrefetchScalarGridSpec(
        num_scalar_prefetch=1, grid=(R // tr,),
        in_specs=[blk, blk, blk,
                  pl.BlockSpec((None, tr, C), lambda i, pos_ref: (pos_ref[0], i, 0)),
                  pl.BlockSpec((None, tr, C), lambda i, pos_ref: (pos_ref[1], i, 0)),
                  pl.BlockSpec((3, tr, C), lambda i, pos_ref: (0, i, 0))],
        out_specs=[blk] * 4)
    out = jax.ShapeDtypeStruct((R, C), F32)
    return pl.pallas_call(
        body, name=name, grid_spec=grid_spec, out_shape=[out] * 4,
        compiler_params=_cparams("parallel"),
    )(pos, w, m, v, g8, from_sibling, from_chips)


def _adamw_small(w, g, m, v, *, name):
    R, L = w.shape

    def body(w_ref, g_ref, m_ref, v_ref, d_ref, mo_ref, vo_ref):
        delta, m_, v_ = _adamw_math(w_ref[...], g_ref[...], m_ref[...], v_ref[...])
        d_ref[...] = delta
        mo_ref[...] = m_
        vo_ref[...] = v_

    vmem = pl.BlockSpec(memory_space=pltpu.VMEM)
    out = jax.ShapeDtypeStruct((R, L), F32)
    return pl.pallas_call(body, name=name, in_specs=[vmem] * 4, out_specs=[vmem] * 3, out_shape=[out] * 3)(w, g, m, v)


_TILE = 8 * LANES


def _pack(pieces):
    rows = []
    for p in pieces:
        flat = p.reshape(-1).astype(F32)
        padded = -(-flat.shape[0] // _TILE) * _TILE
        rows.append(jnp.pad(flat, (0, padded - flat.shape[0])).reshape(-1, LANES))
    return jnp.concatenate(rows, axis=0)


def _unpack(packed, like):
    out, r = [], 0
    for p in like:
        size = int(np.prod(p.shape)) if p.shape else 1
        nrows = -(-size // _TILE) * 8
        out.append(packed[r:r + nrows].reshape(-1)[:size].reshape(p.shape))
        r += nrows
    return out


_BIG = ("w_in", "w_a_out", "w_b_out", "w_o", "w_gate", "w_up", "w_down")
_TRANSPOSED = ("w_in", "w_gate", "w_up")
_COL_SHARDED = ("w_a_out", "w_b_out")
_GATHER_GROUPS = (("w_in",), ("w_a_out", "w_b_out", "w_o"), ("w_gate", "w_up"), ("w_down",))
_START_AFTER_WAIT = {0: (1, 2), 2: (3,)}
_SMALL = ("norm_mix", "sgu_v_gain", "sgu_w_s", "sgu_b_s", "attn_sink", "rel_bias", "norm_ffn", "norm_final")
_ORDER = ("w_in", "norm_mix", "sgu_v_gain", "sgu_w_s", "sgu_b_s", "w_a_out", "attn_sink", "rel_bias", "w_b_out", "w_o",
          "norm_ffn", "w_gate", "w_up", "w_down", "norm_final")


def _shard(name, a):
    return jnp.swapaxes(a, 1, 2)[0] if name in _TRANSPOSED else a[0]


def _unshard(name, a):
    return jnp.swapaxes(a[None], 1, 2) if name in _TRANSPOSED else a[None]


def _whole(name, gathered):
    _, r, c = gathered.shape
    return gathered if name in _COL_SHARDED else gathered.reshape(N_DEV * r, c)


def _blocks(name, grad):
    if name in _COL_SHARDED:
        return grad
    r, c = grad.shape
    return grad.reshape(N_DEV, r // N_DEV, c)


def kernel(x, w_in, norm_mix, sgu_v_gain, sgu_w_s, sgu_b_s, w_a_out, attn_sink, rel_bias, w_b_out, w_o, norm_ffn, w_gate, w_up, w_down, norm_final, loss_target, m_w_in, m_norm_mix, m_sgu_v_gain, m_sgu_w_s, m_sgu_b_s, m_w_a_out, m_attn_sink, m_rel_bias, m_w_b_out, m_w_o, m_norm_ffn, m_w_gate, m_w_up, m_w_down, m_norm_final, v_w_in, v_norm_mix, v_sgu_v_gain, v_sgu_w_s, v_sgu_b_s, v_w_a_out, v_attn_sink, v_rel_bias, v_w_b_out, v_w_o, v_norm_ffn, v_w_gate, v_w_up, v_w_down, v_norm_final):
    w = dict(w_in=w_in, norm_mix=norm_mix, sgu_v_gain=sgu_v_gain, sgu_w_s=sgu_w_s, sgu_b_s=sgu_b_s, w_a_out=w_a_out,
             attn_sink=attn_sink, rel_bias=rel_bias, w_b_out=w_b_out, w_o=w_o, norm_ffn=norm_ffn, w_gate=w_gate,
             w_up=w_up, w_down=w_down, norm_final=norm_final)
    m = dict(w_in=m_w_in, norm_mix=m_norm_mix, sgu_v_gain=m_sgu_v_gain, sgu_w_s=m_sgu_w_s, sgu_b_s=m_sgu_b_s,
             w_a_out=m_w_a_out, attn_sink=m_attn_sink, rel_bias=m_rel_bias, w_b_out=m_w_b_out, w_o=m_w_o,
             norm_ffn=m_norm_ffn, w_gate=m_w_gate, w_up=m_w_up, w_down=m_w_down, norm_final=m_norm_final)
    v = dict(w_in=v_w_in, norm_mix=v_norm_mix, sgu_v_gain=v_sgu_v_gain, sgu_w_s=v_sgu_w_s, sgu_b_s=v_sgu_b_s,
             w_a_out=v_w_a_out, attn_sink=v_attn_sink, rel_bias=v_rel_bias, w_b_out=v_w_b_out, w_o=v_w_o,
             norm_ffn=v_norm_ffn, w_gate=v_w_gate, w_up=v_w_up, w_down=v_w_down, norm_final=v_norm_final)
    xc, yc, cc = _position()
    pos = jnp.stack([_slot(xc, yc, cc), 2 * xc + yc, cc]).astype(jnp.int32)

    in_flight, full = {}, {}

    def start_gather(groups, after):
        names = [n for gi in groups for n in _GATHER_GROUPS[gi]]
        buffers = [_own_slot(_shard(n, w[n]), pos, name="own_slot_" + n, after=after) for n in names]
        flights, token = _ag_start(buffers, [[names.index(n) for n in _GATHER_GROUPS[gi]] for gi in groups],
                                   name="ag_start_%d" % groups[0])
        in_flight.update(zip(groups, flights))
        return token

    def weight(name, after):
        if name not in full:
            gi = next(i for i, grp in enumerate(_GATHER_GROUPS) if name in grp)
            send_sems, recv_sems, lands = in_flight[gi]
            lands, token = _ag_wait(send_sems, recv_sems, lands, after, name="ag_wait_%d" % gi)
            started = start_gather(_START_AFTER_WAIT[gi], token) if gi in _START_AFTER_WAIT else None
            gathered = _ag_forward(lands, name="ag_forward_%d" % gi, after=started)
            full.update({n: _whole(n, g) for n, g in zip(_GATHER_GROUPS[gi], gathered)})
        return full[name]

    start_gather((0,), None)

    to_sibling, reducing = [], {}

    def emit(names, grads):
        g8 = [_blocks(n, g) for n, g in zip(names, grads)]
        send_sems, recv_sems, g8, lands, token = _copies_start(_sibling_copies, 4, g8, name="rs_sibling_start_" + names[0])
        to_sibling.append((names, send_sems, recv_sems, g8, lands))
        return token

    def flush(after):
        names, send_sems, recv_sems, g8, lands = to_sibling.pop()
        g8, from_sibling = _copies_wait(_sibling_copies, send_sems, recv_sems, g8, lands, after,
                                        name="rs_sibling_wait_" + names[0])
        sums4 = [_chip_sums(g, s, pos, name="chip_sums_" + n) for n, g, s in zip(names, g8, from_sibling)]
        send_sems, recv_sems, sums4, lands, token = _copies_start(_chip_copies, 3, sums4, name="rs_chips_start_" + names[0])
        reducing[names] = (g8, from_sibling, send_sems, recv_sems, sums4, lands)
        return token

    loss, grad_x, small_grads_local = _local_step(
        x[0], loss_target[0], weight, emit, flush, norm_mix, sgu_v_gain, sgu_w_s[0], sgu_b_s[0], attn_sink, rel_bias,
        norm_ffn, norm_final[None])

    out_g, out_d, out_m, out_v = {}, {}, {}, {}
    small_like = [w[n] for n in _SMALL]
    small_w = _pack(small_like)
    packed = _pack([small_grads_local[n] for n in _SMALL] + [loss[0, 0]])
    after = grad_x
    for gi, (names, (g8, from_sibling, send_sems, recv_sems, sums4, lands)) in enumerate(reducing.items()):
        if gi == len(reducing) - 1:
            summed = _small_all_reduce(packed, after, name="small_all_reduce")
            after = summed
        _, from_chips = _copies_wait(_chip_copies, send_sems, recv_sems, sums4, lands, after,
                                     name="rs_chips_wait_" + names[0])
        for i, n in enumerate(names):
            g, d, m_, v_ = _adamw_shard(_shard(n, w[n]), _shard(n, m[n]), _shard(n, v[n]), g8[i], from_sibling[i],
                                        from_chips[i], pos, name="adamw_" + n)
            out_g[n], out_d[n], out_m[n], out_v[n] = (_unshard(n, o) for o in (g, d, m_, v_))
            after = d
    *small_grads, loss_sum = _unpack(summed, small_like + [jax.ShapeDtypeStruct((), F32)])
    d_s, m_s, v_s = _adamw_small(small_w, summed[:small_w.shape[0]], _pack([m[n] for n in _SMALL]),
                                 _pack([v[n] for n in _SMALL]), name="adamw_small")
    for n, g, d, m_, v_ in zip(_SMALL, small_grads, _unpack(d_s, small_like), _unpack(m_s, small_like), _unpack(v_s, small_like)):
        out_g[n], out_d[n], out_m[n], out_v[n] = g, d, m_, v_

    return (loss_sum, grad_x[None], *[out_g[n] for n in _ORDER], *[out_d[n] for n in _ORDER],
            *[out_m[n] for n in _ORDER], *[out_v[n] for n in _ORDER])
```

```python
import functools
import math

import numpy as np
import jax
import jax.numpy as jnp
from jax import lax
from jax.experimental import pallas as pl
from jax.experimental.pallas import tpu as pltpu

F32 = jnp.float32
BF16 = jnp.bfloat16

EPS = 1e-6
NEG = -1e30
HEAD_DIM = 128
BLOCK = 128
N_KV_HEADS = 2
KV_WIDTH = N_KV_HEADS * HEAD_DIM
REL_BUCKETS = 32
REL_MAX_DIST = 128

ADAM_LR = 0.001
ADAM_B1 = 0.9
ADAM_B2 = 0.999
ADAM_EPS = 1e-08
ADAM_WD = 0.01
ADAM_STEP = 10

N_DEV = 8
LANES = 128
VMEM_LIMIT = 56 * 1024 * 1024
MESH = pl.DeviceIdType.MESH


def _cparams(*sem):
    return pltpu.CompilerParams(dimension_semantics=sem, vmem_limit_bytes=VMEM_LIMIT)


def _div(n, target, mult=LANES):
    best = None
    for d in range(mult, min(n, target) + 1, mult):
        if n % d == 0:
            best = d
    assert best is not None, (n, target, mult)
    return best


_ANY = pl.BlockSpec(memory_space=pl.ANY)


def _ordered_after(body, n_inputs, in_specs, args, after):
    if after is None:
        return body, in_specs, args

    def wrapped(*refs):
        return body(*refs[:n_inputs], *refs[n_inputs + 1:])

    return wrapped, list(in_specs) + [_ANY], tuple(args) + (after,)


def _bucket_map():
    nb = REL_BUCKETS // 2
    qi = np.arange(BLOCK)[:, None]
    kj = np.arange(3 * BLOCK)[None, :]
    rel = kj - BLOCK - qi
    ret = np.where(rel > 0, nb, 0)
    n = np.abs(rel)
    max_exact = nb // 2
    nf = np.maximum(n, 1).astype(np.float32)
    large = max_exact + (np.log(nf / np.float32(max_exact)) / np.float32(math.log(REL_MAX_DIST / max_exact))
                         * np.float32(nb - max_exact)).astype(np.int32)
    large = np.minimum(large, nb - 1)
    return (ret + np.where(n < max_exact, n, large)).astype(np.int32)


_GELU_C = math.sqrt(2.0 / math.pi)
_GELU_A = 0.044715


def _gelu(x):
    t = jnp.tanh(_GELU_C * (x + _GELU_A * (x * x * x)))
    return 0.5 * x * (1.0 + t)


def _gelu_and_grad(x):
    x2 = x * x
    t = jnp.tanh(_GELU_C * (x + _GELU_A * (x2 * x)))
    g = 0.5 * x * (1.0 + t)
    dg = 0.5 * (1.0 + t) + 0.5 * x * (1.0 - t * t) * (_GELU_C * (1.0 + 3.0 * _GELU_A * x2))
    return g, dg


def _sigmoid(x):
    return 1.0 / (1.0 + jnp.exp(-x))


def _mm(a, b, *, name, ta=False, tb=False, add=None, out_dtype=F32, bm=1024, bn=1024, bk=None, after=None,
        row_blocks=None, into=None):
    if ta:
        K, M = a.shape
    else:
        M, K = a.shape
    N = b.shape[0] if tb else b.shape[1]
    assert (b.shape[1] if tb else b.shape[0]) == K
    bm = _div(M, bm)
    bn = _div(N, bn)
    bk = K if bk is None else _div(K, bk)
    nk = K // bk
    i0, ni = (0, M // bm) if row_blocks is None else row_blocks
    a_spec = (pl.BlockSpec((bk, bm), lambda i, j, k: (k, i + i0)) if ta
              else pl.BlockSpec((bm, bk), lambda i, j, k: (i + i0, k)))
    b_spec = pl.BlockSpec((bn, bk), lambda i, j, k: (j, k)) if tb else pl.BlockSpec((bk, bn), lambda i, j, k: (k, j))
    o_spec = pl.BlockSpec((bm, bn), lambda i, j, k: (i + i0, j))
    dims = (((0 if ta else 1,), (1 if tb else 0,)), ((), ()))
    has_add = add is not None

    def body(*refs):
        if has_add:
            a_ref, b_ref, add_ref, o_ref, *scratch = refs
        else:
            a_ref, b_ref, o_ref, *scratch = refs
            add_ref = None
        p = lax.dot_general(a_ref[...].astype(BF16), b_ref[...].astype(BF16), dims, preferred_element_type=F32)
        if nk == 1:
            if has_add:
                p = p + add_ref[...]
            o_ref[...] = p.astype(out_dtype)
        else:
            acc = scratch[0]
            k = pl.program_id(2)

            @pl.when(k == 0)
            def _():
                acc[...] = p

            @pl.when(k > 0)
            def _():
                acc[...] += p

            @pl.when(k == nk - 1)
            def _():
                r = acc[...]
                if has_add:
                    r = r + add_ref[...]
                o_ref[...] = r.astype(out_dtype)

    in_specs = [a_spec, b_spec] + ([o_spec] if has_add else [])
    args = (a, b) + ((add,) if has_add else ())
    aliases = {}
    if into is not None:
        body, in_specs, args = _ordered_after(body, len(args), in_specs, args, into)
        aliases = {len(args) - 1: 0}
    body, in_specs, args = _ordered_after(body, len(args), in_specs, args, after)
    return pl.pallas_call(
        body, name=name, grid=(ni, N // bn, nk),
        in_specs=in_specs, out_specs=o_spec,
        out_shape=jax.ShapeDtypeStruct((M, N), out_dtype),
        input_output_aliases=aliases,
        scratch_shapes=[pltpu.VMEM((bm, bn), F32)] if nk > 1 else [],
        compiler_params=_cparams("parallel", "parallel", "arbitrary"),
    )(*args)


def _blocks_per_tile(c):
    nb = 1
    while (nb * c) % LANES or (nb * c < 1024 and nb < N_DEV):
        nb *= 2
    assert nb <= N_DEV and (nb * c) % LANES == 0, c
    return nb


def _mm_w8(a, w8, *, name, bm=1024, out_dtype=F32):
    M, K = a.shape
    _, _, c = w8.shape
    nb = _blocks_per_tile(c)
    bm = _div(M, bm)

    def body(a_ref, w_ref, o_ref):
        a_ = a_ref[...]
        for t in range(nb):
            o_ref[:, t * c:(t + 1) * c] = jnp.dot(a_, w_ref[t], preferred_element_type=F32).astype(out_dtype)

    return pl.pallas_call(
        body, name=name, grid=(M // bm, N_DEV // nb),
        in_specs=[pl.BlockSpec((bm, K), lambda i, j: (i, 0)), pl.BlockSpec((nb, K, c), lambda i, j: (j, 0, 0))],
        out_specs=pl.BlockSpec((bm, nb * c), lambda i, j: (i, j)),
        out_shape=jax.ShapeDtypeStruct((M, N_DEV * c), out_dtype),
        compiler_params=_cparams("parallel", "parallel"),
    )(a, w8)


def _mm_w8t(dy, w8, *, name, add=None, out_dtype=F32, bm=1024, bn=1024, after=None, lead=None):
    M = dy.shape[-2]
    _, K, c = w8.shape
    nb = _blocks_per_tile(c)
    nk = N_DEV // nb
    bm, bn = _div(M, bm), _div(K, bn)
    has_add = add is not None
    dims = (((1,), (1,)), ((), ()))

    def body(*refs):
        if has_add:
            dy_ref, w_ref, add_ref, o_ref, acc = refs
        else:
            dy_ref, w_ref, o_ref, acc = refs
        p = lax.dot_general(dy_ref[:, 0:c], w_ref[0], dims, preferred_element_type=F32)
        for t in range(1, nb):
            p = p + lax.dot_general(dy_ref[:, t * c:(t + 1) * c], w_ref[t], dims, preferred_element_type=F32)
        k = pl.program_id(2)

        @pl.when(k == 0)
        def _():
            acc[...] = p

        @pl.when(k > 0)
        def _():
            acc[...] += p

        @pl.when(k == nk - 1)
        def _():
            r = acc[...]
            if has_add:
                r = r + add_ref[...]
            o_ref[...] = r.astype(out_dtype)

    o_spec = pl.BlockSpec((bm, bn), lambda i, j, k: (i, j))
    dy_spec = (pl.BlockSpec((bm, nb * c), lambda i, j, k: (i, k)) if lead is None
               else pl.BlockSpec((None, bm, nb * c), lambda i, j, k: (lead, i, k)))
    in_specs = [dy_spec, pl.BlockSpec((nb, bn, c), lambda i, j, k: (k, j, 0))]
    in_specs += [o_spec] if has_add else []
    args = (dy, w8) + ((add,) if has_add else ())
    body, in_specs, args = _ordered_after(body, len(args), in_specs, args, after)
    return pl.pallas_call(
        body, name=name, grid=(M // bm, K // bn, nk),
        in_specs=in_specs, out_specs=o_spec,
        out_shape=jax.ShapeDtypeStruct((M, K), out_dtype),
        scratch_shapes=[pltpu.VMEM((bm, bn), F32)],
        compiler_params=_cparams("parallel", "parallel", "arbitrary"),
    )(*args)


def _mm_gw8(x, dy, c, *, name, bk=1024, lead=None):
    T, K = x.shape
    nb = _blocks_per_tile(c)
    bk = _div(K, bk)
    dims = (((0,), (0,)), ((), ()))

    def body(x_ref, dy_ref, o_ref):
        x_ = x_ref[...]
        for t in range(nb):
            o_ref[t] = lax.dot_general(x_, dy_ref[:, t * c:(t + 1) * c], dims, preferred_element_type=F32).astype(BF16)

    dy_spec = (pl.BlockSpec((T, nb * c), lambda i, j: (0, j)) if lead is None
               else pl.BlockSpec((None, T, nb * c), lambda i, j: (lead, 0, j)))
    return pl.pallas_call(
        body, name=name, grid=(K // bk, N_DEV // nb),
        in_specs=[pl.BlockSpec((T, bk), lambda i, j: (0, i)), dy_spec],
        out_specs=pl.BlockSpec((nb, bk, c), lambda i, j: (j, i, 0)),
        out_shape=jax.ShapeDtypeStruct((N_DEV, K, c), BF16),
        compiler_params=_cparams("parallel", "parallel"),
    )(x, dy)


def _rms_fwd(x, g, *, name):
    T, D = x.shape
    tm = _div(T, 256, 8)

    def body(x_ref, g_ref, h_ref):
        xf = x_ref[...]
        r = lax.rsqrt(jnp.mean(xf * xf, axis=-1, keepdims=True) + EPS)
        h_ref[...] = ((xf * r) * g_ref[...]).astype(BF16)

    return pl.pallas_call(
        body, name=name, grid=(T // tm,),
        in_specs=[pl.BlockSpec((tm, D), lambda i: (i, 0)), pl.BlockSpec((1, D), lambda i: (0, 0))],
        out_specs=pl.BlockSpec((tm, D), lambda i: (i, 0)),
        out_shape=jax.ShapeDtypeStruct((T, D), BF16),
        compiler_params=_cparams("parallel"),
    )(x, g)


def _rms_bwd(x, g, dh, dres, *, name, want_bf16, after=None):
    T, D = x.shape
    tm = _div(T, 256, 8)

    def body(x_ref, g_ref, dh_ref, dres_ref, dx_ref, *rest):
        if want_bf16:
            dxb_ref, dg_ref = rest
        else:
            (dg_ref,) = rest
        xf = x_ref[...]
        r = lax.rsqrt(jnp.mean(xf * xf, axis=-1, keepdims=True) + EPS)
        xhat = xf * r
        dh_ = dh_ref[...]
        dy = dh_ * g_ref[...]
        dx = dres_ref[...] + r * (dy - xhat * jnp.mean(dy * xhat, axis=-1, keepdims=True))
        dx_ref[...] = dx
        if want_bf16:
            dxb_ref[...] = dx.astype(BF16)
        part = jnp.sum(dh_ * xhat, axis=0, keepdims=True)

        @pl.when(pl.program_id(0) == 0)
        def _():
            dg_ref[...] = part

        @pl.when(pl.program_id(0) > 0)
        def _():
            dg_ref[...] += part

    row = pl.BlockSpec((tm, D), lambda i: (i, 0))
    vec = pl.BlockSpec((1, D), lambda i: (0, 0))
    out_specs = [row] + ([row] if want_bf16 else []) + [vec]
    out_shape = ([jax.ShapeDtypeStruct((T, D), F32)] + ([jax.ShapeDtypeStruct((T, D), BF16)] if want_bf16 else [])
                 + [jax.ShapeDtypeStruct((1, D), F32)])
    body, in_specs, args = _ordered_after(body, 4, [row, vec, row, row], (x, g, dh, dres), after)
    return pl.pallas_call(
        body, name=name, grid=(T // tm,),
        in_specs=in_specs, out_specs=out_specs, out_shape=out_shape,
        compiler_params=_cparams("arbitrary"),
    )(*args)


def _loss_head(x, g, target, *, name):
    T, D = x.shape
    tm = _div(T, 256, 8)

    def body(x_ref, g_ref, t_ref, loss_ref, dx_ref, dxb_ref, dg_ref):
        xf = x_ref[...]
        r = lax.rsqrt(jnp.mean(xf * xf, axis=-1, keepdims=True) + EPS)
        xhat = xf * r
        gain = g_ref[...]
        err = xhat * gain - t_ref[...]
        lpart = 0.5 * jnp.sum(jnp.mean(err * err, axis=-1, keepdims=True), axis=0, keepdims=True)
        dh_ = err * (1.0 / D)
        dy = dh_ * gain
        dx = r * (dy - xhat * jnp.mean(dy * xhat, axis=-1, keepdims=True))
        dx_ref[...] = dx
        dxb_ref[...] = dx.astype(BF16)
        part = jnp.sum(dh_ * xhat, axis=0, keepdims=True)

        @pl.when(pl.program_id(0) == 0)
        def _():
            dg_ref[...] = part
            loss_ref[...] = jnp.broadcast_to(lpart, loss_ref.shape)

        @pl.when(pl.program_id(0) > 0)
        def _():
            dg_ref[...] += part
            loss_ref[...] += jnp.broadcast_to(lpart, loss_ref.shape)

    row = pl.BlockSpec((tm, D), lambda i: (i, 0))
    vec = pl.BlockSpec((1, D), lambda i: (0, 0))
    return pl.pallas_call(
        body, name=name, grid=(T // tm,),
        in_specs=[row, vec, row],
        out_specs=[pl.BlockSpec((8, LANES), lambda i: (0, 0)), row, row, vec],
        out_shape=[jax.ShapeDtypeStruct((8, LANES), F32), jax.ShapeDtypeStruct((T, D), F32),
                   jax.ShapeDtypeStruct((T, D), BF16), jax.ShapeDtypeStruct((1, D), F32)],
        compiler_params=_cparams("arbitrary"),
    )(x, g, target)


def _gate_cols(D):
    off_a = 3 * D // 2 + 2 * KV_WIDTH
    off_b = off_a + D
    cw = math.gcd(math.gcd(off_a, off_b), math.gcd(D, 512))
    return cw, off_a // cw, off_b // cw


def _merge_fwd(z, ya, yb, *, name):
    T, D = ya.shape
    cw, ba, bb = _gate_cols(D)
    tm = _div(T, 512, 8)

    def body(ga_ref, gb_ref, ya_ref, yb_ref, m_ref):
        m_ref[...] = (_sigmoid(ga_ref[...]) * ya_ref[...] + _sigmoid(gb_ref[...]) * yb_ref[...]).astype(BF16)

    blk = pl.BlockSpec((tm, cw), lambda i, j: (i, j))
    return pl.pallas_call(
        body, name=name, grid=(T // tm, D // cw),
        in_specs=[pl.BlockSpec((tm, cw), lambda i, j: (i, ba + j)), pl.BlockSpec((tm, cw), lambda i, j: (i, bb + j)), blk, blk],
        out_specs=blk, out_shape=jax.ShapeDtypeStruct((T, D), BF16),
        compiler_params=_cparams("parallel", "parallel"),
    )(z, z, ya, yb)


def _merge_bwd(z, ya, yb, dm, *, name, after=None):
    T, D = ya.shape
    cw, ba, bb = _gate_cols(D)
    nj = D // cw
    assert bb == ba + nj
    tm = _div(T, 512, 8)

    def body(g_ref, ya_ref, yb_ref, dm_ref, dy_ref, dz_ref):
        sig = _sigmoid(g_ref[...])
        dm_ = dm_ref[...]
        y = jnp.where(pl.program_id(1) == 0, ya_ref[...], yb_ref[...])
        dy_ref[...] = (dm_ * sig).astype(BF16)
        dz_ref[...] = (dm_ * y * (sig * (1.0 - sig))).astype(BF16)

    in_specs = [pl.BlockSpec((tm, cw), lambda i, s, j: (i, ba + s * nj + j)),
                pl.BlockSpec((tm, cw), lambda i, s, j: (i, j * (1 - s))),
                pl.BlockSpec((tm, cw), lambda i, s, j: (i, j * s)),
                pl.BlockSpec((tm, cw), lambda i, s, j: (i, j))]
    body, in_specs, args = _ordered_after(body, 4, in_specs, (z, ya, yb, dm), after)
    return pl.pallas_call(
        body, name=name, grid=(T // tm, 2, nj),
        in_specs=in_specs,
        out_specs=[pl.BlockSpec((None, tm, cw), lambda i, s, j: (s, i, j)),
                   pl.BlockSpec((tm, cw), lambda i, s, j: (i, ba + s * nj + j))],
        out_shape=[jax.ShapeDtypeStruct((2, T, D), BF16), jax.ShapeDtypeStruct(z.shape, BF16)],
        compiler_params=_cparams("parallel", "arbitrary", "arbitrary"),
    )(*args)


def _swiglu_mm_fwd(h, wu_t, gate, *, name, bm=1024, bn=512):
    T, D = h.shape
    F = wu_t.shape[0]
    bm, bn = _div(T, bm), _div(F, bn)

    def body(h_ref, wu_ref, gin_ref, g_ref, u_ref, act_ref):
        u = lax.dot_general(h_ref[...], wu_ref[...], (((1,), (1,)), ((), ())), preferred_element_type=F32)
        g = gin_ref[...]
        g_ref[...] = g.astype(BF16)
        u_ref[...] = u.astype(BF16)
        act_ref[...] = (g * _sigmoid(g) * u).astype(BF16)

    o_spec = pl.BlockSpec((bm, bn), lambda i, j: (i, j))
    return pl.pallas_call(
        body, name=name, grid=(T // bm, F // bn),
        in_specs=[pl.BlockSpec((bm, D), lambda i, j: (i, 0)), pl.BlockSpec((bn, D), lambda i, j: (j, 0)), o_spec],
        out_specs=[o_spec] * 3, out_shape=[jax.ShapeDtypeStruct((T, F), BF16)] * 3,
        compiler_params=_cparams("parallel", "parallel"),
    )(h, wu_t, gate)


def _swiglu_mm_bwd(dx, w_down, gate, up, *, name, bm=2048, bn=512, after=None):
    T, D = dx.shape
    F = w_down.shape[0]
    bm, bn = _div(T, bm), _div(F, bn)
    dims = (((1,), (1,)), ((), ()))

    def body(dx_ref, w_ref, g_ref, u_ref, dg_ref, du_ref):
        d = lax.dot_general(dx_ref[...], w_ref[...], dims, preferred_element_type=F32)
        g = g_ref[...].astype(F32)
        s = _sigmoid(g)
        silu = g * s
        dg_ref[...] = (d * u_ref[...].astype(F32) * (s + silu * (1.0 - s))).astype(BF16)
        du_ref[...] = (d * silu).astype(BF16)

    o_spec = pl.BlockSpec((bm, bn), lambda i, j: (i, j))
    in_specs = [pl.BlockSpec((bm, D), lambda i, j: (i, 0)), pl.BlockSpec((bn, D), lambda i, j: (j, 0)), o_spec, o_spec]
    body, in_specs, args = _ordered_after(body, 4, in_specs, (dx, w_down, gate, up), after)
    out = jax.ShapeDtypeStruct((T, F), BF16)
    return pl.pallas_call(
        body, name=name, grid=(T // bm, F // bn), in_specs=in_specs, out_specs=[o_spec, o_spec], out_shape=[out, out],
        compiler_params=_cparams("parallel", "parallel"),
    )(*args)


def _sgu_fwd(z, gain, ws_b, bs_t, *, name):
    T = z.shape[0]
    SW = gain.shape[1]
    G = SW // BLOCK

    def body(zu_ref, zv_ref, gain_ref, ws_ref, bs_ref, a_ref):
        u = _gelu(zu_ref[...])
        vg = _gelu(zv_ref[...])
        r = lax.rsqrt(jnp.mean(vg * vg, axis=-1, keepdims=True) + EPS)
        vn = ((vg * r) * gain_ref[...]).astype(BF16)
        for g in range(G):
            sl = slice(g * BLOCK, (g + 1) * BLOCK)
            mixed = jnp.dot(ws_ref[g], vn[:, sl], preferred_element_type=F32) + bs_ref[:, g:g + 1]
            a_ref[:, sl] = (u[:, sl] * mixed).astype(BF16)

    return pl.pallas_call(
        body, name=name, grid=(T // BLOCK,),
        in_specs=[pl.BlockSpec((BLOCK, SW), lambda c: (c, 0)), pl.BlockSpec((BLOCK, SW), lambda c: (c, 1)),
                  pl.BlockSpec((1, SW), lambda c: (0, 0)), pl.BlockSpec((G, BLOCK, BLOCK), lambda c: (0, 0, 0)),
                  pl.BlockSpec((BLOCK, G), lambda c: (0, 0))],
        out_specs=pl.BlockSpec((BLOCK, SW), lambda c: (c, 0)),
        out_shape=jax.ShapeDtypeStruct((T, SW), BF16),
        compiler_params=_cparams("parallel"),
    )(z, z, gain, ws_b, bs_t)


def _sgu_bwd(z, gain, ws_b, bs_t, da, dz, *, name):
    T = z.shape[0]
    SW = gain.shape[1]
    G = SW // BLOCK

    def body(zu_ref, zv_ref, gain_ref, ws_ref, bs_ref, da_ref, dz_in_ref, dz_ref, dws_ref, dbs_ref, dgain_ref, dvn_ref):
        first = pl.program_id(0) == 0

        @pl.when(first)
        def _():
            dws_ref[...] = jnp.zeros_like(dws_ref)
            dbs_ref[...] = jnp.zeros_like(dbs_ref)
            dgain_ref[...] = jnp.zeros_like(dgain_ref)

        u, du = _gelu_and_grad(zu_ref[...])
        vg, dvg = _gelu_and_grad(zv_ref[...])
        r = lax.rsqrt(jnp.mean(vg * vg, axis=-1, keepdims=True) + EPS)
        xhat = vg * r
        gain_ = gain_ref[...]
        vn = (xhat * gain_).astype(BF16)
        da_ = da_ref[...]
        for g in range(G):
            sl = slice(g * BLOCK, (g + 1) * BLOCK)
            w = ws_ref[g]
            mixed = jnp.dot(w, vn[:, sl], preferred_element_type=F32) + bs_ref[:, g:g + 1]
            dmix = da_[:, sl] * u[:, sl]
            dz_ref[:, sl] = (da_[:, sl] * mixed * du[:, sl]).astype(BF16)
            dmb = dmix.astype(BF16)
            dws_ref[g] += lax.dot_general(dmb, vn[:, sl], (((1,), (1,)), ((), ())), preferred_element_type=F32)
            dbs_ref[:, g:g + 1] += jnp.sum(dmix, axis=-1, keepdims=True)
            dvn_ref[:, sl] = lax.dot_general(w, dmb, (((0,), (0,)), ((), ())), preferred_element_type=F32)
        dvn = dvn_ref[...]
        dgain_ref[...] += jnp.sum(dvn * xhat, axis=0, keepdims=True)
        dy = dvn * gain_
        dv_ = r * (dy - xhat * jnp.mean(dy * xhat, axis=-1, keepdims=True))
        dz_ref[:, SW:] = (dv_ * dvg).astype(BF16)

    row = pl.BlockSpec((BLOCK, SW), lambda c: (c, 0))
    return pl.pallas_call(
        body, name=name, grid=(T // BLOCK,),
        in_specs=[row, pl.BlockSpec((BLOCK, SW), lambda c: (c, 1)),
                  pl.BlockSpec((1, SW), lambda c: (0, 0)), pl.BlockSpec((G, BLOCK, BLOCK), lambda c: (0, 0, 0)),
                  pl.BlockSpec((BLOCK, G), lambda c: (0, 0)), row, _ANY],
        out_specs=[pl.BlockSpec((BLOCK, 2 * SW), lambda c: (c, 0)), pl.BlockSpec((G, BLOCK, BLOCK), lambda c: (0, 0, 0)),
                   pl.BlockSpec((BLOCK, G), lambda c: (0, 0)), pl.BlockSpec((1, SW), lambda c: (0, 0))],
        out_shape=[jax.ShapeDtypeStruct(dz.shape, dz.dtype),
                   jax.ShapeDtypeStruct((G, BLOCK, BLOCK), F32), jax.ShapeDtypeStruct((BLOCK, G), F32),
                   jax.ShapeDtypeStruct((1, SW), F32)],
        input_output_aliases={6: 0},
        scratch_shapes=[pltpu.VMEM((BLOCK, SW), F32)],
        compiler_params=_cparams("arbitrary"),
    )(z, z, gain, ws_b, bs_t, da, dz)


def _bias_table(rel_bias, bmap, *, name):
    H = rel_bias.shape[1]

    def body(rb_ref, bmap_ref, o_ref):
        bm_ = bmap_ref[...]
        for h in range(H):
            acc = jnp.zeros(bm_.shape, F32)
            for b in range(REL_BUCKETS):
                acc = jnp.where(bm_ == b, rb_ref[b, h], acc)
            o_ref[h] = acc

    return pl.pallas_call(
        body, name=name,
        in_specs=[pl.BlockSpec(memory_space=pltpu.SMEM), pl.BlockSpec(memory_space=pltpu.VMEM)],
        out_specs=pl.BlockSpec(memory_space=pltpu.VMEM),
        out_shape=jax.ShapeDtypeStruct((H, BLOCK, 3 * BLOCK), F32),
    )(rel_bias, bmap)


def _attn_probs(q_ref, kb, bias_ref, sink_ref, s_ref, n, T, group):
    H = s_ref.shape[0]
    for h in range(H):
        kv = h // group
        qh = q_ref[:, h * HEAD_DIM:(h + 1) * HEAD_DIM].astype(BF16)
        s_ref[h] = lax.dot_general(qh, kb[:, kv * HEAD_DIM:(kv + 1) * HEAD_DIM], (((1,), (1,)), ((), ())),
                                   preferred_element_type=F32)
    row = lax.broadcasted_iota(jnp.int32, (BLOCK, 3 * BLOCK), 0)
    col = lax.broadcasted_iota(jnp.int32, (BLOCK, 3 * BLOCK), 1)
    key_pos = n * BLOCK + col - BLOCK
    valid = (jnp.abs(col - BLOCK - row) <= BLOCK) & (key_pos >= 0) & (key_pos < T)
    s = s_ref[...] * (HEAD_DIM ** -0.5) + bias_ref[...]
    s = jnp.where(valid[None], s, NEG)
    sink = sink_ref[...]
    m = jnp.maximum(jnp.max(s, axis=-1, keepdims=True), sink)
    e = jnp.exp(s - m)
    es = jnp.exp(sink - m)
    inv = 1.0 / (jnp.sum(e, axis=-1, keepdims=True) + es)
    return e * inv, es * inv


def _attn_fwd(z, kpad, vpad, bias, sink, *, name):
    T = z.shape[0]
    H = bias.shape[0]
    AW = H * HEAD_DIM
    group = H // N_KV_HEADS

    def body(q_ref, k_ref, v_ref, bias_ref, sink_ref, o_ref, s_ref, p_ref):
        n = pl.program_id(0)
        start = pl.multiple_of(n * BLOCK, BLOCK)
        kb = k_ref[pl.ds(start, 3 * BLOCK), :]
        vb = v_ref[pl.ds(start, 3 * BLOCK), :]
        p, _ = _attn_probs(q_ref, kb, bias_ref, sink_ref, s_ref, n, T, group)
        p_ref[...] = p.astype(BF16)
        for h in range(H):
            kv = h // group
            o = jnp.dot(p_ref[h], vb[:, kv * HEAD_DIM:(kv + 1) * HEAD_DIM], preferred_element_type=F32)
            o_ref[:, h * HEAD_DIM:(h + 1) * HEAD_DIM] = o.astype(BF16)

    full_kv = pl.BlockSpec((T + 2 * BLOCK, KV_WIDTH), lambda n: (0, 0))
    return pl.pallas_call(
        body, name=name, grid=(T // BLOCK,),
        in_specs=[pl.BlockSpec((BLOCK, AW), lambda n: (n, 2)), full_kv, full_kv,
                  pl.BlockSpec((H, BLOCK, 3 * BLOCK), lambda n: (0, 0, 0)), pl.BlockSpec((H, 1, 1), lambda n: (0, 0, 0))],
        out_specs=pl.BlockSpec((BLOCK, AW), lambda n: (n, 0)),
        out_shape=jax.ShapeDtypeStruct((T, AW), BF16),
        scratch_shapes=[pltpu.VMEM((H, BLOCK, 3 * BLOCK), F32), pltpu.VMEM((H, BLOCK, 3 * BLOCK), BF16)],
        compiler_params=_cparams("parallel"),
    )(z, kpad, vpad, bias, sink)


def _attn_bwd(z, kpad, vpad, bias, sink, do, dz, *, name):
    T = z.shape[0]
    H = bias.shape[0]
    AW = H * HEAD_DIM
    group = H // N_KV_HEADS
    scale = HEAD_DIM ** -0.5

    def body(q_ref, k_ref, v_ref, bias_ref, sink_ref, do_ref, dz_in_ref, dq_ref, dk_ref, dv_ref, dbias_ref, dsink_ref,
             s_ref, dp_ref, p_ref, ds_ref):
        n = pl.program_id(0)

        @pl.when(n == 0)
        def _():
            dk_ref[...] = jnp.zeros_like(dk_ref)
            dv_ref[...] = jnp.zeros_like(dv_ref)
            dbias_ref[...] = jnp.zeros_like(dbias_ref)
            dsink_ref[...] = jnp.zeros_like(dsink_ref)

        start = pl.multiple_of(n * BLOCK, BLOCK)
        kb = k_ref[pl.ds(start, 3 * BLOCK), :]
        vb = v_ref[pl.ds(start, 3 * BLOCK), :]
        p, p_sink = _attn_probs(q_ref, kb, bias_ref, sink_ref, s_ref, n, T, group)
        s_ref[...] = p
        p_ref[...] = p.astype(BF16)
        for h in range(H):
            kv = h // group
            dp_ref[h] = lax.dot_general(do_ref[:, h * HEAD_DIM:(h + 1) * HEAD_DIM], vb[:, kv * HEAD_DIM:(kv + 1) * HEAD_DIM],
                                        (((1,), (1,)), ((), ())), preferred_element_type=F32)
        p = s_ref[...]
        dp = dp_ref[...]
        delta = jnp.sum(p * dp, axis=-1, keepdims=True)
        ds = p * (dp - delta)
        dbias_ref[...] += ds
        dsink_ref[...] += -(p_sink * delta)
        ds_ref[...] = ds.astype(BF16)
        for kv in range(N_KV_HEADS):
            ksl = slice(kv * HEAD_DIM, (kv + 1) * HEAD_DIM)
            dk_acc = jnp.zeros((3 * BLOCK, HEAD_DIM), F32)
            dv_acc = jnp.zeros((3 * BLOCK, HEAD_DIM), F32)
            for gi in range(group):
                h = kv * group + gi
                hsl = slice(h * HEAD_DIM, (h + 1) * HEAD_DIM)
                dsb = ds_ref[h]
                dq = jnp.dot(dsb, kb[:, ksl], preferred_element_type=F32) * scale
                dq_ref[:, hsl] = dq.astype(BF16)
                dk_acc = dk_acc + lax.dot_general(dsb, q_ref[:, hsl].astype(BF16), (((0,), (0,)), ((), ())),
                                                  preferred_element_type=F32)
                dv_acc = dv_acc + lax.dot_general(p_ref[h], do_ref[:, hsl], (((0,), (0,)), ((), ())),
                                                  preferred_element_type=F32)
            dk_ref[pl.ds(start, 3 * BLOCK), ksl] += dk_acc * scale
            dv_ref[pl.ds(start, 3 * BLOCK), ksl] += dv_acc

    full_kv = pl.BlockSpec((T + 2 * BLOCK, KV_WIDTH), lambda n: (0, 0))
    bias_spec = pl.BlockSpec((H, BLOCK, 3 * BLOCK), lambda n: (0, 0, 0))
    row = pl.BlockSpec((BLOCK, AW), lambda n: (n, 0))
    q_cols = pl.BlockSpec((BLOCK, AW), lambda n: (n, 2))
    band = (H, BLOCK, 3 * BLOCK)
    return pl.pallas_call(
        body, name=name, grid=(T // BLOCK,),
        in_specs=[q_cols, full_kv, full_kv, bias_spec, pl.BlockSpec((H, 1, 1), lambda n: (0, 0, 0)), row, _ANY],
        out_specs=[q_cols, full_kv, full_kv, bias_spec, pl.BlockSpec((H, BLOCK, 1), lambda n: (0, 0, 0))],
        out_shape=[jax.ShapeDtypeStruct(dz.shape, dz.dtype),
                   jax.ShapeDtypeStruct((T + 2 * BLOCK, KV_WIDTH), F32), jax.ShapeDtypeStruct((T + 2 * BLOCK, KV_WIDTH), F32),
                   jax.ShapeDtypeStruct(band, F32), jax.ShapeDtypeStruct((H, BLOCK, 1), F32)],
        input_output_aliases={6: 0},
        scratch_shapes=[pltpu.VMEM(band, F32), pltpu.VMEM(band, F32), pltpu.VMEM(band, BF16), pltpu.VMEM(band, BF16)],
        compiler_params=_cparams("arbitrary"),
    )(z, kpad, vpad, bias, sink, do, dz)


def _dkv_into(dkp, dvp, dz, *, name):
    T = dz.shape[0]
    D = (dz.shape[1] - 2 * KV_WIDTH) * 2 // 7
    col = (D + D // 2) // (2 * KV_WIDTH)
    assert col * 2 * KV_WIDTH == D + D // 2

    def body(dk_ref, dv_ref, dz_in_ref, o_ref):
        o_ref[:, :KV_WIDTH] = dk_ref[...].astype(BF16)
        o_ref[:, KV_WIDTH:] = dv_ref[...].astype(BF16)

    kv = pl.BlockSpec((BLOCK, KV_WIDTH), lambda n: (n + 1, 0))
    return pl.pallas_call(
        body, name=name, grid=(T // BLOCK,),
        in_specs=[kv, kv, _ANY], out_specs=pl.BlockSpec((BLOCK, 2 * KV_WIDTH), lambda n: (n, col)),
        out_shape=jax.ShapeDtypeStruct(dz.shape, dz.dtype), input_output_aliases={2: 0},
        compiler_params=_cparams("parallel"),
    )(dkp, dvp, dz)


def _kv_pad(z, *, name):
    T = z.shape[0]
    D = (z.shape[1] - 2 * KV_WIDTH) * 2 // 7
    kcol = (D + D // 2) // KV_WIDTH
    nb = T // BLOCK

    def body(k_ref, v_ref, ko_ref, vo_ref):
        b = pl.program_id(0)
        inside = (b >= 1) & (b <= nb)
        ko_ref[...] = jnp.where(inside, k_ref[...], 0.0).astype(BF16)
        vo_ref[...] = jnp.where(inside, v_ref[...], 0.0).astype(BF16)

    out = jax.ShapeDtypeStruct((T + 2 * BLOCK, KV_WIDTH), BF16)
    o_spec = pl.BlockSpec((BLOCK, KV_WIDTH), lambda b: (b, 0))
    return pl.pallas_call(
        body, name=name, grid=(nb + 2,),
        in_specs=[pl.BlockSpec((BLOCK, KV_WIDTH), lambda b: (jnp.clip(b - 1, 0, nb - 1), kcol)),
                  pl.BlockSpec((BLOCK, KV_WIDTH), lambda b: (jnp.clip(b - 1, 0, nb - 1), kcol + 1))],
        out_specs=[o_spec, o_spec], out_shape=[out, out],
        compiler_params=_cparams("parallel"),
    )(z, z)


def _attn_small_grads(dbias, dsink_rows, bmap, after, *, name):
    H = dbias.shape[0]

    def body(dbias_ref, dsink_ref, bmap_ref, drel_ref, ds_ref):
        bm_ = bmap_ref[...]
        for h in range(H):
            d = dbias_ref[h]
            for b in range(REL_BUCKETS):
                drel_ref[b, h] = jnp.sum(jnp.where(bm_ == b, d, 0.0))
            ds_ref[0, h] = jnp.sum(dsink_ref[h])

    vmem = pl.BlockSpec(memory_space=pltpu.VMEM)
    smem = pl.BlockSpec(memory_space=pltpu.SMEM)
    body, in_specs, args = _ordered_after(body, 3, [vmem, vmem, vmem], (dbias, dsink_rows, bmap), after)
    return pl.pallas_call(
        body, name=name, in_specs=in_specs, out_specs=[smem, smem],
        out_shape=[jax.ShapeDtypeStruct((REL_BUCKETS, H), F32), jax.ShapeDtypeStruct((1, H), F32)],
    )(*args)


def _local_step(x, target, weight, emit, flush, norm_mix, v_gain, w_s, b_s, sink, rel_bias, norm_ffn, norm_final):
    T, D = x.shape
    ws_b = w_s.astype(BF16)
    bs_t = b_s.T
    bmap = jnp.asarray(_bucket_map())
    sink = sink.reshape(-1, 1, 1)

    h = _rms_fwd(x, norm_mix, name="rms_mix")
    w_in = weight("w_in", h)
    z = _mm(h, w_in, tb=True, name="mm_z", bm=2048, bn=768)
    a = _sgu_fwd(z, v_gain, ws_b, bs_t, name="sgu_fwd")
    w_a = weight("w_a_out", a)
    ya = _mm_w8(a, w_a, name="mm_ya", bm=2048, out_dtype=BF16)
    kpad, vpad = _kv_pad(z, name="kv_pad")
    bias = _bias_table(rel_bias, bmap, name="bias_table")
    o = _attn_fwd(z, kpad, vpad, bias, sink, name="attn_fwd")
    w_b = weight("w_b_out", o)
    yb = _mm_w8(o, w_b, name="mm_yb", bm=2048, out_dtype=BF16)
    m = _merge_fwd(z, ya, yb, name="merge_fwd")
    w_o = weight("w_o", m)
    x1 = _mm(m, w_o, name="mm_x1", add=x, bm=2048, bn=512)
    h2 = _rms_fwd(x1, norm_ffn, name="rms_ffn")
    w_gate = weight("w_gate", h2)
    gate = _mm(h2, w_gate, tb=True, name="mm_gate", bm=2048, bn=512)
    w_up = weight("w_up", gate)
    gate, up, act = _swiglu_mm_fwd(h2, w_up, gate, name="mm_up_swiglu")
    w_down = weight("w_down", act)
    x2 = _mm(act, w_down, name="mm_x2", add=x1, bm=1024, bn=1024, bk=2816)
    loss, dx2, dx2b, g_norm_final = _loss_head(x2, norm_final, target, name="loss_head")

    g_w_down = _mm(act, dx2b, ta=True, out_dtype=BF16, name="mm_gwdown", bm=512, bn=2048)
    tok = emit(("w_down",), (g_w_down,))
    dgate, dup = _swiglu_mm_bwd(dx2b, w_down, gate, up, name="mm_dact_swiglu", after=tok)
    tok = flush(dgate)
    g_w_gate = _mm(dgate, h2, ta=True, out_dtype=BF16, name="mm_gwgate", bm=512, bn=2048, after=tok)
    g_w_up = _mm(dup, h2, ta=True, out_dtype=BF16, name="mm_gwup", bm=512, bn=2048)
    tok = emit(("w_gate", "w_up"), (g_w_gate, g_w_up))
    dh2 = _mm(dgate, w_gate, name="mm_dh2a", bm=1024, bn=1024, bk=2816, after=tok)
    tok = flush(dh2)
    dh2 = _mm(dup, w_up, add=dh2, name="mm_dh2b", bm=1024, bn=1024, bk=2816, after=tok)
    dx1, dx1b, g_norm_ffn = _rms_bwd(x1, norm_ffn, dh2, dx2, name="rms_ffn_bwd", want_bf16=True)

    g_w_o = _mm(m, dx1b, ta=True, out_dtype=BF16, name="mm_gwo", bm=2048, bn=512)
    tok = emit(("w_o",), (g_w_o,))
    dm = _mm(dx1b, w_o, tb=True, name="mm_dm", bm=2048, bn=512, after=tok)
    tok = flush(dm)
    dy, dz = _merge_bwd(z, ya, yb, dm, name="merge_bwd", after=tok)
    g_w_a = _mm_gw8(a, dy, w_a.shape[2], name="mm_gwa", lead=0)
    g_w_b = _mm_gw8(o, dy, w_b.shape[2], name="mm_gwb", lead=1)
    tok = emit(("w_a_out", "w_b_out"), (g_w_a, g_w_b))
    da = _mm_w8t(dy, w_a, name="mm_da", bm=2048, bn=512, after=tok, lead=0)
    tok = flush(da)
    do = _mm_w8t(dy, w_b, out_dtype=BF16, name="mm_do", bm=2048, bn=512, after=tok, lead=1)
    dz, g_w_s, g_b_s_t, g_v_gain = _sgu_bwd(z, v_gain, ws_b, bs_t, da, dz, name="sgu_bwd")
    dz, dkp, dvp, dbias, dsink_rows = _attn_bwd(z, kpad, vpad, bias, sink, do, dz, name="attn_bwd")
    dz = _dkv_into(dkp, dvp, dz, name="dkv_into_dz")
    g_w_in = _mm(dz, h, ta=True, out_dtype=BF16, name="mm_gwin", bm=768, bn=2048)
    tok = emit(("w_in",), (g_w_in,))
    half = dict(bm=T // 2, bn=1024, bk=2560)
    dh = _mm(dz, w_in, name="mm_dh_top", row_blocks=(0, 1), after=tok, **half)
    tok = flush(dh)
    dh = _mm(dz, w_in, name="mm_dh_bottom", row_blocks=(1, 1), into=dh, after=tok, **half)
    g_rel_bias, g_sink = _attn_small_grads(dbias, dsink_rows, bmap, dh, name="attn_small_grads")
    grad_x, g_norm_mix = _rms_bwd(x, norm_mix, dh, dx1, name="rms_mix_bwd", want_bf16=False)

    small = dict(norm_mix=g_norm_mix, sgu_v_gain=g_v_gain, sgu_w_s=g_w_s, sgu_b_s=g_b_s_t.T, attn_sink=g_sink,
                 rel_bias=g_rel_bias, norm_ffn=g_norm_ffn, norm_final=g_norm_final)
    return loss, grad_x, small


def _position():
    return lax.axis_index("x"), lax.axis_index("y"), lax.axis_index("c")


def _other_chips(x, y):
    return [(1 - x, y), (x, 1 - y), (1 - x, 1 - y)]


def _slot(px, py, pc):
    return 4 * px + 2 * py + pc


_HBM = pl.BlockSpec(memory_space=pltpu.HBM)
_SEM = pl.BlockSpec(memory_space=pltpu.SEMAPHORE)
_DATAFLOW = pltpu.SideEffectType.DATAFLOW_SIDE_EFFECTING


def _in_hbm(a):
    return pltpu.with_memory_space_constraint(a, pltpu.HBM)


def _own_slot(shard, pos, *, name, after=None):
    R, C = shard.shape
    tr = _div(R, 256, 16)

    def body(pos_ref, w_ref, o_ref):
        o_ref[...] = w_ref[...].astype(BF16)

    body, in_specs, args = _ordered_after(body, 2, [pl.BlockSpec((tr, C), lambda i, pos_ref: (i, 0))], (pos, shard), after)
    grid_spec = pltpu.PrefetchScalarGridSpec(
        num_scalar_prefetch=1, grid=(R // tr,), in_specs=in_specs,
        out_specs=pl.BlockSpec((None, tr, C), lambda i, pos_ref: (pos_ref[0], i, 0)))
    return pl.pallas_call(
        body, name=name, grid_spec=grid_spec,
        out_shape=jax.ShapeDtypeStruct((N_DEV, R, C), BF16),
        compiler_params=_cparams("parallel"),
    )(*args)


def _ag_copies(w, land_ref, send_sems, recv_sems):
    x, y, c = _position()
    mine = land_ref.at[_slot(x, y, c)]
    targets = [(px, py, c) for px, py in _other_chips(x, y)] + [(x, y, 1 - c)]
    return [pltpu.make_async_remote_copy(src_ref=mine, dst_ref=mine, send_sem=send_sems.at[4 * w + k],
                                         recv_sem=recv_sems.at[4 * w + k], device_id=to, device_id_type=MESH)
            for k, to in enumerate(targets)]


def _ag_start(buffers, groups, *, name):
    lands = [buffers[i] for g in groups for i in g]
    n, ng = len(lands), len(groups)
    sizes = [len(g) for g in groups]

    def body(*refs):
        land_refs = refs[:n]
        sems = refs[n:n + 2 * ng]
        token = refs[-1]
        i = 0
        for g in range(ng):
            for w in range(sizes[g]):
                for cp in _ag_copies(w, land_refs[i], sems[2 * g], sems[2 * g + 1]):
                    cp.start()
                i += 1
        token[...] = jnp.zeros_like(token)

    sem_shapes = [pltpu.SemaphoreType.DMA((4 * k,)) for k in sizes for _ in range(2)]
    outs = pl.pallas_call(
        body, name=name,
        in_specs=[_HBM] * n,
        out_specs=tuple([_SEM] * (2 * ng) + [_HBM] * n + [pl.BlockSpec(memory_space=pltpu.VMEM)]),
        out_shape=tuple(sem_shapes + [pltpu.HBM(a.shape, a.dtype) for a in lands] + [jax.ShapeDtypeStruct((8, LANES), F32)]),
        input_output_aliases={i: 2 * ng + i for i in range(n)},
        compiler_params=pltpu.CompilerParams(has_side_effects=_DATAFLOW),
    )(*[_in_hbm(a) for a in lands])
    sems, thru = outs[:2 * ng], outs[2 * ng:2 * ng + n]
    result, i = [], 0
    for g in range(ng):
        k = sizes[g]
        result.append((sems[2 * g], sems[2 * g + 1], list(thru[i:i + k])))
        i += k
    return result, outs[-1]


def _ag_wait(send_sems, recv_sems, lands, after, *, name):
    n = len(lands)

    def body(*refs):
        land_refs = refs[:n]
        send_ref, recv_ref = refs[n], refs[n + 1]
        token = refs[-1]
        for w in range(n):
            for cp in _ag_copies(w, land_refs[w], send_ref, recv_ref):
                cp.wait_send()
                cp.wait_recv()
        token[...] = jnp.zeros_like(token)

    outs = pl.pallas_call(
        body, name=name,
        in_specs=[_HBM] * n + [_SEM, _SEM, _ANY],
        out_specs=tuple([_HBM] * n + [pl.BlockSpec(memory_space=pltpu.VMEM)]),
        out_shape=tuple([pltpu.HBM(a.shape, a.dtype) for a in lands] + [jax.ShapeDtypeStruct((8, LANES), F32)]),
        input_output_aliases={i: i for i in range(n)},
        compiler_params=pltpu.CompilerParams(has_side_effects=_DATAFLOW),
    )(*lands, send_sems, recv_sems, after)
    return list(outs[:n]), outs[n]


def _ag_forward(lands, *, name, after=None):
    n = len(lands)

    def body(*refs):
        in_refs, out_refs = refs[:n], refs[n:2 * n]
        send_sems, recv_sems = refs[2 * n:]
        x, y, c = _position()
        copies = []
        for w in range(n):
            for k, (px, py) in enumerate(_other_chips(x, y)):
                cp = pltpu.make_async_remote_copy(
                    src_ref=in_refs[w].at[_slot(px, py, c)], dst_ref=out_refs[w].at[_slot(px, py, c)],
                    send_sem=send_sems.at[3 * w + k], recv_sem=recv_sems.at[3 * w + k],
                    device_id=(x, y, 1 - c), device_id_type=MESH)
                cp.start()
                copies.append(cp)
        for cp in copies:
            cp.wait()

    body, in_specs, args = _ordered_after(body, n, [_ANY] * n, tuple(lands), after)
    return pl.pallas_call(
        body, name=name,
        in_specs=in_specs, out_specs=[_ANY] * n,
        out_shape=[jax.ShapeDtypeStruct(a.shape, a.dtype) for a in lands],
        input_output_aliases={i: i for i in range(n)},
        scratch_shapes=[pltpu.SemaphoreType.DMA((3 * n,)), pltpu.SemaphoreType.DMA((3 * n,))],
    )(*args)


def _sibling_copies(w, g8_ref, land_ref, send_sems, recv_sems):
    x, y, c = _position()
    return [pltpu.make_async_remote_copy(src_ref=g8_ref.at[2 * p + (1 - c)], dst_ref=land_ref.at[p],
                                         send_sem=send_sems.at[4 * w + p], recv_sem=recv_sems.at[4 * w + p],
                                         device_id=(x, y, 1 - c), device_id_type=MESH)
            for p in range(4)]


def _chip_copies(w, sums_ref, land_ref, send_sems, recv_sems):
    x, y, c = _position()
    return [pltpu.make_async_remote_copy(src_ref=sums_ref.at[2 * px + py], dst_ref=land_ref.at[k],
                                         send_sem=send_sems.at[3 * w + k], recv_sem=recv_sems.at[3 * w + k],
                                         device_id=(px, py, c), device_id_type=MESH)
            for k, (px, py) in enumerate(_other_chips(x, y))]


def _copies_start(copies, per_weight, srcs, *, name):
    n = len(srcs)
    lands = [lax.empty((per_weight,) + s.shape[1:], s.dtype) for s in srcs]

    def body(*refs):
        src_refs, land_refs = refs[:n], refs[n:2 * n]
        send_sems, recv_sems = refs[2 * n], refs[2 * n + 1]
        token = refs[-1]
        for w in range(n):
            for cp in copies(w, src_refs[w], land_refs[w], send_sems, recv_sems):
                cp.start()
        token[...] = jnp.zeros_like(token)

    outs = pl.pallas_call(
        body, name=name,
        in_specs=[_HBM] * (2 * n),
        out_specs=tuple([_SEM, _SEM] + [_HBM] * (2 * n) + [pl.BlockSpec(memory_space=pltpu.VMEM)]),
        out_shape=tuple([pltpu.SemaphoreType.DMA((per_weight * n,)), pltpu.SemaphoreType.DMA((per_weight * n,))]
                        + [pltpu.HBM(a.shape, a.dtype) for a in srcs + lands] + [jax.ShapeDtypeStruct((8, LANES), F32)]),
        input_output_aliases={i: 2 + i for i in range(2 * n)},
        compiler_params=pltpu.CompilerParams(has_side_effects=_DATAFLOW),
    )(*[_in_hbm(a) for a in srcs + lands])
    return outs[0], outs[1], list(outs[2:2 + n]), list(outs[2 + n:2 + 2 * n]), outs[-1]


def _copies_wait(copies, send_sems, recv_sems, srcs, lands, after, *, name):
    n = len(srcs)

    def body(*refs):
        src_refs, land_refs = refs[:n], refs[n:2 * n]
        send_ref, recv_ref = refs[2 * n], refs[2 * n + 1]
        for w in range(n):
            for cp in copies(w, src_refs[w], land_refs[w], send_ref, recv_ref):
                cp.wait_send()
                cp.wait_recv()

    outs = pl.pallas_call(
        body, name=name,
        in_specs=[_HBM] * (2 * n) + [_SEM, _SEM, _ANY],
        out_specs=tuple([_HBM] * (2 * n)),
        out_shape=tuple(pltpu.HBM(a.shape, a.dtype) for a in srcs + lands),
        input_output_aliases={i: i for i in range(2 * n)},
        compiler_params=pltpu.CompilerParams(has_side_effects=_DATAFLOW),
    )(*srcs, *lands, send_sems, recv_sems, after)
    return list(outs[:n]), list(outs[n:])


def _chip_sums(g8, from_sibling, pos, *, name):
    _, R, C = g8.shape
    tr = _div(R, 512, 16)

    def body(pos_ref, g_ref, s_ref, o_ref):
        o_ref[...] = (g_ref[...].astype(F32) + s_ref[...].astype(F32)).astype(BF16)

    def chip(k, pos_ref):
        return jnp.where(k >= pos_ref[1], k + 1, k)

    grid_spec = pltpu.PrefetchScalarGridSpec(
        num_scalar_prefetch=1, grid=(3, R // tr),
        in_specs=[pl.BlockSpec((None, tr, C), lambda k, i, pos_ref: (2 * chip(k, pos_ref) + pos_ref[2], i, 0)),
                  pl.BlockSpec((None, tr, C), lambda k, i, pos_ref: (chip(k, pos_ref), i, 0))],
        out_specs=pl.BlockSpec((None, tr, C), lambda k, i, pos_ref: (chip(k, pos_ref), i, 0)))
    return pl.pallas_call(
        body, name=name, grid_spec=grid_spec,
        out_shape=jax.ShapeDtypeStruct((4, R, C), BF16),
        compiler_params=_cparams("parallel", "parallel"),
    )(pos, g8, from_sibling)


def _small_all_reduce(packed, after, *, name):
    R, L = packed.shape

    def body(x_ref, sum_ref, gath_ref, send_sems, recv_sems, local_sem):
        x, y, c = _position()
        me, sibling = (x, y, c), (x, y, 1 - c)
        chips = _other_chips(x, y)

        def rows(px, py, pc):
            return gath_ref.at[pl.ds(_slot(px, py, pc) * R, R), :]

        def copy(k, block, to, src=None):
            return pltpu.make_async_remote_copy(
                src_ref=rows(*block) if src is None else src, dst_ref=rows(*block),
                send_sem=send_sems.at[k], recv_sem=recv_sems.at[k], device_id=to, device_id_type=MESH)

        mine = pltpu.make_async_copy(x_ref, rows(*me), local_sem)
        mine.start()
        first = [copy(0, me, sibling, src=x_ref)]
        first += [copy(1 + j, me, (*chip, c), src=x_ref) for j, chip in enumerate(chips)]
        for cp in first:
            cp.start()
        passed = [copy(4 + j, (*chip, c), sibling) for j, chip in enumerate(chips)]
        for j, chip in enumerate(chips):
            copy(1 + j, (*chip, c), me).wait_recv()
            passed[j].start()
        copy(0, sibling, me).wait_recv()
        for j, chip in enumerate(chips):
            copy(4 + j, (*chip, 1 - c), me).wait_recv()
        for cp in first + passed:
            cp.wait_send()
        mine.wait()
        acc = gath_ref[0:R, :]
        for d in range(1, N_DEV):
            acc = acc + gath_ref[d * R:(d + 1) * R, :]
        sum_ref[...] = acc

    vmem = pl.BlockSpec(memory_space=pltpu.VMEM)
    body, in_specs, args = _ordered_after(body, 1, [vmem], (packed,), after)
    return pl.pallas_call(
        body, name=name, in_specs=in_specs, out_specs=vmem,
        out_shape=jax.ShapeDtypeStruct((R, L), F32),
        scratch_shapes=[pltpu.VMEM((N_DEV * R, L), F32), pltpu.SemaphoreType.DMA((7,)), pltpu.SemaphoreType.DMA((7,)),
                        pltpu.SemaphoreType.DMA],
        compiler_params=pltpu.CompilerParams(vmem_limit_bytes=VMEM_LIMIT),
    )(*args)


def _adamw_math(w, g, m, v):
    m = ADAM_B1 * m + (1.0 - ADAM_B1) * g
    v = ADAM_B2 * v + (1.0 - ADAM_B2) * (g * g)
    m_hat = m / (1.0 - ADAM_B1 ** ADAM_STEP)
    v_hat = v / (1.0 - ADAM_B2 ** ADAM_STEP)
    delta = -ADAM_LR * (m_hat / (jnp.sqrt(v_hat) + ADAM_EPS) + ADAM_WD * w)
    return delta, m, v


def _adamw_shard(w, m, v, g8, from_sibling, from_chips, pos, *, name):
    R, C = w.shape
    tr = _div(R, 256, 16)

    def body(pos_ref, w_ref, m_ref, v_ref, g_ref, s_ref, r_ref, go_ref, d_ref, mo_ref, vo_ref):
        g = g_ref[...].astype(F32) + s_ref[...].astype(F32)
        for k in range(3):
            g = g + r_ref[k].astype(F32)
        delta, m_, v_ = _adamw_math(w_ref[...], g, m_ref[...], v_ref[...])
        go_ref[...] = g
        d_ref[...] = delta
        mo_ref[...] = m_
        vo_ref[...] = v_

    blk = pl.BlockSpec((tr, C), lambda i, pos_ref: (i, 0))
    grid_spec = pltpu.PrefetchScalarGridSpec(
        num_scalar_prefetch=1, grid=(R // tr,),
        in_specs=[blk, blk, blk,
                  pl.BlockSpec((None, tr, C), lambda i, pos_ref: (pos_ref[0], i, 0)),
                  pl.BlockSpec((None, tr, C), lambda i, pos_ref: (pos_ref[1], i, 0)),
                  pl.BlockSpec((3, tr, C), lambda i, pos_ref: (0, i, 0))],
        out_specs=[blk] * 4)
    out = jax.ShapeDtypeStruct((R, C), F32)
    return pl.pallas_call(
        body, name=name, grid_spec=grid_spec, out_shape=[out] * 4,
        compiler_params=_cparams("parallel"),
    )(pos, w, m, v, g8, from_sibling, from_chips)


def _adamw_small(w, g, m, v, *, name):
    R, L = w.shape

    def body(w_ref, g_ref, m_ref, v_ref, d_ref, mo_ref, vo_ref):
        delta, m_, v_ = _adamw_math(w_ref[...], g_ref[...], m_ref[...], v_ref[...])
        d_ref[...] = delta
        mo_ref[...] = m_
        vo_ref[...] = v_

    vmem = pl.BlockSpec(memory_space=pltpu.VMEM)
    out = jax.ShapeDtypeStruct((R, L), F32)
    return pl.pallas_call(body, name=name, in_specs=[vmem] * 4, out_specs=[vmem] * 3, out_shape=[out] * 3)(w, g, m, v)


_TILE = 8 * LANES


def _pack(pieces):
    rows = []
    for p in pieces:
        flat = p.reshape(-1).astype(F32)
        padded = -(-flat.shape[0] // _TILE) * _TILE
        rows.append(jnp.pad(flat, (0, padded - flat.shape[0])).reshape(-1, LANES))
    return jnp.concatenate(rows, axis=0)


def _unpack(packed, like):
    out, r = [], 0
    for p in like:
        size = int(np.prod(p.shape)) if p.shape else 1
        nrows = -(-size // _TILE) * 8
        out.append(packed[r:r + nrows].reshape(-1)[:size].reshape(p.shape))
        r += nrows
    return out


_BIG = ("w_in", "w_a_out", "w_b_out", "w_o", "w_gate", "w_up", "w_down")
_TRANSPOSED = ("w_in", "w_gate", "w_up")
_COL_SHARDED = ("w_a_out", "w_b_out")
_GATHER_GROUPS = (("w_in",), ("w_a_out", "w_b_out", "w_o"), ("w_gate",), ("w_up",), ("w_down",))
_START_AFTER_WAIT = {0: (1, 2), 1: (3,), 2: (4,)}
_SMALL = ("norm_mix", "sgu_v_gain", "sgu_w_s", "sgu_b_s", "attn_sink", "rel_bias", "norm_ffn", "norm_final")
_ORDER = ("w_in", "norm_mix", "sgu_v_gain", "sgu_w_s", "sgu_b_s", "w_a_out", "attn_sink", "rel_bias", "w_b_out", "w_o",
          "norm_ffn", "w_gate", "w_up", "w_down", "norm_final")


def _shard(name, a):
    return jnp.swapaxes(a, 1, 2)[0] if name in _TRANSPOSED else a[0]


def _unshard(name, a):
    return jnp.swapaxes(a[None], 1, 2) if name in _TRANSPOSED else a[None]


def _whole(name, gathered):
    _, r, c = gathered.shape
    return gathered if name in _COL_SHARDED else gathered.reshape(N_DEV * r, c)


def _blocks(name, grad):
    if name in _COL_SHARDED:
        return grad
    r, c = grad.shape
    return grad.reshape(N_DEV, r // N_DEV, c)


def kernel(x, w_in, norm_mix, sgu_v_gain, sgu_w_s, sgu_b_s, w_a_out, attn_sink, rel_bias, w_b_out, w_o, norm_ffn, w_gate, w_up, w_down, norm_final, loss_target, m_w_in, m_norm_mix, m_sgu_v_gain, m_sgu_w_s, m_sgu_b_s, m_w_a_out, m_attn_sink, m_rel_bias, m_w_b_out, m_w_o, m_norm_ffn, m_w_gate, m_w_up, m_w_down, m_norm_final, v_w_in, v_norm_mix, v_sgu_v_gain, v_sgu_w_s, v_sgu_b_s, v_w_a_out, v_attn_sink, v_rel_bias, v_w_b_out, v_w_o, v_norm_ffn, v_w_gate, v_w_up, v_w_down, v_norm_final):
    w = dict(w_in=w_in, norm_mix=norm_mix, sgu_v_gain=sgu_v_gain, sgu_w_s=sgu_w_s, sgu_b_s=sgu_b_s, w_a_out=w_a_out,
             attn_sink=attn_sink, rel_bias=rel_bias, w_b_out=w_b_out, w_o=w_o, norm_ffn=norm_ffn, w_gate=w_gate,
             w_up=w_up, w_down=w_down, norm_final=norm_final)
    m = dict(w_in=m_w_in, norm_mix=m_norm_mix, sgu_v_gain=m_sgu_v_gain, sgu_w_s=m_sgu_w_s, sgu_b_s=m_sgu_b_s,
             w_a_out=m_w_a_out, attn_sink=m_attn_sink, rel_bias=m_rel_bias, w_b_out=m_w_b_out, w_o=m_w_o,
             norm_ffn=m_norm_ffn, w_gate=m_w_gate, w_up=m_w_up, w_down=m_w_down, norm_final=m_norm_final)
    v = dict(w_in=v_w_in, norm_mix=v_norm_mix, sgu_v_gain=v_sgu_v_gain, sgu_w_s=v_sgu_w_s, sgu_b_s=v_sgu_b_s,
             w_a_out=v_w_a_out, attn_sink=v_attn_sink, rel_bias=v_rel_bias, w_b_out=v_w_b_out, w_o=v_w_o,
             norm_ffn=v_norm_ffn, w_gate=v_w_gate, w_up=v_w_up, w_down=v_w_down, norm_final=v_norm_final)
    xc, yc, cc = _position()
    pos = jnp.stack([_slot(xc, yc, cc), 2 * xc + yc, cc]).astype(jnp.int32)

    in_flight, full = {}, {}

    def start_gather(groups, after):
        names = [n for gi in groups for n in _GATHER_GROUPS[gi]]
        buffers = [_own_slot(_shard(n, w[n]), pos, name="own_slot_" + n, after=after) for n in names]
        flights, token = _ag_start(buffers, [[names.index(n) for n in _GATHER_GROUPS[gi]] for gi in groups],
                                   name="ag_start_%d" % groups[0])
        in_flight.update(zip(groups, flights))
        return token

    def weight(name, after):
        if name not in full:
            gi = next(i for i, grp in enumerate(_GATHER_GROUPS) if name in grp)
            send_sems, recv_sems, lands = in_flight[gi]
            lands, token = _ag_wait(send_sems, recv_sems, lands, after, name="ag_wait_%d" % gi)
            started = start_gather(_START_AFTER_WAIT[gi], token) if gi in _START_AFTER_WAIT else None
            gathered = _ag_forward(lands, name="ag_forward_%d" % gi, after=started)
            full.update({n: _whole(n, g) for n, g in zip(_GATHER_GROUPS[gi], gathered)})
        return full[name]

    start_gather((0,), None)

    to_sibling, reducing = [], {}

    def emit(names, grads):
        g8 = [_blocks(n, g) for n, g in zip(names, grads)]
        send_sems, recv_sems, g8, lands, token = _copies_start(_sibling_copies, 4, g8, name="rs_sibling_start_" + names[0])
        to_sibling.append((names, send_sems, recv_sems, g8, lands))
        return token

    def flush(after):
        names, send_sems, recv_sems, g8, lands = to_sibling.pop()
        g8, from_sibling = _copies_wait(_sibling_copies, send_sems, recv_sems, g8, lands, after,
                                        name="rs_sibling_wait_" + names[0])
        sums4 = [_chip_sums(g, s, pos, name="chip_sums_" + n) for n, g, s in zip(names, g8, from_sibling)]
        send_sems, recv_sems, sums4, lands, token = _copies_start(_chip_copies, 3, sums4, name="rs_chips_start_" + names[0])
        reducing[names] = (g8, from_sibling, send_sems, recv_sems, sums4, lands)
        return token

    loss, grad_x, small_grads_local = _local_step(
        x[0], loss_target[0], weight, emit, flush, norm_mix, sgu_v_gain, sgu_w_s[0], sgu_b_s[0], attn_sink, rel_bias,
        norm_ffn, norm_final[None])

    out_g, out_d, out_m, out_v = {}, {}, {}, {}
    small_like = [w[n] for n in _SMALL]
    small_w = _pack(small_like)
    packed = _pack([small_grads_local[n] for n in _SMALL] + [loss[0, 0]])
    after = grad_x
    for gi, (names, (g8, from_sibling, send_sems, recv_sems, sums4, lands)) in enumerate(reducing.items()):
        if gi == len(reducing) - 1:
            summed = _small_all_reduce(packed, after, name="small_all_reduce")
            after = summed
        _, from_chips = _copies_wait(_chip_copies, send_sems, recv_sems, sums4, lands, after,
                                     name="rs_chips_wait_" + names[0])
        for i, n in enumerate(names):
            g, d, m_, v_ = _adamw_shard(_shard(n, w[n]), _shard(n, m[n]), _shard(n, v[n]), g8[i], from_sibling[i],
                                        from_chips[i], pos, name="adamw_" + n)
            out_g[n], out_d[n], out_m[n], out_v[n] = (_unshard(n, o) for o in (g, d, m_, v_))
            after = d
    *small_grads, loss_sum = _unpack(summed, small_like + [jax.ShapeDtypeStruct((), F32)])
    d_s, m_s, v_s = _adamw_small(small_w, summed[:small_w.shape[0]], _pack([m[n] for n in _SMALL]),
                                 _pack([v[n] for n in _SMALL]), name="adamw_small")
    for n, g, d, m_, v_ in zip(_SMALL, small_grads, _unpack(d_s, small_like), _unpack(m_s, small_like), _unpack(v_s, small_like)):
        out_g[n], out_d[n], out_m[n], out_v[n] = g, d, m_, v_

    return (loss_sum, grad_x[None], *[out_g[n] for n in _ORDER], *[out_d[n] for n in _ORDER],
            *[out_m[n] for n in _ORDER], *[out_v[n] for n in _ORDER])
```

```python
import functools
import math

import numpy as np
import jax
import jax.numpy as jnp
from jax import lax
from jax.experimental import pallas as pl
from jax.experimental.pallas import tpu as pltpu

F32 = jnp.float32
BF16 = jnp.bfloat16

EPS = 1e-6
NEG = -1e30
HEAD_DIM = 128
BLOCK = 128
N_KV_HEADS = 2
KV_WIDTH = N_KV_HEADS * HEAD_DIM
REL_BUCKETS = 32
REL_MAX_DIST = 128

ADAM_LR = 0.001
ADAM_B1 = 0.9
ADAM_B2 = 0.999
ADAM_EPS = 1e-08
ADAM_WD = 0.01
ADAM_STEP = 10

N_DEV = 8
LANES = 128
VMEM_LIMIT = 56 * 1024 * 1024
MESH = pl.DeviceIdType.MESH


def _cparams(*sem):
    return pltpu.CompilerParams(dimension_semantics=sem, vmem_limit_bytes=VMEM_LIMIT)


def _div(n, target, mult=LANES):
    best = None
    for d in range(mult, min(n, target) + 1, mult):
        if n % d == 0:
            best = d
    assert best is not None, (n, target, mult)
    return best


_ANY = pl.BlockSpec(memory_space=pl.ANY)


def _ordered_after(body, n_inputs, in_specs, args, after):
    if after is None:
        return body, in_specs, args

    def wrapped(*refs):
        return body(*refs[:n_inputs], *refs[n_inputs + 1:])

    return wrapped, list(in_specs) + [_ANY], tuple(args) + (after,)


def _bucket_map():
    nb = REL_BUCKETS // 2
    qi = np.arange(BLOCK)[:, None]
    kj = np.arange(3 * BLOCK)[None, :]
    rel = kj - BLOCK - qi
    ret = np.where(rel > 0, nb, 0)
    n = np.abs(rel)
    max_exact = nb // 2
    nf = np.maximum(n, 1).astype(np.float32)
    large = max_exact + (np.log(nf / np.float32(max_exact)) / np.float32(math.log(REL_MAX_DIST / max_exact))
                         * np.float32(nb - max_exact)).astype(np.int32)
    large = np.minimum(large, nb - 1)
    return (ret + np.where(n < max_exact, n, large)).astype(np.int32)


_GELU_C = math.sqrt(2.0 / math.pi)
_GELU_A = 0.044715


def _gelu(x):
    t = jnp.tanh(_GELU_C * (x + _GELU_A * (x * x * x)))
    return 0.5 * x * (1.0 + t)


def _gelu_and_grad(x):
    x2 = x * x
    t = jnp.tanh(_GELU_C * (x + _GELU_A * (x2 * x)))
    g = 0.5 * x * (1.0 + t)
    dg = 0.5 * (1.0 + t) + 0.5 * x * (1.0 - t * t) * (_GELU_C * (1.0 + 3.0 * _GELU_A * x2))
    return g, dg


def _sigmoid(x):
    return 1.0 / (1.0 + jnp.exp(-x))


def _mm(a, b, *, name, ta=False, tb=False, add=None, out_dtype=F32, bm=1024, bn=1024, bk=None, after=None,
        row_blocks=None, into=None):
    if ta:
        K, M = a.shape
    else:
        M, K = a.shape
    N = b.shape[0] if tb else b.shape[1]
    assert (b.shape[1] if tb else b.shape[0]) == K
    bm = _div(M, bm)
    bn = _div(N, bn)
    bk = K if bk is None else _div(K, bk)
    nk = K // bk
    i0, ni = (0, M // bm) if row_blocks is None else row_blocks
    a_spec = (pl.BlockSpec((bk, bm), lambda i, j, k: (k, i + i0)) if ta
              else pl.BlockSpec((bm, bk), lambda i, j, k: (i + i0, k)))
    b_spec = pl.BlockSpec((bn, bk), lambda i, j, k: (j, k)) if tb else pl.BlockSpec((bk, bn), lambda i, j, k: (k, j))
    o_spec = pl.BlockSpec((bm, bn), lambda i, j, k: (i + i0, j))
    dims = (((0 if ta else 1,), (1 if tb else 0,)), ((), ()))
    has_add = add is not None

    def body(*refs):
        if has_add:
            a_ref, b_ref, add_ref, o_ref, *scratch = refs
        else:
            a_ref, b_ref, o_ref, *scratch = refs
            add_ref = None
        p = lax.dot_general(a_ref[...].astype(BF16), b_ref[...].astype(BF16), dims, preferred_element_type=F32)
        if nk == 1:
            if has_add:
                p = p + add_ref[...]
            o_ref[...] = p.astype(out_dtype)
        else:
            acc = scratch[0]
            k = pl.program_id(2)

            @pl.when(k == 0)
            def _():
                acc[...] = p

            @pl.when(k > 0)
            def _():
                acc[...] += p

            @pl.when(k == nk - 1)
            def _():
                r = acc[...]
                if has_add:
                    r = r + add_ref[...]
                o_ref[...] = r.astype(out_dtype)

    in_specs = [a_spec, b_spec] + ([o_spec] if has_add else [])
    args = (a, b) + ((add,) if has_add else ())
    aliases = {}
    if into is not None:
        body, in_specs, args = _ordered_after(body, len(args), in_specs, args, into)
        aliases = {len(args) - 1: 0}
    body, in_specs, args = _ordered_after(body, len(args), in_specs, args, after)
    return pl.pallas_call(
        body, name=name, grid=(ni, N // bn, nk),
        in_specs=in_specs, out_specs=o_spec,
        out_shape=jax.ShapeDtypeStruct((M, N), out_dtype),
        input_output_aliases=aliases,
        scratch_shapes=[pltpu.VMEM((bm, bn), F32)] if nk > 1 else [],
        compiler_params=_cparams("parallel", "parallel", "arbitrary"),
    )(*args)


def _blocks_per_tile(c):
    nb = 1
    while (nb * c) % LANES or (nb * c < 1024 and nb < N_DEV):
        nb *= 2
    assert nb <= N_DEV and (nb * c) % LANES == 0, c
    return nb


def _mm_w8(a, w8, *, name, bm=1024, out_dtype=F32):
    M, K = a.shape
    _, _, c = w8.shape
    nb = _blocks_per_tile(c)
    bm = _div(M, bm)

    def body(a_ref, w_ref, o_ref):
        a_ = a_ref[...]
        for t in range(nb):
            o_ref[:, t * c:(t + 1) * c] = jnp.dot(a_, w_ref[t], preferred_element_type=F32).astype(out_dtype)

    return pl.pallas_call(
        body, name=name, grid=(M // bm, N_DEV // nb),
        in_specs=[pl.BlockSpec((bm, K), lambda i, j: (i, 0)), pl.BlockSpec((nb, K, c), lambda i, j: (j, 0, 0))],
        out_specs=pl.BlockSpec((bm, nb * c), lambda i, j: (i, j)),
        out_shape=jax.ShapeDtypeStruct((M, N_DEV * c), out_dtype),
        compiler_params=_cparams("parallel", "parallel"),
    )(a, w8)


def _mm_w8t(dy, w8, *, name, add=None, out_dtype=F32, bm=1024, bn=1024, after=None, lead=None):
    M = dy.shape[-2]
    _, K, c = w8.shape
    nb = _blocks_per_tile(c)
    nk = N_DEV // nb
    bm, bn = _div(M, bm), _div(K, bn)
    has_add = add is not None
    dims = (((1,), (1,)), ((), ()))

    def body(*refs):
        if has_add:
            dy_ref, w_ref, add_ref, o_ref, acc = refs
        else:
            dy_ref, w_ref, o_ref, acc = refs
        p = lax.dot_general(dy_ref[:, 0:c], w_ref[0], dims, preferred_element_type=F32)
        for t in range(1, nb):
            p = p + lax.dot_general(dy_ref[:, t * c:(t + 1) * c], w_ref[t], dims, preferred_element_type=F32)
        k = pl.program_id(2)

        @pl.when(k == 0)
        def _():
            acc[...] = p

        @pl.when(k > 0)
        def _():
            acc[...] += p

        @pl.when(k == nk - 1)
        def _():
            r = acc[...]
            if has_add:
                r = r + add_ref[...]
            o_ref[...] = r.astype(out_dtype)

    o_spec = pl.BlockSpec((bm, bn), lambda i, j, k: (i, j))
    dy_spec = (pl.BlockSpec((bm, nb * c), lambda i, j, k: (i, k)) if lead is None
               else pl.BlockSpec((None, bm, nb * c), lambda i, j, k: (lead, i, k)))
    in_specs = [dy_spec, pl.BlockSpec((nb, bn, c), lambda i, j, k: (k, j, 0))]
    in_specs += [o_spec] if has_add else []
    args = (dy, w8) + ((add,) if has_add else ())
    body, in_specs, args = _ordered_after(body, len(args), in_specs, args, after)
    return pl.pallas_call(
        body, name=name, grid=(M // bm, K // bn, nk),
        in_specs=in_specs, out_specs=o_spec,
        out_shape=jax.ShapeDtypeStruct((M, K), out_dtype),
        scratch_shapes=[pltpu.VMEM((bm, bn), F32)],
        compiler_params=_cparams("parallel", "parallel", "arbitrary"),
    )(*args)


def _mm_gw8(x, dy, c, *, name, bk=1024, lead=None):
    T, K = x.shape
    nb = _blocks_per_tile(c)
    bk = _div(K, bk)
    dims = (((0,), (0,)), ((), ()))

    def body(x_ref, dy_ref, o_ref):
        x_ = x_ref[...]
        for t in range(nb):
            o_ref[t] = lax.dot_general(x_, dy_ref[:, t * c:(t + 1) * c], dims, preferred_element_type=F32).astype(BF16)

    dy_spec = (pl.BlockSpec((T, nb * c), lambda i, j: (0, j)) if lead is None
               else pl.BlockSpec((None, T, nb * c), lambda i, j: (lead, 0, j)))
    return pl.pallas_call(
        body, name=name, grid=(K // bk, N_DEV // nb),
        in_specs=[pl.BlockSpec((T, bk), lambda i, j: (0, i)), dy_spec],
        out_specs=pl.BlockSpec((nb, bk, c), lambda i, j: (j, i, 0)),
        out_shape=jax.ShapeDtypeStruct((N_DEV, K, c), BF16),
        compiler_params=_cparams("parallel", "parallel"),
    )(x, dy)


def _rms_fwd(x, g, *, name):
    T, D = x.shape
    tm = _div(T, 256, 8)

    def body(x_ref, g_ref, h_ref):
        xf = x_ref[...]
        r = lax.rsqrt(jnp.mean(xf * xf, axis=-1, keepdims=True) + EPS)
        h_ref[...] = ((xf * r) * g_ref[...]).astype(BF16)

    return pl.pallas_call(
        body, name=name, grid=(T // tm,),
        in_specs=[pl.BlockSpec((tm, D), lambda i: (i, 0)), pl.BlockSpec((1, D), lambda i: (0, 0))],
        out_specs=pl.BlockSpec((tm, D), lambda i: (i, 0)),
        out_shape=jax.ShapeDtypeStruct((T, D), BF16),
        compiler_params=_cparams("parallel"),
    )(x, g)


def _rms_bwd(x, g, dh, dres, *, name, want_bf16, after=None):
    T, D = x.shape
    tm = _div(T, 256, 8)

    def body(x_ref, g_ref, dh_ref, dres_ref, dx_ref, *rest):
        if want_bf16:
            dxb_ref, dg_ref = rest
        else:
            (dg_ref,) = rest
        xf = x_ref[...]
        r = lax.rsqrt(jnp.mean(xf * xf, axis=-1, keepdims=True) + EPS)
        xhat = xf * r
        dh_ = dh_ref[...]
        dy = dh_ * g_ref[...]
        dx = dres_ref[...] + r * (dy - xhat * jnp.mean(dy * xhat, axis=-1, keepdims=True))
        dx_ref[...] = dx
        if want_bf16:
            dxb_ref[...] = dx.astype(BF16)
        part = jnp.sum(dh_ * xhat, axis=0, keepdims=True)

        @pl.when(pl.program_id(0) == 0)
        def _():
            dg_ref[...] = part

        @pl.when(pl.program_id(0) > 0)
        def _():
            dg_ref[...] += part

    row = pl.BlockSpec((tm, D), lambda i: (i, 0))
    vec = pl.BlockSpec((1, D), lambda i: (0, 0))
    out_specs = [row] + ([row] if want_bf16 else []) + [vec]
    out_shape = ([jax.ShapeDtypeStruct((T, D), F32)] + ([jax.ShapeDtypeStruct((T, D), BF16)] if want_bf16 else [])
                 + [jax.ShapeDtypeStruct((1, D), F32)])
    body, in_specs, args = _ordered_after(body, 4, [row, vec, row, row], (x, g, dh, dres), after)
    return pl.pallas_call(
        body, name=name, grid=(T // tm,),
        in_specs=in_specs, out_specs=out_specs, out_shape=out_shape,
        compiler_params=_cparams("arbitrary"),
    )(*args)


def _loss_head(x, g, target, *, name):
    T, D = x.shape
    tm = _div(T, 256, 8)

    def body(x_ref, g_ref, t_ref, loss_ref, dx_ref, dxb_ref, dg_ref):
        xf = x_ref[...]
        r = lax.rsqrt(jnp.mean(xf * xf, axis=-1, keepdims=True) + EPS)
        xhat = xf * r
        gain = g_ref[...]
        err = xhat * gain - t_ref[...]
        lpart = 0.5 * jnp.sum(jnp.mean(err * err, axis=-1, keepdims=True), axis=0, keepdims=True)
        dh_ = err * (1.0 / D)
        dy = dh_ * gain
        dx = r * (dy - xhat * jnp.mean(dy * xhat, axis=-1, keepdims=True))
        dx_ref[...] = dx
        dxb_ref[...] = dx.astype(BF16)
        part = jnp.sum(dh_ * xhat, axis=0, keepdims=True)

        @pl.when(pl.program_id(0) == 0)
        def _():
            dg_ref[...] = part
            loss_ref[...] = jnp.broadcast_to(lpart, loss_ref.shape)

        @pl.when(pl.program_id(0) > 0)
        def _():
            dg_ref[...] += part
            loss_ref[...] += jnp.broadcast_to(lpart, loss_ref.shape)

    row = pl.BlockSpec((tm, D), lambda i: (i, 0))
    vec = pl.BlockSpec((1, D), lambda i: (0, 0))
    return pl.pallas_call(
        body, name=name, grid=(T // tm,),
        in_specs=[row, vec, row],
        out_specs=[pl.BlockSpec((8, LANES), lambda i: (0, 0)), row, row, vec],
        out_shape=[jax.ShapeDtypeStruct((8, LANES), F32), jax.ShapeDtypeStruct((T, D), F32),
                   jax.ShapeDtypeStruct((T, D), BF16), jax.ShapeDtypeStruct((1, D), F32)],
        compiler_params=_cparams("arbitrary"),
    )(x, g, target)


def _gate_cols(D):
    off_a = 3 * D // 2 + 2 * KV_WIDTH
    off_b = off_a + D
    cw = math.gcd(math.gcd(off_a, off_b), math.gcd(D, 512))
    return cw, off_a // cw, off_b // cw


def _merge_fwd(z, ya, yb, *, name):
    T, D = ya.shape
    cw, ba, bb = _gate_cols(D)
    tm = _div(T, 512, 8)

    def body(ga_ref, gb_ref, ya_ref, yb_ref, m_ref):
        m_ref[...] = (_sigmoid(ga_ref[...]) * ya_ref[...] + _sigmoid(gb_ref[...]) * yb_ref[...]).astype(BF16)

    blk = pl.BlockSpec((tm, cw), lambda i, j: (i, j))
    return pl.pallas_call(
        body, name=name, grid=(T // tm, D // cw),
        in_specs=[pl.BlockSpec((tm, cw), lambda i, j: (i, ba + j)), pl.BlockSpec((tm, cw), lambda i, j: (i, bb + j)), blk, blk],
        out_specs=blk, out_shape=jax.ShapeDtypeStruct((T, D), BF16),
        compiler_params=_cparams("parallel", "parallel"),
    )(z, z, ya, yb)


def _merge_bwd(z, ya, yb, dm, *, name, after=None):
    T, D = ya.shape
    cw, ba, bb = _gate_cols(D)
    nj = D // cw
    assert bb == ba + nj
    tm = _div(T, 512, 8)

    def body(g_ref, ya_ref, yb_ref, dm_ref, dy_ref, dz_ref):
        sig = _sigmoid(g_ref[...])
        dm_ = dm_ref[...]
        y = jnp.where(pl.program_id(1) == 0, ya_ref[...], yb_ref[...])
        dy_ref[...] = (dm_ * sig).astype(BF16)
        dz_ref[...] = (dm_ * y * (sig * (1.0 - sig))).astype(BF16)

    in_specs = [pl.BlockSpec((tm, cw), lambda i, s, j: (i, ba + s * nj + j)),
                pl.BlockSpec((tm, cw), lambda i, s, j: (i, j * (1 - s))),
                pl.BlockSpec((tm, cw), lambda i, s, j: (i, j * s)),
                pl.BlockSpec((tm, cw), lambda i, s, j: (i, j))]
    body, in_specs, args = _ordered_after(body, 4, in_specs, (z, ya, yb, dm), after)
    return pl.pallas_call(
        body, name=name, grid=(T // tm, 2, nj),
        in_specs=in_specs,
        out_specs=[pl.BlockSpec((None, tm, cw), lambda i, s, j: (s, i, j)),
                   pl.BlockSpec((tm, cw), lambda i, s, j: (i, ba + s * nj + j))],
        out_shape=[jax.ShapeDtypeStruct((2, T, D), BF16), jax.ShapeDtypeStruct(z.shape, BF16)],
        compiler_params=_cparams("parallel", "arbitrary", "arbitrary"),
    )(*args)


def _swiglu_mm_fwd(h, wu_t, gate, *, name, bm=1024, bn=512):
    T, D = h.shape
    F = wu_t.shape[0]
    bm, bn = _div(T, bm), _div(F, bn)

    rc = _div(bm, 256, 16)

    def body(h_ref, wu_ref, gin_ref, g_ref, u_ref, act_ref):
        w = wu_ref[...]
        for r in range(0, bm, rc):
            rows = slice(r, r + rc)
            u = lax.dot_general(h_ref[rows, :], w, (((1,), (1,)), ((), ())), preferred_element_type=F32)
            g = gin_ref[rows, :]
            g_ref[rows, :] = g.astype(BF16)
            u_ref[rows, :] = u.astype(BF16)
            act_ref[rows, :] = (g * _sigmoid(g) * u).astype(BF16)

    o_spec = pl.BlockSpec((bm, bn), lambda i, j: (i, j))
    return pl.pallas_call(
        body, name=name, grid=(T // bm, F // bn),
        in_specs=[pl.BlockSpec((bm, D), lambda i, j: (i, 0)), pl.BlockSpec((bn, D), lambda i, j: (j, 0)), o_spec],
        out_specs=[o_spec] * 3, out_shape=[jax.ShapeDtypeStruct((T, F), BF16)] * 3,
        compiler_params=_cparams("parallel", "parallel"),
    )(h, wu_t, gate)


def _swiglu_mm_bwd(dx, w_down, gate, up, *, name, bm=2048, bn=512, after=None):
    T, D = dx.shape
    F = w_down.shape[0]
    bm, bn = _div(T, bm), _div(F, bn)
    dims = (((1,), (1,)), ((), ()))

    rc = _div(bm, 256, 16)

    def body(dx_ref, w_ref, g_ref, u_ref, dg_ref, du_ref):
        w = w_ref[...]
        for r in range(0, bm, rc):
            rows = slice(r, r + rc)
            d = lax.dot_general(dx_ref[rows, :], w, dims, preferred_element_type=F32)
            g = g_ref[rows, :].astype(F32)
            s = _sigmoid(g)
            silu = g * s
            dg_ref[rows, :] = (d * u_ref[rows, :].astype(F32) * (s + silu * (1.0 - s))).astype(BF16)
            du_ref[rows, :] = (d * silu).astype(BF16)

    o_spec = pl.BlockSpec((bm, bn), lambda i, j: (i, j))
    in_specs = [pl.BlockSpec((bm, D), lambda i, j: (i, 0)), pl.BlockSpec((bn, D), lambda i, j: (j, 0)), o_spec, o_spec]
    body, in_specs, args = _ordered_after(body, 4, in_specs, (dx, w_down, gate, up), after)
    out = jax.ShapeDtypeStruct((T, F), BF16)
    return pl.pallas_call(
        body, name=name, grid=(T // bm, F // bn), in_specs=in_specs, out_specs=[o_spec, o_spec], out_shape=[out, out],
        compiler_params=_cparams("parallel", "parallel"),
    )(*args)


def _sgu_fwd(z, gain, ws_b, bs_t, *, name):
    T = z.shape[0]
    SW = gain.shape[1]
    G = SW // BLOCK

    def body(zu_ref, zv_ref, gain_ref, ws_ref, bs_ref, a_ref):
        u = _gelu(zu_ref[...])
        vg = _gelu(zv_ref[...])
        r = lax.rsqrt(jnp.mean(vg * vg, axis=-1, keepdims=True) + EPS)
        vn = ((vg * r) * gain_ref[...]).astype(BF16)
        for g in range(G):
            sl = slice(g * BLOCK, (g + 1) * BLOCK)
            mixed = jnp.dot(ws_ref[g], vn[:, sl], preferred_element_type=F32) + bs_ref[:, g:g + 1]
            a_ref[:, sl] = (u[:, sl] * mixed).astype(BF16)

    return pl.pallas_call(
        body, name=name, grid=(T // BLOCK,),
        in_specs=[pl.BlockSpec((BLOCK, SW), lambda c: (c, 0)), pl.BlockSpec((BLOCK, SW), lambda c: (c, 1)),
                  pl.BlockSpec((1, SW), lambda c: (0, 0)), pl.BlockSpec((G, BLOCK, BLOCK), lambda c: (0, 0, 0)),
                  pl.BlockSpec((BLOCK, G), lambda c: (0, 0))],
        out_specs=pl.BlockSpec((BLOCK, SW), lambda c: (c, 0)),
        out_shape=jax.ShapeDtypeStruct((T, SW), BF16),
        compiler_params=_cparams("parallel"),
    )(z, z, gain, ws_b, bs_t)


def _sgu_bwd(z, gain, ws_b, bs_t, da, dz, *, name):
    T = z.shape[0]
    SW = gain.shape[1]
    G = SW // BLOCK

    def body(zu_ref, zv_ref, gain_ref, ws_ref, bs_ref, da_ref, dz_in_ref, dz_ref, dws_ref, dbs_ref, dgain_ref, dvn_ref):
        first = pl.program_id(0) == 0

        @pl.when(first)
        def _():
            dws_ref[...] = jnp.zeros_like(dws_ref)
            dbs_ref[...] = jnp.zeros_like(dbs_ref)
            dgain_ref[...] = jnp.zeros_like(dgain_ref)

        u, du = _gelu_and_grad(zu_ref[...])
        vg, dvg = _gelu_and_grad(zv_ref[...])
        r = lax.rsqrt(jnp.mean(vg * vg, axis=-1, keepdims=True) + EPS)
        xhat = vg * r
        gain_ = gain_ref[...]
        vn = (xhat * gain_).astype(BF16)
        da_ = da_ref[...]
        for g in range(G):
            sl = slice(g * BLOCK, (g + 1) * BLOCK)
            w = ws_ref[g]
            mixed = jnp.dot(w, vn[:, sl], preferred_element_type=F32) + bs_ref[:, g:g + 1]
            dmix = da_[:, sl] * u[:, sl]
            dz_ref[:, sl] = (da_[:, sl] * mixed * du[:, sl]).astype(BF16)
            dmb = dmix.astype(BF16)
            dws_ref[g] += lax.dot_general(dmb, vn[:, sl], (((1,), (1,)), ((), ())), preferred_element_type=F32)
            dbs_ref[:, g:g + 1] += jnp.sum(dmix, axis=-1, keepdims=True)
            dvn_ref[:, sl] = lax.dot_general(w, dmb, (((0,), (0,)), ((), ())), preferred_element_type=F32)
        dvn = dvn_ref[...]
        dgain_ref[...] += jnp.sum(dvn * xhat, axis=0, keepdims=True)
        dy = dvn * gain_
        dv_ = r * (dy - xhat * jnp.mean(dy * xhat, axis=-1, keepdims=True))
        dz_ref[:, SW:] = (dv_ * dvg).astype(BF16)

    row = pl.BlockSpec((BLOCK, SW), lambda c: (c, 0))
    return pl.pallas_call(
        body, name=name, grid=(T // BLOCK,),
        in_specs=[row, pl.BlockSpec((BLOCK, SW), lambda c: (c, 1)),
                  pl.BlockSpec((1, SW), lambda c: (0, 0)), pl.BlockSpec((G, BLOCK, BLOCK), lambda c: (0, 0, 0)),
                  pl.BlockSpec((BLOCK, G), lambda c: (0, 0)), row, _ANY],
        out_specs=[pl.BlockSpec((BLOCK, 2 * SW), lambda c: (c, 0)), pl.BlockSpec((G, BLOCK, BLOCK), lambda c: (0, 0, 0)),
                   pl.BlockSpec((BLOCK, G), lambda c: (0, 0)), pl.BlockSpec((1, SW), lambda c: (0, 0))],
        out_shape=[jax.ShapeDtypeStruct(dz.shape, dz.dtype),
                   jax.ShapeDtypeStruct((G, BLOCK, BLOCK), F32), jax.ShapeDtypeStruct((BLOCK, G), F32),
                   jax.ShapeDtypeStruct((1, SW), F32)],
        input_output_aliases={6: 0},
        scratch_shapes=[pltpu.VMEM((BLOCK, SW), F32)],
        compiler_params=_cparams("arbitrary"),
    )(z, z, gain, ws_b, bs_t, da, dz)


def _bias_table(rel_bias, bmap, *, name):
    H = rel_bias.shape[1]

    def body(rb_ref, bmap_ref, o_ref):
        bm_ = bmap_ref[...]
        for h in range(H):
            acc = jnp.zeros(bm_.shape, F32)
            for b in range(REL_BUCKETS):
                acc = jnp.where(bm_ == b, rb_ref[b, h], acc)
            o_ref[h] = acc

    return pl.pallas_call(
        body, name=name,
        in_specs=[pl.BlockSpec(memory_space=pltpu.SMEM), pl.BlockSpec(memory_space=pltpu.VMEM)],
        out_specs=pl.BlockSpec(memory_space=pltpu.VMEM),
        out_shape=jax.ShapeDtypeStruct((H, BLOCK, 3 * BLOCK), F32),
    )(rel_bias, bmap)


def _attn_probs(q_ref, kb, bias_ref, sink_ref, s_ref, n, T, group):
    H = s_ref.shape[0]
    for h in range(H):
        kv = h // group
        qh = q_ref[:, h * HEAD_DIM:(h + 1) * HEAD_DIM].astype(BF16)
        s_ref[h] = lax.dot_general(qh, kb[:, kv * HEAD_DIM:(kv + 1) * HEAD_DIM], (((1,), (1,)), ((), ())),
                                   preferred_element_type=F32)
    row = lax.broadcasted_iota(jnp.int32, (BLOCK, 3 * BLOCK), 0)
    col = lax.broadcasted_iota(jnp.int32, (BLOCK, 3 * BLOCK), 1)
    key_pos = n * BLOCK + col - BLOCK
    valid = (jnp.abs(col - BLOCK - row) <= BLOCK) & (key_pos >= 0) & (key_pos < T)
    s = s_ref[...] * (HEAD_DIM ** -0.5) + bias_ref[...]
    s = jnp.where(valid[None], s, NEG)
    sink = sink_ref[...]
    m = jnp.maximum(jnp.max(s, axis=-1, keepdims=True), sink)
    e = jnp.exp(s - m)
    es = jnp.exp(sink - m)
    inv = 1.0 / (jnp.sum(e, axis=-1, keepdims=True) + es)
    return e * inv, es * inv


def _attn_fwd(z, kpad, vpad, bias, sink, *, name):
    T = z.shape[0]
    H = bias.shape[0]
    AW = H * HEAD_DIM
    group = H // N_KV_HEADS

    def body(q_ref, k_ref, v_ref, bias_ref, sink_ref, o_ref, s_ref, p_ref):
        n = pl.program_id(0)
        start = pl.multiple_of(n * BLOCK, BLOCK)
        kb = k_ref[pl.ds(start, 3 * BLOCK), :]
        vb = v_ref[pl.ds(start, 3 * BLOCK), :]
        p, _ = _attn_probs(q_ref, kb, bias_ref, sink_ref, s_ref, n, T, group)
        p_ref[...] = p.astype(BF16)
        for h in range(H):
            kv = h // group
            o = jnp.dot(p_ref[h], vb[:, kv * HEAD_DIM:(kv + 1) * HEAD_DIM], preferred_element_type=F32)
            o_ref[:, h * HEAD_DIM:(h + 1) * HEAD_DIM] = o.astype(BF16)

    full_kv = pl.BlockSpec((T + 2 * BLOCK, KV_WIDTH), lambda n: (0, 0))
    return pl.pallas_call(
        body, name=name, grid=(T // BLOCK,),
        in_specs=[pl.BlockSpec((BLOCK, AW), lambda n: (n, 2)), full_kv, full_kv,
                  pl.BlockSpec((H, BLOCK, 3 * BLOCK), lambda n: (0, 0, 0)), pl.BlockSpec((H, 1, 1), lambda n: (0, 0, 0))],
        out_specs=pl.BlockSpec((BLOCK, AW), lambda n: (n, 0)),
        out_shape=jax.ShapeDtypeStruct((T, AW), BF16),
        scratch_shapes=[pltpu.VMEM((H, BLOCK, 3 * BLOCK), F32), pltpu.VMEM((H, BLOCK, 3 * BLOCK), BF16)],
        compiler_params=_cparams("parallel"),
    )(z, kpad, vpad, bias, sink)


def _attn_bwd(z, kpad, vpad, bias, sink, do, dz, *, name):
    T = z.shape[0]
    H = bias.shape[0]
    AW = H * HEAD_DIM
    group = H // N_KV_HEADS
    scale = HEAD_DIM ** -0.5

    def body(q_ref, k_ref, v_ref, bias_ref, sink_ref, do_ref, dz_in_ref, dq_ref, dk_ref, dv_ref, dbias_ref, dsink_ref,
             s_ref, dp_ref, p_ref, ds_ref):
        n = pl.program_id(0)

        @pl.when(n == 0)
        def _():
            dk_ref[...] = jnp.zeros_like(dk_ref)
            dv_ref[...] = jnp.zeros_like(dv_ref)
            dbias_ref[...] = jnp.zeros_like(dbias_ref)
            dsink_ref[...] = jnp.zeros_like(dsink_ref)

        start = pl.multiple_of(n * BLOCK, BLOCK)
        kb = k_ref[pl.ds(start, 3 * BLOCK), :]
        vb = v_ref[pl.ds(start, 3 * BLOCK), :]
        p, p_sink = _attn_probs(q_ref, kb, bias_ref, sink_ref, s_ref, n, T, group)
        s_ref[...] = p
        p_ref[...] = p.astype(BF16)
        for h in range(H):
            kv = h // group
            dp_ref[h] = lax.dot_general(do_ref[:, h * HEAD_DIM:(h + 1) * HEAD_DIM], vb[:, kv * HEAD_DIM:(kv + 1) * HEAD_DIM],
                                        (((1,), (1,)), ((), ())), preferred_element_type=F32)
        p = s_ref[...]
        dp = dp_ref[...]
        delta = jnp.sum(p * dp, axis=-1, keepdims=True)
        ds = p * (dp - delta)
        dbias_ref[...] += ds
        dsink_ref[...] += -(p_sink * delta)
        ds_ref[...] = ds.astype(BF16)
        for kv in range(N_KV_HEADS):
            ksl = slice(kv * HEAD_DIM, (kv + 1) * HEAD_DIM)
            dk_acc = jnp.zeros((3 * BLOCK, HEAD_DIM), F32)
            dv_acc = jnp.zeros((3 * BLOCK, HEAD_DIM), F32)
            for gi in range(group):
                h = kv * group + gi
                hsl = slice(h * HEAD_DIM, (h + 1) * HEAD_DIM)
                dsb = ds_ref[h]
                dq = jnp.dot(dsb, kb[:, ksl], preferred_element_type=F32) * scale
                dq_ref[:, hsl] = dq.astype(BF16)
                dk_acc = dk_acc + lax.dot_general(dsb, q_ref[:, hsl].astype(BF16), (((0,), (0,)), ((), ())),
                                                  preferred_element_type=F32)
                dv_acc = dv_acc + lax.dot_general(p_ref[h], do_ref[:, hsl], (((0,), (0,)), ((), ())),
                                                  preferred_element_type=F32)
            dk_ref[pl.ds(start, 3 * BLOCK), ksl] += dk_acc * scale
            dv_ref[pl.ds(start, 3 * BLOCK), ksl] += dv_acc

    full_kv = pl.BlockSpec((T + 2 * BLOCK, KV_WIDTH), lambda n: (0, 0))
    bias_spec = pl.BlockSpec((H, BLOCK, 3 * BLOCK), lambda n: (0, 0, 0))
    row = pl.BlockSpec((BLOCK, AW), lambda n: (n, 0))
    q_cols = pl.BlockSpec((BLOCK, AW), lambda n: (n, 2))
    band = (H, BLOCK, 3 * BLOCK)
    return pl.pallas_call(
        body, name=name, grid=(T // BLOCK,),
        in_specs=[q_cols, full_kv, full_kv, bias_spec, pl.BlockSpec((H, 1, 1), lambda n: (0, 0, 0)), row, _ANY],
        out_specs=[q_cols, full_kv, full_kv, bias_spec, pl.BlockSpec((H, BLOCK, 1), lambda n: (0, 0, 0))],
        out_shape=[jax.ShapeDtypeStruct(dz.shape, dz.dtype),
                   jax.ShapeDtypeStruct((T + 2 * BLOCK, KV_WIDTH), F32), jax.ShapeDtypeStruct((T + 2 * BLOCK, KV_WIDTH), F32),
                   jax.ShapeDtypeStruct(band, F32), jax.ShapeDtypeStruct((H, BLOCK, 1), F32)],
        input_output_aliases={6: 0},
        scratch_shapes=[pltpu.VMEM(band, F32), pltpu.VMEM(band, F32), pltpu.VMEM(band, BF16), pltpu.VMEM(band, BF16)],
        compiler_params=_cparams("arbitrary"),
    )(z, kpad, vpad, bias, sink, do, dz)


def _dkv_into(dkp, dvp, dz, *, name):
    T = dz.shape[0]
    D = (dz.shape[1] - 2 * KV_WIDTH) * 2 // 7
    col = (D + D // 2) // (2 * KV_WIDTH)
    assert col * 2 * KV_WIDTH == D + D // 2

    def body(dk_ref, dv_ref, dz_in_ref, o_ref):
        o_ref[:, :KV_WIDTH] = dk_ref[...].astype(BF16)
        o_ref[:, KV_WIDTH:] = dv_ref[...].astype(BF16)

    kv = pl.BlockSpec((BLOCK, KV_WIDTH), lambda n: (n + 1, 0))
    return pl.pallas_call(
        body, name=name, grid=(T // BLOCK,),
        in_specs=[kv, kv, _ANY], out_specs=pl.BlockSpec((BLOCK, 2 * KV_WIDTH), lambda n: (n, col)),
        out_shape=jax.ShapeDtypeStruct(dz.shape, dz.dtype), input_output_aliases={2: 0},
        compiler_params=_cparams("parallel"),
    )(dkp, dvp, dz)


def _kv_pad(z, *, name):
    T = z.shape[0]
    D = (z.shape[1] - 2 * KV_WIDTH) * 2 // 7
    kcol = (D + D // 2) // KV_WIDTH
    nb = T // BLOCK

    def body(k_ref, v_ref, ko_ref, vo_ref):
        b = pl.program_id(0)
        inside = (b >= 1) & (b <= nb)
        ko_ref[...] = jnp.where(inside, k_ref[...], 0.0).astype(BF16)
        vo_ref[...] = jnp.where(inside, v_ref[...], 0.0).astype(BF16)

    out = jax.ShapeDtypeStruct((T + 2 * BLOCK, KV_WIDTH), BF16)
    o_spec = pl.BlockSpec((BLOCK, KV_WIDTH), lambda b: (b, 0))
    return pl.pallas_call(
        body, name=name, grid=(nb + 2,),
        in_specs=[pl.BlockSpec((BLOCK, KV_WIDTH), lambda b: (jnp.clip(b - 1, 0, nb - 1), kcol)),
                  pl.BlockSpec((BLOCK, KV_WIDTH), lambda b: (jnp.clip(b - 1, 0, nb - 1), kcol + 1))],
        out_specs=[o_spec, o_spec], out_shape=[out, out],
        compiler_params=_cparams("parallel"),
    )(z, z)


def _attn_small_grads(dbias, dsink_rows, bmap, after, *, name):
    H = dbias.shape[0]

    def body(dbias_ref, dsink_ref, bmap_ref, drel_ref, ds_ref):
        bm_ = bmap_ref[...]
        for h in range(H):
            d = dbias_ref[h]
            for b in range(REL_BUCKETS):
                drel_ref[b, h] = jnp.sum(jnp.where(bm_ == b, d, 0.0))
            ds_ref[0, h] = jnp.sum(dsink_ref[h])

    vmem = pl.BlockSpec(memory_space=pltpu.VMEM)
    smem = pl.BlockSpec(memory_space=pltpu.SMEM)
    body, in_specs, args = _ordered_after(body, 3, [vmem, vmem, vmem], (dbias, dsink_rows, bmap), after)
    return pl.pallas_call(
        body, name=name, in_specs=in_specs, out_specs=[smem, smem],
        out_shape=[jax.ShapeDtypeStruct((REL_BUCKETS, H), F32), jax.ShapeDtypeStruct((1, H), F32)],
    )(*args)


def _local_step(x, target, weight, emit, flush, norm_mix, v_gain, w_s, b_s, sink, rel_bias, norm_ffn, norm_final):
    T, D = x.shape
    ws_b = w_s.astype(BF16)
    bs_t = b_s.T
    bmap = jnp.asarray(_bucket_map())
    sink = sink.reshape(-1, 1, 1)

    h = _rms_fwd(x, norm_mix, name="rms_mix")
    w_in = weight("w_in", h)
    z = _mm(h, w_in, tb=True, name="mm_z", bm=2048, bn=768)
    a = _sgu_fwd(z, v_gain, ws_b, bs_t, name="sgu_fwd")
    w_a = weight("w_a_out", a)
    ya = _mm_w8(a, w_a, name="mm_ya", bm=2048, out_dtype=BF16)
    kpad, vpad = _kv_pad(z, name="kv_pad")
    bias = _bias_table(rel_bias, bmap, name="bias_table")
    o = _attn_fwd(z, kpad, vpad, bias, sink, name="attn_fwd")
    w_b = weight("w_b_out", o)
    yb = _mm_w8(o, w_b, name="mm_yb", bm=2048, out_dtype=BF16)
    m = _merge_fwd(z, ya, yb, name="merge_fwd")
    w_o = weight("w_o", m)
    x1 = _mm(m, w_o, name="mm_x1", add=x, bm=2048, bn=512)
    h2 = _rms_fwd(x1, norm_ffn, name="rms_ffn")
    w_gate = weight("w_gate", h2)
    gate = _mm(h2, w_gate, tb=True, name="mm_gate", bm=2048, bn=512)
    w_up = weight("w_up", gate)
    gate, up, act = _swiglu_mm_fwd(h2, w_up, gate, name="mm_up_swiglu")
    w_down = weight("w_down", act)
    x2 = _mm(act, w_down, name="mm_x2", add=x1, bm=1024, bn=512)
    loss, dx2, dx2b, g_norm_final = _loss_head(x2, norm_final, target, name="loss_head")

    g_w_down = _mm(act, dx2b, ta=True, out_dtype=BF16, name="mm_gwdown", bm=512, bn=2048)
    tok = emit(("w_down",), (g_w_down,))
    dgate, dup = _swiglu_mm_bwd(dx2b, w_down, gate, up, name="mm_dact_swiglu", after=tok)
    tok = flush(dgate)
    g_w_gate = _mm(dgate, h2, ta=True, out_dtype=BF16, name="mm_gwgate", bm=512, bn=2048, after=tok)
    g_w_up = _mm(dup, h2, ta=True, out_dtype=BF16, name="mm_gwup", bm=512, bn=2048)
    tok = emit(("w_gate", "w_up"), (g_w_gate, g_w_up))
    dh2 = _mm(dgate, w_gate, name="mm_dh2a", bm=1024, bn=512, after=tok)
    tok = flush(dh2)
    dh2 = _mm(dup, w_up, add=dh2, name="mm_dh2b", bm=1024, bn=512, after=tok)
    dx1, dx1b, g_norm_ffn = _rms_bwd(x1, norm_ffn, dh2, dx2, name="rms_ffn_bwd", want_bf16=True)

    g_w_o = _mm(m, dx1b, ta=True, out_dtype=BF16, name="mm_gwo", bm=2048, bn=512)
    tok = emit(("w_o",), (g_w_o,))
    dm = _mm(dx1b, w_o, tb=True, name="mm_dm", bm=2048, bn=512, after=tok)
    tok = flush(dm)
    dy, dz = _merge_bwd(z, ya, yb, dm, name="merge_bwd", after=tok)
    g_w_a = _mm_gw8(a, dy, w_a.shape[2], name="mm_gwa", lead=0)
    g_w_b = _mm_gw8(o, dy, w_b.shape[2], name="mm_gwb", lead=1)
    tok = emit(("w_a_out", "w_b_out"), (g_w_a, g_w_b))
    da = _mm_w8t(dy, w_a, name="mm_da", bm=2048, bn=512, after=tok, lead=0)
    tok = flush(da)
    do = _mm_w8t(dy, w_b, out_dtype=BF16, name="mm_do", bm=2048, bn=512, after=tok, lead=1)
    dz, g_w_s, g_b_s_t, g_v_gain = _sgu_bwd(z, v_gain, ws_b, bs_t, da, dz, name="sgu_bwd")
    dz, dkp, dvp, dbias, dsink_rows = _attn_bwd(z, kpad, vpad, bias, sink, do, dz, name="attn_bwd")
    dz = _dkv_into(dkp, dvp, dz, name="dkv_into_dz")
    g_w_in = _mm(dz, h, ta=True, out_dtype=BF16, name="mm_gwin", bm=768, bn=2048)
    tok = emit(("w_in",), (g_w_in,))
    half = dict(bm=T // 2, bn=256)
    dh = _mm(dz, w_in, name="mm_dh_top", row_blocks=(0, 1), after=tok, **half)
    tok = flush(dh)
    dh = _mm(dz, w_in, name="mm_dh_bottom", row_blocks=(1, 1), into=dh, after=tok, **half)
    g_rel_bias, g_sink = _attn_small_grads(dbias, dsink_rows, bmap, dh, name="attn_small_grads")
    grad_x, g_norm_mix = _rms_bwd(x, norm_mix, dh, dx1, name="rms_mix_bwd", want_bf16=False)

    small = dict(norm_mix=g_norm_mix, sgu_v_gain=g_v_gain, sgu_w_s=g_w_s, sgu_b_s=g_b_s_t.T, attn_sink=g_sink,
                 rel_bias=g_rel_bias, norm_ffn=g_norm_ffn, norm_final=g_norm_final)
    return loss, grad_x, small


def _position():
    return lax.axis_index("x"), lax.axis_index("y"), lax.axis_index("c")


def _other_chips(x, y):
    return [(1 - x, y), (x, 1 - y), (1 - x, 1 - y)]


def _slot(px, py, pc):
    return 4 * px + 2 * py + pc


_HBM = pl.BlockSpec(memory_space=pltpu.HBM)
_SEM = pl.BlockSpec(memory_space=pltpu.SEMAPHORE)
_DATAFLOW = pltpu.SideEffectType.DATAFLOW_SIDE_EFFECTING


def _in_hbm(a):
    return pltpu.with_memory_space_constraint(a, pltpu.HBM)


def _own_slot(shard, pos, *, name, after=None):
    R, C = shard.shape
    tr = _div(R, 256, 16)

    def body(pos_ref, w_ref, o_ref):
        o_ref[...] = w_ref[...].astype(BF16)

    body, in_specs, args = _ordered_after(body, 2, [pl.BlockSpec((tr, C), lambda i, pos_ref: (i, 0))], (pos, shard), after)
    grid_spec = pltpu.PrefetchScalarGridSpec(
        num_scalar_prefetch=1, grid=(R // tr,), in_specs=in_specs,
        out_specs=pl.BlockSpec((None, tr, C), lambda i, pos_ref: (pos_ref[0], i, 0)))
    return pl.pallas_call(
        body, name=name, grid_spec=grid_spec,
        out_shape=jax.ShapeDtypeStruct((N_DEV, R, C), BF16),
        compiler_params=_cparams("parallel"),
    )(*args)


def _ag_copies(w, land_ref, send_sems, recv_sems):
    x, y, c = _position()
    mine = land_ref.at[_slot(x, y, c)]
    targets = [(px, py, c) for px, py in _other_chips(x, y)] + [(x, y, 1 - c)]
    return [pltpu.make_async_remote_copy(src_ref=mine, dst_ref=mine, send_sem=send_sems.at[4 * w + k],
                                         recv_sem=recv_sems.at[4 * w + k], device_id=to, device_id_type=MESH)
            for k, to in enumerate(targets)]


def _ag_start(buffers, groups, *, name):
    lands = [buffers[i] for g in groups for i in g]
    n, ng = len(lands), len(groups)
    sizes = [len(g) for g in groups]

    def body(*refs):
        land_refs = refs[:n]
        sems = refs[n:n + 2 * ng]
        token = refs[-1]
        i = 0
        for g in range(ng):
            for w in range(sizes[g]):
                for cp in _ag_copies(w, land_refs[i], sems[2 * g], sems[2 * g + 1]):
                    cp.start()
                i += 1
        token[...] = jnp.zeros_like(token)

    sem_shapes = [pltpu.SemaphoreType.DMA((4 * k,)) for k in sizes for _ in range(2)]
    outs = pl.pallas_call(
        body, name=name,
        in_specs=[_HBM] * n,
        out_specs=tuple([_SEM] * (2 * ng) + [_HBM] * n + [pl.BlockSpec(memory_space=pltpu.VMEM)]),
        out_shape=tuple(sem_shapes + [pltpu.HBM(a.shape, a.dtype) for a in lands] + [jax.ShapeDtypeStruct((8, LANES), F32)]),
        input_output_aliases={i: 2 * ng + i for i in range(n)},
        compiler_params=pltpu.CompilerParams(has_side_effects=_DATAFLOW),
    )(*[_in_hbm(a) for a in lands])
    sems, thru = outs[:2 * ng], outs[2 * ng:2 * ng + n]
    result, i = [], 0
    for g in range(ng):
        k = sizes[g]
        result.append((sems[2 * g], sems[2 * g + 1], list(thru[i:i + k])))
        i += k
    return result, outs[-1]


def _ag_wait(send_sems, recv_sems, lands, after, *, name):
    n = len(lands)

    def body(*refs):
        land_refs = refs[:n]
        send_ref, recv_ref = refs[n], refs[n + 1]
        token = refs[-1]
        for w in range(n):
            for cp in _ag_copies(w, land_refs[w], send_ref, recv_ref):
                cp.wait_send()
                cp.wait_recv()
        token[...] = jnp.zeros_like(token)

    outs = pl.pallas_call(
        body, name=name,
        in_specs=[_HBM] * n + [_SEM, _SEM, _ANY],
        out_specs=tuple([_HBM] * n + [pl.BlockSpec(memory_space=pltpu.VMEM)]),
        out_shape=tuple([pltpu.HBM(a.shape, a.dtype) for a in lands] + [jax.ShapeDtypeStruct((8, LANES), F32)]),
        input_output_aliases={i: i for i in range(n)},
        compiler_params=pltpu.CompilerParams(has_side_effects=_DATAFLOW),
    )(*lands, send_sems, recv_sems, after)
    return list(outs[:n]), outs[n]


def _ag_forward(lands, *, name, after=None):
    n = len(lands)

    def body(*refs):
        in_refs, out_refs = refs[:n], refs[n:2 * n]
        send_sems, recv_sems = refs[2 * n:]
        x, y, c = _position()
        copies = []
        for w in range(n):
            for k, (px, py) in enumerate(_other_chips(x, y)):
                cp = pltpu.make_async_remote_copy(
                    src_ref=in_refs[w].at[_slot(px, py, c)], dst_ref=out_refs[w].at[_slot(px, py, c)],
                    send_sem=send_sems.at[3 * w + k], recv_sem=recv_sems.at[3 * w + k],
                    device_id=(x, y, 1 - c), device_id_type=MESH)
                cp.start()
                copies.append(cp)
        for cp in copies:
            cp.wait()

    body, in_specs, args = _ordered_after(body, n, [_ANY] * n, tuple(lands), after)
    return pl.pallas_call(
        body, name=name,
        in_specs=in_specs, out_specs=[_ANY] * n,
        out_shape=[jax.ShapeDtypeStruct(a.shape, a.dtype) for a in lands],
        input_output_aliases={i: i for i in range(n)},
        scratch_shapes=[pltpu.SemaphoreType.DMA((3 * n,)), pltpu.SemaphoreType.DMA((3 * n,))],
    )(*args)


def _sibling_copies(w, g8_ref, land_ref, send_sems, recv_sems):
    x, y, c = _position()
    return [pltpu.make_async_remote_copy(src_ref=g8_ref.at[2 * p + (1 - c)], dst_ref=land_ref.at[p],
                                         send_sem=send_sems.at[4 * w + p], recv_sem=recv_sems.at[4 * w + p],
                                         device_id=(x, y, 1 - c), device_id_type=MESH)
            for p in range(4)]


def _chip_copies(w, sums_ref, land_ref, send_sems, recv_sems):
    x, y, c = _position()
    return [pltpu.make_async_remote_copy(src_ref=sums_ref.at[2 * px + py], dst_ref=land_ref.at[k],
                                         send_sem=send_sems.at[3 * w + k], recv_sem=recv_sems.at[3 * w + k],
                                         device_id=(px, py, c), device_id_type=MESH)
            for k, (px, py) in enumerate(_other_chips(x, y))]


def _copies_start(copies, per_weight, srcs, *, name):
    n = len(srcs)
    lands = [lax.empty((per_weight,) + s.shape[1:], s.dtype) for s in srcs]

    def body(*refs):
        src_refs, land_refs = refs[:n], refs[n:2 * n]
        send_sems, recv_sems = refs[2 * n], refs[2 * n + 1]
        token = refs[-1]
        for w in range(n):
            for cp in copies(w, src_refs[w], land_refs[w], send_sems, recv_sems):
                cp.start()
        token[...] = jnp.zeros_like(token)

    outs = pl.pallas_call(
        body, name=name,
        in_specs=[_HBM] * (2 * n),
        out_specs=tuple([_SEM, _SEM] + [_HBM] * (2 * n) + [pl.BlockSpec(memory_space=pltpu.VMEM)]),
        out_shape=tuple([pltpu.SemaphoreType.DMA((per_weight * n,)), pltpu.SemaphoreType.DMA((per_weight * n,))]
                        + [pltpu.HBM(a.shape, a.dtype) for a in srcs + lands] + [jax.ShapeDtypeStruct((8, LANES), F32)]),
        input_output_aliases={i: 2 + i for i in range(2 * n)},
        compiler_params=pltpu.CompilerParams(has_side_effects=_DATAFLOW),
    )(*[_in_hbm(a) for a in srcs + lands])
    return outs[0], outs[1], list(outs[2:2 + n]), list(outs[2 + n:2 + 2 * n]), outs[-1]


def _copies_wait(copies, send_sems, recv_sems, srcs, lands, after, *, name):
    n = len(srcs)

    def body(*refs):
        src_refs, land_refs = refs[:n], refs[n:2 * n]
        send_ref, recv_ref = refs[2 * n], refs[2 * n + 1]
        for w in range(n):
            for cp in copies(w, src_refs[w], land_refs[w], send_ref, recv_ref):
                cp.wait_send()
                cp.wait_recv()

    outs = pl.pallas_call(
        body, name=name,
        in_specs=[_HBM] * (2 * n) + [_SEM, _SEM, _ANY],
        out_specs=tuple([_HBM] * (2 * n)),
        out_shape=tuple(pltpu.HBM(a.shape, a.dtype) for a in srcs + lands),
        input_output_aliases={i: i for i in range(2 * n)},
        compiler_params=pltpu.CompilerParams(has_side_effects=_DATAFLOW),
    )(*srcs, *lands, send_sems, recv_sems, after)
    return list(outs[:n]), list(outs[n:])


def _chip_sums(g8, from_sibling, pos, *, name):
    _, R, C = g8.shape
    tr = _div(R, 512, 16)

    def body(pos_ref, g_ref, s_ref, o_ref):
        o_ref[...] = (g_ref[...].astype(F32) + s_ref[...].astype(F32)).astype(BF16)

    def chip(k, pos_ref):
        return jnp.where(k >= pos_ref[1], k + 1, k)

    grid_spec = pltpu.PrefetchScalarGridSpec(
        num_scalar_prefetch=1, grid=(3, R // tr),
        in_specs=[pl.BlockSpec((None, tr, C), lambda k, i, pos_ref: (2 * chip(k, pos_ref) + pos_ref[2], i, 0)),
                  pl.BlockSpec((None, tr, C), lambda k, i, pos_ref: (chip(k, pos_ref), i, 0))],
        out_specs=pl.BlockSpec((None, tr, C), lambda k, i, pos_ref: (chip(k, pos_ref), i, 0)))
    return pl.pallas_call(
        body, name=name, grid_spec=grid_spec,
        out_shape=jax.ShapeDtypeStruct((4, R, C), BF16),
        compiler_params=_cparams("parallel", "parallel"),
    )(pos, g8, from_sibling)


def _small_all_reduce(packed, after, *, name):
    R, L = packed.shape

    def body(x_ref, sum_ref, gath_ref, send_sems, recv_sems, local_sem):
        x, y, c = _position()
        me, sibling = (x, y, c), (x, y, 1 - c)
        chips = _other_chips(x, y)

        def rows(px, py, pc):
            return gath_ref.at[pl.ds(_slot(px, py, pc) * R, R), :]

        def copy(k, block, to, src=None):
            return pltpu.make_async_remote_copy(
                src_ref=rows(*block) if src is None else src, dst_ref=rows(*block),
                send_sem=send_sems.at[k], recv_sem=recv_sems.at[k], device_id=to, device_id_type=MESH)

        mine = pltpu.make_async_copy(x_ref, rows(*me), local_sem)
        mine.start()
        first = [copy(0, me, sibling, src=x_ref)]
        first += [copy(1 + j, me, (*chip, c), src=x_ref) for j, chip in enumerate(chips)]
        for cp in first:
            cp.start()
        passed = [copy(4 + j, (*chip, c), sibling) for j, chip in enumerate(chips)]
        for j, chip in enumerate(chips):
            copy(1 + j, (*chip, c), me).wait_recv()
            passed[j].start()
        copy(0, sibling, me).wait_recv()
        for j, chip in enumerate(chips):
            copy(4 + j, (*chip, 1 - c), me).wait_recv()
        for cp in first + passed:
            cp.wait_send()
        mine.wait()
        acc = gath_ref[0:R, :]
        for d in range(1, N_DEV):
            acc = acc + gath_ref[d * R:(d + 1) * R, :]
        sum_ref[...] = acc

    vmem = pl.BlockSpec(memory_space=pltpu.VMEM)
    body, in_specs, args = _ordered_after(body, 1, [vmem], (packed,), after)
    return pl.pallas_call(
        body, name=name, in_specs=in_specs, out_specs=vmem,
        out_shape=jax.ShapeDtypeStruct((R, L), F32),
        scratch_shapes=[pltpu.VMEM((N_DEV * R, L), F32), pltpu.SemaphoreType.DMA((7,)), pltpu.SemaphoreType.DMA((7,)),
                        pltpu.SemaphoreType.DMA],
        compiler_params=pltpu.CompilerParams(vmem_limit_bytes=VMEM_LIMIT),
    )(*args)


def _adamw_math(w, g, m, v):
    m = ADAM_B1 * m + (1.0 - ADAM_B1) * g
    v = ADAM_B2 * v + (1.0 - ADAM_B2) * (g * g)
    m_hat = m / (1.0 - ADAM_B1 ** ADAM_STEP)
    v_hat = v / (1.0 - ADAM_B2 ** ADAM_STEP)
    delta = -ADAM_LR * (m_hat / (jnp.sqrt(v_hat) + ADAM_EPS) + ADAM_WD * w)
    return delta, m, v


def _adamw_shard(w, m, v, g8, from_sibling, from_chips, pos, *, name):
    R, C = w.shape
    tr = _div(R, 256, 16)

    def body(pos_ref, w_ref, m_ref, v_ref, g_ref, s_ref, r_ref, go_ref, d_ref, mo_ref, vo_ref):
        g = g_ref[...].astype(F32) + s_ref[...].astype(F32)
        for k in range(3):
            g = g + r_ref[k].astype(F32)
        delta, m_, v_ = _adamw_math(w_ref[...], g, m_ref[...], v_ref[...])
        go_ref[...] = g
        d_ref[...] = delta
        mo_ref[...] = m_
        vo_ref[...] = v_

    blk = pl.BlockSpec((tr, C), lambda i, pos_ref: (i, 0))
    grid_spec = pltpu.PrefetchScalarGridSpec(
        num_scalar_prefetch=1, grid=(R // tr,),
        in_specs=[blk, blk, blk,
                  pl.BlockSpec((None, tr, C), lambda i, pos_ref: (pos_ref[0], i, 0)),
                  pl.BlockSpec((None, tr, C), lambda i, pos_ref: (pos_ref[1], i, 0)),
                  pl.BlockSpec((3, tr, C), lambda i, pos_ref: (0, i, 0))],
        out_specs=[blk] * 4)
    out = jax.ShapeDtypeStruct((R, C), F32)
    return pl.pallas_call(
        body, name=name, grid_spec=grid_spec, out_shape=[out] * 4,
        compiler_params=_cparams("parallel"),
    )(pos, w, m, v, g8, from_sibling, from_chips)


def _adamw_small(w, g, m, v, *, name):
    R, L = w.shape

    def body(w_ref, g_ref, m_ref, v_ref, d_ref, mo_ref, vo_ref):
        delta, m_, v_ = _adamw_math(w_ref[...], g_ref[...], m_ref[...], v_ref[...])
        d_ref[...] = delta
        mo_ref[...] = m_
        vo_ref[...] = v_

    vmem = pl.BlockSpec(memory_space=pltpu.VMEM)
    out = jax.ShapeDtypeStruct((R, L), F32)
    return pl.pallas_call(body, name=name, in_specs=[vmem] * 4, out_specs=[vmem] * 3, out_shape=[out] * 3)(w, g, m, v)


_TILE = 8 * LANES


def _pack(pieces):
    rows = []
    for p in pieces:
        flat = p.reshape(-1).astype(F32)
        padded = -(-flat.shape[0] // _TILE) * _TILE
        rows.append(jnp.pad(flat, (0, padded - flat.shape[0])).reshape(-1, LANES))
    return jnp.concatenate(rows, axis=0)


def _unpack(packed, like):
    out, r = [], 0
    for p in like:
        size = int(np.prod(p.shape)) if p.shape else 1
        nrows = -(-size // _TILE) * 8
        out.append(packed[r:r + nrows].reshape(-1)[:size].reshape(p.shape))
        r += nrows
    return out


_BIG = ("w_in", "w_a_out", "w_b_out", "w_o", "w_gate", "w_up", "w_down")
_TRANSPOSED = ("w_in", "w_gate", "w_up")
_COL_SHARDED = ("w_a_out", "w_b_out")
_GATHER_GROUPS = (("w_in",), ("w_a_out", "w_b_out", "w_o"), ("w_gate",), ("w_up",), ("w_down",))
_START_AFTER_WAIT = {0: (1, 2), 1: (3,), 2: (4,)}
_SMALL = ("norm_mix", "sgu_v_gain", "sgu_w_s", "sgu_b_s", "attn_sink", "rel_bias", "norm_ffn", "norm_final")
_ORDER = ("w_in", "norm_mix", "sgu_v_gain", "sgu_w_s", "sgu_b_s", "w_a_out", "attn_sink", "rel_bias", "w_b_out", "w_o",
          "norm_ffn", "w_gate", "w_up", "w_down", "norm_final")


def _shard(name, a):
    return jnp.swapaxes(a, 1, 2)[0] if name in _TRANSPOSED else a[0]


def _unshard(name, a):
    return jnp.swapaxes(a[None], 1, 2) if name in _TRANSPOSED else a[None]


def _whole(name, gathered):
    _, r, c = gathered.shape
    return gathered if name in _COL_SHARDED else gathered.reshape(N_DEV * r, c)


def _blocks(name, grad):
    if name in _COL_SHARDED:
        return grad
    r, c = grad.shape
    return grad.reshape(N_DEV, r // N_DEV, c)


def kernel(x, w_in, norm_mix, sgu_v_gain, sgu_w_s, sgu_b_s, w_a_out, attn_sink, rel_bias, w_b_out, w_o, norm_ffn, w_gate, w_up, w_down, norm_final, loss_target, m_w_in, m_norm_mix, m_sgu_v_gain, m_sgu_w_s, m_sgu_b_s, m_w_a_out, m_attn_sink, m_rel_bias, m_w_b_out, m_w_o, m_norm_ffn, m_w_gate, m_w_up, m_w_down, m_norm_final, v_w_in, v_norm_mix, v_sgu_v_gain, v_sgu_w_s, v_sgu_b_s, v_w_a_out, v_attn_sink, v_rel_bias, v_w_b_out, v_w_o, v_norm_ffn, v_w_gate, v_w_up, v_w_down, v_norm_final):
    w = dict(w_in=w_in, norm_mix=norm_mix, sgu_v_gain=sgu_v_gain, sgu_w_s=sgu_w_s, sgu_b_s=sgu_b_s, w_a_out=w_a_out,
             attn_sink=attn_sink, rel_bias=rel_bias, w_b_out=w_b_out, w_o=w_o, norm_ffn=norm_ffn, w_gate=w_gate,
             w_up=w_up, w_down=w_down, norm_final=norm_final)
    m = dict(w_in=m_w_in, norm_mix=m_norm_mix, sgu_v_gain=m_sgu_v_gain, sgu_w_s=m_sgu_w_s, sgu_b_s=m_sgu_b_s,
             w_a_out=m_w_a_out, attn_sink=m_attn_sink, rel_bias=m_rel_bias, w_b_out=m_w_b_out, w_o=m_w_o,
             norm_ffn=m_norm_ffn, w_gate=m_w_gate, w_up=m_w_up, w_down=m_w_down, norm_final=m_norm_final)
    v = dict(w_in=v_w_in, norm_mix=v_norm_mix, sgu_v_gain=v_sgu_v_gain, sgu_w_s=v_sgu_w_s, sgu_b_s=v_sgu_b_s,
             w_a_out=v_w_a_out, attn_sink=v_attn_sink, rel_bias=v_rel_bias, w_b_out=v_w_b_out, w_o=v_w_o,
             norm_ffn=v_norm_ffn, w_gate=v_w_gate, w_up=v_w_up, w_down=v_w_down, norm_final=v_norm_final)
    xc, yc, cc = _position()
    pos = jnp.stack([_slot(xc, yc, cc), 2 * xc + yc, cc]).astype(jnp.int32)

    in_flight, full = {}, {}

    def start_gather(groups, after):
        names = [n for gi in groups for n in _GATHER_GROUPS[gi]]
        buffers = [_own_slot(_shard(n, w[n]), pos, name="own_slot_" + n, after=after) for n in names]
        flights, token = _ag_start(buffers, [[names.index(n) for n in _GATHER_GROUPS[gi]] for gi in groups],
                                   name="ag_start_%d" % groups[0])
        in_flight.update(zip(groups, flights))
        return token

    def weight(name, after):
        if name not in full:
            gi = next(i for i, grp in enumerate(_GATHER_GROUPS) if name in grp)
            send_sems, recv_sems, lands = in_flight[gi]
            lands, token = _ag_wait(send_sems, recv_sems, lands, after, name="ag_wait_%d" % gi)
            started = start_gather(_START_AFTER_WAIT[gi], token) if gi in _START_AFTER_WAIT else None
            gathered = _ag_forward(lands, name="ag_forward_%d" % gi, after=started)
            full.update({n: _whole(n, g) for n, g in zip(_GATHER_GROUPS[gi], gathered)})
        return full[name]

    start_gather((0,), None)

    to_sibling, reducing = [], {}

    def emit(names, grads):
        g8 = [_blocks(n, g) for n, g in zip(names, grads)]
        send_sems, recv_sems, g8, lands, token = _copies_start(_sibling_copies, 4, g8, name="rs_sibling_start_" + names[0])
        to_sibling.append((names, send_sems, recv_sems, g8, lands))
        return token

    def flush(after):
        names, send_sems, recv_sems, g8, lands = to_sibling.pop()
        g8, from_sibling = _copies_wait(_sibling_copies, send_sems, recv_sems, g8, lands, after,
                                        name="rs_sibling_wait_" + names[0])
        sums4 = [_chip_sums(g, s, pos, name="chip_sums_" + n) for n, g, s in zip(names, g8, from_sibling)]
        send_sems, recv_sems, sums4, lands, token = _copies_start(_chip_copies, 3, sums4, name="rs_chips_start_" + names[0])
        reducing[names] = (g8, from_sibling, send_sems, recv_sems, sums4, lands)
        return token

    loss, grad_x, small_grads_local = _local_step(
        x[0], loss_target[0], weight, emit, flush, norm_mix, sgu_v_gain, sgu_w_s[0], sgu_b_s[0], attn_sink, rel_bias,
        norm_ffn, norm_final[None])

    out_g, out_d, out_m, out_v = {}, {}, {}, {}
    small_like = [w[n] for n in _SMALL]
    small_w = _pack(small_like)
    packed = _pack([small_grads_local[n] for n in _SMALL] + [loss[0, 0]])
    after = grad_x
    for gi, (names, (g8, from_sibling, send_sems, recv_sems, sums4, lands)) in enumerate(reducing.items()):
        if gi == len(reducing) - 1:
            summed = _small_all_reduce(packed, after, name="small_all_reduce")
            after = summed
        _, from_chips = _copies_wait(_chip_copies, send_sems, recv_sems, sums4, lands, after,
                                     name="rs_chips_wait_" + names[0])
        for i, n in enumerate(names):
            g, d, m_, v_ = _adamw_shard(_shard(n, w[n]), _shard(n, m[n]), _shard(n, v[n]), g8[i], from_sibling[i],
                                        from_chips[i], pos, name="adamw_" + n)
            out_g[n], out_d[n], out_m[n], out_v[n] = (_unshard(n, o) for o in (g, d, m_, v_))
            after = d
    *small_grads, loss_sum = _unpack(summed, small_like + [jax.ShapeDtypeStruct((), F32)])
    d_s, m_s, v_s = _adamw_small(small_w, summed[:small_w.shape[0]], _pack([m[n] for n in _SMALL]),
                                 _pack([v[n] for n in _SMALL]), name="adamw_small")
    for n, g, d, m_, v_ in zip(_SMALL, small_grads, _unpack(d_s, small_like), _unpack(m_s, small_like), _unpack(v_s, small_like)):
        out_g[n], out_d[n], out_m[n], out_v[n] = g, d, m_, v_

    return (loss_sum, grad_x[None], *[out_g[n] for n in _ORDER], *[out_d[n] for n in _ORDER],
            *[out_m[n] for n in _ORDER], *[out_v[n] for n in _ORDER])
```

```python
import functools
import math

import numpy as np
import jax
import jax.numpy as jnp
from jax import lax
from jax.experimental import pallas as pl
from jax.experimental.pallas import tpu as pltpu

F32 = jnp.float32
BF16 = jnp.bfloat16

EPS = 1e-6
NEG = -1e30
HEAD_DIM = 128
BLOCK = 128
N_KV_HEADS = 2
KV_WIDTH = N_KV_HEADS * HEAD_DIM
REL_BUCKETS = 32
REL_MAX_DIST = 128

ADAM_LR = 0.001
ADAM_B1 = 0.9
ADAM_B2 = 0.999
ADAM_EPS = 1e-08
ADAM_WD = 0.01
ADAM_STEP = 10

N_DEV = 8
LANES = 128
VMEM_LIMIT = 56 * 1024 * 1024
MESH = pl.DeviceIdType.MESH


def _cparams(*sem):
    return pltpu.CompilerParams(dimension_semantics=sem, vmem_limit_bytes=VMEM_LIMIT)


def _div(n, target, mult=LANES):
    best = None
    for d in range(mult, min(n, target) + 1, mult):
        if n % d == 0:
            best = d
    assert best is not None, (n, target, mult)
    return best


_ANY = pl.BlockSpec(memory_space=pl.ANY)


def _ordered_after(body, n_inputs, in_specs, args, after):
    if after is None:
        return body, in_specs, args

    def wrapped(*refs):
        return body(*refs[:n_inputs], *refs[n_inputs + 1:])

    return wrapped, list(in_specs) + [_ANY], tuple(args) + (after,)


def _bucket_map():
    nb = REL_BUCKETS // 2
    qi = np.arange(BLOCK)[:, None]
    kj = np.arange(3 * BLOCK)[None, :]
    rel = kj - BLOCK - qi
    ret = np.where(rel > 0, nb, 0)
    n = np.abs(rel)
    max_exact = nb // 2
    nf = np.maximum(n, 1).astype(np.float32)
    large = max_exact + (np.log(nf / np.float32(max_exact)) / np.float32(math.log(REL_MAX_DIST / max_exact))
                         * np.float32(nb - max_exact)).astype(np.int32)
    large = np.minimum(large, nb - 1)
    return (ret + np.where(n < max_exact, n, large)).astype(np.int32)


_GELU_C = math.sqrt(2.0 / math.pi)
_GELU_A = 0.044715


def _gelu(x):
    t = jnp.tanh(_GELU_C * (x + _GELU_A * (x * x * x)))
    return 0.5 * x * (1.0 + t)


def _gelu_and_grad(x):
    x2 = x * x
    t = jnp.tanh(_GELU_C * (x + _GELU_A * (x2 * x)))
    g = 0.5 * x * (1.0 + t)
    dg = 0.5 * (1.0 + t) + 0.5 * x * (1.0 - t * t) * (_GELU_C * (1.0 + 3.0 * _GELU_A * x2))
    return g, dg


def _sigmoid(x):
    return 1.0 / (1.0 + jnp.exp(-x))


def _mm(a, b, *, name, ta=False, tb=False, add=None, out_dtype=F32, bm=1024, bn=1024, bk=None, after=None,
        row_blocks=None, into=None):
    if ta:
        K, M = a.shape
    else:
        M, K = a.shape
    N = b.shape[0] if tb else b.shape[1]
    assert (b.shape[1] if tb else b.shape[0]) == K
    bm = _div(M, bm)
    bn = _div(N, bn)
    bk = K if bk is None else _div(K, bk)
    nk = K // bk
    i0, ni = (0, M // bm) if row_blocks is None else row_blocks
    a_spec = (pl.BlockSpec((bk, bm), lambda i, j, k: (k, i + i0)) if ta
              else pl.BlockSpec((bm, bk), lambda i, j, k: (i + i0, k)))
    b_spec = pl.BlockSpec((bn, bk), lambda i, j, k: (j, k)) if tb else pl.BlockSpec((bk, bn), lambda i, j, k: (k, j))
    o_spec = pl.BlockSpec((bm, bn), lambda i, j, k: (i + i0, j))
    dims = (((0 if ta else 1,), (1 if tb else 0,)), ((), ()))
    has_add = add is not None

    def body(*refs):
        if has_add:
            a_ref, b_ref, add_ref, o_ref, *scratch = refs
        else:
            a_ref, b_ref, o_ref, *scratch = refs
            add_ref = None
        p = lax.dot_general(a_ref[...].astype(BF16), b_ref[...].astype(BF16), dims, preferred_element_type=F32)
        if nk == 1:
            if has_add:
                p = p + add_ref[...]
            o_ref[...] = p.astype(out_dtype)
        else:
            acc = scratch[0]
            k = pl.program_id(2)

            @pl.when(k == 0)
            def _():
                acc[...] = p

            @pl.when(k > 0)
            def _():
                acc[...] += p

            @pl.when(k == nk - 1)
            def _():
                r = acc[...]
                if has_add:
                    r = r + add_ref[...]
                o_ref[...] = r.astype(out_dtype)

    in_specs = [a_spec, b_spec] + ([o_spec] if has_add else [])
    args = (a, b) + ((add,) if has_add else ())
    aliases = {}
    if into is not None:
        body, in_specs, args = _ordered_after(body, len(args), in_specs, args, into)
        aliases = {len(args) - 1: 0}
    body, in_specs, args = _ordered_after(body, len(args), in_specs, args, after)
    return pl.pallas_call(
        body, name=name, grid=(ni, N // bn, nk),
        in_specs=in_specs, out_specs=o_spec,
        out_shape=jax.ShapeDtypeStruct((M, N), out_dtype),
        input_output_aliases=aliases,
        scratch_shapes=[pltpu.VMEM((bm, bn), F32)] if nk > 1 else [],
        compiler_params=_cparams("parallel", "parallel", "arbitrary"),
    )(*args)


def _mm_resid_rms(a, b, resid, gain, *, name, bm=512):
    M, K = a.shape
    N = b.shape[1]
    bm = _div(M, bm)

    def body(a_ref, b_ref, r_ref, g_ref, x_ref, h_ref):
        x = r_ref[...] + jnp.dot(a_ref[...], b_ref[...], preferred_element_type=F32)
        x_ref[...] = x
        r = lax.rsqrt(jnp.mean(x * x, axis=-1, keepdims=True) + EPS)
        h_ref[...] = ((x * r) * g_ref[...]).astype(BF16)

    row = pl.BlockSpec((bm, N), lambda i: (i, 0))
    return pl.pallas_call(
        body, name=name, grid=(M // bm,),
        in_specs=[pl.BlockSpec((bm, K), lambda i: (i, 0)), pl.BlockSpec((K, N), lambda i: (0, 0)), row,
                  pl.BlockSpec((1, N), lambda i: (0, 0))],
        out_specs=[row, row], out_shape=[jax.ShapeDtypeStruct((M, N), F32), jax.ShapeDtypeStruct((M, N), BF16)],
        compiler_params=_cparams("parallel"),
    )(a, b, resid, gain)


def _mm_sum2(a1, b1, a2, b2, *, name, bm=1024, bn=512, bk=2816, after=None):
    M, K = a1.shape
    N = b1.shape[1]
    bm, bn, bk = _div(M, bm), _div(N, bn), _div(K, bk)
    nk = K // bk

    def body(a1_ref, b1_ref, a2_ref, b2_ref, o_ref, acc):
        p = (jnp.dot(a1_ref[...], b1_ref[...], preferred_element_type=F32)
             + jnp.dot(a2_ref[...], b2_ref[...], preferred_element_type=F32))
        k = pl.program_id(2)

        @pl.when(k == 0)
        def _():
            acc[...] = p

        @pl.when(k > 0)
        def _():
            acc[...] += p

        @pl.when(k == nk - 1)
        def _():
            o_ref[...] = acc[...]

    a_spec = pl.BlockSpec((bm, bk), lambda i, j, k: (i, k))
    b_spec = pl.BlockSpec((bk, bn), lambda i, j, k: (k, j))
    body, in_specs, args = _ordered_after(body, 4, [a_spec, b_spec, a_spec, b_spec], (a1, b1, a2, b2), after)
    return pl.pallas_call(
        body, name=name, grid=(M // bm, N // bn, nk),
        in_specs=in_specs, out_specs=pl.BlockSpec((bm, bn), lambda i, j, k: (i, j)),
        out_shape=jax.ShapeDtypeStruct((M, N), F32),
        scratch_shapes=[pltpu.VMEM((bm, bn), F32)],
        compiler_params=_cparams("parallel", "parallel", "arbitrary"),
    )(*args)


def _blocks_per_tile(c):
    nb = 1
    while (nb * c) % LANES or (nb * c < 1024 and nb < N_DEV):
        nb *= 2
    assert nb <= N_DEV and (nb * c) % LANES == 0, c
    return nb


def _mm_w8(a, w8, *, name, bm=1024, out_dtype=F32):
    M, K = a.shape
    _, _, c = w8.shape
    nb = _blocks_per_tile(c)
    bm = _div(M, bm)

    def body(a_ref, w_ref, o_ref):
        a_ = a_ref[...]
        for t in range(nb):
            o_ref[:, t * c:(t + 1) * c] = jnp.dot(a_, w_ref[t], preferred_element_type=F32).astype(out_dtype)

    return pl.pallas_call(
        body, name=name, grid=(M // bm, N_DEV // nb),
        in_specs=[pl.BlockSpec((bm, K), lambda i, j: (i, 0)), pl.BlockSpec((nb, K, c), lambda i, j: (j, 0, 0))],
        out_specs=pl.BlockSpec((bm, nb * c), lambda i, j: (i, j)),
        out_shape=jax.ShapeDtypeStruct((M, N_DEV * c), out_dtype),
        compiler_params=_cparams("parallel", "parallel"),
    )(a, w8)


def _mm_w8t(dy, w8, *, name, add=None, out_dtype=F32, bm=1024, bn=1024, after=None, lead=None):
    M = dy.shape[-2]
    _, K, c = w8.shape
    nb = _blocks_per_tile(c)
    nk = N_DEV // nb
    bm, bn = _div(M, bm), _div(K, bn)
    has_add = add is not None
    dims = (((1,), (1,)), ((), ()))

    def body(*refs):
        if has_add:
            dy_ref, w_ref, add_ref, o_ref, acc = refs
        else:
            dy_ref, w_ref, o_ref, acc = refs
        p = lax.dot_general(dy_ref[:, 0:c], w_ref[0], dims, preferred_element_type=F32)
        for t in range(1, nb):
            p = p + lax.dot_general(dy_ref[:, t * c:(t + 1) * c], w_ref[t], dims, preferred_element_type=F32)
        k = pl.program_id(2)

        @pl.when(k == 0)
        def _():
            acc[...] = p

        @pl.when(k > 0)
        def _():
            acc[...] += p

        @pl.when(k == nk - 1)
        def _():
            r = acc[...]
            if has_add:
                r = r + add_ref[...]
            o_ref[...] = r.astype(out_dtype)

    o_spec = pl.BlockSpec((bm, bn), lambda i, j, k: (i, j))
    dy_spec = (pl.BlockSpec((bm, nb * c), lambda i, j, k: (i, k)) if lead is None
               else pl.BlockSpec((None, bm, nb * c), lambda i, j, k: (lead, i, k)))
    in_specs = [dy_spec, pl.BlockSpec((nb, bn, c), lambda i, j, k: (k, j, 0))]
    in_specs += [o_spec] if has_add else []
    args = (dy, w8) + ((add,) if has_add else ())
    body, in_specs, args = _ordered_after(body, len(args), in_specs, args, after)
    return pl.pallas_call(
        body, name=name, grid=(M // bm, K // bn, nk),
        in_specs=in_specs, out_specs=o_spec,
        out_shape=jax.ShapeDtypeStruct((M, K), out_dtype),
        scratch_shapes=[pltpu.VMEM((bm, bn), F32)],
        compiler_params=_cparams("parallel", "parallel", "arbitrary"),
    )(*args)


def _mm_gw8(x, dy, c, *, name, bk=1024, lead=None):
    T, K = x.shape
    nb = _blocks_per_tile(c)
    bk = _div(K, bk)
    dims = (((0,), (0,)), ((), ()))

    def body(x_ref, dy_ref, o_ref):
        x_ = x_ref[...]
        for t in range(nb):
            o_ref[t] = lax.dot_general(x_, dy_ref[:, t * c:(t + 1) * c], dims, preferred_element_type=F32).astype(BF16)

    dy_spec = (pl.BlockSpec((T, nb * c), lambda i, j: (0, j)) if lead is None
               else pl.BlockSpec((None, T, nb * c), lambda i, j: (lead, 0, j)))
    return pl.pallas_call(
        body, name=name, grid=(K // bk, N_DEV // nb),
        in_specs=[pl.BlockSpec((T, bk), lambda i, j: (0, i)), dy_spec],
        out_specs=pl.BlockSpec((nb, bk, c), lambda i, j: (j, i, 0)),
        out_shape=jax.ShapeDtypeStruct((N_DEV, K, c), BF16),
        compiler_params=_cparams("parallel", "parallel"),
    )(x, dy)


def _rms_fwd(x, g, *, name):
    T, D = x.shape
    tm = _div(T, 256, 8)

    def body(x_ref, g_ref, h_ref):
        xf = x_ref[...]
        r = lax.rsqrt(jnp.mean(xf * xf, axis=-1, keepdims=True) + EPS)
        h_ref[...] = ((xf * r) * g_ref[...]).astype(BF16)

    return pl.pallas_call(
        body, name=name, grid=(T // tm,),
        in_specs=[pl.BlockSpec((tm, D), lambda i: (i, 0)), pl.BlockSpec((1, D), lambda i: (0, 0))],
        out_specs=pl.BlockSpec((tm, D), lambda i: (i, 0)),
        out_shape=jax.ShapeDtypeStruct((T, D), BF16),
        compiler_params=_cparams("parallel"),
    )(x, g)


def _rms_bwd(x, g, dh, dres, *, name, want_bf16, after=None):
    T, D = x.shape
    tm = _div(T, 256, 8)

    def body(x_ref, g_ref, dh_ref, dres_ref, dx_ref, *rest):
        if want_bf16:
            dxb_ref, dg_ref = rest
        else:
            (dg_ref,) = rest
        xf = x_ref[...]
        r = lax.rsqrt(jnp.mean(xf * xf, axis=-1, keepdims=True) + EPS)
        xhat = xf * r
        dh_ = dh_ref[...]
        dy = dh_ * g_ref[...]
        dx = dres_ref[...] + r * (dy - xhat * jnp.mean(dy * xhat, axis=-1, keepdims=True))
        dx_ref[...] = dx
        if want_bf16:
            dxb_ref[...] = dx.astype(BF16)
        part = jnp.sum(dh_ * xhat, axis=0, keepdims=True)

        @pl.when(pl.program_id(0) == 0)
        def _():
            dg_ref[...] = part

        @pl.when(pl.program_id(0) > 0)
        def _():
            dg_ref[...] += part

    row = pl.BlockSpec((tm, D), lambda i: (i, 0))
    vec = pl.BlockSpec((1, D), lambda i: (0, 0))
    out_specs = [row] + ([row] if want_bf16 else []) + [vec]
    out_shape = ([jax.ShapeDtypeStruct((T, D), F32)] + ([jax.ShapeDtypeStruct((T, D), BF16)] if want_bf16 else [])
                 + [jax.ShapeDtypeStruct((1, D), F32)])
    body, in_specs, args = _ordered_after(body, 4, [row, vec, row, row], (x, g, dh, dres), after)
    return pl.pallas_call(
        body, name=name, grid=(T // tm,),
        in_specs=in_specs, out_specs=out_specs, out_shape=out_shape,
        compiler_params=_cparams("arbitrary"),
    )(*args)


def _loss_head(x, g, target, *, name):
    T, D = x.shape
    tm = _div(T, 256, 8)

    def body(x_ref, g_ref, t_ref, loss_ref, dx_ref, dxb_ref, dg_ref):
        xf = x_ref[...]
        r = lax.rsqrt(jnp.mean(xf * xf, axis=-1, keepdims=True) + EPS)
        xhat = xf * r
        gain = g_ref[...]
        err = xhat * gain - t_ref[...]
        lpart = 0.5 * jnp.sum(jnp.mean(err * err, axis=-1, keepdims=True), axis=0, keepdims=True)
        dh_ = err * (1.0 / D)
        dy = dh_ * gain
        dx = r * (dy - xhat * jnp.mean(dy * xhat, axis=-1, keepdims=True))
        dx_ref[...] = dx
        dxb_ref[...] = dx.astype(BF16)
        part = jnp.sum(dh_ * xhat, axis=0, keepdims=True)

        @pl.when(pl.program_id(0) == 0)
        def _():
            dg_ref[...] = part
            loss_ref[...] = jnp.broadcast_to(lpart, loss_ref.shape)

        @pl.when(pl.program_id(0) > 0)
        def _():
            dg_ref[...] += part
            loss_ref[...] += jnp.broadcast_to(lpart, loss_ref.shape)

    row = pl.BlockSpec((tm, D), lambda i: (i, 0))
    vec = pl.BlockSpec((1, D), lambda i: (0, 0))
    return pl.pallas_call(
        body, name=name, grid=(T // tm,),
        in_specs=[row, vec, row],
        out_specs=[pl.BlockSpec((8, LANES), lambda i: (0, 0)), row, row, vec],
        out_shape=[jax.ShapeDtypeStruct((8, LANES), F32), jax.ShapeDtypeStruct((T, D), F32),
                   jax.ShapeDtypeStruct((T, D), BF16), jax.ShapeDtypeStruct((1, D), F32)],
        compiler_params=_cparams("arbitrary"),
    )(x, g, target)


def _gate_cols(D):
    off_a = 3 * D // 2 + 2 * KV_WIDTH
    off_b = off_a + D
    cw = math.gcd(math.gcd(off_a, off_b), math.gcd(D, 512))
    return cw, off_a // cw, off_b // cw


def _merge_fwd(z, ya, yb, *, name):
    T, D = ya.shape
    cw, ba, bb = _gate_cols(D)
    tm = _div(T, 512, 8)

    def body(ga_ref, gb_ref, ya_ref, yb_ref, m_ref):
        m_ref[...] = (_sigmoid(ga_ref[...]) * ya_ref[...] + _sigmoid(gb_ref[...]) * yb_ref[...]).astype(BF16)

    blk = pl.BlockSpec((tm, cw), lambda i, j: (i, j))
    return pl.pallas_call(
        body, name=name, grid=(T // tm, D // cw),
        in_specs=[pl.BlockSpec((tm, cw), lambda i, j: (i, ba + j)), pl.BlockSpec((tm, cw), lambda i, j: (i, bb + j)), blk, blk],
        out_specs=blk, out_shape=jax.ShapeDtypeStruct((T, D), BF16),
        compiler_params=_cparams("parallel", "parallel"),
    )(z, z, ya, yb)


def _merge_bwd(z, ya, yb, dm, *, name, after=None):
    T, D = ya.shape
    cw, ba, bb = _gate_cols(D)
    nj = D // cw
    assert bb == ba + nj
    tm = _div(T, 512, 8)

    def body(g_ref, ya_ref, yb_ref, dm_ref, dy_ref, dz_ref):
        sig = _sigmoid(g_ref[...])
        dm_ = dm_ref[...]
        y = jnp.where(pl.program_id(1) == 0, ya_ref[...], yb_ref[...])
        dy_ref[...] = (dm_ * sig).astype(BF16)
        dz_ref[...] = (dm_ * y * (sig * (1.0 - sig))).astype(BF16)

    in_specs = [pl.BlockSpec((tm, cw), lambda i, s, j: (i, ba + s * nj + j)),
                pl.BlockSpec((tm, cw), lambda i, s, j: (i, j * (1 - s))),
                pl.BlockSpec((tm, cw), lambda i, s, j: (i, j * s)),
                pl.BlockSpec((tm, cw), lambda i, s, j: (i, j))]
    body, in_specs, args = _ordered_after(body, 4, in_specs, (z, ya, yb, dm), after)
    return pl.pallas_call(
        body, name=name, grid=(T // tm, 2, nj),
        in_specs=in_specs,
        out_specs=[pl.BlockSpec((None, tm, cw), lambda i, s, j: (s, i, j)),
                   pl.BlockSpec((tm, cw), lambda i, s, j: (i, ba + s * nj + j))],
        out_shape=[jax.ShapeDtypeStruct((2, T, D), BF16), jax.ShapeDtypeStruct(z.shape, BF16)],
        compiler_params=_cparams("parallel", "arbitrary", "arbitrary"),
    )(*args)


def _swiglu_mm_fwd(h, wu_t, gate, *, name, bm=1024, bn=512):
    T, D = h.shape
    F = wu_t.shape[0]
    bm, bn = _div(T, bm), _div(F, bn)

    rc = _div(bm, 256, 16)

    def body(h_ref, wu_ref, gin_ref, g_ref, u_ref, act_ref):
        w = wu_ref[...]
        for r in range(0, bm, rc):
            rows = slice(r, r + rc)
            u = lax.dot_general(h_ref[rows, :], w, (((1,), (1,)), ((), ())), preferred_element_type=F32)
            g = gin_ref[rows, :]
            g_ref[rows, :] = g.astype(BF16)
            u_ref[rows, :] = u.astype(BF16)
            act_ref[rows, :] = (g * _sigmoid(g) * u).astype(BF16)

    o_spec = pl.BlockSpec((bm, bn), lambda i, j: (i, j))
    return pl.pallas_call(
        body, name=name, grid=(T // bm, F // bn),
        in_specs=[pl.BlockSpec((bm, D), lambda i, j: (i, 0)), pl.BlockSpec((bn, D), lambda i, j: (j, 0)), o_spec],
        out_specs=[o_spec] * 3, out_shape=[jax.ShapeDtypeStruct((T, F), BF16)] * 3,
        compiler_params=_cparams("parallel", "parallel"),
    )(h, wu_t, gate)


def _swiglu_mm_bwd(dx, w_down, gate, up, *, name, bm=2048, bn=512, after=None):
    T, D = dx.shape
    F = w_down.shape[0]
    bm, bn = _div(T, bm), _div(F, bn)
    dims = (((1,), (1,)), ((), ()))

    rc = _div(bm, 256, 16)

    def body(dx_ref, w_ref, g_ref, u_ref, dg_ref, du_ref):
        w = w_ref[...]
        for r in range(0, bm, rc):
            rows = slice(r, r + rc)
            d = lax.dot_general(dx_ref[rows, :], w, dims, preferred_element_type=F32)
            g = g_ref[rows, :].astype(F32)
            s = _sigmoid(g)
            silu = g * s
            dg_ref[rows, :] = (d * u_ref[rows, :].astype(F32) * (s + silu * (1.0 - s))).astype(BF16)
            du_ref[rows, :] = (d * silu).astype(BF16)

    o_spec = pl.BlockSpec((bm, bn), lambda i, j: (i, j))
    in_specs = [pl.BlockSpec((bm, D), lambda i, j: (i, 0)), pl.BlockSpec((bn, D), lambda i, j: (j, 0)), o_spec, o_spec]
    body, in_specs, args = _ordered_after(body, 4, in_specs, (dx, w_down, gate, up), after)
    out = jax.ShapeDtypeStruct((T, F), BF16)
    return pl.pallas_call(
        body, name=name, grid=(T // bm, F // bn), in_specs=in_specs, out_specs=[o_spec, o_spec], out_shape=[out, out],
        compiler_params=_cparams("parallel", "parallel"),
    )(*args)


def _sgu_fwd(z, gain, ws_b, bs_t, *, name):
    T = z.shape[0]
    SW = gain.shape[1]
    G = SW // BLOCK

    def body(zu_ref, zv_ref, gain_ref, ws_ref, bs_ref, a_ref):
        u = _gelu(zu_ref[...])
        vg = _gelu(zv_ref[...])
        r = lax.rsqrt(jnp.mean(vg * vg, axis=-1, keepdims=True) + EPS)
        vn = ((vg * r) * gain_ref[...]).astype(BF16)
        for g in range(G):
            sl = slice(g * BLOCK, (g + 1) * BLOCK)
            mixed = jnp.dot(ws_ref[g], vn[:, sl], preferred_element_type=F32) + bs_ref[:, g:g + 1]
            a_ref[:, sl] = (u[:, sl] * mixed).astype(BF16)

    return pl.pallas_call(
        body, name=name, grid=(T // BLOCK,),
        in_specs=[pl.BlockSpec((BLOCK, SW), lambda c: (c, 0)), pl.BlockSpec((BLOCK, SW), lambda c: (c, 1)),
                  pl.BlockSpec((1, SW), lambda c: (0, 0)), pl.BlockSpec((G, BLOCK, BLOCK), lambda c: (0, 0, 0)),
                  pl.BlockSpec((BLOCK, G), lambda c: (0, 0))],
        out_specs=pl.BlockSpec((BLOCK, SW), lambda c: (c, 0)),
        out_shape=jax.ShapeDtypeStruct((T, SW), BF16),
        compiler_params=_cparams("parallel"),
    )(z, z, gain, ws_b, bs_t)


def _sgu_bwd(z, gain, ws_b, bs_t, da, dz, *, name):
    T = z.shape[0]
    SW = gain.shape[1]
    G = SW // BLOCK

    def body(zu_ref, zv_ref, gain_ref, ws_ref, bs_ref, da_ref, dz_in_ref, dz_ref, dws_ref, dbs_ref, dgain_ref, dvn_ref):
        first = pl.program_id(0) == 0

        @pl.when(first)
        def _():
            dws_ref[...] = jnp.zeros_like(dws_ref)
            dbs_ref[...] = jnp.zeros_like(dbs_ref)
            dgain_ref[...] = jnp.zeros_like(dgain_ref)

        u, du = _gelu_and_grad(zu_ref[...])
        vg, dvg = _gelu_and_grad(zv_ref[...])
        r = lax.rsqrt(jnp.mean(vg * vg, axis=-1, keepdims=True) + EPS)
        xhat = vg * r
        gain_ = gain_ref[...]
        vn = (xhat * gain_).astype(BF16)
        da_ = da_ref[...]
        for g in range(G):
            sl = slice(g * BLOCK, (g + 1) * BLOCK)
            w = ws_ref[g]
            mixed = jnp.dot(w, vn[:, sl], preferred_element_type=F32) + bs_ref[:, g:g + 1]
            dmix = da_[:, sl] * u[:, sl]
            dz_ref[:, sl] = (da_[:, sl] * mixed * du[:, sl]).astype(BF16)
            dmb = dmix.astype(BF16)
            dws_ref[g] += lax.dot_general(dmb, vn[:, sl], (((1,), (1,)), ((), ())), preferred_element_type=F32)
            dbs_ref[:, g:g + 1] += jnp.sum(dmix, axis=-1, keepdims=True)
            dvn_ref[:, sl] = lax.dot_general(w, dmb, (((0,), (0,)), ((), ())), preferred_element_type=F32)
        dvn = dvn_ref[...]
        dgain_ref[...] += jnp.sum(dvn * xhat, axis=0, keepdims=True)
        dy = dvn * gain_
        dv_ = r * (dy - xhat * jnp.mean(dy * xhat, axis=-1, keepdims=True))
        dz_ref[:, SW:] = (dv_ * dvg).astype(BF16)

    row = pl.BlockSpec((BLOCK, SW), lambda c: (c, 0))
    return pl.pallas_call(
        body, name=name, grid=(T // BLOCK,),
        in_specs=[row, pl.BlockSpec((BLOCK, SW), lambda c: (c, 1)),
                  pl.BlockSpec((1, SW), lambda c: (0, 0)), pl.BlockSpec((G, BLOCK, BLOCK), lambda c: (0, 0, 0)),
                  pl.BlockSpec((BLOCK, G), lambda c: (0, 0)), row, _ANY],
        out_specs=[pl.BlockSpec((BLOCK, 2 * SW), lambda c: (c, 0)), pl.BlockSpec((G, BLOCK, BLOCK), lambda c: (0, 0, 0)),
                   pl.BlockSpec((BLOCK, G), lambda c: (0, 0)), pl.BlockSpec((1, SW), lambda c: (0, 0))],
        out_shape=[jax.ShapeDtypeStruct(dz.shape, dz.dtype),
                   jax.ShapeDtypeStruct((G, BLOCK, BLOCK), F32), jax.ShapeDtypeStruct((BLOCK, G), F32),
                   jax.ShapeDtypeStruct((1, SW), F32)],
        input_output_aliases={6: 0},
        scratch_shapes=[pltpu.VMEM((BLOCK, SW), F32)],
        compiler_params=_cparams("arbitrary"),
    )(z, z, gain, ws_b, bs_t, da, dz)


def _bias_table(rel_bias, bmap, *, name):
    H = rel_bias.shape[1]

    def body(rb_ref, bmap_ref, o_ref):
        bm_ = bmap_ref[...]
        for h in range(H):
            acc = jnp.zeros(bm_.shape, F32)
            for b in range(REL_BUCKETS):
                acc = jnp.where(bm_ == b, rb_ref[b, h], acc)
            o_ref[h] = acc

    return pl.pallas_call(
        body, name=name,
        in_specs=[pl.BlockSpec(memory_space=pltpu.SMEM), pl.BlockSpec(memory_space=pltpu.VMEM)],
        out_specs=pl.BlockSpec(memory_space=pltpu.VMEM),
        out_shape=jax.ShapeDtypeStruct((H, BLOCK, 3 * BLOCK), F32),
    )(rel_bias, bmap)


def _attn_probs(q_ref, kb, bias_ref, sink_ref, s_ref, n, T, group):
    H = s_ref.shape[0]
    for h in range(H):
        kv = h // group
        qh = q_ref[:, h * HEAD_DIM:(h + 1) * HEAD_DIM].astype(BF16)
        s_ref[h] = lax.dot_general(qh, kb[:, kv * HEAD_DIM:(kv + 1) * HEAD_DIM], (((1,), (1,)), ((), ())),
                                   preferred_element_type=F32)
    row = lax.broadcasted_iota(jnp.int32, (BLOCK, 3 * BLOCK), 0)
    col = lax.broadcasted_iota(jnp.int32, (BLOCK, 3 * BLOCK), 1)
    key_pos = n * BLOCK + col - BLOCK
    valid = (jnp.abs(col - BLOCK - row) <= BLOCK) & (key_pos >= 0) & (key_pos < T)
    s = s_ref[...] * (HEAD_DIM ** -0.5) + bias_ref[...]
    s = jnp.where(valid[None], s, NEG)
    sink = sink_ref[...]
    m = jnp.maximum(jnp.max(s, axis=-1, keepdims=True), sink)
    e = jnp.exp(s - m)
    es = jnp.exp(sink - m)
    inv = 1.0 / (jnp.sum(e, axis=-1, keepdims=True) + es)
    return e * inv, es * inv


def _attn_fwd(z, kpad, vpad, bias, sink, *, name):
    T = z.shape[0]
    H = bias.shape[0]
    AW = H * HEAD_DIM
    group = H // N_KV_HEADS

    def body(q_ref, k_ref, v_ref, bias_ref, sink_ref, o_ref, s_ref, p_ref):
        n = pl.program_id(0)
        start = pl.multiple_of(n * BLOCK, BLOCK)
        kb = k_ref[pl.ds(start, 3 * BLOCK), :]
        vb = v_ref[pl.ds(start, 3 * BLOCK), :]
        p, _ = _attn_probs(q_ref, kb, bias_ref, sink_ref, s_ref, n, T, group)
        p_ref[...] = p.astype(BF16)
        for h in range(H):
            kv = h // group
            o = jnp.dot(p_ref[h], vb[:, kv * HEAD_DIM:(kv + 1) * HEAD_DIM], preferred_element_type=F32)
            o_ref[:, h * HEAD_DIM:(h + 1) * HEAD_DIM] = o.astype(BF16)

    full_kv = pl.BlockSpec((T + 2 * BLOCK, KV_WIDTH), lambda n: (0, 0))
    return pl.pallas_call(
        body, name=name, grid=(T // BLOCK,),
        in_specs=[pl.BlockSpec((BLOCK, AW), lambda n: (n, 2)), full_kv, full_kv,
                  pl.BlockSpec((H, BLOCK, 3 * BLOCK), lambda n: (0, 0, 0)), pl.BlockSpec((H, 1, 1), lambda n: (0, 0, 0))],
        out_specs=pl.BlockSpec((BLOCK, AW), lambda n: (n, 0)),
        out_shape=jax.ShapeDtypeStruct((T, AW), BF16),
        scratch_shapes=[pltpu.VMEM((H, BLOCK, 3 * BLOCK), F32), pltpu.VMEM((H, BLOCK, 3 * BLOCK), BF16)],
        compiler_params=_cparams("parallel"),
    )(z, kpad, vpad, bias, sink)


def _attn_bwd(z, kpad, vpad, bias, sink, do, dz, *, name):
    T = z.shape[0]
    H = bias.shape[0]
    AW = H * HEAD_DIM
    group = H // N_KV_HEADS
    scale = HEAD_DIM ** -0.5

    def body(q_ref, k_ref, v_ref, bias_ref, sink_ref, do_ref, dz_in_ref, dq_ref, dk_ref, dv_ref, dbias_ref, dsink_ref,
             s_ref, dp_ref, p_ref, ds_ref):
        n = pl.program_id(0)

        @pl.when(n == 0)
        def _():
            dk_ref[...] = jnp.zeros_like(dk_ref)
            dv_ref[...] = jnp.zeros_like(dv_ref)
            dbias_ref[...] = jnp.zeros_like(dbias_ref)
            dsink_ref[...] = jnp.zeros_like(dsink_ref)

        start = pl.multiple_of(n * BLOCK, BLOCK)
        kb = k_ref[pl.ds(start, 3 * BLOCK), :]
        vb = v_ref[pl.ds(start, 3 * BLOCK), :]
        p, p_sink = _attn_probs(q_ref, kb, bias_ref, sink_ref, s_ref, n, T, group)
        s_ref[...] = p
        p_ref[...] = p.astype(BF16)
        for h in range(H):
            kv = h // group
            dp_ref[h] = lax.dot_general(do_ref[:, h * HEAD_DIM:(h + 1) * HEAD_DIM], vb[:, kv * HEAD_DIM:(kv + 1) * HEAD_DIM],
                                        (((1,), (1,)), ((), ())), preferred_element_type=F32)
        p = s_ref[...]
        dp = dp_ref[...]
        delta = jnp.sum(p * dp, axis=-1, keepdims=True)
        ds = p * (dp - delta)
        dbias_ref[...] += ds
        dsink_ref[...] += -(p_sink * delta)
        ds_ref[...] = ds.astype(BF16)
        for kv in range(N_KV_HEADS):
            ksl = slice(kv * HEAD_DIM, (kv + 1) * HEAD_DIM)
            dk_acc = jnp.zeros((3 * BLOCK, HEAD_DIM), F32)
            dv_acc = jnp.zeros((3 * BLOCK, HEAD_DIM), F32)
            for gi in range(group):
                h = kv * group + gi
                hsl = slice(h * HEAD_DIM, (h + 1) * HEAD_DIM)
                dsb = ds_ref[h]
                dq = jnp.dot(dsb, kb[:, ksl], preferred_element_type=F32) * scale
                dq_ref[:, hsl] = dq.astype(BF16)
                dk_acc = dk_acc + lax.dot_general(dsb, q_ref[:, hsl].astype(BF16), (((0,), (0,)), ((), ())),
                                                  preferred_element_type=F32)
                dv_acc = dv_acc + lax.dot_general(p_ref[h], do_ref[:, hsl], (((0,), (0,)), ((), ())),
                                                  preferred_element_type=F32)
            dk_ref[pl.ds(start, 3 * BLOCK), ksl] += dk_acc * scale
            dv_ref[pl.ds(start, 3 * BLOCK), ksl] += dv_acc

    full_kv = pl.BlockSpec((T + 2 * BLOCK, KV_WIDTH), lambda n: (0, 0))
    bias_spec = pl.BlockSpec((H, BLOCK, 3 * BLOCK), lambda n: (0, 0, 0))
    row = pl.BlockSpec((BLOCK, AW), lambda n: (n, 0))
    q_cols = pl.BlockSpec((BLOCK, AW), lambda n: (n, 2))
    band = (H, BLOCK, 3 * BLOCK)
    return pl.pallas_call(
        body, name=name, grid=(T // BLOCK,),
        in_specs=[q_cols, full_kv, full_kv, bias_spec, pl.BlockSpec((H, 1, 1), lambda n: (0, 0, 0)), row, _ANY],
        out_specs=[q_cols, full_kv, full_kv, bias_spec, pl.BlockSpec((H, BLOCK, 1), lambda n: (0, 0, 0))],
        out_shape=[jax.ShapeDtypeStruct(dz.shape, dz.dtype),
                   jax.ShapeDtypeStruct((T + 2 * BLOCK, KV_WIDTH), F32), jax.ShapeDtypeStruct((T + 2 * BLOCK, KV_WIDTH), F32),
                   jax.ShapeDtypeStruct(band, F32), jax.ShapeDtypeStruct((H, BLOCK, 1), F32)],
        input_output_aliases={6: 0},
        scratch_shapes=[pltpu.VMEM(band, F32), pltpu.VMEM(band, F32), pltpu.VMEM(band, BF16), pltpu.VMEM(band, BF16)],
        compiler_params=_cparams("arbitrary"),
    )(z, kpad, vpad, bias, sink, do, dz)


def _dkv_into(dkp, dvp, dz, *, name):
    T = dz.shape[0]
    D = (dz.shape[1] - 2 * KV_WIDTH) * 2 // 7
    col = (D + D // 2) // (2 * KV_WIDTH)
    assert col * 2 * KV_WIDTH == D + D // 2

    def body(dk_ref, dv_ref, dz_in_ref, o_ref):
        o_ref[:, :KV_WIDTH] = dk_ref[...].astype(BF16)
        o_ref[:, KV_WIDTH:] = dv_ref[...].astype(BF16)

    kv = pl.BlockSpec((BLOCK, KV_WIDTH), lambda n: (n + 1, 0))
    return pl.pallas_call(
        body, name=name, grid=(T // BLOCK,),
        in_specs=[kv, kv, _ANY], out_specs=pl.BlockSpec((BLOCK, 2 * KV_WIDTH), lambda n: (n, col)),
        out_shape=jax.ShapeDtypeStruct(dz.shape, dz.dtype), input_output_aliases={2: 0},
        compiler_params=_cparams("parallel"),
    )(dkp, dvp, dz)


def _kv_pad(z, *, name):
    T = z.shape[0]
    D = (z.shape[1] - 2 * KV_WIDTH) * 2 // 7
    kcol = (D + D // 2) // KV_WIDTH
    nb = T // BLOCK

    def body(k_ref, v_ref, ko_ref, vo_ref):
        b = pl.program_id(0)
        inside = (b >= 1) & (b <= nb)
        ko_ref[...] = jnp.where(inside, k_ref[...], 0.0).astype(BF16)
        vo_ref[...] = jnp.where(inside, v_ref[...], 0.0).astype(BF16)

    out = jax.ShapeDtypeStruct((T + 2 * BLOCK, KV_WIDTH), BF16)
    o_spec = pl.BlockSpec((BLOCK, KV_WIDTH), lambda b: (b, 0))
    return pl.pallas_call(
        body, name=name, grid=(nb + 2,),
        in_specs=[pl.BlockSpec((BLOCK, KV_WIDTH), lambda b: (jnp.clip(b - 1, 0, nb - 1), kcol)),
                  pl.BlockSpec((BLOCK, KV_WIDTH), lambda b: (jnp.clip(b - 1, 0, nb - 1), kcol + 1))],
        out_specs=[o_spec, o_spec], out_shape=[out, out],
        compiler_params=_cparams("parallel"),
    )(z, z)


def _attn_small_grads(dbias, dsink_rows, bmap, after, *, name):
    H = dbias.shape[0]

    def body(dbias_ref, dsink_ref, bmap_ref, drel_ref, ds_ref):
        bm_ = bmap_ref[...]
        for h in range(H):
            d = dbias_ref[h]
            for b in range(REL_BUCKETS):
                drel_ref[b, h] = jnp.sum(jnp.where(bm_ == b, d, 0.0))
            ds_ref[0, h] = jnp.sum(dsink_ref[h])

    vmem = pl.BlockSpec(memory_space=pltpu.VMEM)
    smem = pl.BlockSpec(memory_space=pltpu.SMEM)
    body, in_specs, args = _ordered_after(body, 3, [vmem, vmem, vmem], (dbias, dsink_rows, bmap), after)
    return pl.pallas_call(
        body, name=name, in_specs=in_specs, out_specs=[smem, smem],
        out_shape=[jax.ShapeDtypeStruct((REL_BUCKETS, H), F32), jax.ShapeDtypeStruct((1, H), F32)],
    )(*args)


def _local_step(x, target, weight, emit, flush, norm_mix, v_gain, w_s, b_s, sink, rel_bias, norm_ffn, norm_final):
    T, D = x.shape
    ws_b = w_s.astype(BF16)
    bs_t = b_s.T
    bmap = jnp.asarray(_bucket_map())
    sink = sink.reshape(-1, 1, 1)

    h = _rms_fwd(x, norm_mix, name="rms_mix")
    w_in = weight("w_in", h)
    z = _mm(h, w_in, tb=True, name="mm_z", bm=2048, bn=768)
    a = _sgu_fwd(z, v_gain, ws_b, bs_t, name="sgu_fwd")
    w_a = weight("w_a_out", a)
    ya = _mm_w8(a, w_a, name="mm_ya", bm=2048, out_dtype=BF16)
    kpad, vpad = _kv_pad(z, name="kv_pad")
    bias = _bias_table(rel_bias, bmap, name="bias_table")
    o = _attn_fwd(z, kpad, vpad, bias, sink, name="attn_fwd")
    w_b = weight("w_b_out", o)
    yb = _mm_w8(o, w_b, name="mm_yb", bm=2048, out_dtype=BF16)
    m = _merge_fwd(z, ya, yb, name="merge_fwd")
    w_o = weight("w_o", m)
    x1, h2 = _mm_resid_rms(m, w_o, x, norm_ffn, name="mm_x1_rms")
    w_gate = weight("w_gate", h2)
    gate = _mm(h2, w_gate, tb=True, name="mm_gate", bm=2048, bn=512)
    w_up = weight("w_up", gate)
    gate, up, act = _swiglu_mm_fwd(h2, w_up, gate, name="mm_up_swiglu")
    w_down = weight("w_down", act)
    x2 = _mm(act, w_down, name="mm_x2", add=x1, bm=1024, bn=512)
    loss, dx2, dx2b, g_norm_final = _loss_head(x2, norm_final, target, name="loss_head")

    g_w_down = _mm(act, dx2b, ta=True, out_dtype=BF16, name="mm_gwdown", bm=512, bn=2048)
    tok = emit(("w_down",), (g_w_down,))
    dgate, dup = _swiglu_mm_bwd(dx2b, w_down, gate, up, name="mm_dact_swiglu", after=tok)
    tok = flush(dgate)
    g_w_gate = _mm(dgate, h2, ta=True, out_dtype=BF16, name="mm_gwgate", bm=512, bn=2048, after=tok)
    g_w_up = _mm(dup, h2, ta=True, out_dtype=BF16, name="mm_gwup", bm=512, bn=2048)
    tok = emit(("w_gate", "w_up"), (g_w_gate, g_w_up))
    dh2 = _mm_sum2(dgate, w_gate, dup, w_up, name="mm_dh2", after=tok)
    tok = flush(dh2)
    dx1, dx1b, g_norm_ffn = _rms_bwd(x1, norm_ffn, dh2, dx2, name="rms_ffn_bwd", want_bf16=True, after=tok)

    g_w_o = _mm(m, dx1b, ta=True, out_dtype=BF16, name="mm_gwo", bm=2048, bn=512)
    tok = emit(("w_o",), (g_w_o,))
    dm = _mm(dx1b, w_o, tb=True, name="mm_dm", bm=2048, bn=512, after=tok)
    tok = flush(dm)
    dy, dz = _merge_bwd(z, ya, yb, dm, name="merge_bwd", after=tok)
    g_w_a = _mm_gw8(a, dy, w_a.shape[2], name="mm_gwa", lead=0)
    g_w_b = _mm_gw8(o, dy, w_b.shape[2], name="mm_gwb", lead=1)
    tok = emit(("w_a_out", "w_b_out"), (g_w_a, g_w_b))
    da = _mm_w8t(dy, w_a, name="mm_da", bm=2048, bn=512, after=tok, lead=0)
    tok = flush(da)
    do = _mm_w8t(dy, w_b, out_dtype=BF16, name="mm_do", bm=2048, bn=512, after=tok, lead=1)
    dz, g_w_s, g_b_s_t, g_v_gain = _sgu_bwd(z, v_gain, ws_b, bs_t, da, dz, name="sgu_bwd")
    dz, dkp, dvp, dbias, dsink_rows = _attn_bwd(z, kpad, vpad, bias, sink, do, dz, name="attn_bwd")
    dz = _dkv_into(dkp, dvp, dz, name="dkv_into_dz")
    g_w_in = _mm(dz, h, ta=True, out_dtype=BF16, name="mm_gwin", bm=768, bn=2048)
    tok = emit(("w_in",), (g_w_in,))
    half = dict(bm=T // 2, bn=256)
    dh = _mm(dz, w_in, name="mm_dh_top", row_blocks=(0, 1), after=tok, **half)
    tok = flush(dh)
    dh = _mm(dz, w_in, name="mm_dh_bottom", row_blocks=(1, 1), into=dh, after=tok, **half)
    g_rel_bias, g_sink = _attn_small_grads(dbias, dsink_rows, bmap, dh, name="attn_small_grads")
    grad_x, g_norm_mix = _rms_bwd(x, norm_mix, dh, dx1, name="rms_mix_bwd", want_bf16=False)

    small = dict(norm_mix=g_norm_mix, sgu_v_gain=g_v_gain, sgu_w_s=g_w_s, sgu_b_s=g_b_s_t.T, attn_sink=g_sink,
                 rel_bias=g_rel_bias, norm_ffn=g_norm_ffn, norm_final=g_norm_final)
    return loss, grad_x, small


def _position():
    return lax.axis_index("x"), lax.axis_index("y"), lax.axis_index("c")


def _other_chips(x, y):
    return [(1 - x, y), (x, 1 - y), (1 - x, 1 - y)]


def _slot(px, py, pc):
    return 4 * px + 2 * py + pc


_HBM = pl.BlockSpec(memory_space=pltpu.HBM)
_SEM = pl.BlockSpec(memory_space=pltpu.SEMAPHORE)
_DATAFLOW = pltpu.SideEffectType.DATAFLOW_SIDE_EFFECTING


def _in_hbm(a):
    return pltpu.with_memory_space_constraint(a, pltpu.HBM)


def _own_slot(shard, pos, *, name, after=None):
    R, C = shard.shape
    tr = _div(R, 256, 16)

    def body(pos_ref, w_ref, o_ref):
        o_ref[...] = w_ref[...].astype(BF16)

    body, in_specs, args = _ordered_after(body, 2, [pl.BlockSpec((tr, C), lambda i, pos_ref: (i, 0))], (pos, shard), after)
    grid_spec = pltpu.PrefetchScalarGridSpec(
        num_scalar_prefetch=1, grid=(R // tr,), in_specs=in_specs,
        out_specs=pl.BlockSpec((None, tr, C), lambda i, pos_ref: (pos_ref[0], i, 0)))
    return pl.pallas_call(
        body, name=name, grid_spec=grid_spec,
        out_shape=jax.ShapeDtypeStruct((N_DEV, R, C), BF16),
        compiler_params=_cparams("parallel"),
    )(*args)


def _ag_copies(w, land_ref, send_sems, recv_sems):
    x, y, c = _position()
    mine = land_ref.at[_slot(x, y, c)]
    targets = [(px, py, c) for px, py in _other_chips(x, y)] + [(x, y, 1 - c)]
    return [pltpu.make_async_remote_copy(src_ref=mine, dst_ref=mine, send_sem=send_sems.at[4 * w + k],
                                         recv_sem=recv_sems.at[4 * w + k], device_id=to, device_id_type=MESH)
            for k, to in enumerate(targets)]


def _ag_start(buffers, groups, *, name):
    lands = [buffers[i] for g in groups for i in g]
    n, ng = len(lands), len(groups)
    sizes = [len(g) for g in groups]

    def body(*refs):
        land_refs = refs[:n]
        sems = refs[n:n + 2 * ng]
        token = refs[-1]
        i = 0
        for g in range(ng):
            for w in range(sizes[g]):
                for cp in _ag_copies(w, land_refs[i], sems[2 * g], sems[2 * g + 1]):
                    cp.start()
                i += 1
        token[...] = jnp.zeros_like(token)

    sem_shapes = [pltpu.SemaphoreType.DMA((4 * k,)) for k in sizes for _ in range(2)]
    outs = pl.pallas_call(
        body, name=name,
        in_specs=[_HBM] * n,
        out_specs=tuple([_SEM] * (2 * ng) + [_HBM] * n + [pl.BlockSpec(memory_space=pltpu.VMEM)]),
        out_shape=tuple(sem_shapes + [pltpu.HBM(a.shape, a.dtype) for a in lands] + [jax.ShapeDtypeStruct((8, LANES), F32)]),
        input_output_aliases={i: 2 * ng + i for i in range(n)},
        compiler_params=pltpu.CompilerParams(has_side_effects=_DATAFLOW),
    )(*[_in_hbm(a) for a in lands])
    sems, thru = outs[:2 * ng], outs[2 * ng:2 * ng + n]
    result, i = [], 0
    for g in range(ng):
        k = sizes[g]
        result.append((sems[2 * g], sems[2 * g + 1], list(thru[i:i + k])))
        i += k
    return result, outs[-1]


def _ag_wait(send_sems, recv_sems, lands, after, *, name):
    n = len(lands)

    def body(*refs):
        land_refs = refs[:n]
        send_ref, recv_ref = refs[n], refs[n + 1]
        token = refs[-1]
        for w in range(n):
            for cp in _ag_copies(w, land_refs[w], send_ref, recv_ref):
                cp.wait_send()
                cp.wait_recv()
        token[...] = jnp.zeros_like(token)

    outs = pl.pallas_call(
        body, name=name,
        in_specs=[_HBM] * n + [_SEM, _SEM, _ANY],
        out_specs=tuple([_HBM] * n + [pl.BlockSpec(memory_space=pltpu.VMEM)]),
        out_shape=tuple([pltpu.HBM(a.shape, a.dtype) for a in lands] + [jax.ShapeDtypeStruct((8, LANES), F32)]),
        input_output_aliases={i: i for i in range(n)},
        compiler_params=pltpu.CompilerParams(has_side_effects=_DATAFLOW),
    )(*lands, send_sems, recv_sems, after)
    return list(outs[:n]), outs[n]


def _ag_forward(lands, *, name, after=None):
    n = len(lands)

    def body(*refs):
        in_refs, out_refs = refs[:n], refs[n:2 * n]
        send_sems, recv_sems = refs[2 * n:]
        x, y, c = _position()
        copies = []
        for w in range(n):
            for k, (px, py) in enumerate(_other_chips(x, y)):
                cp = pltpu.make_async_remote_copy(
                    src_ref=in_refs[w].at[_slot(px, py, c)], dst_ref=out_refs[w].at[_slot(px, py, c)],
                    send_sem=send_sems.at[3 * w + k], recv_sem=recv_sems.at[3 * w + k],
                    device_id=(x, y, 1 - c), device_id_type=MESH)
                cp.start()
                copies.append(cp)
        for cp in copies:
            cp.wait()

    body, in_specs, args = _ordered_after(body, n, [_ANY] * n, tuple(lands), after)
    return pl.pallas_call(
        body, name=name,
        in_specs=in_specs, out_specs=[_ANY] * n,
        out_shape=[jax.ShapeDtypeStruct(a.shape, a.dtype) for a in lands],
        input_output_aliases={i: i for i in range(n)},
        scratch_shapes=[pltpu.SemaphoreType.DMA((3 * n,)), pltpu.SemaphoreType.DMA((3 * n,))],
    )(*args)


def _sibling_copies(w, g8_ref, land_ref, send_sems, recv_sems):
    x, y, c = _position()
    return [pltpu.make_async_remote_copy(src_ref=g8_ref.at[2 * p + (1 - c)], dst_ref=land_ref.at[p],
                                         send_sem=send_sems.at[4 * w + p], recv_sem=recv_sems.at[4 * w + p],
                                         device_id=(x, y, 1 - c), device_id_type=MESH)
            for p in range(4)]


def _chip_copies(w, sums_ref, land_ref, send_sems, recv_sems):
    x, y, c = _position()
    return [pltpu.make_async_remote_copy(src_ref=sums_ref.at[2 * px + py], dst_ref=land_ref.at[k],
                                         send_sem=send_sems.at[3 * w + k], recv_sem=recv_sems.at[3 * w + k],
                                         device_id=(px, py, c), device_id_type=MESH)
            for k, (px, py) in enumerate(_other_chips(x, y))]


def _copies_start(copies, per_weight, srcs, *, name):
    n = len(srcs)
    lands = [lax.empty((per_weight,) + s.shape[1:], s.dtype) for s in srcs]

    def body(*refs):
        src_refs, land_refs = refs[:n], refs[n:2 * n]
        send_sems, recv_sems = refs[2 * n], refs[2 * n + 1]
        token = refs[-1]
        for w in range(n):
            for cp in copies(w, src_refs[w], land_refs[w], send_sems, recv_sems):
                cp.start()
        token[...] = jnp.zeros_like(token)

    outs = pl.pallas_call(
        body, name=name,
        in_specs=[_HBM] * (2 * n),
        out_specs=tuple([_SEM, _SEM] + [_HBM] * (2 * n) + [pl.BlockSpec(memory_space=pltpu.VMEM)]),
        out_shape=tuple([pltpu.SemaphoreType.DMA((per_weight * n,)), pltpu.SemaphoreType.DMA((per_weight * n,))]
                        + [pltpu.HBM(a.shape, a.dtype) for a in srcs + lands] + [jax.ShapeDtypeStruct((8, LANES), F32)]),
        input_output_aliases={i: 2 + i for i in range(2 * n)},
        compiler_params=pltpu.CompilerParams(has_side_effects=_DATAFLOW),
    )(*[_in_hbm(a) for a in srcs + lands])
    return outs[0], outs[1], list(outs[2:2 + n]), list(outs[2 + n:2 + 2 * n]), outs[-1]


def _copies_wait(copies, send_sems, recv_sems, srcs, lands, after, *, name):
    n = len(srcs)

    def body(*refs):
        src_refs, land_refs = refs[:n], refs[n:2 * n]
        send_ref, recv_ref = refs[2 * n], refs[2 * n + 1]
        for w in range(n):
            for cp in copies(w, src_refs[w], land_refs[w], send_ref, recv_ref):
                cp.wait_send()
                cp.wait_recv()

    outs = pl.pallas_call(
        body, name=name,
        in_specs=[_HBM] * (2 * n) + [_SEM, _SEM, _ANY],
        out_specs=tuple([_HBM] * (2 * n)),
        out_shape=tuple(pltpu.HBM(a.shape, a.dtype) for a in srcs + lands),
        input_output_aliases={i: i for i in range(2 * n)},
        compiler_params=pltpu.CompilerParams(has_side_effects=_DATAFLOW),
    )(*srcs, *lands, send_sems, recv_sems, after)
    return list(outs[:n]), list(outs[n:])


def _chip_sums(g8, from_sibling, pos, *, name):
    _, R, C = g8.shape
    tr = _div(R, 512, 16)

    def body(pos_ref, g_ref, s_ref, o_ref):
        o_ref[...] = (g_ref[...].astype(F32) + s_ref[...].astype(F32)).astype(BF16)

    def chip(k, pos_ref):
        return jnp.where(k >= pos_ref[1], k + 1, k)

    grid_spec = pltpu.PrefetchScalarGridSpec(
        num_scalar_prefetch=1, grid=(3, R // tr),
        in_specs=[pl.BlockSpec((None, tr, C), lambda k, i, pos_ref: (2 * chip(k, pos_ref) + pos_ref[2], i, 0)),
                  pl.BlockSpec((None, tr, C), lambda k, i, pos_ref: (chip(k, pos_ref), i, 0))],
        out_specs=pl.BlockSpec((None, tr, C), lambda k, i, pos_ref: (chip(k, pos_ref), i, 0)))
    return pl.pallas_call(
        body, name=name, grid_spec=grid_spec,
        out_shape=jax.ShapeDtypeStruct((4, R, C), BF16),
        compiler_params=_cparams("parallel", "parallel"),
    )(pos, g8, from_sibling)


def _small_all_reduce(packed, after, *, name):
    R, L = packed.shape

    def body(x_ref, sum_ref, gath_ref, send_sems, recv_sems, local_sem):
        x, y, c = _position()
        me, sibling = (x, y, c), (x, y, 1 - c)
        chips = _other_chips(x, y)

        def rows(px, py, pc):
            return gath_ref.at[pl.ds(_slot(px, py, pc) * R, R), :]

        def copy(k, block, to, src=None):
            return pltpu.make_async_remote_copy(
                src_ref=rows(*block) if src is None else src, dst_ref=rows(*block),
                send_sem=send_sems.at[k], recv_sem=recv_sems.at[k], device_id=to, device_id_type=MESH)

        mine = pltpu.make_async_copy(x_ref, rows(*me), local_sem)
        mine.start()
        first = [copy(0, me, sibling, src=x_ref)]
        first += [copy(1 + j, me, (*chip, c), src=x_ref) for j, chip in enumerate(chips)]
        for cp in first:
            cp.start()
        passed = [copy(4 + j, (*chip, c), sibling) for j, chip in enumerate(chips)]
        for j, chip in enumerate(chips):
            copy(1 + j, (*chip, c), me).wait_recv()
            passed[j].start()
        copy(0, sibling, me).wait_recv()
        for j, chip in enumerate(chips):
            copy(4 + j, (*chip, 1 - c), me).wait_recv()
        for cp in first + passed:
            cp.wait_send()
        mine.wait()
        acc = gath_ref[0:R, :]
        for d in range(1, N_DEV):
            acc = acc + gath_ref[d * R:(d + 1) * R, :]
        sum_ref[...] = acc

    vmem = pl.BlockSpec(memory_space=pltpu.VMEM)
    body, in_specs, args = _ordered_after(body, 1, [vmem], (packed,), after)
    return pl.pallas_call(
        body, name=name, in_specs=in_specs, out_specs=vmem,
        out_shape=jax.ShapeDtypeStruct((R, L), F32),
        scratch_shapes=[pltpu.VMEM((N_DEV * R, L), F32), pltpu.SemaphoreType.DMA((7,)), pltpu.SemaphoreType.DMA((7,)),
                        pltpu.SemaphoreType.DMA],
        compiler_params=pltpu.CompilerParams(vmem_limit_bytes=VMEM_LIMIT),
    )(*args)


def _adamw_math(w, g, m, v):
    m = ADAM_B1 * m + (1.0 - ADAM_B1) * g
    v = ADAM_B2 * v + (1.0 - ADAM_B2) * (g * g)
    m_hat = m / (1.0 - ADAM_B1 ** ADAM_STEP)
    v_hat = v / (1.0 - ADAM_B2 ** ADAM_STEP)
    delta = -ADAM_LR * (m_hat / (jnp.sqrt(v_hat) + ADAM_EPS) + ADAM_WD * w)
    return delta, m, v


def _adamw_shard(w, m, v, g8, from_sibling, from_chips, pos, *, name):
    R, C = w.shape
    tr = _div(R, 256, 16)

    def body(pos_ref, w_ref, m_ref, v_ref, g_ref, s_ref, r_ref, go_ref, d_ref, mo_ref, vo_ref):
        g = g_ref[...].astype(F32) + s_ref[...].astype(F32)
        for k in range(3):
            g = g + r_ref[k].astype(F32)
        delta, m_, v_ = _adamw_math(w_ref[...], g, m_ref[...], v_ref[...])
        go_ref[...] = g
        d_ref[...] = delta
        mo_ref[...] = m_
        vo_ref[...] = v_

    blk = pl.BlockSpec((tr, C), lambda i, pos_ref: (i, 0))
    grid_spec = pltpu.PrefetchScalarGridSpec(
        num_scalar_prefetch=1, grid=(R // tr,),
        in_specs=[blk, blk, blk,
                  pl.BlockSpec((None, tr, C), lambda i, pos_ref: (pos_ref[0], i, 0)),
                  pl.BlockSpec((None, tr, C), lambda i, pos_ref: (pos_ref[1], i, 0)),
                  pl.BlockSpec((3, tr, C), lambda i, pos_ref: (0, i, 0))],
        out_specs=[blk] * 4)
    out = jax.ShapeDtypeStruct((R, C), F32)
    return pl.pallas_call(
        body, name=name, grid_spec=grid_spec, out_shape=[out] * 4,
        compiler_params=_cparams("parallel"),
    )(pos, w, m, v, g8, from_sibling, from_chips)


def _adamw_small(w, g, m, v, *, name):
    R, L = w.shape

    def body(w_ref, g_ref, m_ref, v_ref, d_ref, mo_ref, vo_ref):
        delta, m_, v_ = _adamw_math(w_ref[...], g_ref[...], m_ref[...], v_ref[...])
        d_ref[...] = delta
        mo_ref[...] = m_
        vo_ref[...] = v_

    vmem = pl.BlockSpec(memory_space=pltpu.VMEM)
    out = jax.ShapeDtypeStruct((R, L), F32)
    return pl.pallas_call(body, name=name, in_specs=[vmem] * 4, out_specs=[vmem] * 3, out_shape=[out] * 3)(w, g, m, v)


_TILE = 8 * LANES


def _pack(pieces):
    rows = []
    for p in pieces:
        flat = p.reshape(-1).astype(F32)
        padded = -(-flat.shape[0] // _TILE) * _TILE
        rows.append(jnp.pad(flat, (0, padded - flat.shape[0])).reshape(-1, LANES))
    return jnp.concatenate(rows, axis=0)


def _unpack(packed, like):
    out, r = [], 0
    for p in like:
        size = int(np.prod(p.shape)) if p.shape else 1
        nrows = -(-size // _TILE) * 8
        out.append(packed[r:r + nrows].reshape(-1)[:size].reshape(p.shape))
        r += nrows
    return out


_BIG = ("w_in", "w_a_out", "w_b_out", "w_o", "w_gate", "w_up", "w_down")
_TRANSPOSED = ("w_in", "w_gate", "w_up")
_COL_SHARDED = ("w_a_out", "w_b_out")
_GATHER_GROUPS = (("w_in",), ("w_a_out", "w_b_out", "w_o"), ("w_gate",), ("w_up",), ("w_down",))
_START_AFTER_WAIT = {0: (1, 2), 1: (3,), 2: (4,)}
_SMALL = ("norm_mix", "sgu_v_gain", "sgu_w_s", "sgu_b_s", "attn_sink", "rel_bias", "norm_ffn", "norm_final")
_ORDER = ("w_in", "norm_mix", "sgu_v_gain", "sgu_w_s", "sgu_b_s", "w_a_out", "attn_sink", "rel_bias", "w_b_out", "w_o",
          "norm_ffn", "w_gate", "w_up", "w_down", "norm_final")


def _shard(name, a):
    return jnp.swapaxes(a, 1, 2)[0] if name in _TRANSPOSED else a[0]


def _unshard(name, a):
    return jnp.swapaxes(a[None], 1, 2) if name in _TRANSPOSED else a[None]


def _whole(name, gathered):
    _, r, c = gathered.shape
    return gathered if name in _COL_SHARDED else gathered.reshape(N_DEV * r, c)


def _blocks(name, grad):
    if name in _COL_SHARDED:
        return grad
    r, c = grad.shape
    return grad.reshape(N_DEV, r // N_DEV, c)


def kernel(x, w_in, norm_mix, sgu_v_gain, sgu_w_s, sgu_b_s, w_a_out, attn_sink, rel_bias, w_b_out, w_o, norm_ffn, w_gate, w_up, w_down, norm_final, loss_target, m_w_in, m_norm_mix, m_sgu_v_gain, m_sgu_w_s, m_sgu_b_s, m_w_a_out, m_attn_sink, m_rel_bias, m_w_b_out, m_w_o, m_norm_ffn, m_w_gate, m_w_up, m_w_down, m_norm_final, v_w_in, v_norm_mix, v_sgu_v_gain, v_sgu_w_s, v_sgu_b_s, v_w_a_out, v_attn_sink, v_rel_bias, v_w_b_out, v_w_o, v_norm_ffn, v_w_gate, v_w_up, v_w_down, v_norm_final):
    w = dict(w_in=w_in, norm_mix=norm_mix, sgu_v_gain=sgu_v_gain, sgu_w_s=sgu_w_s, sgu_b_s=sgu_b_s, w_a_out=w_a_out,
             attn_sink=attn_sink, rel_bias=rel_bias, w_b_out=w_b_out, w_o=w_o, norm_ffn=norm_ffn, w_gate=w_gate,
             w_up=w_up, w_down=w_down, norm_final=norm_final)
    m = dict(w_in=m_w_in, norm_mix=m_norm_mix, sgu_v_gain=m_sgu_v_gain, sgu_w_s=m_sgu_w_s, sgu_b_s=m_sgu_b_s,
             w_a_out=m_w_a_out, attn_sink=m_attn_sink, rel_bias=m_rel_bias, w_b_out=m_w_b_out, w_o=m_w_o,
             norm_ffn=m_norm_ffn, w_gate=m_w_gate, w_up=m_w_up, w_down=m_w_down, norm_final=m_norm_final)
    v = dict(w_in=v_w_in, norm_mix=v_norm_mix, sgu_v_gain=v_sgu_v_gain, sgu_w_s=v_sgu_w_s, sgu_b_s=v_sgu_b_s,
             w_a_out=v_w_a_out, attn_sink=v_attn_sink, rel_bias=v_rel_bias, w_b_out=v_w_b_out, w_o=v_w_o,
             norm_ffn=v_norm_ffn, w_gate=v_w_gate, w_up=v_w_up, w_down=v_w_down, norm_final=v_norm_final)
    xc, yc, cc = _position()
    pos = jnp.stack([_slot(xc, yc, cc), 2 * xc + yc, cc]).astype(jnp.int32)

    in_flight, full = {}, {}

    def start_gather(groups, after):
        names = [n for gi in groups for n in _GATHER_GROUPS[gi]]
        buffers = [_own_slot(_shard(n, w[n]), pos, name="own_slot_" + n, after=after) for n in names]
        flights, token = _ag_start(buffers, [[names.index(n) for n in _GATHER_GROUPS[gi]] for gi in groups],
                                   name="ag_start_%d" % groups[0])
        in_flight.update(zip(groups, flights))
        return token

    def weight(name, after):
        if name not in full:
            gi = next(i for i, grp in enumerate(_GATHER_GROUPS) if name in grp)
            send_sems, recv_sems, lands = in_flight[gi]
            lands, token = _ag_wait(send_sems, recv_sems, lands, after, name="ag_wait_%d" % gi)
            started = start_gather(_START_AFTER_WAIT[gi], token) if gi in _START_AFTER_WAIT else None
            gathered = _ag_forward(lands, name="ag_forward_%d" % gi, after=started)
            full.update({n: _whole(n, g) for n, g in zip(_GATHER_GROUPS[gi], gathered)})
        return full[name]

    start_gather((0,), None)

    to_sibling, reducing = [], {}

    def emit(names, grads):
        g8 = [_blocks(n, g) for n, g in zip(names, grads)]
        send_sems, recv_sems, g8, lands, token = _copies_start(_sibling_copies, 4, g8, name="rs_sibling_start_" + names[0])
        to_sibling.append((names, send_sems, recv_sems, g8, lands))
        return token

    def flush(after):
        names, send_sems, recv_sems, g8, lands = to_sibling.pop()
        g8, from_sibling = _copies_wait(_sibling_copies, send_sems, recv_sems, g8, lands, after,
                                        name="rs_sibling_wait_" + names[0])
        sums4 = [_chip_sums(g, s, pos, name="chip_sums_" + n) for n, g, s in zip(names, g8, from_sibling)]
        send_sems, recv_sems, sums4, lands, token = _copies_start(_chip_copies, 3, sums4, name="rs_chips_start_" + names[0])
        reducing[names] = (g8, from_sibling, send_sems, recv_sems, sums4, lands)
        return token

    loss, grad_x, small_grads_local = _local_step(
        x[0], loss_target[0], weight, emit, flush, norm_mix, sgu_v_gain, sgu_w_s[0], sgu_b_s[0], attn_sink, rel_bias,
        norm_ffn, norm_final[None])

    out_g, out_d, out_m, out_v = {}, {}, {}, {}
    small_like = [w[n] for n in _SMALL]
    small_w = _pack(small_like)
    packed = _pack([small_grads_local[n] for n in _SMALL] + [loss[0, 0]])
    after = grad_x
    for gi, (names, (g8, from_sibling, send_sems, recv_sems, sums4, lands)) in enumerate(reducing.items()):
        if gi == len(reducing) - 1:
            summed = _small_all_reduce(packed, after, name="small_all_reduce")
            after = summed
        _, from_chips = _copies_wait(_chip_copies, send_sems, recv_sems, sums4, lands, after,
                                     name="rs_chips_wait_" + names[0])
        for i, n in enumerate(names):
            g, d, m_, v_ = _adamw_shard(_shard(n, w[n]), _shard(n, m[n]), _shard(n, v[n]), g8[i], from_sibling[i],
                                        from_chips[i], pos, name="adamw_" + n)
            out_g[n], out_d[n], out_m[n], out_v[n] = (_unshard(n, o) for o in (g, d, m_, v_))
            after = d
    *small_grads, loss_sum = _unpack(summed, small_like + [jax.ShapeDtypeStruct((), F32)])
    d_s, m_s, v_s = _adamw_small(small_w, summed[:small_w.shape[0]], _pack([m[n] for n in _SMALL]),
                                 _pack([v[n] for n in _SMALL]), name="adamw_small")
    for n, g, d, m_, v_ in zip(_SMALL, small_grads, _unpack(d_s, small_like), _unpack(m_s, small_like), _unpack(v_s, small_like)):
        out_g[n], out_d[n], out_m[n], out_v[n] = g, d, m_, v_

    return (loss_sum, grad_x[None], *[out_g[n] for n in _ORDER], *[out_d[n] for n in _ORDER],
            *[out_m[n] for n in _ORDER], *[out_v[n] for n in _ORDER])
```

```python
import functools
import math

import numpy as np
import jax
import jax.numpy as jnp
from jax import lax
from jax.experimental import pallas as pl
from jax.experimental.pallas import tpu as pltpu

F32 = jnp.float32
BF16 = jnp.bfloat16

EPS = 1e-6
NEG = -1e30
HEAD_DIM = 128
BLOCK = 128
N_KV_HEADS = 2
KV_WIDTH = N_KV_HEADS * HEAD_DIM
REL_BUCKETS = 32
REL_MAX_DIST = 128

ADAM_LR = 0.001
ADAM_B1 = 0.9
ADAM_B2 = 0.999
ADAM_EPS = 1e-08
ADAM_WD = 0.01
ADAM_STEP = 10

N_DEV = 8
LANES = 128
VMEM_LIMIT = 56 * 1024 * 1024
MESH = pl.DeviceIdType.MESH


def _cparams(*sem):
    return pltpu.CompilerParams(dimension_semantics=sem, vmem_limit_bytes=VMEM_LIMIT)


def _div(n, target, mult=LANES):
    best = None
    for d in range(mult, min(n, target) + 1, mult):
        if n % d == 0:
            best = d
    assert best is not None, (n, target, mult)
    return best


_ANY = pl.BlockSpec(memory_space=pl.ANY)


def _ordered_after(body, n_inputs, in_specs, args, after):
    if after is None:
        return body, in_specs, args

    def wrapped(*refs):
        return body(*refs[:n_inputs], *refs[n_inputs + 1:])

    return wrapped, list(in_specs) + [_ANY], tuple(args) + (after,)


def _bucket_map():
    nb = REL_BUCKETS // 2
    qi = np.arange(BLOCK)[:, None]
    kj = np.arange(3 * BLOCK)[None, :]
    rel = kj - BLOCK - qi
    ret = np.where(rel > 0, nb, 0)
    n = np.abs(rel)
    max_exact = nb // 2
    nf = np.maximum(n, 1).astype(np.float32)
    large = max_exact + (np.log(nf / np.float32(max_exact)) / np.float32(math.log(REL_MAX_DIST / max_exact))
                         * np.float32(nb - max_exact)).astype(np.int32)
    large = np.minimum(large, nb - 1)
    return (ret + np.where(n < max_exact, n, large)).astype(np.int32)


_GELU_C = math.sqrt(2.0 / math.pi)
_GELU_A = 0.044715


def _gelu(x):
    t = jnp.tanh(_GELU_C * (x + _GELU_A * (x * x * x)))
    return 0.5 * x * (1.0 + t)


def _gelu_and_grad(x):
    x2 = x * x
    t = jnp.tanh(_GELU_C * (x + _GELU_A * (x2 * x)))
    g = 0.5 * x * (1.0 + t)
    dg = 0.5 * (1.0 + t) + 0.5 * x * (1.0 - t * t) * (_GELU_C * (1.0 + 3.0 * _GELU_A * x2))
    return g, dg


def _sigmoid(x):
    return 1.0 / (1.0 + jnp.exp(-x))


def _mm(a, b, *, name, ta=False, tb=False, add=None, out_dtype=F32, bm=1024, bn=1024, bk=None, after=None,
        row_blocks=None, into=None):
    if ta:
        K, M = a.shape
    else:
        M, K = a.shape
    N = b.shape[0] if tb else b.shape[1]
    assert (b.shape[1] if tb else b.shape[0]) == K
    bm = _div(M, bm)
    bn = _div(N, bn)
    bk = K if bk is None else _div(K, bk)
    nk = K // bk
    i0, ni = (0, M // bm) if row_blocks is None else row_blocks
    a_spec = (pl.BlockSpec((bk, bm), lambda i, j, k: (k, i + i0)) if ta
              else pl.BlockSpec((bm, bk), lambda i, j, k: (i + i0, k)))
    b_spec = pl.BlockSpec((bn, bk), lambda i, j, k: (j, k)) if tb else pl.BlockSpec((bk, bn), lambda i, j, k: (k, j))
    o_spec = pl.BlockSpec((bm, bn), lambda i, j, k: (i + i0, j))
    dims = (((0 if ta else 1,), (1 if tb else 0,)), ((), ()))
    has_add = add is not None

    def body(*refs):
        if has_add:
            a_ref, b_ref, add_ref, o_ref, *scratch = refs
        else:
            a_ref, b_ref, o_ref, *scratch = refs
            add_ref = None
        p = lax.dot_general(a_ref[...].astype(BF16), b_ref[...].astype(BF16), dims, preferred_element_type=F32)
        if nk == 1:
            if has_add:
                p = p + add_ref[...]
            o_ref[...] = p.astype(out_dtype)
        else:
            acc = scratch[0]
            k = pl.program_id(2)

            @pl.when(k == 0)
            def _():
                acc[...] = p

            @pl.when(k > 0)
            def _():
                acc[...] += p

            @pl.when(k == nk - 1)
            def _():
                r = acc[...]
                if has_add:
                    r = r + add_ref[...]
                o_ref[...] = r.astype(out_dtype)

    in_specs = [a_spec, b_spec] + ([o_spec] if has_add else [])
    args = (a, b) + ((add,) if has_add else ())
    aliases = {}
    if into is not None:
        body, in_specs, args = _ordered_after(body, len(args), in_specs, args, into)
        aliases = {len(args) - 1: 0}
    body, in_specs, args = _ordered_after(body, len(args), in_specs, args, after)
    return pl.pallas_call(
        body, name=name, grid=(ni, N // bn, nk),
        in_specs=in_specs, out_specs=o_spec,
        out_shape=jax.ShapeDtypeStruct((M, N), out_dtype),
        input_output_aliases=aliases,
        scratch_shapes=[pltpu.VMEM((bm, bn), F32)] if nk > 1 else [],
        compiler_params=_cparams("parallel", "parallel", "arbitrary"),
    )(*args)


def _mm_resid_rms(a, b, resid, gain, *, name, bm=512):
    M, K = a.shape
    N = b.shape[1]
    bm = _div(M, bm)

    def body(a_ref, b_ref, r_ref, g_ref, x_ref, h_ref):
        x = r_ref[...] + jnp.dot(a_ref[...], b_ref[...], preferred_element_type=F32)
        x_ref[...] = x
        r = lax.rsqrt(jnp.mean(x * x, axis=-1, keepdims=True) + EPS)
        h_ref[...] = ((x * r) * g_ref[...]).astype(BF16)

    row = pl.BlockSpec((bm, N), lambda i: (i, 0))
    return pl.pallas_call(
        body, name=name, grid=(M // bm,),
        in_specs=[pl.BlockSpec((bm, K), lambda i: (i, 0)), pl.BlockSpec((K, N), lambda i: (0, 0)), row,
                  pl.BlockSpec((1, N), lambda i: (0, 0))],
        out_specs=[row, row], out_shape=[jax.ShapeDtypeStruct((M, N), F32), jax.ShapeDtypeStruct((M, N), BF16)],
        compiler_params=_cparams("parallel"),
    )(a, b, resid, gain)


def _mm_sum2(a1, b1, a2, b2, *, name, bm=1024, bn=512, bk=2816, after=None):
    M, K = a1.shape
    N = b1.shape[1]
    bm, bn, bk = _div(M, bm), _div(N, bn), _div(K, bk)
    nk = K // bk

    def body(a1_ref, b1_ref, a2_ref, b2_ref, o_ref, acc):
        p = (jnp.dot(a1_ref[...], b1_ref[...], preferred_element_type=F32)
             + jnp.dot(a2_ref[...], b2_ref[...], preferred_element_type=F32))
        k = pl.program_id(2)

        @pl.when(k == 0)
        def _():
            acc[...] = p

        @pl.when(k > 0)
        def _():
            acc[...] += p

        @pl.when(k == nk - 1)
        def _():
            o_ref[...] = acc[...]

    a_spec = pl.BlockSpec((bm, bk), lambda i, j, k: (i, k))
    b_spec = pl.BlockSpec((bk, bn), lambda i, j, k: (k, j))
    body, in_specs, args = _ordered_after(body, 4, [a_spec, b_spec, a_spec, b_spec], (a1, b1, a2, b2), after)
    return pl.pallas_call(
        body, name=name, grid=(M // bm, N // bn, nk),
        in_specs=in_specs, out_specs=pl.BlockSpec((bm, bn), lambda i, j, k: (i, j)),
        out_shape=jax.ShapeDtypeStruct((M, N), F32),
        scratch_shapes=[pltpu.VMEM((bm, bn), F32)],
        compiler_params=_cparams("parallel", "parallel", "arbitrary"),
    )(*args)


def _blocks_per_tile(c):
    nb = 1
    while (nb * c) % LANES or (nb * c < 1024 and nb < N_DEV):
        nb *= 2
    assert nb <= N_DEV and (nb * c) % LANES == 0, c
    return nb


def _mm_w8(a, w8, *, name, bm=1024, out_dtype=F32):
    M, K = a.shape
    _, _, c = w8.shape
    nb = _blocks_per_tile(c)
    bm = _div(M, bm)

    def body(a_ref, w_ref, o_ref):
        a_ = a_ref[...]
        for t in range(nb):
            o_ref[:, t * c:(t + 1) * c] = jnp.dot(a_, w_ref[t], preferred_element_type=F32).astype(out_dtype)

    return pl.pallas_call(
        body, name=name, grid=(M // bm, N_DEV // nb),
        in_specs=[pl.BlockSpec((bm, K), lambda i, j: (i, 0)), pl.BlockSpec((nb, K, c), lambda i, j: (j, 0, 0))],
        out_specs=pl.BlockSpec((bm, nb * c), lambda i, j: (i, j)),
        out_shape=jax.ShapeDtypeStruct((M, N_DEV * c), out_dtype),
        compiler_params=_cparams("parallel", "parallel"),
    )(a, w8)


def _mm_w8t(dy, w8, *, name, add=None, out_dtype=F32, bm=1024, bn=1024, after=None, lead=None):
    M = dy.shape[-2]
    _, K, c = w8.shape
    nb = _blocks_per_tile(c)
    nk = N_DEV // nb
    bm, bn = _div(M, bm), _div(K, bn)
    has_add = add is not None
    dims = (((1,), (1,)), ((), ()))

    def body(*refs):
        if has_add:
            dy_ref, w_ref, add_ref, o_ref, acc = refs
        else:
            dy_ref, w_ref, o_ref, acc = refs
        p = lax.dot_general(dy_ref[:, 0:c], w_ref[0], dims, preferred_element_type=F32)
        for t in range(1, nb):
            p = p + lax.dot_general(dy_ref[:, t * c:(t + 1) * c], w_ref[t], dims, preferred_element_type=F32)
        k = pl.program_id(2)

        @pl.when(k == 0)
        def _():
            acc[...] = p

        @pl.when(k > 0)
        def _():
            acc[...] += p

        @pl.when(k == nk - 1)
        def _():
            r = acc[...]
            if has_add:
                r = r + add_ref[...]
            o_ref[...] = r.astype(out_dtype)

    o_spec = pl.BlockSpec((bm, bn), lambda i, j, k: (i, j))
    dy_spec = (pl.BlockSpec((bm, nb * c), lambda i, j, k: (i, k)) if lead is None
               else pl.BlockSpec((None, bm, nb * c), lambda i, j, k: (lead, i, k)))
    in_specs = [dy_spec, pl.BlockSpec((nb, bn, c), lambda i, j, k: (k, j, 0))]
    in_specs += [o_spec] if has_add else []
    args = (dy, w8) + ((add,) if has_add else ())
    body, in_specs, args = _ordered_after(body, len(args), in_specs, args, after)
    return pl.pallas_call(
        body, name=name, grid=(M // bm, K // bn, nk),
        in_specs=in_specs, out_specs=o_spec,
        out_shape=jax.ShapeDtypeStruct((M, K), out_dtype),
        scratch_shapes=[pltpu.VMEM((bm, bn), F32)],
        compiler_params=_cparams("parallel", "parallel", "arbitrary"),
    )(*args)


def _mm_gw8(x, dy, c, *, name, bk=1024, lead=None):
    T, K = x.shape
    nb = _blocks_per_tile(c)
    bk = _div(K, bk)
    dims = (((0,), (0,)), ((), ()))

    def body(x_ref, dy_ref, o_ref):
        x_ = x_ref[...]
        for t in range(nb):
            o_ref[t] = lax.dot_general(x_, dy_ref[:, t * c:(t + 1) * c], dims, preferred_element_type=F32).astype(BF16)

    dy_spec = (pl.BlockSpec((T, nb * c), lambda i, j: (0, j)) if lead is None
               else pl.BlockSpec((None, T, nb * c), lambda i, j: (lead, 0, j)))
    return pl.pallas_call(
        body, name=name, grid=(K // bk, N_DEV // nb),
        in_specs=[pl.BlockSpec((T, bk), lambda i, j: (0, i)), dy_spec],
        out_specs=pl.BlockSpec((nb, bk, c), lambda i, j: (j, i, 0)),
        out_shape=jax.ShapeDtypeStruct((N_DEV, K, c), BF16),
        compiler_params=_cparams("parallel", "parallel"),
    )(x, dy)


def _rms_fwd(x, g, *, name):
    T, D = x.shape
    tm = _div(T, 256, 8)

    def body(x_ref, g_ref, h_ref):
        xf = x_ref[...]
        r = lax.rsqrt(jnp.mean(xf * xf, axis=-1, keepdims=True) + EPS)
        h_ref[...] = ((xf * r) * g_ref[...]).astype(BF16)

    return pl.pallas_call(
        body, name=name, grid=(T // tm,),
        in_specs=[pl.BlockSpec((tm, D), lambda i: (i, 0)), pl.BlockSpec((1, D), lambda i: (0, 0))],
        out_specs=pl.BlockSpec((tm, D), lambda i: (i, 0)),
        out_shape=jax.ShapeDtypeStruct((T, D), BF16),
        compiler_params=_cparams("parallel"),
    )(x, g)


def _rms_bwd(x, g, dh, dres, *, name, want_bf16, after=None):
    T, D = x.shape
    tm = _div(T, 256, 8)

    def body(x_ref, g_ref, dh_ref, dres_ref, dx_ref, *rest):
        if want_bf16:
            dxb_ref, dg_ref = rest
        else:
            (dg_ref,) = rest
        xf = x_ref[...]
        r = lax.rsqrt(jnp.mean(xf * xf, axis=-1, keepdims=True) + EPS)
        xhat = xf * r
        dh_ = dh_ref[...]
        dy = dh_ * g_ref[...]
        dx = dres_ref[...] + r * (dy - xhat * jnp.mean(dy * xhat, axis=-1, keepdims=True))
        dx_ref[...] = dx
        if want_bf16:
            dxb_ref[...] = dx.astype(BF16)
        part = jnp.sum(dh_ * xhat, axis=0, keepdims=True)

        @pl.when(pl.program_id(0) == 0)
        def _():
            dg_ref[...] = part

        @pl.when(pl.program_id(0) > 0)
        def _():
            dg_ref[...] += part

    row = pl.BlockSpec((tm, D), lambda i: (i, 0))
    vec = pl.BlockSpec((1, D), lambda i: (0, 0))
    out_specs = [row] + ([row] if want_bf16 else []) + [vec]
    out_shape = ([jax.ShapeDtypeStruct((T, D), F32)] + ([jax.ShapeDtypeStruct((T, D), BF16)] if want_bf16 else [])
                 + [jax.ShapeDtypeStruct((1, D), F32)])
    body, in_specs, args = _ordered_after(body, 4, [row, vec, row, row], (x, g, dh, dres), after)
    return pl.pallas_call(
        body, name=name, grid=(T // tm,),
        in_specs=in_specs, out_specs=out_specs, out_shape=out_shape,
        compiler_params=_cparams("arbitrary"),
    )(*args)


def _loss_head(x, g, target, *, name):
    T, D = x.shape
    tm = _div(T, 256, 8)

    def body(x_ref, g_ref, t_ref, loss_ref, dx_ref, dxb_ref, dg_ref):
        xf = x_ref[...]
        r = lax.rsqrt(jnp.mean(xf * xf, axis=-1, keepdims=True) + EPS)
        xhat = xf * r
        gain = g_ref[...]
        err = xhat * gain - t_ref[...]
        lpart = 0.5 * jnp.sum(jnp.mean(err * err, axis=-1, keepdims=True), axis=0, keepdims=True)
        dh_ = err * (1.0 / D)
        dy = dh_ * gain
        dx = r * (dy - xhat * jnp.mean(dy * xhat, axis=-1, keepdims=True))
        dx_ref[...] = dx
        dxb_ref[...] = dx.astype(BF16)
        part = jnp.sum(dh_ * xhat, axis=0, keepdims=True)

        @pl.when(pl.program_id(0) == 0)
        def _():
            dg_ref[...] = part
            loss_ref[...] = jnp.broadcast_to(lpart, loss_ref.shape)

        @pl.when(pl.program_id(0) > 0)
        def _():
            dg_ref[...] += part
            loss_ref[...] += jnp.broadcast_to(lpart, loss_ref.shape)

    row = pl.BlockSpec((tm, D), lambda i: (i, 0))
    vec = pl.BlockSpec((1, D), lambda i: (0, 0))
    return pl.pallas_call(
        body, name=name, grid=(T // tm,),
        in_specs=[row, vec, row],
        out_specs=[pl.BlockSpec((8, LANES), lambda i: (0, 0)), row, row, vec],
        out_shape=[jax.ShapeDtypeStruct((8, LANES), F32), jax.ShapeDtypeStruct((T, D), F32),
                   jax.ShapeDtypeStruct((T, D), BF16), jax.ShapeDtypeStruct((1, D), F32)],
        compiler_params=_cparams("arbitrary"),
    )(x, g, target)


def _gate_cols(D):
    off_a = 3 * D // 2 + 2 * KV_WIDTH
    off_b = off_a + D
    cw = math.gcd(math.gcd(off_a, off_b), math.gcd(D, 512))
    return cw, off_a // cw, off_b // cw


def _merge_fwd(z, ya, yb, *, name):
    T, D = ya.shape
    cw, ba, bb = _gate_cols(D)
    tm = _div(T, 512, 8)

    def body(ga_ref, gb_ref, ya_ref, yb_ref, m_ref):
        m_ref[...] = (_sigmoid(ga_ref[...].astype(F32)) * ya_ref[...]
                      + _sigmoid(gb_ref[...].astype(F32)) * yb_ref[...]).astype(BF16)

    blk = pl.BlockSpec((tm, cw), lambda i, j: (i, j))
    return pl.pallas_call(
        body, name=name, grid=(T // tm, D // cw),
        in_specs=[pl.BlockSpec((tm, cw), lambda i, j: (i, ba + j)), pl.BlockSpec((tm, cw), lambda i, j: (i, bb + j)), blk, blk],
        out_specs=blk, out_shape=jax.ShapeDtypeStruct((T, D), BF16),
        compiler_params=_cparams("parallel", "parallel"),
    )(z, z, ya, yb)


def _merge_bwd(z, ya, yb, dm, *, name, after=None):
    T, D = ya.shape
    cw, ba, bb = _gate_cols(D)
    nj = D // cw
    assert bb == ba + nj
    tm = _div(T, 512, 8)

    def body(g_ref, ya_ref, yb_ref, dm_ref, dy_ref, dz_ref):
        sig = _sigmoid(g_ref[...].astype(F32))
        dm_ = dm_ref[...]
        y = jnp.where(pl.program_id(1) == 0, ya_ref[...], yb_ref[...])
        dy_ref[...] = (dm_ * sig).astype(BF16)
        dz_ref[...] = (dm_ * y * (sig * (1.0 - sig))).astype(BF16)

    in_specs = [pl.BlockSpec((tm, cw), lambda i, s, j: (i, ba + s * nj + j)),
                pl.BlockSpec((tm, cw), lambda i, s, j: (i, j * (1 - s))),
                pl.BlockSpec((tm, cw), lambda i, s, j: (i, j * s)),
                pl.BlockSpec((tm, cw), lambda i, s, j: (i, j))]
    body, in_specs, args = _ordered_after(body, 4, in_specs, (z, ya, yb, dm), after)
    return pl.pallas_call(
        body, name=name, grid=(T // tm, 2, nj),
        in_specs=in_specs,
        out_specs=[pl.BlockSpec((None, tm, cw), lambda i, s, j: (s, i, j)),
                   pl.BlockSpec((tm, cw), lambda i, s, j: (i, ba + s * nj + j))],
        out_shape=[jax.ShapeDtypeStruct((2, T, D), BF16), jax.ShapeDtypeStruct(z.shape, BF16)],
        compiler_params=_cparams("parallel", "arbitrary", "arbitrary"),
    )(*args)


def _swiglu_mm_fwd(h, wu_t, gate, *, name, bm=1024, bn=512):
    T, D = h.shape
    F = wu_t.shape[0]
    bm, bn = _div(T, bm), _div(F, bn)

    rc = _div(bm, 256, 16)

    def body(h_ref, wu_ref, gin_ref, g_ref, u_ref, act_ref):
        w = wu_ref[...]
        for r in range(0, bm, rc):
            rows = slice(r, r + rc)
            u = lax.dot_general(h_ref[rows, :], w, (((1,), (1,)), ((), ())), preferred_element_type=F32)
            g = gin_ref[rows, :]
            g_ref[rows, :] = g.astype(BF16)
            u_ref[rows, :] = u.astype(BF16)
            act_ref[rows, :] = (g * _sigmoid(g) * u).astype(BF16)

    o_spec = pl.BlockSpec((bm, bn), lambda i, j: (i, j))
    return pl.pallas_call(
        body, name=name, grid=(T // bm, F // bn),
        in_specs=[pl.BlockSpec((bm, D), lambda i, j: (i, 0)), pl.BlockSpec((bn, D), lambda i, j: (j, 0)), o_spec],
        out_specs=[o_spec] * 3, out_shape=[jax.ShapeDtypeStruct((T, F), BF16)] * 3,
        compiler_params=_cparams("parallel", "parallel"),
    )(h, wu_t, gate)


def _swiglu_mm_bwd(dx, w_down, gate, up, *, name, bm=2048, bn=512, after=None):
    T, D = dx.shape
    F = w_down.shape[0]
    bm, bn = _div(T, bm), _div(F, bn)
    dims = (((1,), (1,)), ((), ()))

    rc = _div(bm, 256, 16)

    def body(dx_ref, w_ref, g_ref, u_ref, dg_ref, du_ref):
        w = w_ref[...]
        for r in range(0, bm, rc):
            rows = slice(r, r + rc)
            d = lax.dot_general(dx_ref[rows, :], w, dims, preferred_element_type=F32)
            g = g_ref[rows, :].astype(F32)
            s = _sigmoid(g)
            silu = g * s
            dg_ref[rows, :] = (d * u_ref[rows, :].astype(F32) * (s + silu * (1.0 - s))).astype(BF16)
            du_ref[rows, :] = (d * silu).astype(BF16)

    o_spec = pl.BlockSpec((bm, bn), lambda i, j: (i, j))
    in_specs = [pl.BlockSpec((bm, D), lambda i, j: (i, 0)), pl.BlockSpec((bn, D), lambda i, j: (j, 0)), o_spec, o_spec]
    body, in_specs, args = _ordered_after(body, 4, in_specs, (dx, w_down, gate, up), after)
    out = jax.ShapeDtypeStruct((T, F), BF16)
    return pl.pallas_call(
        body, name=name, grid=(T // bm, F // bn), in_specs=in_specs, out_specs=[o_spec, o_spec], out_shape=[out, out],
        compiler_params=_cparams("parallel", "parallel"),
    )(*args)


def _sgu_fwd(z, gain, ws_b, bs_t, *, name):
    T = z.shape[0]
    SW = gain.shape[1]
    G = SW // BLOCK

    def body(zu_ref, zv_ref, gain_ref, ws_ref, bs_ref, a_ref):
        u = _gelu(zu_ref[...].astype(F32))
        vg = _gelu(zv_ref[...].astype(F32))
        r = lax.rsqrt(jnp.mean(vg * vg, axis=-1, keepdims=True) + EPS)
        vn = ((vg * r) * gain_ref[...]).astype(BF16)
        for g in range(G):
            sl = slice(g * BLOCK, (g + 1) * BLOCK)
            mixed = jnp.dot(ws_ref[g], vn[:, sl], preferred_element_type=F32) + bs_ref[:, g:g + 1]
            a_ref[:, sl] = (u[:, sl] * mixed).astype(BF16)

    return pl.pallas_call(
        body, name=name, grid=(T // BLOCK,),
        in_specs=[pl.BlockSpec((BLOCK, SW), lambda c: (c, 0)), pl.BlockSpec((BLOCK, SW), lambda c: (c, 1)),
                  pl.BlockSpec((1, SW), lambda c: (0, 0)), pl.BlockSpec((G, BLOCK, BLOCK), lambda c: (0, 0, 0)),
                  pl.BlockSpec((BLOCK, G), lambda c: (0, 0))],
        out_specs=pl.BlockSpec((BLOCK, SW), lambda c: (c, 0)),
        out_shape=jax.ShapeDtypeStruct((T, SW), BF16),
        compiler_params=_cparams("parallel"),
    )(z, z, gain, ws_b, bs_t)


def _sgu_bwd(z, gain, ws_b, bs_t, da, dz, *, name):
    T = z.shape[0]
    SW = gain.shape[1]
    G = SW // BLOCK

    def body(zu_ref, zv_ref, gain_ref, ws_ref, bs_ref, da_ref, dz_in_ref, dz_ref, dws_ref, dbs_ref, dgain_ref, dvn_ref):
        first = pl.program_id(0) == 0

        @pl.when(first)
        def _():
            dws_ref[...] = jnp.zeros_like(dws_ref)
            dbs_ref[...] = jnp.zeros_like(dbs_ref)
            dgain_ref[...] = jnp.zeros_like(dgain_ref)

        u, du = _gelu_and_grad(zu_ref[...].astype(F32))
        vg, dvg = _gelu_and_grad(zv_ref[...].astype(F32))
        r = lax.rsqrt(jnp.mean(vg * vg, axis=-1, keepdims=True) + EPS)
        xhat = vg * r
        gain_ = gain_ref[...]
        vn = (xhat * gain_).astype(BF16)
        da_ = da_ref[...]
        for g in range(G):
            sl = slice(g * BLOCK, (g + 1) * BLOCK)
            w = ws_ref[g]
            mixed = jnp.dot(w, vn[:, sl], preferred_element_type=F32) + bs_ref[:, g:g + 1]
            dmix = da_[:, sl] * u[:, sl]
            dz_ref[:, sl] = (da_[:, sl] * mixed * du[:, sl]).astype(BF16)
            dmb = dmix.astype(BF16)
            dws_ref[g] += lax.dot_general(dmb, vn[:, sl], (((1,), (1,)), ((), ())), preferred_element_type=F32)
            dbs_ref[:, g:g + 1] += jnp.sum(dmix, axis=-1, keepdims=True)
            dvn_ref[:, sl] = lax.dot_general(w, dmb, (((0,), (0,)), ((), ())), preferred_element_type=F32)
        dvn = dvn_ref[...]
        dgain_ref[...] += jnp.sum(dvn * xhat, axis=0, keepdims=True)
        dy = dvn * gain_
        dv_ = r * (dy - xhat * jnp.mean(dy * xhat, axis=-1, keepdims=True))
        dz_ref[:, SW:] = (dv_ * dvg).astype(BF16)

    row = pl.BlockSpec((BLOCK, SW), lambda c: (c, 0))
    return pl.pallas_call(
        body, name=name, grid=(T // BLOCK,),
        in_specs=[row, pl.BlockSpec((BLOCK, SW), lambda c: (c, 1)),
                  pl.BlockSpec((1, SW), lambda c: (0, 0)), pl.BlockSpec((G, BLOCK, BLOCK), lambda c: (0, 0, 0)),
                  pl.BlockSpec((BLOCK, G), lambda c: (0, 0)), row, _ANY],
        out_specs=[pl.BlockSpec((BLOCK, 2 * SW), lambda c: (c, 0)), pl.BlockSpec((G, BLOCK, BLOCK), lambda c: (0, 0, 0)),
                   pl.BlockSpec((BLOCK, G), lambda c: (0, 0)), pl.BlockSpec((1, SW), lambda c: (0, 0))],
        out_shape=[jax.ShapeDtypeStruct(dz.shape, dz.dtype),
                   jax.ShapeDtypeStruct((G, BLOCK, BLOCK), F32), jax.ShapeDtypeStruct((BLOCK, G), F32),
                   jax.ShapeDtypeStruct((1, SW), F32)],
        input_output_aliases={6: 0},
        scratch_shapes=[pltpu.VMEM((BLOCK, SW), F32)],
        compiler_params=_cparams("arbitrary"),
    )(z, z, gain, ws_b, bs_t, da, dz)


def _bias_table(rel_bias, bmap, *, name):
    H = rel_bias.shape[1]

    def body(rb_ref, bmap_ref, o_ref):
        bm_ = bmap_ref[...]
        for h in range(H):
            acc = jnp.zeros(bm_.shape, F32)
            for b in range(REL_BUCKETS):
                acc = jnp.where(bm_ == b, rb_ref[b, h], acc)
            o_ref[h] = acc

    return pl.pallas_call(
        body, name=name,
        in_specs=[pl.BlockSpec(memory_space=pltpu.SMEM), pl.BlockSpec(memory_space=pltpu.VMEM)],
        out_specs=pl.BlockSpec(memory_space=pltpu.VMEM),
        out_shape=jax.ShapeDtypeStruct((H, BLOCK, 3 * BLOCK), F32),
    )(rel_bias, bmap)


def _attn_probs(q_ref, kb, bias_ref, sink_ref, s_ref, n, T, group):
    H = s_ref.shape[0]
    for h in range(H):
        kv = h // group
        qh = q_ref[:, h * HEAD_DIM:(h + 1) * HEAD_DIM].astype(BF16)
        s_ref[h] = lax.dot_general(qh, kb[:, kv * HEAD_DIM:(kv + 1) * HEAD_DIM], (((1,), (1,)), ((), ())),
                                   preferred_element_type=F32)
    row = lax.broadcasted_iota(jnp.int32, (BLOCK, 3 * BLOCK), 0)
    col = lax.broadcasted_iota(jnp.int32, (BLOCK, 3 * BLOCK), 1)
    key_pos = n * BLOCK + col - BLOCK
    valid = (jnp.abs(col - BLOCK - row) <= BLOCK) & (key_pos >= 0) & (key_pos < T)
    s = s_ref[...] * (HEAD_DIM ** -0.5) + bias_ref[...]
    s = jnp.where(valid[None], s, NEG)
    sink = sink_ref[...]
    m = jnp.maximum(jnp.max(s, axis=-1, keepdims=True), sink)
    e = jnp.exp(s - m)
    es = jnp.exp(sink - m)
    inv = 1.0 / (jnp.sum(e, axis=-1, keepdims=True) + es)
    return e * inv, es * inv


def _attn_fwd(z, kpad, vpad, bias, sink, *, name):
    T = z.shape[0]
    H = bias.shape[0]
    AW = H * HEAD_DIM
    group = H // N_KV_HEADS

    def body(q_ref, k_ref, v_ref, bias_ref, sink_ref, o_ref, s_ref, p_ref):
        n = pl.program_id(0)
        start = pl.multiple_of(n * BLOCK, BLOCK)
        kb = k_ref[pl.ds(start, 3 * BLOCK), :]
        vb = v_ref[pl.ds(start, 3 * BLOCK), :]
        p, _ = _attn_probs(q_ref, kb, bias_ref, sink_ref, s_ref, n, T, group)
        p_ref[...] = p.astype(BF16)
        for h in range(H):
            kv = h // group
            o = jnp.dot(p_ref[h], vb[:, kv * HEAD_DIM:(kv + 1) * HEAD_DIM], preferred_element_type=F32)
            o_ref[:, h * HEAD_DIM:(h + 1) * HEAD_DIM] = o.astype(BF16)

    full_kv = pl.BlockSpec((T + 2 * BLOCK, KV_WIDTH), lambda n: (0, 0))
    return pl.pallas_call(
        body, name=name, grid=(T // BLOCK,),
        in_specs=[pl.BlockSpec((BLOCK, AW), lambda n: (n, 2)), full_kv, full_kv,
                  pl.BlockSpec((H, BLOCK, 3 * BLOCK), lambda n: (0, 0, 0)), pl.BlockSpec((H, 1, 1), lambda n: (0, 0, 0))],
        out_specs=pl.BlockSpec((BLOCK, AW), lambda n: (n, 0)),
        out_shape=jax.ShapeDtypeStruct((T, AW), BF16),
        scratch_shapes=[pltpu.VMEM((H, BLOCK, 3 * BLOCK), F32), pltpu.VMEM((H, BLOCK, 3 * BLOCK), BF16)],
        compiler_params=_cparams("parallel"),
    )(z, kpad, vpad, bias, sink)


def _attn_bwd(z, kpad, vpad, bias, sink, do, dz, *, name):
    T = z.shape[0]
    H = bias.shape[0]
    AW = H * HEAD_DIM
    group = H // N_KV_HEADS
    scale = HEAD_DIM ** -0.5

    def body(q_ref, k_ref, v_ref, bias_ref, sink_ref, do_ref, dz_in_ref, dq_ref, dk_ref, dv_ref, dbias_ref, dsink_ref,
             s_ref, dp_ref, p_ref, ds_ref):
        n = pl.program_id(0)

        @pl.when(n == 0)
        def _():
            dk_ref[...] = jnp.zeros_like(dk_ref)
            dv_ref[...] = jnp.zeros_like(dv_ref)
            dbias_ref[...] = jnp.zeros_like(dbias_ref)
            dsink_ref[...] = jnp.zeros_like(dsink_ref)

        start = pl.multiple_of(n * BLOCK, BLOCK)
        kb = k_ref[pl.ds(start, 3 * BLOCK), :]
        vb = v_ref[pl.ds(start, 3 * BLOCK), :]
        p, p_sink = _attn_probs(q_ref, kb, bias_ref, sink_ref, s_ref, n, T, group)
        s_ref[...] = p
        p_ref[...] = p.astype(BF16)
        for h in range(H):
            kv = h // group
            dp_ref[h] = lax.dot_general(do_ref[:, h * HEAD_DIM:(h + 1) * HEAD_DIM], vb[:, kv * HEAD_DIM:(kv + 1) * HEAD_DIM],
                                        (((1,), (1,)), ((), ())), preferred_element_type=F32)
        p = s_ref[...]
        dp = dp_ref[...]
        delta = jnp.sum(p * dp, axis=-1, keepdims=True)
        ds = p * (dp - delta)
        dbias_ref[...] += ds
        dsink_ref[...] += -(p_sink * delta)
        ds_ref[...] = ds.astype(BF16)
        for kv in range(N_KV_HEADS):
            ksl = slice(kv * HEAD_DIM, (kv + 1) * HEAD_DIM)
            dk_acc = jnp.zeros((3 * BLOCK, HEAD_DIM), F32)
            dv_acc = jnp.zeros((3 * BLOCK, HEAD_DIM), F32)
            for gi in range(group):
                h = kv * group + gi
                hsl = slice(h * HEAD_DIM, (h + 1) * HEAD_DIM)
                dsb = ds_ref[h]
                dq = jnp.dot(dsb, kb[:, ksl], preferred_element_type=F32) * scale
                dq_ref[:, hsl] = dq.astype(BF16)
                dk_acc = dk_acc + lax.dot_general(dsb, q_ref[:, hsl].astype(BF16), (((0,), (0,)), ((), ())),
                                                  preferred_element_type=F32)
                dv_acc = dv_acc + lax.dot_general(p_ref[h], do_ref[:, hsl], (((0,), (0,)), ((), ())),
                                                  preferred_element_type=F32)
            dk_ref[pl.ds(start, 3 * BLOCK), ksl] += dk_acc * scale
            dv_ref[pl.ds(start, 3 * BLOCK), ksl] += dv_acc

    full_kv = pl.BlockSpec((T + 2 * BLOCK, KV_WIDTH), lambda n: (0, 0))
    bias_spec = pl.BlockSpec((H, BLOCK, 3 * BLOCK), lambda n: (0, 0, 0))
    row = pl.BlockSpec((BLOCK, AW), lambda n: (n, 0))
    q_cols = pl.BlockSpec((BLOCK, AW), lambda n: (n, 2))
    band = (H, BLOCK, 3 * BLOCK)
    return pl.pallas_call(
        body, name=name, grid=(T // BLOCK,),
        in_specs=[q_cols, full_kv, full_kv, bias_spec, pl.BlockSpec((H, 1, 1), lambda n: (0, 0, 0)), row, _ANY],
        out_specs=[q_cols, full_kv, full_kv, bias_spec, pl.BlockSpec((H, BLOCK, 1), lambda n: (0, 0, 0))],
        out_shape=[jax.ShapeDtypeStruct(dz.shape, dz.dtype),
                   jax.ShapeDtypeStruct((T + 2 * BLOCK, KV_WIDTH), F32), jax.ShapeDtypeStruct((T + 2 * BLOCK, KV_WIDTH), F32),
                   jax.ShapeDtypeStruct(band, F32), jax.ShapeDtypeStruct((H, BLOCK, 1), F32)],
        input_output_aliases={6: 0},
        scratch_shapes=[pltpu.VMEM(band, F32), pltpu.VMEM(band, F32), pltpu.VMEM(band, BF16), pltpu.VMEM(band, BF16)],
        compiler_params=_cparams("arbitrary"),
    )(z, kpad, vpad, bias, sink, do, dz)


def _dkv_into(dkp, dvp, dz, *, name):
    T = dz.shape[0]
    D = (dz.shape[1] - 2 * KV_WIDTH) * 2 // 7
    col = (D + D // 2) // (2 * KV_WIDTH)
    assert col * 2 * KV_WIDTH == D + D // 2

    def body(dk_ref, dv_ref, dz_in_ref, o_ref):
        o_ref[:, :KV_WIDTH] = dk_ref[...].astype(BF16)
        o_ref[:, KV_WIDTH:] = dv_ref[...].astype(BF16)

    kv = pl.BlockSpec((BLOCK, KV_WIDTH), lambda n: (n + 1, 0))
    return pl.pallas_call(
        body, name=name, grid=(T // BLOCK,),
        in_specs=[kv, kv, _ANY], out_specs=pl.BlockSpec((BLOCK, 2 * KV_WIDTH), lambda n: (n, col)),
        out_shape=jax.ShapeDtypeStruct(dz.shape, dz.dtype), input_output_aliases={2: 0},
        compiler_params=_cparams("parallel"),
    )(dkp, dvp, dz)


def _kv_pad(z, *, name):
    T = z.shape[0]
    D = (z.shape[1] - 2 * KV_WIDTH) * 2 // 7
    kcol = (D + D // 2) // KV_WIDTH
    nb = T // BLOCK

    def body(k_ref, v_ref, ko_ref, vo_ref):
        b = pl.program_id(0)
        inside = (b >= 1) & (b <= nb)
        ko_ref[...] = jnp.where(inside, k_ref[...].astype(F32), 0.0).astype(BF16)
        vo_ref[...] = jnp.where(inside, v_ref[...].astype(F32), 0.0).astype(BF16)

    out = jax.ShapeDtypeStruct((T + 2 * BLOCK, KV_WIDTH), BF16)
    o_spec = pl.BlockSpec((BLOCK, KV_WIDTH), lambda b: (b, 0))
    return pl.pallas_call(
        body, name=name, grid=(nb + 2,),
        in_specs=[pl.BlockSpec((BLOCK, KV_WIDTH), lambda b: (jnp.clip(b - 1, 0, nb - 1), kcol)),
                  pl.BlockSpec((BLOCK, KV_WIDTH), lambda b: (jnp.clip(b - 1, 0, nb - 1), kcol + 1))],
        out_specs=[o_spec, o_spec], out_shape=[out, out],
        compiler_params=_cparams("parallel"),
    )(z, z)


def _attn_small_grads(dbias, dsink_rows, bmap, after, *, name):
    H = dbias.shape[0]

    def body(dbias_ref, dsink_ref, bmap_ref, drel_ref, ds_ref):
        bm_ = bmap_ref[...]
        for h in range(H):
            d = dbias_ref[h]
            for b in range(REL_BUCKETS):
                drel_ref[b, h] = jnp.sum(jnp.where(bm_ == b, d, 0.0))
            ds_ref[0, h] = jnp.sum(dsink_ref[h])

    vmem = pl.BlockSpec(memory_space=pltpu.VMEM)
    smem = pl.BlockSpec(memory_space=pltpu.SMEM)
    body, in_specs, args = _ordered_after(body, 3, [vmem, vmem, vmem], (dbias, dsink_rows, bmap), after)
    return pl.pallas_call(
        body, name=name, in_specs=in_specs, out_specs=[smem, smem],
        out_shape=[jax.ShapeDtypeStruct((REL_BUCKETS, H), F32), jax.ShapeDtypeStruct((1, H), F32)],
    )(*args)


def _local_step(x, target, weight, emit, flush, norm_mix, v_gain, w_s, b_s, sink, rel_bias, norm_ffn, norm_final):
    T, D = x.shape
    ws_b = w_s.astype(BF16)
    bs_t = b_s.T
    bmap = jnp.asarray(_bucket_map())
    sink = sink.reshape(-1, 1, 1)

    h = _rms_fwd(x, norm_mix, name="rms_mix")
    w_in = weight("w_in", h)
    z = _mm(h, w_in, tb=True, out_dtype=BF16, name="mm_z", bm=2048, bn=768)
    a = _sgu_fwd(z, v_gain, ws_b, bs_t, name="sgu_fwd")
    w_a = weight("w_a_out", a)
    ya = _mm_w8(a, w_a, name="mm_ya", bm=2048, out_dtype=BF16)
    kpad, vpad = _kv_pad(z, name="kv_pad")
    bias = _bias_table(rel_bias, bmap, name="bias_table")
    o = _attn_fwd(z, kpad, vpad, bias, sink, name="attn_fwd")
    w_b = weight("w_b_out", o)
    yb = _mm_w8(o, w_b, name="mm_yb", bm=2048, out_dtype=BF16)
    m = _merge_fwd(z, ya, yb, name="merge_fwd")
    w_o = weight("w_o", m)
    x1, h2 = _mm_resid_rms(m, w_o, x, norm_ffn, name="mm_x1_rms")
    w_gate = weight("w_gate", h2)
    gate = _mm(h2, w_gate, tb=True, name="mm_gate", bm=2048, bn=512)
    w_up = weight("w_up", gate)
    gate, up, act = _swiglu_mm_fwd(h2, w_up, gate, name="mm_up_swiglu")
    w_down = weight("w_down", act)
    x2 = _mm(act, w_down, name="mm_x2", add=x1, bm=1024, bn=512)
    loss, dx2, dx2b, g_norm_final = _loss_head(x2, norm_final, target, name="loss_head")

    g_w_down = _mm(act, dx2b, ta=True, out_dtype=BF16, name="mm_gwdown", bm=512, bn=2048)
    tok = emit(("w_down",), (g_w_down,))
    dgate, dup = _swiglu_mm_bwd(dx2b, w_down, gate, up, name="mm_dact_swiglu", after=tok)
    tok = flush(dgate)
    g_w_gate = _mm(dgate, h2, ta=True, out_dtype=BF16, name="mm_gwgate", bm=512, bn=2048, after=tok)
    g_w_up = _mm(dup, h2, ta=True, out_dtype=BF16, name="mm_gwup", bm=512, bn=2048)
    tok = emit(("w_gate", "w_up"), (g_w_gate, g_w_up))
    dh2 = _mm_sum2(dgate, w_gate, dup, w_up, name="mm_dh2", after=tok)
    tok = flush(dh2)
    dx1, dx1b, g_norm_ffn = _rms_bwd(x1, norm_ffn, dh2, dx2, name="rms_ffn_bwd", want_bf16=True, after=tok)

    g_w_o = _mm(m, dx1b, ta=True, out_dtype=BF16, name="mm_gwo", bm=2048, bn=512)
    tok = emit(("w_o",), (g_w_o,))
    dm = _mm(dx1b, w_o, tb=True, name="mm_dm", bm=2048, bn=512, after=tok)
    tok = flush(dm)
    dy, dz = _merge_bwd(z, ya, yb, dm, name="merge_bwd", after=tok)
    g_w_a = _mm_gw8(a, dy, w_a.shape[2], name="mm_gwa", lead=0)
    g_w_b = _mm_gw8(o, dy, w_b.shape[2], name="mm_gwb", lead=1)
    tok = emit(("w_a_out", "w_b_out"), (g_w_a, g_w_b))
    da = _mm_w8t(dy, w_a, name="mm_da", bm=2048, bn=512, after=tok, lead=0)
    tok = flush(da)
    do = _mm_w8t(dy, w_b, out_dtype=BF16, name="mm_do", bm=2048, bn=512, after=tok, lead=1)
    dz, g_w_s, g_b_s_t, g_v_gain = _sgu_bwd(z, v_gain, ws_b, bs_t, da, dz, name="sgu_bwd")
    dz, dkp, dvp, dbias, dsink_rows = _attn_bwd(z, kpad, vpad, bias, sink, do, dz, name="attn_bwd")
    dz = _dkv_into(dkp, dvp, dz, name="dkv_into_dz")
    g_w_in = _mm(dz, h, ta=True, out_dtype=BF16, name="mm_gwin", bm=768, bn=2048)
    tok = emit(("w_in",), (g_w_in,))
    half = dict(bm=T // 2, bn=256)
    dh = _mm(dz, w_in, name="mm_dh_top", row_blocks=(0, 1), after=tok, **half)
    tok = flush(dh)
    dh = _mm(dz, w_in, name="mm_dh_bottom", row_blocks=(1, 1), into=dh, after=tok, **half)
    g_rel_bias, g_sink = _attn_small_grads(dbias, dsink_rows, bmap, dh, name="attn_small_grads")
    grad_x, g_norm_mix = _rms_bwd(x, norm_mix, dh, dx1, name="rms_mix_bwd", want_bf16=False)

    small = dict(norm_mix=g_norm_mix, sgu_v_gain=g_v_gain, sgu_w_s=g_w_s, sgu_b_s=g_b_s_t.T, attn_sink=g_sink,
                 rel_bias=g_rel_bias, norm_ffn=g_norm_ffn, norm_final=g_norm_final)
    return loss, grad_x, small


def _position():
    return lax.axis_index("x"), lax.axis_index("y"), lax.axis_index("c")


def _other_chips(x, y):
    return [(1 - x, y), (x, 1 - y), (1 - x, 1 - y)]


def _slot(px, py, pc):
    return 4 * px + 2 * py + pc


_HBM = pl.BlockSpec(memory_space=pltpu.HBM)
_SEM = pl.BlockSpec(memory_space=pltpu.SEMAPHORE)
_DATAFLOW = pltpu.SideEffectType.DATAFLOW_SIDE_EFFECTING


def _in_hbm(a):
    return pltpu.with_memory_space_constraint(a, pltpu.HBM)


def _own_slot(shard, pos, *, name, after=None):
    R, C = shard.shape
    tr = _div(R, 256, 16)

    def body(pos_ref, w_ref, o_ref):
        o_ref[...] = w_ref[...].astype(BF16)

    body, in_specs, args = _ordered_after(body, 2, [pl.BlockSpec((tr, C), lambda i, pos_ref: (i, 0))], (pos, shard), after)
    grid_spec = pltpu.PrefetchScalarGridSpec(
        num_scalar_prefetch=1, grid=(R // tr,), in_specs=in_specs,
        out_specs=pl.BlockSpec((None, tr, C), lambda i, pos_ref: (pos_ref[0], i, 0)))
    return pl.pallas_call(
        body, name=name, grid_spec=grid_spec,
        out_shape=jax.ShapeDtypeStruct((N_DEV, R, C), BF16),
        compiler_params=_cparams("parallel"),
    )(*args)


def _ag_copies(w, land_ref, send_sems, recv_sems):
    x, y, c = _position()
    mine = land_ref.at[_slot(x, y, c)]
    targets = [(px, py, c) for px, py in _other_chips(x, y)] + [(x, y, 1 - c)]
    return [pltpu.make_async_remote_copy(src_ref=mine, dst_ref=mine, send_sem=send_sems.at[4 * w + k],
                                         recv_sem=recv_sems.at[4 * w + k], device_id=to, device_id_type=MESH)
            for k, to in enumerate(targets)]


def _ag_start(buffers, groups, *, name):
    lands = [buffers[i] for g in groups for i in g]
    n, ng = len(lands), len(groups)
    sizes = [len(g) for g in groups]

    def body(*refs):
        land_refs = refs[:n]
        sems = refs[n:n + 2 * ng]
        token = refs[-1]
        i = 0
        for g in range(ng):
            for w in range(sizes[g]):
                for cp in _ag_copies(w, land_refs[i], sems[2 * g], sems[2 * g + 1]):
                    cp.start()
                i += 1
        token[...] = jnp.zeros_like(token)

    sem_shapes = [pltpu.SemaphoreType.DMA((4 * k,)) for k in sizes for _ in range(2)]
    outs = pl.pallas_call(
        body, name=name,
        in_specs=[_HBM] * n,
        out_specs=tuple([_SEM] * (2 * ng) + [_HBM] * n + [pl.BlockSpec(memory_space=pltpu.VMEM)]),
        out_shape=tuple(sem_shapes + [pltpu.HBM(a.shape, a.dtype) for a in lands] + [jax.ShapeDtypeStruct((8, LANES), F32)]),
        input_output_aliases={i: 2 * ng + i for i in range(n)},
        compiler_params=pltpu.CompilerParams(has_side_effects=_DATAFLOW),
    )(*[_in_hbm(a) for a in lands])
    sems, thru = outs[:2 * ng], outs[2 * ng:2 * ng + n]
    result, i = [], 0
    for g in range(ng):
        k = sizes[g]
        result.append((sems[2 * g], sems[2 * g + 1], list(thru[i:i + k])))
        i += k
    return result, outs[-1]


def _ag_wait(send_sems, recv_sems, lands, after, *, name):
    n = len(lands)

    def body(*refs):
        land_refs = refs[:n]
        send_ref, recv_ref = refs[n], refs[n + 1]
        token = refs[-1]
        for w in range(n):
            for cp in _ag_copies(w, land_refs[w], send_ref, recv_ref):
                cp.wait_send()
                cp.wait_recv()
        token[...] = jnp.zeros_like(token)

    outs = pl.pallas_call(
        body, name=name,
        in_specs=[_HBM] * n + [_SEM, _SEM, _ANY],
        out_specs=tuple([_HBM] * n + [pl.BlockSpec(memory_space=pltpu.VMEM)]),
        out_shape=tuple([pltpu.HBM(a.shape, a.dtype) for a in lands] + [jax.ShapeDtypeStruct((8, LANES), F32)]),
        input_output_aliases={i: i for i in range(n)},
        compiler_params=pltpu.CompilerParams(has_side_effects=_DATAFLOW),
    )(*lands, send_sems, recv_sems, after)
    return list(outs[:n]), outs[n]


def _ag_forward(lands, *, name, after=None):
    n = len(lands)

    def body(*refs):
        in_refs, out_refs = refs[:n], refs[n:2 * n]
        send_sems, recv_sems = refs[2 * n:]
        x, y, c = _position()
        copies = []
        for w in range(n):
            for k, (px, py) in enumerate(_other_chips(x, y)):
                cp = pltpu.make_async_remote_copy(
                    src_ref=in_refs[w].at[_slot(px, py, c)], dst_ref=out_refs[w].at[_slot(px, py, c)],
                    send_sem=send_sems.at[3 * w + k], recv_sem=recv_sems.at[3 * w + k],
                    device_id=(x, y, 1 - c), device_id_type=MESH)
                cp.start()
                copies.append(cp)
        for cp in copies:
            cp.wait()

    body, in_specs, args = _ordered_after(body, n, [_ANY] * n, tuple(lands), after)
    return pl.pallas_call(
        body, name=name,
        in_specs=in_specs, out_specs=[_ANY] * n,
        out_shape=[jax.ShapeDtypeStruct(a.shape, a.dtype) for a in lands],
        input_output_aliases={i: i for i in range(n)},
        scratch_shapes=[pltpu.SemaphoreType.DMA((3 * n,)), pltpu.SemaphoreType.DMA((3 * n,))],
    )(*args)


def _sibling_copies(w, g8_ref, land_ref, send_sems, recv_sems):
    x, y, c = _position()
    return [pltpu.make_async_remote_copy(src_ref=g8_ref.at[2 * p + (1 - c)], dst_ref=land_ref.at[p],
                                         send_sem=send_sems.at[4 * w + p], recv_sem=recv_sems.at[4 * w + p],
                                         device_id=(x, y, 1 - c), device_id_type=MESH)
            for p in range(4)]


def _chip_copies(w, sums_ref, land_ref, send_sems, recv_sems):
    x, y, c = _position()
    return [pltpu.make_async_remote_copy(src_ref=sums_ref.at[2 * px + py], dst_ref=land_ref.at[k],
                                         send_sem=send_sems.at[3 * w + k], recv_sem=recv_sems.at[3 * w + k],
                                         device_id=(px, py, c), device_id_type=MESH)
            for k, (px, py) in enumerate(_other_chips(x, y))]


def _copies_start(copies, per_weight, srcs, *, name):
    n = len(srcs)
    lands = [lax.empty((per_weight,) + s.shape[1:], s.dtype) for s in srcs]

    def body(*refs):
        src_refs, land_refs = refs[:n], refs[n:2 * n]
        send_sems, recv_sems = refs[2 * n], refs[2 * n + 1]
        token = refs[-1]
        for w in range(n):
            for cp in copies(w, src_refs[w], land_refs[w], send_sems, recv_sems):
                cp.start()
        token[...] = jnp.zeros_like(token)

    outs = pl.pallas_call(
        body, name=name,
        in_specs=[_HBM] * (2 * n),
        out_specs=tuple([_SEM, _SEM] + [_HBM] * (2 * n) + [pl.BlockSpec(memory_space=pltpu.VMEM)]),
        out_shape=tuple([pltpu.SemaphoreType.DMA((per_weight * n,)), pltpu.SemaphoreType.DMA((per_weight * n,))]
                        + [pltpu.HBM(a.shape, a.dtype) for a in srcs + lands] + [jax.ShapeDtypeStruct((8, LANES), F32)]),
        input_output_aliases={i: 2 + i for i in range(2 * n)},
        compiler_params=pltpu.CompilerParams(has_side_effects=_DATAFLOW),
    )(*[_in_hbm(a) for a in srcs + lands])
    return outs[0], outs[1], list(outs[2:2 + n]), list(outs[2 + n:2 + 2 * n]), outs[-1]


def _copies_wait(copies, send_sems, recv_sems, srcs, lands, after, *, name):
    n = len(srcs)

    def body(*refs):
        src_refs, land_refs = refs[:n], refs[n:2 * n]
        send_ref, recv_ref = refs[2 * n], refs[2 * n + 1]
        for w in range(n):
            for cp in copies(w, src_refs[w], land_refs[w], send_ref, recv_ref):
                cp.wait_send()
                cp.wait_recv()

    outs = pl.pallas_call(
        body, name=name,
        in_specs=[_HBM] * (2 * n) + [_SEM, _SEM, _ANY],
        out_specs=tuple([_HBM] * (2 * n)),
        out_shape=tuple(pltpu.HBM(a.shape, a.dtype) for a in srcs + lands),
        input_output_aliases={i: i for i in range(2 * n)},
        compiler_params=pltpu.CompilerParams(has_side_effects=_DATAFLOW),
    )(*srcs, *lands, send_sems, recv_sems, after)
    return list(outs[:n]), list(outs[n:])


def _chip_sums(g8, from_sibling, pos, *, name):
    _, R, C = g8.shape
    tr = _div(R, 512, 16)

    def body(pos_ref, g_ref, s_ref, o_ref):
        o_ref[...] = (g_ref[...].astype(F32) + s_ref[...].astype(F32)).astype(BF16)

    def chip(k, pos_ref):
        return jnp.where(k >= pos_ref[1], k + 1, k)

    grid_spec = pltpu.PrefetchScalarGridSpec(
        num_scalar_prefetch=1, grid=(3, R // tr),
        in_specs=[pl.BlockSpec((None, tr, C), lambda k, i, pos_ref: (2 * chip(k, pos_ref) + pos_ref[2], i, 0)),
                  pl.BlockSpec((None, tr, C), lambda k, i, pos_ref: (chip(k, pos_ref), i, 0))],
        out_specs=pl.BlockSpec((None, tr, C), lambda k, i, pos_ref: (chip(k, pos_ref), i, 0)))
    return pl.pallas_call(
        body, name=name, grid_spec=grid_spec,
        out_shape=jax.ShapeDtypeStruct((4, R, C), BF16),
        compiler_params=_cparams("parallel", "parallel"),
    )(pos, g8, from_sibling)


def _small_all_reduce(packed, after, *, name):
    R, L = packed.shape

    def body(x_ref, sum_ref, gath_ref, send_sems, recv_sems, local_sem):
        x, y, c = _position()
        me, sibling = (x, y, c), (x, y, 1 - c)
        chips = _other_chips(x, y)

        def rows(px, py, pc):
            return gath_ref.at[pl.ds(_slot(px, py, pc) * R, R), :]

        def copy(k, block, to, src=None):
            return pltpu.make_async_remote_copy(
                src_ref=rows(*block) if src is None else src, dst_ref=rows(*block),
                send_sem=send_sems.at[k], recv_sem=recv_sems.at[k], device_id=to, device_id_type=MESH)

        mine = pltpu.make_async_copy(x_ref, rows(*me), local_sem)
        mine.start()
        first = [copy(0, me, sibling, src=x_ref)]
        first += [copy(1 + j, me, (*chip, c), src=x_ref) for j, chip in enumerate(chips)]
        for cp in first:
            cp.start()
        passed = [copy(4 + j, (*chip, c), sibling) for j, chip in enumerate(chips)]
        for j, chip in enumerate(chips):
            copy(1 + j, (*chip, c), me).wait_recv()
            passed[j].start()
        copy(0, sibling, me).wait_recv()
        for j, chip in enumerate(chips):
            copy(4 + j, (*chip, 1 - c), me).wait_recv()
        for cp in first + passed:
            cp.wait_send()
        mine.wait()
        acc = gath_ref[0:R, :]
        for d in range(1, N_DEV):
            acc = acc + gath_ref[d * R:(d + 1) * R, :]
        sum_ref[...] = acc

    vmem = pl.BlockSpec(memory_space=pltpu.VMEM)
    body, in_specs, args = _ordered_after(body, 1, [vmem], (packed,), after)
    return pl.pallas_call(
        body, name=name, in_specs=in_specs, out_specs=vmem,
        out_shape=jax.ShapeDtypeStruct((R, L), F32),
        scratch_shapes=[pltpu.VMEM((N_DEV * R, L), F32), pltpu.SemaphoreType.DMA((7,)), pltpu.SemaphoreType.DMA((7,)),
                        pltpu.SemaphoreType.DMA],
        compiler_params=pltpu.CompilerParams(vmem_limit_bytes=VMEM_LIMIT),
    )(*args)


def _adamw_math(w, g, m, v):
    m = ADAM_B1 * m + (1.0 - ADAM_B1) * g
    v = ADAM_B2 * v + (1.0 - ADAM_B2) * (g * g)
    m_hat = m / (1.0 - ADAM_B1 ** ADAM_STEP)
    v_hat = v / (1.0 - ADAM_B2 ** ADAM_STEP)
    delta = -ADAM_LR * (m_hat / (jnp.sqrt(v_hat) + ADAM_EPS) + ADAM_WD * w)
    return delta, m, v


def _adamw_shard(w, m, v, g8, from_sibling, from_chips, pos, *, name):
    R, C = w.shape
    tr = _div(R, 256, 16)

    def body(pos_ref, w_ref, m_ref, v_ref, g_ref, s_ref, r_ref, go_ref, d_ref, mo_ref, vo_ref):
        g = g_ref[...].astype(F32) + s_ref[...].astype(F32)
        for k in range(3):
            g = g + r_ref[k].astype(F32)
        delta, m_, v_ = _adamw_math(w_ref[...], g, m_ref[...], v_ref[...])
        go_ref[...] = g
        d_ref[...] = delta
        mo_ref[...] = m_
        vo_ref[...] = v_

    blk = pl.BlockSpec((tr, C), lambda i, pos_ref: (i, 0))
    grid_spec = pltpu.PrefetchScalarGridSpec(
        num_scalar_prefetch=1, grid=(R // tr,),
        in_specs=[blk, blk, blk,
                  pl.BlockSpec((None, tr, C), lambda i, pos_ref: (pos_ref[0], i, 0)),
                  pl.BlockSpec((None, tr, C), lambda i, pos_ref: (pos_ref[1], i, 0)),
                  pl.BlockSpec((3, tr, C), lambda i, pos_ref: (0, i, 0))],
        out_specs=[blk] * 4)
    out = jax.ShapeDtypeStruct((R, C), F32)
    return pl.pallas_call(
        body, name=name, grid_spec=grid_spec, out_shape=[out] * 4,
        compiler_params=_cparams("parallel"),
    )(pos, w, m, v, g8, from_sibling, from_chips)


def _adamw_small(w, g, m, v, *, name):
    R, L = w.shape

    def body(w_ref, g_ref, m_ref, v_ref, d_ref, mo_ref, vo_ref):
        delta, m_, v_ = _adamw_math(w_ref[...], g_ref[...], m_ref[...], v_ref[...])
        d_ref[...] = delta
        mo_ref[...] = m_
        vo_ref[...] = v_

    vmem = pl.BlockSpec(memory_space=pltpu.VMEM)
    out = jax.ShapeDtypeStruct((R, L), F32)
    return pl.pallas_call(body, name=name, in_specs=[vmem] * 4, out_specs=[vmem] * 3, out_shape=[out] * 3)(w, g, m, v)


_TILE = 8 * LANES


def _pack(pieces):
    rows = []
    for p in pieces:
        flat = p.reshape(-1).astype(F32)
        padded = -(-flat.shape[0] // _TILE) * _TILE
        rows.append(jnp.pad(flat, (0, padded - flat.shape[0])).reshape(-1, LANES))
    return jnp.concatenate(rows, axis=0)


def _unpack(packed, like):
    out, r = [], 0
    for p in like:
        size = int(np.prod(p.shape)) if p.shape else 1
        nrows = -(-size // _TILE) * 8
        out.append(packed[r:r + nrows].reshape(-1)[:size].reshape(p.shape))
        r += nrows
    return out


_BIG = ("w_in", "w_a_out", "w_b_out", "w_o", "w_gate", "w_up", "w_down")
_TRANSPOSED = ("w_in", "w_gate", "w_up")
_COL_SHARDED = ("w_a_out", "w_b_out")
_GATHER_GROUPS = (("w_in",), ("w_a_out", "w_b_out", "w_o"), ("w_gate",), ("w_up",), ("w_down",))
_START_AFTER_WAIT = {0: (1, 2), 1: (3,), 2: (4,)}
_SMALL = ("norm_mix", "sgu_v_gain", "sgu_w_s", "sgu_b_s", "attn_sink", "rel_bias", "norm_ffn", "norm_final")
_ORDER = ("w_in", "norm_mix", "sgu_v_gain", "sgu_w_s", "sgu_b_s", "w_a_out", "attn_sink", "rel_bias", "w_b_out", "w_o",
          "norm_ffn", "w_gate", "w_up", "w_down", "norm_final")


def _shard(name, a):
    return jnp.swapaxes(a, 1, 2)[0] if name in _TRANSPOSED else a[0]


def _unshard(name, a):
    return jnp.swapaxes(a[None], 1, 2) if name in _TRANSPOSED else a[None]


def _whole(name, gathered):
    _, r, c = gathered.shape
    return gathered if name in _COL_SHARDED else gathered.reshape(N_DEV * r, c)


def _blocks(name, grad):
    if name in _COL_SHARDED:
        return grad
    r, c = grad.shape
    return grad.reshape(N_DEV, r // N_DEV, c)


def kernel(x, w_in, norm_mix, sgu_v_gain, sgu_w_s, sgu_b_s, w_a_out, attn_sink, rel_bias, w_b_out, w_o, norm_ffn, w_gate, w_up, w_down, norm_final, loss_target, m_w_in, m_norm_mix, m_sgu_v_gain, m_sgu_w_s, m_sgu_b_s, m_w_a_out, m_attn_sink, m_rel_bias, m_w_b_out, m_w_o, m_norm_ffn, m_w_gate, m_w_up, m_w_down, m_norm_final, v_w_in, v_norm_mix, v_sgu_v_gain, v_sgu_w_s, v_sgu_b_s, v_w_a_out, v_attn_sink, v_rel_bias, v_w_b_out, v_w_o, v_norm_ffn, v_w_gate, v_w_up, v_w_down, v_norm_final):
    w = dict(w_in=w_in, norm_mix=norm_mix, sgu_v_gain=sgu_v_gain, sgu_w_s=sgu_w_s, sgu_b_s=sgu_b_s, w_a_out=w_a_out,
             attn_sink=attn_sink, rel_bias=rel_bias, w_b_out=w_b_out, w_o=w_o, norm_ffn=norm_ffn, w_gate=w_gate,
             w_up=w_up, w_down=w_down, norm_final=norm_final)
    m = dict(w_in=m_w_in, norm_mix=m_norm_mix, sgu_v_gain=m_sgu_v_gain, sgu_w_s=m_sgu_w_s, sgu_b_s=m_sgu_b_s,
             w_a_out=m_w_a_out, attn_sink=m_attn_sink, rel_bias=m_rel_bias, w_b_out=m_w_b_out, w_o=m_w_o,
             norm_ffn=m_norm_ffn, w_gate=m_w_gate, w_up=m_w_up, w_down=m_w_down, norm_final=m_norm_final)
    v = dict(w_in=v_w_in, norm_mix=v_norm_mix, sgu_v_gain=v_sgu_v_gain, sgu_w_s=v_sgu_w_s, sgu_b_s=v_sgu_b_s,
             w_a_out=v_w_a_out, attn_sink=v_attn_sink, rel_bias=v_rel_bias, w_b_out=v_w_b_out, w_o=v_w_o,
             norm_ffn=v_norm_ffn, w_gate=v_w_gate, w_up=v_w_up, w_down=v_w_down, norm_final=v_norm_final)
    xc, yc, cc = _position()
    pos = jnp.stack([_slot(xc, yc, cc), 2 * xc + yc, cc]).astype(jnp.int32)

    in_flight, full = {}, {}

    def start_gather(groups, after):
        names = [n for gi in groups for n in _GATHER_GROUPS[gi]]
        buffers = [_own_slot(_shard(n, w[n]), pos, name="own_slot_" + n, after=after) for n in names]
        flights, token = _ag_start(buffers, [[names.index(n) for n in _GATHER_GROUPS[gi]] for gi in groups],
                                   name="ag_start_%d" % groups[0])
        in_flight.update(zip(groups, flights))
        return token

    def weight(name, after):
        if name not in full:
            gi = next(i for i, grp in enumerate(_GATHER_GROUPS) if name in grp)
            send_sems, recv_sems, lands = in_flight[gi]
            lands, token = _ag_wait(send_sems, recv_sems, lands, after, name="ag_wait_%d" % gi)
            started = start_gather(_START_AFTER_WAIT[gi], token) if gi in _START_AFTER_WAIT else None
            gathered = _ag_forward(lands, name="ag_forward_%d" % gi, after=started)
            full.update({n: _whole(n, g) for n, g in zip(_GATHER_GROUPS[gi], gathered)})
        return full[name]

    start_gather((0,), None)

    to_sibling, reducing = [], {}

    def emit(names, grads):
        g8 = [_blocks(n, g) for n, g in zip(names, grads)]
        send_sems, recv_sems, g8, lands, token = _copies_start(_sibling_copies, 4, g8, name="rs_sibling_start_" + names[0])
        to_sibling.append((names, send_sems, recv_sems, g8, lands))
        return token

    def flush(after):
        names, send_sems, recv_sems, g8, lands = to_sibling.pop()
        g8, from_sibling = _copies_wait(_sibling_copies, send_sems, recv_sems, g8, lands, after,
                                        name="rs_sibling_wait_" + names[0])
        sums4 = [_chip_sums(g, s, pos, name="chip_sums_" + n) for n, g, s in zip(names, g8, from_sibling)]
        send_sems, recv_sems, sums4, lands, token = _copies_start(_chip_copies, 3, sums4, name="rs_chips_start_" + names[0])
        reducing[names] = (g8, from_sibling, send_sems, recv_sems, sums4, lands)
        return token

    loss, grad_x, small_grads_local = _local_step(
        x[0], loss_target[0], weight, emit, flush, norm_mix, sgu_v_gain, sgu_w_s[0], sgu_b_s[0], attn_sink, rel_bias,
        norm_ffn, norm_final[None])

    out_g, out_d, out_m, out_v = {}, {}, {}, {}
    small_like = [w[n] for n in _SMALL]
    small_w = _pack(small_like)
    packed = _pack([small_grads_local[n] for n in _SMALL] + [loss[0, 0]])
    after = grad_x
    for gi, (names, (g8, from_sibling, send_sems, recv_sems, sums4, lands)) in enumerate(reducing.items()):
        if gi == len(reducing) - 1:
            summed = _small_all_reduce(packed, after, name="small_all_reduce")
            after = summed
        _, from_chips = _copies_wait(_chip_copies, send_sems, recv_sems, sums4, lands, after,
                                     name="rs_chips_wait_" + names[0])
        for i, n in enumerate(names):
            g, d, m_, v_ = _adamw_shard(_shard(n, w[n]), _shard(n, m[n]), _shard(n, v[n]), g8[i], from_sibling[i],
                                        from_chips[i], pos, name="adamw_" + n)
            out_g[n], out_d[n], out_m[n], out_v[n] = (_unshard(n, o) for o in (g, d, m_, v_))
            after = d
    *small_grads, loss_sum = _unpack(summed, small_like + [jax.ShapeDtypeStruct((), F32)])
    d_s, m_s, v_s = _adamw_small(small_w, summed[:small_w.shape[0]], _pack([m[n] for n in _SMALL]),
                                 _pack([v[n] for n in _SMALL]), name="adamw_small")
    for n, g, d, m_, v_ in zip(_SMALL, small_grads, _unpack(d_s, small_like), _unpack(m_s, small_like), _unpack(v_s, small_like)):
        out_g[n], out_d[n], out_m[n], out_v[n] = g, d, m_, v_

    return (loss_sum, grad_x[None], *[out_g[n] for n in _ORDER], *[out_d[n] for n in _ORDER],
            *[out_m[n] for n in _ORDER], *[out_v[n] for n in _ORDER])
```

```python
import functools
import math

import numpy as np
import jax
import jax.numpy as jnp
from jax import lax
from jax.experimental import pallas as pl
from jax.experimental.pallas import tpu as pltpu

F32 = jnp.float32
BF16 = jnp.bfloat16

EPS = 1e-6
NEG = -1e30
HEAD_DIM = 128
BLOCK = 128
N_KV_HEADS = 2
KV_WIDTH = N_KV_HEADS * HEAD_DIM
REL_BUCKETS = 32
REL_MAX_DIST = 128

ADAM_LR = 0.001
ADAM_B1 = 0.9
ADAM_B2 = 0.999
ADAM_EPS = 1e-08
ADAM_WD = 0.01
ADAM_STEP = 10

N_DEV = 8
LANES = 128
VMEM_LIMIT = 56 * 1024 * 1024
MESH = pl.DeviceIdType.MESH


def _cparams(*sem):
    return pltpu.CompilerParams(dimension_semantics=sem, vmem_limit_bytes=VMEM_LIMIT)


def _div(n, target, mult=LANES):
    best = None
    for d in range(mult, min(n, target) + 1, mult):
        if n % d == 0:
            best = d
    assert best is not None, (n, target, mult)
    return best


_ANY = pl.BlockSpec(memory_space=pl.ANY)


def _ordered_after(body, n_inputs, in_specs, args, after):
    if after is None:
        return body, in_specs, args

    def wrapped(*refs):
        return body(*refs[:n_inputs], *refs[n_inputs + 1:])

    return wrapped, list(in_specs) + [_ANY], tuple(args) + (after,)


def _bucket_map():
    nb = REL_BUCKETS // 2
    qi = np.arange(BLOCK)[:, None]
    kj = np.arange(3 * BLOCK)[None, :]
    rel = kj - BLOCK - qi
    ret = np.where(rel > 0, nb, 0)
    n = np.abs(rel)
    max_exact = nb // 2
    nf = np.maximum(n, 1).astype(np.float32)
    large = max_exact + (np.log(nf / np.float32(max_exact)) / np.float32(math.log(REL_MAX_DIST / max_exact))
                         * np.float32(nb - max_exact)).astype(np.int32)
    large = np.minimum(large, nb - 1)
    return (ret + np.where(n < max_exact, n, large)).astype(np.int32)


_GELU_C = math.sqrt(2.0 / math.pi)
_GELU_A = 0.044715


def _gelu(x):
    t = jnp.tanh(_GELU_C * (x + _GELU_A * (x * x * x)))
    return 0.5 * x * (1.0 + t)


def _gelu_and_grad(x):
    x2 = x * x
    t = jnp.tanh(_GELU_C * (x + _GELU_A * (x2 * x)))
    g = 0.5 * x * (1.0 + t)
    dg = 0.5 * (1.0 + t) + 0.5 * x * (1.0 - t * t) * (_GELU_C * (1.0 + 3.0 * _GELU_A * x2))
    return g, dg


def _sigmoid(x):
    return 1.0 / (1.0 + jnp.exp(-x))


def _mm(a, b, *, name, ta=False, tb=False, add=None, out_dtype=F32, bm=1024, bn=1024, bk=None, after=None,
        row_blocks=None, into=None):
    if ta:
        K, M = a.shape
    else:
        M, K = a.shape
    N = b.shape[0] if tb else b.shape[1]
    assert (b.shape[1] if tb else b.shape[0]) == K
    bm = _div(M, bm)
    bn = _div(N, bn)
    bk = K if bk is None else _div(K, bk)
    nk = K // bk
    i0, ni = (0, M // bm) if row_blocks is None else row_blocks
    a_spec = (pl.BlockSpec((bk, bm), lambda i, j, k: (k, i + i0)) if ta
              else pl.BlockSpec((bm, bk), lambda i, j, k: (i + i0, k)))
    b_spec = pl.BlockSpec((bn, bk), lambda i, j, k: (j, k)) if tb else pl.BlockSpec((bk, bn), lambda i, j, k: (k, j))
    o_spec = pl.BlockSpec((bm, bn), lambda i, j, k: (i + i0, j))
    dims = (((0 if ta else 1,), (1 if tb else 0,)), ((), ()))
    has_add = add is not None

    def body(*refs):
        if has_add:
            a_ref, b_ref, add_ref, o_ref, *scratch = refs
        else:
            a_ref, b_ref, o_ref, *scratch = refs
            add_ref = None
        p = lax.dot_general(a_ref[...].astype(BF16), b_ref[...].astype(BF16), dims, preferred_element_type=F32)
        if nk == 1:
            if has_add:
                p = p + add_ref[...]
            o_ref[...] = p.astype(out_dtype)
        else:
            acc = scratch[0]
            k = pl.program_id(2)

            @pl.when(k == 0)
            def _():
                acc[...] = p

            @pl.when(k > 0)
            def _():
                acc[...] += p

            @pl.when(k == nk - 1)
            def _():
                r = acc[...]
                if has_add:
                    r = r + add_ref[...]
                o_ref[...] = r.astype(out_dtype)

    in_specs = [a_spec, b_spec] + ([o_spec] if has_add else [])
    args = (a, b) + ((add,) if has_add else ())
    aliases = {}
    if into is not None:
        body, in_specs, args = _ordered_after(body, len(args), in_specs, args, into)
        aliases = {len(args) - 1: 0}
    body, in_specs, args = _ordered_after(body, len(args), in_specs, args, after)
    return pl.pallas_call(
        body, name=name, grid=(ni, N // bn, nk),
        in_specs=in_specs, out_specs=o_spec,
        out_shape=jax.ShapeDtypeStruct((M, N), out_dtype),
        input_output_aliases=aliases,
        scratch_shapes=[pltpu.VMEM((bm, bn), F32)] if nk > 1 else [],
        compiler_params=_cparams("parallel", "parallel", "arbitrary"),
    )(*args)


def _mm_resid_rms(a, b, resid, gain, *, name, bm=512):
    M, K = a.shape
    N = b.shape[1]
    bm = _div(M, bm)

    def body(a_ref, b_ref, r_ref, g_ref, x_ref, h_ref):
        x = r_ref[...] + jnp.dot(a_ref[...], b_ref[...], preferred_element_type=F32)
        x_ref[...] = x
        r = lax.rsqrt(jnp.mean(x * x, axis=-1, keepdims=True) + EPS)
        h_ref[...] = ((x * r) * g_ref[...]).astype(BF16)

    row = pl.BlockSpec((bm, N), lambda i: (i, 0))
    return pl.pallas_call(
        body, name=name, grid=(M // bm,),
        in_specs=[pl.BlockSpec((bm, K), lambda i: (i, 0)), pl.BlockSpec((K, N), lambda i: (0, 0)), row,
                  pl.BlockSpec((1, N), lambda i: (0, 0))],
        out_specs=[row, row], out_shape=[jax.ShapeDtypeStruct((M, N), F32), jax.ShapeDtypeStruct((M, N), BF16)],
        compiler_params=_cparams("parallel"),
    )(a, b, resid, gain)


def _mm_sum2(a1, b1, a2, b2, *, name, bm=1024, bn=512, bk=2816, after=None):
    M, K = a1.shape
    N = b1.shape[1]
    bm, bn, bk = _div(M, bm), _div(N, bn), _div(K, bk)
    nk = K // bk

    def body(a1_ref, b1_ref, a2_ref, b2_ref, o_ref, acc):
        p = (jnp.dot(a1_ref[...], b1_ref[...], preferred_element_type=F32)
             + jnp.dot(a2_ref[...], b2_ref[...], preferred_element_type=F32))
        k = pl.program_id(2)

        @pl.when(k == 0)
        def _():
            acc[...] = p

        @pl.when(k > 0)
        def _():
            acc[...] += p

        @pl.when(k == nk - 1)
        def _():
            o_ref[...] = acc[...]

    a_spec = pl.BlockSpec((bm, bk), lambda i, j, k: (i, k))
    b_spec = pl.BlockSpec((bk, bn), lambda i, j, k: (k, j))
    body, in_specs, args = _ordered_after(body, 4, [a_spec, b_spec, a_spec, b_spec], (a1, b1, a2, b2), after)
    return pl.pallas_call(
        body, name=name, grid=(M // bm, N // bn, nk),
        in_specs=in_specs, out_specs=pl.BlockSpec((bm, bn), lambda i, j, k: (i, j)),
        out_shape=jax.ShapeDtypeStruct((M, N), F32),
        scratch_shapes=[pltpu.VMEM((bm, bn), F32)],
        compiler_params=_cparams("parallel", "parallel", "arbitrary"),
    )(*args)


def _blocks_per_tile(c):
    nb = 1
    while (nb * c) % LANES or (nb * c < 1024 and nb < N_DEV):
        nb *= 2
    assert nb <= N_DEV and (nb * c) % LANES == 0, c
    return nb


def _mm_w8(a, w8, *, name, bm=1024, out_dtype=F32):
    M, K = a.shape
    _, _, c = w8.shape
    nb = _blocks_per_tile(c)
    bm = _div(M, bm)

    def body(a_ref, w_ref, o_ref):
        a_ = a_ref[...]
        for t in range(nb):
            o_ref[:, t * c:(t + 1) * c] = jnp.dot(a_, w_ref[t], preferred_element_type=F32).astype(out_dtype)

    return pl.pallas_call(
        body, name=name, grid=(M // bm, N_DEV // nb),
        in_specs=[pl.BlockSpec((bm, K), lambda i, j: (i, 0)), pl.BlockSpec((nb, K, c), lambda i, j: (j, 0, 0))],
        out_specs=pl.BlockSpec((bm, nb * c), lambda i, j: (i, j)),
        out_shape=jax.ShapeDtypeStruct((M, N_DEV * c), out_dtype),
        compiler_params=_cparams("parallel", "parallel"),
    )(a, w8)


def _mm_w8t(dy, w8, *, name, add=None, out_dtype=F32, bm=1024, bn=1024, after=None, lead=None):
    M = dy.shape[-2]
    _, K, c = w8.shape
    nb = _blocks_per_tile(c)
    nk = N_DEV // nb
    bm, bn = _div(M, bm), _div(K, bn)
    has_add = add is not None
    dims = (((1,), (1,)), ((), ()))

    def body(*refs):
        if has_add:
            dy_ref, w_ref, add_ref, o_ref, acc = refs
        else:
            dy_ref, w_ref, o_ref, acc = refs
        p = lax.dot_general(dy_ref[:, 0:c], w_ref[0], dims, preferred_element_type=F32)
        for t in range(1, nb):
            p = p + lax.dot_general(dy_ref[:, t * c:(t + 1) * c], w_ref[t], dims, preferred_element_type=F32)
        k = pl.program_id(2)

        @pl.when(k == 0)
        def _():
            acc[...] = p

        @pl.when(k > 0)
        def _():
            acc[...] += p

        @pl.when(k == nk - 1)
        def _():
            r = acc[...]
            if has_add:
                r = r + add_ref[...]
            o_ref[...] = r.astype(out_dtype)

    o_spec = pl.BlockSpec((bm, bn), lambda i, j, k: (i, j))
    dy_spec = (pl.BlockSpec((bm, nb * c), lambda i, j, k: (i, k)) if lead is None
               else pl.BlockSpec((None, bm, nb * c), lambda i, j, k: (lead, i, k)))
    in_specs = [dy_spec, pl.BlockSpec((nb, bn, c), lambda i, j, k: (k, j, 0))]
    in_specs += [o_spec] if has_add else []
    args = (dy, w8) + ((add,) if has_add else ())
    body, in_specs, args = _ordered_after(body, len(args), in_specs, args, after)
    return pl.pallas_call(
        body, name=name, grid=(M // bm, K // bn, nk),
        in_specs=in_specs, out_specs=o_spec,
        out_shape=jax.ShapeDtypeStruct((M, K), out_dtype),
        scratch_shapes=[pltpu.VMEM((bm, bn), F32)],
        compiler_params=_cparams("parallel", "parallel", "arbitrary"),
    )(*args)


def _mm_gw8(x, dy, c, *, name, bk=1024, lead=None):
    T, K = x.shape
    nb = _blocks_per_tile(c)
    bk = _div(K, bk)
    dims = (((0,), (0,)), ((), ()))

    def body(x_ref, dy_ref, o_ref):
        x_ = x_ref[...]
        for t in range(nb):
            o_ref[t] = lax.dot_general(x_, dy_ref[:, t * c:(t + 1) * c], dims, preferred_element_type=F32).astype(BF16)

    dy_spec = (pl.BlockSpec((T, nb * c), lambda i, j: (0, j)) if lead is None
               else pl.BlockSpec((None, T, nb * c), lambda i, j: (lead, 0, j)))
    return pl.pallas_call(
        body, name=name, grid=(K // bk, N_DEV // nb),
        in_specs=[pl.BlockSpec((T, bk), lambda i, j: (0, i)), dy_spec],
        out_specs=pl.BlockSpec((nb, bk, c), lambda i, j: (j, i, 0)),
        out_shape=jax.ShapeDtypeStruct((N_DEV, K, c), BF16),
        compiler_params=_cparams("parallel", "parallel"),
    )(x, dy)


def _rms_fwd(x, g, *, name):
    T, D = x.shape
    tm = _div(T, 256, 8)

    def body(x_ref, g_ref, h_ref):
        xf = x_ref[...]
        r = lax.rsqrt(jnp.mean(xf * xf, axis=-1, keepdims=True) + EPS)
        h_ref[...] = ((xf * r) * g_ref[...]).astype(BF16)

    return pl.pallas_call(
        body, name=name, grid=(T // tm,),
        in_specs=[pl.BlockSpec((tm, D), lambda i: (i, 0)), pl.BlockSpec((1, D), lambda i: (0, 0))],
        out_specs=pl.BlockSpec((tm, D), lambda i: (i, 0)),
        out_shape=jax.ShapeDtypeStruct((T, D), BF16),
        compiler_params=_cparams("parallel"),
    )(x, g)


def _rms_bwd(x, g, dh, dres, *, name, want_bf16, after=None):
    T, D = x.shape
    tm = _div(T, 256, 8)

    def body(x_ref, g_ref, dh_ref, dres_ref, dx_ref, *rest):
        if want_bf16:
            dxb_ref, dg_ref = rest
        else:
            (dg_ref,) = rest
        xf = x_ref[...]
        r = lax.rsqrt(jnp.mean(xf * xf, axis=-1, keepdims=True) + EPS)
        xhat = xf * r
        dh_ = dh_ref[...]
        dy = dh_ * g_ref[...]
        dx = dres_ref[...] + r * (dy - xhat * jnp.mean(dy * xhat, axis=-1, keepdims=True))
        dx_ref[...] = dx
        if want_bf16:
            dxb_ref[...] = dx.astype(BF16)
        part = jnp.sum(dh_ * xhat, axis=0, keepdims=True)

        @pl.when(pl.program_id(0) == 0)
        def _():
            dg_ref[...] = part

        @pl.when(pl.program_id(0) > 0)
        def _():
            dg_ref[...] += part

    row = pl.BlockSpec((tm, D), lambda i: (i, 0))
    vec = pl.BlockSpec((1, D), lambda i: (0, 0))
    out_specs = [row] + ([row] if want_bf16 else []) + [vec]
    out_shape = ([jax.ShapeDtypeStruct((T, D), F32)] + ([jax.ShapeDtypeStruct((T, D), BF16)] if want_bf16 else [])
                 + [jax.ShapeDtypeStruct((1, D), F32)])
    body, in_specs, args = _ordered_after(body, 4, [row, vec, row, row], (x, g, dh, dres), after)
    return pl.pallas_call(
        body, name=name, grid=(T // tm,),
        in_specs=in_specs, out_specs=out_specs, out_shape=out_shape,
        compiler_params=_cparams("arbitrary"),
    )(*args)


def _loss_head(x, g, target, *, name):
    T, D = x.shape
    tm = _div(T, 256, 8)

    def body(x_ref, g_ref, t_ref, loss_ref, dx_ref, dxb_ref, dg_ref):
        xf = x_ref[...]
        r = lax.rsqrt(jnp.mean(xf * xf, axis=-1, keepdims=True) + EPS)
        xhat = xf * r
        gain = g_ref[...]
        err = xhat * gain - t_ref[...]
        lpart = 0.5 * jnp.sum(jnp.mean(err * err, axis=-1, keepdims=True), axis=0, keepdims=True)
        dh_ = err * (1.0 / D)
        dy = dh_ * gain
        dx = r * (dy - xhat * jnp.mean(dy * xhat, axis=-1, keepdims=True))
        dx_ref[...] = dx
        dxb_ref[...] = dx.astype(BF16)
        part = jnp.sum(dh_ * xhat, axis=0, keepdims=True)

        @pl.when(pl.program_id(0) == 0)
        def _():
            dg_ref[...] = part
            loss_ref[...] = jnp.broadcast_to(lpart, loss_ref.shape)

        @pl.when(pl.program_id(0) > 0)
        def _():
            dg_ref[...] += part
            loss_ref[...] += jnp.broadcast_to(lpart, loss_ref.shape)

    row = pl.BlockSpec((tm, D), lambda i: (i, 0))
    vec = pl.BlockSpec((1, D), lambda i: (0, 0))
    return pl.pallas_call(
        body, name=name, grid=(T // tm,),
        in_specs=[row, vec, row],
        out_specs=[pl.BlockSpec((8, LANES), lambda i: (0, 0)), row, row, vec],
        out_shape=[jax.ShapeDtypeStruct((8, LANES), F32), jax.ShapeDtypeStruct((T, D), F32),
                   jax.ShapeDtypeStruct((T, D), BF16), jax.ShapeDtypeStruct((1, D), F32)],
        compiler_params=_cparams("arbitrary"),
    )(x, g, target)


def _gate_cols(D):
    off_a = 3 * D // 2 + 2 * KV_WIDTH
    off_b = off_a + D
    cw = math.gcd(math.gcd(off_a, off_b), math.gcd(D, 512))
    return cw, off_a // cw, off_b // cw


def _merge_fwd(z, ya, yb, *, name):
    T, D = ya.shape
    cw, ba, bb = _gate_cols(D)
    tm = _div(T, 512, 8)

    def body(ga_ref, gb_ref, ya_ref, yb_ref, m_ref):
        m_ref[...] = (_sigmoid(ga_ref[...].astype(F32)) * ya_ref[...]
                      + _sigmoid(gb_ref[...].astype(F32)) * yb_ref[...]).astype(BF16)

    blk = pl.BlockSpec((tm, cw), lambda i, j: (i, j))
    return pl.pallas_call(
        body, name=name, grid=(T // tm, D // cw),
        in_specs=[pl.BlockSpec((tm, cw), lambda i, j: (i, ba + j)), pl.BlockSpec((tm, cw), lambda i, j: (i, bb + j)), blk, blk],
        out_specs=blk, out_shape=jax.ShapeDtypeStruct((T, D), BF16),
        compiler_params=_cparams("parallel", "parallel"),
    )(z, z, ya, yb)


def _merge_bwd(z, ya, yb, dm, *, name, after=None):
    T, D = ya.shape
    cw, ba, bb = _gate_cols(D)
    nj = D // cw
    assert bb == ba + nj
    tm = _div(T, 512, 8)

    def body(g_ref, ya_ref, yb_ref, dm_ref, dy_ref, dz_ref):
        sig = _sigmoid(g_ref[...].astype(F32))
        dm_ = dm_ref[...]
        y = jnp.where(pl.program_id(1) == 0, ya_ref[...], yb_ref[...])
        dy_ref[...] = (dm_ * sig).astype(BF16)
        dz_ref[...] = (dm_ * y * (sig * (1.0 - sig))).astype(BF16)

    in_specs = [pl.BlockSpec((tm, cw), lambda i, s, j: (i, ba + s * nj + j)),
                pl.BlockSpec((tm, cw), lambda i, s, j: (i, j * (1 - s))),
                pl.BlockSpec((tm, cw), lambda i, s, j: (i, j * s)),
                pl.BlockSpec((tm, cw), lambda i, s, j: (i, j))]
    body, in_specs, args = _ordered_after(body, 4, in_specs, (z, ya, yb, dm), after)
    return pl.pallas_call(
        body, name=name, grid=(T // tm, 2, nj),
        in_specs=in_specs,
        out_specs=[pl.BlockSpec((None, tm, cw), lambda i, s, j: (s, i, j)),
                   pl.BlockSpec((tm, cw), lambda i, s, j: (i, ba + s * nj + j))],
        out_shape=[jax.ShapeDtypeStruct((2, T, D), BF16), jax.ShapeDtypeStruct(z.shape, BF16)],
        compiler_params=_cparams("parallel", "arbitrary", "arbitrary"),
    )(*args)


def _swiglu_mm_fwd(h, wu_t, gate, *, name, bm=1024, bn=512):
    T, D = h.shape
    F = wu_t.shape[0]
    bm, bn = _div(T, bm), _div(F, bn)

    rc = _div(bm, 256, 16)

    def body(h_ref, wu_ref, gin_ref, g_ref, u_ref, act_ref):
        w = wu_ref[...]
        for r in range(0, bm, rc):
            rows = slice(r, r + rc)
            u = lax.dot_general(h_ref[rows, :], w, (((1,), (1,)), ((), ())), preferred_element_type=F32)
            g = gin_ref[rows, :]
            g_ref[rows, :] = g.astype(BF16)
            u_ref[rows, :] = u.astype(BF16)
            act_ref[rows, :] = (g * _sigmoid(g) * u).astype(BF16)

    o_spec = pl.BlockSpec((bm, bn), lambda i, j: (i, j))
    return pl.pallas_call(
        body, name=name, grid=(T // bm, F // bn),
        in_specs=[pl.BlockSpec((bm, D), lambda i, j: (i, 0)), pl.BlockSpec((bn, D), lambda i, j: (j, 0)), o_spec],
        out_specs=[o_spec] * 3, out_shape=[jax.ShapeDtypeStruct((T, F), BF16)] * 3,
        compiler_params=_cparams("parallel", "parallel"),
    )(h, wu_t, gate)


def _swiglu_mm_bwd(dx, w_down, gate, up, *, name, bm=2048, bn=512, after=None):
    T, D = dx.shape
    F = w_down.shape[0]
    bm, bn = _div(T, bm), _div(F, bn)
    dims = (((1,), (1,)), ((), ()))

    rc = _div(bm, 256, 16)

    def body(dx_ref, w_ref, g_ref, u_ref, dg_ref, du_ref):
        w = w_ref[...]
        for r in range(0, bm, rc):
            rows = slice(r, r + rc)
            d = lax.dot_general(dx_ref[rows, :], w, dims, preferred_element_type=F32)
            g = g_ref[rows, :].astype(F32)
            s = _sigmoid(g)
            silu = g * s
            dg_ref[rows, :] = (d * u_ref[rows, :].astype(F32) * (s + silu * (1.0 - s))).astype(BF16)
            du_ref[rows, :] = (d * silu).astype(BF16)

    o_spec = pl.BlockSpec((bm, bn), lambda i, j: (i, j))
    in_specs = [pl.BlockSpec((bm, D), lambda i, j: (i, 0)), pl.BlockSpec((bn, D), lambda i, j: (j, 0)), o_spec, o_spec]
    body, in_specs, args = _ordered_after(body, 4, in_specs, (dx, w_down, gate, up), after)
    out = jax.ShapeDtypeStruct((T, F), BF16)
    return pl.pallas_call(
        body, name=name, grid=(T // bm, F // bn), in_specs=in_specs, out_specs=[o_spec, o_spec], out_shape=[out, out],
        compiler_params=_cparams("parallel", "parallel"),
    )(*args)


def _sgu_fwd(z, gain, ws_b, bs_t, *, name):
    T = z.shape[0]
    SW = gain.shape[1]
    G = SW // BLOCK

    def body(zu_ref, zv_ref, gain_ref, ws_ref, bs_ref, a_ref):
        u = _gelu(zu_ref[...].astype(F32))
        vg = _gelu(zv_ref[...].astype(F32))
        r = lax.rsqrt(jnp.mean(vg * vg, axis=-1, keepdims=True) + EPS)
        vn = ((vg * r) * gain_ref[...]).astype(BF16)
        for g in range(G):
            sl = slice(g * BLOCK, (g + 1) * BLOCK)
            mixed = jnp.dot(ws_ref[g], vn[:, sl], preferred_element_type=F32) + bs_ref[:, g:g + 1]
            a_ref[:, sl] = (u[:, sl] * mixed).astype(BF16)

    return pl.pallas_call(
        body, name=name, grid=(T // BLOCK,),
        in_specs=[pl.BlockSpec((BLOCK, SW), lambda c: (c, 0)), pl.BlockSpec((BLOCK, SW), lambda c: (c, 1)),
                  pl.BlockSpec((1, SW), lambda c: (0, 0)), pl.BlockSpec((G, BLOCK, BLOCK), lambda c: (0, 0, 0)),
                  pl.BlockSpec((BLOCK, G), lambda c: (0, 0))],
        out_specs=pl.BlockSpec((BLOCK, SW), lambda c: (c, 0)),
        out_shape=jax.ShapeDtypeStruct((T, SW), BF16),
        compiler_params=_cparams("parallel"),
    )(z, z, gain, ws_b, bs_t)


def _sgu_bwd(z, gain, ws_b, bs_t, da, dz, *, name):
    T = z.shape[0]
    SW = gain.shape[1]
    G = SW // BLOCK

    def body(zu_ref, zv_ref, gain_ref, ws_ref, bs_ref, da_ref, dz_in_ref, dz_ref, dws_ref, dbs_ref, dgain_ref, dvn_ref):
        first = pl.program_id(0) == 0

        @pl.when(first)
        def _():
            dws_ref[...] = jnp.zeros_like(dws_ref)
            dbs_ref[...] = jnp.zeros_like(dbs_ref)
            dgain_ref[...] = jnp.zeros_like(dgain_ref)

        u, du = _gelu_and_grad(zu_ref[...].astype(F32))
        vg, dvg = _gelu_and_grad(zv_ref[...].astype(F32))
        r = lax.rsqrt(jnp.mean(vg * vg, axis=-1, keepdims=True) + EPS)
        xhat = vg * r
        gain_ = gain_ref[...]
        vn = (xhat * gain_).astype(BF16)
        da_ = da_ref[...]
        for g in range(G):
            sl = slice(g * BLOCK, (g + 1) * BLOCK)
            w = ws_ref[g]
            mixed = jnp.dot(w, vn[:, sl], preferred_element_type=F32) + bs_ref[:, g:g + 1]
            dmix = da_[:, sl] * u[:, sl]
            dz_ref[:, sl] = (da_[:, sl] * mixed * du[:, sl]).astype(BF16)
            dmb = dmix.astype(BF16)
            dws_ref[g] += lax.dot_general(dmb, vn[:, sl], (((1,), (1,)), ((), ())), preferred_element_type=F32)
            dbs_ref[:, g:g + 1] += jnp.sum(dmix, axis=-1, keepdims=True)
            dvn_ref[:, sl] = lax.dot_general(w, dmb, (((0,), (0,)), ((), ())), preferred_element_type=F32)
        dvn = dvn_ref[...]
        dgain_ref[...] += jnp.sum(dvn * xhat, axis=0, keepdims=True)
        dy = dvn * gain_
        dv_ = r * (dy - xhat * jnp.mean(dy * xhat, axis=-1, keepdims=True))
        dz_ref[:, SW:] = (dv_ * dvg).astype(BF16)

    row = pl.BlockSpec((BLOCK, SW), lambda c: (c, 0))
    return pl.pallas_call(
        body, name=name, grid=(T // BLOCK,),
        in_specs=[row, pl.BlockSpec((BLOCK, SW), lambda c: (c, 1)),
                  pl.BlockSpec((1, SW), lambda c: (0, 0)), pl.BlockSpec((G, BLOCK, BLOCK), lambda c: (0, 0, 0)),
                  pl.BlockSpec((BLOCK, G), lambda c: (0, 0)), row, _ANY],
        out_specs=[pl.BlockSpec((BLOCK, 2 * SW), lambda c: (c, 0)), pl.BlockSpec((G, BLOCK, BLOCK), lambda c: (0, 0, 0)),
                   pl.BlockSpec((BLOCK, G), lambda c: (0, 0)), pl.BlockSpec((1, SW), lambda c: (0, 0))],
        out_shape=[jax.ShapeDtypeStruct(dz.shape, dz.dtype),
                   jax.ShapeDtypeStruct((G, BLOCK, BLOCK), F32), jax.ShapeDtypeStruct((BLOCK, G), F32),
                   jax.ShapeDtypeStruct((1, SW), F32)],
        input_output_aliases={6: 0},
        scratch_shapes=[pltpu.VMEM((BLOCK, SW), F32)],
        compiler_params=_cparams("arbitrary"),
    )(z, z, gain, ws_b, bs_t, da, dz)


def _bias_table(rel_bias, bmap, *, name):
    H = rel_bias.shape[1]

    def body(rb_ref, bmap_ref, o_ref):
        bm_ = bmap_ref[...]
        for h in range(H):
            acc = jnp.zeros(bm_.shape, F32)
            for b in range(REL_BUCKETS):
                acc = jnp.where(bm_ == b, rb_ref[b, h], acc)
            o_ref[h] = acc

    return pl.pallas_call(
        body, name=name,
        in_specs=[pl.BlockSpec(memory_space=pltpu.SMEM), pl.BlockSpec(memory_space=pltpu.VMEM)],
        out_specs=pl.BlockSpec(memory_space=pltpu.VMEM),
        out_shape=jax.ShapeDtypeStruct((H, BLOCK, 3 * BLOCK), F32),
    )(rel_bias, bmap)


def _attn_probs(q_ref, kb, bias_ref, sink_ref, s_ref, n, T, group):
    H = s_ref.shape[0]
    for h in range(H):
        kv = h // group
        qh = q_ref[:, h * HEAD_DIM:(h + 1) * HEAD_DIM].astype(BF16)
        s_ref[h] = lax.dot_general(qh, kb[:, kv * HEAD_DIM:(kv + 1) * HEAD_DIM], (((1,), (1,)), ((), ())),
                                   preferred_element_type=F32)
    row = lax.broadcasted_iota(jnp.int32, (BLOCK, 3 * BLOCK), 0)
    col = lax.broadcasted_iota(jnp.int32, (BLOCK, 3 * BLOCK), 1)
    key_pos = n * BLOCK + col - BLOCK
    valid = (jnp.abs(col - BLOCK - row) <= BLOCK) & (key_pos >= 0) & (key_pos < T)
    s = s_ref[...] * (HEAD_DIM ** -0.5) + bias_ref[...]
    s = jnp.where(valid[None], s, NEG)
    sink = sink_ref[...]
    m = jnp.maximum(jnp.max(s, axis=-1, keepdims=True), sink)
    e = jnp.exp(s - m)
    es = jnp.exp(sink - m)
    inv = 1.0 / (jnp.sum(e, axis=-1, keepdims=True) + es)
    return e * inv, es * inv


def _attn_fwd(z, kpad, vpad, bias, sink, *, name):
    T = z.shape[0]
    H = bias.shape[0]
    AW = H * HEAD_DIM
    group = H // N_KV_HEADS

    def body(q_ref, k_ref, v_ref, bias_ref, sink_ref, o_ref, s_ref, p_ref):
        n = pl.program_id(0)
        start = pl.multiple_of(n * BLOCK, BLOCK)
        kb = k_ref[pl.ds(start, 3 * BLOCK), :]
        vb = v_ref[pl.ds(start, 3 * BLOCK), :]
        p, _ = _attn_probs(q_ref, kb, bias_ref, sink_ref, s_ref, n, T, group)
        p_ref[...] = p.astype(BF16)
        for h in range(H):
            kv = h // group
            o = jnp.dot(p_ref[h], vb[:, kv * HEAD_DIM:(kv + 1) * HEAD_DIM], preferred_element_type=F32)
            o_ref[:, h * HEAD_DIM:(h + 1) * HEAD_DIM] = o.astype(BF16)

    full_kv = pl.BlockSpec((T + 2 * BLOCK, KV_WIDTH), lambda n: (0, 0))
    return pl.pallas_call(
        body, name=name, grid=(T // BLOCK,),
        in_specs=[pl.BlockSpec((BLOCK, AW), lambda n: (n, 2)), full_kv, full_kv,
                  pl.BlockSpec((H, BLOCK, 3 * BLOCK), lambda n: (0, 0, 0)), pl.BlockSpec((H, 1, 1), lambda n: (0, 0, 0))],
        out_specs=pl.BlockSpec((BLOCK, AW), lambda n: (n, 0)),
        out_shape=jax.ShapeDtypeStruct((T, AW), BF16),
        scratch_shapes=[pltpu.VMEM((H, BLOCK, 3 * BLOCK), F32), pltpu.VMEM((H, BLOCK, 3 * BLOCK), BF16)],
        compiler_params=_cparams("parallel"),
    )(z, kpad, vpad, bias, sink)


def _attn_bwd(z, kpad, vpad, bias, sink, do, dz, *, name):
    T = z.shape[0]
    H = bias.shape[0]
    AW = H * HEAD_DIM
    group = H // N_KV_HEADS
    scale = HEAD_DIM ** -0.5

    def body(q_ref, k_ref, v_ref, bias_ref, sink_ref, do_ref, dz_in_ref, dq_ref, dk_ref, dv_ref, dbias_ref, dsink_ref,
             s_ref, dp_ref, p_ref, ds_ref):
        n = pl.program_id(0)

        @pl.when(n == 0)
        def _():
            dk_ref[...] = jnp.zeros_like(dk_ref)
            dv_ref[...] = jnp.zeros_like(dv_ref)
            dbias_ref[...] = jnp.zeros_like(dbias_ref)
            dsink_ref[...] = jnp.zeros_like(dsink_ref)

        start = pl.multiple_of(n * BLOCK, BLOCK)
        kb = k_ref[pl.ds(start, 3 * BLOCK), :]
        vb = v_ref[pl.ds(start, 3 * BLOCK), :]
        p, p_sink = _attn_probs(q_ref, kb, bias_ref, sink_ref, s_ref, n, T, group)
        s_ref[...] = p
        p_ref[...] = p.astype(BF16)
        for h in range(H):
            kv = h // group
            dp_ref[h] = lax.dot_general(do_ref[:, h * HEAD_DIM:(h + 1) * HEAD_DIM], vb[:, kv * HEAD_DIM:(kv + 1) * HEAD_DIM],
                                        (((1,), (1,)), ((), ())), preferred_element_type=F32)
        p = s_ref[...]
        dp = dp_ref[...]
        delta = jnp.sum(p * dp, axis=-1, keepdims=True)
        ds = p * (dp - delta)
        dbias_ref[...] += ds
        dsink_ref[...] += -(p_sink * delta)
        ds_ref[...] = ds.astype(BF16)
        for kv in range(N_KV_HEADS):
            ksl = slice(kv * HEAD_DIM, (kv + 1) * HEAD_DIM)
            dk_acc = jnp.zeros((3 * BLOCK, HEAD_DIM), F32)
            dv_acc = jnp.zeros((3 * BLOCK, HEAD_DIM), F32)
            for gi in range(group):
                h = kv * group + gi
                hsl = slice(h * HEAD_DIM, (h + 1) * HEAD_DIM)
                dsb = ds_ref[h]
                dq = jnp.dot(dsb, kb[:, ksl], preferred_element_type=F32) * scale
                dq_ref[:, hsl] = dq.astype(BF16)
                dk_acc = dk_acc + lax.dot_general(dsb, q_ref[:, hsl].astype(BF16), (((0,), (0,)), ((), ())),
                                                  preferred_element_type=F32)
                dv_acc = dv_acc + lax.dot_general(p_ref[h], do_ref[:, hsl], (((0,), (0,)), ((), ())),
                                                  preferred_element_type=F32)
            dk_ref[pl.ds(start, 3 * BLOCK), ksl] += dk_acc * scale
            dv_ref[pl.ds(start, 3 * BLOCK), ksl] += dv_acc

    full_kv = pl.BlockSpec((T + 2 * BLOCK, KV_WIDTH), lambda n: (0, 0))
    bias_spec = pl.BlockSpec((H, BLOCK, 3 * BLOCK), lambda n: (0, 0, 0))
    row = pl.BlockSpec((BLOCK, AW), lambda n: (n, 0))
    q_cols = pl.BlockSpec((BLOCK, AW), lambda n: (n, 2))
    band = (H, BLOCK, 3 * BLOCK)
    return pl.pallas_call(
        body, name=name, grid=(T // BLOCK,),
        in_specs=[q_cols, full_kv, full_kv, bias_spec, pl.BlockSpec((H, 1, 1), lambda n: (0, 0, 0)), row, _ANY],
        out_specs=[q_cols, full_kv, full_kv, bias_spec, pl.BlockSpec((H, BLOCK, 1), lambda n: (0, 0, 0))],
        out_shape=[jax.ShapeDtypeStruct(dz.shape, dz.dtype),
                   jax.ShapeDtypeStruct((T + 2 * BLOCK, KV_WIDTH), F32), jax.ShapeDtypeStruct((T + 2 * BLOCK, KV_WIDTH), F32),
                   jax.ShapeDtypeStruct(band, F32), jax.ShapeDtypeStruct((H, BLOCK, 1), F32)],
        input_output_aliases={6: 0},
        scratch_shapes=[pltpu.VMEM(band, F32), pltpu.VMEM(band, F32), pltpu.VMEM(band, BF16), pltpu.VMEM(band, BF16)],
        compiler_params=_cparams("arbitrary"),
    )(z, kpad, vpad, bias, sink, do, dz)


def _dkv_into(dkp, dvp, dz, *, name):
    T = dz.shape[0]
    D = (dz.shape[1] - 2 * KV_WIDTH) * 2 // 7
    col = (D + D // 2) // (2 * KV_WIDTH)
    assert col * 2 * KV_WIDTH == D + D // 2

    def body(dk_ref, dv_ref, dz_in_ref, o_ref):
        o_ref[:, :KV_WIDTH] = dk_ref[...].astype(BF16)
        o_ref[:, KV_WIDTH:] = dv_ref[...].astype(BF16)

    kv = pl.BlockSpec((BLOCK, KV_WIDTH), lambda n: (n + 1, 0))
    return pl.pallas_call(
        body, name=name, grid=(T // BLOCK,),
        in_specs=[kv, kv, _ANY], out_specs=pl.BlockSpec((BLOCK, 2 * KV_WIDTH), lambda n: (n, col)),
        out_shape=jax.ShapeDtypeStruct(dz.shape, dz.dtype), input_output_aliases={2: 0},
        compiler_params=_cparams("parallel"),
    )(dkp, dvp, dz)


def _kv_pad(z, *, name):
    T = z.shape[0]
    D = (z.shape[1] - 2 * KV_WIDTH) * 2 // 7
    kcol = (D + D // 2) // KV_WIDTH
    nb = T // BLOCK

    def body(k_ref, v_ref, ko_ref, vo_ref):
        b = pl.program_id(0)
        inside = (b >= 1) & (b <= nb)
        ko_ref[...] = jnp.where(inside, k_ref[...].astype(F32), 0.0).astype(BF16)
        vo_ref[...] = jnp.where(inside, v_ref[...].astype(F32), 0.0).astype(BF16)

    out = jax.ShapeDtypeStruct((T + 2 * BLOCK, KV_WIDTH), BF16)
    o_spec = pl.BlockSpec((BLOCK, KV_WIDTH), lambda b: (b, 0))
    return pl.pallas_call(
        body, name=name, grid=(nb + 2,),
        in_specs=[pl.BlockSpec((BLOCK, KV_WIDTH), lambda b: (jnp.clip(b - 1, 0, nb - 1), kcol)),
                  pl.BlockSpec((BLOCK, KV_WIDTH), lambda b: (jnp.clip(b - 1, 0, nb - 1), kcol + 1))],
        out_specs=[o_spec, o_spec], out_shape=[out, out],
        compiler_params=_cparams("parallel"),
    )(z, z)


def _attn_small_grads(dbias, dsink_rows, bmap, after, *, name):
    H = dbias.shape[0]

    def body(dbias_ref, dsink_ref, bmap_ref, drel_ref, ds_ref):
        bm_ = bmap_ref[...]
        for h in range(H):
            d = dbias_ref[h]
            for b in range(REL_BUCKETS):
                drel_ref[b, h] = jnp.sum(jnp.where(bm_ == b, d, 0.0))
            ds_ref[0, h] = jnp.sum(dsink_ref[h])

    vmem = pl.BlockSpec(memory_space=pltpu.VMEM)
    smem = pl.BlockSpec(memory_space=pltpu.SMEM)
    body, in_specs, args = _ordered_after(body, 3, [vmem, vmem, vmem], (dbias, dsink_rows, bmap), after)
    return pl.pallas_call(
        body, name=name, in_specs=in_specs, out_specs=[smem, smem],
        out_shape=[jax.ShapeDtypeStruct((REL_BUCKETS, H), F32), jax.ShapeDtypeStruct((1, H), F32)],
    )(*args)


def _local_step(x, target, weight, emit, flush, norm_mix, v_gain, w_s, b_s, sink, rel_bias, norm_ffn, norm_final):
    T, D = x.shape
    ws_b = w_s.astype(BF16)
    bs_t = b_s.T
    bmap = jnp.asarray(_bucket_map())
    sink = sink.reshape(-1, 1, 1)

    bias = _bias_table(rel_bias, bmap, name="bias_table")
    h = _rms_fwd(x, norm_mix, name="rms_mix")
    w_in = weight("w_in", h)
    z = _mm(h, w_in, tb=True, out_dtype=BF16, name="mm_z", bm=2048, bn=768)
    a = _sgu_fwd(z, v_gain, ws_b, bs_t, name="sgu_fwd")
    w_a = weight("w_a_out", a)
    ya = _mm_w8(a, w_a, name="mm_ya", bm=2048, out_dtype=BF16)
    kpad, vpad = _kv_pad(z, name="kv_pad")
    o = _attn_fwd(z, kpad, vpad, bias, sink, name="attn_fwd")
    w_b = weight("w_b_out", o)
    yb = _mm_w8(o, w_b, name="mm_yb", bm=2048, out_dtype=BF16)
    m = _merge_fwd(z, ya, yb, name="merge_fwd")
    w_o = weight("w_o", m)
    x1, h2 = _mm_resid_rms(m, w_o, x, norm_ffn, name="mm_x1_rms")
    w_gate = weight("w_gate", h2)
    gate = _mm(h2, w_gate, tb=True, name="mm_gate", bm=2048, bn=512)
    w_up = weight("w_up", gate)
    gate, up, act = _swiglu_mm_fwd(h2, w_up, gate, name="mm_up_swiglu")
    w_down = weight("w_down", act)
    x2 = _mm(act, w_down, name="mm_x2", add=x1, bm=1024, bn=512)
    loss, dx2, dx2b, g_norm_final = _loss_head(x2, norm_final, target, name="loss_head")

    g_w_down = _mm(act, dx2b, ta=True, out_dtype=BF16, name="mm_gwdown", bm=512, bn=2048)
    tok = emit(("w_down",), (g_w_down,))
    dgate, dup = _swiglu_mm_bwd(dx2b, w_down, gate, up, name="mm_dact_swiglu", after=tok)
    tok = flush(dgate)
    g_w_gate = _mm(dgate, h2, ta=True, out_dtype=BF16, name="mm_gwgate", bm=512, bn=2048, after=tok)
    g_w_up = _mm(dup, h2, ta=True, out_dtype=BF16, name="mm_gwup", bm=512, bn=2048)
    tok = emit(("w_gate", "w_up"), (g_w_gate, g_w_up))
    dh2 = _mm_sum2(dgate, w_gate, dup, w_up, name="mm_dh2", after=tok)
    tok = flush(dh2)
    dx1, dx1b, g_norm_ffn = _rms_bwd(x1, norm_ffn, dh2, dx2, name="rms_ffn_bwd", want_bf16=True, after=tok)

    g_w_o = _mm(m, dx1b, ta=True, out_dtype=BF16, name="mm_gwo", bm=2048, bn=512)
    tok = emit(("w_o",), (g_w_o,))
    dm = _mm(dx1b, w_o, tb=True, name="mm_dm", bm=2048, bn=512, after=tok)
    tok = flush(dm)
    dy, dz = _merge_bwd(z, ya, yb, dm, name="merge_bwd", after=tok)
    g_w_a = _mm_gw8(a, dy, w_a.shape[2], name="mm_gwa", lead=0)
    g_w_b = _mm_gw8(o, dy, w_b.shape[2], name="mm_gwb", lead=1)
    tok = emit(("w_a_out", "w_b_out"), (g_w_a, g_w_b))
    da = _mm_w8t(dy, w_a, name="mm_da", bm=2048, bn=512, after=tok, lead=0)
    tok = flush(da)
    do = _mm_w8t(dy, w_b, out_dtype=BF16, name="mm_do", bm=2048, bn=512, after=tok, lead=1)
    dz, g_w_s, g_b_s_t, g_v_gain = _sgu_bwd(z, v_gain, ws_b, bs_t, da, dz, name="sgu_bwd")
    dz, dkp, dvp, dbias, dsink_rows = _attn_bwd(z, kpad, vpad, bias, sink, do, dz, name="attn_bwd")
    dz = _dkv_into(dkp, dvp, dz, name="dkv_into_dz")
    g_w_in = _mm(dz, h, ta=True, out_dtype=BF16, name="mm_gwin", bm=768, bn=2048)
    tok = emit(("w_in",), (g_w_in,))
    half = dict(bm=T // 2, bn=256)
    dh = _mm(dz, w_in, name="mm_dh_top", row_blocks=(0, 1), after=tok, **half)
    tok = flush(dh)
    dh = _mm(dz, w_in, name="mm_dh_bottom", row_blocks=(1, 1), into=dh, after=tok, **half)
    g_rel_bias, g_sink = _attn_small_grads(dbias, dsink_rows, bmap, dh, name="attn_small_grads")
    grad_x, g_norm_mix = _rms_bwd(x, norm_mix, dh, dx1, name="rms_mix_bwd", want_bf16=False)

    small = dict(norm_mix=g_norm_mix, sgu_v_gain=g_v_gain, sgu_w_s=g_w_s, sgu_b_s=g_b_s_t.T, attn_sink=g_sink,
                 rel_bias=g_rel_bias, norm_ffn=g_norm_ffn, norm_final=g_norm_final)
    return loss, grad_x, small


def _position():
    return lax.axis_index("x"), lax.axis_index("y"), lax.axis_index("c")


def _other_chips(x, y):
    return [(1 - x, y), (x, 1 - y), (1 - x, 1 - y)]


def _slot(px, py, pc):
    return 4 * px + 2 * py + pc


_HBM = pl.BlockSpec(memory_space=pltpu.HBM)
_SEM = pl.BlockSpec(memory_space=pltpu.SEMAPHORE)
_DATAFLOW = pltpu.SideEffectType.DATAFLOW_SIDE_EFFECTING


def _in_hbm(a):
    return pltpu.with_memory_space_constraint(a, pltpu.HBM)


def _own_slot(shard, pos, *, name, after=None):
    R, C = shard.shape
    tr = _div(R, 256, 16)

    def body(pos_ref, w_ref, o_ref):
        o_ref[...] = w_ref[...].astype(BF16)

    body, in_specs, args = _ordered_after(body, 2, [pl.BlockSpec((tr, C), lambda i, pos_ref: (i, 0))], (pos, shard), after)
    grid_spec = pltpu.PrefetchScalarGridSpec(
        num_scalar_prefetch=1, grid=(R // tr,), in_specs=in_specs,
        out_specs=pl.BlockSpec((None, tr, C), lambda i, pos_ref: (pos_ref[0], i, 0)))
    return pl.pallas_call(
        body, name=name, grid_spec=grid_spec,
        out_shape=jax.ShapeDtypeStruct((N_DEV, R, C), BF16),
        compiler_params=_cparams("parallel"),
    )(*args)


def _ag_copies(w, land_ref, send_sems, recv_sems):
    x, y, c = _position()
    mine = land_ref.at[_slot(x, y, c)]
    targets = [(px, py, c) for px, py in _other_chips(x, y)] + [(x, y, 1 - c)]
    return [pltpu.make_async_remote_copy(src_ref=mine, dst_ref=mine, send_sem=send_sems.at[4 * w + k],
                                         recv_sem=recv_sems.at[4 * w + k], device_id=to, device_id_type=MESH)
            for k, to in enumerate(targets)]


def _ag_start(buffers, groups, *, name, after=None):
    lands = [buffers[i] for g in groups for i in g]
    n, ng = len(lands), len(groups)
    sizes = [len(g) for g in groups]

    def body(*refs):
        land_refs = refs[:n]
        sems = refs[n:n + 2 * ng]
        token = refs[-1]
        i = 0
        for g in range(ng):
            for w in range(sizes[g]):
                for cp in _ag_copies(w, land_refs[i], sems[2 * g], sems[2 * g + 1]):
                    cp.start()
                i += 1
        token[...] = jnp.zeros_like(token)

    sem_shapes = [pltpu.SemaphoreType.DMA((4 * k,)) for k in sizes for _ in range(2)]
    body, in_specs, args = _ordered_after(body, n, [_HBM] * n, tuple(_in_hbm(a) for a in lands), after)
    outs = pl.pallas_call(
        body, name=name,
        in_specs=in_specs,
        out_specs=tuple([_SEM] * (2 * ng) + [_HBM] * n + [pl.BlockSpec(memory_space=pltpu.VMEM)]),
        out_shape=tuple(sem_shapes + [pltpu.HBM(a.shape, a.dtype) for a in lands] + [jax.ShapeDtypeStruct((8, LANES), F32)]),
        input_output_aliases={i: 2 * ng + i for i in range(n)},
        compiler_params=pltpu.CompilerParams(has_side_effects=_DATAFLOW),
    )(*args)
    sems, thru = outs[:2 * ng], outs[2 * ng:2 * ng + n]
    result, i = [], 0
    for g in range(ng):
        k = sizes[g]
        result.append((sems[2 * g], sems[2 * g + 1], list(thru[i:i + k])))
        i += k
    return result, outs[-1]


def _ag_wait(send_sems, recv_sems, lands, after, *, name):
    n = len(lands)

    def body(*refs):
        land_refs = refs[:n]
        send_ref, recv_ref = refs[n], refs[n + 1]
        token = refs[-1]
        for w in range(n):
            for cp in _ag_copies(w, land_refs[w], send_ref, recv_ref):
                cp.wait_send()
                cp.wait_recv()
        token[...] = jnp.zeros_like(token)

    outs = pl.pallas_call(
        body, name=name,
        in_specs=[_HBM] * n + [_SEM, _SEM, _ANY],
        out_specs=tuple([_HBM] * n + [pl.BlockSpec(memory_space=pltpu.VMEM)]),
        out_shape=tuple([pltpu.HBM(a.shape, a.dtype) for a in lands] + [jax.ShapeDtypeStruct((8, LANES), F32)]),
        input_output_aliases={i: i for i in range(n)},
        compiler_params=pltpu.CompilerParams(has_side_effects=_DATAFLOW),
    )(*lands, send_sems, recv_sems, after)
    return list(outs[:n]), outs[n]


def _ag_forward(lands, *, name, after=None):
    n = len(lands)

    def body(*refs):
        in_refs, out_refs = refs[:n], refs[n:2 * n]
        send_sems, recv_sems = refs[2 * n:]
        x, y, c = _position()
        copies = []
        for w in range(n):
            for k, (px, py) in enumerate(_other_chips(x, y)):
                cp = pltpu.make_async_remote_copy(
                    src_ref=in_refs[w].at[_slot(px, py, c)], dst_ref=out_refs[w].at[_slot(px, py, c)],
                    send_sem=send_sems.at[3 * w + k], recv_sem=recv_sems.at[3 * w + k],
                    device_id=(x, y, 1 - c), device_id_type=MESH)
                cp.start()
                copies.append(cp)
        for cp in copies:
            cp.wait()

    body, in_specs, args = _ordered_after(body, n, [_ANY] * n, tuple(lands), after)
    return pl.pallas_call(
        body, name=name,
        in_specs=in_specs, out_specs=[_ANY] * n,
        out_shape=[jax.ShapeDtypeStruct(a.shape, a.dtype) for a in lands],
        input_output_aliases={i: i for i in range(n)},
        scratch_shapes=[pltpu.SemaphoreType.DMA((3 * n,)), pltpu.SemaphoreType.DMA((3 * n,))],
    )(*args)


def _sibling_copies(w, g8_ref, land_ref, send_sems, recv_sems):
    x, y, c = _position()
    return [pltpu.make_async_remote_copy(src_ref=g8_ref.at[2 * p + (1 - c)], dst_ref=land_ref.at[p],
                                         send_sem=send_sems.at[4 * w + p], recv_sem=recv_sems.at[4 * w + p],
                                         device_id=(x, y, 1 - c), device_id_type=MESH)
            for p in range(4)]


def _chip_copies(w, sums_ref, land_ref, send_sems, recv_sems):
    x, y, c = _position()
    return [pltpu.make_async_remote_copy(src_ref=sums_ref.at[2 * px + py], dst_ref=land_ref.at[k],
                                         send_sem=send_sems.at[3 * w + k], recv_sem=recv_sems.at[3 * w + k],
                                         device_id=(px, py, c), device_id_type=MESH)
            for k, (px, py) in enumerate(_other_chips(x, y))]


def _copies_start(copies, per_weight, srcs, *, name):
    n = len(srcs)
    lands = [lax.empty((per_weight,) + s.shape[1:], s.dtype) for s in srcs]

    def body(*refs):
        src_refs, land_refs = refs[:n], refs[n:2 * n]
        send_sems, recv_sems = refs[2 * n], refs[2 * n + 1]
        token = refs[-1]
        for w in range(n):
            for cp in copies(w, src_refs[w], land_refs[w], send_sems, recv_sems):
                cp.start()
        token[...] = jnp.zeros_like(token)

    outs = pl.pallas_call(
        body, name=name,
        in_specs=[_HBM] * (2 * n),
        out_specs=tuple([_SEM, _SEM] + [_HBM] * (2 * n) + [pl.BlockSpec(memory_space=pltpu.VMEM)]),
        out_shape=tuple([pltpu.SemaphoreType.DMA((per_weight * n,)), pltpu.SemaphoreType.DMA((per_weight * n,))]
                        + [pltpu.HBM(a.shape, a.dtype) for a in srcs + lands] + [jax.ShapeDtypeStruct((8, LANES), F32)]),
        input_output_aliases={i: 2 + i for i in range(2 * n)},
        compiler_params=pltpu.CompilerParams(has_side_effects=_DATAFLOW),
    )(*[_in_hbm(a) for a in srcs + lands])
    return outs[0], outs[1], list(outs[2:2 + n]), list(outs[2 + n:2 + 2 * n]), outs[-1]


def _copies_wait(copies, send_sems, recv_sems, srcs, lands, after, *, name):
    n = len(srcs)

    def body(*refs):
        src_refs, land_refs = refs[:n], refs[n:2 * n]
        send_ref, recv_ref = refs[2 * n], refs[2 * n + 1]
        for w in range(n):
            for cp in copies(w, src_refs[w], land_refs[w], send_ref, recv_ref):
                cp.wait_send()
                cp.wait_recv()

    outs = pl.pallas_call(
        body, name=name,
        in_specs=[_HBM] * (2 * n) + [_SEM, _SEM, _ANY],
        out_specs=tuple([_HBM] * (2 * n)),
        out_shape=tuple(pltpu.HBM(a.shape, a.dtype) for a in srcs + lands),
        input_output_aliases={i: i for i in range(2 * n)},
        compiler_params=pltpu.CompilerParams(has_side_effects=_DATAFLOW),
    )(*srcs, *lands, send_sems, recv_sems, after)
    return list(outs[:n]), list(outs[n:])


def _chip_sums(g8, from_sibling, pos, *, name):
    _, R, C = g8.shape
    tr = _div(R, 512, 16)

    def body(pos_ref, g_ref, s_ref, o_ref):
        o_ref[...] = (g_ref[...].astype(F32) + s_ref[...].astype(F32)).astype(BF16)

    def chip(k, pos_ref):
        return jnp.where(k >= pos_ref[1], k + 1, k)

    grid_spec = pltpu.PrefetchScalarGridSpec(
        num_scalar_prefetch=1, grid=(3, R // tr),
        in_specs=[pl.BlockSpec((None, tr, C), lambda k, i, pos_ref: (2 * chip(k, pos_ref) + pos_ref[2], i, 0)),
                  pl.BlockSpec((None, tr, C), lambda k, i, pos_ref: (chip(k, pos_ref), i, 0))],
        out_specs=pl.BlockSpec((None, tr, C), lambda k, i, pos_ref: (chip(k, pos_ref), i, 0)))
    return pl.pallas_call(
        body, name=name, grid_spec=grid_spec,
        out_shape=jax.ShapeDtypeStruct((4, R, C), BF16),
        compiler_params=_cparams("parallel", "parallel"),
    )(pos, g8, from_sibling)


def _small_all_reduce(packed, after, *, name):
    R, L = packed.shape

    def body(x_ref, sum_ref, gath_ref, send_sems, recv_sems, local_sem):
        x, y, c = _position()
        me, sibling = (x, y, c), (x, y, 1 - c)
        chips = _other_chips(x, y)

        def rows(px, py, pc):
            return gath_ref.at[pl.ds(_slot(px, py, pc) * R, R), :]

        def copy(k, block, to, src=None):
            return pltpu.make_async_remote_copy(
                src_ref=rows(*block) if src is None else src, dst_ref=rows(*block),
                send_sem=send_sems.at[k], recv_sem=recv_sems.at[k], device_id=to, device_id_type=MESH)

        mine = pltpu.make_async_copy(x_ref, rows(*me), local_sem)
        mine.start()
        first = [copy(0, me, sibling, src=x_ref)]
        first += [copy(1 + j, me, (*chip, c), src=x_ref) for j, chip in enumerate(chips)]
        for cp in first:
            cp.start()
        passed = [copy(4 + j, (*chip, c), sibling) for j, chip in enumerate(chips)]
        for j, chip in enumerate(chips):
            copy(1 + j, (*chip, c), me).wait_recv()
            passed[j].start()
        copy(0, sibling, me).wait_recv()
        for j, chip in enumerate(chips):
            copy(4 + j, (*chip, 1 - c), me).wait_recv()
        for cp in first + passed:
            cp.wait_send()
        mine.wait()
        acc = gath_ref[0:R, :]
        for d in range(1, N_DEV):
            acc = acc + gath_ref[d * R:(d + 1) * R, :]
        sum_ref[...] = acc

    vmem = pl.BlockSpec(memory_space=pltpu.VMEM)
    body, in_specs, args = _ordered_after(body, 1, [vmem], (packed,), after)
    return pl.pallas_call(
        body, name=name, in_specs=in_specs, out_specs=vmem,
        out_shape=jax.ShapeDtypeStruct((R, L), F32),
        scratch_shapes=[pltpu.VMEM((N_DEV * R, L), F32), pltpu.SemaphoreType.DMA((7,)), pltpu.SemaphoreType.DMA((7,)),
                        pltpu.SemaphoreType.DMA],
        compiler_params=pltpu.CompilerParams(vmem_limit_bytes=VMEM_LIMIT),
    )(*args)


def _adamw_math(w, g, m, v):
    m = ADAM_B1 * m + (1.0 - ADAM_B1) * g
    v = ADAM_B2 * v + (1.0 - ADAM_B2) * (g * g)
    m_hat = m / (1.0 - ADAM_B1 ** ADAM_STEP)
    v_hat = v / (1.0 - ADAM_B2 ** ADAM_STEP)
    delta = -ADAM_LR * (m_hat / (jnp.sqrt(v_hat) + ADAM_EPS) + ADAM_WD * w)
    return delta, m, v


def _adamw_shard(w, m, v, g8, from_sibling, from_chips, pos, *, name):
    R, C = w.shape
    tr = _div(R, 256, 16)

    def body(pos_ref, w_ref, m_ref, v_ref, g_ref, s_ref, r_ref, go_ref, d_ref, mo_ref, vo_ref):
        g = g_ref[...].astype(F32) + s_ref[...].astype(F32)
        for k in range(3):
            g = g + r_ref[k].astype(F32)
        delta, m_, v_ = _adamw_math(w_ref[...], g, m_ref[...], v_ref[...])
        go_ref[...] = g
        d_ref[...] = delta
        mo_ref[...] = m_
        vo_ref[...] = v_

    blk = pl.BlockSpec((tr, C), lambda i, pos_ref: (i, 0))
    grid_spec = pltpu.PrefetchScalarGridSpec(
        num_scalar_prefetch=1, grid=(R // tr,),
        in_specs=[blk, blk, blk,
                  pl.BlockSpec((None, tr, C), lambda i, pos_ref: (pos_ref[0], i, 0)),
                  pl.BlockSpec((None, tr, C), lambda i, pos_ref: (pos_ref[1], i, 0)),
                  pl.BlockSpec((3, tr, C), lambda i, pos_ref: (0, i, 0))],
        out_specs=[blk] * 4)
    out = jax.ShapeDtypeStruct((R, C), F32)
    return pl.pallas_call(
        body, name=name, grid_spec=grid_spec, out_shape=[out] * 4,
        compiler_params=_cparams("parallel"),
    )(pos, w, m, v, g8, from_sibling, from_chips)


def _adamw_small(w, g, m, v, *, name):
    R, L = w.shape

    def body(w_ref, g_ref, m_ref, v_ref, d_ref, mo_ref, vo_ref):
        delta, m_, v_ = _adamw_math(w_ref[...], g_ref[...], m_ref[...], v_ref[...])
        d_ref[...] = delta
        mo_ref[...] = m_
        vo_ref[...] = v_

    vmem = pl.BlockSpec(memory_space=pltpu.VMEM)
    out = jax.ShapeDtypeStruct((R, L), F32)
    return pl.pallas_call(body, name=name, in_specs=[vmem] * 4, out_specs=[vmem] * 3, out_shape=[out] * 3)(w, g, m, v)


_TILE = 8 * LANES


def _pack(pieces):
    rows = []
    for p in pieces:
        flat = p.reshape(-1).astype(F32)
        padded = -(-flat.shape[0] // _TILE) * _TILE
        rows.append(jnp.pad(flat, (0, padded - flat.shape[0])).reshape(-1, LANES))
    return jnp.concatenate(rows, axis=0)


def _unpack(packed, like):
    out, r = [], 0
    for p in like:
        size = int(np.prod(p.shape)) if p.shape else 1
        nrows = -(-size // _TILE) * 8
        out.append(packed[r:r + nrows].reshape(-1)[:size].reshape(p.shape))
        r += nrows
    return out


_BIG = ("w_in", "w_a_out", "w_b_out", "w_o", "w_gate", "w_up", "w_down")
_TRANSPOSED = ("w_in", "w_gate", "w_up")
_COL_SHARDED = ("w_a_out", "w_b_out")
_GATHER_GROUPS = (("w_in",), ("w_a_out", "w_b_out", "w_o"), ("w_gate",), ("w_up",), ("w_down",))
_START_AFTER_WAIT = {0: (1, 2), 1: (3,), 2: (4,)}
_SMALL = ("norm_mix", "sgu_v_gain", "sgu_w_s", "sgu_b_s", "attn_sink", "rel_bias", "norm_ffn", "norm_final")
_ORDER = ("w_in", "norm_mix", "sgu_v_gain", "sgu_w_s", "sgu_b_s", "w_a_out", "attn_sink", "rel_bias", "w_b_out", "w_o",
          "norm_ffn", "w_gate", "w_up", "w_down", "norm_final")


def _shard(name, a):
    return jnp.swapaxes(a, 1, 2)[0] if name in _TRANSPOSED else a[0]


def _unshard(name, a):
    return jnp.swapaxes(a[None], 1, 2) if name in _TRANSPOSED else a[None]


def _whole(name, gathered):
    _, r, c = gathered.shape
    return gathered if name in _COL_SHARDED else gathered.reshape(N_DEV * r, c)


def _blocks(name, grad):
    if name in _COL_SHARDED:
        return grad
    r, c = grad.shape
    return grad.reshape(N_DEV, r // N_DEV, c)


def kernel(x, w_in, norm_mix, sgu_v_gain, sgu_w_s, sgu_b_s, w_a_out, attn_sink, rel_bias, w_b_out, w_o, norm_ffn, w_gate, w_up, w_down, norm_final, loss_target, m_w_in, m_norm_mix, m_sgu_v_gain, m_sgu_w_s, m_sgu_b_s, m_w_a_out, m_attn_sink, m_rel_bias, m_w_b_out, m_w_o, m_norm_ffn, m_w_gate, m_w_up, m_w_down, m_norm_final, v_w_in, v_norm_mix, v_sgu_v_gain, v_sgu_w_s, v_sgu_b_s, v_w_a_out, v_attn_sink, v_rel_bias, v_w_b_out, v_w_o, v_norm_ffn, v_w_gate, v_w_up, v_w_down, v_norm_final):
    w = dict(w_in=w_in, norm_mix=norm_mix, sgu_v_gain=sgu_v_gain, sgu_w_s=sgu_w_s, sgu_b_s=sgu_b_s, w_a_out=w_a_out,
             attn_sink=attn_sink, rel_bias=rel_bias, w_b_out=w_b_out, w_o=w_o, norm_ffn=norm_ffn, w_gate=w_gate,
             w_up=w_up, w_down=w_down, norm_final=norm_final)
    m = dict(w_in=m_w_in, norm_mix=m_norm_mix, sgu_v_gain=m_sgu_v_gain, sgu_w_s=m_sgu_w_s, sgu_b_s=m_sgu_b_s,
             w_a_out=m_w_a_out, attn_sink=m_attn_sink, rel_bias=m_rel_bias, w_b_out=m_w_b_out, w_o=m_w_o,
             norm_ffn=m_norm_ffn, w_gate=m_w_gate, w_up=m_w_up, w_down=m_w_down, norm_final=m_norm_final)
    v = dict(w_in=v_w_in, norm_mix=v_norm_mix, sgu_v_gain=v_sgu_v_gain, sgu_w_s=v_sgu_w_s, sgu_b_s=v_sgu_b_s,
             w_a_out=v_w_a_out, attn_sink=v_attn_sink, rel_bias=v_rel_bias, w_b_out=v_w_b_out, w_o=v_w_o,
             norm_ffn=v_norm_ffn, w_gate=v_w_gate, w_up=v_w_up, w_down=v_w_down, norm_final=v_norm_final)
    xc, yc, cc = _position()
    pos = jnp.stack([_slot(xc, yc, cc), 2 * xc + yc, cc]).astype(jnp.int32)

    in_flight, full, slots = {}, {}, {}

    def start_gather(groups, after):
        names = [n for gi in groups for n in _GATHER_GROUPS[gi]]
        flights, token = _ag_start([slots[n] for n in names], [[names.index(n) for n in _GATHER_GROUPS[gi]] for gi in groups],
                                   name="ag_start_%d" % groups[0], after=after)
        in_flight.update(zip(groups, flights))
        return token

    def weight(name, after):
        if name not in full:
            gi = next(i for i, grp in enumerate(_GATHER_GROUPS) if name in grp)
            send_sems, recv_sems, lands = in_flight[gi]
            lands, token = _ag_wait(send_sems, recv_sems, lands, after, name="ag_wait_%d" % gi)
            started = start_gather(_START_AFTER_WAIT[gi], token) if gi in _START_AFTER_WAIT else None
            gathered = _ag_forward(lands, name="ag_forward_%d" % gi, after=started)
            full.update({n: _whole(n, g) for n, g in zip(_GATHER_GROUPS[gi], gathered)})
        return full[name]

    for n in _GATHER_GROUPS[0]:
        slots[n] = _own_slot(_shard(n, w[n]), pos, name="own_slot_" + n)
    first_started = start_gather((0,), None)
    for grp in _GATHER_GROUPS[1:]:
        for n in grp:
            slots[n] = _own_slot(_shard(n, w[n]), pos, name="own_slot_" + n, after=first_started)

    to_sibling, reducing = [], {}

    def emit(names, grads):
        g8 = [_blocks(n, g) for n, g in zip(names, grads)]
        send_sems, recv_sems, g8, lands, token = _copies_start(_sibling_copies, 4, g8, name="rs_sibling_start_" + names[0])
        to_sibling.append((names, send_sems, recv_sems, g8, lands))
        return token

    def flush(after):
        names, send_sems, recv_sems, g8, lands = to_sibling.pop()
        g8, from_sibling = _copies_wait(_sibling_copies, send_sems, recv_sems, g8, lands, after,
                                        name="rs_sibling_wait_" + names[0])
        sums4 = [_chip_sums(g, s, pos, name="chip_sums_" + n) for n, g, s in zip(names, g8, from_sibling)]
        send_sems, recv_sems, sums4, lands, token = _copies_start(_chip_copies, 3, sums4, name="rs_chips_start_" + names[0])
        reducing[names] = (g8, from_sibling, send_sems, recv_sems, sums4, lands)
        return token

    loss, grad_x, small_grads_local = _local_step(
        x[0], loss_target[0], weight, emit, flush, norm_mix, sgu_v_gain, sgu_w_s[0], sgu_b_s[0], attn_sink, rel_bias,
        norm_ffn, norm_final[None])

    out_g, out_d, out_m, out_v = {}, {}, {}, {}
    small_like = [w[n] for n in _SMALL]
    small_w = _pack(small_like)
    packed = _pack([small_grads_local[n] for n in _SMALL] + [loss[0, 0]])
    after = grad_x
    for gi, (names, (g8, from_sibling, send_sems, recv_sems, sums4, lands)) in enumerate(reducing.items()):
        if gi == len(reducing) - 1:
            summed = _small_all_reduce(packed, after, name="small_all_reduce")
            after = summed
        _, from_chips = _copies_wait(_chip_copies, send_sems, recv_sems, sums4, lands, after,
                                     name="rs_chips_wait_" + names[0])
        for i, n in enumerate(names):
            g, d, m_, v_ = _adamw_shard(_shard(n, w[n]), _shard(n, m[n]), _shard(n, v[n]), g8[i], from_sibling[i],
                                        from_chips[i], pos, name="adamw_" + n)
            out_g[n], out_d[n], out_m[n], out_v[n] = (_unshard(n, o) for o in (g, d, m_, v_))
            after = d
    *small_grads, loss_sum = _unpack(summed, small_like + [jax.ShapeDtypeStruct((), F32)])
    d_s, m_s, v_s = _adamw_small(small_w, summed[:small_w.shape[0]], _pack([m[n] for n in _SMALL]),
                                 _pack([v[n] for n in _SMALL]), name="adamw_small")
    for n, g, d, m_, v_ in zip(_SMALL, small_grads, _unpack(d_s, small_like), _unpack(m_s, small_like), _unpack(v_s, small_like)):
        out_g[n], out_d[n], out_m[n], out_v[n] = g, d, m_, v_

    return (loss_sum, grad_x[None], *[out_g[n] for n in _ORDER], *[out_d[n] for n in _ORDER],
            *[out_m[n] for n in _ORDER], *[out_v[n] for n in _ORDER])
```

```python
import functools
import math

import numpy as np
import jax
import jax.numpy as jnp
from jax import lax
from jax.experimental import pallas as pl
from jax.experimental.pallas import tpu as pltpu

F32 = jnp.float32
BF16 = jnp.bfloat16

EPS = 1e-6
NEG = -1e30
HEAD_DIM = 128
BLOCK = 128
N_KV_HEADS = 2
KV_WIDTH = N_KV_HEADS * HEAD_DIM
REL_BUCKETS = 32
REL_MAX_DIST = 128

ADAM_LR = 0.001
ADAM_B1 = 0.9
ADAM_B2 = 0.999
ADAM_EPS = 1e-08
ADAM_WD = 0.01
ADAM_STEP = 10

N_DEV = 8
LANES = 128
VMEM_LIMIT = 56 * 1024 * 1024
MESH = pl.DeviceIdType.MESH


def _cparams(*sem):
    return pltpu.CompilerParams(dimension_semantics=sem, vmem_limit_bytes=VMEM_LIMIT)


def _div(n, target, mult=LANES):
    best = None
    for d in range(mult, min(n, target) + 1, mult):
        if n % d == 0:
            best = d
    assert best is not None, (n, target, mult)
    return best


_ANY = pl.BlockSpec(memory_space=pl.ANY)


def _ordered_after(body, n_inputs, in_specs, args, after):
    if after is None:
        return body, in_specs, args
    extra = tuple(after) if isinstance(after, (tuple, list)) else (after,)

    def wrapped(*refs):
        return body(*refs[:n_inputs], *refs[n_inputs + len(extra):])

    return wrapped, list(in_specs) + [_ANY] * len(extra), tuple(args) + extra


def _bucket_map():
    nb = REL_BUCKETS // 2
    qi = np.arange(BLOCK)[:, None]
    kj = np.arange(3 * BLOCK)[None, :]
    rel = kj - BLOCK - qi
    ret = np.where(rel > 0, nb, 0)
    n = np.abs(rel)
    max_exact = nb // 2
    nf = np.maximum(n, 1).astype(np.float32)
    large = max_exact + (np.log(nf / np.float32(max_exact)) / np.float32(math.log(REL_MAX_DIST / max_exact))
                         * np.float32(nb - max_exact)).astype(np.int32)
    large = np.minimum(large, nb - 1)
    return (ret + np.where(n < max_exact, n, large)).astype(np.int32)


_GELU_C = math.sqrt(2.0 / math.pi)
_GELU_A = 0.044715


def _gelu(x):
    t = jnp.tanh(_GELU_C * (x + _GELU_A * (x * x * x)))
    return 0.5 * x * (1.0 + t)


def _gelu_and_grad(x):
    x2 = x * x
    t = jnp.tanh(_GELU_C * (x + _GELU_A * (x2 * x)))
    g = 0.5 * x * (1.0 + t)
    dg = 0.5 * (1.0 + t) + 0.5 * x * (1.0 - t * t) * (_GELU_C * (1.0 + 3.0 * _GELU_A * x2))
    return g, dg


def _sigmoid(x):
    return 1.0 / (1.0 + jnp.exp(-x))


def _mm(a, b, *, name, ta=False, tb=False, add=None, out_dtype=F32, bm=1024, bn=1024, bk=None, after=None,
        row_blocks=None, into=None):
    if ta:
        K, M = a.shape
    else:
        M, K = a.shape
    N = b.shape[0] if tb else b.shape[1]
    assert (b.shape[1] if tb else b.shape[0]) == K
    bm = _div(M, bm)
    bn = _div(N, bn)
    bk = K if bk is None else _div(K, bk)
    nk = K // bk
    i0, ni = (0, M // bm) if row_blocks is None else row_blocks
    a_spec = (pl.BlockSpec((bk, bm), lambda i, j, k: (k, i + i0)) if ta
              else pl.BlockSpec((bm, bk), lambda i, j, k: (i + i0, k)))
    b_spec = pl.BlockSpec((bn, bk), lambda i, j, k: (j, k)) if tb else pl.BlockSpec((bk, bn), lambda i, j, k: (k, j))
    o_spec = pl.BlockSpec((bm, bn), lambda i, j, k: (i + i0, j))
    dims = (((0 if ta else 1,), (1 if tb else 0,)), ((), ()))
    has_add = add is not None

    def body(*refs):
        if has_add:
            a_ref, b_ref, add_ref, o_ref, *scratch = refs
        else:
            a_ref, b_ref, o_ref, *scratch = refs
            add_ref = None
        p = lax.dot_general(a_ref[...].astype(BF16), b_ref[...].astype(BF16), dims, preferred_element_type=F32)
        if nk == 1:
            if has_add:
                p = p + add_ref[...]
            o_ref[...] = p.astype(out_dtype)
        else:
            acc = scratch[0]
            k = pl.program_id(2)

            @pl.when(k == 0)
            def _():
                acc[...] = p

            @pl.when(k > 0)
            def _():
                acc[...] += p

            @pl.when(k == nk - 1)
            def _():
                r = acc[...]
                if has_add:
                    r = r + add_ref[...]
                o_ref[...] = r.astype(out_dtype)

    in_specs = [a_spec, b_spec] + ([o_spec] if has_add else [])
    args = (a, b) + ((add,) if has_add else ())
    aliases = {}
    if into is not None:
        body, in_specs, args = _ordered_after(body, len(args), in_specs, args, into)
        aliases = {len(args) - 1: 0}
    body, in_specs, args = _ordered_after(body, len(args), in_specs, args, after)
    return pl.pallas_call(
        body, name=name, grid=(ni, N // bn, nk),
        in_specs=in_specs, out_specs=o_spec,
        out_shape=jax.ShapeDtypeStruct((M, N), out_dtype),
        input_output_aliases=aliases,
        scratch_shapes=[pltpu.VMEM((bm, bn), F32)] if nk > 1 else [],
        compiler_params=_cparams("parallel", "parallel", "arbitrary"),
    )(*args)


def _mm_resid_rms(a, b, resid, gain, *, name, bm=512):
    M, K = a.shape
    N = b.shape[1]
    bm = _div(M, bm)

    def body(a_ref, b_ref, r_ref, g_ref, x_ref, h_ref):
        x = r_ref[...] + jnp.dot(a_ref[...], b_ref[...], preferred_element_type=F32)
        x_ref[...] = x
        r = lax.rsqrt(jnp.mean(x * x, axis=-1, keepdims=True) + EPS)
        h_ref[...] = ((x * r) * g_ref[...]).astype(BF16)

    row = pl.BlockSpec((bm, N), lambda i: (i, 0))
    return pl.pallas_call(
        body, name=name, grid=(M // bm,),
        in_specs=[pl.BlockSpec((bm, K), lambda i: (i, 0)), pl.BlockSpec((K, N), lambda i: (0, 0)), row,
                  pl.BlockSpec((1, N), lambda i: (0, 0))],
        out_specs=[row, row], out_shape=[jax.ShapeDtypeStruct((M, N), F32), jax.ShapeDtypeStruct((M, N), BF16)],
        compiler_params=_cparams("parallel"),
    )(a, b, resid, gain)


def _mm_sum2(a1, b1, a2, b2, *, name, bm=1024, bn=512, bk=2816, after=None):
    M, K = a1.shape
    N = b1.shape[1]
    bm, bn, bk = _div(M, bm), _div(N, bn), _div(K, bk)
    nk = K // bk

    def body(a1_ref, b1_ref, a2_ref, b2_ref, o_ref, acc):
        p = (jnp.dot(a1_ref[...], b1_ref[...], preferred_element_type=F32)
             + jnp.dot(a2_ref[...], b2_ref[...], preferred_element_type=F32))
        k = pl.program_id(2)

        @pl.when(k == 0)
        def _():
            acc[...] = p

        @pl.when(k > 0)
        def _():
            acc[...] += p

        @pl.when(k == nk - 1)
        def _():
            o_ref[...] = acc[...]

    a_spec = pl.BlockSpec((bm, bk), lambda i, j, k: (i, k))
    b_spec = pl.BlockSpec((bk, bn), lambda i, j, k: (k, j))
    body, in_specs, args = _ordered_after(body, 4, [a_spec, b_spec, a_spec, b_spec], (a1, b1, a2, b2), after)
    return pl.pallas_call(
        body, name=name, grid=(M // bm, N // bn, nk),
        in_specs=in_specs, out_specs=pl.BlockSpec((bm, bn), lambda i, j, k: (i, j)),
        out_shape=jax.ShapeDtypeStruct((M, N), F32),
        scratch_shapes=[pltpu.VMEM((bm, bn), F32)],
        compiler_params=_cparams("parallel", "parallel", "arbitrary"),
    )(*args)


def _blocks_per_tile(c):
    nb = 1
    while (nb * c) % LANES or (nb * c < 1024 and nb < N_DEV):
        nb *= 2
    assert nb <= N_DEV and (nb * c) % LANES == 0, c
    return nb


def _mm_w8(a, w8, *, name, bm=1024, out_dtype=F32):
    M, K = a.shape
    _, _, c = w8.shape
    nb = _blocks_per_tile(c)
    bm = _div(M, bm)

    def body(a_ref, w_ref, o_ref):
        a_ = a_ref[...]
        for t in range(nb):
            o_ref[:, t * c:(t + 1) * c] = jnp.dot(a_, w_ref[t], preferred_element_type=F32).astype(out_dtype)

    return pl.pallas_call(
        body, name=name, grid=(M // bm, N_DEV // nb),
        in_specs=[pl.BlockSpec((bm, K), lambda i, j: (i, 0)), pl.BlockSpec((nb, K, c), lambda i, j: (j, 0, 0))],
        out_specs=pl.BlockSpec((bm, nb * c), lambda i, j: (i, j)),
        out_shape=jax.ShapeDtypeStruct((M, N_DEV * c), out_dtype),
        compiler_params=_cparams("parallel", "parallel"),
    )(a, w8)


def _mm_w8t(dy, w8, *, name, add=None, out_dtype=F32, bm=1024, bn=1024, after=None, lead=None):
    M = dy.shape[-2]
    _, K, c = w8.shape
    nb = _blocks_per_tile(c)
    nk = N_DEV // nb
    bm, bn = _div(M, bm), _div(K, bn)
    has_add = add is not None
    dims = (((1,), (1,)), ((), ()))

    def body(*refs):
        if has_add:
            dy_ref, w_ref, add_ref, o_ref, acc = refs
        else:
            dy_ref, w_ref, o_ref, acc = refs
        p = lax.dot_general(dy_ref[:, 0:c], w_ref[0], dims, preferred_element_type=F32)
        for t in range(1, nb):
            p = p + lax.dot_general(dy_ref[:, t * c:(t + 1) * c], w_ref[t], dims, preferred_element_type=F32)
        k = pl.program_id(2)

        @pl.when(k == 0)
        def _():
            acc[...] = p

        @pl.when(k > 0)
        def _():
            acc[...] += p

        @pl.when(k == nk - 1)
        def _():
            r = acc[...]
            if has_add:
                r = r + add_ref[...]
            o_ref[...] = r.astype(out_dtype)

    o_spec = pl.BlockSpec((bm, bn), lambda i, j, k: (i, j))
    dy_spec = (pl.BlockSpec((bm, nb * c), lambda i, j, k: (i, k)) if lead is None
               else pl.BlockSpec((None, bm, nb * c), lambda i, j, k: (lead, i, k)))
    in_specs = [dy_spec, pl.BlockSpec((nb, bn, c), lambda i, j, k: (k, j, 0))]
    in_specs += [o_spec] if has_add else []
    args = (dy, w8) + ((add,) if has_add else ())
    body, in_specs, args = _ordered_after(body, len(args), in_specs, args, after)
    return pl.pallas_call(
        body, name=name, grid=(M // bm, K // bn, nk),
        in_specs=in_specs, out_specs=o_spec,
        out_shape=jax.ShapeDtypeStruct((M, K), out_dtype),
        scratch_shapes=[pltpu.VMEM((bm, bn), F32)],
        compiler_params=_cparams("parallel", "parallel", "arbitrary"),
    )(*args)


def _mm_gw8(x, dy, c, *, name, bk=1024, lead=None):
    T, K = x.shape
    nb = _blocks_per_tile(c)
    bk = _div(K, bk)
    dims = (((0,), (0,)), ((), ()))

    def body(x_ref, dy_ref, o_ref):
        x_ = x_ref[...]
        for t in range(nb):
            o_ref[t] = lax.dot_general(x_, dy_ref[:, t * c:(t + 1) * c], dims, preferred_element_type=F32).astype(BF16)

    dy_spec = (pl.BlockSpec((T, nb * c), lambda i, j: (0, j)) if lead is None
               else pl.BlockSpec((None, T, nb * c), lambda i, j: (lead, 0, j)))
    return pl.pallas_call(
        body, name=name, grid=(K // bk, N_DEV // nb),
        in_specs=[pl.BlockSpec((T, bk), lambda i, j: (0, i)), dy_spec],
        out_specs=pl.BlockSpec((nb, bk, c), lambda i, j: (j, i, 0)),
        out_shape=jax.ShapeDtypeStruct((N_DEV, K, c), BF16),
        compiler_params=_cparams("parallel", "parallel"),
    )(x, dy)


def _rms_fwd(x, g, *, name, after=None):
    T, D = x.shape
    tm = _div(T, 256, 8)

    def body(x_ref, g_ref, h_ref):
        xf = x_ref[...]
        r = lax.rsqrt(jnp.mean(xf * xf, axis=-1, keepdims=True) + EPS)
        h_ref[...] = ((xf * r) * g_ref[...]).astype(BF16)

    in_specs = [pl.BlockSpec((tm, D), lambda i: (i, 0)), pl.BlockSpec((1, D), lambda i: (0, 0))]
    body, in_specs, args = _ordered_after(body, 2, in_specs, (x, g), after)
    return pl.pallas_call(
        body, name=name, grid=(T // tm,),
        in_specs=in_specs,
        out_specs=pl.BlockSpec((tm, D), lambda i: (i, 0)),
        out_shape=jax.ShapeDtypeStruct((T, D), BF16),
        compiler_params=_cparams("parallel"),
    )(*args)


def _rms_bwd(x, g, dh, dres, *, name, want_bf16, after=None):
    T, D = x.shape
    tm = _div(T, 256, 8)

    def body(x_ref, g_ref, dh_ref, dres_ref, dx_ref, *rest):
        if want_bf16:
            dxb_ref, dg_ref = rest
        else:
            (dg_ref,) = rest
        xf = x_ref[...]
        r = lax.rsqrt(jnp.mean(xf * xf, axis=-1, keepdims=True) + EPS)
        xhat = xf * r
        dh_ = dh_ref[...]
        dy = dh_ * g_ref[...]
        dx = dres_ref[...] + r * (dy - xhat * jnp.mean(dy * xhat, axis=-1, keepdims=True))
        dx_ref[...] = dx
        if want_bf16:
            dxb_ref[...] = dx.astype(BF16)
        part = jnp.sum(dh_ * xhat, axis=0, keepdims=True)

        @pl.when(pl.program_id(0) == 0)
        def _():
            dg_ref[...] = part

        @pl.when(pl.program_id(0) > 0)
        def _():
            dg_ref[...] += part

    row = pl.BlockSpec((tm, D), lambda i: (i, 0))
    vec = pl.BlockSpec((1, D), lambda i: (0, 0))
    out_specs = [row] + ([row] if want_bf16 else []) + [vec]
    out_shape = ([jax.ShapeDtypeStruct((T, D), F32)] + ([jax.ShapeDtypeStruct((T, D), BF16)] if want_bf16 else [])
                 + [jax.ShapeDtypeStruct((1, D), F32)])
    body, in_specs, args = _ordered_after(body, 4, [row, vec, row, row], (x, g, dh, dres), after)
    return pl.pallas_call(
        body, name=name, grid=(T // tm,),
        in_specs=in_specs, out_specs=out_specs, out_shape=out_shape,
        compiler_params=_cparams("arbitrary"),
    )(*args)


def _loss_head(x, g, target, *, name):
    T, D = x.shape
    tm = _div(T, 256, 8)

    def body(x_ref, g_ref, t_ref, loss_ref, dx_ref, dxb_ref, dg_ref):
        xf = x_ref[...]
        r = lax.rsqrt(jnp.mean(xf * xf, axis=-1, keepdims=True) + EPS)
        xhat = xf * r
        gain = g_ref[...]
        err = xhat * gain - t_ref[...]
        lpart = 0.5 * jnp.sum(jnp.mean(err * err, axis=-1, keepdims=True), axis=0, keepdims=True)
        dh_ = err * (1.0 / D)
        dy = dh_ * gain
        dx = r * (dy - xhat * jnp.mean(dy * xhat, axis=-1, keepdims=True))
        dx_ref[...] = dx
        dxb_ref[...] = dx.astype(BF16)
        part = jnp.sum(dh_ * xhat, axis=0, keepdims=True)

        @pl.when(pl.program_id(0) == 0)
        def _():
            dg_ref[...] = part
            loss_ref[...] = jnp.broadcast_to(lpart, loss_ref.shape)

        @pl.when(pl.program_id(0) > 0)
        def _():
            dg_ref[...] += part
            loss_ref[...] += jnp.broadcast_to(lpart, loss_ref.shape)

    row = pl.BlockSpec((tm, D), lambda i: (i, 0))
    vec = pl.BlockSpec((1, D), lambda i: (0, 0))
    return pl.pallas_call(
        body, name=name, grid=(T // tm,),
        in_specs=[row, vec, row],
        out_specs=[pl.BlockSpec((8, LANES), lambda i: (0, 0)), row, row, vec],
        out_shape=[jax.ShapeDtypeStruct((8, LANES), F32), jax.ShapeDtypeStruct((T, D), F32),
                   jax.ShapeDtypeStruct((T, D), BF16), jax.ShapeDtypeStruct((1, D), F32)],
        compiler_params=_cparams("arbitrary"),
    )(x, g, target)


def _gate_cols(D):
    off_a = 3 * D // 2 + 2 * KV_WIDTH
    off_b = off_a + D
    cw = math.gcd(math.gcd(off_a, off_b), math.gcd(D, 512))
    return cw, off_a // cw, off_b // cw


def _merge_fwd(z, ya, yb, *, name):
    T, D = ya.shape
    cw, ba, bb = _gate_cols(D)
    tm = _div(T, 512, 8)

    def body(ga_ref, gb_ref, ya_ref, yb_ref, m_ref):
        m_ref[...] = (_sigmoid(ga_ref[...].astype(F32)) * ya_ref[...]
                      + _sigmoid(gb_ref[...].astype(F32)) * yb_ref[...]).astype(BF16)

    blk = pl.BlockSpec((tm, cw), lambda i, j: (i, j))
    return pl.pallas_call(
        body, name=name, grid=(T // tm, D // cw),
        in_specs=[pl.BlockSpec((tm, cw), lambda i, j: (i, ba + j)), pl.BlockSpec((tm, cw), lambda i, j: (i, bb + j)), blk, blk],
        out_specs=blk, out_shape=jax.ShapeDtypeStruct((T, D), BF16),
        compiler_params=_cparams("parallel", "parallel"),
    )(z, z, ya, yb)


def _merge_bwd(z, ya, yb, dm, *, name, after=None):
    T, D = ya.shape
    cw, ba, bb = _gate_cols(D)
    nj = D // cw
    assert bb == ba + nj
    tm = _div(T, 512, 8)

    def body(g_ref, ya_ref, yb_ref, dm_ref, dy_ref, dz_ref):
        sig = _sigmoid(g_ref[...].astype(F32))
        dm_ = dm_ref[...]
        y = jnp.where(pl.program_id(1) == 0, ya_ref[...], yb_ref[...])
        dy_ref[...] = (dm_ * sig).astype(BF16)
        dz_ref[...] = (dm_ * y * (sig * (1.0 - sig))).astype(BF16)

    in_specs = [pl.BlockSpec((tm, cw), lambda i, s, j: (i, ba + s * nj + j)),
                pl.BlockSpec((tm, cw), lambda i, s, j: (i, j * (1 - s))),
                pl.BlockSpec((tm, cw), lambda i, s, j: (i, j * s)),
                pl.BlockSpec((tm, cw), lambda i, s, j: (i, j))]
    body, in_specs, args = _ordered_after(body, 4, in_specs, (z, ya, yb, dm), after)
    return pl.pallas_call(
        body, name=name, grid=(T // tm, 2, nj),
        in_specs=in_specs,
        out_specs=[pl.BlockSpec((None, tm, cw), lambda i, s, j: (s, i, j)),
                   pl.BlockSpec((tm, cw), lambda i, s, j: (i, ba + s * nj + j))],
        out_shape=[jax.ShapeDtypeStruct((2, T, D), BF16), jax.ShapeDtypeStruct(z.shape, BF16)],
        compiler_params=_cparams("parallel", "arbitrary", "arbitrary"),
    )(*args)


def _swiglu_mm_fwd(h, wu_t, gate, *, name, bm=1024, bn=512):
    T, D = h.shape
    F = wu_t.shape[0]
    bm, bn = _div(T, bm), _div(F, bn)

    rc = _div(bm, 256, 16)

    def body(h_ref, wu_ref, gin_ref, g_ref, u_ref, act_ref):
        w = wu_ref[...]
        for r in range(0, bm, rc):
            rows = slice(r, r + rc)
            u = lax.dot_general(h_ref[rows, :], w, (((1,), (1,)), ((), ())), preferred_element_type=F32)
            g = gin_ref[rows, :]
            g_ref[rows, :] = g.astype(BF16)
            u_ref[rows, :] = u.astype(BF16)
            act_ref[rows, :] = (g * _sigmoid(g) * u).astype(BF16)

    o_spec = pl.BlockSpec((bm, bn), lambda i, j: (i, j))
    return pl.pallas_call(
        body, name=name, grid=(T // bm, F // bn),
        in_specs=[pl.BlockSpec((bm, D), lambda i, j: (i, 0)), pl.BlockSpec((bn, D), lambda i, j: (j, 0)), o_spec],
        out_specs=[o_spec] * 3, out_shape=[jax.ShapeDtypeStruct((T, F), BF16)] * 3,
        compiler_params=_cparams("parallel", "parallel"),
    )(h, wu_t, gate)


def _swiglu_mm_bwd(dx, w_down, gate, up, *, name, bm=2048, bn=512, after=None):
    T, D = dx.shape
    F = w_down.shape[0]
    bm, bn = _div(T, bm), _div(F, bn)
    dims = (((1,), (1,)), ((), ()))

    rc = _div(bm, 256, 16)

    def body(dx_ref, w_ref, g_ref, u_ref, dg_ref, du_ref):
        w = w_ref[...]
        for r in range(0, bm, rc):
            rows = slice(r, r + rc)
            d = lax.dot_general(dx_ref[rows, :], w, dims, preferred_element_type=F32)
            g = g_ref[rows, :].astype(F32)
            s = _sigmoid(g)
            silu = g * s
            dg_ref[rows, :] = (d * u_ref[rows, :].astype(F32) * (s + silu * (1.0 - s))).astype(BF16)
            du_ref[rows, :] = (d * silu).astype(BF16)

    o_spec = pl.BlockSpec((bm, bn), lambda i, j: (i, j))
    in_specs = [pl.BlockSpec((bm, D), lambda i, j: (i, 0)), pl.BlockSpec((bn, D), lambda i, j: (j, 0)), o_spec, o_spec]
    body, in_specs, args = _ordered_after(body, 4, in_specs, (dx, w_down, gate, up), after)
    out = jax.ShapeDtypeStruct((T, F), BF16)
    return pl.pallas_call(
        body, name=name, grid=(T // bm, F // bn), in_specs=in_specs, out_specs=[o_spec, o_spec], out_shape=[out, out],
        compiler_params=_cparams("parallel", "parallel"),
    )(*args)


def _sgu_fwd(z, gain, ws_b, bs_t, *, name):
    T = z.shape[0]
    SW = gain.shape[1]
    G = SW // BLOCK

    def body(zu_ref, zv_ref, gain_ref, ws_ref, bs_ref, a_ref):
        u = _gelu(zu_ref[...].astype(F32))
        vg = _gelu(zv_ref[...].astype(F32))
        r = lax.rsqrt(jnp.mean(vg * vg, axis=-1, keepdims=True) + EPS)
        vn = ((vg * r) * gain_ref[...]).astype(BF16)
        for g in range(G):
            sl = slice(g * BLOCK, (g + 1) * BLOCK)
            mixed = jnp.dot(ws_ref[g], vn[:, sl], preferred_element_type=F32) + bs_ref[:, g:g + 1]
            a_ref[:, sl] = (u[:, sl] * mixed).astype(BF16)

    return pl.pallas_call(
        body, name=name, grid=(T // BLOCK,),
        in_specs=[pl.BlockSpec((BLOCK, SW), lambda c: (c, 0)), pl.BlockSpec((BLOCK, SW), lambda c: (c, 1)),
                  pl.BlockSpec((1, SW), lambda c: (0, 0)), pl.BlockSpec((G, BLOCK, BLOCK), lambda c: (0, 0, 0)),
                  pl.BlockSpec((BLOCK, G), lambda c: (0, 0))],
        out_specs=pl.BlockSpec((BLOCK, SW), lambda c: (c, 0)),
        out_shape=jax.ShapeDtypeStruct((T, SW), BF16),
        compiler_params=_cparams("parallel"),
    )(z, z, gain, ws_b, bs_t)


def _sgu_bwd(z, gain, ws_b, bs_t, da, dz, *, name):
    T = z.shape[0]
    SW = gain.shape[1]
    G = SW // BLOCK

    def body(zu_ref, zv_ref, gain_ref, ws_ref, bs_ref, da_ref, dz_in_ref, dz_ref, dws_ref, dbs_ref, dgain_ref, dvn_ref):
        first = pl.program_id(0) == 0

        @pl.when(first)
        def _():
            dws_ref[...] = jnp.zeros_like(dws_ref)
            dbs_ref[...] = jnp.zeros_like(dbs_ref)
            dgain_ref[...] = jnp.zeros_like(dgain_ref)

        u, du = _gelu_and_grad(zu_ref[...].astype(F32))
        vg, dvg = _gelu_and_grad(zv_ref[...].astype(F32))
        r = lax.rsqrt(jnp.mean(vg * vg, axis=-1, keepdims=True) + EPS)
        xhat = vg * r
        gain_ = gain_ref[...]
        vn = (xhat * gain_).astype(BF16)
        da_ = da_ref[...]
        for g in range(G):
            sl = slice(g * BLOCK, (g + 1) * BLOCK)
            w = ws_ref[g]
            mixed = jnp.dot(w, vn[:, sl], preferred_element_type=F32) + bs_ref[:, g:g + 1]
            dmix = da_[:, sl] * u[:, sl]
            dz_ref[:, sl] = (da_[:, sl] * mixed * du[:, sl]).astype(BF16)
            dmb = dmix.astype(BF16)
            dws_ref[g] += lax.dot_general(dmb, vn[:, sl], (((1,), (1,)), ((), ())), preferred_element_type=F32)
            dbs_ref[:, g:g + 1] += jnp.sum(dmix, axis=-1, keepdims=True)
            dvn_ref[:, sl] = lax.dot_general(w, dmb, (((0,), (0,)), ((), ())), preferred_element_type=F32)
        dvn = dvn_ref[...]
        dgain_ref[...] += jnp.sum(dvn * xhat, axis=0, keepdims=True)
        dy = dvn * gain_
        dv_ = r * (dy - xhat * jnp.mean(dy * xhat, axis=-1, keepdims=True))
        dz_ref[:, SW:] = (dv_ * dvg).astype(BF16)

    row = pl.BlockSpec((BLOCK, SW), lambda c: (c, 0))
    return pl.pallas_call(
        body, name=name, grid=(T // BLOCK,),
        in_specs=[row, pl.BlockSpec((BLOCK, SW), lambda c: (c, 1)),
                  pl.BlockSpec((1, SW), lambda c: (0, 0)), pl.BlockSpec((G, BLOCK, BLOCK), lambda c: (0, 0, 0)),
                  pl.BlockSpec((BLOCK, G), lambda c: (0, 0)), row, _ANY],
        out_specs=[pl.BlockSpec((BLOCK, 2 * SW), lambda c: (c, 0)), pl.BlockSpec((G, BLOCK, BLOCK), lambda c: (0, 0, 0)),
                   pl.BlockSpec((BLOCK, G), lambda c: (0, 0)), pl.BlockSpec((1, SW), lambda c: (0, 0))],
        out_shape=[jax.ShapeDtypeStruct(dz.shape, dz.dtype),
                   jax.ShapeDtypeStruct((G, BLOCK, BLOCK), F32), jax.ShapeDtypeStruct((BLOCK, G), F32),
                   jax.ShapeDtypeStruct((1, SW), F32)],
        input_output_aliases={6: 0},
        scratch_shapes=[pltpu.VMEM((BLOCK, SW), F32)],
        compiler_params=_cparams("arbitrary"),
    )(z, z, gain, ws_b, bs_t, da, dz)


def _bias_table(rel_bias, bmap, *, name):
    H = rel_bias.shape[1]

    def body(rb_ref, bmap_ref, o_ref):
        bm_ = bmap_ref[...]
        for h in range(H):
            acc = jnp.zeros(bm_.shape, F32)
            for b in range(REL_BUCKETS):
                acc = jnp.where(bm_ == b, rb_ref[b, h], acc)
            o_ref[h] = acc

    return pl.pallas_call(
        body, name=name,
        in_specs=[pl.BlockSpec(memory_space=pltpu.SMEM), pl.BlockSpec(memory_space=pltpu.VMEM)],
        out_specs=pl.BlockSpec(memory_space=pltpu.VMEM),
        out_shape=jax.ShapeDtypeStruct((H, BLOCK, 3 * BLOCK), F32),
    )(rel_bias, bmap)


def _attn_probs(q_ref, kb, bias_ref, sink_ref, s_ref, n, T, group):
    H = s_ref.shape[0]
    for h in range(H):
        kv = h // group
        qh = q_ref[:, h * HEAD_DIM:(h + 1) * HEAD_DIM].astype(BF16)
        s_ref[h] = lax.dot_general(qh, kb[:, kv * HEAD_DIM:(kv + 1) * HEAD_DIM], (((1,), (1,)), ((), ())),
                                   preferred_element_type=F32)
    row = lax.broadcasted_iota(jnp.int32, (BLOCK, 3 * BLOCK), 0)
    col = lax.broadcasted_iota(jnp.int32, (BLOCK, 3 * BLOCK), 1)
    key_pos = n * BLOCK + col - BLOCK
    valid = (jnp.abs(col - BLOCK - row) <= BLOCK) & (key_pos >= 0) & (key_pos < T)
    s = s_ref[...] * (HEAD_DIM ** -0.5) + bias_ref[...]
    s = jnp.where(valid[None], s, NEG)
    sink = sink_ref[...]
    m = jnp.maximum(jnp.max(s, axis=-1, keepdims=True), sink)
    e = jnp.exp(s - m)
    es = jnp.exp(sink - m)
    inv = 1.0 / (jnp.sum(e, axis=-1, keepdims=True) + es)
    return e * inv, es * inv


def _attn_fwd(z, kpad, vpad, bias, sink, *, name):
    T = z.shape[0]
    H = bias.shape[0]
    AW = H * HEAD_DIM
    group = H // N_KV_HEADS

    def body(q_ref, k_ref, v_ref, bias_ref, sink_ref, o_ref, s_ref, p_ref):
        n = pl.program_id(0)
        start = pl.multiple_of(n * BLOCK, BLOCK)
        kb = k_ref[pl.ds(start, 3 * BLOCK), :]
        vb = v_ref[pl.ds(start, 3 * BLOCK), :]
        p, _ = _attn_probs(q_ref, kb, bias_ref, sink_ref, s_ref, n, T, group)
        p_ref[...] = p.astype(BF16)
        for h in range(H):
            kv = h // group
            o = jnp.dot(p_ref[h], vb[:, kv * HEAD_DIM:(kv + 1) * HEAD_DIM], preferred_element_type=F32)
            o_ref[:, h * HEAD_DIM:(h + 1) * HEAD_DIM] = o.astype(BF16)

    full_kv = pl.BlockSpec((T + 2 * BLOCK, KV_WIDTH), lambda n: (0, 0))
    return pl.pallas_call(
        body, name=name, grid=(T // BLOCK,),
        in_specs=[pl.BlockSpec((BLOCK, AW), lambda n: (n, 2)), full_kv, full_kv,
                  pl.BlockSpec((H, BLOCK, 3 * BLOCK), lambda n: (0, 0, 0)), pl.BlockSpec((H, 1, 1), lambda n: (0, 0, 0))],
        out_specs=pl.BlockSpec((BLOCK, AW), lambda n: (n, 0)),
        out_shape=jax.ShapeDtypeStruct((T, AW), BF16),
        scratch_shapes=[pltpu.VMEM((H, BLOCK, 3 * BLOCK), F32), pltpu.VMEM((H, BLOCK, 3 * BLOCK), BF16)],
        compiler_params=_cparams("parallel"),
    )(z, kpad, vpad, bias, sink)


def _attn_bwd(z, kpad, vpad, bias, sink, do, dz, *, name):
    T = z.shape[0]
    H = bias.shape[0]
    AW = H * HEAD_DIM
    group = H // N_KV_HEADS
    scale = HEAD_DIM ** -0.5

    def body(q_ref, k_ref, v_ref, bias_ref, sink_ref, do_ref, dz_in_ref, dq_ref, dk_ref, dv_ref, dbias_ref, dsink_ref,
             s_ref, dp_ref, p_ref, ds_ref):
        n = pl.program_id(0)

        @pl.when(n == 0)
        def _():
            dk_ref[...] = jnp.zeros_like(dk_ref)
            dv_ref[...] = jnp.zeros_like(dv_ref)
            dbias_ref[...] = jnp.zeros_like(dbias_ref)
            dsink_ref[...] = jnp.zeros_like(dsink_ref)

        start = pl.multiple_of(n * BLOCK, BLOCK)
        kb = k_ref[pl.ds(start, 3 * BLOCK), :]
        vb = v_ref[pl.ds(start, 3 * BLOCK), :]
        p, p_sink = _attn_probs(q_ref, kb, bias_ref, sink_ref, s_ref, n, T, group)
        s_ref[...] = p
        p_ref[...] = p.astype(BF16)
        for h in range(H):
            kv = h // group
            dp_ref[h] = lax.dot_general(do_ref[:, h * HEAD_DIM:(h + 1) * HEAD_DIM], vb[:, kv * HEAD_DIM:(kv + 1) * HEAD_DIM],
                                        (((1,), (1,)), ((), ())), preferred_element_type=F32)
        p = s_ref[...]
        dp = dp_ref[...]
        delta = jnp.sum(p * dp, axis=-1, keepdims=True)
        ds = p * (dp - delta)
        dbias_ref[...] += ds
        dsink_ref[...] += -(p_sink * delta)
        ds_ref[...] = ds.astype(BF16)
        for kv in range(N_KV_HEADS):
            ksl = slice(kv * HEAD_DIM, (kv + 1) * HEAD_DIM)
            dk_acc = jnp.zeros((3 * BLOCK, HEAD_DIM), F32)
            dv_acc = jnp.zeros((3 * BLOCK, HEAD_DIM), F32)
            for gi in range(group):
                h = kv * group + gi
                hsl = slice(h * HEAD_DIM, (h + 1) * HEAD_DIM)
                dsb = ds_ref[h]
                dq = jnp.dot(dsb, kb[:, ksl], preferred_element_type=F32) * scale
                dq_ref[:, hsl] = dq.astype(BF16)
                dk_acc = dk_acc + lax.dot_general(dsb, q_ref[:, hsl].astype(BF16), (((0,), (0,)), ((), ())),
                                                  preferred_element_type=F32)
                dv_acc = dv_acc + lax.dot_general(p_ref[h], do_ref[:, hsl], (((0,), (0,)), ((), ())),
                                                  preferred_element_type=F32)
            dk_ref[pl.ds(start, 3 * BLOCK), ksl] += dk_acc * scale
            dv_ref[pl.ds(start, 3 * BLOCK), ksl] += dv_acc

    full_kv = pl.BlockSpec((T + 2 * BLOCK, KV_WIDTH), lambda n: (0, 0))
    bias_spec = pl.BlockSpec((H, BLOCK, 3 * BLOCK), lambda n: (0, 0, 0))
    row = pl.BlockSpec((BLOCK, AW), lambda n: (n, 0))
    q_cols = pl.BlockSpec((BLOCK, AW), lambda n: (n, 2))
    band = (H, BLOCK, 3 * BLOCK)
    return pl.pallas_call(
        body, name=name, grid=(T // BLOCK,),
        in_specs=[q_cols, full_kv, full_kv, bias_spec, pl.BlockSpec((H, 1, 1), lambda n: (0, 0, 0)), row, _ANY],
        out_specs=[q_cols, full_kv, full_kv, bias_spec, pl.BlockSpec((H, BLOCK, 1), lambda n: (0, 0, 0))],
        out_shape=[jax.ShapeDtypeStruct(dz.shape, dz.dtype),
                   jax.ShapeDtypeStruct((T + 2 * BLOCK, KV_WIDTH), F32), jax.ShapeDtypeStruct((T + 2 * BLOCK, KV_WIDTH), F32),
                   jax.ShapeDtypeStruct(band, F32), jax.ShapeDtypeStruct((H, BLOCK, 1), F32)],
        input_output_aliases={6: 0},
        scratch_shapes=[pltpu.VMEM(band, F32), pltpu.VMEM(band, F32), pltpu.VMEM(band, BF16), pltpu.VMEM(band, BF16)],
        compiler_params=_cparams("arbitrary"),
    )(z, kpad, vpad, bias, sink, do, dz)


def _dkv_into(dkp, dvp, dz, *, name):
    T = dz.shape[0]
    D = (dz.shape[1] - 2 * KV_WIDTH) * 2 // 7
    col = (D + D // 2) // (2 * KV_WIDTH)
    assert col * 2 * KV_WIDTH == D + D // 2

    def body(dk_ref, dv_ref, dz_in_ref, o_ref):
        o_ref[:, :KV_WIDTH] = dk_ref[...].astype(BF16)
        o_ref[:, KV_WIDTH:] = dv_ref[...].astype(BF16)

    kv = pl.BlockSpec((BLOCK, KV_WIDTH), lambda n: (n + 1, 0))
    return pl.pallas_call(
        body, name=name, grid=(T // BLOCK,),
        in_specs=[kv, kv, _ANY], out_specs=pl.BlockSpec((BLOCK, 2 * KV_WIDTH), lambda n: (n, col)),
        out_shape=jax.ShapeDtypeStruct(dz.shape, dz.dtype), input_output_aliases={2: 0},
        compiler_params=_cparams("parallel"),
    )(dkp, dvp, dz)


def _kv_pad(z, *, name):
    T = z.shape[0]
    D = (z.shape[1] - 2 * KV_WIDTH) * 2 // 7
    kcol = (D + D // 2) // KV_WIDTH
    nb = T // BLOCK

    def body(k_ref, v_ref, ko_ref, vo_ref):
        b = pl.program_id(0)
        inside = (b >= 1) & (b <= nb)
        ko_ref[...] = jnp.where(inside, k_ref[...].astype(F32), 0.0).astype(BF16)
        vo_ref[...] = jnp.where(inside, v_ref[...].astype(F32), 0.0).astype(BF16)

    out = jax.ShapeDtypeStruct((T + 2 * BLOCK, KV_WIDTH), BF16)
    o_spec = pl.BlockSpec((BLOCK, KV_WIDTH), lambda b: (b, 0))
    return pl.pallas_call(
        body, name=name, grid=(nb + 2,),
        in_specs=[pl.BlockSpec((BLOCK, KV_WIDTH), lambda b: (jnp.clip(b - 1, 0, nb - 1), kcol)),
                  pl.BlockSpec((BLOCK, KV_WIDTH), lambda b: (jnp.clip(b - 1, 0, nb - 1), kcol + 1))],
        out_specs=[o_spec, o_spec], out_shape=[out, out],
        compiler_params=_cparams("parallel"),
    )(z, z)


def _attn_small_grads(dbias, dsink_rows, bmap, after, *, name):
    H = dbias.shape[0]

    def body(dbias_ref, dsink_ref, bmap_ref, drel_ref, ds_ref):
        bm_ = bmap_ref[...]
        for h in range(H):
            d = dbias_ref[h]
            for b in range(REL_BUCKETS):
                drel_ref[b, h] = jnp.sum(jnp.where(bm_ == b, d, 0.0))
            ds_ref[0, h] = jnp.sum(dsink_ref[h])

    vmem = pl.BlockSpec(memory_space=pltpu.VMEM)
    smem = pl.BlockSpec(memory_space=pltpu.SMEM)
    body, in_specs, args = _ordered_after(body, 3, [vmem, vmem, vmem], (dbias, dsink_rows, bmap), after)
    return pl.pallas_call(
        body, name=name, in_specs=in_specs, out_specs=[smem, smem],
        out_shape=[jax.ShapeDtypeStruct((REL_BUCKETS, H), F32), jax.ShapeDtypeStruct((1, H), F32)],
    )(*args)


def _local_step(x, target, weight, emit, flush, norm_mix, v_gain, w_s, b_s, sink, rel_bias, norm_ffn, norm_final, early=()):
    T, D = x.shape
    ws_b = w_s.astype(BF16)
    bs_t = b_s.T
    bmap = jnp.asarray(_bucket_map())
    sink = sink.reshape(-1, 1, 1)

    bias = _bias_table(rel_bias, bmap, name="bias_table")
    h = _rms_fwd(x, norm_mix, name="rms_mix", after=[bias, *early])
    w_in = weight("w_in", h)
    z = _mm(h, w_in, tb=True, out_dtype=BF16, name="mm_z", bm=2048, bn=768)
    a = _sgu_fwd(z, v_gain, ws_b, bs_t, name="sgu_fwd")
    w_a = weight("w_a_out", a)
    ya = _mm_w8(a, w_a, name="mm_ya", bm=2048, out_dtype=BF16)
    kpad, vpad = _kv_pad(z, name="kv_pad")
    o = _attn_fwd(z, kpad, vpad, bias, sink, name="attn_fwd")
    w_b = weight("w_b_out", o)
    yb = _mm_w8(o, w_b, name="mm_yb", bm=2048, out_dtype=BF16)
    m = _merge_fwd(z, ya, yb, name="merge_fwd")
    w_o = weight("w_o", m)
    x1, h2 = _mm_resid_rms(m, w_o, x, norm_ffn, name="mm_x1_rms")
    w_gate = weight("w_gate", h2)
    gate = _mm(h2, w_gate, tb=True, name="mm_gate", bm=2048, bn=512)
    w_up = weight("w_up", gate)
    gate, up, act = _swiglu_mm_fwd(h2, w_up, gate, name="mm_up_swiglu")
    w_down = weight("w_down", act)
    x2 = _mm(act, w_down, name="mm_x2", add=x1, bm=1024, bn=512)
    loss, dx2, dx2b, g_norm_final = _loss_head(x2, norm_final, target, name="loss_head")

    g_w_down = _mm(act, dx2b, ta=True, out_dtype=BF16, name="mm_gwdown", bm=512, bn=2048)
    tok = emit(("w_down",), (g_w_down,))
    dgate, dup = _swiglu_mm_bwd(dx2b, w_down, gate, up, name="mm_dact_swiglu", after=tok)
    tok = flush(dgate)
    g_w_gate = _mm(dgate, h2, ta=True, out_dtype=BF16, name="mm_gwgate", bm=512, bn=2048, after=tok)
    g_w_up = _mm(dup, h2, ta=True, out_dtype=BF16, name="mm_gwup", bm=512, bn=2048)
    tok = emit(("w_gate", "w_up"), (g_w_gate, g_w_up))
    dh2 = _mm_sum2(dgate, w_gate, dup, w_up, name="mm_dh2", after=tok)
    tok = flush(dh2)
    dx1, dx1b, g_norm_ffn = _rms_bwd(x1, norm_ffn, dh2, dx2, name="rms_ffn_bwd", want_bf16=True, after=tok)

    g_w_o = _mm(m, dx1b, ta=True, out_dtype=BF16, name="mm_gwo", bm=2048, bn=512)
    tok = emit(("w_o",), (g_w_o,))
    dm = _mm(dx1b, w_o, tb=True, name="mm_dm", bm=2048, bn=512, after=tok)
    tok = flush(dm)
    dy, dz = _merge_bwd(z, ya, yb, dm, name="merge_bwd", after=tok)
    g_w_a = _mm_gw8(a, dy, w_a.shape[2], name="mm_gwa", lead=0)
    g_w_b = _mm_gw8(o, dy, w_b.shape[2], name="mm_gwb", lead=1)
    tok = emit(("w_a_out", "w_b_out"), (g_w_a, g_w_b))
    da = _mm_w8t(dy, w_a, name="mm_da", bm=2048, bn=512, after=tok, lead=0)
    tok = flush(da)
    do = _mm_w8t(dy, w_b, out_dtype=BF16, name="mm_do", bm=2048, bn=512, after=tok, lead=1)
    dz, g_w_s, g_b_s_t, g_v_gain = _sgu_bwd(z, v_gain, ws_b, bs_t, da, dz, name="sgu_bwd")
    dz, dkp, dvp, dbias, dsink_rows = _attn_bwd(z, kpad, vpad, bias, sink, do, dz, name="attn_bwd")
    dz = _dkv_into(dkp, dvp, dz, name="dkv_into_dz")
    g_w_in = _mm(dz, h, ta=True, out_dtype=BF16, name="mm_gwin", bm=768, bn=2048)
    tok = emit(("w_in",), (g_w_in,))
    half = dict(bm=T // 2, bn=256)
    dh = _mm(dz, w_in, name="mm_dh_top", row_blocks=(0, 1), after=tok, **half)
    tok = flush(dh)
    dh = _mm(dz, w_in, name="mm_dh_bottom", row_blocks=(1, 1), into=dh, after=tok, **half)
    g_rel_bias, g_sink = _attn_small_grads(dbias, dsink_rows, bmap, dh, name="attn_small_grads")
    grad_x, g_norm_mix = _rms_bwd(x, norm_mix, dh, dx1, name="rms_mix_bwd", want_bf16=False)

    small = dict(norm_mix=g_norm_mix, sgu_v_gain=g_v_gain, sgu_w_s=g_w_s, sgu_b_s=g_b_s_t.T, attn_sink=g_sink,
                 rel_bias=g_rel_bias, norm_ffn=g_norm_ffn, norm_final=g_norm_final)
    return loss, grad_x, small


def _position():
    return lax.axis_index("x"), lax.axis_index("y"), lax.axis_index("c")


def _other_chips(x, y):
    return [(1 - x, y), (x, 1 - y), (1 - x, 1 - y)]


def _slot(px, py, pc):
    return 4 * px + 2 * py + pc


_HBM = pl.BlockSpec(memory_space=pltpu.HBM)
_SEM = pl.BlockSpec(memory_space=pltpu.SEMAPHORE)
_DATAFLOW = pltpu.SideEffectType.DATAFLOW_SIDE_EFFECTING


def _in_hbm(a):
    return pltpu.with_memory_space_constraint(a, pltpu.HBM)


def _own_slot(shard, pos, *, name, after=None):
    R, C = shard.shape
    tr = _div(R, 256, 16)

    def body(pos_ref, w_ref, o_ref):
        o_ref[...] = w_ref[...].astype(BF16)

    body, in_specs, args = _ordered_after(body, 2, [pl.BlockSpec((tr, C), lambda i, pos_ref: (i, 0))], (pos, shard), after)
    grid_spec = pltpu.PrefetchScalarGridSpec(
        num_scalar_prefetch=1, grid=(R // tr,), in_specs=in_specs,
        out_specs=pl.BlockSpec((None, tr, C), lambda i, pos_ref: (pos_ref[0], i, 0)))
    return pl.pallas_call(
        body, name=name, grid_spec=grid_spec,
        out_shape=jax.ShapeDtypeStruct((N_DEV, R, C), BF16),
        compiler_params=_cparams("parallel"),
    )(*args)


def _ag_copies(w, land_ref, send_sems, recv_sems):
    x, y, c = _position()
    mine = land_ref.at[_slot(x, y, c)]
    targets = [(px, py, c) for px, py in _other_chips(x, y)] + [(x, y, 1 - c)]
    return [pltpu.make_async_remote_copy(src_ref=mine, dst_ref=mine, send_sem=send_sems.at[4 * w + k],
                                         recv_sem=recv_sems.at[4 * w + k], device_id=to, device_id_type=MESH)
            for k, to in enumerate(targets)]


def _ag_start(buffers, groups, *, name, after=None):
    lands = [buffers[i] for g in groups for i in g]
    n, ng = len(lands), len(groups)
    sizes = [len(g) for g in groups]

    def body(*refs):
        land_refs = refs[:n]
        sems = refs[n:n + 2 * ng]
        token = refs[-1]
        i = 0
        for g in range(ng):
            for w in range(sizes[g]):
                for cp in _ag_copies(w, land_refs[i], sems[2 * g], sems[2 * g + 1]):
                    cp.start()
                i += 1
        token[...] = jnp.zeros_like(token)

    sem_shapes = [pltpu.SemaphoreType.DMA((4 * k,)) for k in sizes for _ in range(2)]
    body, in_specs, args = _ordered_after(body, n, [_HBM] * n, tuple(_in_hbm(a) for a in lands), after)
    outs = pl.pallas_call(
        body, name=name,
        in_specs=in_specs,
        out_specs=tuple([_SEM] * (2 * ng) + [_HBM] * n + [pl.BlockSpec(memory_space=pltpu.VMEM)]),
        out_shape=tuple(sem_shapes + [pltpu.HBM(a.shape, a.dtype) for a in lands] + [jax.ShapeDtypeStruct((8, LANES), F32)]),
        input_output_aliases={i: 2 * ng + i for i in range(n)},
        compiler_params=pltpu.CompilerParams(has_side_effects=_DATAFLOW),
    )(*args)
    sems, thru = outs[:2 * ng], outs[2 * ng:2 * ng + n]
    result, i = [], 0
    for g in range(ng):
        k = sizes[g]
        result.append((sems[2 * g], sems[2 * g + 1], list(thru[i:i + k])))
        i += k
    return result, outs[-1]


def _ag_wait(send_sems, recv_sems, lands, after, *, name):
    n = len(lands)

    def body(*refs):
        land_refs = refs[:n]
        send_ref, recv_ref = refs[n], refs[n + 1]
        token = refs[-1]
        for w in range(n):
            for cp in _ag_copies(w, land_refs[w], send_ref, recv_ref):
                cp.wait_send()
                cp.wait_recv()
        token[...] = jnp.zeros_like(token)

    outs = pl.pallas_call(
        body, name=name,
        in_specs=[_HBM] * n + [_SEM, _SEM, _ANY],
        out_specs=tuple([_HBM] * n + [pl.BlockSpec(memory_space=pltpu.VMEM)]),
        out_shape=tuple([pltpu.HBM(a.shape, a.dtype) for a in lands] + [jax.ShapeDtypeStruct((8, LANES), F32)]),
        input_output_aliases={i: i for i in range(n)},
        compiler_params=pltpu.CompilerParams(has_side_effects=_DATAFLOW),
    )(*lands, send_sems, recv_sems, after)
    return list(outs[:n]), outs[n]


def _ag_forward(lands, *, name, after=None):
    n = len(lands)

    def body(*refs):
        in_refs, out_refs = refs[:n], refs[n:2 * n]
        send_sems, recv_sems = refs[2 * n:]
        x, y, c = _position()
        copies = []
        for w in range(n):
            for k, (px, py) in enumerate(_other_chips(x, y)):
                cp = pltpu.make_async_remote_copy(
                    src_ref=in_refs[w].at[_slot(px, py, c)], dst_ref=out_refs[w].at[_slot(px, py, c)],
                    send_sem=send_sems.at[3 * w + k], recv_sem=recv_sems.at[3 * w + k],
                    device_id=(x, y, 1 - c), device_id_type=MESH)
                cp.start()
                copies.append(cp)
        for cp in copies:
            cp.wait()

    body, in_specs, args = _ordered_after(body, n, [_ANY] * n, tuple(lands), after)
    return pl.pallas_call(
        body, name=name,
        in_specs=in_specs, out_specs=[_ANY] * n,
        out_shape=[jax.ShapeDtypeStruct(a.shape, a.dtype) for a in lands],
        input_output_aliases={i: i for i in range(n)},
        scratch_shapes=[pltpu.SemaphoreType.DMA((3 * n,)), pltpu.SemaphoreType.DMA((3 * n,))],
    )(*args)


def _sibling_copies(w, g8_ref, land_ref, send_sems, recv_sems):
    x, y, c = _position()
    return [pltpu.make_async_remote_copy(src_ref=g8_ref.at[2 * p + (1 - c)], dst_ref=land_ref.at[p],
                                         send_sem=send_sems.at[4 * w + p], recv_sem=recv_sems.at[4 * w + p],
                                         device_id=(x, y, 1 - c), device_id_type=MESH)
            for p in range(4)]


def _chip_copies(w, sums_ref, land_ref, send_sems, recv_sems):
    x, y, c = _position()
    return [pltpu.make_async_remote_copy(src_ref=sums_ref.at[2 * px + py], dst_ref=land_ref.at[k],
                                         send_sem=send_sems.at[3 * w + k], recv_sem=recv_sems.at[3 * w + k],
                                         device_id=(px, py, c), device_id_type=MESH)
            for k, (px, py) in enumerate(_other_chips(x, y))]


def _copies_start(copies, per_weight, srcs, *, name):
    n = len(srcs)
    lands = [lax.empty((per_weight,) + s.shape[1:], s.dtype) for s in srcs]

    def body(*refs):
        src_refs, land_refs = refs[:n], refs[n:2 * n]
        send_sems, recv_sems = refs[2 * n], refs[2 * n + 1]
        token = refs[-1]
        for w in range(n):
            for cp in copies(w, src_refs[w], land_refs[w], send_sems, recv_sems):
                cp.start()
        token[...] = jnp.zeros_like(token)

    outs = pl.pallas_call(
        body, name=name,
        in_specs=[_HBM] * (2 * n),
        out_specs=tuple([_SEM, _SEM] + [_HBM] * (2 * n) + [pl.BlockSpec(memory_space=pltpu.VMEM)]),
        out_shape=tuple([pltpu.SemaphoreType.DMA((per_weight * n,)), pltpu.SemaphoreType.DMA((per_weight * n,))]
                        + [pltpu.HBM(a.shape, a.dtype) for a in srcs + lands] + [jax.ShapeDtypeStruct((8, LANES), F32)]),
        input_output_aliases={i: 2 + i for i in range(2 * n)},
        compiler_params=pltpu.CompilerParams(has_side_effects=_DATAFLOW),
    )(*[_in_hbm(a) for a in srcs + lands])
    return outs[0], outs[1], list(outs[2:2 + n]), list(outs[2 + n:2 + 2 * n]), outs[-1]


def _copies_wait(copies, send_sems, recv_sems, srcs, lands, after, *, name):
    n = len(srcs)

    def body(*refs):
        src_refs, land_refs = refs[:n], refs[n:2 * n]
        send_ref, recv_ref = refs[2 * n], refs[2 * n + 1]
        for w in range(n):
            for cp in copies(w, src_refs[w], land_refs[w], send_ref, recv_ref):
                cp.wait_send()
                cp.wait_recv()

    outs = pl.pallas_call(
        body, name=name,
        in_specs=[_HBM] * (2 * n) + [_SEM, _SEM, _ANY],
        out_specs=tuple([_HBM] * (2 * n)),
        out_shape=tuple(pltpu.HBM(a.shape, a.dtype) for a in srcs + lands),
        input_output_aliases={i: i for i in range(2 * n)},
        compiler_params=pltpu.CompilerParams(has_side_effects=_DATAFLOW),
    )(*srcs, *lands, send_sems, recv_sems, after)
    return list(outs[:n]), list(outs[n:])


def _chip_sums(g8, from_sibling, pos, *, name):
    _, R, C = g8.shape
    tr = _div(R, 512, 16)

    def body(pos_ref, g_ref, s_ref, o_ref):
        o_ref[...] = (g_ref[...].astype(F32) + s_ref[...].astype(F32)).astype(BF16)

    def chip(k, pos_ref):
        return jnp.where(k >= pos_ref[1], k + 1, k)

    grid_spec = pltpu.PrefetchScalarGridSpec(
        num_scalar_prefetch=1, grid=(3, R // tr),
        in_specs=[pl.BlockSpec((None, tr, C), lambda k, i, pos_ref: (2 * chip(k, pos_ref) + pos_ref[2], i, 0)),
                  pl.BlockSpec((None, tr, C), lambda k, i, pos_ref: (chip(k, pos_ref), i, 0))],
        out_specs=pl.BlockSpec((None, tr, C), lambda k, i, pos_ref: (chip(k, pos_ref), i, 0)))
    return pl.pallas_call(
        body, name=name, grid_spec=grid_spec,
        out_shape=jax.ShapeDtypeStruct((4, R, C), BF16),
        compiler_params=_cparams("parallel", "parallel"),
    )(pos, g8, from_sibling)


def _small_all_reduce(packed, after, *, name):
    R, L = packed.shape

    def body(x_ref, sum_ref, gath_ref, send_sems, recv_sems, local_sem):
        x, y, c = _position()
        me, sibling = (x, y, c), (x, y, 1 - c)
        chips = _other_chips(x, y)

        def rows(px, py, pc):
            return gath_ref.at[pl.ds(_slot(px, py, pc) * R, R), :]

        def copy(k, block, to, src=None):
            return pltpu.make_async_remote_copy(
                src_ref=rows(*block) if src is None else src, dst_ref=rows(*block),
                send_sem=send_sems.at[k], recv_sem=recv_sems.at[k], device_id=to, device_id_type=MESH)

        mine = pltpu.make_async_copy(x_ref, rows(*me), local_sem)
        mine.start()
        first = [copy(0, me, sibling, src=x_ref)]
        first += [copy(1 + j, me, (*chip, c), src=x_ref) for j, chip in enumerate(chips)]
        for cp in first:
            cp.start()
        passed = [copy(4 + j, (*chip, c), sibling) for j, chip in enumerate(chips)]
        for j, chip in enumerate(chips):
            copy(1 + j, (*chip, c), me).wait_recv()
            passed[j].start()
        copy(0, sibling, me).wait_recv()
        for j, chip in enumerate(chips):
            copy(4 + j, (*chip, 1 - c), me).wait_recv()
        for cp in first + passed:
            cp.wait_send()
        mine.wait()
        acc = gath_ref[0:R, :]
        for d in range(1, N_DEV):
            acc = acc + gath_ref[d * R:(d + 1) * R, :]
        sum_ref[...] = acc

    vmem = pl.BlockSpec(memory_space=pltpu.VMEM)
    body, in_specs, args = _ordered_after(body, 1, [vmem], (packed,), after)
    return pl.pallas_call(
        body, name=name, in_specs=in_specs, out_specs=vmem,
        out_shape=jax.ShapeDtypeStruct((R, L), F32),
        scratch_shapes=[pltpu.VMEM((N_DEV * R, L), F32), pltpu.SemaphoreType.DMA((7,)), pltpu.SemaphoreType.DMA((7,)),
                        pltpu.SemaphoreType.DMA],
        compiler_params=pltpu.CompilerParams(vmem_limit_bytes=VMEM_LIMIT),
    )(*args)


def _adamw_math(w, g, m, v):
    m = ADAM_B1 * m + (1.0 - ADAM_B1) * g
    v = ADAM_B2 * v + (1.0 - ADAM_B2) * (g * g)
    m_hat = m / (1.0 - ADAM_B1 ** ADAM_STEP)
    v_hat = v / (1.0 - ADAM_B2 ** ADAM_STEP)
    delta = -ADAM_LR * (m_hat / (jnp.sqrt(v_hat) + ADAM_EPS) + ADAM_WD * w)
    return delta, m, v


def _adamw_shard(w, m, v, g8, from_sibling, from_chips, pos, *, name):
    R, C = w.shape
    tr = _div(R, 256, 16)

    def body(pos_ref, w_ref, m_ref, v_ref, g_ref, s_ref, r_ref, go_ref, d_ref, mo_ref, vo_ref):
        g = g_ref[...].astype(F32) + s_ref[...].astype(F32)
        for k in range(3):
            g = g + r_ref[k].astype(F32)
        delta, m_, v_ = _adamw_math(w_ref[...], g, m_ref[...], v_ref[...])
        go_ref[...] = g
        d_ref[...] = delta
        mo_ref[...] = m_
        vo_ref[...] = v_

    blk = pl.BlockSpec((tr, C), lambda i, pos_ref: (i, 0))
    grid_spec = pltpu.PrefetchScalarGridSpec(
        num_scalar_prefetch=1, grid=(R // tr,),
        in_specs=[blk, blk, blk,
                  pl.BlockSpec((None, tr, C), lambda i, pos_ref: (pos_ref[0], i, 0)),
                  pl.BlockSpec((None, tr, C), lambda i, pos_ref: (pos_ref[1], i, 0)),
                  pl.BlockSpec((3, tr, C), lambda i, pos_ref: (0, i, 0))],
        out_specs=[blk] * 4)
    out = jax.ShapeDtypeStruct((R, C), F32)
    return pl.pallas_call(
        body, name=name, grid_spec=grid_spec, out_shape=[out] * 4,
        compiler_params=_cparams("parallel"),
    )(pos, w, m, v, g8, from_sibling, from_chips)


def _adamw_small(w, g, m, v, *, name):
    R, L = w.shape

    def body(w_ref, g_ref, m_ref, v_ref, d_ref, mo_ref, vo_ref):
        delta, m_, v_ = _adamw_math(w_ref[...], g_ref[...], m_ref[...], v_ref[...])
        d_ref[...] = delta
        mo_ref[...] = m_
        vo_ref[...] = v_

    vmem = pl.BlockSpec(memory_space=pltpu.VMEM)
    out = jax.ShapeDtypeStruct((R, L), F32)
    return pl.pallas_call(body, name=name, in_specs=[vmem] * 4, out_specs=[vmem] * 3, out_shape=[out] * 3)(w, g, m, v)


_TILE = 8 * LANES


def _pack(pieces):
    rows = []
    for p in pieces:
        flat = p.reshape(-1).astype(F32)
        padded = -(-flat.shape[0] // _TILE) * _TILE
        rows.append(jnp.pad(flat, (0, padded - flat.shape[0])).reshape(-1, LANES))
    return jnp.concatenate(rows, axis=0)


def _unpack(packed, like):
    out, r = [], 0
    for p in like:
        size = int(np.prod(p.shape)) if p.shape else 1
        nrows = -(-size // _TILE) * 8
        out.append(packed[r:r + nrows].reshape(-1)[:size].reshape(p.shape))
        r += nrows
    return out


_BIG = ("w_in", "w_a_out", "w_b_out", "w_o", "w_gate", "w_up", "w_down")
_TRANSPOSED = ("w_in", "w_gate", "w_up")
_COL_SHARDED = ("w_a_out", "w_b_out")
_GATHER_GROUPS = (("w_in",), ("w_a_out", "w_b_out", "w_o"), ("w_gate",), ("w_up",), ("w_down",))
_START_AFTER_WAIT = {0: (1, 2), 1: (3,), 2: (4,)}
_SMALL = ("norm_mix", "sgu_v_gain", "sgu_w_s", "sgu_b_s", "attn_sink", "rel_bias", "norm_ffn", "norm_final")
_ORDER = ("w_in", "norm_mix", "sgu_v_gain", "sgu_w_s", "sgu_b_s", "w_a_out", "attn_sink", "rel_bias", "w_b_out", "w_o",
          "norm_ffn", "w_gate", "w_up", "w_down", "norm_final")


def _shard(name, a):
    return jnp.swapaxes(a, 1, 2)[0] if name in _TRANSPOSED else a[0]


def _unshard(name, a):
    return jnp.swapaxes(a[None], 1, 2) if name in _TRANSPOSED else a[None]


def _whole(name, gathered):
    _, r, c = gathered.shape
    return gathered if name in _COL_SHARDED else gathered.reshape(N_DEV * r, c)


def _blocks(name, grad):
    if name in _COL_SHARDED:
        return grad
    r, c = grad.shape
    return grad.reshape(N_DEV, r // N_DEV, c)


def kernel(x, w_in, norm_mix, sgu_v_gain, sgu_w_s, sgu_b_s, w_a_out, attn_sink, rel_bias, w_b_out, w_o, norm_ffn, w_gate, w_up, w_down, norm_final, loss_target, m_w_in, m_norm_mix, m_sgu_v_gain, m_sgu_w_s, m_sgu_b_s, m_w_a_out, m_attn_sink, m_rel_bias, m_w_b_out, m_w_o, m_norm_ffn, m_w_gate, m_w_up, m_w_down, m_norm_final, v_w_in, v_norm_mix, v_sgu_v_gain, v_sgu_w_s, v_sgu_b_s, v_w_a_out, v_attn_sink, v_rel_bias, v_w_b_out, v_w_o, v_norm_ffn, v_w_gate, v_w_up, v_w_down, v_norm_final):
    w = dict(w_in=w_in, norm_mix=norm_mix, sgu_v_gain=sgu_v_gain, sgu_w_s=sgu_w_s, sgu_b_s=sgu_b_s, w_a_out=w_a_out,
             attn_sink=attn_sink, rel_bias=rel_bias, w_b_out=w_b_out, w_o=w_o, norm_ffn=norm_ffn, w_gate=w_gate,
             w_up=w_up, w_down=w_down, norm_final=norm_final)
    m = dict(w_in=m_w_in, norm_mix=m_norm_mix, sgu_v_gain=m_sgu_v_gain, sgu_w_s=m_sgu_w_s, sgu_b_s=m_sgu_b_s,
             w_a_out=m_w_a_out, attn_sink=m_attn_sink, rel_bias=m_rel_bias, w_b_out=m_w_b_out, w_o=m_w_o,
             norm_ffn=m_norm_ffn, w_gate=m_w_gate, w_up=m_w_up, w_down=m_w_down, norm_final=m_norm_final)
    v = dict(w_in=v_w_in, norm_mix=v_norm_mix, sgu_v_gain=v_sgu_v_gain, sgu_w_s=v_sgu_w_s, sgu_b_s=v_sgu_b_s,
             w_a_out=v_w_a_out, attn_sink=v_attn_sink, rel_bias=v_rel_bias, w_b_out=v_w_b_out, w_o=v_w_o,
             norm_ffn=v_norm_ffn, w_gate=v_w_gate, w_up=v_w_up, w_down=v_w_down, norm_final=v_norm_final)
    xc, yc, cc = _position()
    pos = jnp.stack([_slot(xc, yc, cc), 2 * xc + yc, cc]).astype(jnp.int32)

    in_flight, full, slots = {}, {}, {}

    def start_gather(groups, after):
        names = [n for gi in groups for n in _GATHER_GROUPS[gi]]
        flights, token = _ag_start([slots[n] for n in names], [[names.index(n) for n in _GATHER_GROUPS[gi]] for gi in groups],
                                   name="ag_start_%d" % groups[0], after=after)
        in_flight.update(zip(groups, flights))
        return token

    def weight(name, after):
        if name not in full:
            gi = next(i for i, grp in enumerate(_GATHER_GROUPS) if name in grp)
            send_sems, recv_sems, lands = in_flight[gi]
            lands, token = _ag_wait(send_sems, recv_sems, lands, after, name="ag_wait_%d" % gi)
            started = start_gather(_START_AFTER_WAIT[gi], token) if gi in _START_AFTER_WAIT else None
            gathered = _ag_forward(lands, name="ag_forward_%d" % gi, after=started)
            full.update({n: _whole(n, g) for n, g in zip(_GATHER_GROUPS[gi], gathered)})
        return full[name]

    for n in _GATHER_GROUPS[0]:
        slots[n] = _own_slot(_shard(n, w[n]), pos, name="own_slot_" + n)
    first_started = start_gather((0,), None)
    for grp in _GATHER_GROUPS[1:]:
        for n in grp:
            slots[n] = _own_slot(_shard(n, w[n]), pos, name="own_slot_" + n, after=first_started)

    to_sibling, reducing = [], {}

    def emit(names, grads):
        g8 = [_blocks(n, g) for n, g in zip(names, grads)]
        send_sems, recv_sems, g8, lands, token = _copies_start(_sibling_copies, 4, g8, name="rs_sibling_start_" + names[0])
        to_sibling.append((names, send_sems, recv_sems, g8, lands))
        return token

    def flush(after):
        names, send_sems, recv_sems, g8, lands = to_sibling.pop()
        g8, from_sibling = _copies_wait(_sibling_copies, send_sems, recv_sems, g8, lands, after,
                                        name="rs_sibling_wait_" + names[0])
        sums4 = [_chip_sums(g, s, pos, name="chip_sums_" + n) for n, g, s in zip(names, g8, from_sibling)]
        send_sems, recv_sems, sums4, lands, token = _copies_start(_chip_copies, 3, sums4, name="rs_chips_start_" + names[0])
        reducing[names] = (g8, from_sibling, send_sems, recv_sems, sums4, lands)
        return token

    loss, grad_x, small_grads_local = _local_step(
        x[0], loss_target[0], weight, emit, flush, norm_mix, sgu_v_gain, sgu_w_s[0], sgu_b_s[0], attn_sink, rel_bias,
        norm_ffn, norm_final[None], early=[slots[n] for grp in _GATHER_GROUPS[1:] for n in grp])

    out_g, out_d, out_m, out_v = {}, {}, {}, {}
    small_like = [w[n] for n in _SMALL]
    small_w = _pack(small_like)
    packed = _pack([small_grads_local[n] for n in _SMALL] + [loss[0, 0]])
    after = grad_x
    for gi, (names, (g8, from_sibling, send_sems, recv_sems, sums4, lands)) in enumerate(reducing.items()):
        if gi == len(reducing) - 1:
            summed = _small_all_reduce(packed, after, name="small_all_reduce")
            after = summed
        _, from_chips = _copies_wait(_chip_copies, send_sems, recv_sems, sums4, lands, after,
                                     name="rs_chips_wait_" + names[0])
        for i, n in enumerate(names):
            g, d, m_, v_ = _adamw_shard(_shard(n, w[n]), _shard(n, m[n]), _shard(n, v[n]), g8[i], from_sibling[i],
                                        from_chips[i], pos, name="adamw_" + n)
            out_g[n], out_d[n], out_m[n], out_v[n] = (_unshard(n, o) for o in (g, d, m_, v_))
            after = d
    *small_grads, loss_sum = _unpack(summed, small_like + [jax.ShapeDtypeStruct((), F32)])
    d_s, m_s, v_s = _adamw_small(small_w, summed[:small_w.shape[0]], _pack([m[n] for n in _SMALL]),
                                 _pack([v[n] for n in _SMALL]), name="adamw_small")
    for n, g, d, m_, v_ in zip(_SMALL, small_grads, _unpack(d_s, small_like), _unpack(m_s, small_like), _unpack(v_s, small_like)):
        out_g[n], out_d[n], out_m[n], out_v[n] = g, d, m_, v_

    return (loss_sum, grad_x[None], *[out_g[n] for n in _ORDER], *[out_d[n] for n in _ORDER],
            *[out_m[n] for n in _ORDER], *[out_v[n] for n in _ORDER])
```

```python
import functools
import math

import numpy as np
import jax
import jax.numpy as jnp
from jax import lax
from jax.experimental import pallas as pl
from jax.experimental.pallas import tpu as pltpu

F32 = jnp.float32
BF16 = jnp.bfloat16

EPS = 1e-6
NEG = -1e30
HEAD_DIM = 128
BLOCK = 128
N_KV_HEADS = 2
KV_WIDTH = N_KV_HEADS * HEAD_DIM
REL_BUCKETS = 32
REL_MAX_DIST = 128

ADAM_LR = 0.001
ADAM_B1 = 0.9
ADAM_B2 = 0.999
ADAM_EPS = 1e-08
ADAM_WD = 0.01
ADAM_STEP = 10

N_DEV = 8
LANES = 128
VMEM_LIMIT = 56 * 1024 * 1024
MESH = pl.DeviceIdType.MESH


def _cparams(*sem):
    return pltpu.CompilerParams(dimension_semantics=sem, vmem_limit_bytes=VMEM_LIMIT)


def _div(n, target, mult=LANES):
    best = None
    for d in range(mult, min(n, target) + 1, mult):
        if n % d == 0:
            best = d
    assert best is not None, (n, target, mult)
    return best


_ANY = pl.BlockSpec(memory_space=pl.ANY)


def _ordered_after(body, n_inputs, in_specs, args, after):
    if after is None:
        return body, in_specs, args
    extra = tuple(after) if isinstance(after, (tuple, list)) else (after,)

    def wrapped(*refs):
        return body(*refs[:n_inputs], *refs[n_inputs + len(extra):])

    return wrapped, list(in_specs) + [_ANY] * len(extra), tuple(args) + extra


def _bucket_map():
    nb = REL_BUCKETS // 2
    qi = np.arange(BLOCK)[:, None]
    kj = np.arange(3 * BLOCK)[None, :]
    rel = kj - BLOCK - qi
    ret = np.where(rel > 0, nb, 0)
    n = np.abs(rel)
    max_exact = nb // 2
    nf = np.maximum(n, 1).astype(np.float32)
    large = max_exact + (np.log(nf / np.float32(max_exact)) / np.float32(math.log(REL_MAX_DIST / max_exact))
                         * np.float32(nb - max_exact)).astype(np.int32)
    large = np.minimum(large, nb - 1)
    return (ret + np.where(n < max_exact, n, large)).astype(np.int32)


_GELU_C = math.sqrt(2.0 / math.pi)
_GELU_A = 0.044715


def _gelu(x):
    t = jnp.tanh(_GELU_C * (x + _GELU_A * (x * x * x)))
    return 0.5 * x * (1.0 + t)


def _gelu_and_grad(x):
    x2 = x * x
    t = jnp.tanh(_GELU_C * (x + _GELU_A * (x2 * x)))
    g = 0.5 * x * (1.0 + t)
    dg = 0.5 * (1.0 + t) + 0.5 * x * (1.0 - t * t) * (_GELU_C * (1.0 + 3.0 * _GELU_A * x2))
    return g, dg


def _sigmoid(x):
    return 1.0 / (1.0 + jnp.exp(-x))


def _mm(a, b, *, name, ta=False, tb=False, add=None, out_dtype=F32, bm=1024, bn=1024, bk=None, after=None,
        row_blocks=None, into=None, k_blocks=None):
    if ta:
        K, M = a.shape
    else:
        M, K = a.shape
    N = b.shape[0] if tb else b.shape[1]
    assert (b.shape[1] if tb else b.shape[0]) == K
    bm = _div(M, bm)
    bn = _div(N, bn)
    bk = K if bk is None else _div(K, bk)
    k0, nk = (0, K // bk) if k_blocks is None else k_blocks
    i0, ni = (0, M // bm) if row_blocks is None else row_blocks
    a_spec = (pl.BlockSpec((bk, bm), lambda i, j, k: (k + k0, i + i0)) if ta
              else pl.BlockSpec((bm, bk), lambda i, j, k: (i + i0, k + k0)))
    b_spec = (pl.BlockSpec((bn, bk), lambda i, j, k: (j, k + k0)) if tb
              else pl.BlockSpec((bk, bn), lambda i, j, k: (k + k0, j)))
    o_spec = pl.BlockSpec((bm, bn), lambda i, j, k: (i + i0, j))
    dims = (((0 if ta else 1,), (1 if tb else 0,)), ((), ()))
    has_add = add is not None

    def body(*refs):
        if has_add:
            a_ref, b_ref, add_ref, o_ref, *scratch = refs
        else:
            a_ref, b_ref, o_ref, *scratch = refs
            add_ref = None
        p = lax.dot_general(a_ref[...].astype(BF16), b_ref[...].astype(BF16), dims, preferred_element_type=F32)
        if nk == 1:
            if has_add:
                p = p + add_ref[...]
            o_ref[...] = p.astype(out_dtype)
        else:
            acc = scratch[0]
            k = pl.program_id(2)

            @pl.when(k == 0)
            def _():
                acc[...] = p

            @pl.when(k > 0)
            def _():
                acc[...] += p

            @pl.when(k == nk - 1)
            def _():
                r = acc[...]
                if has_add:
                    r = r + add_ref[...]
                o_ref[...] = r.astype(out_dtype)

    in_specs = [a_spec, b_spec] + ([o_spec] if has_add else [])
    args = (a, b) + ((add,) if has_add else ())
    aliases = {}
    if into is not None:
        body, in_specs, args = _ordered_after(body, len(args), in_specs, args, into)
        aliases = {len(args) - 1: 0}
    body, in_specs, args = _ordered_after(body, len(args), in_specs, args, after)
    return pl.pallas_call(
        body, name=name, grid=(ni, N // bn, nk),
        in_specs=in_specs, out_specs=o_spec,
        out_shape=jax.ShapeDtypeStruct((M, N), out_dtype),
        input_output_aliases=aliases,
        scratch_shapes=[pltpu.VMEM((bm, bn), F32)] if nk > 1 else [],
        compiler_params=_cparams("parallel", "parallel", "arbitrary"),
    )(*args)


def _mm_resid_rms(a, b, resid, gain, *, name, bm=512):
    M, K = a.shape
    N = b.shape[1]
    bm = _div(M, bm)

    def body(a_ref, b_ref, r_ref, g_ref, x_ref, h_ref):
        x = r_ref[...] + jnp.dot(a_ref[...], b_ref[...], preferred_element_type=F32)
        x_ref[...] = x
        r = lax.rsqrt(jnp.mean(x * x, axis=-1, keepdims=True) + EPS)
        h_ref[...] = ((x * r) * g_ref[...]).astype(BF16)

    row = pl.BlockSpec((bm, N), lambda i: (i, 0))
    return pl.pallas_call(
        body, name=name, grid=(M // bm,),
        in_specs=[pl.BlockSpec((bm, K), lambda i: (i, 0)), pl.BlockSpec((K, N), lambda i: (0, 0)), row,
                  pl.BlockSpec((1, N), lambda i: (0, 0))],
        out_specs=[row, row], out_shape=[jax.ShapeDtypeStruct((M, N), F32), jax.ShapeDtypeStruct((M, N), BF16)],
        compiler_params=_cparams("parallel"),
    )(a, b, resid, gain)


def _mm_sum2(a1, b1, a2, b2, *, name, bm=1024, bn=512, bk=2816, after=None):
    M, K = a1.shape
    N = b1.shape[1]
    bm, bn, bk = _div(M, bm), _div(N, bn), _div(K, bk)
    nk = K // bk

    def body(a1_ref, b1_ref, a2_ref, b2_ref, o_ref, acc):
        p = (jnp.dot(a1_ref[...], b1_ref[...], preferred_element_type=F32)
             + jnp.dot(a2_ref[...], b2_ref[...], preferred_element_type=F32))
        k = pl.program_id(2)

        @pl.when(k == 0)
        def _():
            acc[...] = p

        @pl.when(k > 0)
        def _():
            acc[...] += p

        @pl.when(k == nk - 1)
        def _():
            o_ref[...] = acc[...]

    a_spec = pl.BlockSpec((bm, bk), lambda i, j, k: (i, k))
    b_spec = pl.BlockSpec((bk, bn), lambda i, j, k: (k, j))
    body, in_specs, args = _ordered_after(body, 4, [a_spec, b_spec, a_spec, b_spec], (a1, b1, a2, b2), after)
    return pl.pallas_call(
        body, name=name, grid=(M // bm, N // bn, nk),
        in_specs=in_specs, out_specs=pl.BlockSpec((bm, bn), lambda i, j, k: (i, j)),
        out_shape=jax.ShapeDtypeStruct((M, N), F32),
        scratch_shapes=[pltpu.VMEM((bm, bn), F32)],
        compiler_params=_cparams("parallel", "parallel", "arbitrary"),
    )(*args)


def _blocks_per_tile(c):
    nb = 1
    while (nb * c) % LANES or (nb * c < 1024 and nb < N_DEV):
        nb *= 2
    assert nb <= N_DEV and (nb * c) % LANES == 0, c
    return nb


def _mm_w8(a, w8, *, name, bm=1024, out_dtype=F32):
    M, K = a.shape
    _, _, c = w8.shape
    nb = _blocks_per_tile(c)
    bm = _div(M, bm)

    def body(a_ref, w_ref, o_ref):
        a_ = a_ref[...]
        for t in range(nb):
            o_ref[:, t * c:(t + 1) * c] = jnp.dot(a_, w_ref[t], preferred_element_type=F32).astype(out_dtype)

    return pl.pallas_call(
        body, name=name, grid=(M // bm, N_DEV // nb),
        in_specs=[pl.BlockSpec((bm, K), lambda i, j: (i, 0)), pl.BlockSpec((nb, K, c), lambda i, j: (j, 0, 0))],
        out_specs=pl.BlockSpec((bm, nb * c), lambda i, j: (i, j)),
        out_shape=jax.ShapeDtypeStruct((M, N_DEV * c), out_dtype),
        compiler_params=_cparams("parallel", "parallel"),
    )(a, w8)


def _mm_w8t(dy, w8, *, name, add=None, out_dtype=F32, bm=1024, bn=1024, after=None, lead=None):
    M = dy.shape[-2]
    _, K, c = w8.shape
    nb = _blocks_per_tile(c)
    nk = N_DEV // nb
    bm, bn = _div(M, bm), _div(K, bn)
    has_add = add is not None
    dims = (((1,), (1,)), ((), ()))

    def body(*refs):
        if has_add:
            dy_ref, w_ref, add_ref, o_ref, acc = refs
        else:
            dy_ref, w_ref, o_ref, acc = refs
        p = lax.dot_general(dy_ref[:, 0:c], w_ref[0], dims, preferred_element_type=F32)
        for t in range(1, nb):
            p = p + lax.dot_general(dy_ref[:, t * c:(t + 1) * c], w_ref[t], dims, preferred_element_type=F32)
        k = pl.program_id(2)

        @pl.when(k == 0)
        def _():
            acc[...] = p

        @pl.when(k > 0)
        def _():
            acc[...] += p

        @pl.when(k == nk - 1)
        def _():
            r = acc[...]
            if has_add:
                r = r + add_ref[...]
            o_ref[...] = r.astype(out_dtype)

    o_spec = pl.BlockSpec((bm, bn), lambda i, j, k: (i, j))
    dy_spec = (pl.BlockSpec((bm, nb * c), lambda i, j, k: (i, k)) if lead is None
               else pl.BlockSpec((None, bm, nb * c), lambda i, j, k: (lead, i, k)))
    in_specs = [dy_spec, pl.BlockSpec((nb, bn, c), lambda i, j, k: (k, j, 0))]
    in_specs += [o_spec] if has_add else []
    args = (dy, w8) + ((add,) if has_add else ())
    body, in_specs, args = _ordered_after(body, len(args), in_specs, args, after)
    return pl.pallas_call(
        body, name=name, grid=(M // bm, K // bn, nk),
        in_specs=in_specs, out_specs=o_spec,
        out_shape=jax.ShapeDtypeStruct((M, K), out_dtype),
        scratch_shapes=[pltpu.VMEM((bm, bn), F32)],
        compiler_params=_cparams("parallel", "parallel", "arbitrary"),
    )(*args)


def _mm_gw8(x, dy, c, *, name, bk=1024, lead=None):
    T, K = x.shape
    nb = _blocks_per_tile(c)
    bk = _div(K, bk)
    dims = (((0,), (0,)), ((), ()))

    def body(x_ref, dy_ref, o_ref):
        x_ = x_ref[...]
        for t in range(nb):
            o_ref[t] = lax.dot_general(x_, dy_ref[:, t * c:(t + 1) * c], dims, preferred_element_type=F32).astype(BF16)

    dy_spec = (pl.BlockSpec((T, nb * c), lambda i, j: (0, j)) if lead is None
               else pl.BlockSpec((None, T, nb * c), lambda i, j: (lead, 0, j)))
    return pl.pallas_call(
        body, name=name, grid=(K // bk, N_DEV // nb),
        in_specs=[pl.BlockSpec((T, bk), lambda i, j: (0, i)), dy_spec],
        out_specs=pl.BlockSpec((nb, bk, c), lambda i, j: (j, i, 0)),
        out_shape=jax.ShapeDtypeStruct((N_DEV, K, c), BF16),
        compiler_params=_cparams("parallel", "parallel"),
    )(x, dy)


def _rms_fwd(x, g, *, name, after=None):
    T, D = x.shape
    tm = _div(T, 256, 8)

    def body(x_ref, g_ref, h_ref):
        xf = x_ref[...]
        r = lax.rsqrt(jnp.mean(xf * xf, axis=-1, keepdims=True) + EPS)
        h_ref[...] = ((xf * r) * g_ref[...]).astype(BF16)

    in_specs = [pl.BlockSpec((tm, D), lambda i: (i, 0)), pl.BlockSpec((1, D), lambda i: (0, 0))]
    body, in_specs, args = _ordered_after(body, 2, in_specs, (x, g), after)
    return pl.pallas_call(
        body, name=name, grid=(T // tm,),
        in_specs=in_specs,
        out_specs=pl.BlockSpec((tm, D), lambda i: (i, 0)),
        out_shape=jax.ShapeDtypeStruct((T, D), BF16),
        compiler_params=_cparams("parallel"),
    )(*args)


def _rms_bwd(x, g, dh, dres, *, name, want_bf16, after=None):
    T, D = x.shape
    tm = _div(T, 256, 8)

    def body(x_ref, g_ref, dh_ref, dres_ref, dx_ref, *rest):
        if want_bf16:
            dxb_ref, dg_ref = rest
        else:
            (dg_ref,) = rest
        xf = x_ref[...]
        r = lax.rsqrt(jnp.mean(xf * xf, axis=-1, keepdims=True) + EPS)
        xhat = xf * r
        dh_ = dh_ref[...]
        dy = dh_ * g_ref[...]
        dx = dres_ref[...] + r * (dy - xhat * jnp.mean(dy * xhat, axis=-1, keepdims=True))
        dx_ref[...] = dx
        if want_bf16:
            dxb_ref[...] = dx.astype(BF16)
        part = jnp.sum(dh_ * xhat, axis=0, keepdims=True)

        @pl.when(pl.program_id(0) == 0)
        def _():
            dg_ref[...] = part

        @pl.when(pl.program_id(0) > 0)
        def _():
            dg_ref[...] += part

    row = pl.BlockSpec((tm, D), lambda i: (i, 0))
    vec = pl.BlockSpec((1, D), lambda i: (0, 0))
    out_specs = [row] + ([row] if want_bf16 else []) + [vec]
    out_shape = ([jax.ShapeDtypeStruct((T, D), F32)] + ([jax.ShapeDtypeStruct((T, D), BF16)] if want_bf16 else [])
                 + [jax.ShapeDtypeStruct((1, D), F32)])
    body, in_specs, args = _ordered_after(body, 4, [row, vec, row, row], (x, g, dh, dres), after)
    return pl.pallas_call(
        body, name=name, grid=(T // tm,),
        in_specs=in_specs, out_specs=out_specs, out_shape=out_shape,
        compiler_params=_cparams("arbitrary"),
    )(*args)


def _loss_head(x, g, target, *, name):
    T, D = x.shape
    tm = _div(T, 256, 8)

    def body(x_ref, g_ref, t_ref, loss_ref, dx_ref, dxb_ref, dg_ref):
        xf = x_ref[...]
        r = lax.rsqrt(jnp.mean(xf * xf, axis=-1, keepdims=True) + EPS)
        xhat = xf * r
        gain = g_ref[...]
        err = xhat * gain - t_ref[...]
        lpart = 0.5 * jnp.sum(jnp.mean(err * err, axis=-1, keepdims=True), axis=0, keepdims=True)
        dh_ = err * (1.0 / D)
        dy = dh_ * gain
        dx = r * (dy - xhat * jnp.mean(dy * xhat, axis=-1, keepdims=True))
        dx_ref[...] = dx
        dxb_ref[...] = dx.astype(BF16)
        part = jnp.sum(dh_ * xhat, axis=0, keepdims=True)

        @pl.when(pl.program_id(0) == 0)
        def _():
            dg_ref[...] = part
            loss_ref[...] = jnp.broadcast_to(lpart, loss_ref.shape)

        @pl.when(pl.program_id(0) > 0)
        def _():
            dg_ref[...] += part
            loss_ref[...] += jnp.broadcast_to(lpart, loss_ref.shape)

    row = pl.BlockSpec((tm, D), lambda i: (i, 0))
    vec = pl.BlockSpec((1, D), lambda i: (0, 0))
    return pl.pallas_call(
        body, name=name, grid=(T // tm,),
        in_specs=[row, vec, row],
        out_specs=[pl.BlockSpec((8, LANES), lambda i: (0, 0)), row, row, vec],
        out_shape=[jax.ShapeDtypeStruct((8, LANES), F32), jax.ShapeDtypeStruct((T, D), F32),
                   jax.ShapeDtypeStruct((T, D), BF16), jax.ShapeDtypeStruct((1, D), F32)],
        compiler_params=_cparams("arbitrary"),
    )(x, g, target)


def _gate_cols(D):
    off_a = 3 * D // 2 + 2 * KV_WIDTH
    off_b = off_a + D
    cw = math.gcd(math.gcd(off_a, off_b), math.gcd(D, 512))
    return cw, off_a // cw, off_b // cw


def _merge_fwd(z, ya, yb, *, name):
    T, D = ya.shape
    cw, ba, bb = _gate_cols(D)
    tm = _div(T, 512, 8)

    def body(ga_ref, gb_ref, ya_ref, yb_ref, m_ref):
        m_ref[...] = (_sigmoid(ga_ref[...].astype(F32)) * ya_ref[...]
                      + _sigmoid(gb_ref[...].astype(F32)) * yb_ref[...]).astype(BF16)

    blk = pl.BlockSpec((tm, cw), lambda i, j: (i, j))
    return pl.pallas_call(
        body, name=name, grid=(T // tm, D // cw),
        in_specs=[pl.BlockSpec((tm, cw), lambda i, j: (i, ba + j)), pl.BlockSpec((tm, cw), lambda i, j: (i, bb + j)), blk, blk],
        out_specs=blk, out_shape=jax.ShapeDtypeStruct((T, D), BF16),
        compiler_params=_cparams("parallel", "parallel"),
    )(z, z, ya, yb)


def _merge_bwd(z, ya, yb, dm, *, name, after=None):
    T, D = ya.shape
    cw, ba, bb = _gate_cols(D)
    nj = D // cw
    assert bb == ba + nj
    tm = _div(T, 512, 8)

    def body(g_ref, ya_ref, yb_ref, dm_ref, dy_ref, dz_ref):
        sig = _sigmoid(g_ref[...].astype(F32))
        dm_ = dm_ref[...]
        y = jnp.where(pl.program_id(1) == 0, ya_ref[...], yb_ref[...])
        dy_ref[...] = (dm_ * sig).astype(BF16)
        dz_ref[...] = (dm_ * y * (sig * (1.0 - sig))).astype(BF16)

    in_specs = [pl.BlockSpec((tm, cw), lambda i, s, j: (i, ba + s * nj + j)),
                pl.BlockSpec((tm, cw), lambda i, s, j: (i, j * (1 - s))),
                pl.BlockSpec((tm, cw), lambda i, s, j: (i, j * s)),
                pl.BlockSpec((tm, cw), lambda i, s, j: (i, j))]
    body, in_specs, args = _ordered_after(body, 4, in_specs, (z, ya, yb, dm), after)
    return pl.pallas_call(
        body, name=name, grid=(T // tm, 2, nj),
        in_specs=in_specs,
        out_specs=[pl.BlockSpec((None, tm, cw), lambda i, s, j: (s, i, j)),
                   pl.BlockSpec((tm, cw), lambda i, s, j: (i, ba + s * nj + j))],
        out_shape=[jax.ShapeDtypeStruct((2, T, D), BF16), jax.ShapeDtypeStruct(z.shape, BF16)],
        compiler_params=_cparams("parallel", "arbitrary", "arbitrary"),
    )(*args)


def _swiglu_mm_fwd(h, wu_t, gate, *, name, bm=1024, bn=512):
    T, D = h.shape
    F = wu_t.shape[0]
    bm, bn = _div(T, bm), _div(F, bn)

    rc = _div(bm, 256, 16)

    def body(h_ref, wu_ref, gin_ref, g_ref, u_ref, act_ref):
        w = wu_ref[...]
        for r in range(0, bm, rc):
            rows = slice(r, r + rc)
            u = lax.dot_general(h_ref[rows, :], w, (((1,), (1,)), ((), ())), preferred_element_type=F32)
            g = gin_ref[rows, :]
            g_ref[rows, :] = g.astype(BF16)
            u_ref[rows, :] = u.astype(BF16)
            act_ref[rows, :] = (g * _sigmoid(g) * u).astype(BF16)

    o_spec = pl.BlockSpec((bm, bn), lambda i, j: (i, j))
    return pl.pallas_call(
        body, name=name, grid=(T // bm, F // bn),
        in_specs=[pl.BlockSpec((bm, D), lambda i, j: (i, 0)), pl.BlockSpec((bn, D), lambda i, j: (j, 0)), o_spec],
        out_specs=[o_spec] * 3, out_shape=[jax.ShapeDtypeStruct((T, F), BF16)] * 3,
        compiler_params=_cparams("parallel", "parallel"),
    )(h, wu_t, gate)


def _swiglu_mm_bwd(dx, w_down, gate, up, *, name, bm=2048, bn=512, after=None):
    T, D = dx.shape
    F = w_down.shape[0]
    bm, bn = _div(T, bm), _div(F, bn)
    dims = (((1,), (1,)), ((), ()))

    rc = _div(bm, 256, 16)

    def body(dx_ref, w_ref, g_ref, u_ref, dg_ref, du_ref):
        w = w_ref[...]
        for r in range(0, bm, rc):
            rows = slice(r, r + rc)
            d = lax.dot_general(dx_ref[rows, :], w, dims, preferred_element_type=F32)
            g = g_ref[rows, :].astype(F32)
            s = _sigmoid(g)
            silu = g * s
            dg_ref[rows, :] = (d * u_ref[rows, :].astype(F32) * (s + silu * (1.0 - s))).astype(BF16)
            du_ref[rows, :] = (d * silu).astype(BF16)

    o_spec = pl.BlockSpec((bm, bn), lambda i, j: (i, j))
    in_specs = [pl.BlockSpec((bm, D), lambda i, j: (i, 0)), pl.BlockSpec((bn, D), lambda i, j: (j, 0)), o_spec, o_spec]
    body, in_specs, args = _ordered_after(body, 4, in_specs, (dx, w_down, gate, up), after)
    out = jax.ShapeDtypeStruct((T, F), BF16)
    return pl.pallas_call(
        body, name=name, grid=(T // bm, F // bn), in_specs=in_specs, out_specs=[o_spec, o_spec], out_shape=[out, out],
        compiler_params=_cparams("parallel", "parallel"),
    )(*args)


def _sgu_fwd(z, gain, ws_b, bs_t, *, name):
    T = z.shape[0]
    SW = gain.shape[1]
    G = SW // BLOCK

    def body(zu_ref, zv_ref, gain_ref, ws_ref, bs_ref, a_ref):
        u = _gelu(zu_ref[...].astype(F32))
        vg = _gelu(zv_ref[...].astype(F32))
        r = lax.rsqrt(jnp.mean(vg * vg, axis=-1, keepdims=True) + EPS)
        vn = ((vg * r) * gain_ref[...]).astype(BF16)
        for g in range(G):
            sl = slice(g * BLOCK, (g + 1) * BLOCK)
            mixed = jnp.dot(ws_ref[g], vn[:, sl], preferred_element_type=F32) + bs_ref[:, g:g + 1]
            a_ref[:, sl] = (u[:, sl] * mixed).astype(BF16)

    return pl.pallas_call(
        body, name=name, grid=(T // BLOCK,),
        in_specs=[pl.BlockSpec((BLOCK, SW), lambda c: (c, 0)), pl.BlockSpec((BLOCK, SW), lambda c: (c, 1)),
                  pl.BlockSpec((1, SW), lambda c: (0, 0)), pl.BlockSpec((G, BLOCK, BLOCK), lambda c: (0, 0, 0)),
                  pl.BlockSpec((BLOCK, G), lambda c: (0, 0))],
        out_specs=pl.BlockSpec((BLOCK, SW), lambda c: (c, 0)),
        out_shape=jax.ShapeDtypeStruct((T, SW), BF16),
        compiler_params=_cparams("parallel"),
    )(z, z, gain, ws_b, bs_t)


def _sgu_bwd(z, gain, ws_b, bs_t, da, dz, *, name):
    T = z.shape[0]
    SW = gain.shape[1]
    G = SW // BLOCK

    def body(zu_ref, zv_ref, gain_ref, ws_ref, bs_ref, da_ref, dz_in_ref, dz_ref, dws_ref, dbs_ref, dgain_ref, dvn_ref):
        first = pl.program_id(0) == 0

        @pl.when(first)
        def _():
            dws_ref[...] = jnp.zeros_like(dws_ref)
            dbs_ref[...] = jnp.zeros_like(dbs_ref)
            dgain_ref[...] = jnp.zeros_like(dgain_ref)

        u, du = _gelu_and_grad(zu_ref[...].astype(F32))
        vg, dvg = _gelu_and_grad(zv_ref[...].astype(F32))
        r = lax.rsqrt(jnp.mean(vg * vg, axis=-1, keepdims=True) + EPS)
        xhat = vg * r
        gain_ = gain_ref[...]
        vn = (xhat * gain_).astype(BF16)
        da_ = da_ref[...]
        for g in range(G):
            sl = slice(g * BLOCK, (g + 1) * BLOCK)
            w = ws_ref[g]
            mixed = jnp.dot(w, vn[:, sl], preferred_element_type=F32) + bs_ref[:, g:g + 1]
            dmix = da_[:, sl] * u[:, sl]
            dz_ref[:, sl] = (da_[:, sl] * mixed * du[:, sl]).astype(BF16)
            dmb = dmix.astype(BF16)
            dws_ref[g] += lax.dot_general(dmb, vn[:, sl], (((1,), (1,)), ((), ())), preferred_element_type=F32)
            dbs_ref[:, g:g + 1] += jnp.sum(dmix, axis=-1, keepdims=True)
            dvn_ref[:, sl] = lax.dot_general(w, dmb, (((0,), (0,)), ((), ())), preferred_element_type=F32)
        dvn = dvn_ref[...]
        dgain_ref[...] += jnp.sum(dvn * xhat, axis=0, keepdims=True)
        dy = dvn * gain_
        dv_ = r * (dy - xhat * jnp.mean(dy * xhat, axis=-1, keepdims=True))
        dz_ref[:, SW:] = (dv_ * dvg).astype(BF16)

    row = pl.BlockSpec((BLOCK, SW), lambda c: (c, 0))
    return pl.pallas_call(
        body, name=name, grid=(T // BLOCK,),
        in_specs=[row, pl.BlockSpec((BLOCK, SW), lambda c: (c, 1)),
                  pl.BlockSpec((1, SW), lambda c: (0, 0)), pl.BlockSpec((G, BLOCK, BLOCK), lambda c: (0, 0, 0)),
                  pl.BlockSpec((BLOCK, G), lambda c: (0, 0)), row, _ANY],
        out_specs=[pl.BlockSpec((BLOCK, 2 * SW), lambda c: (c, 0)), pl.BlockSpec((G, BLOCK, BLOCK), lambda c: (0, 0, 0)),
                   pl.BlockSpec((BLOCK, G), lambda c: (0, 0)), pl.BlockSpec((1, SW), lambda c: (0, 0))],
        out_shape=[jax.ShapeDtypeStruct(dz.shape, dz.dtype),
                   jax.ShapeDtypeStruct((G, BLOCK, BLOCK), F32), jax.ShapeDtypeStruct((BLOCK, G), F32),
                   jax.ShapeDtypeStruct((1, SW), F32)],
        input_output_aliases={6: 0},
        scratch_shapes=[pltpu.VMEM((BLOCK, SW), F32)],
        compiler_params=_cparams("arbitrary"),
    )(z, z, gain, ws_b, bs_t, da, dz)


def _bias_table(rel_bias, bmap, *, name):
    H = rel_bias.shape[1]

    def body(rb_ref, bmap_ref, o_ref):
        bm_ = bmap_ref[...]
        for h in range(H):
            acc = jnp.zeros(bm_.shape, F32)
            for b in range(REL_BUCKETS):
                acc = jnp.where(bm_ == b, rb_ref[b, h], acc)
            o_ref[h] = acc

    return pl.pallas_call(
        body, name=name,
        in_specs=[pl.BlockSpec(memory_space=pltpu.SMEM), pl.BlockSpec(memory_space=pltpu.VMEM)],
        out_specs=pl.BlockSpec(memory_space=pltpu.VMEM),
        out_shape=jax.ShapeDtypeStruct((H, BLOCK, 3 * BLOCK), F32),
    )(rel_bias, bmap)


def _attn_probs(q_ref, kb, bias_ref, sink_ref, s_ref, n, T, group):
    H = s_ref.shape[0]
    for h in range(H):
        kv = h // group
        qh = q_ref[:, h * HEAD_DIM:(h + 1) * HEAD_DIM].astype(BF16)
        s_ref[h] = lax.dot_general(qh, kb[:, kv * HEAD_DIM:(kv + 1) * HEAD_DIM], (((1,), (1,)), ((), ())),
                                   preferred_element_type=F32)
    row = lax.broadcasted_iota(jnp.int32, (BLOCK, 3 * BLOCK), 0)
    col = lax.broadcasted_iota(jnp.int32, (BLOCK, 3 * BLOCK), 1)
    key_pos = n * BLOCK + col - BLOCK
    valid = (jnp.abs(col - BLOCK - row) <= BLOCK) & (key_pos >= 0) & (key_pos < T)
    s = s_ref[...] * (HEAD_DIM ** -0.5) + bias_ref[...]
    s = jnp.where(valid[None], s, NEG)
    sink = sink_ref[...]
    m = jnp.maximum(jnp.max(s, axis=-1, keepdims=True), sink)
    e = jnp.exp(s - m)
    es = jnp.exp(sink - m)
    inv = 1.0 / (jnp.sum(e, axis=-1, keepdims=True) + es)
    return e * inv, es * inv


def _attn_fwd(z, kpad, vpad, bias, sink, *, name):
    T = z.shape[0]
    H = bias.shape[0]
    AW = H * HEAD_DIM
    group = H // N_KV_HEADS

    def body(q_ref, k_ref, v_ref, bias_ref, sink_ref, o_ref, s_ref, p_ref):
        n = pl.program_id(0)
        start = pl.multiple_of(n * BLOCK, BLOCK)
        kb = k_ref[pl.ds(start, 3 * BLOCK), :]
        vb = v_ref[pl.ds(start, 3 * BLOCK), :]
        p, _ = _attn_probs(q_ref, kb, bias_ref, sink_ref, s_ref, n, T, group)
        p_ref[...] = p.astype(BF16)
        for h in range(H):
            kv = h // group
            o = jnp.dot(p_ref[h], vb[:, kv * HEAD_DIM:(kv + 1) * HEAD_DIM], preferred_element_type=F32)
            o_ref[:, h * HEAD_DIM:(h + 1) * HEAD_DIM] = o.astype(BF16)

    full_kv = pl.BlockSpec((T + 2 * BLOCK, KV_WIDTH), lambda n: (0, 0))
    return pl.pallas_call(
        body, name=name, grid=(T // BLOCK,),
        in_specs=[pl.BlockSpec((BLOCK, AW), lambda n: (n, 2)), full_kv, full_kv,
                  pl.BlockSpec((H, BLOCK, 3 * BLOCK), lambda n: (0, 0, 0)), pl.BlockSpec((H, 1, 1), lambda n: (0, 0, 0))],
        out_specs=pl.BlockSpec((BLOCK, AW), lambda n: (n, 0)),
        out_shape=jax.ShapeDtypeStruct((T, AW), BF16),
        scratch_shapes=[pltpu.VMEM((H, BLOCK, 3 * BLOCK), F32), pltpu.VMEM((H, BLOCK, 3 * BLOCK), BF16)],
        compiler_params=_cparams("parallel"),
    )(z, kpad, vpad, bias, sink)


def _attn_bwd(z, kpad, vpad, bias, sink, do, dz, *, name):
    T = z.shape[0]
    H = bias.shape[0]
    AW = H * HEAD_DIM
    group = H // N_KV_HEADS
    scale = HEAD_DIM ** -0.5

    def body(q_ref, k_ref, v_ref, bias_ref, sink_ref, do_ref, dz_in_ref, dq_ref, dk_ref, dv_ref, dbias_ref, dsink_ref,
             s_ref, dp_ref, p_ref, ds_ref):
        n = pl.program_id(0)

        @pl.when(n == 0)
        def _():
            dk_ref[...] = jnp.zeros_like(dk_ref)
            dv_ref[...] = jnp.zeros_like(dv_ref)
            dbias_ref[...] = jnp.zeros_like(dbias_ref)
            dsink_ref[...] = jnp.zeros_like(dsink_ref)

        start = pl.multiple_of(n * BLOCK, BLOCK)
        kb = k_ref[pl.ds(start, 3 * BLOCK), :]
        vb = v_ref[pl.ds(start, 3 * BLOCK), :]
        p, p_sink = _attn_probs(q_ref, kb, bias_ref, sink_ref, s_ref, n, T, group)
        s_ref[...] = p
        p_ref[...] = p.astype(BF16)
        for h in range(H):
            kv = h // group
            dp_ref[h] = lax.dot_general(do_ref[:, h * HEAD_DIM:(h + 1) * HEAD_DIM], vb[:, kv * HEAD_DIM:(kv + 1) * HEAD_DIM],
                                        (((1,), (1,)), ((), ())), preferred_element_type=F32)
        p = s_ref[...]
        dp = dp_ref[...]
        delta = jnp.sum(p * dp, axis=-1, keepdims=True)
        ds = p * (dp - delta)
        dbias_ref[...] += ds
        dsink_ref[...] += -(p_sink * delta)
        ds_ref[...] = ds.astype(BF16)
        for kv in range(N_KV_HEADS):
            ksl = slice(kv * HEAD_DIM, (kv + 1) * HEAD_DIM)
            dk_acc = jnp.zeros((3 * BLOCK, HEAD_DIM), F32)
            dv_acc = jnp.zeros((3 * BLOCK, HEAD_DIM), F32)
            for gi in range(group):
                h = kv * group + gi
                hsl = slice(h * HEAD_DIM, (h + 1) * HEAD_DIM)
                dsb = ds_ref[h]
                dq = jnp.dot(dsb, kb[:, ksl], preferred_element_type=F32) * scale
                dq_ref[:, hsl] = dq.astype(BF16)
                dk_acc = dk_acc + lax.dot_general(dsb, q_ref[:, hsl].astype(BF16), (((0,), (0,)), ((), ())),
                                                  preferred_element_type=F32)
                dv_acc = dv_acc + lax.dot_general(p_ref[h], do_ref[:, hsl], (((0,), (0,)), ((), ())),
                                                  preferred_element_type=F32)
            dk_ref[pl.ds(start, 3 * BLOCK), ksl] += dk_acc * scale
            dv_ref[pl.ds(start, 3 * BLOCK), ksl] += dv_acc

    full_kv = pl.BlockSpec((T + 2 * BLOCK, KV_WIDTH), lambda n: (0, 0))
    bias_spec = pl.BlockSpec((H, BLOCK, 3 * BLOCK), lambda n: (0, 0, 0))
    row = pl.BlockSpec((BLOCK, AW), lambda n: (n, 0))
    q_cols = pl.BlockSpec((BLOCK, AW), lambda n: (n, 2))
    band = (H, BLOCK, 3 * BLOCK)
    return pl.pallas_call(
        body, name=name, grid=(T // BLOCK,),
        in_specs=[q_cols, full_kv, full_kv, bias_spec, pl.BlockSpec((H, 1, 1), lambda n: (0, 0, 0)), row, _ANY],
        out_specs=[q_cols, full_kv, full_kv, bias_spec, pl.BlockSpec((H, BLOCK, 1), lambda n: (0, 0, 0))],
        out_shape=[jax.ShapeDtypeStruct(dz.shape, dz.dtype),
                   jax.ShapeDtypeStruct((T + 2 * BLOCK, KV_WIDTH), F32), jax.ShapeDtypeStruct((T + 2 * BLOCK, KV_WIDTH), F32),
                   jax.ShapeDtypeStruct(band, F32), jax.ShapeDtypeStruct((H, BLOCK, 1), F32)],
        input_output_aliases={6: 0},
        scratch_shapes=[pltpu.VMEM(band, F32), pltpu.VMEM(band, F32), pltpu.VMEM(band, BF16), pltpu.VMEM(band, BF16)],
        compiler_params=_cparams("arbitrary"),
    )(z, kpad, vpad, bias, sink, do, dz)


def _dkv_into(dkp, dvp, dz, *, name):
    T = dz.shape[0]
    D = (dz.shape[1] - 2 * KV_WIDTH) * 2 // 7
    col = (D + D // 2) // (2 * KV_WIDTH)
    assert col * 2 * KV_WIDTH == D + D // 2

    def body(dk_ref, dv_ref, dz_in_ref, o_ref):
        o_ref[:, :KV_WIDTH] = dk_ref[...].astype(BF16)
        o_ref[:, KV_WIDTH:] = dv_ref[...].astype(BF16)

    kv = pl.BlockSpec((BLOCK, KV_WIDTH), lambda n: (n + 1, 0))
    return pl.pallas_call(
        body, name=name, grid=(T // BLOCK,),
        in_specs=[kv, kv, _ANY], out_specs=pl.BlockSpec((BLOCK, 2 * KV_WIDTH), lambda n: (n, col)),
        out_shape=jax.ShapeDtypeStruct(dz.shape, dz.dtype), input_output_aliases={2: 0},
        compiler_params=_cparams("parallel"),
    )(dkp, dvp, dz)


def _kv_pad(z, *, name):
    T = z.shape[0]
    D = (z.shape[1] - 2 * KV_WIDTH) * 2 // 7
    kcol = (D + D // 2) // KV_WIDTH
    nb = T // BLOCK

    def body(k_ref, v_ref, ko_ref, vo_ref):
        b = pl.program_id(0)
        inside = (b >= 1) & (b <= nb)
        ko_ref[...] = jnp.where(inside, k_ref[...].astype(F32), 0.0).astype(BF16)
        vo_ref[...] = jnp.where(inside, v_ref[...].astype(F32), 0.0).astype(BF16)

    out = jax.ShapeDtypeStruct((T + 2 * BLOCK, KV_WIDTH), BF16)
    o_spec = pl.BlockSpec((BLOCK, KV_WIDTH), lambda b: (b, 0))
    return pl.pallas_call(
        body, name=name, grid=(nb + 2,),
        in_specs=[pl.BlockSpec((BLOCK, KV_WIDTH), lambda b: (jnp.clip(b - 1, 0, nb - 1), kcol)),
                  pl.BlockSpec((BLOCK, KV_WIDTH), lambda b: (jnp.clip(b - 1, 0, nb - 1), kcol + 1))],
        out_specs=[o_spec, o_spec], out_shape=[out, out],
        compiler_params=_cparams("parallel"),
    )(z, z)


def _attn_small_grads(dbias, dsink_rows, bmap, after, *, name):
    H = dbias.shape[0]

    def body(dbias_ref, dsink_ref, bmap_ref, drel_ref, ds_ref):
        bm_ = bmap_ref[...]
        for h in range(H):
            d = dbias_ref[h]
            for b in range(REL_BUCKETS):
                drel_ref[b, h] = jnp.sum(jnp.where(bm_ == b, d, 0.0))
            ds_ref[0, h] = jnp.sum(dsink_ref[h])

    vmem = pl.BlockSpec(memory_space=pltpu.VMEM)
    smem = pl.BlockSpec(memory_space=pltpu.SMEM)
    body, in_specs, args = _ordered_after(body, 3, [vmem, vmem, vmem], (dbias, dsink_rows, bmap), after)
    return pl.pallas_call(
        body, name=name, in_specs=in_specs, out_specs=[smem, smem],
        out_shape=[jax.ShapeDtypeStruct((REL_BUCKETS, H), F32), jax.ShapeDtypeStruct((1, H), F32)],
    )(*args)


def _local_step(x, target, weight, emit, flush, norm_mix, v_gain, w_s, b_s, sink, rel_bias, norm_ffn, norm_final, early=()):
    T, D = x.shape
    ws_b = w_s.astype(BF16)
    bs_t = b_s.T
    bmap = jnp.asarray(_bucket_map())
    sink = sink.reshape(-1, 1, 1)

    bias = _bias_table(rel_bias, bmap, name="bias_table")
    h = _rms_fwd(x, norm_mix, name="rms_mix", after=[bias, *early])
    w_in = weight("w_in", h, part=0)
    z_lo = _mm(h, w_in, tb=True, name="mm_z_lo", bm=2048, bn=768, bk=D // 2, k_blocks=(0, 1))
    w_in = weight("w_in", z_lo, part=1)
    z = _mm(h, w_in, tb=True, add=z_lo, out_dtype=BF16, name="mm_z_hi", bm=2048, bn=768, bk=D // 2, k_blocks=(1, 1))
    a = _sgu_fwd(z, v_gain, ws_b, bs_t, name="sgu_fwd")
    w_a = weight("w_a_out", a)
    ya = _mm_w8(a, w_a, name="mm_ya", bm=2048, out_dtype=BF16)
    kpad, vpad = _kv_pad(z, name="kv_pad")
    o = _attn_fwd(z, kpad, vpad, bias, sink, name="attn_fwd")
    w_b = weight("w_b_out", o)
    yb = _mm_w8(o, w_b, name="mm_yb", bm=2048, out_dtype=BF16)
    m = _merge_fwd(z, ya, yb, name="merge_fwd")
    w_o = weight("w_o", m)
    x1, h2 = _mm_resid_rms(m, w_o, x, norm_ffn, name="mm_x1_rms")
    w_gate = weight("w_gate", h2)
    gate = _mm(h2, w_gate, tb=True, name="mm_gate", bm=2048, bn=512)
    w_up = weight("w_up", gate)
    gate, up, act = _swiglu_mm_fwd(h2, w_up, gate, name="mm_up_swiglu")
    w_down = weight("w_down", act)
    x2 = _mm(act, w_down, name="mm_x2", add=x1, bm=1024, bn=512)
    loss, dx2, dx2b, g_norm_final = _loss_head(x2, norm_final, target, name="loss_head")

    g_w_down = _mm(act, dx2b, ta=True, out_dtype=BF16, name="mm_gwdown", bm=512, bn=2048)
    tok = emit(("w_down",), (g_w_down,))
    dgate, dup = _swiglu_mm_bwd(dx2b, w_down, gate, up, name="mm_dact_swiglu", after=tok)
    tok = flush(dgate)
    g_w_gate = _mm(dgate, h2, ta=True, out_dtype=BF16, name="mm_gwgate", bm=512, bn=2048, after=tok)
    g_w_up = _mm(dup, h2, ta=True, out_dtype=BF16, name="mm_gwup", bm=512, bn=2048)
    tok = emit(("w_gate", "w_up"), (g_w_gate, g_w_up))
    dh2 = _mm_sum2(dgate, w_gate, dup, w_up, name="mm_dh2", after=tok)
    tok = flush(dh2)
    dx1, dx1b, g_norm_ffn = _rms_bwd(x1, norm_ffn, dh2, dx2, name="rms_ffn_bwd", want_bf16=True, after=tok)

    g_w_o = _mm(m, dx1b, ta=True, out_dtype=BF16, name="mm_gwo", bm=2048, bn=512)
    tok = emit(("w_o",), (g_w_o,))
    dm = _mm(dx1b, w_o, tb=True, name="mm_dm", bm=2048, bn=512, after=tok)
    tok = flush(dm)
    dy, dz = _merge_bwd(z, ya, yb, dm, name="merge_bwd", after=tok)
    g_w_a = _mm_gw8(a, dy, w_a.shape[2], name="mm_gwa", lead=0)
    g_w_b = _mm_gw8(o, dy, w_b.shape[2], name="mm_gwb", lead=1)
    tok = emit(("w_a_out", "w_b_out"), (g_w_a, g_w_b))
    da = _mm_w8t(dy, w_a, name="mm_da", bm=2048, bn=512, after=tok, lead=0)
    tok = flush(da)
    do = _mm_w8t(dy, w_b, out_dtype=BF16, name="mm_do", bm=2048, bn=512, after=tok, lead=1)
    dz, g_w_s, g_b_s_t, g_v_gain = _sgu_bwd(z, v_gain, ws_b, bs_t, da, dz, name="sgu_bwd")
    dz, dkp, dvp, dbias, dsink_rows = _attn_bwd(z, kpad, vpad, bias, sink, do, dz, name="attn_bwd")
    dz = _dkv_into(dkp, dvp, dz, name="dkv_into_dz")
    g_w_in = _mm(dz, h, ta=True, out_dtype=BF16, name="mm_gwin", bm=768, bn=2048)
    tok = emit(("w_in",), (g_w_in,))
    half = dict(bm=T // 2, bn=256)
    dh = _mm(dz, w_in, name="mm_dh_top", row_blocks=(0, 1), after=tok, **half)
    tok = flush(dh)
    dh = _mm(dz, w_in, name="mm_dh_bottom", row_blocks=(1, 1), into=dh, after=tok, **half)
    g_rel_bias, g_sink = _attn_small_grads(dbias, dsink_rows, bmap, dh, name="attn_small_grads")
    grad_x, g_norm_mix = _rms_bwd(x, norm_mix, dh, dx1, name="rms_mix_bwd", want_bf16=False)

    small = dict(norm_mix=g_norm_mix, sgu_v_gain=g_v_gain, sgu_w_s=g_w_s, sgu_b_s=g_b_s_t.T, attn_sink=g_sink,
                 rel_bias=g_rel_bias, norm_ffn=g_norm_ffn, norm_final=g_norm_final)
    return loss, grad_x, small


def _position():
    return lax.axis_index("x"), lax.axis_index("y"), lax.axis_index("c")


def _other_chips(x, y):
    return [(1 - x, y), (x, 1 - y), (1 - x, 1 - y)]


def _slot(px, py, pc):
    return 4 * px + 2 * py + pc


_HBM = pl.BlockSpec(memory_space=pltpu.HBM)
_SEM = pl.BlockSpec(memory_space=pltpu.SEMAPHORE)
_DATAFLOW = pltpu.SideEffectType.DATAFLOW_SIDE_EFFECTING


def _in_hbm(a):
    return pltpu.with_memory_space_constraint(a, pltpu.HBM)


def _own_slot(shard, pos, *, name, after=None):
    R, C = shard.shape
    tr = _div(R, 256, 16)

    def body(pos_ref, w_ref, o_ref):
        o_ref[...] = w_ref[...].astype(BF16)

    body, in_specs, args = _ordered_after(body, 2, [pl.BlockSpec((tr, C), lambda i, pos_ref: (i, 0))], (pos, shard), after)
    grid_spec = pltpu.PrefetchScalarGridSpec(
        num_scalar_prefetch=1, grid=(R // tr,), in_specs=in_specs,
        out_specs=pl.BlockSpec((None, tr, C), lambda i, pos_ref: (pos_ref[0], i, 0)))
    return pl.pallas_call(
        body, name=name, grid_spec=grid_spec,
        out_shape=jax.ShapeDtypeStruct((N_DEV, R, C), BF16),
        compiler_params=_cparams("parallel"),
    )(*args)


def _slot_part(ref, slot, cols):
    return ref.at[slot] if cols is None else ref.at[slot, :, pl.ds(cols[0], cols[1])]


def _ag_copies(w, land_ref, send_sems, recv_sems, cols=None):
    x, y, c = _position()
    mine = _slot_part(land_ref, _slot(x, y, c), cols)
    targets = [(px, py, c) for px, py in _other_chips(x, y)] + [(x, y, 1 - c)]
    return [pltpu.make_async_remote_copy(src_ref=mine, dst_ref=mine, send_sem=send_sems.at[4 * w + k],
                                         recv_sem=recv_sems.at[4 * w + k], device_id=to, device_id_type=MESH)
            for k, to in enumerate(targets)]


def _ag_start(buffers, groups, *, name, after=None, cols=None):
    lands = [buffers[i] for g in groups for i in g]
    n, ng = len(lands), len(groups)
    sizes = [len(g) for g in groups]

    def body(*refs):
        land_refs = refs[:n]
        sems = refs[n:n + 2 * ng]
        token = refs[-1]
        i = 0
        for g in range(ng):
            for w in range(sizes[g]):
                for cp in _ag_copies(w, land_refs[i], sems[2 * g], sems[2 * g + 1], cols):
                    cp.start()
                i += 1
        token[...] = jnp.zeros_like(token)

    sem_shapes = [pltpu.SemaphoreType.DMA((4 * k,)) for k in sizes for _ in range(2)]
    body, in_specs, args = _ordered_after(body, n, [_HBM] * n, tuple(_in_hbm(a) for a in lands), after)
    outs = pl.pallas_call(
        body, name=name,
        in_specs=in_specs,
        out_specs=tuple([_SEM] * (2 * ng) + [_HBM] * n + [pl.BlockSpec(memory_space=pltpu.VMEM)]),
        out_shape=tuple(sem_shapes + [pltpu.HBM(a.shape, a.dtype) for a in lands] + [jax.ShapeDtypeStruct((8, LANES), F32)]),
        input_output_aliases={i: 2 * ng + i for i in range(n)},
        compiler_params=pltpu.CompilerParams(has_side_effects=_DATAFLOW),
    )(*args)
    sems, thru = outs[:2 * ng], outs[2 * ng:2 * ng + n]
    result, i = [], 0
    for g in range(ng):
        k = sizes[g]
        result.append((sems[2 * g], sems[2 * g + 1], list(thru[i:i + k])))
        i += k
    return result, outs[-1]


def _ag_wait(send_sems, recv_sems, lands, after, *, name, cols=None):
    n = len(lands)

    def body(*refs):
        land_refs = refs[:n]
        send_ref, recv_ref = refs[n], refs[n + 1]
        token = refs[-1]
        for w in range(n):
            for cp in _ag_copies(w, land_refs[w], send_ref, recv_ref, cols):
                cp.wait_send()
                cp.wait_recv()
        token[...] = jnp.zeros_like(token)

    outs = pl.pallas_call(
        body, name=name,
        in_specs=[_HBM] * n + [_SEM, _SEM, _ANY],
        out_specs=tuple([_HBM] * n + [pl.BlockSpec(memory_space=pltpu.VMEM)]),
        out_shape=tuple([pltpu.HBM(a.shape, a.dtype) for a in lands] + [jax.ShapeDtypeStruct((8, LANES), F32)]),
        input_output_aliases={i: i for i in range(n)},
        compiler_params=pltpu.CompilerParams(has_side_effects=_DATAFLOW),
    )(*lands, send_sems, recv_sems, after)
    return list(outs[:n]), outs[n]


def _ag_forward(lands, *, name, after=None, cols=None):
    n = len(lands)

    def body(*refs):
        in_refs, out_refs = refs[:n], refs[n:2 * n]
        send_sems, recv_sems = refs[2 * n:]
        x, y, c = _position()
        copies = []
        for w in range(n):
            for k, (px, py) in enumerate(_other_chips(x, y)):
                cp = pltpu.make_async_remote_copy(
                    src_ref=_slot_part(in_refs[w], _slot(px, py, c), cols),
                    dst_ref=_slot_part(out_refs[w], _slot(px, py, c), cols),
                    send_sem=send_sems.at[3 * w + k], recv_sem=recv_sems.at[3 * w + k],
                    device_id=(x, y, 1 - c), device_id_type=MESH)
                cp.start()
                copies.append(cp)
        for cp in copies:
            cp.wait()

    body, in_specs, args = _ordered_after(body, n, [_ANY] * n, tuple(lands), after)
    return pl.pallas_call(
        body, name=name,
        in_specs=in_specs, out_specs=[_ANY] * n,
        out_shape=[jax.ShapeDtypeStruct(a.shape, a.dtype) for a in lands],
        input_output_aliases={i: i for i in range(n)},
        scratch_shapes=[pltpu.SemaphoreType.DMA((3 * n,)), pltpu.SemaphoreType.DMA((3 * n,))],
    )(*args)


def _sibling_copies(w, g8_ref, land_ref, send_sems, recv_sems):
    x, y, c = _position()
    return [pltpu.make_async_remote_copy(src_ref=g8_ref.at[2 * p + (1 - c)], dst_ref=land_ref.at[p],
                                         send_sem=send_sems.at[4 * w + p], recv_sem=recv_sems.at[4 * w + p],
                                         device_id=(x, y, 1 - c), device_id_type=MESH)
            for p in range(4)]


def _chip_copies(w, sums_ref, land_ref, send_sems, recv_sems):
    x, y, c = _position()
    return [pltpu.make_async_remote_copy(src_ref=sums_ref.at[2 * px + py], dst_ref=land_ref.at[k],
                                         send_sem=send_sems.at[3 * w + k], recv_sem=recv_sems.at[3 * w + k],
                                         device_id=(px, py, c), device_id_type=MESH)
            for k, (px, py) in enumerate(_other_chips(x, y))]


def _copies_start(copies, per_weight, srcs, *, name):
    n = len(srcs)
    lands = [lax.empty((per_weight,) + s.shape[1:], s.dtype) for s in srcs]

    def body(*refs):
        src_refs, land_refs = refs[:n], refs[n:2 * n]
        send_sems, recv_sems = refs[2 * n], refs[2 * n + 1]
        token = refs[-1]
        for w in range(n):
            for cp in copies(w, src_refs[w], land_refs[w], send_sems, recv_sems):
                cp.start()
        token[...] = jnp.zeros_like(token)

    outs = pl.pallas_call(
        body, name=name,
        in_specs=[_HBM] * (2 * n),
        out_specs=tuple([_SEM, _SEM] + [_HBM] * (2 * n) + [pl.BlockSpec(memory_space=pltpu.VMEM)]),
        out_shape=tuple([pltpu.SemaphoreType.DMA((per_weight * n,)), pltpu.SemaphoreType.DMA((per_weight * n,))]
                        + [pltpu.HBM(a.shape, a.dtype) for a in srcs + lands] + [jax.ShapeDtypeStruct((8, LANES), F32)]),
        input_output_aliases={i: 2 + i for i in range(2 * n)},
        compiler_params=pltpu.CompilerParams(has_side_effects=_DATAFLOW),
    )(*[_in_hbm(a) for a in srcs + lands])
    return outs[0], outs[1], list(outs[2:2 + n]), list(outs[2 + n:2 + 2 * n]), outs[-1]


def _copies_wait(copies, send_sems, recv_sems, srcs, lands, after, *, name):
    n = len(srcs)

    def body(*refs):
        src_refs, land_refs = refs[:n], refs[n:2 * n]
        send_ref, recv_ref = refs[2 * n], refs[2 * n + 1]
        for w in range(n):
            for cp in copies(w, src_refs[w], land_refs[w], send_ref, recv_ref):
                cp.wait_send()
                cp.wait_recv()

    outs = pl.pallas_call(
        body, name=name,
        in_specs=[_HBM] * (2 * n) + [_SEM, _SEM, _ANY],
        out_specs=tuple([_HBM] * (2 * n)),
        out_shape=tuple(pltpu.HBM(a.shape, a.dtype) for a in srcs + lands),
        input_output_aliases={i: i for i in range(2 * n)},
        compiler_params=pltpu.CompilerParams(has_side_effects=_DATAFLOW),
    )(*srcs, *lands, send_sems, recv_sems, after)
    return list(outs[:n]), list(outs[n:])


def _chip_sums(g8, from_sibling, pos, *, name):
    _, R, C = g8.shape
    tr = _div(R, 512, 16)

    def body(pos_ref, g_ref, s_ref, o_ref):
        o_ref[...] = (g_ref[...].astype(F32) + s_ref[...].astype(F32)).astype(BF16)

    def chip(k, pos_ref):
        return jnp.where(k >= pos_ref[1], k + 1, k)

    grid_spec = pltpu.PrefetchScalarGridSpec(
        num_scalar_prefetch=1, grid=(3, R // tr),
        in_specs=[pl.BlockSpec((None, tr, C), lambda k, i, pos_ref: (2 * chip(k, pos_ref) + pos_ref[2], i, 0)),
                  pl.BlockSpec((None, tr, C), lambda k, i, pos_ref: (chip(k, pos_ref), i, 0))],
        out_specs=pl.BlockSpec((None, tr, C), lambda k, i, pos_ref: (chip(k, pos_ref), i, 0)))
    return pl.pallas_call(
        body, name=name, grid_spec=grid_spec,
        out_shape=jax.ShapeDtypeStruct((4, R, C), BF16),
        compiler_params=_cparams("parallel", "parallel"),
    )(pos, g8, from_sibling)


def _small_all_reduce(packed, after, *, name):
    R, L = packed.shape

    def body(x_ref, sum_ref, gath_ref, send_sems, recv_sems, local_sem):
        x, y, c = _position()
        me, sibling = (x, y, c), (x, y, 1 - c)
        chips = _other_chips(x, y)

        def rows(px, py, pc):
            return gath_ref.at[pl.ds(_slot(px, py, pc) * R, R), :]

        def copy(k, block, to, src=None):
            return pltpu.make_async_remote_copy(
                src_ref=rows(*block) if src is None else src, dst_ref=rows(*block),
                send_sem=send_sems.at[k], recv_sem=recv_sems.at[k], device_id=to, device_id_type=MESH)

        mine = pltpu.make_async_copy(x_ref, rows(*me), local_sem)
        mine.start()
        first = [copy(0, me, sibling, src=x_ref)]
        first += [copy(1 + j, me, (*chip, c), src=x_ref) for j, chip in enumerate(chips)]
        for cp in first:
            cp.start()
        passed = [copy(4 + j, (*chip, c), sibling) for j, chip in enumerate(chips)]
        for j, chip in enumerate(chips):
            copy(1 + j, (*chip, c), me).wait_recv()
            passed[j].start()
        copy(0, sibling, me).wait_recv()
        for j, chip in enumerate(chips):
            copy(4 + j, (*chip, 1 - c), me).wait_recv()
        for cp in first + passed:
            cp.wait_send()
        mine.wait()
        acc = gath_ref[0:R, :]
        for d in range(1, N_DEV):
            acc = acc + gath_ref[d * R:(d + 1) * R, :]
        sum_ref[...] = acc

    vmem = pl.BlockSpec(memory_space=pltpu.VMEM)
    body, in_specs, args = _ordered_after(body, 1, [vmem], (packed,), after)
    return pl.pallas_call(
        body, name=name, in_specs=in_specs, out_specs=vmem,
        out_shape=jax.ShapeDtypeStruct((R, L), F32),
        scratch_shapes=[pltpu.VMEM((N_DEV * R, L), F32), pltpu.SemaphoreType.DMA((7,)), pltpu.SemaphoreType.DMA((7,)),
                        pltpu.SemaphoreType.DMA],
        compiler_params=pltpu.CompilerParams(vmem_limit_bytes=VMEM_LIMIT),
    )(*args)


def _adamw_math(w, g, m, v):
    m = ADAM_B1 * m + (1.0 - ADAM_B1) * g
    v = ADAM_B2 * v + (1.0 - ADAM_B2) * (g * g)
    m_hat = m / (1.0 - ADAM_B1 ** ADAM_STEP)
    v_hat = v / (1.0 - ADAM_B2 ** ADAM_STEP)
    delta = -ADAM_LR * (m_hat / (jnp.sqrt(v_hat) + ADAM_EPS) + ADAM_WD * w)
    return delta, m, v


def _adamw_shard(w, m, v, g8, from_sibling, from_chips, pos, *, name):
    R, C = w.shape
    tr = _div(R, 256, 16)

    def body(pos_ref, w_ref, m_ref, v_ref, g_ref, s_ref, r_ref, go_ref, d_ref, mo_ref, vo_ref):
        g = g_ref[...].astype(F32) + s_ref[...].astype(F32)
        for k in range(3):
            g = g + r_ref[k].astype(F32)
        delta, m_, v_ = _adamw_math(w_ref[...], g, m_ref[...], v_ref[...])
        go_ref[...] = g
        d_ref[...] = delta
        mo_ref[...] = m_
        vo_ref[...] = v_

    blk = pl.BlockSpec((tr, C), lambda i, pos_ref: (i, 0))
    grid_spec = pltpu.PrefetchScalarGridSpec(
        num_scalar_prefetch=1, grid=(R // tr,),
        in_specs=[blk, blk, blk,
                  pl.BlockSpec((None, tr, C), lambda i, pos_ref: (pos_ref[0], i, 0)),
                  pl.BlockSpec((None, tr, C), lambda i, pos_ref: (pos_ref[1], i, 0)),
                  pl.BlockSpec((3, tr, C), lambda i, pos_ref: (0, i, 0))],
        out_specs=[blk] * 4)
    out = jax.ShapeDtypeStruct((R, C), F32)
    return pl.pallas_call(
        body, name=name, grid_spec=grid_spec, out_shape=[out] * 4,
        compiler_params=_cparams("parallel"),
    )(pos, w, m, v, g8, from_sibling, from_chips)


def _adamw_small(w, g, m, v, *, name):
    R, L = w.shape

    def body(w_ref, g_ref, m_ref, v_ref, d_ref, mo_ref, vo_ref):
        delta, m_, v_ = _adamw_math(w_ref[...], g_ref[...], m_ref[...], v_ref[...])
        d_ref[...] = delta
        mo_ref[...] = m_
        vo_ref[...] = v_

    vmem = pl.BlockSpec(memory_space=pltpu.VMEM)
    out = jax.ShapeDtypeStruct((R, L), F32)
    return pl.pallas_call(body, name=name, in_specs=[vmem] * 4, out_specs=[vmem] * 3, out_shape=[out] * 3)(w, g, m, v)


_TILE = 8 * LANES


def _pack(pieces):
    rows = []
    for p in pieces:
        flat = p.reshape(-1).astype(F32)
        padded = -(-flat.shape[0] // _TILE) * _TILE
        rows.append(jnp.pad(flat, (0, padded - flat.shape[0])).reshape(-1, LANES))
    return jnp.concatenate(rows, axis=0)


def _unpack(packed, like):
    out, r = [], 0
    for p in like:
        size = int(np.prod(p.shape)) if p.shape else 1
        nrows = -(-size // _TILE) * 8
        out.append(packed[r:r + nrows].reshape(-1)[:size].reshape(p.shape))
        r += nrows
    return out


_BIG = ("w_in", "w_a_out", "w_b_out", "w_o", "w_gate", "w_up", "w_down")
_TRANSPOSED = ("w_in", "w_gate", "w_up")
_COL_SHARDED = ("w_a_out", "w_b_out")
_GATHER_GROUPS = (("w_in",), ("w_a_out", "w_b_out", "w_o"), ("w_gate",), ("w_up",), ("w_down",))
_START_AFTER_WAIT = {0: (1, 2), 1: (3,), 2: (4,)}
_SMALL = ("norm_mix", "sgu_v_gain", "sgu_w_s", "sgu_b_s", "attn_sink", "rel_bias", "norm_ffn", "norm_final")
_ORDER = ("w_in", "norm_mix", "sgu_v_gain", "sgu_w_s", "sgu_b_s", "w_a_out", "attn_sink", "rel_bias", "w_b_out", "w_o",
          "norm_ffn", "w_gate", "w_up", "w_down", "norm_final")


def _shard(name, a):
    return jnp.swapaxes(a, 1, 2)[0] if name in _TRANSPOSED else a[0]


def _unshard(name, a):
    return jnp.swapaxes(a[None], 1, 2) if name in _TRANSPOSED else a[None]


def _whole(name, gathered):
    _, r, c = gathered.shape
    return gathered if name in _COL_SHARDED else gathered.reshape(N_DEV * r, c)


def _blocks(name, grad):
    if name in _COL_SHARDED:
        return grad
    r, c = grad.shape
    return grad.reshape(N_DEV, r // N_DEV, c)


def kernel(x, w_in, norm_mix, sgu_v_gain, sgu_w_s, sgu_b_s, w_a_out, attn_sink, rel_bias, w_b_out, w_o, norm_ffn, w_gate, w_up, w_down, norm_final, loss_target, m_w_in, m_norm_mix, m_sgu_v_gain, m_sgu_w_s, m_sgu_b_s, m_w_a_out, m_attn_sink, m_rel_bias, m_w_b_out, m_w_o, m_norm_ffn, m_w_gate, m_w_up, m_w_down, m_norm_final, v_w_in, v_norm_mix, v_sgu_v_gain, v_sgu_w_s, v_sgu_b_s, v_w_a_out, v_attn_sink, v_rel_bias, v_w_b_out, v_w_o, v_norm_ffn, v_w_gate, v_w_up, v_w_down, v_norm_final):
    w = dict(w_in=w_in, norm_mix=norm_mix, sgu_v_gain=sgu_v_gain, sgu_w_s=sgu_w_s, sgu_b_s=sgu_b_s, w_a_out=w_a_out,
             attn_sink=attn_sink, rel_bias=rel_bias, w_b_out=w_b_out, w_o=w_o, norm_ffn=norm_ffn, w_gate=w_gate,
             w_up=w_up, w_down=w_down, norm_final=norm_final)
    m = dict(w_in=m_w_in, norm_mix=m_norm_mix, sgu_v_gain=m_sgu_v_gain, sgu_w_s=m_sgu_w_s, sgu_b_s=m_sgu_b_s,
             w_a_out=m_w_a_out, attn_sink=m_attn_sink, rel_bias=m_rel_bias, w_b_out=m_w_b_out, w_o=m_w_o,
             norm_ffn=m_norm_ffn, w_gate=m_w_gate, w_up=m_w_up, w_down=m_w_down, norm_final=m_norm_final)
    v = dict(w_in=v_w_in, norm_mix=v_norm_mix, sgu_v_gain=v_sgu_v_gain, sgu_w_s=v_sgu_w_s, sgu_b_s=v_sgu_b_s,
             w_a_out=v_w_a_out, attn_sink=v_attn_sink, rel_bias=v_rel_bias, w_b_out=v_w_b_out, w_o=v_w_o,
             norm_ffn=v_norm_ffn, w_gate=v_w_gate, w_up=v_w_up, w_down=v_w_down, norm_final=v_norm_final)
    xc, yc, cc = _position()
    pos = jnp.stack([_slot(xc, yc, cc), 2 * xc + yc, cc]).astype(jnp.int32)

    in_flight, full, slots = {}, {}, {}

    def start_gather(groups, after):
        names = [n for gi in groups for n in _GATHER_GROUPS[gi]]
        flights, token = _ag_start([slots[n] for n in names], [[names.index(n) for n in _GATHER_GROUPS[gi]] for gi in groups],
                                   name="ag_start_%d" % groups[0], after=after)
        in_flight.update(zip(groups, flights))
        return token

    def weight(name, after, part=None):
        if name == "w_in":
            return first_weight(after, part)
        if name not in full:
            gi = next(i for i, grp in enumerate(_GATHER_GROUPS) if name in grp)
            send_sems, recv_sems, lands = in_flight[gi]
            lands, token = _ag_wait(send_sems, recv_sems, lands, after, name="ag_wait_%d" % gi)
            started = start_gather(_START_AFTER_WAIT[gi], token) if gi in _START_AFTER_WAIT else None
            gathered = _ag_forward(lands, name="ag_forward_%d" % gi, after=started)
            full.update({n: _whole(n, g) for n, g in zip(_GATHER_GROUPS[gi], gathered)})
        return full[name]

    slots["w_in"] = _own_slot(_shard("w_in", w["w_in"]), pos, name="own_slot_w_in")
    half = slots["w_in"].shape[2] // 2
    lo, hi = (0, half), (half, half)
    (flight,), first_started = _ag_start([slots["w_in"]], [[0]], name="ag_start_0_lo", cols=lo)
    in_flight["w_in"] = flight
    for grp in _GATHER_GROUPS[1:]:
        for n in grp:
            slots[n] = _own_slot(_shard(n, w[n]), pos, name="own_slot_" + n, after=first_started)

    def first_weight(after, part):
        send_sems, recv_sems, lands = in_flight["w_in"]
        if part == 0:
            lands, token = _ag_wait(send_sems, recv_sems, lands, after, name="ag_wait_0_lo", cols=lo)
            (flight,), started = _ag_start(lands, [[0]], name="ag_start_0_hi", after=token, cols=hi)
            gathered = _ag_forward(flight[2], name="ag_forward_0_lo", after=started, cols=lo)
            in_flight["w_in"] = (flight[0], flight[1], gathered)
        else:
            lands, token = _ag_wait(send_sems, recv_sems, lands, after, name="ag_wait_0_hi", cols=hi)
            started = start_gather(_START_AFTER_WAIT[0], token)
            gathered = _ag_forward(lands, name="ag_forward_0_hi", after=started, cols=hi)
        return _whole("w_in", gathered[0])

    to_sibling, reducing = [], {}

    def emit(names, grads):
        g8 = [_blocks(n, g) for n, g in zip(names, grads)]
        send_sems, recv_sems, g8, lands, token = _copies_start(_sibling_copies, 4, g8, name="rs_sibling_start_" + names[0])
        to_sibling.append((names, send_sems, recv_sems, g8, lands))
        return token

    def flush(after):
        names, send_sems, recv_sems, g8, lands = to_sibling.pop()
        g8, from_sibling = _copies_wait(_sibling_copies, send_sems, recv_sems, g8, lands, after,
                                        name="rs_sibling_wait_" + names[0])
        sums4 = [_chip_sums(g, s, pos, name="chip_sums_" + n) for n, g, s in zip(names, g8, from_sibling)]
        send_sems, recv_sems, sums4, lands, token = _copies_start(_chip_copies, 3, sums4, name="rs_chips_start_" + names[0])
        reducing[names] = (g8, from_sibling, send_sems, recv_sems, sums4, lands)
        return token

    loss, grad_x, small_grads_local = _local_step(
        x[0], loss_target[0], weight, emit, flush, norm_mix, sgu_v_gain, sgu_w_s[0], sgu_b_s[0], attn_sink, rel_bias,
        norm_ffn, norm_final[None], early=[slots[n] for grp in _GATHER_GROUPS[1:] for n in grp])

    out_g, out_d, out_m, out_v = {}, {}, {}, {}
    small_like = [w[n] for n in _SMALL]
    small_w = _pack(small_like)
    packed = _pack([small_grads_local[n] for n in _SMALL] + [loss[0, 0]])
    after = grad_x
    for gi, (names, (g8, from_sibling, send_sems, recv_sems, sums4, lands)) in enumerate(reducing.items()):
        if gi == len(reducing) - 1:
            summed = _small_all_reduce(packed, after, name="small_all_reduce")
            after = summed
        _, from_chips = _copies_wait(_chip_copies, send_sems, recv_sems, sums4, lands, after,
                                     name="rs_chips_wait_" + names[0])
        for i, n in enumerate(names):
            g, d, m_, v_ = _adamw_shard(_shard(n, w[n]), _shard(n, m[n]), _shard(n, v[n]), g8[i], from_sibling[i],
                                        from_chips[i], pos, name="adamw_" + n)
            out_g[n], out_d[n], out_m[n], out_v[n] = (_unshard(n, o) for o in (g, d, m_, v_))
            after = d
    *small_grads, loss_sum = _unpack(summed, small_like + [jax.ShapeDtypeStruct((), F32)])
    d_s, m_s, v_s = _adamw_small(small_w, summed[:small_w.shape[0]], _pack([m[n] for n in _SMALL]),
                                 _pack([v[n] for n in _SMALL]), name="adamw_small")
    for n, g, d, m_, v_ in zip(_SMALL, small_grads, _unpack(d_s, small_like), _unpack(m_s, small_like), _unpack(v_s, small_like)):
        out_g[n], out_d[n], out_m[n], out_v[n] = g, d, m_, v_

    return (loss_sum, grad_x[None], *[out_g[n] for n in _ORDER], *[out_d[n] for n in _ORDER],
            *[out_m[n] for n in _ORDER], *[out_v[n] for n in _ORDER])
```

```python
import functools
import math

import numpy as np
import jax
import jax.numpy as jnp
from jax import lax
from jax.experimental import pallas as pl
from jax.experimental.pallas import tpu as pltpu

F32 = jnp.float32
BF16 = jnp.bfloat16

EPS = 1e-6
NEG = -1e30
HEAD_DIM = 128
BLOCK = 128
N_KV_HEADS = 2
KV_WIDTH = N_KV_HEADS * HEAD_DIM
REL_BUCKETS = 32
REL_MAX_DIST = 128

ADAM_LR = 0.001
ADAM_B1 = 0.9
ADAM_B2 = 0.999
ADAM_EPS = 1e-08
ADAM_WD = 0.01
ADAM_STEP = 10

N_DEV = 8
LANES = 128
VMEM_LIMIT = 56 * 1024 * 1024
MESH = pl.DeviceIdType.MESH


def _cparams(*sem):
    return pltpu.CompilerParams(dimension_semantics=sem, vmem_limit_bytes=VMEM_LIMIT)


def _div(n, target, mult=LANES):
    best = None
    for d in range(mult, min(n, target) + 1, mult):
        if n % d == 0:
            best = d
    assert best is not None, (n, target, mult)
    return best


_ANY = pl.BlockSpec(memory_space=pl.ANY)


def _ordered_after(body, n_inputs, in_specs, args, after):
    if after is None:
        return body, in_specs, args
    extra = tuple(after) if isinstance(after, (tuple, list)) else (after,)

    def wrapped(*refs):
        return body(*refs[:n_inputs], *refs[n_inputs + len(extra):])

    return wrapped, list(in_specs) + [_ANY] * len(extra), tuple(args) + extra


def _bucket_map():
    nb = REL_BUCKETS // 2
    qi = np.arange(BLOCK)[:, None]
    kj = np.arange(3 * BLOCK)[None, :]
    rel = kj - BLOCK - qi
    ret = np.where(rel > 0, nb, 0)
    n = np.abs(rel)
    max_exact = nb // 2
    nf = np.maximum(n, 1).astype(np.float32)
    large = max_exact + (np.log(nf / np.float32(max_exact)) / np.float32(math.log(REL_MAX_DIST / max_exact))
                         * np.float32(nb - max_exact)).astype(np.int32)
    large = np.minimum(large, nb - 1)
    return (ret + np.where(n < max_exact, n, large)).astype(np.int32)


_GELU_C = math.sqrt(2.0 / math.pi)
_GELU_A = 0.044715


def _gelu(x):
    t = jnp.tanh(_GELU_C * (x + _GELU_A * (x * x * x)))
    return 0.5 * x * (1.0 + t)


def _gelu_and_grad(x):
    x2 = x * x
    t = jnp.tanh(_GELU_C * (x + _GELU_A * (x2 * x)))
    g = 0.5 * x * (1.0 + t)
    dg = 0.5 * (1.0 + t) + 0.5 * x * (1.0 - t * t) * (_GELU_C * (1.0 + 3.0 * _GELU_A * x2))
    return g, dg


def _sigmoid(x):
    return 1.0 / (1.0 + jnp.exp(-x))


def _mm(a, b, *, name, ta=False, tb=False, add=None, out_dtype=F32, bm=1024, bn=1024, bk=None, after=None,
        row_blocks=None, into=None):
    if ta:
        K, M = a.shape
    else:
        M, K = a.shape
    N = b.shape[0] if tb else b.shape[1]
    assert (b.shape[1] if tb else b.shape[0]) == K
    bm = _div(M, bm)
    bn = _div(N, bn)
    bk = K if bk is None else _div(K, bk)
    nk = K // bk
    i0, ni = (0, M // bm) if row_blocks is None else row_blocks
    a_spec = (pl.BlockSpec((bk, bm), lambda i, j, k: (k, i + i0)) if ta
              else pl.BlockSpec((bm, bk), lambda i, j, k: (i + i0, k)))
    b_spec = pl.BlockSpec((bn, bk), lambda i, j, k: (j, k)) if tb else pl.BlockSpec((bk, bn), lambda i, j, k: (k, j))
    o_spec = pl.BlockSpec((bm, bn), lambda i, j, k: (i + i0, j))
    dims = (((0 if ta else 1,), (1 if tb else 0,)), ((), ()))
    has_add = add is not None

    def body(*refs):
        if has_add:
            a_ref, b_ref, add_ref, o_ref, *scratch = refs
        else:
            a_ref, b_ref, o_ref, *scratch = refs
            add_ref = None
        p = lax.dot_general(a_ref[...].astype(BF16), b_ref[...].astype(BF16), dims, preferred_element_type=F32)
        if nk == 1:
            if has_add:
                p = p + add_ref[...]
            o_ref[...] = p.astype(out_dtype)
        else:
            acc = scratch[0]
            k = pl.program_id(2)

            @pl.when(k == 0)
            def _():
                acc[...] = p

            @pl.when(k > 0)
            def _():
                acc[...] += p

            @pl.when(k == nk - 1)
            def _():
                r = acc[...]
                if has_add:
                    r = r + add_ref[...]
                o_ref[...] = r.astype(out_dtype)

    in_specs = [a_spec, b_spec] + ([o_spec] if has_add else [])
    args = (a, b) + ((add,) if has_add else ())
    aliases = {}
    if into is not None:
        body, in_specs, args = _ordered_after(body, len(args), in_specs, args, into)
        aliases = {len(args) - 1: 0}
    body, in_specs, args = _ordered_after(body, len(args), in_specs, args, after)
    return pl.pallas_call(
        body, name=name, grid=(ni, N // bn, nk),
        in_specs=in_specs, out_specs=o_spec,
        out_shape=jax.ShapeDtypeStruct((M, N), out_dtype),
        input_output_aliases=aliases,
        scratch_shapes=[pltpu.VMEM((bm, bn), F32)] if nk > 1 else [],
        compiler_params=_cparams("parallel", "parallel", "arbitrary"),
    )(*args)


def _mm_resid_rms(a, b, resid, gain, *, name, bm=512):
    M, K = a.shape
    N = b.shape[1]
    bm = _div(M, bm)

    def body(a_ref, b_ref, r_ref, g_ref, x_ref, h_ref):
        x = r_ref[...] + jnp.dot(a_ref[...], b_ref[...], preferred_element_type=F32)
        x_ref[...] = x
        r = lax.rsqrt(jnp.mean(x * x, axis=-1, keepdims=True) + EPS)
        h_ref[...] = ((x * r) * g_ref[...]).astype(BF16)

    row = pl.BlockSpec((bm, N), lambda i: (i, 0))
    return pl.pallas_call(
        body, name=name, grid=(M // bm,),
        in_specs=[pl.BlockSpec((bm, K), lambda i: (i, 0)), pl.BlockSpec((K, N), lambda i: (0, 0)), row,
                  pl.BlockSpec((1, N), lambda i: (0, 0))],
        out_specs=[row, row], out_shape=[jax.ShapeDtypeStruct((M, N), F32), jax.ShapeDtypeStruct((M, N), BF16)],
        compiler_params=_cparams("parallel"),
    )(a, b, resid, gain)


def _mm_sum2(a1, b1, a2, b2, *, name, bm=1024, bn=512, bk=2816, after=None):
    M, K = a1.shape
    N = b1.shape[1]
    bm, bn, bk = _div(M, bm), _div(N, bn), _div(K, bk)
    nk = K // bk

    def body(a1_ref, b1_ref, a2_ref, b2_ref, o_ref, acc):
        p = (jnp.dot(a1_ref[...], b1_ref[...], preferred_element_type=F32)
             + jnp.dot(a2_ref[...], b2_ref[...], preferred_element_type=F32))
        k = pl.program_id(2)

        @pl.when(k == 0)
        def _():
            acc[...] = p

        @pl.when(k > 0)
        def _():
            acc[...] += p

        @pl.when(k == nk - 1)
        def _():
            o_ref[...] = acc[...]

    a_spec = pl.BlockSpec((bm, bk), lambda i, j, k: (i, k))
    b_spec = pl.BlockSpec((bk, bn), lambda i, j, k: (k, j))
    body, in_specs, args = _ordered_after(body, 4, [a_spec, b_spec, a_spec, b_spec], (a1, b1, a2, b2), after)
    return pl.pallas_call(
        body, name=name, grid=(M // bm, N // bn, nk),
        in_specs=in_specs, out_specs=pl.BlockSpec((bm, bn), lambda i, j, k: (i, j)),
        out_shape=jax.ShapeDtypeStruct((M, N), F32),
        scratch_shapes=[pltpu.VMEM((bm, bn), F32)],
        compiler_params=_cparams("parallel", "parallel", "arbitrary"),
    )(*args)


def _blocks_per_tile(c):
    nb = 1
    while (nb * c) % LANES or (nb * c < 1024 and nb < N_DEV):
        nb *= 2
    assert nb <= N_DEV and (nb * c) % LANES == 0, c
    return nb


def _mm_w8(a, w8, *, name, bm=1024, out_dtype=F32):
    M, K = a.shape
    _, _, c = w8.shape
    nb = _blocks_per_tile(c)
    bm = _div(M, bm)

    def body(a_ref, w_ref, o_ref):
        a_ = a_ref[...]
        for t in range(nb):
            o_ref[:, t * c:(t + 1) * c] = jnp.dot(a_, w_ref[t], preferred_element_type=F32).astype(out_dtype)

    return pl.pallas_call(
        body, name=name, grid=(M // bm, N_DEV // nb),
        in_specs=[pl.BlockSpec((bm, K), lambda i, j: (i, 0)), pl.BlockSpec((nb, K, c), lambda i, j: (j, 0, 0))],
        out_specs=pl.BlockSpec((bm, nb * c), lambda i, j: (i, j)),
        out_shape=jax.ShapeDtypeStruct((M, N_DEV * c), out_dtype),
        compiler_params=_cparams("parallel", "parallel"),
    )(a, w8)


def _mm_w8t(dy, w8, *, name, add=None, out_dtype=F32, bm=1024, bn=1024, after=None, lead=None):
    M = dy.shape[-2]
    _, K, c = w8.shape
    nb = _blocks_per_tile(c)
    nk = N_DEV // nb
    bm, bn = _div(M, bm), _div(K, bn)
    has_add = add is not None
    dims = (((1,), (1,)), ((), ()))

    def body(*refs):
        if has_add:
            dy_ref, w_ref, add_ref, o_ref, acc = refs
        else:
            dy_ref, w_ref, o_ref, acc = refs
        p = lax.dot_general(dy_ref[:, 0:c], w_ref[0], dims, preferred_element_type=F32)
        for t in range(1, nb):
            p = p + lax.dot_general(dy_ref[:, t * c:(t + 1) * c], w_ref[t], dims, preferred_element_type=F32)
        k = pl.program_id(2)

        @pl.when(k == 0)
        def _():
            acc[...] = p

        @pl.when(k > 0)
        def _():
            acc[...] += p

        @pl.when(k == nk - 1)
        def _():
            r = acc[...]
            if has_add:
                r = r + add_ref[...]
            o_ref[...] = r.astype(out_dtype)

    o_spec = pl.BlockSpec((bm, bn), lambda i, j, k: (i, j))
    dy_spec = (pl.BlockSpec((bm, nb * c), lambda i, j, k: (i, k)) if lead is None
               else pl.BlockSpec((None, bm, nb * c), lambda i, j, k: (lead, i, k)))
    in_specs = [dy_spec, pl.BlockSpec((nb, bn, c), lambda i, j, k: (k, j, 0))]
    in_specs += [o_spec] if has_add else []
    args = (dy, w8) + ((add,) if has_add else ())
    body, in_specs, args = _ordered_after(body, len(args), in_specs, args, after)
    return pl.pallas_call(
        body, name=name, grid=(M // bm, K // bn, nk),
        in_specs=in_specs, out_specs=o_spec,
        out_shape=jax.ShapeDtypeStruct((M, K), out_dtype),
        scratch_shapes=[pltpu.VMEM((bm, bn), F32)],
        compiler_params=_cparams("parallel", "parallel", "arbitrary"),
    )(*args)


def _mm_gw8(x, dy, c, *, name, bk=1024, lead=None):
    T, K = x.shape
    nb = _blocks_per_tile(c)
    bk = _div(K, bk)
    dims = (((0,), (0,)), ((), ()))

    def body(x_ref, dy_ref, o_ref):
        x_ = x_ref[...]
        for t in range(nb):
            o_ref[t] = lax.dot_general(x_, dy_ref[:, t * c:(t + 1) * c], dims, preferred_element_type=F32).astype(BF16)

    dy_spec = (pl.BlockSpec((T, nb * c), lambda i, j: (0, j)) if lead is None
               else pl.BlockSpec((None, T, nb * c), lambda i, j: (lead, 0, j)))
    return pl.pallas_call(
        body, name=name, grid=(K // bk, N_DEV // nb),
        in_specs=[pl.BlockSpec((T, bk), lambda i, j: (0, i)), dy_spec],
        out_specs=pl.BlockSpec((nb, bk, c), lambda i, j: (j, i, 0)),
        out_shape=jax.ShapeDtypeStruct((N_DEV, K, c), BF16),
        compiler_params=_cparams("parallel", "parallel"),
    )(x, dy)


def _rms_fwd(x, g, *, name, after=None):
    T, D = x.shape
    tm = _div(T, 256, 8)

    def body(x_ref, g_ref, h_ref):
        xf = x_ref[...]
        r = lax.rsqrt(jnp.mean(xf * xf, axis=-1, keepdims=True) + EPS)
        h_ref[...] = ((xf * r) * g_ref[...]).astype(BF16)

    in_specs = [pl.BlockSpec((tm, D), lambda i: (i, 0)), pl.BlockSpec((1, D), lambda i: (0, 0))]
    body, in_specs, args = _ordered_after(body, 2, in_specs, (x, g), after)
    return pl.pallas_call(
        body, name=name, grid=(T // tm,),
        in_specs=in_specs,
        out_specs=pl.BlockSpec((tm, D), lambda i: (i, 0)),
        out_shape=jax.ShapeDtypeStruct((T, D), BF16),
        compiler_params=_cparams("parallel"),
    )(*args)


def _rms_bwd(x, g, dh, dres, *, name, want_bf16, after=None):
    T, D = x.shape
    tm = _div(T, 256, 8)

    def body(x_ref, g_ref, dh_ref, dres_ref, dx_ref, *rest):
        if want_bf16:
            dxb_ref, dg_ref = rest
        else:
            (dg_ref,) = rest
        xf = x_ref[...]
        r = lax.rsqrt(jnp.mean(xf * xf, axis=-1, keepdims=True) + EPS)
        xhat = xf * r
        dh_ = dh_ref[...]
        dy = dh_ * g_ref[...]
        dx = dres_ref[...].astype(F32) + r * (dy - xhat * jnp.mean(dy * xhat, axis=-1, keepdims=True))
        dx_ref[...] = dx
        if want_bf16:
            dxb_ref[...] = dx.astype(BF16)
        part = jnp.sum(dh_ * xhat, axis=0, keepdims=True)

        @pl.when(pl.program_id(0) == 0)
        def _():
            dg_ref[...] = part

        @pl.when(pl.program_id(0) > 0)
        def _():
            dg_ref[...] += part

    row = pl.BlockSpec((tm, D), lambda i: (i, 0))
    vec = pl.BlockSpec((1, D), lambda i: (0, 0))
    out_specs = [row] + ([row] if want_bf16 else []) + [vec]
    out_shape = ([jax.ShapeDtypeStruct((T, D), F32)] + ([jax.ShapeDtypeStruct((T, D), BF16)] if want_bf16 else [])
                 + [jax.ShapeDtypeStruct((1, D), F32)])
    body, in_specs, args = _ordered_after(body, 4, [row, vec, row, row], (x, g, dh, dres), after)
    return pl.pallas_call(
        body, name=name, grid=(T // tm,),
        in_specs=in_specs, out_specs=out_specs, out_shape=out_shape,
        compiler_params=_cparams("arbitrary"),
    )(*args)


def _loss_head(x, g, target, *, name):
    T, D = x.shape
    tm = _div(T, 256, 16)

    def body(x_ref, g_ref, t_ref, loss_ref, dxb_ref, dg_ref):
        xf = x_ref[...]
        r = lax.rsqrt(jnp.mean(xf * xf, axis=-1, keepdims=True) + EPS)
        xhat = xf * r
        gain = g_ref[...]
        err = xhat * gain - t_ref[...]
        lpart = 0.5 * jnp.sum(jnp.mean(err * err, axis=-1, keepdims=True), axis=0, keepdims=True)
        dh_ = err * (1.0 / D)
        dy = dh_ * gain
        dx = r * (dy - xhat * jnp.mean(dy * xhat, axis=-1, keepdims=True))
        dxb_ref[...] = dx.astype(BF16)
        part = jnp.sum(dh_ * xhat, axis=0, keepdims=True)

        @pl.when(pl.program_id(0) == 0)
        def _():
            dg_ref[...] = part
            loss_ref[...] = jnp.broadcast_to(lpart, loss_ref.shape)

        @pl.when(pl.program_id(0) > 0)
        def _():
            dg_ref[...] += part
            loss_ref[...] += jnp.broadcast_to(lpart, loss_ref.shape)

    row = pl.BlockSpec((tm, D), lambda i: (i, 0))
    vec = pl.BlockSpec((1, D), lambda i: (0, 0))
    return pl.pallas_call(
        body, name=name, grid=(T // tm,),
        in_specs=[row, vec, row],
        out_specs=[pl.BlockSpec((8, LANES), lambda i: (0, 0)), row, vec],
        out_shape=[jax.ShapeDtypeStruct((8, LANES), F32), jax.ShapeDtypeStruct((T, D), BF16), jax.ShapeDtypeStruct((1, D), F32)],
        compiler_params=_cparams("arbitrary"),
    )(x, g, target)


def _gate_cols(D):
    off_a = 3 * D // 2 + 2 * KV_WIDTH
    off_b = off_a + D
    cw = math.gcd(math.gcd(off_a, off_b), math.gcd(D, 512))
    return cw, off_a // cw, off_b // cw


def _merge_fwd(z, ya, yb, *, name):
    T, D = ya.shape
    cw, ba, bb = _gate_cols(D)
    tm = _div(T, 512, 8)

    def body(ga_ref, gb_ref, ya_ref, yb_ref, m_ref):
        m_ref[...] = (_sigmoid(ga_ref[...].astype(F32)) * ya_ref[...]
                      + _sigmoid(gb_ref[...].astype(F32)) * yb_ref[...]).astype(BF16)

    blk = pl.BlockSpec((tm, cw), lambda i, j: (i, j))
    return pl.pallas_call(
        body, name=name, grid=(T // tm, D // cw),
        in_specs=[pl.BlockSpec((tm, cw), lambda i, j: (i, ba + j)), pl.BlockSpec((tm, cw), lambda i, j: (i, bb + j)), blk, blk],
        out_specs=blk, out_shape=jax.ShapeDtypeStruct((T, D), BF16),
        compiler_params=_cparams("parallel", "parallel"),
    )(z, z, ya, yb)


def _merge_bwd(z, ya, yb, dm, *, name, after=None):
    T, D = ya.shape
    cw, ba, bb = _gate_cols(D)
    nj = D // cw
    assert bb == ba + nj
    tm = _div(T, 512, 8)

    def body(g_ref, ya_ref, yb_ref, dm_ref, dy_ref, dz_ref):
        sig = _sigmoid(g_ref[...].astype(F32))
        dm_ = dm_ref[...]
        y = jnp.where(pl.program_id(1) == 0, ya_ref[...], yb_ref[...])
        dy_ref[...] = (dm_ * sig).astype(BF16)
        dz_ref[...] = (dm_ * y * (sig * (1.0 - sig))).astype(BF16)

    in_specs = [pl.BlockSpec((tm, cw), lambda i, s, j: (i, ba + s * nj + j)),
                pl.BlockSpec((tm, cw), lambda i, s, j: (i, j * (1 - s))),
                pl.BlockSpec((tm, cw), lambda i, s, j: (i, j * s)),
                pl.BlockSpec((tm, cw), lambda i, s, j: (i, j))]
    body, in_specs, args = _ordered_after(body, 4, in_specs, (z, ya, yb, dm), after)
    return pl.pallas_call(
        body, name=name, grid=(T // tm, 2, nj),
        in_specs=in_specs,
        out_specs=[pl.BlockSpec((None, tm, cw), lambda i, s, j: (s, i, j)),
                   pl.BlockSpec((tm, cw), lambda i, s, j: (i, ba + s * nj + j))],
        out_shape=[jax.ShapeDtypeStruct((2, T, D), BF16), jax.ShapeDtypeStruct(z.shape, BF16)],
        compiler_params=_cparams("parallel", "arbitrary", "arbitrary"),
    )(*args)


def _swiglu_mm_fwd(h, wu_t, gate, *, name, bm=1024, bn=512):
    T, D = h.shape
    F = wu_t.shape[0]
    bm, bn = _div(T, bm), _div(F, bn)

    rc = _div(bm, 256, 16)

    def body(h_ref, wu_ref, gin_ref, g_ref, u_ref, act_ref):
        w = wu_ref[...]
        for r in range(0, bm, rc):
            rows = slice(r, r + rc)
            u = lax.dot_general(h_ref[rows, :], w, (((1,), (1,)), ((), ())), preferred_element_type=F32)
            g = gin_ref[rows, :]
            g_ref[rows, :] = g.astype(BF16)
            u_ref[rows, :] = u.astype(BF16)
            act_ref[rows, :] = (g * _sigmoid(g) * u).astype(BF16)

    o_spec = pl.BlockSpec((bm, bn), lambda i, j: (i, j))
    return pl.pallas_call(
        body, name=name, grid=(T // bm, F // bn),
        in_specs=[pl.BlockSpec((bm, D), lambda i, j: (i, 0)), pl.BlockSpec((bn, D), lambda i, j: (j, 0)), o_spec],
        out_specs=[o_spec] * 3, out_shape=[jax.ShapeDtypeStruct((T, F), BF16)] * 3,
        compiler_params=_cparams("parallel", "parallel"),
    )(h, wu_t, gate)


def _swiglu_mm_bwd(dx, w_down, gate, up, *, name, bm=2048, bn=512, after=None):
    T, D = dx.shape
    F = w_down.shape[0]
    bm, bn = _div(T, bm), _div(F, bn)
    dims = (((1,), (1,)), ((), ()))

    rc = _div(bm, 256, 16)

    def body(dx_ref, w_ref, g_ref, u_ref, dg_ref, du_ref):
        w = w_ref[...]
        for r in range(0, bm, rc):
            rows = slice(r, r + rc)
            d = lax.dot_general(dx_ref[rows, :], w, dims, preferred_element_type=F32)
            g = g_ref[rows, :].astype(F32)
            s = _sigmoid(g)
            silu = g * s
            dg_ref[rows, :] = (d * u_ref[rows, :].astype(F32) * (s + silu * (1.0 - s))).astype(BF16)
            du_ref[rows, :] = (d * silu).astype(BF16)

    o_spec = pl.BlockSpec((bm, bn), lambda i, j: (i, j))
    in_specs = [pl.BlockSpec((bm, D), lambda i, j: (i, 0)), pl.BlockSpec((bn, D), lambda i, j: (j, 0)), o_spec, o_spec]
    body, in_specs, args = _ordered_after(body, 4, in_specs, (dx, w_down, gate, up), after)
    out = jax.ShapeDtypeStruct((T, F), BF16)
    return pl.pallas_call(
        body, name=name, grid=(T // bm, F // bn), in_specs=in_specs, out_specs=[o_spec, o_spec], out_shape=[out, out],
        compiler_params=_cparams("parallel", "parallel"),
    )(*args)


def _sgu_fwd(z, gain, ws_b, bs_t, *, name):
    T = z.shape[0]
    SW = gain.shape[1]
    G = SW // BLOCK

    def body(zu_ref, zv_ref, gain_ref, ws_ref, bs_ref, a_ref):
        u = _gelu(zu_ref[...].astype(F32))
        vg = _gelu(zv_ref[...].astype(F32))
        r = lax.rsqrt(jnp.mean(vg * vg, axis=-1, keepdims=True) + EPS)
        vn = ((vg * r) * gain_ref[...]).astype(BF16)
        for g in range(G):
            sl = slice(g * BLOCK, (g + 1) * BLOCK)
            mixed = jnp.dot(ws_ref[g], vn[:, sl], preferred_element_type=F32) + bs_ref[:, g:g + 1]
            a_ref[:, sl] = (u[:, sl] * mixed).astype(BF16)

    return pl.pallas_call(
        body, name=name, grid=(T // BLOCK,),
        in_specs=[pl.BlockSpec((BLOCK, SW), lambda c: (c, 0)), pl.BlockSpec((BLOCK, SW), lambda c: (c, 1)),
                  pl.BlockSpec((1, SW), lambda c: (0, 0)), pl.BlockSpec((G, BLOCK, BLOCK), lambda c: (0, 0, 0)),
                  pl.BlockSpec((BLOCK, G), lambda c: (0, 0))],
        out_specs=pl.BlockSpec((BLOCK, SW), lambda c: (c, 0)),
        out_shape=jax.ShapeDtypeStruct((T, SW), BF16),
        compiler_params=_cparams("parallel"),
    )(z, z, gain, ws_b, bs_t)


def _sgu_bwd(z, gain, ws_b, bs_t, da, dz, *, name):
    T = z.shape[0]
    SW = gain.shape[1]
    G = SW // BLOCK

    def body(zu_ref, zv_ref, gain_ref, ws_ref, bs_ref, da_ref, dz_in_ref, dz_ref, dws_ref, dbs_ref, dgain_ref, dvn_ref):
        first = pl.program_id(0) == 0

        @pl.when(first)
        def _():
            dws_ref[...] = jnp.zeros_like(dws_ref)
            dbs_ref[...] = jnp.zeros_like(dbs_ref)
            dgain_ref[...] = jnp.zeros_like(dgain_ref)

        u, du = _gelu_and_grad(zu_ref[...].astype(F32))
        vg, dvg = _gelu_and_grad(zv_ref[...].astype(F32))
        r = lax.rsqrt(jnp.mean(vg * vg, axis=-1, keepdims=True) + EPS)
        xhat = vg * r
        gain_ = gain_ref[...]
        vn = (xhat * gain_).astype(BF16)
        da_ = da_ref[...]
        for g in range(G):
            sl = slice(g * BLOCK, (g + 1) * BLOCK)
            w = ws_ref[g]
            mixed = jnp.dot(w, vn[:, sl], preferred_element_type=F32) + bs_ref[:, g:g + 1]
            dmix = da_[:, sl] * u[:, sl]
            dz_ref[:, sl] = (da_[:, sl] * mixed * du[:, sl]).astype(BF16)
            dmb = dmix.astype(BF16)
            dws_ref[g] += lax.dot_general(dmb, vn[:, sl], (((1,), (1,)), ((), ())), preferred_element_type=F32)
            dbs_ref[:, g:g + 1] += jnp.sum(dmix, axis=-1, keepdims=True)
            dvn_ref[:, sl] = lax.dot_general(w, dmb, (((0,), (0,)), ((), ())), preferred_element_type=F32)
        dvn = dvn_ref[...]
        dgain_ref[...] += jnp.sum(dvn * xhat, axis=0, keepdims=True)
        dy = dvn * gain_
        dv_ = r * (dy - xhat * jnp.mean(dy * xhat, axis=-1, keepdims=True))
        dz_ref[:, SW:] = (dv_ * dvg).astype(BF16)

    row = pl.BlockSpec((BLOCK, SW), lambda c: (c, 0))
    return pl.pallas_call(
        body, name=name, grid=(T // BLOCK,),
        in_specs=[row, pl.BlockSpec((BLOCK, SW), lambda c: (c, 1)),
                  pl.BlockSpec((1, SW), lambda c: (0, 0)), pl.BlockSpec((G, BLOCK, BLOCK), lambda c: (0, 0, 0)),
                  pl.BlockSpec((BLOCK, G), lambda c: (0, 0)), row, _ANY],
        out_specs=[pl.BlockSpec((BLOCK, 2 * SW), lambda c: (c, 0)), pl.BlockSpec((G, BLOCK, BLOCK), lambda c: (0, 0, 0)),
                   pl.BlockSpec((BLOCK, G), lambda c: (0, 0)), pl.BlockSpec((1, SW), lambda c: (0, 0))],
        out_shape=[jax.ShapeDtypeStruct(dz.shape, dz.dtype),
                   jax.ShapeDtypeStruct((G, BLOCK, BLOCK), F32), jax.ShapeDtypeStruct((BLOCK, G), F32),
                   jax.ShapeDtypeStruct((1, SW), F32)],
        input_output_aliases={6: 0},
        scratch_shapes=[pltpu.VMEM((BLOCK, SW), F32)],
        compiler_params=_cparams("arbitrary"),
    )(z, z, gain, ws_b, bs_t, da, dz)


def _bias_table(rel_bias, bmap, *, name):
    H = rel_bias.shape[1]

    def body(rb_ref, bmap_ref, o_ref):
        bm_ = bmap_ref[...]
        for h in range(H):
            acc = jnp.zeros(bm_.shape, F32)
            for b in range(REL_BUCKETS):
                acc = jnp.where(bm_ == b, rb_ref[b, h], acc)
            o_ref[h] = acc

    return pl.pallas_call(
        body, name=name,
        in_specs=[pl.BlockSpec(memory_space=pltpu.SMEM), pl.BlockSpec(memory_space=pltpu.VMEM)],
        out_specs=pl.BlockSpec(memory_space=pltpu.VMEM),
        out_shape=jax.ShapeDtypeStruct((H, BLOCK, 3 * BLOCK), F32),
    )(rel_bias, bmap)


def _attn_probs(q_ref, kb, bias_ref, sink_ref, s_ref, n, T, group):
    H = s_ref.shape[0]
    for h in range(H):
        kv = h // group
        qh = q_ref[:, h * HEAD_DIM:(h + 1) * HEAD_DIM].astype(BF16)
        s_ref[h] = lax.dot_general(qh, kb[:, kv * HEAD_DIM:(kv + 1) * HEAD_DIM], (((1,), (1,)), ((), ())),
                                   preferred_element_type=F32)
    row = lax.broadcasted_iota(jnp.int32, (BLOCK, 3 * BLOCK), 0)
    col = lax.broadcasted_iota(jnp.int32, (BLOCK, 3 * BLOCK), 1)
    key_pos = n * BLOCK + col - BLOCK
    valid = (jnp.abs(col - BLOCK - row) <= BLOCK) & (key_pos >= 0) & (key_pos < T)
    s = s_ref[...] * (HEAD_DIM ** -0.5) + bias_ref[...]
    s = jnp.where(valid[None], s, NEG)
    sink = sink_ref[...]
    m = jnp.maximum(jnp.max(s, axis=-1, keepdims=True), sink)
    e = jnp.exp(s - m)
    es = jnp.exp(sink - m)
    inv = 1.0 / (jnp.sum(e, axis=-1, keepdims=True) + es)
    return e * inv, es * inv


def _attn_fwd(z, kpad, vpad, bias, sink, *, name):
    T = z.shape[0]
    H = bias.shape[0]
    AW = H * HEAD_DIM
    group = H // N_KV_HEADS

    def body(q_ref, k_ref, v_ref, bias_ref, sink_ref, o_ref, s_ref, p_ref):
        n = pl.program_id(0)
        start = pl.multiple_of(n * BLOCK, BLOCK)
        kb = k_ref[pl.ds(start, 3 * BLOCK), :]
        vb = v_ref[pl.ds(start, 3 * BLOCK), :]
        p, _ = _attn_probs(q_ref, kb, bias_ref, sink_ref, s_ref, n, T, group)
        p_ref[...] = p.astype(BF16)
        for h in range(H):
            kv = h // group
            o = jnp.dot(p_ref[h], vb[:, kv * HEAD_DIM:(kv + 1) * HEAD_DIM], preferred_element_type=F32)
            o_ref[:, h * HEAD_DIM:(h + 1) * HEAD_DIM] = o.astype(BF16)

    full_kv = pl.BlockSpec((T + 2 * BLOCK, KV_WIDTH), lambda n: (0, 0))
    return pl.pallas_call(
        body, name=name, grid=(T // BLOCK,),
        in_specs=[pl.BlockSpec((BLOCK, AW), lambda n: (n, 2)), full_kv, full_kv,
                  pl.BlockSpec((H, BLOCK, 3 * BLOCK), lambda n: (0, 0, 0)), pl.BlockSpec((H, 1, 1), lambda n: (0, 0, 0))],
        out_specs=pl.BlockSpec((BLOCK, AW), lambda n: (n, 0)),
        out_shape=jax.ShapeDtypeStruct((T, AW), BF16),
        scratch_shapes=[pltpu.VMEM((H, BLOCK, 3 * BLOCK), F32), pltpu.VMEM((H, BLOCK, 3 * BLOCK), BF16)],
        compiler_params=_cparams("parallel"),
    )(z, kpad, vpad, bias, sink)


def _attn_bwd(z, kpad, vpad, bias, sink, do, dz, *, name):
    T = z.shape[0]
    H = bias.shape[0]
    AW = H * HEAD_DIM
    group = H // N_KV_HEADS
    scale = HEAD_DIM ** -0.5

    def body(q_ref, k_ref, v_ref, bias_ref, sink_ref, do_ref, dz_in_ref, dq_ref, dk_ref, dv_ref, dbias_ref, dsink_ref,
             s_ref, dp_ref, p_ref, ds_ref):
        n = pl.program_id(0)

        @pl.when(n == 0)
        def _():
            dk_ref[...] = jnp.zeros_like(dk_ref)
            dv_ref[...] = jnp.zeros_like(dv_ref)
            dbias_ref[...] = jnp.zeros_like(dbias_ref)
            dsink_ref[...] = jnp.zeros_like(dsink_ref)

        start = pl.multiple_of(n * BLOCK, BLOCK)
        kb = k_ref[pl.ds(start, 3 * BLOCK), :]
        vb = v_ref[pl.ds(start, 3 * BLOCK), :]
        p, p_sink = _attn_probs(q_ref, kb, bias_ref, sink_ref, s_ref, n, T, group)
        s_ref[...] = p
        p_ref[...] = p.astype(BF16)
        for h in range(H):
            kv = h // group
            dp_ref[h] = lax.dot_general(do_ref[:, h * HEAD_DIM:(h + 1) * HEAD_DIM], vb[:, kv * HEAD_DIM:(kv + 1) * HEAD_DIM],
                                        (((1,), (1,)), ((), ())), preferred_element_type=F32)
        p = s_ref[...]
        dp = dp_ref[...]
        delta = jnp.sum(p * dp, axis=-1, keepdims=True)
        ds = p * (dp - delta)
        dbias_ref[...] += ds
        dsink_ref[...] += -(p_sink * delta)
        ds_ref[...] = ds.astype(BF16)
        for kv in range(N_KV_HEADS):
            ksl = slice(kv * HEAD_DIM, (kv + 1) * HEAD_DIM)
            dk_acc = jnp.zeros((3 * BLOCK, HEAD_DIM), F32)
            dv_acc = jnp.zeros((3 * BLOCK, HEAD_DIM), F32)
            for gi in range(group):
                h = kv * group + gi
                hsl = slice(h * HEAD_DIM, (h + 1) * HEAD_DIM)
                dsb = ds_ref[h]
                dq = jnp.dot(dsb, kb[:, ksl], preferred_element_type=F32) * scale
                dq_ref[:, hsl] = dq.astype(BF16)
                dk_acc = dk_acc + lax.dot_general(dsb, q_ref[:, hsl].astype(BF16), (((0,), (0,)), ((), ())),
                                                  preferred_element_type=F32)
                dv_acc = dv_acc + lax.dot_general(p_ref[h], do_ref[:, hsl], (((0,), (0,)), ((), ())),
                                                  preferred_element_type=F32)
            dk_ref[pl.ds(start, 3 * BLOCK), ksl] += dk_acc * scale
            dv_ref[pl.ds(start, 3 * BLOCK), ksl] += dv_acc

    full_kv = pl.BlockSpec((T + 2 * BLOCK, KV_WIDTH), lambda n: (0, 0))
    bias_spec = pl.BlockSpec((H, BLOCK, 3 * BLOCK), lambda n: (0, 0, 0))
    row = pl.BlockSpec((BLOCK, AW), lambda n: (n, 0))
    q_cols = pl.BlockSpec((BLOCK, AW), lambda n: (n, 2))
    band = (H, BLOCK, 3 * BLOCK)
    return pl.pallas_call(
        body, name=name, grid=(T // BLOCK,),
        in_specs=[q_cols, full_kv, full_kv, bias_spec, pl.BlockSpec((H, 1, 1), lambda n: (0, 0, 0)), row, _ANY],
        out_specs=[q_cols, full_kv, full_kv, bias_spec, pl.BlockSpec((H, BLOCK, 1), lambda n: (0, 0, 0))],
        out_shape=[jax.ShapeDtypeStruct(dz.shape, dz.dtype),
                   jax.ShapeDtypeStruct((T + 2 * BLOCK, KV_WIDTH), F32), jax.ShapeDtypeStruct((T + 2 * BLOCK, KV_WIDTH), F32),
                   jax.ShapeDtypeStruct(band, F32), jax.ShapeDtypeStruct((H, BLOCK, 1), F32)],
        input_output_aliases={6: 0},
        scratch_shapes=[pltpu.VMEM(band, F32), pltpu.VMEM(band, F32), pltpu.VMEM(band, BF16), pltpu.VMEM(band, BF16)],
        compiler_params=_cparams("arbitrary"),
    )(z, kpad, vpad, bias, sink, do, dz)


def _dkv_into(dkp, dvp, dz, *, name):
    T = dz.shape[0]
    D = (dz.shape[1] - 2 * KV_WIDTH) * 2 // 7
    col = (D + D // 2) // (2 * KV_WIDTH)
    assert col * 2 * KV_WIDTH == D + D // 2

    def body(dk_ref, dv_ref, dz_in_ref, o_ref):
        o_ref[:, :KV_WIDTH] = dk_ref[...].astype(BF16)
        o_ref[:, KV_WIDTH:] = dv_ref[...].astype(BF16)

    kv = pl.BlockSpec((BLOCK, KV_WIDTH), lambda n: (n + 1, 0))
    return pl.pallas_call(
        body, name=name, grid=(T // BLOCK,),
        in_specs=[kv, kv, _ANY], out_specs=pl.BlockSpec((BLOCK, 2 * KV_WIDTH), lambda n: (n, col)),
        out_shape=jax.ShapeDtypeStruct(dz.shape, dz.dtype), input_output_aliases={2: 0},
        compiler_params=_cparams("parallel"),
    )(dkp, dvp, dz)


def _kv_pad(z, *, name):
    T = z.shape[0]
    D = (z.shape[1] - 2 * KV_WIDTH) * 2 // 7
    kcol = (D + D // 2) // KV_WIDTH
    nb = T // BLOCK

    def body(k_ref, v_ref, ko_ref, vo_ref):
        b = pl.program_id(0)
        inside = (b >= 1) & (b <= nb)
        ko_ref[...] = jnp.where(inside, k_ref[...].astype(F32), 0.0).astype(BF16)
        vo_ref[...] = jnp.where(inside, v_ref[...].astype(F32), 0.0).astype(BF16)

    out = jax.ShapeDtypeStruct((T + 2 * BLOCK, KV_WIDTH), BF16)
    o_spec = pl.BlockSpec((BLOCK, KV_WIDTH), lambda b: (b, 0))
    return pl.pallas_call(
        body, name=name, grid=(nb + 2,),
        in_specs=[pl.BlockSpec((BLOCK, KV_WIDTH), lambda b: (jnp.clip(b - 1, 0, nb - 1), kcol)),
                  pl.BlockSpec((BLOCK, KV_WIDTH), lambda b: (jnp.clip(b - 1, 0, nb - 1), kcol + 1))],
        out_specs=[o_spec, o_spec], out_shape=[out, out],
        compiler_params=_cparams("parallel"),
    )(z, z)


def _attn_small_grads(dbias, dsink_rows, bmap, after, *, name):
    H = dbias.shape[0]

    def body(dbias_ref, dsink_ref, bmap_ref, drel_ref, ds_ref):
        bm_ = bmap_ref[...]
        for h in range(H):
            d = dbias_ref[h]
            for b in range(REL_BUCKETS):
                drel_ref[b, h] = jnp.sum(jnp.where(bm_ == b, d, 0.0))
            ds_ref[0, h] = jnp.sum(dsink_ref[h])

    vmem = pl.BlockSpec(memory_space=pltpu.VMEM)
    smem = pl.BlockSpec(memory_space=pltpu.SMEM)
    body, in_specs, args = _ordered_after(body, 3, [vmem, vmem, vmem], (dbias, dsink_rows, bmap), after)
    return pl.pallas_call(
        body, name=name, in_specs=in_specs, out_specs=[smem, smem],
        out_shape=[jax.ShapeDtypeStruct((REL_BUCKETS, H), F32), jax.ShapeDtypeStruct((1, H), F32)],
    )(*args)


def _local_step(x, target, weight, emit, flush, norm_mix, v_gain, w_s, b_s, sink, rel_bias, norm_ffn, norm_final, early=()):
    T, D = x.shape
    ws_b = w_s.astype(BF16)
    bs_t = b_s.T
    bmap = jnp.asarray(_bucket_map())
    sink = sink.reshape(-1, 1, 1)

    bias = _bias_table(rel_bias, bmap, name="bias_table")
    h = _rms_fwd(x, norm_mix, name="rms_mix", after=[bias, *early])
    w_in = weight("w_in", h)
    z = _mm(h, w_in, tb=True, out_dtype=BF16, name="mm_z", bm=2048, bn=768)
    a = _sgu_fwd(z, v_gain, ws_b, bs_t, name="sgu_fwd")
    w_a = weight("w_a_out", a)
    ya = _mm_w8(a, w_a, name="mm_ya", bm=2048, out_dtype=BF16)
    kpad, vpad = _kv_pad(z, name="kv_pad")
    o = _attn_fwd(z, kpad, vpad, bias, sink, name="attn_fwd")
    w_b = weight("w_b_out", o)
    yb = _mm_w8(o, w_b, name="mm_yb", bm=2048, out_dtype=BF16)
    m = _merge_fwd(z, ya, yb, name="merge_fwd")
    w_o = weight("w_o", m)
    x1, h2 = _mm_resid_rms(m, w_o, x, norm_ffn, name="mm_x1_rms")
    w_gate = weight("w_gate", h2)
    gate = _mm(h2, w_gate, tb=True, name="mm_gate", bm=2048, bn=512)
    w_up = weight("w_up", gate)
    gate, up, act = _swiglu_mm_fwd(h2, w_up, gate, name="mm_up_swiglu")
    w_down = weight("w_down", act)
    x2 = _mm(act, w_down, name="mm_x2", add=x1, bm=1024, bn=512)
    loss, dx2b, g_norm_final = _loss_head(x2, norm_final, target, name="loss_head")

    g_w_down = _mm(act, dx2b, ta=True, out_dtype=BF16, name="mm_gwdown", bm=512, bn=2048)
    tok = emit(("w_down",), (g_w_down,))
    dgate, dup = _swiglu_mm_bwd(dx2b, w_down, gate, up, name="mm_dact_swiglu", after=tok)
    tok = flush(dgate)
    g_w_gate = _mm(dgate, h2, ta=True, out_dtype=BF16, name="mm_gwgate", bm=512, bn=2048, after=tok)
    g_w_up = _mm(dup, h2, ta=True, out_dtype=BF16, name="mm_gwup", bm=512, bn=2048)
    tok = emit(("w_gate", "w_up"), (g_w_gate, g_w_up))
    dh2 = _mm_sum2(dgate, w_gate, dup, w_up, name="mm_dh2", after=tok)
    tok = flush(dh2)
    dx1, dx1b, g_norm_ffn = _rms_bwd(x1, norm_ffn, dh2, dx2b, name="rms_ffn_bwd", want_bf16=True, after=tok)

    g_w_o = _mm(m, dx1b, ta=True, out_dtype=BF16, name="mm_gwo", bm=2048, bn=512)
    tok = emit(("w_o",), (g_w_o,))
    dm = _mm(dx1b, w_o, tb=True, name="mm_dm", bm=2048, bn=512, after=tok)
    tok = flush(dm)
    dy, dz = _merge_bwd(z, ya, yb, dm, name="merge_bwd", after=tok)
    g_w_a = _mm_gw8(a, dy, w_a.shape[2], name="mm_gwa", lead=0)
    g_w_b = _mm_gw8(o, dy, w_b.shape[2], name="mm_gwb", lead=1)
    tok = emit(("w_a_out", "w_b_out"), (g_w_a, g_w_b))
    da = _mm_w8t(dy, w_a, name="mm_da", bm=2048, bn=512, after=tok, lead=0)
    tok = flush(da)
    do = _mm_w8t(dy, w_b, out_dtype=BF16, name="mm_do", bm=2048, bn=512, after=tok, lead=1)
    dz, g_w_s, g_b_s_t, g_v_gain = _sgu_bwd(z, v_gain, ws_b, bs_t, da, dz, name="sgu_bwd")
    dz, dkp, dvp, dbias, dsink_rows = _attn_bwd(z, kpad, vpad, bias, sink, do, dz, name="attn_bwd")
    dz = _dkv_into(dkp, dvp, dz, name="dkv_into_dz")
    g_w_in = _mm(dz, h, ta=True, out_dtype=BF16, name="mm_gwin", bm=768, bn=2048)
    tok = emit(("w_in",), (g_w_in,))
    half = dict(bm=T // 2, bn=256)
    dh = _mm(dz, w_in, name="mm_dh_top", row_blocks=(0, 1), after=tok, **half)
    tok = flush(dh)
    dh = _mm(dz, w_in, name="mm_dh_bottom", row_blocks=(1, 1), into=dh, after=tok, **half)
    g_rel_bias, g_sink = _attn_small_grads(dbias, dsink_rows, bmap, dh, name="attn_small_grads")
    grad_x, g_norm_mix = _rms_bwd(x, norm_mix, dh, dx1, name="rms_mix_bwd", want_bf16=False)

    small = dict(norm_mix=g_norm_mix, sgu_v_gain=g_v_gain, sgu_w_s=g_w_s, sgu_b_s=g_b_s_t.T, attn_sink=g_sink,
                 rel_bias=g_rel_bias, norm_ffn=g_norm_ffn, norm_final=g_norm_final)
    return loss, grad_x, small


def _position():
    return lax.axis_index("x"), lax.axis_index("y"), lax.axis_index("c")


def _other_chips(x, y):
    return [(1 - x, y), (x, 1 - y), (1 - x, 1 - y)]


def _slot(px, py, pc):
    return 4 * px + 2 * py + pc


_HBM = pl.BlockSpec(memory_space=pltpu.HBM)
_SEM = pl.BlockSpec(memory_space=pltpu.SEMAPHORE)
_DATAFLOW = pltpu.SideEffectType.DATAFLOW_SIDE_EFFECTING


def _in_hbm(a):
    return pltpu.with_memory_space_constraint(a, pltpu.HBM)


def _own_slot(shard, pos, *, name, after=None):
    R, C = shard.shape
    tr = _div(R, 256, 16)

    def body(pos_ref, w_ref, o_ref):
        o_ref[...] = w_ref[...].astype(BF16)

    body, in_specs, args = _ordered_after(body, 2, [pl.BlockSpec((tr, C), lambda i, pos_ref: (i, 0))], (pos, shard), after)
    grid_spec = pltpu.PrefetchScalarGridSpec(
        num_scalar_prefetch=1, grid=(R // tr,), in_specs=in_specs,
        out_specs=pl.BlockSpec((None, tr, C), lambda i, pos_ref: (pos_ref[0], i, 0)))
    return pl.pallas_call(
        body, name=name, grid_spec=grid_spec,
        out_shape=jax.ShapeDtypeStruct((N_DEV, R, C), BF16),
        compiler_params=_cparams("parallel"),
    )(*args)


def _ag_copies(w, land_ref, send_sems, recv_sems):
    x, y, c = _position()
    mine = land_ref.at[_slot(x, y, c)]
    targets = [(px, py, c) for px, py in _other_chips(x, y)] + [(x, y, 1 - c)]
    return [pltpu.make_async_remote_copy(src_ref=mine, dst_ref=mine, send_sem=send_sems.at[4 * w + k],
                                         recv_sem=recv_sems.at[4 * w + k], device_id=to, device_id_type=MESH)
            for k, to in enumerate(targets)]


def _ag_start(buffers, groups, *, name, after=None):
    lands = [buffers[i] for g in groups for i in g]
    n, ng = len(lands), len(groups)
    sizes = [len(g) for g in groups]

    def body(*refs):
        land_refs = refs[:n]
        sems = refs[n:n + 2 * ng]
        token = refs[-1]
        i = 0
        for g in range(ng):
            for w in range(sizes[g]):
                for cp in _ag_copies(w, land_refs[i], sems[2 * g], sems[2 * g + 1]):
                    cp.start()
                i += 1
        token[...] = jnp.zeros_like(token)

    sem_shapes = [pltpu.SemaphoreType.DMA((4 * k,)) for k in sizes for _ in range(2)]
    body, in_specs, args = _ordered_after(body, n, [_HBM] * n, tuple(_in_hbm(a) for a in lands), after)
    outs = pl.pallas_call(
        body, name=name,
        in_specs=in_specs,
        out_specs=tuple([_SEM] * (2 * ng) + [_HBM] * n + [pl.BlockSpec(memory_space=pltpu.VMEM)]),
        out_shape=tuple(sem_shapes + [pltpu.HBM(a.shape, a.dtype) for a in lands] + [jax.ShapeDtypeStruct((8, LANES), F32)]),
        input_output_aliases={i: 2 * ng + i for i in range(n)},
        compiler_params=pltpu.CompilerParams(has_side_effects=_DATAFLOW),
    )(*args)
    sems, thru = outs[:2 * ng], outs[2 * ng:2 * ng + n]
    result, i = [], 0
    for g in range(ng):
        k = sizes[g]
        result.append((sems[2 * g], sems[2 * g + 1], list(thru[i:i + k])))
        i += k
    return result, outs[-1]


def _ag_wait(send_sems, recv_sems, lands, after, *, name):
    n = len(lands)

    def body(*refs):
        land_refs = refs[:n]
        send_ref, recv_ref = refs[n], refs[n + 1]
        token = refs[-1]
        for w in range(n):
            for cp in _ag_copies(w, land_refs[w], send_ref, recv_ref):
                cp.wait_send()
                cp.wait_recv()
        token[...] = jnp.zeros_like(token)

    outs = pl.pallas_call(
        body, name=name,
        in_specs=[_HBM] * n + [_SEM, _SEM, _ANY],
        out_specs=tuple([_HBM] * n + [pl.BlockSpec(memory_space=pltpu.VMEM)]),
        out_shape=tuple([pltpu.HBM(a.shape, a.dtype) for a in lands] + [jax.ShapeDtypeStruct((8, LANES), F32)]),
        input_output_aliases={i: i for i in range(n)},
        compiler_params=pltpu.CompilerParams(has_side_effects=_DATAFLOW),
    )(*lands, send_sems, recv_sems, after)
    return list(outs[:n]), outs[n]


def _ag_forward(lands, *, name, after=None):
    n = len(lands)

    def body(*refs):
        in_refs, out_refs = refs[:n], refs[n:2 * n]
        send_sems, recv_sems = refs[2 * n:]
        x, y, c = _position()
        copies = []
        for w in range(n):
            for k, (px, py) in enumerate(_other_chips(x, y)):
                cp = pltpu.make_async_remote_copy(
                    src_ref=in_refs[w].at[_slot(px, py, c)], dst_ref=out_refs[w].at[_slot(px, py, c)],
                    send_sem=send_sems.at[3 * w + k], recv_sem=recv_sems.at[3 * w + k],
                    device_id=(x, y, 1 - c), device_id_type=MESH)
                cp.start()
                copies.append(cp)
        for cp in copies:
            cp.wait()

    body, in_specs, args = _ordered_after(body, n, [_ANY] * n, tuple(lands), after)
    return pl.pallas_call(
        body, name=name,
        in_specs=in_specs, out_specs=[_ANY] * n,
        out_shape=[jax.ShapeDtypeStruct(a.shape, a.dtype) for a in lands],
        input_output_aliases={i: i for i in range(n)},
        scratch_shapes=[pltpu.SemaphoreType.DMA((3 * n,)), pltpu.SemaphoreType.DMA((3 * n,))],
    )(*args)


def _sibling_copies(w, g8_ref, land_ref, send_sems, recv_sems):
    x, y, c = _position()
    return [pltpu.make_async_remote_copy(src_ref=g8_ref.at[2 * p + (1 - c)], dst_ref=land_ref.at[p],
                                         send_sem=send_sems.at[4 * w + p], recv_sem=recv_sems.at[4 * w + p],
                                         device_id=(x, y, 1 - c), device_id_type=MESH)
            for p in range(4)]


def _chip_copies(w, sums_ref, land_ref, send_sems, recv_sems):
    x, y, c = _position()
    return [pltpu.make_async_remote_copy(src_ref=sums_ref.at[2 * px + py], dst_ref=land_ref.at[k],
                                         send_sem=send_sems.at[3 * w + k], recv_sem=recv_sems.at[3 * w + k],
                                         device_id=(px, py, c), device_id_type=MESH)
            for k, (px, py) in enumerate(_other_chips(x, y))]


def _copies_start(copies, per_weight, srcs, *, name):
    n = len(srcs)
    lands = [lax.empty((per_weight,) + s.shape[1:], s.dtype) for s in srcs]

    def body(*refs):
        src_refs, land_refs = refs[:n], refs[n:2 * n]
        send_sems, recv_sems = refs[2 * n], refs[2 * n + 1]
        token = refs[-1]
        for w in range(n):
            for cp in copies(w, src_refs[w], land_refs[w], send_sems, recv_sems):
                cp.start()
        token[...] = jnp.zeros_like(token)

    outs = pl.pallas_call(
        body, name=name,
        in_specs=[_HBM] * (2 * n),
        out_specs=tuple([_SEM, _SEM] + [_HBM] * (2 * n) + [pl.BlockSpec(memory_space=pltpu.VMEM)]),
        out_shape=tuple([pltpu.SemaphoreType.DMA((per_weight * n,)), pltpu.SemaphoreType.DMA((per_weight * n,))]
                        + [pltpu.HBM(a.shape, a.dtype) for a in srcs + lands] + [jax.ShapeDtypeStruct((8, LANES), F32)]),
        input_output_aliases={i: 2 + i for i in range(2 * n)},
        compiler_params=pltpu.CompilerParams(has_side_effects=_DATAFLOW),
    )(*[_in_hbm(a) for a in srcs + lands])
    return outs[0], outs[1], list(outs[2:2 + n]), list(outs[2 + n:2 + 2 * n]), outs[-1]


def _copies_wait(copies, send_sems, recv_sems, srcs, lands, after, *, name):
    n = len(srcs)

    def body(*refs):
        src_refs, land_refs = refs[:n], refs[n:2 * n]
        send_ref, recv_ref = refs[2 * n], refs[2 * n + 1]
        for w in range(n):
            for cp in copies(w, src_refs[w], land_refs[w], send_ref, recv_ref):
                cp.wait_send()
                cp.wait_recv()

    outs = pl.pallas_call(
        body, name=name,
        in_specs=[_HBM] * (2 * n) + [_SEM, _SEM, _ANY],
        out_specs=tuple([_HBM] * (2 * n)),
        out_shape=tuple(pltpu.HBM(a.shape, a.dtype) for a in srcs + lands),
        input_output_aliases={i: i for i in range(2 * n)},
        compiler_params=pltpu.CompilerParams(has_side_effects=_DATAFLOW),
    )(*srcs, *lands, send_sems, recv_sems, after)
    return list(outs[:n]), list(outs[n:])


def _chip_sums(g8, from_sibling, pos, *, name):
    _, R, C = g8.shape
    tr = _div(R, 512, 16)

    def body(pos_ref, g_ref, s_ref, o_ref):
        o_ref[...] = (g_ref[...].astype(F32) + s_ref[...].astype(F32)).astype(BF16)

    def chip(k, pos_ref):
        return jnp.where(k >= pos_ref[1], k + 1, k)

    grid_spec = pltpu.PrefetchScalarGridSpec(
        num_scalar_prefetch=1, grid=(3, R // tr),
        in_specs=[pl.BlockSpec((None, tr, C), lambda k, i, pos_ref: (2 * chip(k, pos_ref) + pos_ref[2], i, 0)),
                  pl.BlockSpec((None, tr, C), lambda k, i, pos_ref: (chip(k, pos_ref), i, 0))],
        out_specs=pl.BlockSpec((None, tr, C), lambda k, i, pos_ref: (chip(k, pos_ref), i, 0)))
    return pl.pallas_call(
        body, name=name, grid_spec=grid_spec,
        out_shape=jax.ShapeDtypeStruct((4, R, C), BF16),
        compiler_params=_cparams("parallel", "parallel"),
    )(pos, g8, from_sibling)


def _small_all_reduce(packed, after, *, name):
    R, L = packed.shape

    def body(x_ref, sum_ref, gath_ref, send_sems, recv_sems, local_sem):
        x, y, c = _position()
        me, sibling = (x, y, c), (x, y, 1 - c)
        chips = _other_chips(x, y)

        def rows(px, py, pc):
            return gath_ref.at[pl.ds(_slot(px, py, pc) * R, R), :]

        def copy(k, block, to, src=None):
            return pltpu.make_async_remote_copy(
                src_ref=rows(*block) if src is None else src, dst_ref=rows(*block),
                send_sem=send_sems.at[k], recv_sem=recv_sems.at[k], device_id=to, device_id_type=MESH)

        mine = pltpu.make_async_copy(x_ref, rows(*me), local_sem)
        mine.start()
        first = [copy(0, me, sibling, src=x_ref)]
        first += [copy(1 + j, me, (*chip, c), src=x_ref) for j, chip in enumerate(chips)]
        for cp in first:
            cp.start()
        passed = [copy(4 + j, (*chip, c), sibling) for j, chip in enumerate(chips)]
        for j, chip in enumerate(chips):
            copy(1 + j, (*chip, c), me).wait_recv()
            passed[j].start()
        copy(0, sibling, me).wait_recv()
        for j, chip in enumerate(chips):
            copy(4 + j, (*chip, 1 - c), me).wait_recv()
        for cp in first + passed:
            cp.wait_send()
        mine.wait()
        acc = gath_ref[0:R, :]
        for d in range(1, N_DEV):
            acc = acc + gath_ref[d * R:(d + 1) * R, :]
        sum_ref[...] = acc

    vmem = pl.BlockSpec(memory_space=pltpu.VMEM)
    body, in_specs, args = _ordered_after(body, 1, [vmem], (packed,), after)
    return pl.pallas_call(
        body, name=name, in_specs=in_specs, out_specs=vmem,
        out_shape=jax.ShapeDtypeStruct((R, L), F32),
        scratch_shapes=[pltpu.VMEM((N_DEV * R, L), F32), pltpu.SemaphoreType.DMA((7,)), pltpu.SemaphoreType.DMA((7,)),
                        pltpu.SemaphoreType.DMA],
        compiler_params=pltpu.CompilerParams(vmem_limit_bytes=VMEM_LIMIT),
    )(*args)


def _adamw_math(w, g, m, v):
    m = ADAM_B1 * m + (1.0 - ADAM_B1) * g
    v = ADAM_B2 * v + (1.0 - ADAM_B2) * (g * g)
    m_hat = m / (1.0 - ADAM_B1 ** ADAM_STEP)
    v_hat = v / (1.0 - ADAM_B2 ** ADAM_STEP)
    delta = -ADAM_LR * (m_hat / (jnp.sqrt(v_hat) + ADAM_EPS) + ADAM_WD * w)
    return delta, m, v


def _adamw_shard(w, m, v, g8, from_sibling, from_chips, pos, *, name):
    R, C = w.shape
    tr = _div(R, 256, 16)

    def body(pos_ref, w_ref, m_ref, v_ref, g_ref, s_ref, r_ref, go_ref, d_ref, mo_ref, vo_ref):
        g = g_ref[...].astype(F32) + s_ref[...].astype(F32)
        for k in range(3):
            g = g + r_ref[k].astype(F32)
        delta, m_, v_ = _adamw_math(w_ref[...], g, m_ref[...], v_ref[...])
        go_ref[...] = g
        d_ref[...] = delta
        mo_ref[...] = m_
        vo_ref[...] = v_

    blk = pl.BlockSpec((tr, C), lambda i, pos_ref: (i, 0))
    grid_spec = pltpu.PrefetchScalarGridSpec(
        num_scalar_prefetch=1, grid=(R // tr,),
        in_specs=[blk, blk, blk,
                  pl.BlockSpec((None, tr, C), lambda i, pos_ref: (pos_ref[0], i, 0)),
                  pl.BlockSpec((None, tr, C), lambda i, pos_ref: (pos_ref[1], i, 0)),
                  pl.BlockSpec((3, tr, C), lambda i, pos_ref: (0, i, 0))],
        out_specs=[blk] * 4)
    out = jax.ShapeDtypeStruct((R, C), F32)
    return pl.pallas_call(
        body, name=name, grid_spec=grid_spec, out_shape=[out] * 4,
        compiler_params=_cparams("parallel"),
    )(pos, w, m, v, g8, from_sibling, from_chips)


def _adamw_small(w, g, m, v, *, name):
    R, L = w.shape

    def body(w_ref, g_ref, m_ref, v_ref, d_ref, mo_ref, vo_ref):
        delta, m_, v_ = _adamw_math(w_ref[...], g_ref[...], m_ref[...], v_ref[...])
        d_ref[...] = delta
        mo_ref[...] = m_
        vo_ref[...] = v_

    vmem = pl.BlockSpec(memory_space=pltpu.VMEM)
    out = jax.ShapeDtypeStruct((R, L), F32)
    return pl.pallas_call(body, name=name, in_specs=[vmem] * 4, out_specs=[vmem] * 3, out_shape=[out] * 3)(w, g, m, v)


_TILE = 8 * LANES


def _pack(pieces):
    rows = []
    for p in pieces:
        flat = p.reshape(-1).astype(F32)
        padded = -(-flat.shape[0] // _TILE) * _TILE
        rows.append(jnp.pad(flat, (0, padded - flat.shape[0])).reshape(-1, LANES))
    return jnp.concatenate(rows, axis=0)


def _unpack(packed, like):
    out, r = [], 0
    for p in like:
        size = int(np.prod(p.shape)) if p.shape else 1
        nrows = -(-size // _TILE) * 8
        out.append(packed[r:r + nrows].reshape(-1)[:size].reshape(p.shape))
        r += nrows
    return out


_BIG = ("w_in", "w_a_out", "w_b_out", "w_o", "w_gate", "w_up", "w_down")
_TRANSPOSED = ("w_in", "w_gate", "w_up")
_COL_SHARDED = ("w_a_out", "w_b_out")
_GATHER_GROUPS = (("w_in",), ("w_a_out", "w_b_out", "w_o"), ("w_gate",), ("w_up",), ("w_down",))
_START_AFTER_WAIT = {0: (1, 2), 1: (3,), 2: (4,)}
_SMALL = ("norm_mix", "sgu_v_gain", "sgu_w_s", "sgu_b_s", "attn_sink", "rel_bias", "norm_ffn", "norm_final")
_ORDER = ("w_in", "norm_mix", "sgu_v_gain", "sgu_w_s", "sgu_b_s", "w_a_out", "attn_sink", "rel_bias", "w_b_out", "w_o",
          "norm_ffn", "w_gate", "w_up", "w_down", "norm_final")


def _shard(name, a):
    return jnp.swapaxes(a, 1, 2)[0] if name in _TRANSPOSED else a[0]


def _unshard(name, a):
    return jnp.swapaxes(a[None], 1, 2) if name in _TRANSPOSED else a[None]


def _whole(name, gathered):
    _, r, c = gathered.shape
    return gathered if name in _COL_SHARDED else gathered.reshape(N_DEV * r, c)


def _blocks(name, grad):
    if name in _COL_SHARDED:
        return grad
    r, c = grad.shape
    return grad.reshape(N_DEV, r // N_DEV, c)


def kernel(x, w_in, norm_mix, sgu_v_gain, sgu_w_s, sgu_b_s, w_a_out, attn_sink, rel_bias, w_b_out, w_o, norm_ffn, w_gate, w_up, w_down, norm_final, loss_target, m_w_in, m_norm_mix, m_sgu_v_gain, m_sgu_w_s, m_sgu_b_s, m_w_a_out, m_attn_sink, m_rel_bias, m_w_b_out, m_w_o, m_norm_ffn, m_w_gate, m_w_up, m_w_down, m_norm_final, v_w_in, v_norm_mix, v_sgu_v_gain, v_sgu_w_s, v_sgu_b_s, v_w_a_out, v_attn_sink, v_rel_bias, v_w_b_out, v_w_o, v_norm_ffn, v_w_gate, v_w_up, v_w_down, v_norm_final):
    w = dict(w_in=w_in, norm_mix=norm_mix, sgu_v_gain=sgu_v_gain, sgu_w_s=sgu_w_s, sgu_b_s=sgu_b_s, w_a_out=w_a_out,
             attn_sink=attn_sink, rel_bias=rel_bias, w_b_out=w_b_out, w_o=w_o, norm_ffn=norm_ffn, w_gate=w_gate,
             w_up=w_up, w_down=w_down, norm_final=norm_final)
    m = dict(w_in=m_w_in, norm_mix=m_norm_mix, sgu_v_gain=m_sgu_v_gain, sgu_w_s=m_sgu_w_s, sgu_b_s=m_sgu_b_s,
             w_a_out=m_w_a_out, attn_sink=m_attn_sink, rel_bias=m_rel_bias, w_b_out=m_w_b_out, w_o=m_w_o,
             norm_ffn=m_norm_ffn, w_gate=m_w_gate, w_up=m_w_up, w_down=m_w_down, norm_final=m_norm_final)
    v = dict(w_in=v_w_in, norm_mix=v_norm_mix, sgu_v_gain=v_sgu_v_gain, sgu_w_s=v_sgu_w_s, sgu_b_s=v_sgu_b_s,
             w_a_out=v_w_a_out, attn_sink=v_attn_sink, rel_bias=v_rel_bias, w_b_out=v_w_b_out, w_o=v_w_o,
             norm_ffn=v_norm_ffn, w_gate=v_w_gate, w_up=v_w_up, w_down=v_w_down, norm_final=v_norm_final)
    xc, yc, cc = _position()
    pos = jnp.stack([_slot(xc, yc, cc), 2 * xc + yc, cc]).astype(jnp.int32)

    in_flight, full, slots = {}, {}, {}

    def start_gather(groups, after):
        names = [n for gi in groups for n in _GATHER_GROUPS[gi]]
        flights, token = _ag_start([slots[n] for n in names], [[names.index(n) for n in _GATHER_GROUPS[gi]] for gi in groups],
                                   name="ag_start_%d" % groups[0], after=after)
        in_flight.update(zip(groups, flights))
        return token

    def weight(name, after):
        if name not in full:
            gi = next(i for i, grp in enumerate(_GATHER_GROUPS) if name in grp)
            send_sems, recv_sems, lands = in_flight[gi]
            lands, token = _ag_wait(send_sems, recv_sems, lands, after, name="ag_wait_%d" % gi)
            started = start_gather(_START_AFTER_WAIT[gi], token) if gi in _START_AFTER_WAIT else None
            gathered = _ag_forward(lands, name="ag_forward_%d" % gi, after=started)
            full.update({n: _whole(n, g) for n, g in zip(_GATHER_GROUPS[gi], gathered)})
        return full[name]

    for n in _GATHER_GROUPS[0]:
        slots[n] = _own_slot(_shard(n, w[n]), pos, name="own_slot_" + n)
    first_started = start_gather((0,), None)
    for grp in _GATHER_GROUPS[1:]:
        for n in grp:
            slots[n] = _own_slot(_shard(n, w[n]), pos, name="own_slot_" + n, after=first_started)

    to_sibling, reducing = [], {}

    def emit(names, grads):
        g8 = [_blocks(n, g) for n, g in zip(names, grads)]
        send_sems, recv_sems, g8, lands, token = _copies_start(_sibling_copies, 4, g8, name="rs_sibling_start_" + names[0])
        to_sibling.append((names, send_sems, recv_sems, g8, lands))
        return token

    def flush(after):
        names, send_sems, recv_sems, g8, lands = to_sibling.pop()
        g8, from_sibling = _copies_wait(_sibling_copies, send_sems, recv_sems, g8, lands, after,
                                        name="rs_sibling_wait_" + names[0])
        sums4 = [_chip_sums(g, s, pos, name="chip_sums_" + n) for n, g, s in zip(names, g8, from_sibling)]
        send_sems, recv_sems, sums4, lands, token = _copies_start(_chip_copies, 3, sums4, name="rs_chips_start_" + names[0])
        reducing[names] = (g8, from_sibling, send_sems, recv_sems, sums4, lands)
        return token

    loss, grad_x, small_grads_local = _local_step(
        x[0], loss_target[0], weight, emit, flush, norm_mix, sgu_v_gain, sgu_w_s[0], sgu_b_s[0], attn_sink, rel_bias,
        norm_ffn, norm_final[None], early=[slots[n] for grp in _GATHER_GROUPS[1:] for n in grp])

    out_g, out_d, out_m, out_v = {}, {}, {}, {}
    small_like = [w[n] for n in _SMALL]
    small_w = _pack(small_like)
    packed = _pack([small_grads_local[n] for n in _SMALL] + [loss[0, 0]])
    after = grad_x
    for gi, (names, (g8, from_sibling, send_sems, recv_sems, sums4, lands)) in enumerate(reducing.items()):
        if gi == len(reducing) - 1:
            summed = _small_all_reduce(packed, after, name="small_all_reduce")
            after = summed
        _, from_chips = _copies_wait(_chip_copies, send_sems, recv_sems, sums4, lands, after,
                                     name="rs_chips_wait_" + names[0])
        for i, n in enumerate(names):
            g, d, m_, v_ = _adamw_shard(_shard(n, w[n]), _shard(n, m[n]), _shard(n, v[n]), g8[i], from_sibling[i],
                                        from_chips[i], pos, name="adamw_" + n)
            out_g[n], out_d[n], out_m[n], out_v[n] = (_unshard(n, o) for o in (g, d, m_, v_))
            after = d
    *small_grads, loss_sum = _unpack(summed, small_like + [jax.ShapeDtypeStruct((), F32)])
    d_s, m_s, v_s = _adamw_small(small_w, summed[:small_w.shape[0]], _pack([m[n] for n in _SMALL]),
                                 _pack([v[n] for n in _SMALL]), name="adamw_small")
    for n, g, d, m_, v_ in zip(_SMALL, small_grads, _unpack(d_s, small_like), _unpack(m_s, small_like), _unpack(v_s, small_like)):
        out_g[n], out_d[n], out_m[n], out_v[n] = g, d, m_, v_

    return (loss_sum, grad_x[None], *[out_g[n] for n in _ORDER], *[out_d[n] for n in _ORDER],
            *[out_m[n] for n in _ORDER], *[out_v[n] for n in _ORDER])
```

```python
import functools
import math

import numpy as np
import jax
import jax.numpy as jnp
from jax import lax
from jax.experimental import pallas as pl
from jax.experimental.pallas import tpu as pltpu

F32 = jnp.float32
BF16 = jnp.bfloat16

EPS = 1e-6
NEG = -1e30
HEAD_DIM = 128
BLOCK = 128
N_KV_HEADS = 2
KV_WIDTH = N_KV_HEADS * HEAD_DIM
REL_BUCKETS = 32
REL_MAX_DIST = 128

ADAM_LR = 0.001
ADAM_B1 = 0.9
ADAM_B2 = 0.999
ADAM_EPS = 1e-08
ADAM_WD = 0.01
ADAM_STEP = 10

N_DEV = 8
LANES = 128
VMEM_LIMIT = 56 * 1024 * 1024
MESH = pl.DeviceIdType.MESH


def _cparams(*sem):
    return pltpu.CompilerParams(dimension_semantics=sem, vmem_limit_bytes=VMEM_LIMIT)


def _div(n, target, mult=LANES):
    best = None
    for d in range(mult, min(n, target) + 1, mult):
        if n % d == 0:
            best = d
    assert best is not None, (n, target, mult)
    return best


_ANY = pl.BlockSpec(memory_space=pl.ANY)


def _ordered_after(body, n_inputs, in_specs, args, after):
    if after is None:
        return body, in_specs, args
    extra = tuple(after) if isinstance(after, (tuple, list)) else (after,)

    def wrapped(*refs):
        return body(*refs[:n_inputs], *refs[n_inputs + len(extra):])

    return wrapped, list(in_specs) + [_ANY] * len(extra), tuple(args) + extra


def _bucket_map():
    nb = REL_BUCKETS // 2
    qi = np.arange(BLOCK)[:, None]
    kj = np.arange(3 * BLOCK)[None, :]
    rel = kj - BLOCK - qi
    ret = np.where(rel > 0, nb, 0)
    n = np.abs(rel)
    max_exact = nb // 2
    nf = np.maximum(n, 1).astype(np.float32)
    large = max_exact + (np.log(nf / np.float32(max_exact)) / np.float32(math.log(REL_MAX_DIST / max_exact))
                         * np.float32(nb - max_exact)).astype(np.int32)
    large = np.minimum(large, nb - 1)
    return (ret + np.where(n < max_exact, n, large)).astype(np.int32)


_GELU_C = math.sqrt(2.0 / math.pi)
_GELU_A = 0.044715


def _gelu(x):
    t = jnp.tanh(_GELU_C * (x + _GELU_A * (x * x * x)))
    return 0.5 * x * (1.0 + t)


def _gelu_and_grad(x):
    x2 = x * x
    t = jnp.tanh(_GELU_C * (x + _GELU_A * (x2 * x)))
    g = 0.5 * x * (1.0 + t)
    dg = 0.5 * (1.0 + t) + 0.5 * x * (1.0 - t * t) * (_GELU_C * (1.0 + 3.0 * _GELU_A * x2))
    return g, dg


def _sigmoid(x):
    return 1.0 / (1.0 + jnp.exp(-x))


def _mm(a, b, *, name, ta=False, tb=False, add=None, out_dtype=F32, bm=1024, bn=1024, bk=None, after=None,
        row_blocks=None, into=None):
    if ta:
        K, M = a.shape
    else:
        M, K = a.shape
    N = b.shape[0] if tb else b.shape[1]
    assert (b.shape[1] if tb else b.shape[0]) == K
    bm = _div(M, bm)
    bn = _div(N, bn)
    bk = K if bk is None else _div(K, bk)
    nk = K // bk
    i0, ni = (0, M // bm) if row_blocks is None else row_blocks
    a_spec = (pl.BlockSpec((bk, bm), lambda i, j, k: (k, i + i0)) if ta
              else pl.BlockSpec((bm, bk), lambda i, j, k: (i + i0, k)))
    b_spec = pl.BlockSpec((bn, bk), lambda i, j, k: (j, k)) if tb else pl.BlockSpec((bk, bn), lambda i, j, k: (k, j))
    o_spec = pl.BlockSpec((bm, bn), lambda i, j, k: (i + i0, j))
    dims = (((0 if ta else 1,), (1 if tb else 0,)), ((), ()))
    has_add = add is not None

    def body(*refs):
        if has_add:
            a_ref, b_ref, add_ref, o_ref, *scratch = refs
        else:
            a_ref, b_ref, o_ref, *scratch = refs
            add_ref = None
        p = lax.dot_general(a_ref[...].astype(BF16), b_ref[...].astype(BF16), dims, preferred_element_type=F32)
        if nk == 1:
            if has_add:
                p = p + add_ref[...]
            o_ref[...] = p.astype(out_dtype)
        else:
            acc = scratch[0]
            k = pl.program_id(2)

            @pl.when(k == 0)
            def _():
                acc[...] = p

            @pl.when(k > 0)
            def _():
                acc[...] += p

            @pl.when(k == nk - 1)
            def _():
                r = acc[...]
                if has_add:
                    r = r + add_ref[...]
                o_ref[...] = r.astype(out_dtype)

    in_specs = [a_spec, b_spec] + ([o_spec] if has_add else [])
    args = (a, b) + ((add,) if has_add else ())
    aliases = {}
    if into is not None:
        body, in_specs, args = _ordered_after(body, len(args), in_specs, args, into)
        aliases = {len(args) - 1: 0}
    body, in_specs, args = _ordered_after(body, len(args), in_specs, args, after)
    return pl.pallas_call(
        body, name=name, grid=(ni, N // bn, nk),
        in_specs=in_specs, out_specs=o_spec,
        out_shape=jax.ShapeDtypeStruct((M, N), out_dtype),
        input_output_aliases=aliases,
        scratch_shapes=[pltpu.VMEM((bm, bn), F32)] if nk > 1 else [],
        compiler_params=_cparams("parallel", "parallel", "arbitrary"),
    )(*args)


def _mm_resid_rms(a, b, resid, gain, *, name, bm=512):
    M, K = a.shape
    N = b.shape[1]
    bm = _div(M, bm)

    def body(a_ref, b_ref, r_ref, g_ref, x_ref, h_ref):
        x = r_ref[...] + jnp.dot(a_ref[...], b_ref[...], preferred_element_type=F32)
        x_ref[...] = x
        r = lax.rsqrt(jnp.mean(x * x, axis=-1, keepdims=True) + EPS)
        h_ref[...] = ((x * r) * g_ref[...]).astype(BF16)

    row = pl.BlockSpec((bm, N), lambda i: (i, 0))
    return pl.pallas_call(
        body, name=name, grid=(M // bm,),
        in_specs=[pl.BlockSpec((bm, K), lambda i: (i, 0)), pl.BlockSpec((K, N), lambda i: (0, 0)), row,
                  pl.BlockSpec((1, N), lambda i: (0, 0))],
        out_specs=[row, row], out_shape=[jax.ShapeDtypeStruct((M, N), F32), jax.ShapeDtypeStruct((M, N), BF16)],
        compiler_params=_cparams("parallel"),
    )(a, b, resid, gain)


def _mm_sum2(a1, b1, a2, b2, *, name, bm=1024, bn=512, bk=2816, after=None):
    M, K = a1.shape
    N = b1.shape[1]
    bm, bn, bk = _div(M, bm), _div(N, bn), _div(K, bk)
    nk = K // bk

    def body(a1_ref, b1_ref, a2_ref, b2_ref, o_ref, acc):
        p = (jnp.dot(a1_ref[...], b1_ref[...], preferred_element_type=F32)
             + jnp.dot(a2_ref[...], b2_ref[...], preferred_element_type=F32))
        k = pl.program_id(2)

        @pl.when(k == 0)
        def _():
            acc[...] = p

        @pl.when(k > 0)
        def _():
            acc[...] += p

        @pl.when(k == nk - 1)
        def _():
            o_ref[...] = acc[...]

    a_spec = pl.BlockSpec((bm, bk), lambda i, j, k: (i, k))
    b_spec = pl.BlockSpec((bk, bn), lambda i, j, k: (k, j))
    body, in_specs, args = _ordered_after(body, 4, [a_spec, b_spec, a_spec, b_spec], (a1, b1, a2, b2), after)
    return pl.pallas_call(
        body, name=name, grid=(M // bm, N // bn, nk),
        in_specs=in_specs, out_specs=pl.BlockSpec((bm, bn), lambda i, j, k: (i, j)),
        out_shape=jax.ShapeDtypeStruct((M, N), F32),
        scratch_shapes=[pltpu.VMEM((bm, bn), F32)],
        compiler_params=_cparams("parallel", "parallel", "arbitrary"),
    )(*args)


def _blocks_per_tile(c):
    nb = 1
    while (nb * c) % LANES or (nb * c < 1024 and nb < N_DEV):
        nb *= 2
    assert nb <= N_DEV and (nb * c) % LANES == 0, c
    return nb


def _mm_w8(a, w8, *, name, bm=1024, out_dtype=F32):
    M, K = a.shape
    _, _, c = w8.shape
    nb = _blocks_per_tile(c)
    bm = _div(M, bm)

    def body(a_ref, w_ref, o_ref):
        a_ = a_ref[...]
        for t in range(nb):
            o_ref[:, t * c:(t + 1) * c] = jnp.dot(a_, w_ref[t], preferred_element_type=F32).astype(out_dtype)

    return pl.pallas_call(
        body, name=name, grid=(M // bm, N_DEV // nb),
        in_specs=[pl.BlockSpec((bm, K), lambda i, j: (i, 0)), pl.BlockSpec((nb, K, c), lambda i, j: (j, 0, 0))],
        out_specs=pl.BlockSpec((bm, nb * c), lambda i, j: (i, j)),
        out_shape=jax.ShapeDtypeStruct((M, N_DEV * c), out_dtype),
        compiler_params=_cparams("parallel", "parallel"),
    )(a, w8)


def _mm_w8t(dy, w8, *, name, add=None, out_dtype=F32, bm=1024, bn=1024, after=None, lead=None):
    M = dy.shape[-2]
    _, K, c = w8.shape
    nb = _blocks_per_tile(c)
    nk = N_DEV // nb
    bm, bn = _div(M, bm), _div(K, bn)
    has_add = add is not None
    dims = (((1,), (1,)), ((), ()))

    def body(*refs):
        if has_add:
            dy_ref, w_ref, add_ref, o_ref, acc = refs
        else:
            dy_ref, w_ref, o_ref, acc = refs
        p = lax.dot_general(dy_ref[:, 0:c], w_ref[0], dims, preferred_element_type=F32)
        for t in range(1, nb):
            p = p + lax.dot_general(dy_ref[:, t * c:(t + 1) * c], w_ref[t], dims, preferred_element_type=F32)
        k = pl.program_id(2)

        @pl.when(k == 0)
        def _():
            acc[...] = p

        @pl.when(k > 0)
        def _():
            acc[...] += p

        @pl.when(k == nk - 1)
        def _():
            r = acc[...]
            if has_add:
                r = r + add_ref[...]
            o_ref[...] = r.astype(out_dtype)

    o_spec = pl.BlockSpec((bm, bn), lambda i, j, k: (i, j))
    dy_spec = (pl.BlockSpec((bm, nb * c), lambda i, j, k: (i, k)) if lead is None
               else pl.BlockSpec((None, bm, nb * c), lambda i, j, k: (lead, i, k)))
    in_specs = [dy_spec, pl.BlockSpec((nb, bn, c), lambda i, j, k: (k, j, 0))]
    in_specs += [o_spec] if has_add else []
    args = (dy, w8) + ((add,) if has_add else ())
    body, in_specs, args = _ordered_after(body, len(args), in_specs, args, after)
    return pl.pallas_call(
        body, name=name, grid=(M // bm, K // bn, nk),
        in_specs=in_specs, out_specs=o_spec,
        out_shape=jax.ShapeDtypeStruct((M, K), out_dtype),
        scratch_shapes=[pltpu.VMEM((bm, bn), F32)],
        compiler_params=_cparams("parallel", "parallel", "arbitrary"),
    )(*args)


def _mm_gw8(x, dy, c, *, name, bk=1024, lead=None):
    T, K = x.shape
    nb = _blocks_per_tile(c)
    bk = _div(K, bk)
    dims = (((0,), (0,)), ((), ()))

    def body(x_ref, dy_ref, o_ref):
        x_ = x_ref[...]
        for t in range(nb):
            o_ref[t] = lax.dot_general(x_, dy_ref[:, t * c:(t + 1) * c], dims, preferred_element_type=F32).astype(BF16)

    dy_spec = (pl.BlockSpec((T, nb * c), lambda i, j: (0, j)) if lead is None
               else pl.BlockSpec((None, T, nb * c), lambda i, j: (lead, 0, j)))
    return pl.pallas_call(
        body, name=name, grid=(K // bk, N_DEV // nb),
        in_specs=[pl.BlockSpec((T, bk), lambda i, j: (0, i)), dy_spec],
        out_specs=pl.BlockSpec((nb, bk, c), lambda i, j: (j, i, 0)),
        out_shape=jax.ShapeDtypeStruct((N_DEV, K, c), BF16),
        compiler_params=_cparams("parallel", "parallel"),
    )(x, dy)


def _rms_fwd(x, g, *, name, after=None):
    T, D = x.shape
    tm = _div(T, 256, 8)

    def body(x_ref, g_ref, h_ref):
        xf = x_ref[...]
        r = lax.rsqrt(jnp.mean(xf * xf, axis=-1, keepdims=True) + EPS)
        h_ref[...] = ((xf * r) * g_ref[...]).astype(BF16)

    in_specs = [pl.BlockSpec((tm, D), lambda i: (i, 0)), pl.BlockSpec((1, D), lambda i: (0, 0))]
    body, in_specs, args = _ordered_after(body, 2, in_specs, (x, g), after)
    return pl.pallas_call(
        body, name=name, grid=(T // tm,),
        in_specs=in_specs,
        out_specs=pl.BlockSpec((tm, D), lambda i: (i, 0)),
        out_shape=jax.ShapeDtypeStruct((T, D), BF16),
        compiler_params=_cparams("parallel"),
    )(*args)


def _rms_bwd(x, g, dh, dres, *, name, want_bf16, after=None):
    T, D = x.shape
    tm = _div(T, 256, 8)

    def body(x_ref, g_ref, dh_ref, dres_ref, dx_ref, *rest):
        if want_bf16:
            dxb_ref, dg_ref = rest
        else:
            (dg_ref,) = rest
        xf = x_ref[...]
        r = lax.rsqrt(jnp.mean(xf * xf, axis=-1, keepdims=True) + EPS)
        xhat = xf * r
        dh_ = dh_ref[...]
        dy = dh_ * g_ref[...]
        dx = dres_ref[...].astype(F32) + r * (dy - xhat * jnp.mean(dy * xhat, axis=-1, keepdims=True))
        dx_ref[...] = dx
        if want_bf16:
            dxb_ref[...] = dx.astype(BF16)
        part = jnp.sum(dh_ * xhat, axis=0, keepdims=True)

        @pl.when(pl.program_id(0) == 0)
        def _():
            dg_ref[...] = part

        @pl.when(pl.program_id(0) > 0)
        def _():
            dg_ref[...] += part

    row = pl.BlockSpec((tm, D), lambda i: (i, 0))
    vec = pl.BlockSpec((1, D), lambda i: (0, 0))
    out_specs = [row] + ([row] if want_bf16 else []) + [vec]
    out_shape = ([jax.ShapeDtypeStruct((T, D), F32)] + ([jax.ShapeDtypeStruct((T, D), BF16)] if want_bf16 else [])
                 + [jax.ShapeDtypeStruct((1, D), F32)])
    body, in_specs, args = _ordered_after(body, 4, [row, vec, row, row], (x, g, dh, dres), after)
    return pl.pallas_call(
        body, name=name, grid=(T // tm,),
        in_specs=in_specs, out_specs=out_specs, out_shape=out_shape,
        compiler_params=_cparams("arbitrary"),
    )(*args)


def _loss_head(x, g, target, *, name):
    T, D = x.shape
    tm = _div(T, 256, 16)

    def body(x_ref, g_ref, t_ref, loss_ref, dxb_ref, dg_ref):
        xf = x_ref[...]
        r = lax.rsqrt(jnp.mean(xf * xf, axis=-1, keepdims=True) + EPS)
        xhat = xf * r
        gain = g_ref[...]
        err = xhat * gain - t_ref[...]
        lpart = 0.5 * jnp.sum(jnp.mean(err * err, axis=-1, keepdims=True), axis=0, keepdims=True)
        dh_ = err * (1.0 / D)
        dy = dh_ * gain
        dx = r * (dy - xhat * jnp.mean(dy * xhat, axis=-1, keepdims=True))
        dxb_ref[...] = dx.astype(BF16)
        part = jnp.sum(dh_ * xhat, axis=0, keepdims=True)

        @pl.when(pl.program_id(0) == 0)
        def _():
            dg_ref[...] = part
            loss_ref[...] = jnp.broadcast_to(lpart, loss_ref.shape)

        @pl.when(pl.program_id(0) > 0)
        def _():
            dg_ref[...] += part
            loss_ref[...] += jnp.broadcast_to(lpart, loss_ref.shape)

    row = pl.BlockSpec((tm, D), lambda i: (i, 0))
    vec = pl.BlockSpec((1, D), lambda i: (0, 0))
    return pl.pallas_call(
        body, name=name, grid=(T // tm,),
        in_specs=[row, vec, row],
        out_specs=[pl.BlockSpec((8, LANES), lambda i: (0, 0)), row, vec],
        out_shape=[jax.ShapeDtypeStruct((8, LANES), F32), jax.ShapeDtypeStruct((T, D), BF16), jax.ShapeDtypeStruct((1, D), F32)],
        compiler_params=_cparams("arbitrary"),
    )(x, g, target)


def _gate_cols(D):
    off_a = 3 * D // 2 + 2 * KV_WIDTH
    off_b = off_a + D
    cw = math.gcd(math.gcd(off_a, off_b), math.gcd(D, 512))
    return cw, off_a // cw, off_b // cw


def _merge_fwd(z, ya, yb, *, name):
    T, D = ya.shape
    cw, ba, bb = _gate_cols(D)
    tm = _div(T, 512, 8)

    def body(ga_ref, gb_ref, ya_ref, yb_ref, m_ref):
        m_ref[...] = (_sigmoid(ga_ref[...].astype(F32)) * ya_ref[...]
                      + _sigmoid(gb_ref[...].astype(F32)) * yb_ref[...]).astype(BF16)

    blk = pl.BlockSpec((tm, cw), lambda i, j: (i, j))
    return pl.pallas_call(
        body, name=name, grid=(T // tm, D // cw),
        in_specs=[pl.BlockSpec((tm, cw), lambda i, j: (i, ba + j)), pl.BlockSpec((tm, cw), lambda i, j: (i, bb + j)), blk, blk],
        out_specs=blk, out_shape=jax.ShapeDtypeStruct((T, D), BF16),
        compiler_params=_cparams("parallel", "parallel"),
    )(z, z, ya, yb)


def _merge_bwd(z, ya, yb, dm, *, name, after=None):
    T, D = ya.shape
    cw, ba, bb = _gate_cols(D)
    nj = D // cw
    assert bb == ba + nj
    tm = _div(T, 512, 8)

    def body(g_ref, ya_ref, yb_ref, dm_ref, dy_ref, dz_ref):
        sig = _sigmoid(g_ref[...].astype(F32))
        dm_ = dm_ref[...].astype(F32)
        y = jnp.where(pl.program_id(1) == 0, ya_ref[...], yb_ref[...])
        dy_ref[...] = (dm_ * sig).astype(BF16)
        dz_ref[...] = (dm_ * y * (sig * (1.0 - sig))).astype(BF16)

    in_specs = [pl.BlockSpec((tm, cw), lambda i, s, j: (i, ba + s * nj + j)),
                pl.BlockSpec((tm, cw), lambda i, s, j: (i, j * (1 - s))),
                pl.BlockSpec((tm, cw), lambda i, s, j: (i, j * s)),
                pl.BlockSpec((tm, cw), lambda i, s, j: (i, j))]
    body, in_specs, args = _ordered_after(body, 4, in_specs, (z, ya, yb, dm), after)
    return pl.pallas_call(
        body, name=name, grid=(T // tm, 2, nj),
        in_specs=in_specs,
        out_specs=[pl.BlockSpec((None, tm, cw), lambda i, s, j: (s, i, j)),
                   pl.BlockSpec((tm, cw), lambda i, s, j: (i, ba + s * nj + j))],
        out_shape=[jax.ShapeDtypeStruct((2, T, D), BF16), jax.ShapeDtypeStruct(z.shape, BF16)],
        compiler_params=_cparams("parallel", "arbitrary", "arbitrary"),
    )(*args)


def _swiglu_mm_fwd(h, wu_t, gate, *, name, bm=1024, bn=512):
    T, D = h.shape
    F = wu_t.shape[0]
    bm, bn = _div(T, bm), _div(F, bn)

    rc = _div(bm, 256, 16)

    def body(h_ref, wu_ref, gin_ref, g_ref, u_ref, act_ref):
        w = wu_ref[...]
        for r in range(0, bm, rc):
            rows = slice(r, r + rc)
            u = lax.dot_general(h_ref[rows, :], w, (((1,), (1,)), ((), ())), preferred_element_type=F32)
            g = gin_ref[rows, :]
            g_ref[rows, :] = g.astype(BF16)
            u_ref[rows, :] = u.astype(BF16)
            act_ref[rows, :] = (g * _sigmoid(g) * u).astype(BF16)

    o_spec = pl.BlockSpec((bm, bn), lambda i, j: (i, j))
    return pl.pallas_call(
        body, name=name, grid=(T // bm, F // bn),
        in_specs=[pl.BlockSpec((bm, D), lambda i, j: (i, 0)), pl.BlockSpec((bn, D), lambda i, j: (j, 0)), o_spec],
        out_specs=[o_spec] * 3, out_shape=[jax.ShapeDtypeStruct((T, F), BF16)] * 3,
        compiler_params=_cparams("parallel", "parallel"),
    )(h, wu_t, gate)


def _swiglu_mm_bwd(dx, w_down, gate, up, *, name, bm=2048, bn=512, after=None):
    T, D = dx.shape
    F = w_down.shape[0]
    bm, bn = _div(T, bm), _div(F, bn)
    dims = (((1,), (1,)), ((), ()))

    rc = _div(bm, 256, 16)

    def body(dx_ref, w_ref, g_ref, u_ref, dg_ref, du_ref):
        w = w_ref[...]
        for r in range(0, bm, rc):
            rows = slice(r, r + rc)
            d = lax.dot_general(dx_ref[rows, :], w, dims, preferred_element_type=F32)
            g = g_ref[rows, :].astype(F32)
            s = _sigmoid(g)
            silu = g * s
            dg_ref[rows, :] = (d * u_ref[rows, :].astype(F32) * (s + silu * (1.0 - s))).astype(BF16)
            du_ref[rows, :] = (d * silu).astype(BF16)

    o_spec = pl.BlockSpec((bm, bn), lambda i, j: (i, j))
    in_specs = [pl.BlockSpec((bm, D), lambda i, j: (i, 0)), pl.BlockSpec((bn, D), lambda i, j: (j, 0)), o_spec, o_spec]
    body, in_specs, args = _ordered_after(body, 4, in_specs, (dx, w_down, gate, up), after)
    out = jax.ShapeDtypeStruct((T, F), BF16)
    return pl.pallas_call(
        body, name=name, grid=(T // bm, F // bn), in_specs=in_specs, out_specs=[o_spec, o_spec], out_shape=[out, out],
        compiler_params=_cparams("parallel", "parallel"),
    )(*args)


def _sgu_fwd(z, gain, ws_b, bs_t, *, name):
    T = z.shape[0]
    SW = gain.shape[1]
    G = SW // BLOCK

    def body(zu_ref, zv_ref, gain_ref, ws_ref, bs_ref, a_ref):
        u = _gelu(zu_ref[...].astype(F32))
        vg = _gelu(zv_ref[...].astype(F32))
        r = lax.rsqrt(jnp.mean(vg * vg, axis=-1, keepdims=True) + EPS)
        vn = ((vg * r) * gain_ref[...]).astype(BF16)
        for g in range(G):
            sl = slice(g * BLOCK, (g + 1) * BLOCK)
            mixed = jnp.dot(ws_ref[g], vn[:, sl], preferred_element_type=F32) + bs_ref[:, g:g + 1]
            a_ref[:, sl] = (u[:, sl] * mixed).astype(BF16)

    return pl.pallas_call(
        body, name=name, grid=(T // BLOCK,),
        in_specs=[pl.BlockSpec((BLOCK, SW), lambda c: (c, 0)), pl.BlockSpec((BLOCK, SW), lambda c: (c, 1)),
                  pl.BlockSpec((1, SW), lambda c: (0, 0)), pl.BlockSpec((G, BLOCK, BLOCK), lambda c: (0, 0, 0)),
                  pl.BlockSpec((BLOCK, G), lambda c: (0, 0))],
        out_specs=pl.BlockSpec((BLOCK, SW), lambda c: (c, 0)),
        out_shape=jax.ShapeDtypeStruct((T, SW), BF16),
        compiler_params=_cparams("parallel"),
    )(z, z, gain, ws_b, bs_t)


def _sgu_bwd(z, gain, ws_b, bs_t, da, dz, *, name):
    T = z.shape[0]
    SW = gain.shape[1]
    G = SW // BLOCK

    def body(zu_ref, zv_ref, gain_ref, ws_ref, bs_ref, da_ref, dz_in_ref, dz_ref, dws_ref, dbs_ref, dgain_ref, dvn_ref):
        first = pl.program_id(0) == 0

        @pl.when(first)
        def _():
            dws_ref[...] = jnp.zeros_like(dws_ref)
            dbs_ref[...] = jnp.zeros_like(dbs_ref)
            dgain_ref[...] = jnp.zeros_like(dgain_ref)

        u, du = _gelu_and_grad(zu_ref[...].astype(F32))
        vg, dvg = _gelu_and_grad(zv_ref[...].astype(F32))
        r = lax.rsqrt(jnp.mean(vg * vg, axis=-1, keepdims=True) + EPS)
        xhat = vg * r
        gain_ = gain_ref[...]
        vn = (xhat * gain_).astype(BF16)
        da_ = da_ref[...]
        for g in range(G):
            sl = slice(g * BLOCK, (g + 1) * BLOCK)
            w = ws_ref[g]
            mixed = jnp.dot(w, vn[:, sl], preferred_element_type=F32) + bs_ref[:, g:g + 1]
            dmix = da_[:, sl] * u[:, sl]
            dz_ref[:, sl] = (da_[:, sl] * mixed * du[:, sl]).astype(BF16)
            dmb = dmix.astype(BF16)
            dws_ref[g] += lax.dot_general(dmb, vn[:, sl], (((1,), (1,)), ((), ())), preferred_element_type=F32)
            dbs_ref[:, g:g + 1] += jnp.sum(dmix, axis=-1, keepdims=True)
            dvn_ref[:, sl] = lax.dot_general(w, dmb, (((0,), (0,)), ((), ())), preferred_element_type=F32)
        dvn = dvn_ref[...]
        dgain_ref[...] += jnp.sum(dvn * xhat, axis=0, keepdims=True)
        dy = dvn * gain_
        dv_ = r * (dy - xhat * jnp.mean(dy * xhat, axis=-1, keepdims=True))
        dz_ref[:, SW:] = (dv_ * dvg).astype(BF16)

    row = pl.BlockSpec((BLOCK, SW), lambda c: (c, 0))
    return pl.pallas_call(
        body, name=name, grid=(T // BLOCK,),
        in_specs=[row, pl.BlockSpec((BLOCK, SW), lambda c: (c, 1)),
                  pl.BlockSpec((1, SW), lambda c: (0, 0)), pl.BlockSpec((G, BLOCK, BLOCK), lambda c: (0, 0, 0)),
                  pl.BlockSpec((BLOCK, G), lambda c: (0, 0)), row, _ANY],
        out_specs=[pl.BlockSpec((BLOCK, 2 * SW), lambda c: (c, 0)), pl.BlockSpec((G, BLOCK, BLOCK), lambda c: (0, 0, 0)),
                   pl.BlockSpec((BLOCK, G), lambda c: (0, 0)), pl.BlockSpec((1, SW), lambda c: (0, 0))],
        out_shape=[jax.ShapeDtypeStruct(dz.shape, dz.dtype),
                   jax.ShapeDtypeStruct((G, BLOCK, BLOCK), F32), jax.ShapeDtypeStruct((BLOCK, G), F32),
                   jax.ShapeDtypeStruct((1, SW), F32)],
        input_output_aliases={6: 0},
        scratch_shapes=[pltpu.VMEM((BLOCK, SW), F32)],
        compiler_params=_cparams("arbitrary"),
    )(z, z, gain, ws_b, bs_t, da, dz)


def _bias_table(rel_bias, bmap, *, name):
    H = rel_bias.shape[1]

    def body(rb_ref, bmap_ref, o_ref):
        bm_ = bmap_ref[...]
        for h in range(H):
            acc = jnp.zeros(bm_.shape, F32)
            for b in range(REL_BUCKETS):
                acc = jnp.where(bm_ == b, rb_ref[b, h], acc)
            o_ref[h] = acc

    return pl.pallas_call(
        body, name=name,
        in_specs=[pl.BlockSpec(memory_space=pltpu.SMEM), pl.BlockSpec(memory_space=pltpu.VMEM)],
        out_specs=pl.BlockSpec(memory_space=pltpu.VMEM),
        out_shape=jax.ShapeDtypeStruct((H, BLOCK, 3 * BLOCK), F32),
    )(rel_bias, bmap)


def _attn_probs(q_ref, kb, bias_ref, sink_ref, s_ref, n, T, group):
    H = s_ref.shape[0]
    for h in range(H):
        kv = h // group
        qh = q_ref[:, h * HEAD_DIM:(h + 1) * HEAD_DIM].astype(BF16)
        s_ref[h] = lax.dot_general(qh, kb[:, kv * HEAD_DIM:(kv + 1) * HEAD_DIM], (((1,), (1,)), ((), ())),
                                   preferred_element_type=F32)
    row = lax.broadcasted_iota(jnp.int32, (BLOCK, 3 * BLOCK), 0)
    col = lax.broadcasted_iota(jnp.int32, (BLOCK, 3 * BLOCK), 1)
    key_pos = n * BLOCK + col - BLOCK
    valid = (jnp.abs(col - BLOCK - row) <= BLOCK) & (key_pos >= 0) & (key_pos < T)
    s = s_ref[...] * (HEAD_DIM ** -0.5) + bias_ref[...]
    s = jnp.where(valid[None], s, NEG)
    sink = sink_ref[...]
    m = jnp.maximum(jnp.max(s, axis=-1, keepdims=True), sink)
    e = jnp.exp(s - m)
    es = jnp.exp(sink - m)
    inv = 1.0 / (jnp.sum(e, axis=-1, keepdims=True) + es)
    return e * inv, es * inv


def _attn_fwd(z, kpad, vpad, bias, sink, *, name):
    T = z.shape[0]
    H = bias.shape[0]
    AW = H * HEAD_DIM
    group = H // N_KV_HEADS

    def body(q_ref, k_ref, v_ref, bias_ref, sink_ref, o_ref, s_ref, p_ref):
        n = pl.program_id(0)
        start = pl.multiple_of(n * BLOCK, BLOCK)
        kb = k_ref[pl.ds(start, 3 * BLOCK), :]
        vb = v_ref[pl.ds(start, 3 * BLOCK), :]
        p, _ = _attn_probs(q_ref, kb, bias_ref, sink_ref, s_ref, n, T, group)
        p_ref[...] = p.astype(BF16)
        for h in range(H):
            kv = h // group
            o = jnp.dot(p_ref[h], vb[:, kv * HEAD_DIM:(kv + 1) * HEAD_DIM], preferred_element_type=F32)
            o_ref[:, h * HEAD_DIM:(h + 1) * HEAD_DIM] = o.astype(BF16)

    full_kv = pl.BlockSpec((T + 2 * BLOCK, KV_WIDTH), lambda n: (0, 0))
    return pl.pallas_call(
        body, name=name, grid=(T // BLOCK,),
        in_specs=[pl.BlockSpec((BLOCK, AW), lambda n: (n, 2)), full_kv, full_kv,
                  pl.BlockSpec((H, BLOCK, 3 * BLOCK), lambda n: (0, 0, 0)), pl.BlockSpec((H, 1, 1), lambda n: (0, 0, 0))],
        out_specs=pl.BlockSpec((BLOCK, AW), lambda n: (n, 0)),
        out_shape=jax.ShapeDtypeStruct((T, AW), BF16),
        scratch_shapes=[pltpu.VMEM((H, BLOCK, 3 * BLOCK), F32), pltpu.VMEM((H, BLOCK, 3 * BLOCK), BF16)],
        compiler_params=_cparams("parallel"),
    )(z, kpad, vpad, bias, sink)


def _attn_bwd(z, kpad, vpad, bias, sink, do, dz, *, name):
    T = z.shape[0]
    H = bias.shape[0]
    AW = H * HEAD_DIM
    group = H // N_KV_HEADS
    scale = HEAD_DIM ** -0.5

    def body(q_ref, k_ref, v_ref, bias_ref, sink_ref, do_ref, dz_in_ref, dq_ref, dk_ref, dv_ref, dbias_ref, dsink_ref,
             s_ref, dp_ref, p_ref, ds_ref):
        n = pl.program_id(0)

        @pl.when(n == 0)
        def _():
            dk_ref[...] = jnp.zeros_like(dk_ref)
            dv_ref[...] = jnp.zeros_like(dv_ref)
            dbias_ref[...] = jnp.zeros_like(dbias_ref)
            dsink_ref[...] = jnp.zeros_like(dsink_ref)

        start = pl.multiple_of(n * BLOCK, BLOCK)
        kb = k_ref[pl.ds(start, 3 * BLOCK), :]
        vb = v_ref[pl.ds(start, 3 * BLOCK), :]
        p, p_sink = _attn_probs(q_ref, kb, bias_ref, sink_ref, s_ref, n, T, group)
        s_ref[...] = p
        p_ref[...] = p.astype(BF16)
        for h in range(H):
            kv = h // group
            dp_ref[h] = lax.dot_general(do_ref[:, h * HEAD_DIM:(h + 1) * HEAD_DIM], vb[:, kv * HEAD_DIM:(kv + 1) * HEAD_DIM],
                                        (((1,), (1,)), ((), ())), preferred_element_type=F32)
        p = s_ref[...]
        dp = dp_ref[...]
        delta = jnp.sum(p * dp, axis=-1, keepdims=True)
        ds = p * (dp - delta)
        dbias_ref[...] += ds
        dsink_ref[...] += -(p_sink * delta)
        ds_ref[...] = ds.astype(BF16)
        for kv in range(N_KV_HEADS):
            ksl = slice(kv * HEAD_DIM, (kv + 1) * HEAD_DIM)
            dk_acc = jnp.zeros((3 * BLOCK, HEAD_DIM), F32)
            dv_acc = jnp.zeros((3 * BLOCK, HEAD_DIM), F32)
            for gi in range(group):
                h = kv * group + gi
                hsl = slice(h * HEAD_DIM, (h + 1) * HEAD_DIM)
                dsb = ds_ref[h]
                dq = jnp.dot(dsb, kb[:, ksl], preferred_element_type=F32) * scale
                dq_ref[:, hsl] = dq.astype(BF16)
                dk_acc = dk_acc + lax.dot_general(dsb, q_ref[:, hsl].astype(BF16), (((0,), (0,)), ((), ())),
                                                  preferred_element_type=F32)
                dv_acc = dv_acc + lax.dot_general(p_ref[h], do_ref[:, hsl], (((0,), (0,)), ((), ())),
                                                  preferred_element_type=F32)
            dk_ref[pl.ds(start, 3 * BLOCK), ksl] += dk_acc * scale
            dv_ref[pl.ds(start, 3 * BLOCK), ksl] += dv_acc

    full_kv = pl.BlockSpec((T + 2 * BLOCK, KV_WIDTH), lambda n: (0, 0))
    bias_spec = pl.BlockSpec((H, BLOCK, 3 * BLOCK), lambda n: (0, 0, 0))
    row = pl.BlockSpec((BLOCK, AW), lambda n: (n, 0))
    q_cols = pl.BlockSpec((BLOCK, AW), lambda n: (n, 2))
    band = (H, BLOCK, 3 * BLOCK)
    return pl.pallas_call(
        body, name=name, grid=(T // BLOCK,),
        in_specs=[q_cols, full_kv, full_kv, bias_spec, pl.BlockSpec((H, 1, 1), lambda n: (0, 0, 0)), row, _ANY],
        out_specs=[q_cols, full_kv, full_kv, bias_spec, pl.BlockSpec((H, BLOCK, 1), lambda n: (0, 0, 0))],
        out_shape=[jax.ShapeDtypeStruct(dz.shape, dz.dtype),
                   jax.ShapeDtypeStruct((T + 2 * BLOCK, KV_WIDTH), F32), jax.ShapeDtypeStruct((T + 2 * BLOCK, KV_WIDTH), F32),
                   jax.ShapeDtypeStruct(band, F32), jax.ShapeDtypeStruct((H, BLOCK, 1), F32)],
        input_output_aliases={6: 0},
        scratch_shapes=[pltpu.VMEM(band, F32), pltpu.VMEM(band, F32), pltpu.VMEM(band, BF16), pltpu.VMEM(band, BF16)],
        compiler_params=_cparams("arbitrary"),
    )(z, kpad, vpad, bias, sink, do, dz)


def _dkv_into(dkp, dvp, dz, *, name):
    T = dz.shape[0]
    D = (dz.shape[1] - 2 * KV_WIDTH) * 2 // 7
    col = (D + D // 2) // (2 * KV_WIDTH)
    assert col * 2 * KV_WIDTH == D + D // 2

    def body(dk_ref, dv_ref, dz_in_ref, o_ref):
        o_ref[:, :KV_WIDTH] = dk_ref[...].astype(BF16)
        o_ref[:, KV_WIDTH:] = dv_ref[...].astype(BF16)

    kv = pl.BlockSpec((BLOCK, KV_WIDTH), lambda n: (n + 1, 0))
    return pl.pallas_call(
        body, name=name, grid=(T // BLOCK,),
        in_specs=[kv, kv, _ANY], out_specs=pl.BlockSpec((BLOCK, 2 * KV_WIDTH), lambda n: (n, col)),
        out_shape=jax.ShapeDtypeStruct(dz.shape, dz.dtype), input_output_aliases={2: 0},
        compiler_params=_cparams("parallel"),
    )(dkp, dvp, dz)


def _kv_pad(z, *, name):
    T = z.shape[0]
    D = (z.shape[1] - 2 * KV_WIDTH) * 2 // 7
    kcol = (D + D // 2) // KV_WIDTH
    nb = T // BLOCK

    def body(k_ref, v_ref, ko_ref, vo_ref):
        b = pl.program_id(0)
        inside = (b >= 1) & (b <= nb)
        ko_ref[...] = jnp.where(inside, k_ref[...].astype(F32), 0.0).astype(BF16)
        vo_ref[...] = jnp.where(inside, v_ref[...].astype(F32), 0.0).astype(BF16)

    out = jax.ShapeDtypeStruct((T + 2 * BLOCK, KV_WIDTH), BF16)
    o_spec = pl.BlockSpec((BLOCK, KV_WIDTH), lambda b: (b, 0))
    return pl.pallas_call(
        body, name=name, grid=(nb + 2,),
        in_specs=[pl.BlockSpec((BLOCK, KV_WIDTH), lambda b: (jnp.clip(b - 1, 0, nb - 1), kcol)),
                  pl.BlockSpec((BLOCK, KV_WIDTH), lambda b: (jnp.clip(b - 1, 0, nb - 1), kcol + 1))],
        out_specs=[o_spec, o_spec], out_shape=[out, out],
        compiler_params=_cparams("parallel"),
    )(z, z)


def _attn_small_grads(dbias, dsink_rows, bmap, after, *, name):
    H = dbias.shape[0]

    def body(dbias_ref, dsink_ref, bmap_ref, drel_ref, ds_ref):
        bm_ = bmap_ref[...]
        for h in range(H):
            d = dbias_ref[h]
            for b in range(REL_BUCKETS):
                drel_ref[b, h] = jnp.sum(jnp.where(bm_ == b, d, 0.0))
            ds_ref[0, h] = jnp.sum(dsink_ref[h])

    vmem = pl.BlockSpec(memory_space=pltpu.VMEM)
    smem = pl.BlockSpec(memory_space=pltpu.SMEM)
    body, in_specs, args = _ordered_after(body, 3, [vmem, vmem, vmem], (dbias, dsink_rows, bmap), after)
    return pl.pallas_call(
        body, name=name, in_specs=in_specs, out_specs=[smem, smem],
        out_shape=[jax.ShapeDtypeStruct((REL_BUCKETS, H), F32), jax.ShapeDtypeStruct((1, H), F32)],
    )(*args)


def _local_step(x, target, weight, emit, flush, norm_mix, v_gain, w_s, b_s, sink, rel_bias, norm_ffn, norm_final, early=()):
    T, D = x.shape
    ws_b = w_s.astype(BF16)
    bs_t = b_s.T
    bmap = jnp.asarray(_bucket_map())
    sink = sink.reshape(-1, 1, 1)

    bias = _bias_table(rel_bias, bmap, name="bias_table")
    h = _rms_fwd(x, norm_mix, name="rms_mix", after=[bias, *early])
    w_in = weight("w_in", h)
    z = _mm(h, w_in, tb=True, out_dtype=BF16, name="mm_z", bm=2048, bn=768)
    a = _sgu_fwd(z, v_gain, ws_b, bs_t, name="sgu_fwd")
    w_a = weight("w_a_out", a)
    ya = _mm_w8(a, w_a, name="mm_ya", bm=2048, out_dtype=BF16)
    kpad, vpad = _kv_pad(z, name="kv_pad")
    o = _attn_fwd(z, kpad, vpad, bias, sink, name="attn_fwd")
    w_b = weight("w_b_out", o)
    yb = _mm_w8(o, w_b, name="mm_yb", bm=2048, out_dtype=BF16)
    m = _merge_fwd(z, ya, yb, name="merge_fwd")
    w_o = weight("w_o", m)
    x1, h2 = _mm_resid_rms(m, w_o, x, norm_ffn, name="mm_x1_rms")
    w_gate = weight("w_gate", h2)
    gate = _mm(h2, w_gate, tb=True, name="mm_gate", bm=2048, bn=512)
    w_up = weight("w_up", gate)
    gate, up, act = _swiglu_mm_fwd(h2, w_up, gate, name="mm_up_swiglu")
    w_down = weight("w_down", act)
    x2 = _mm(act, w_down, name="mm_x2", add=x1, bm=1024, bn=512)
    loss, dx2b, g_norm_final = _loss_head(x2, norm_final, target, name="loss_head")

    g_w_down = _mm(act, dx2b, ta=True, out_dtype=BF16, name="mm_gwdown", bm=512, bn=2048)
    tok = emit(("w_down",), (g_w_down,))
    dgate, dup = _swiglu_mm_bwd(dx2b, w_down, gate, up, name="mm_dact_swiglu", after=tok)
    tok = flush(dgate)
    g_w_gate = _mm(dgate, h2, ta=True, out_dtype=BF16, name="mm_gwgate", bm=512, bn=2048, after=tok)
    g_w_up = _mm(dup, h2, ta=True, out_dtype=BF16, name="mm_gwup", bm=512, bn=2048)
    tok = emit(("w_gate", "w_up"), (g_w_gate, g_w_up))
    dh2 = _mm_sum2(dgate, w_gate, dup, w_up, name="mm_dh2", after=tok)
    tok = flush(dh2)
    dx1, dx1b, g_norm_ffn = _rms_bwd(x1, norm_ffn, dh2, dx2b, name="rms_ffn_bwd", want_bf16=True, after=tok)

    g_w_o = _mm(m, dx1b, ta=True, out_dtype=BF16, name="mm_gwo", bm=2048, bn=512)
    tok = emit(("w_o",), (g_w_o,))
    dm = _mm(dx1b, w_o, tb=True, out_dtype=BF16, name="mm_dm", bm=2048, bn=512, after=tok)
    tok = flush(dm)
    dy, dz = _merge_bwd(z, ya, yb, dm, name="merge_bwd", after=tok)
    g_w_a = _mm_gw8(a, dy, w_a.shape[2], name="mm_gwa", lead=0)
    g_w_b = _mm_gw8(o, dy, w_b.shape[2], name="mm_gwb", lead=1)
    tok = emit(("w_a_out", "w_b_out"), (g_w_a, g_w_b))
    da = _mm_w8t(dy, w_a, name="mm_da", bm=2048, bn=512, after=tok, lead=0)
    tok = flush(da)
    do = _mm_w8t(dy, w_b, out_dtype=BF16, name="mm_do", bm=2048, bn=512, after=tok, lead=1)
    dz, g_w_s, g_b_s_t, g_v_gain = _sgu_bwd(z, v_gain, ws_b, bs_t, da, dz, name="sgu_bwd")
    dz, dkp, dvp, dbias, dsink_rows = _attn_bwd(z, kpad, vpad, bias, sink, do, dz, name="attn_bwd")
    dz = _dkv_into(dkp, dvp, dz, name="dkv_into_dz")
    g_w_in = _mm(dz, h, ta=True, out_dtype=BF16, name="mm_gwin", bm=768, bn=2048)
    tok = emit(("w_in",), (g_w_in,))
    half = dict(bm=T // 2, bn=256)
    dh = _mm(dz, w_in, name="mm_dh_top", row_blocks=(0, 1), after=tok, **half)
    tok = flush(dh)
    dh = _mm(dz, w_in, name="mm_dh_bottom", row_blocks=(1, 1), into=dh, after=tok, **half)
    g_rel_bias, g_sink = _attn_small_grads(dbias, dsink_rows, bmap, dh, name="attn_small_grads")
    grad_x, g_norm_mix = _rms_bwd(x, norm_mix, dh, dx1, name="rms_mix_bwd", want_bf16=False)

    small = dict(norm_mix=g_norm_mix, sgu_v_gain=g_v_gain, sgu_w_s=g_w_s, sgu_b_s=g_b_s_t.T, attn_sink=g_sink,
                 rel_bias=g_rel_bias, norm_ffn=g_norm_ffn, norm_final=g_norm_final)
    return loss, grad_x, small


def _position():
    return lax.axis_index("x"), lax.axis_index("y"), lax.axis_index("c")


def _other_chips(x, y):
    return [(1 - x, y), (x, 1 - y), (1 - x, 1 - y)]


def _slot(px, py, pc):
    return 4 * px + 2 * py + pc


_HBM = pl.BlockSpec(memory_space=pltpu.HBM)
_SEM = pl.BlockSpec(memory_space=pltpu.SEMAPHORE)
_DATAFLOW = pltpu.SideEffectType.DATAFLOW_SIDE_EFFECTING


def _in_hbm(a):
    return pltpu.with_memory_space_constraint(a, pltpu.HBM)


def _own_slot(shard, pos, *, name, after=None):
    R, C = shard.shape
    tr = _div(R, 256, 16)

    def body(pos_ref, w_ref, o_ref):
        o_ref[...] = w_ref[...].astype(BF16)

    body, in_specs, args = _ordered_after(body, 2, [pl.BlockSpec((tr, C), lambda i, pos_ref: (i, 0))], (pos, shard), after)
    grid_spec = pltpu.PrefetchScalarGridSpec(
        num_scalar_prefetch=1, grid=(R // tr,), in_specs=in_specs,
        out_specs=pl.BlockSpec((None, tr, C), lambda i, pos_ref: (pos_ref[0], i, 0)))
    return pl.pallas_call(
        body, name=name, grid_spec=grid_spec,
        out_shape=jax.ShapeDtypeStruct((N_DEV, R, C), BF16),
        compiler_params=_cparams("parallel"),
    )(*args)


def _ag_copies(w, land_ref, send_sems, recv_sems):
    x, y, c = _position()
    mine = land_ref.at[_slot(x, y, c)]
    targets = [(px, py, c) for px, py in _other_chips(x, y)] + [(x, y, 1 - c)]
    return [pltpu.make_async_remote_copy(src_ref=mine, dst_ref=mine, send_sem=send_sems.at[4 * w + k],
                                         recv_sem=recv_sems.at[4 * w + k], device_id=to, device_id_type=MESH)
            for k, to in enumerate(targets)]


def _ag_start(buffers, groups, *, name, after=None):
    lands = [buffers[i] for g in groups for i in g]
    n, ng = len(lands), len(groups)
    sizes = [len(g) for g in groups]

    def body(*refs):
        land_refs = refs[:n]
        sems = refs[n:n + 2 * ng]
        token = refs[-1]
        i = 0
        for g in range(ng):
            for w in range(sizes[g]):
                for cp in _ag_copies(w, land_refs[i], sems[2 * g], sems[2 * g + 1]):
                    cp.start()
                i += 1
        token[...] = jnp.zeros_like(token)

    sem_shapes = [pltpu.SemaphoreType.DMA((4 * k,)) for k in sizes for _ in range(2)]
    body, in_specs, args = _ordered_after(body, n, [_HBM] * n, tuple(_in_hbm(a) for a in lands), after)
    outs = pl.pallas_call(
        body, name=name,
        in_specs=in_specs,
        out_specs=tuple([_SEM] * (2 * ng) + [_HBM] * n + [pl.BlockSpec(memory_space=pltpu.VMEM)]),
        out_shape=tuple(sem_shapes + [pltpu.HBM(a.shape, a.dtype) for a in lands] + [jax.ShapeDtypeStruct((8, LANES), F32)]),
        input_output_aliases={i: 2 * ng + i for i in range(n)},
        compiler_params=pltpu.CompilerParams(has_side_effects=_DATAFLOW),
    )(*args)
    sems, thru = outs[:2 * ng], outs[2 * ng:2 * ng + n]
    result, i = [], 0
    for g in range(ng):
        k = sizes[g]
        result.append((sems[2 * g], sems[2 * g + 1], list(thru[i:i + k])))
        i += k
    return result, outs[-1]


def _ag_wait(send_sems, recv_sems, lands, after, *, name):
    n = len(lands)

    def body(*refs):
        land_refs = refs[:n]
        send_ref, recv_ref = refs[n], refs[n + 1]
        token = refs[-1]
        for w in range(n):
            for cp in _ag_copies(w, land_refs[w], send_ref, recv_ref):
                cp.wait_send()
                cp.wait_recv()
        token[...] = jnp.zeros_like(token)

    outs = pl.pallas_call(
        body, name=name,
        in_specs=[_HBM] * n + [_SEM, _SEM, _ANY],
        out_specs=tuple([_HBM] * n + [pl.BlockSpec(memory_space=pltpu.VMEM)]),
        out_shape=tuple([pltpu.HBM(a.shape, a.dtype) for a in lands] + [jax.ShapeDtypeStruct((8, LANES), F32)]),
        input_output_aliases={i: i for i in range(n)},
        compiler_params=pltpu.CompilerParams(has_side_effects=_DATAFLOW),
    )(*lands, send_sems, recv_sems, after)
    return list(outs[:n]), outs[n]


def _ag_forward(lands, *, name, after=None):
    n = len(lands)

    def body(*refs):
        in_refs, out_refs = refs[:n], refs[n:2 * n]
        send_sems, recv_sems = refs[2 * n:]
        x, y, c = _position()
        copies = []
        for w in range(n):
            for k, (px, py) in enumerate(_other_chips(x, y)):
                cp = pltpu.make_async_remote_copy(
                    src_ref=in_refs[w].at[_slot(px, py, c)], dst_ref=out_refs[w].at[_slot(px, py, c)],
                    send_sem=send_sems.at[3 * w + k], recv_sem=recv_sems.at[3 * w + k],
                    device_id=(x, y, 1 - c), device_id_type=MESH)
                cp.start()
                copies.append(cp)
        for cp in copies:
            cp.wait()

    body, in_specs, args = _ordered_after(body, n, [_ANY] * n, tuple(lands), after)
    return pl.pallas_call(
        body, name=name,
        in_specs=in_specs, out_specs=[_ANY] * n,
        out_shape=[jax.ShapeDtypeStruct(a.shape, a.dtype) for a in lands],
        input_output_aliases={i: i for i in range(n)},
        scratch_shapes=[pltpu.SemaphoreType.DMA((3 * n,)), pltpu.SemaphoreType.DMA((3 * n,))],
    )(*args)


def _sibling_copies(w, g8_ref, land_ref, send_sems, recv_sems):
    x, y, c = _position()
    return [pltpu.make_async_remote_copy(src_ref=g8_ref.at[2 * p + (1 - c)], dst_ref=land_ref.at[p],
                                         send_sem=send_sems.at[4 * w + p], recv_sem=recv_sems.at[4 * w + p],
                                         device_id=(x, y, 1 - c), device_id_type=MESH)
            for p in range(4)]


def _chip_copies(w, sums_ref, land_ref, send_sems, recv_sems):
    x, y, c = _position()
    return [pltpu.make_async_remote_copy(src_ref=sums_ref.at[2 * px + py], dst_ref=land_ref.at[k],
                                         send_sem=send_sems.at[3 * w + k], recv_sem=recv_sems.at[3 * w + k],
                                         device_id=(px, py, c), device_id_type=MESH)
            for k, (px, py) in enumerate(_other_chips(x, y))]


def _copies_start(copies, per_weight, srcs, *, name):
    n = len(srcs)
    lands = [lax.empty((per_weight,) + s.shape[1:], s.dtype) for s in srcs]

    def body(*refs):
        src_refs, land_refs = refs[:n], refs[n:2 * n]
        send_sems, recv_sems = refs[2 * n], refs[2 * n + 1]
        token = refs[-1]
        for w in range(n):
            for cp in copies(w, src_refs[w], land_refs[w], send_sems, recv_sems):
                cp.start()
        token[...] = jnp.zeros_like(token)

    outs = pl.pallas_call(
        body, name=name,
        in_specs=[_HBM] * (2 * n),
        out_specs=tuple([_SEM, _SEM] + [_HBM] * (2 * n) + [pl.BlockSpec(memory_space=pltpu.VMEM)]),
        out_shape=tuple([pltpu.SemaphoreType.DMA((per_weight * n,)), pltpu.SemaphoreType.DMA((per_weight * n,))]
                        + [pltpu.HBM(a.shape, a.dtype) for a in srcs + lands] + [jax.ShapeDtypeStruct((8, LANES), F32)]),
        input_output_aliases={i: 2 + i for i in range(2 * n)},
        compiler_params=pltpu.CompilerParams(has_side_effects=_DATAFLOW),
    )(*[_in_hbm(a) for a in srcs + lands])
    return outs[0], outs[1], list(outs[2:2 + n]), list(outs[2 + n:2 + 2 * n]), outs[-1]


def _copies_wait(copies, send_sems, recv_sems, srcs, lands, after, *, name):
    n = len(srcs)

    def body(*refs):
        src_refs, land_refs = refs[:n], refs[n:2 * n]
        send_ref, recv_ref = refs[2 * n], refs[2 * n + 1]
        for w in range(n):
            for cp in copies(w, src_refs[w], land_refs[w], send_ref, recv_ref):
                cp.wait_send()
                cp.wait_recv()

    outs = pl.pallas_call(
        body, name=name,
        in_specs=[_HBM] * (2 * n) + [_SEM, _SEM, _ANY],
        out_specs=tuple([_HBM] * (2 * n)),
        out_shape=tuple(pltpu.HBM(a.shape, a.dtype) for a in srcs + lands),
        input_output_aliases={i: i for i in range(2 * n)},
        compiler_params=pltpu.CompilerParams(has_side_effects=_DATAFLOW),
    )(*srcs, *lands, send_sems, recv_sems, after)
    return list(outs[:n]), list(outs[n:])


def _chip_sums(g8, from_sibling, pos, *, name):
    _, R, C = g8.shape
    tr = _div(R, 512, 16)

    def body(pos_ref, g_ref, s_ref, o_ref):
        o_ref[...] = (g_ref[...].astype(F32) + s_ref[...].astype(F32)).astype(BF16)

    def chip(k, pos_ref):
        return jnp.where(k >= pos_ref[1], k + 1, k)

    grid_spec = pltpu.PrefetchScalarGridSpec(
        num_scalar_prefetch=1, grid=(3, R // tr),
        in_specs=[pl.BlockSpec((None, tr, C), lambda k, i, pos_ref: (2 * chip(k, pos_ref) + pos_ref[2], i, 0)),
                  pl.BlockSpec((None, tr, C), lambda k, i, pos_ref: (chip(k, pos_ref), i, 0))],
        out_specs=pl.BlockSpec((None, tr, C), lambda k, i, pos_ref: (chip(k, pos_ref), i, 0)))
    return pl.pallas_call(
        body, name=name, grid_spec=grid_spec,
        out_shape=jax.ShapeDtypeStruct((4, R, C), BF16),
        compiler_params=_cparams("parallel", "parallel"),
    )(pos, g8, from_sibling)


def _small_all_reduce(packed, after, *, name):
    R, L = packed.shape

    def body(x_ref, sum_ref, gath_ref, send_sems, recv_sems, local_sem):
        x, y, c = _position()
        me, sibling = (x, y, c), (x, y, 1 - c)
        chips = _other_chips(x, y)

        def rows(px, py, pc):
            return gath_ref.at[pl.ds(_slot(px, py, pc) * R, R), :]

        def copy(k, block, to, src=None):
            return pltpu.make_async_remote_copy(
                src_ref=rows(*block) if src is None else src, dst_ref=rows(*block),
                send_sem=send_sems.at[k], recv_sem=recv_sems.at[k], device_id=to, device_id_type=MESH)

        mine = pltpu.make_async_copy(x_ref, rows(*me), local_sem)
        mine.start()
        first = [copy(0, me, sibling, src=x_ref)]
        first += [copy(1 + j, me, (*chip, c), src=x_ref) for j, chip in enumerate(chips)]
        for cp in first:
            cp.start()
        passed = [copy(4 + j, (*chip, c), sibling) for j, chip in enumerate(chips)]
        for j, chip in enumerate(chips):
            copy(1 + j, (*chip, c), me).wait_recv()
            passed[j].start()
        copy(0, sibling, me).wait_recv()
        for j, chip in enumerate(chips):
            copy(4 + j, (*chip, 1 - c), me).wait_recv()
        for cp in first + passed:
            cp.wait_send()
        mine.wait()
        acc = gath_ref[0:R, :]
        for d in range(1, N_DEV):
            acc = acc + gath_ref[d * R:(d + 1) * R, :]
        sum_ref[...] = acc

    vmem = pl.BlockSpec(memory_space=pltpu.VMEM)
    body, in_specs, args = _ordered_after(body, 1, [vmem], (packed,), after)
    return pl.pallas_call(
        body, name=name, in_specs=in_specs, out_specs=vmem,
        out_shape=jax.ShapeDtypeStruct((R, L), F32),
        scratch_shapes=[pltpu.VMEM((N_DEV * R, L), F32), pltpu.SemaphoreType.DMA((7,)), pltpu.SemaphoreType.DMA((7,)),
                        pltpu.SemaphoreType.DMA],
        compiler_params=pltpu.CompilerParams(vmem_limit_bytes=VMEM_LIMIT),
    )(*args)


def _adamw_math(w, g, m, v):
    m = ADAM_B1 * m + (1.0 - ADAM_B1) * g
    v = ADAM_B2 * v + (1.0 - ADAM_B2) * (g * g)
    m_hat = m / (1.0 - ADAM_B1 ** ADAM_STEP)
    v_hat = v / (1.0 - ADAM_B2 ** ADAM_STEP)
    delta = -ADAM_LR * (m_hat / (jnp.sqrt(v_hat) + ADAM_EPS) + ADAM_WD * w)
    return delta, m, v


def _adamw_shard(w, m, v, g8, from_sibling, from_chips, pos, *, name):
    R, C = w.shape
    tr = _div(R, 256, 16)

    def body(pos_ref, w_ref, m_ref, v_ref, g_ref, s_ref, r_ref, go_ref, d_ref, mo_ref, vo_ref):
        g = g_ref[...].astype(F32) + s_ref[...].astype(F32)
        for k in range(3):
            g = g + r_ref[k].astype(F32)
        delta, m_, v_ = _adamw_math(w_ref[...], g, m_ref[...], v_ref[...])
        go_ref[...] = g
        d_ref[...] = delta
        mo_ref[...] = m_
        vo_ref[...] = v_

    blk = pl.BlockSpec((tr, C), lambda i, pos_ref: (i, 0))
    grid_spec = pltpu.PrefetchScalarGridSpec(
        num_scalar_prefetch=1, grid=(R // tr,),
        in_specs=[blk, blk, blk,
                  pl.BlockSpec((None, tr, C), lambda i, pos_ref: (pos_ref[0], i, 0)),
                  pl.BlockSpec((None, tr, C), lambda i, pos_ref: (pos_ref[1], i, 0)),
                  pl.BlockSpec((3, tr, C), lambda i, pos_ref: (0, i, 0))],
        out_specs=[blk] * 4)
    out = jax.ShapeDtypeStruct((R, C), F32)
    return pl.pallas_call(
        body, name=name, grid_spec=grid_spec, out_shape=[out] * 4,
        compiler_params=_cparams("parallel"),
    )(pos, w, m, v, g8, from_sibling, from_chips)


def _adamw_small(w, g, m, v, *, name):
    R, L = w.shape

    def body(w_ref, g_ref, m_ref, v_ref, d_ref, mo_ref, vo_ref):
        delta, m_, v_ = _adamw_math(w_ref[...], g_ref[...], m_ref[...], v_ref[...])
        d_ref[...] = delta
        mo_ref[...] = m_
        vo_ref[...] = v_

    vmem = pl.BlockSpec(memory_space=pltpu.VMEM)
    out = jax.ShapeDtypeStruct((R, L), F32)
    return pl.pallas_call(body, name=name, in_specs=[vmem] * 4, out_specs=[vmem] * 3, out_shape=[out] * 3)(w, g, m, v)


_TILE = 8 * LANES


def _pack(pieces):
    rows = []
    for p in pieces:
        flat = p.reshape(-1).astype(F32)
        padded = -(-flat.shape[0] // _TILE) * _TILE
        rows.append(jnp.pad(flat, (0, padded - flat.shape[0])).reshape(-1, LANES))
    return jnp.concatenate(rows, axis=0)


def _unpack(packed, like):
    out, r = [], 0
    for p in like:
        size = int(np.prod(p.shape)) if p.shape else 1
        nrows = -(-size // _TILE) * 8
        out.append(packed[r:r + nrows].reshape(-1)[:size].reshape(p.shape))
        r += nrows
    return out


_BIG = ("w_in", "w_a_out", "w_b_out", "w_o", "w_gate", "w_up", "w_down")
_TRANSPOSED = ("w_in", "w_gate", "w_up")
_COL_SHARDED = ("w_a_out", "w_b_out")
_GATHER_GROUPS = (("w_in",), ("w_a_out", "w_b_out", "w_o"), ("w_gate",), ("w_up",), ("w_down",))
_START_AFTER_WAIT = {0: (1, 2), 1: (3,), 2: (4,)}
_SMALL = ("norm_mix", "sgu_v_gain", "sgu_w_s", "sgu_b_s", "attn_sink", "rel_bias", "norm_ffn", "norm_final")
_ORDER = ("w_in", "norm_mix", "sgu_v_gain", "sgu_w_s", "sgu_b_s", "w_a_out", "attn_sink", "rel_bias", "w_b_out", "w_o",
          "norm_ffn", "w_gate", "w_up", "w_down", "norm_final")


def _shard(name, a):
    return jnp.swapaxes(a, 1, 2)[0] if name in _TRANSPOSED else a[0]


def _unshard(name, a):
    return jnp.swapaxes(a[None], 1, 2) if name in _TRANSPOSED else a[None]


def _whole(name, gathered):
    _, r, c = gathered.shape
    return gathered if name in _COL_SHARDED else gathered.reshape(N_DEV * r, c)


def _blocks(name, grad):
    if name in _COL_SHARDED:
        return grad
    r, c = grad.shape
    return grad.reshape(N_DEV, r // N_DEV, c)


def kernel(x, w_in, norm_mix, sgu_v_gain, sgu_w_s, sgu_b_s, w_a_out, attn_sink, rel_bias, w_b_out, w_o, norm_ffn, w_gate, w_up, w_down, norm_final, loss_target, m_w_in, m_norm_mix, m_sgu_v_gain, m_sgu_w_s, m_sgu_b_s, m_w_a_out, m_attn_sink, m_rel_bias, m_w_b_out, m_w_o, m_norm_ffn, m_w_gate, m_w_up, m_w_down, m_norm_final, v_w_in, v_norm_mix, v_sgu_v_gain, v_sgu_w_s, v_sgu_b_s, v_w_a_out, v_attn_sink, v_rel_bias, v_w_b_out, v_w_o, v_norm_ffn, v_w_gate, v_w_up, v_w_down, v_norm_final):
    w = dict(w_in=w_in, norm_mix=norm_mix, sgu_v_gain=sgu_v_gain, sgu_w_s=sgu_w_s, sgu_b_s=sgu_b_s, w_a_out=w_a_out,
             attn_sink=attn_sink, rel_bias=rel_bias, w_b_out=w_b_out, w_o=w_o, norm_ffn=norm_ffn, w_gate=w_gate,
             w_up=w_up, w_down=w_down, norm_final=norm_final)
    m = dict(w_in=m_w_in, norm_mix=m_norm_mix, sgu_v_gain=m_sgu_v_gain, sgu_w_s=m_sgu_w_s, sgu_b_s=m_sgu_b_s,
             w_a_out=m_w_a_out, attn_sink=m_attn_sink, rel_bias=m_rel_bias, w_b_out=m_w_b_out, w_o=m_w_o,
             norm_ffn=m_norm_ffn, w_gate=m_w_gate, w_up=m_w_up, w_down=m_w_down, norm_final=m_norm_final)
    v = dict(w_in=v_w_in, norm_mix=v_norm_mix, sgu_v_gain=v_sgu_v_gain, sgu_w_s=v_sgu_w_s, sgu_b_s=v_sgu_b_s,
             w_a_out=v_w_a_out, attn_sink=v_attn_sink, rel_bias=v_rel_bias, w_b_out=v_w_b_out, w_o=v_w_o,
             norm_ffn=v_norm_ffn, w_gate=v_w_gate, w_up=v_w_up, w_down=v_w_down, norm_final=v_norm_final)
    xc, yc, cc = _position()
    pos = jnp.stack([_slot(xc, yc, cc), 2 * xc + yc, cc]).astype(jnp.int32)

    in_flight, full, slots = {}, {}, {}

    def start_gather(groups, after):
        names = [n for gi in groups for n in _GATHER_GROUPS[gi]]
        flights, token = _ag_start([slots[n] for n in names], [[names.index(n) for n in _GATHER_GROUPS[gi]] for gi in groups],
                                   name="ag_start_%d" % groups[0], after=after)
        in_flight.update(zip(groups, flights))
        return token

    def weight(name, after):
        if name not in full:
            gi = next(i for i, grp in enumerate(_GATHER_GROUPS) if name in grp)
            send_sems, recv_sems, lands = in_flight[gi]
            lands, token = _ag_wait(send_sems, recv_sems, lands, after, name="ag_wait_%d" % gi)
            started = start_gather(_START_AFTER_WAIT[gi], token) if gi in _START_AFTER_WAIT else None
            gathered = _ag_forward(lands, name="ag_forward_%d" % gi, after=started)
            full.update({n: _whole(n, g) for n, g in zip(_GATHER_GROUPS[gi], gathered)})
        return full[name]

    for n in _GATHER_GROUPS[0]:
        slots[n] = _own_slot(_shard(n, w[n]), pos, name="own_slot_" + n)
    first_started = start_gather((0,), None)
    for grp in _GATHER_GROUPS[1:]:
        for n in grp:
            slots[n] = _own_slot(_shard(n, w[n]), pos, name="own_slot_" + n, after=first_started)

    to_sibling, reducing = [], {}

    def emit(names, grads):
        g8 = [_blocks(n, g) for n, g in zip(names, grads)]
        send_sems, recv_sems, g8, lands, token = _copies_start(_sibling_copies, 4, g8, name="rs_sibling_start_" + names[0])
        to_sibling.append((names, send_sems, recv_sems, g8, lands))
        return token

    def flush(after):
        names, send_sems, recv_sems, g8, lands = to_sibling.pop()
        g8, from_sibling = _copies_wait(_sibling_copies, send_sems, recv_sems, g8, lands, after,
                                        name="rs_sibling_wait_" + names[0])
        sums4 = [_chip_sums(g, s, pos, name="chip_sums_" + n) for n, g, s in zip(names, g8, from_sibling)]
        send_sems, recv_sems, sums4, lands, token = _copies_start(_chip_copies, 3, sums4, name="rs_chips_start_" + names[0])
        reducing[names] = (g8, from_sibling, send_sems, recv_sems, sums4, lands)
        return token

    loss, grad_x, small_grads_local = _local_step(
        x[0], loss_target[0], weight, emit, flush, norm_mix, sgu_v_gain, sgu_w_s[0], sgu_b_s[0], attn_sink, rel_bias,
        norm_ffn, norm_final[None], early=[slots[n] for grp in _GATHER_GROUPS[1:] for n in grp])

    out_g, out_d, out_m, out_v = {}, {}, {}, {}
    small_like = [w[n] for n in _SMALL]
    small_w = _pack(small_like)
    packed = _pack([small_grads_local[n] for n in _SMALL] + [loss[0, 0]])
    after = grad_x
    for gi, (names, (g8, from_sibling, send_sems, recv_sems, sums4, lands)) in enumerate(reducing.items()):
        if gi == len(reducing) - 1:
            summed = _small_all_reduce(packed, after, name="small_all_reduce")
            after = summed
        _, from_chips = _copies_wait(_chip_copies, send_sems, recv_sems, sums4, lands, after,
                                     name="rs_chips_wait_" + names[0])
        for i, n in enumerate(names):
            g, d, m_, v_ = _adamw_shard(_shard(n, w[n]), _shard(n, m[n]), _shard(n, v[n]), g8[i], from_sibling[i],
                                        from_chips[i], pos, name="adamw_" + n)
            out_g[n], out_d[n], out_m[n], out_v[n] = (_unshard(n, o) for o in (g, d, m_, v_))
            after = d
    *small_grads, loss_sum = _unpack(summed, small_like + [jax.ShapeDtypeStruct((), F32)])
    d_s, m_s, v_s = _adamw_small(small_w, summed[:small_w.shape[0]], _pack([m[n] for n in _SMALL]),
                                 _pack([v[n] for n in _SMALL]), name="adamw_small")
    for n, g, d, m_, v_ in zip(_SMALL, small_grads, _unpack(d_s, small_like), _unpack(m_s, small_like), _unpack(v_s, small_like)):
        out_g[n], out_d[n], out_m[n], out_v[n] = g, d, m_, v_

    return (loss_sum, grad_x[None], *[out_g[n] for n in _ORDER], *[out_d[n] for n in _ORDER],
            *[out_m[n] for n in _ORDER], *[out_v[n] for n in _ORDER])
```

```python
import functools
import math

import numpy as np
import jax
import jax.numpy as jnp
from jax import lax
from jax.experimental import pallas as pl
from jax.experimental.pallas import tpu as pltpu

F32 = jnp.float32
BF16 = jnp.bfloat16

EPS = 1e-6
NEG = -1e30
HEAD_DIM = 128
BLOCK = 128
N_KV_HEADS = 2
KV_WIDTH = N_KV_HEADS * HEAD_DIM
REL_BUCKETS = 32
REL_MAX_DIST = 128

ADAM_LR = 0.001
ADAM_B1 = 0.9
ADAM_B2 = 0.999
ADAM_EPS = 1e-08
ADAM_WD = 0.01
ADAM_STEP = 10

N_DEV = 8
LANES = 128
VMEM_LIMIT = 56 * 1024 * 1024
MESH = pl.DeviceIdType.MESH


def _cparams(*sem):
    return pltpu.CompilerParams(dimension_semantics=sem, vmem_limit_bytes=VMEM_LIMIT)


def _div(n, target, mult=LANES):
    best = None
    for d in range(mult, min(n, target) + 1, mult):
        if n % d == 0:
            best = d
    assert best is not None, (n, target, mult)
    return best


_ANY = pl.BlockSpec(memory_space=pl.ANY)


def _ordered_after(body, n_inputs, in_specs, args, after):
    if after is None:
        return body, in_specs, args
    extra = tuple(after) if isinstance(after, (tuple, list)) else (after,)

    def wrapped(*refs):
        return body(*refs[:n_inputs], *refs[n_inputs + len(extra):])

    return wrapped, list(in_specs) + [_ANY] * len(extra), tuple(args) + extra


def _bucket_map():
    nb = REL_BUCKETS // 2
    qi = np.arange(BLOCK)[:, None]
    kj = np.arange(3 * BLOCK)[None, :]
    rel = kj - BLOCK - qi
    ret = np.where(rel > 0, nb, 0)
    n = np.abs(rel)
    max_exact = nb // 2
    nf = np.maximum(n, 1).astype(np.float32)
    large = max_exact + (np.log(nf / np.float32(max_exact)) / np.float32(math.log(REL_MAX_DIST / max_exact))
                         * np.float32(nb - max_exact)).astype(np.int32)
    large = np.minimum(large, nb - 1)
    return (ret + np.where(n < max_exact, n, large)).astype(np.int32)


_GELU_C = math.sqrt(2.0 / math.pi)
_GELU_A = 0.044715


def _gelu(x):
    t = jnp.tanh(_GELU_C * (x + _GELU_A * (x * x * x)))
    return 0.5 * x * (1.0 + t)


def _gelu_and_grad(x):
    x2 = x * x
    t = jnp.tanh(_GELU_C * (x + _GELU_A * (x2 * x)))
    g = 0.5 * x * (1.0 + t)
    dg = 0.5 * (1.0 + t) + 0.5 * x * (1.0 - t * t) * (_GELU_C * (1.0 + 3.0 * _GELU_A * x2))
    return g, dg


def _sigmoid(x):
    return 1.0 / (1.0 + jnp.exp(-x))


def _mm(a, b, *, name, ta=False, tb=False, add=None, out_dtype=F32, bm=1024, bn=1024, bk=None, after=None,
        row_blocks=None, into=None):
    if ta:
        K, M = a.shape
    else:
        M, K = a.shape
    N = b.shape[0] if tb else b.shape[1]
    assert (b.shape[1] if tb else b.shape[0]) == K
    bm = _div(M, bm)
    bn = _div(N, bn)
    bk = K if bk is None else _div(K, bk)
    nk = K // bk
    i0, ni = (0, M // bm) if row_blocks is None else row_blocks
    a_spec = (pl.BlockSpec((bk, bm), lambda i, j, k: (k, i + i0)) if ta
              else pl.BlockSpec((bm, bk), lambda i, j, k: (i + i0, k)))
    b_spec = pl.BlockSpec((bn, bk), lambda i, j, k: (j, k)) if tb else pl.BlockSpec((bk, bn), lambda i, j, k: (k, j))
    o_spec = pl.BlockSpec((bm, bn), lambda i, j, k: (i + i0, j))
    dims = (((0 if ta else 1,), (1 if tb else 0,)), ((), ()))
    has_add = add is not None

    def body(*refs):
        if has_add:
            a_ref, b_ref, add_ref, o_ref, *scratch = refs
        else:
            a_ref, b_ref, o_ref, *scratch = refs
            add_ref = None
        p = lax.dot_general(a_ref[...].astype(BF16), b_ref[...].astype(BF16), dims, preferred_element_type=F32)
        if nk == 1:
            if has_add:
                p = p + add_ref[...]
            o_ref[...] = p.astype(out_dtype)
        else:
            acc = scratch[0]
            k = pl.program_id(2)

            @pl.when(k == 0)
            def _():
                acc[...] = p

            @pl.when(k > 0)
            def _():
                acc[...] += p

            @pl.when(k == nk - 1)
            def _():
                r = acc[...]
                if has_add:
                    r = r + add_ref[...]
                o_ref[...] = r.astype(out_dtype)

    in_specs = [a_spec, b_spec] + ([o_spec] if has_add else [])
    args = (a, b) + ((add,) if has_add else ())
    aliases = {}
    if into is not None:
        body, in_specs, args = _ordered_after(body, len(args), in_specs, args, into)
        aliases = {len(args) - 1: 0}
    body, in_specs, args = _ordered_after(body, len(args), in_specs, args, after)
    return pl.pallas_call(
        body, name=name, grid=(ni, N // bn, nk),
        in_specs=in_specs, out_specs=o_spec,
        out_shape=jax.ShapeDtypeStruct((M, N), out_dtype),
        input_output_aliases=aliases,
        scratch_shapes=[pltpu.VMEM((bm, bn), F32)] if nk > 1 else [],
        compiler_params=_cparams("parallel", "parallel", "arbitrary"),
    )(*args)


def _mm_resid_rms(a, b, resid, gain, *, name, bm=512):
    M, K = a.shape
    N = b.shape[1]
    bm = _div(M, bm)

    def body(a_ref, b_ref, r_ref, g_ref, x_ref, h_ref):
        x = r_ref[...] + jnp.dot(a_ref[...], b_ref[...], preferred_element_type=F32)
        x_ref[...] = x
        r = lax.rsqrt(jnp.mean(x * x, axis=-1, keepdims=True) + EPS)
        h_ref[...] = ((x * r) * g_ref[...]).astype(BF16)

    row = pl.BlockSpec((bm, N), lambda i: (i, 0))
    return pl.pallas_call(
        body, name=name, grid=(M // bm,),
        in_specs=[pl.BlockSpec((bm, K), lambda i: (i, 0)), pl.BlockSpec((K, N), lambda i: (0, 0)), row,
                  pl.BlockSpec((1, N), lambda i: (0, 0))],
        out_specs=[row, row], out_shape=[jax.ShapeDtypeStruct((M, N), F32), jax.ShapeDtypeStruct((M, N), BF16)],
        compiler_params=_cparams("parallel"),
    )(a, b, resid, gain)


def _mm_sum2(a1, b1, a2, b2, *, name, bm=1024, bn=512, bk=2816, after=None):
    M, K = a1.shape
    N = b1.shape[1]
    bm, bn, bk = _div(M, bm), _div(N, bn), _div(K, bk)
    nk = K // bk

    def body(a1_ref, b1_ref, a2_ref, b2_ref, o_ref, acc):
        p = (jnp.dot(a1_ref[...], b1_ref[...], preferred_element_type=F32)
             + jnp.dot(a2_ref[...], b2_ref[...], preferred_element_type=F32))
        k = pl.program_id(2)

        @pl.when(k == 0)
        def _():
            acc[...] = p

        @pl.when(k > 0)
        def _():
            acc[...] += p

        @pl.when(k == nk - 1)
        def _():
            o_ref[...] = acc[...]

    a_spec = pl.BlockSpec((bm, bk), lambda i, j, k: (i, k))
    b_spec = pl.BlockSpec((bk, bn), lambda i, j, k: (k, j))
    body, in_specs, args = _ordered_after(body, 4, [a_spec, b_spec, a_spec, b_spec], (a1, b1, a2, b2), after)
    return pl.pallas_call(
        body, name=name, grid=(M // bm, N // bn, nk),
        in_specs=in_specs, out_specs=pl.BlockSpec((bm, bn), lambda i, j, k: (i, j)),
        out_shape=jax.ShapeDtypeStruct((M, N), F32),
        scratch_shapes=[pltpu.VMEM((bm, bn), F32)],
        compiler_params=_cparams("parallel", "parallel", "arbitrary"),
    )(*args)


def _blocks_per_tile(c):
    nb = 1
    while (nb * c) % LANES or (nb * c < 1024 and nb < N_DEV):
        nb *= 2
    assert nb <= N_DEV and (nb * c) % LANES == 0, c
    return nb


def _mm_w8(a, w8, *, name, bm=1024, out_dtype=F32):
    M, K = a.shape
    _, _, c = w8.shape
    nb = _blocks_per_tile(c)
    bm = _div(M, bm)

    def body(a_ref, w_ref, o_ref):
        a_ = a_ref[...]
        for t in range(nb):
            o_ref[:, t * c:(t + 1) * c] = jnp.dot(a_, w_ref[t], preferred_element_type=F32).astype(out_dtype)

    return pl.pallas_call(
        body, name=name, grid=(M // bm, N_DEV // nb),
        in_specs=[pl.BlockSpec((bm, K), lambda i, j: (i, 0)), pl.BlockSpec((nb, K, c), lambda i, j: (j, 0, 0))],
        out_specs=pl.BlockSpec((bm, nb * c), lambda i, j: (i, j)),
        out_shape=jax.ShapeDtypeStruct((M, N_DEV * c), out_dtype),
        compiler_params=_cparams("parallel", "parallel"),
    )(a, w8)


def _mm_w8t(dy, w8, *, name, add=None, out_dtype=F32, bm=1024, bn=1024, after=None, lead=None):
    M = dy.shape[-2]
    _, K, c = w8.shape
    nb = _blocks_per_tile(c)
    nk = N_DEV // nb
    bm, bn = _div(M, bm), _div(K, bn)
    has_add = add is not None
    dims = (((1,), (1,)), ((), ()))

    def body(*refs):
        if has_add:
            dy_ref, w_ref, add_ref, o_ref, acc = refs
        else:
            dy_ref, w_ref, o_ref, acc = refs
        p = lax.dot_general(dy_ref[:, 0:c], w_ref[0], dims, preferred_element_type=F32)
        for t in range(1, nb):
            p = p + lax.dot_general(dy_ref[:, t * c:(t + 1) * c], w_ref[t], dims, preferred_element_type=F32)
        k = pl.program_id(2)

        @pl.when(k == 0)
        def _():
            acc[...] = p

        @pl.when(k > 0)
        def _():
            acc[...] += p

        @pl.when(k == nk - 1)
        def _():
            r = acc[...]
            if has_add:
                r = r + add_ref[...]
            o_ref[...] = r.astype(out_dtype)

    o_spec = pl.BlockSpec((bm, bn), lambda i, j, k: (i, j))
    dy_spec = (pl.BlockSpec((bm, nb * c), lambda i, j, k: (i, k)) if lead is None
               else pl.BlockSpec((None, bm, nb * c), lambda i, j, k: (lead, i, k)))
    in_specs = [dy_spec, pl.BlockSpec((nb, bn, c), lambda i, j, k: (k, j, 0))]
    in_specs += [o_spec] if has_add else []
    args = (dy, w8) + ((add,) if has_add else ())
    body, in_specs, args = _ordered_after(body, len(args), in_specs, args, after)
    return pl.pallas_call(
        body, name=name, grid=(M // bm, K // bn, nk),
        in_specs=in_specs, out_specs=o_spec,
        out_shape=jax.ShapeDtypeStruct((M, K), out_dtype),
        scratch_shapes=[pltpu.VMEM((bm, bn), F32)],
        compiler_params=_cparams("parallel", "parallel", "arbitrary"),
    )(*args)


def _mm_gw8(x, dy, c, *, name, bk=1024, lead=None):
    T, K = x.shape
    nb = _blocks_per_tile(c)
    bk = _div(K, bk)
    dims = (((0,), (0,)), ((), ()))

    def body(x_ref, dy_ref, o_ref):
        x_ = x_ref[...]
        for t in range(nb):
            o_ref[t] = lax.dot_general(x_, dy_ref[:, t * c:(t + 1) * c], dims, preferred_element_type=F32).astype(BF16)

    dy_spec = (pl.BlockSpec((T, nb * c), lambda i, j: (0, j)) if lead is None
               else pl.BlockSpec((None, T, nb * c), lambda i, j: (lead, 0, j)))
    return pl.pallas_call(
        body, name=name, grid=(K // bk, N_DEV // nb),
        in_specs=[pl.BlockSpec((T, bk), lambda i, j: (0, i)), dy_spec],
        out_specs=pl.BlockSpec((nb, bk, c), lambda i, j: (j, i, 0)),
        out_shape=jax.ShapeDtypeStruct((N_DEV, K, c), BF16),
        compiler_params=_cparams("parallel", "parallel"),
    )(x, dy)


def _rms_fwd(x, g, *, name, after=None):
    T, D = x.shape
    tm = _div(T, 256, 8)

    def body(x_ref, g_ref, h_ref):
        xf = x_ref[...]
        r = lax.rsqrt(jnp.mean(xf * xf, axis=-1, keepdims=True) + EPS)
        h_ref[...] = ((xf * r) * g_ref[...]).astype(BF16)

    in_specs = [pl.BlockSpec((tm, D), lambda i: (i, 0)), pl.BlockSpec((1, D), lambda i: (0, 0))]
    body, in_specs, args = _ordered_after(body, 2, in_specs, (x, g), after)
    return pl.pallas_call(
        body, name=name, grid=(T // tm,),
        in_specs=in_specs,
        out_specs=pl.BlockSpec((tm, D), lambda i: (i, 0)),
        out_shape=jax.ShapeDtypeStruct((T, D), BF16),
        compiler_params=_cparams("parallel"),
    )(*args)


def _rms_bwd(x, g, dh, dres, *, name, want_bf16, after=None):
    T, D = x.shape
    tm = _div(T, 256, 8)

    def body(x_ref, g_ref, dh_ref, dres_ref, dx_ref, *rest):
        if want_bf16:
            dxb_ref, dg_ref = rest
        else:
            (dg_ref,) = rest
        xf = x_ref[...]
        r = lax.rsqrt(jnp.mean(xf * xf, axis=-1, keepdims=True) + EPS)
        xhat = xf * r
        dh_ = dh_ref[...]
        dy = dh_ * g_ref[...]
        dx = dres_ref[...].astype(F32) + r * (dy - xhat * jnp.mean(dy * xhat, axis=-1, keepdims=True))
        dx_ref[...] = dx
        if want_bf16:
            dxb_ref[...] = dx.astype(BF16)
        part = jnp.sum(dh_ * xhat, axis=0, keepdims=True)

        @pl.when(pl.program_id(0) == 0)
        def _():
            dg_ref[...] = part

        @pl.when(pl.program_id(0) > 0)
        def _():
            dg_ref[...] += part

    row = pl.BlockSpec((tm, D), lambda i: (i, 0))
    vec = pl.BlockSpec((1, D), lambda i: (0, 0))
    out_specs = [row] + ([row] if want_bf16 else []) + [vec]
    out_shape = ([jax.ShapeDtypeStruct((T, D), F32)] + ([jax.ShapeDtypeStruct((T, D), BF16)] if want_bf16 else [])
                 + [jax.ShapeDtypeStruct((1, D), F32)])
    body, in_specs, args = _ordered_after(body, 4, [row, vec, row, row], (x, g, dh, dres), after)
    return pl.pallas_call(
        body, name=name, grid=(T // tm,),
        in_specs=in_specs, out_specs=out_specs, out_shape=out_shape,
        compiler_params=_cparams("arbitrary"),
    )(*args)


def _loss_head(x, g, target, *, name):
    T, D = x.shape
    tm = _div(T, 256, 16)

    def body(x_ref, g_ref, t_ref, loss_ref, dxb_ref, dg_ref):
        xf = x_ref[...]
        r = lax.rsqrt(jnp.mean(xf * xf, axis=-1, keepdims=True) + EPS)
        xhat = xf * r
        gain = g_ref[...]
        err = xhat * gain - t_ref[...]
        lpart = 0.5 * jnp.sum(jnp.mean(err * err, axis=-1, keepdims=True), axis=0, keepdims=True)
        dh_ = err * (1.0 / D)
        dy = dh_ * gain
        dx = r * (dy - xhat * jnp.mean(dy * xhat, axis=-1, keepdims=True))
        dxb_ref[...] = dx.astype(BF16)
        part = jnp.sum(dh_ * xhat, axis=0, keepdims=True)

        @pl.when(pl.program_id(0) == 0)
        def _():
            dg_ref[...] = part
            loss_ref[...] = jnp.broadcast_to(lpart, loss_ref.shape)

        @pl.when(pl.program_id(0) > 0)
        def _():
            dg_ref[...] += part
            loss_ref[...] += jnp.broadcast_to(lpart, loss_ref.shape)

    row = pl.BlockSpec((tm, D), lambda i: (i, 0))
    vec = pl.BlockSpec((1, D), lambda i: (0, 0))
    return pl.pallas_call(
        body, name=name, grid=(T // tm,),
        in_specs=[row, vec, row],
        out_specs=[pl.BlockSpec((8, LANES), lambda i: (0, 0)), row, vec],
        out_shape=[jax.ShapeDtypeStruct((8, LANES), F32), jax.ShapeDtypeStruct((T, D), BF16), jax.ShapeDtypeStruct((1, D), F32)],
        compiler_params=_cparams("arbitrary"),
    )(x, g, target)


def _gate_cols(D):
    off_a = 3 * D // 2 + 2 * KV_WIDTH
    off_b = off_a + D
    cw = math.gcd(math.gcd(off_a, off_b), math.gcd(D, 512))
    return cw, off_a // cw, off_b // cw


def _merge_fwd(z, ya, yb, *, name):
    T, D = ya.shape
    cw, ba, bb = _gate_cols(D)
    tm = _div(T, 512, 8)

    def body(ga_ref, gb_ref, ya_ref, yb_ref, m_ref):
        m_ref[...] = (_sigmoid(ga_ref[...].astype(F32)) * ya_ref[...]
                      + _sigmoid(gb_ref[...].astype(F32)) * yb_ref[...]).astype(BF16)

    blk = pl.BlockSpec((tm, cw), lambda i, j: (i, j))
    return pl.pallas_call(
        body, name=name, grid=(T // tm, D // cw),
        in_specs=[pl.BlockSpec((tm, cw), lambda i, j: (i, ba + j)), pl.BlockSpec((tm, cw), lambda i, j: (i, bb + j)), blk, blk],
        out_specs=blk, out_shape=jax.ShapeDtypeStruct((T, D), BF16),
        compiler_params=_cparams("parallel", "parallel"),
    )(z, z, ya, yb)


def _merge_bwd(z, ya, yb, dm, *, name, after=None):
    T, D = ya.shape
    cw, ba, bb = _gate_cols(D)
    nj = D // cw
    assert bb == ba + nj
    tm = _div(T, 512, 8)

    def body(g_ref, ya_ref, yb_ref, dm_ref, dy_ref, dz_ref):
        sig = _sigmoid(g_ref[...].astype(F32))
        dm_ = dm_ref[...]
        y = jnp.where(pl.program_id(1) == 0, ya_ref[...], yb_ref[...])
        dy_ref[...] = (dm_ * sig).astype(BF16)
        dz_ref[...] = (dm_ * y * (sig * (1.0 - sig))).astype(BF16)

    in_specs = [pl.BlockSpec((tm, cw), lambda i, s, j: (i, ba + s * nj + j)),
                pl.BlockSpec((tm, cw), lambda i, s, j: (i, j * (1 - s))),
                pl.BlockSpec((tm, cw), lambda i, s, j: (i, j * s)),
                pl.BlockSpec((tm, cw), lambda i, s, j: (i, j))]
    body, in_specs, args = _ordered_after(body, 4, in_specs, (z, ya, yb, dm), after)
    return pl.pallas_call(
        body, name=name, grid=(T // tm, 2, nj),
        in_specs=in_specs,
        out_specs=[pl.BlockSpec((None, tm, cw), lambda i, s, j: (s, i, j)),
                   pl.BlockSpec((tm, cw), lambda i, s, j: (i, ba + s * nj + j))],
        out_shape=[jax.ShapeDtypeStruct((2, T, D), BF16), jax.ShapeDtypeStruct(z.shape, BF16)],
        compiler_params=_cparams("parallel", "arbitrary", "arbitrary"),
    )(*args)


def _swiglu_mm_fwd(h, wu_t, gate, *, name, bm=1024, bn=512):
    T, D = h.shape
    F = wu_t.shape[0]
    bm, bn = _div(T, bm), _div(F, bn)

    rc = _div(bm, 256, 16)

    def body(h_ref, wu_ref, gin_ref, g_ref, u_ref, act_ref):
        w = wu_ref[...]
        for r in range(0, bm, rc):
            rows = slice(r, r + rc)
            u = lax.dot_general(h_ref[rows, :], w, (((1,), (1,)), ((), ())), preferred_element_type=F32)
            g = gin_ref[rows, :]
            g_ref[rows, :] = g.astype(BF16)
            u_ref[rows, :] = u.astype(BF16)
            act_ref[rows, :] = (g * _sigmoid(g) * u).astype(BF16)

    o_spec = pl.BlockSpec((bm, bn), lambda i, j: (i, j))
    return pl.pallas_call(
        body, name=name, grid=(T // bm, F // bn),
        in_specs=[pl.BlockSpec((bm, D), lambda i, j: (i, 0)), pl.BlockSpec((bn, D), lambda i, j: (j, 0)), o_spec],
        out_specs=[o_spec] * 3, out_shape=[jax.ShapeDtypeStruct((T, F), BF16)] * 3,
        compiler_params=_cparams("parallel", "parallel"),
    )(h, wu_t, gate)


def _swiglu_mm_bwd(dx, w_down, gate, up, *, name, bm=2048, bn=512, after=None):
    T, D = dx.shape
    F = w_down.shape[0]
    bm, bn = _div(T, bm), _div(F, bn)
    dims = (((1,), (1,)), ((), ()))

    rc = _div(bm, 256, 16)

    def body(dx_ref, w_ref, g_ref, u_ref, dg_ref, du_ref):
        w = w_ref[...]
        for r in range(0, bm, rc):
            rows = slice(r, r + rc)
            d = lax.dot_general(dx_ref[rows, :], w, dims, preferred_element_type=F32)
            g = g_ref[rows, :].astype(F32)
            s = _sigmoid(g)
            silu = g * s
            dg_ref[rows, :] = (d * u_ref[rows, :].astype(F32) * (s + silu * (1.0 - s))).astype(BF16)
            du_ref[rows, :] = (d * silu).astype(BF16)

    o_spec = pl.BlockSpec((bm, bn), lambda i, j: (i, j))
    in_specs = [pl.BlockSpec((bm, D), lambda i, j: (i, 0)), pl.BlockSpec((bn, D), lambda i, j: (j, 0)), o_spec, o_spec]
    body, in_specs, args = _ordered_after(body, 4, in_specs, (dx, w_down, gate, up), after)
    out = jax.ShapeDtypeStruct((T, F), BF16)
    return pl.pallas_call(
        body, name=name, grid=(T // bm, F // bn), in_specs=in_specs, out_specs=[o_spec, o_spec], out_shape=[out, out],
        compiler_params=_cparams("parallel", "parallel"),
    )(*args)


def _sgu_fwd(z, gain, ws_b, bs_t, *, name):
    T = z.shape[0]
    SW = gain.shape[1]
    G = SW // BLOCK

    def body(zu_ref, zv_ref, gain_ref, ws_ref, bs_ref, a_ref):
        u = _gelu(zu_ref[...].astype(F32))
        vg = _gelu(zv_ref[...].astype(F32))
        r = lax.rsqrt(jnp.mean(vg * vg, axis=-1, keepdims=True) + EPS)
        vn = ((vg * r) * gain_ref[...]).astype(BF16)
        for g in range(G):
            sl = slice(g * BLOCK, (g + 1) * BLOCK)
            mixed = jnp.dot(ws_ref[g], vn[:, sl], preferred_element_type=F32) + bs_ref[:, g:g + 1]
            a_ref[:, sl] = (u[:, sl] * mixed).astype(BF16)

    return pl.pallas_call(
        body, name=name, grid=(T // BLOCK,),
        in_specs=[pl.BlockSpec((BLOCK, SW), lambda c: (c, 0)), pl.BlockSpec((BLOCK, SW), lambda c: (c, 1)),
                  pl.BlockSpec((1, SW), lambda c: (0, 0)), pl.BlockSpec((G, BLOCK, BLOCK), lambda c: (0, 0, 0)),
                  pl.BlockSpec((BLOCK, G), lambda c: (0, 0))],
        out_specs=pl.BlockSpec((BLOCK, SW), lambda c: (c, 0)),
        out_shape=jax.ShapeDtypeStruct((T, SW), BF16),
        compiler_params=_cparams("parallel"),
    )(z, z, gain, ws_b, bs_t)


def _sgu_bwd(z, gain, ws_b, bs_t, da, dz, *, name):
    T = z.shape[0]
    SW = gain.shape[1]
    G = SW // BLOCK

    def body(zu_ref, zv_ref, gain_ref, ws_ref, bs_ref, da_ref, dz_in_ref, dz_ref, dws_ref, dbs_ref, dgain_ref, dvn_ref):
        first = pl.program_id(0) == 0

        @pl.when(first)
        def _():
            dws_ref[...] = jnp.zeros_like(dws_ref)
            dbs_ref[...] = jnp.zeros_like(dbs_ref)
            dgain_ref[...] = jnp.zeros_like(dgain_ref)

        u, du = _gelu_and_grad(zu_ref[...].astype(F32))
        vg, dvg = _gelu_and_grad(zv_ref[...].astype(F32))
        r = lax.rsqrt(jnp.mean(vg * vg, axis=-1, keepdims=True) + EPS)
        xhat = vg * r
        gain_ = gain_ref[...]
        vn = (xhat * gain_).astype(BF16)
        da_ = da_ref[...]
        for g in range(G):
            sl = slice(g * BLOCK, (g + 1) * BLOCK)
            w = ws_ref[g]
            mixed = jnp.dot(w, vn[:, sl], preferred_element_type=F32) + bs_ref[:, g:g + 1]
            dmix = da_[:, sl] * u[:, sl]
            dz_ref[:, sl] = (da_[:, sl] * mixed * du[:, sl]).astype(BF16)
            dmb = dmix.astype(BF16)
            dws_ref[g] += lax.dot_general(dmb, vn[:, sl], (((1,), (1,)), ((), ())), preferred_element_type=F32)
            dbs_ref[:, g:g + 1] += jnp.sum(dmix, axis=-1, keepdims=True)
            dvn_ref[:, sl] = lax.dot_general(w, dmb, (((0,), (0,)), ((), ())), preferred_element_type=F32)
        dvn = dvn_ref[...]
        dgain_ref[...] += jnp.sum(dvn * xhat, axis=0, keepdims=True)
        dy = dvn * gain_
        dv_ = r * (dy - xhat * jnp.mean(dy * xhat, axis=-1, keepdims=True))
        dz_ref[:, SW:] = (dv_ * dvg).astype(BF16)

    row = pl.BlockSpec((BLOCK, SW), lambda c: (c, 0))
    return pl.pallas_call(
        body, name=name, grid=(T // BLOCK,),
        in_specs=[row, pl.BlockSpec((BLOCK, SW), lambda c: (c, 1)),
                  pl.BlockSpec((1, SW), lambda c: (0, 0)), pl.BlockSpec((G, BLOCK, BLOCK), lambda c: (0, 0, 0)),
                  pl.BlockSpec((BLOCK, G), lambda c: (0, 0)), row, _ANY],
        out_specs=[pl.BlockSpec((BLOCK, 2 * SW), lambda c: (c, 0)), pl.BlockSpec((G, BLOCK, BLOCK), lambda c: (0, 0, 0)),
                   pl.BlockSpec((BLOCK, G), lambda c: (0, 0)), pl.BlockSpec((1, SW), lambda c: (0, 0))],
        out_shape=[jax.ShapeDtypeStruct(dz.shape, dz.dtype),
                   jax.ShapeDtypeStruct((G, BLOCK, BLOCK), F32), jax.ShapeDtypeStruct((BLOCK, G), F32),
                   jax.ShapeDtypeStruct((1, SW), F32)],
        input_output_aliases={6: 0},
        scratch_shapes=[pltpu.VMEM((BLOCK, SW), F32)],
        compiler_params=_cparams("arbitrary"),
    )(z, z, gain, ws_b, bs_t, da, dz)


def _bias_table(rel_bias, bmap, *, name):
    H = rel_bias.shape[1]

    def body(rb_ref, bmap_ref, o_ref):
        bm_ = bmap_ref[...]
        for h in range(H):
            acc = jnp.zeros(bm_.shape, F32)
            for b in range(REL_BUCKETS):
                acc = jnp.where(bm_ == b, rb_ref[b, h], acc)
            o_ref[h] = acc

    return pl.pallas_call(
        body, name=name,
        in_specs=[pl.BlockSpec(memory_space=pltpu.SMEM), pl.BlockSpec(memory_space=pltpu.VMEM)],
        out_specs=pl.BlockSpec(memory_space=pltpu.VMEM),
        out_shape=jax.ShapeDtypeStruct((H, BLOCK, 3 * BLOCK), F32),
    )(rel_bias, bmap)


def _attn_probs(q_ref, kb, bias_ref, sink_ref, s_ref, n, T, group):
    H = s_ref.shape[0]
    for h in range(H):
        kv = h // group
        qh = q_ref[:, h * HEAD_DIM:(h + 1) * HEAD_DIM].astype(BF16)
        s_ref[h] = lax.dot_general(qh, kb[:, kv * HEAD_DIM:(kv + 1) * HEAD_DIM], (((1,), (1,)), ((), ())),
                                   preferred_element_type=F32)
    row = lax.broadcasted_iota(jnp.int32, (BLOCK, 3 * BLOCK), 0)
    col = lax.broadcasted_iota(jnp.int32, (BLOCK, 3 * BLOCK), 1)
    key_pos = n * BLOCK + col - BLOCK
    valid = (jnp.abs(col - BLOCK - row) <= BLOCK) & (key_pos >= 0) & (key_pos < T)
    s = s_ref[...] * (HEAD_DIM ** -0.5) + bias_ref[...]
    s = jnp.where(valid[None], s, NEG)
    sink = sink_ref[...]
    m = jnp.maximum(jnp.max(s, axis=-1, keepdims=True), sink)
    e = jnp.exp(s - m)
    es = jnp.exp(sink - m)
    inv = 1.0 / (jnp.sum(e, axis=-1, keepdims=True) + es)
    return e * inv, es * inv


def _attn_fwd(z, kpad, vpad, bias, sink, *, name):
    T = z.shape[0]
    H = bias.shape[0]
    AW = H * HEAD_DIM
    group = H // N_KV_HEADS

    def body(q_ref, k_ref, v_ref, bias_ref, sink_ref, o_ref, s_ref, p_ref):
        n = pl.program_id(0)
        start = pl.multiple_of(n * BLOCK, BLOCK)
        kb = k_ref[pl.ds(start, 3 * BLOCK), :]
        vb = v_ref[pl.ds(start, 3 * BLOCK), :]
        p, _ = _attn_probs(q_ref, kb, bias_ref, sink_ref, s_ref, n, T, group)
        p_ref[...] = p.astype(BF16)
        for h in range(H):
            kv = h // group
            o = jnp.dot(p_ref[h], vb[:, kv * HEAD_DIM:(kv + 1) * HEAD_DIM], preferred_element_type=F32)
            o_ref[:, h * HEAD_DIM:(h + 1) * HEAD_DIM] = o.astype(BF16)

    full_kv = pl.BlockSpec((T + 2 * BLOCK, KV_WIDTH), lambda n: (0, 0))
    return pl.pallas_call(
        body, name=name, grid=(T // BLOCK,),
        in_specs=[pl.BlockSpec((BLOCK, AW), lambda n: (n, 2)), full_kv, full_kv,
                  pl.BlockSpec((H, BLOCK, 3 * BLOCK), lambda n: (0, 0, 0)), pl.BlockSpec((H, 1, 1), lambda n: (0, 0, 0))],
        out_specs=pl.BlockSpec((BLOCK, AW), lambda n: (n, 0)),
        out_shape=jax.ShapeDtypeStruct((T, AW), BF16),
        scratch_shapes=[pltpu.VMEM((H, BLOCK, 3 * BLOCK), F32), pltpu.VMEM((H, BLOCK, 3 * BLOCK), BF16)],
        compiler_params=_cparams("parallel"),
    )(z, kpad, vpad, bias, sink)


def _attn_bwd(z, kpad, vpad, bias, sink, do, dz, *, name):
    T = z.shape[0]
    H = bias.shape[0]
    AW = H * HEAD_DIM
    group = H // N_KV_HEADS
    scale = HEAD_DIM ** -0.5

    def body(q_ref, k_ref, v_ref, bias_ref, sink_ref, do_ref, dz_in_ref, dq_ref, dk_ref, dv_ref, dbias_ref, dsink_ref,
             s_ref, dp_ref, p_ref, ds_ref):
        n = pl.program_id(0)

        @pl.when(n == 0)
        def _():
            dk_ref[...] = jnp.zeros_like(dk_ref)
            dv_ref[...] = jnp.zeros_like(dv_ref)
            dbias_ref[...] = jnp.zeros_like(dbias_ref)
            dsink_ref[...] = jnp.zeros_like(dsink_ref)

        start = pl.multiple_of(n * BLOCK, BLOCK)
        kb = k_ref[pl.ds(start, 3 * BLOCK), :]
        vb = v_ref[pl.ds(start, 3 * BLOCK), :]
        p, p_sink = _attn_probs(q_ref, kb, bias_ref, sink_ref, s_ref, n, T, group)
        s_ref[...] = p
        p_ref[...] = p.astype(BF16)
        for h in range(H):
            kv = h // group
            dp_ref[h] = lax.dot_general(do_ref[:, h * HEAD_DIM:(h + 1) * HEAD_DIM], vb[:, kv * HEAD_DIM:(kv + 1) * HEAD_DIM],
                                        (((1,), (1,)), ((), ())), preferred_element_type=F32)
        p = s_ref[...]
        dp = dp_ref[...]
        delta = jnp.sum(p * dp, axis=-1, keepdims=True)
        ds = p * (dp - delta)
        dbias_ref[...] += ds
        dsink_ref[...] += -(p_sink * delta)
        ds_ref[...] = ds.astype(BF16)
        for kv in range(N_KV_HEADS):
            ksl = slice(kv * HEAD_DIM, (kv + 1) * HEAD_DIM)
            dk_acc = jnp.zeros((3 * BLOCK, HEAD_DIM), F32)
            dv_acc = jnp.zeros((3 * BLOCK, HEAD_DIM), F32)
            for gi in range(group):
                h = kv * group + gi
                hsl = slice(h * HEAD_DIM, (h + 1) * HEAD_DIM)
                dsb = ds_ref[h]
                dq = jnp.dot(dsb, kb[:, ksl], preferred_element_type=F32) * scale
                dq_ref[:, hsl] = dq.astype(BF16)
                dk_acc = dk_acc + lax.dot_general(dsb, q_ref[:, hsl].astype(BF16), (((0,), (0,)), ((), ())),
                                                  preferred_element_type=F32)
                dv_acc = dv_acc + lax.dot_general(p_ref[h], do_ref[:, hsl], (((0,), (0,)), ((), ())),
                                                  preferred_element_type=F32)
            dk_ref[pl.ds(start, 3 * BLOCK), ksl] += dk_acc * scale
            dv_ref[pl.ds(start, 3 * BLOCK), ksl] += dv_acc

    full_kv = pl.BlockSpec((T + 2 * BLOCK, KV_WIDTH), lambda n: (0, 0))
    bias_spec = pl.BlockSpec((H, BLOCK, 3 * BLOCK), lambda n: (0, 0, 0))
    row = pl.BlockSpec((BLOCK, AW), lambda n: (n, 0))
    q_cols = pl.BlockSpec((BLOCK, AW), lambda n: (n, 2))
    band = (H, BLOCK, 3 * BLOCK)
    return pl.pallas_call(
        body, name=name, grid=(T // BLOCK,),
        in_specs=[q_cols, full_kv, full_kv, bias_spec, pl.BlockSpec((H, 1, 1), lambda n: (0, 0, 0)), row, _ANY],
        out_specs=[q_cols, full_kv, full_kv, bias_spec, pl.BlockSpec((H, BLOCK, 1), lambda n: (0, 0, 0))],
        out_shape=[jax.ShapeDtypeStruct(dz.shape, dz.dtype),
                   jax.ShapeDtypeStruct((T + 2 * BLOCK, KV_WIDTH), F32), jax.ShapeDtypeStruct((T + 2 * BLOCK, KV_WIDTH), F32),
                   jax.ShapeDtypeStruct(band, F32), jax.ShapeDtypeStruct((H, BLOCK, 1), F32)],
        input_output_aliases={6: 0},
        scratch_shapes=[pltpu.VMEM(band, F32), pltpu.VMEM(band, F32), pltpu.VMEM(band, BF16), pltpu.VMEM(band, BF16)],
        compiler_params=_cparams("arbitrary"),
    )(z, kpad, vpad, bias, sink, do, dz)


def _dkv_into(dkp, dvp, dz, *, name):
    T = dz.shape[0]
    D = (dz.shape[1] - 2 * KV_WIDTH) * 2 // 7
    col = (D + D // 2) // (2 * KV_WIDTH)
    assert col * 2 * KV_WIDTH == D + D // 2

    def body(dk_ref, dv_ref, dz_in_ref, o_ref):
        o_ref[:, :KV_WIDTH] = dk_ref[...].astype(BF16)
        o_ref[:, KV_WIDTH:] = dv_ref[...].astype(BF16)

    kv = pl.BlockSpec((BLOCK, KV_WIDTH), lambda n: (n + 1, 0))
    return pl.pallas_call(
        body, name=name, grid=(T // BLOCK,),
        in_specs=[kv, kv, _ANY], out_specs=pl.BlockSpec((BLOCK, 2 * KV_WIDTH), lambda n: (n, col)),
        out_shape=jax.ShapeDtypeStruct(dz.shape, dz.dtype), input_output_aliases={2: 0},
        compiler_params=_cparams("parallel"),
    )(dkp, dvp, dz)


def _kv_pad(z, *, name):
    T = z.shape[0]
    D = (z.shape[1] - 2 * KV_WIDTH) * 2 // 7
    kcol = (D + D // 2) // KV_WIDTH
    nb = T // BLOCK

    def body(k_ref, v_ref, ko_ref, vo_ref):
        b = pl.program_id(0)
        inside = (b >= 1) & (b <= nb)
        ko_ref[...] = jnp.where(inside, k_ref[...].astype(F32), 0.0).astype(BF16)
        vo_ref[...] = jnp.where(inside, v_ref[...].astype(F32), 0.0).astype(BF16)

    out = jax.ShapeDtypeStruct((T + 2 * BLOCK, KV_WIDTH), BF16)
    o_spec = pl.BlockSpec((BLOCK, KV_WIDTH), lambda b: (b, 0))
    return pl.pallas_call(
        body, name=name, grid=(nb + 2,),
        in_specs=[pl.BlockSpec((BLOCK, KV_WIDTH), lambda b: (jnp.clip(b - 1, 0, nb - 1), kcol)),
                  pl.BlockSpec((BLOCK, KV_WIDTH), lambda b: (jnp.clip(b - 1, 0, nb - 1), kcol + 1))],
        out_specs=[o_spec, o_spec], out_shape=[out, out],
        compiler_params=_cparams("parallel"),
    )(z, z)


def _attn_small_grads(dbias, dsink_rows, bmap, after, *, name):
    H = dbias.shape[0]

    def body(dbias_ref, dsink_ref, bmap_ref, drel_ref, ds_ref):
        bm_ = bmap_ref[...]
        for h in range(H):
            d = dbias_ref[h]
            for b in range(REL_BUCKETS):
                drel_ref[b, h] = jnp.sum(jnp.where(bm_ == b, d, 0.0))
            ds_ref[0, h] = jnp.sum(dsink_ref[h])

    vmem = pl.BlockSpec(memory_space=pltpu.VMEM)
    smem = pl.BlockSpec(memory_space=pltpu.SMEM)
    body, in_specs, args = _ordered_after(body, 3, [vmem, vmem, vmem], (dbias, dsink_rows, bmap), after)
    return pl.pallas_call(
        body, name=name, in_specs=in_specs, out_specs=[smem, smem],
        out_shape=[jax.ShapeDtypeStruct((REL_BUCKETS, H), F32), jax.ShapeDtypeStruct((1, H), F32)],
    )(*args)


def _local_step(x, target, weight, emit, flush, norm_mix, v_gain, w_s, b_s, sink, rel_bias, norm_ffn, norm_final, early=()):
    T, D = x.shape
    ws_b = w_s.astype(BF16)
    bs_t = b_s.T
    bmap = jnp.asarray(_bucket_map())
    sink = sink.reshape(-1, 1, 1)

    bias = _bias_table(rel_bias, bmap, name="bias_table")
    h = _rms_fwd(x, norm_mix, name="rms_mix", after=[bias, *early])
    w_in = weight("w_in", h)
    z = _mm(h, w_in, tb=True, out_dtype=BF16, name="mm_z", bm=2048, bn=768)
    a = _sgu_fwd(z, v_gain, ws_b, bs_t, name="sgu_fwd")
    w_a = weight("w_a_out", a)
    ya = _mm_w8(a, w_a, name="mm_ya", bm=2048, out_dtype=BF16)
    kpad, vpad = _kv_pad(z, name="kv_pad")
    o = _attn_fwd(z, kpad, vpad, bias, sink, name="attn_fwd")
    w_b = weight("w_b_out", o)
    yb = _mm_w8(o, w_b, name="mm_yb", bm=2048, out_dtype=BF16)
    m = _merge_fwd(z, ya, yb, name="merge_fwd")
    w_o = weight("w_o", m)
    x1, h2 = _mm_resid_rms(m, w_o, x, norm_ffn, name="mm_x1_rms")
    w_gate = weight("w_gate", h2)
    gate = _mm(h2, w_gate, tb=True, name="mm_gate", bm=2048, bn=512)
    w_up = weight("w_up", gate)
    gate, up, act = _swiglu_mm_fwd(h2, w_up, gate, name="mm_up_swiglu")
    w_down = weight("w_down", act)
    x2 = _mm(act, w_down, name="mm_x2", add=x1, bm=1024, bn=512)
    loss, dx2b, g_norm_final = _loss_head(x2, norm_final, target, name="loss_head")

    g_w_down = _mm(act, dx2b, ta=True, out_dtype=BF16, name="mm_gwdown", bm=512, bn=2048)
    tok = emit(("w_down",), (g_w_down,))
    dgate, dup = _swiglu_mm_bwd(dx2b, w_down, gate, up, name="mm_dact_swiglu", after=tok)
    tok = flush(dgate)
    g_w_gate = _mm(dgate, h2, ta=True, out_dtype=BF16, name="mm_gwgate", bm=512, bn=2048, after=tok)
    g_w_up = _mm(dup, h2, ta=True, out_dtype=BF16, name="mm_gwup", bm=512, bn=2048)
    tok = emit(("w_gate", "w_up"), (g_w_gate, g_w_up))
    dh2 = _mm_sum2(dgate, w_gate, dup, w_up, name="mm_dh2", after=tok)
    tok = flush(dh2)
    dx1, dx1b, g_norm_ffn = _rms_bwd(x1, norm_ffn, dh2, dx2b, name="rms_ffn_bwd", want_bf16=True, after=tok)

    g_w_o = _mm(m, dx1b, ta=True, out_dtype=BF16, name="mm_gwo", bm=2048, bn=512)
    tok = emit(("w_o",), (g_w_o,))
    dm = _mm(dx1b, w_o, tb=True, name="mm_dm", bm=2048, bn=512, after=tok)
    tok = flush(dm)
    dy, dz = _merge_bwd(z, ya, yb, dm, name="merge_bwd", after=tok)
    g_w_a = _mm_gw8(a, dy, w_a.shape[2], name="mm_gwa", lead=0)
    g_w_b = _mm_gw8(o, dy, w_b.shape[2], name="mm_gwb", lead=1)
    tok = emit(("w_a_out", "w_b_out"), (g_w_a, g_w_b))
    da = _mm_w8t(dy, w_a, name="mm_da", bm=2048, bn=512, after=tok, lead=0)
    tok = flush(da)
    do = _mm_w8t(dy, w_b, out_dtype=BF16, name="mm_do", bm=2048, bn=512, after=tok, lead=1)
    dz, g_w_s, g_b_s_t, g_v_gain = _sgu_bwd(z, v_gain, ws_b, bs_t, da, dz, name="sgu_bwd")
    dz, dkp, dvp, dbias, dsink_rows = _attn_bwd(z, kpad, vpad, bias, sink, do, dz, name="attn_bwd")
    dz = _dkv_into(dkp, dvp, dz, name="dkv_into_dz")
    g_w_in = _mm(dz, h, ta=True, out_dtype=BF16, name="mm_gwin", bm=768, bn=2048)
    tok = emit(("w_in",), (g_w_in,))
    half = dict(bm=T // 2, bn=256)
    dh = _mm(dz, w_in, name="mm_dh_top", row_blocks=(0, 1), after=tok, **half)
    tok = flush(dh)
    dh = _mm(dz, w_in, name="mm_dh_bottom", row_blocks=(1, 1), into=dh, after=tok, **half)
    g_rel_bias, g_sink = _attn_small_grads(dbias, dsink_rows, bmap, dh, name="attn_small_grads")
    grad_x, g_norm_mix = _rms_bwd(x, norm_mix, dh, dx1, name="rms_mix_bwd", want_bf16=False)

    small = dict(norm_mix=g_norm_mix, sgu_v_gain=g_v_gain, sgu_w_s=g_w_s, sgu_b_s=g_b_s_t.T, attn_sink=g_sink,
                 rel_bias=g_rel_bias, norm_ffn=g_norm_ffn, norm_final=g_norm_final)
    return loss, grad_x, small


def _position():
    return lax.axis_index("x"), lax.axis_index("y"), lax.axis_index("c")


def _other_chips(x, y):
    return [(1 - x, y), (x, 1 - y), (1 - x, 1 - y)]


def _slot(px, py, pc):
    return 4 * px + 2 * py + pc


_HBM = pl.BlockSpec(memory_space=pltpu.HBM)
_SEM = pl.BlockSpec(memory_space=pltpu.SEMAPHORE)
_DATAFLOW = pltpu.SideEffectType.DATAFLOW_SIDE_EFFECTING


def _in_hbm(a):
    return pltpu.with_memory_space_constraint(a, pltpu.HBM)


def _own_slot(shard, pos, *, name, after=None):
    R, C = shard.shape
    tr = _div(R, 256, 16)

    def body(pos_ref, w_ref, o_ref):
        o_ref[...] = w_ref[...].astype(BF16)

    body, in_specs, args = _ordered_after(body, 2, [pl.BlockSpec((tr, C), lambda i, pos_ref: (i, 0))], (pos, shard), after)
    grid_spec = pltpu.PrefetchScalarGridSpec(
        num_scalar_prefetch=1, grid=(R // tr,), in_specs=in_specs,
        out_specs=pl.BlockSpec((None, tr, C), lambda i, pos_ref: (pos_ref[0], i, 0)))
    return pl.pallas_call(
        body, name=name, grid_spec=grid_spec,
        out_shape=jax.ShapeDtypeStruct((N_DEV, R, C), BF16),
        compiler_params=_cparams("parallel"),
    )(*args)


def _ag_copies(w, land_ref, send_sems, recv_sems):
    x, y, c = _position()
    mine = land_ref.at[_slot(x, y, c)]
    targets = [(px, py, c) for px, py in _other_chips(x, y)] + [(x, y, 1 - c)]
    return [pltpu.make_async_remote_copy(src_ref=mine, dst_ref=mine, send_sem=send_sems.at[4 * w + k],
                                         recv_sem=recv_sems.at[4 * w + k], device_id=to, device_id_type=MESH)
            for k, to in enumerate(targets)]


def _ag_start(buffers, groups, *, name, after=None):
    lands = [buffers[i] for g in groups for i in g]
    n, ng = len(lands), len(groups)
    sizes = [len(g) for g in groups]

    def body(*refs):
        land_refs = refs[:n]
        sems = refs[n:n + 2 * ng]
        token = refs[-1]
        i = 0
        for g in range(ng):
            for w in range(sizes[g]):
                for cp in _ag_copies(w, land_refs[i], sems[2 * g], sems[2 * g + 1]):
                    cp.start()
                i += 1
        token[...] = jnp.zeros_like(token)

    sem_shapes = [pltpu.SemaphoreType.DMA((4 * k,)) for k in sizes for _ in range(2)]
    body, in_specs, args = _ordered_after(body, n, [_HBM] * n, tuple(_in_hbm(a) for a in lands), after)
    outs = pl.pallas_call(
        body, name=name,
        in_specs=in_specs,
        out_specs=tuple([_SEM] * (2 * ng) + [_HBM] * n + [pl.BlockSpec(memory_space=pltpu.VMEM)]),
        out_shape=tuple(sem_shapes + [pltpu.HBM(a.shape, a.dtype) for a in lands] + [jax.ShapeDtypeStruct((8, LANES), F32)]),
        input_output_aliases={i: 2 * ng + i for i in range(n)},
        compiler_params=pltpu.CompilerParams(has_side_effects=_DATAFLOW),
    )(*args)
    sems, thru = outs[:2 * ng], outs[2 * ng:2 * ng + n]
    result, i = [], 0
    for g in range(ng):
        k = sizes[g]
        result.append((sems[2 * g], sems[2 * g + 1], list(thru[i:i + k])))
        i += k
    return result, outs[-1]


def _ag_wait(send_sems, recv_sems, lands, after, *, name):
    n = len(lands)

    def body(*refs):
        land_refs = refs[:n]
        send_ref, recv_ref = refs[n], refs[n + 1]
        token = refs[-1]
        for w in range(n):
            for cp in _ag_copies(w, land_refs[w], send_ref, recv_ref):
                cp.wait_send()
                cp.wait_recv()
        token[...] = jnp.zeros_like(token)

    outs = pl.pallas_call(
        body, name=name,
        in_specs=[_HBM] * n + [_SEM, _SEM, _ANY],
        out_specs=tuple([_HBM] * n + [pl.BlockSpec(memory_space=pltpu.VMEM)]),
        out_shape=tuple([pltpu.HBM(a.shape, a.dtype) for a in lands] + [jax.ShapeDtypeStruct((8, LANES), F32)]),
        input_output_aliases={i: i for i in range(n)},
        compiler_params=pltpu.CompilerParams(has_side_effects=_DATAFLOW),
    )(*lands, send_sems, recv_sems, after)
    return list(outs[:n]), outs[n]


def _ag_forward(lands, *, name, after=None):
    n = len(lands)

    def body(*refs):
        in_refs, out_refs = refs[:n], refs[n:2 * n]
        send_sems, recv_sems = refs[2 * n:]
        x, y, c = _position()
        copies = []
        for w in range(n):
            for k, (px, py) in enumerate(_other_chips(x, y)):
                cp = pltpu.make_async_remote_copy(
                    src_ref=in_refs[w].at[_slot(px, py, c)], dst_ref=out_refs[w].at[_slot(px, py, c)],
                    send_sem=send_sems.at[3 * w + k], recv_sem=recv_sems.at[3 * w + k],
                    device_id=(x, y, 1 - c), device_id_type=MESH)
                cp.start()
                copies.append(cp)
        for cp in copies:
            cp.wait()

    body, in_specs, args = _ordered_after(body, n, [_ANY] * n, tuple(lands), after)
    return pl.pallas_call(
        body, name=name,
        in_specs=in_specs, out_specs=[_ANY] * n,
        out_shape=[jax.ShapeDtypeStruct(a.shape, a.dtype) for a in lands],
        input_output_aliases={i: i for i in range(n)},
        scratch_shapes=[pltpu.SemaphoreType.DMA((3 * n,)), pltpu.SemaphoreType.DMA((3 * n,))],
    )(*args)


def _sibling_copies(w, g8_ref, land_ref, send_sems, recv_sems):
    x, y, c = _position()
    return [pltpu.make_async_remote_copy(src_ref=g8_ref.at[2 * p + (1 - c)], dst_ref=land_ref.at[p],
                                         send_sem=send_sems.at[4 * w + p], recv_sem=recv_sems.at[4 * w + p],
                                         device_id=(x, y, 1 - c), device_id_type=MESH)
            for p in range(4)]


def _chip_copies(w, sums_ref, land_ref, send_sems, recv_sems):
    x, y, c = _position()
    return [pltpu.make_async_remote_copy(src_ref=sums_ref.at[2 * px + py], dst_ref=land_ref.at[k],
                                         send_sem=send_sems.at[3 * w + k], recv_sem=recv_sems.at[3 * w + k],
                                         device_id=(px, py, c), device_id_type=MESH)
            for k, (px, py) in enumerate(_other_chips(x, y))]


def _copies_start(copies, per_weight, srcs, *, name):
    n = len(srcs)
    lands = [lax.empty((per_weight,) + s.shape[1:], s.dtype) for s in srcs]

    def body(*refs):
        src_refs, land_refs = refs[:n], refs[n:2 * n]
        send_sems, recv_sems = refs[2 * n], refs[2 * n + 1]
        token = refs[-1]
        for w in range(n):
            for cp in copies(w, src_refs[w], land_refs[w], send_sems, recv_sems):
                cp.start()
        token[...] = jnp.zeros_like(token)

    outs = pl.pallas_call(
        body, name=name,
        in_specs=[_HBM] * (2 * n),
        out_specs=tuple([_SEM, _SEM] + [_HBM] * (2 * n) + [pl.BlockSpec(memory_space=pltpu.VMEM)]),
        out_shape=tuple([pltpu.SemaphoreType.DMA((per_weight * n,)), pltpu.SemaphoreType.DMA((per_weight * n,))]
                        + [pltpu.HBM(a.shape, a.dtype) for a in srcs + lands] + [jax.ShapeDtypeStruct((8, LANES), F32)]),
        input_output_aliases={i: 2 + i for i in range(2 * n)},
        compiler_params=pltpu.CompilerParams(has_side_effects=_DATAFLOW),
    )(*[_in_hbm(a) for a in srcs + lands])
    return outs[0], outs[1], list(outs[2:2 + n]), list(outs[2 + n:2 + 2 * n]), outs[-1]


def _copies_wait(copies, send_sems, recv_sems, srcs, lands, after, *, name):
    n = len(srcs)

    def body(*refs):
        src_refs, land_refs = refs[:n], refs[n:2 * n]
        send_ref, recv_ref = refs[2 * n], refs[2 * n + 1]
        for w in range(n):
            for cp in copies(w, src_refs[w], land_refs[w], send_ref, recv_ref):
                cp.wait_send()
                cp.wait_recv()

    outs = pl.pallas_call(
        body, name=name,
        in_specs=[_HBM] * (2 * n) + [_SEM, _SEM, _ANY],
        out_specs=tuple([_HBM] * (2 * n)),
        out_shape=tuple(pltpu.HBM(a.shape, a.dtype) for a in srcs + lands),
        input_output_aliases={i: i for i in range(2 * n)},
        compiler_params=pltpu.CompilerParams(has_side_effects=_DATAFLOW),
    )(*srcs, *lands, send_sems, recv_sems, after)
    return list(outs[:n]), list(outs[n:])


def _chip_sums(g8, from_sibling, pos, *, name):
    _, R, C = g8.shape
    tr = _div(R, 512, 16)

    def body(pos_ref, g_ref, s_ref, o_ref):
        o_ref[...] = (g_ref[...].astype(F32) + s_ref[...].astype(F32)).astype(BF16)

    def chip(k, pos_ref):
        return jnp.where(k >= pos_ref[1], k + 1, k)

    grid_spec = pltpu.PrefetchScalarGridSpec(
        num_scalar_prefetch=1, grid=(3, R // tr),
        in_specs=[pl.BlockSpec((None, tr, C), lambda k, i, pos_ref: (2 * chip(k, pos_ref) + pos_ref[2], i, 0)),
                  pl.BlockSpec((None, tr, C), lambda k, i, pos_ref: (chip(k, pos_ref), i, 0))],
        out_specs=pl.BlockSpec((None, tr, C), lambda k, i, pos_ref: (chip(k, pos_ref), i, 0)))
    return pl.pallas_call(
        body, name=name, grid_spec=grid_spec,
        out_shape=jax.ShapeDtypeStruct((4, R, C), BF16),
        compiler_params=_cparams("parallel", "parallel"),
    )(pos, g8, from_sibling)


def _peer_copies(land_ref, send_sems, recv_sems):
    x, y, c = _position()
    mine = land_ref.at[_slot(x, y, c)]
    copies = []
    for k in range(1, N_DEV):
        to = (1 - x if k & 4 else x, 1 - y if k & 2 else y, 1 - c if k & 1 else c)
        copies.append(pltpu.make_async_remote_copy(src_ref=mine, dst_ref=mine, send_sem=send_sems.at[k - 1],
                                                   recv_sem=recv_sems.at[k - 1], device_id=to, device_id_type=MESH))
    return copies


def _small_exchange_start(buf, *, name):
    def body(buf_ref, send_sems, recv_sems, thru_ref, token):
        for cp in _peer_copies(buf_ref, send_sems, recv_sems):
            cp.start()
        token[...] = jnp.zeros_like(token)

    return pl.pallas_call(
        body, name=name, in_specs=[_HBM],
        out_specs=(_SEM, _SEM, _HBM, pl.BlockSpec(memory_space=pltpu.VMEM)),
        out_shape=(pltpu.SemaphoreType.DMA((N_DEV - 1,)), pltpu.SemaphoreType.DMA((N_DEV - 1,)),
                   pltpu.HBM(buf.shape, buf.dtype), jax.ShapeDtypeStruct((8, LANES), F32)),
        input_output_aliases={0: 2},
        compiler_params=pltpu.CompilerParams(has_side_effects=_DATAFLOW),
    )(_in_hbm(buf))


def _small_exchange_wait(send_sems, recv_sems, buf, after, *, name):
    def body(buf_ref, send_ref, recv_ref, after_ref, out_ref):
        for cp in _peer_copies(buf_ref, send_ref, recv_ref):
            cp.wait_send()
            cp.wait_recv()

    return pl.pallas_call(
        body, name=name, in_specs=[_HBM, _SEM, _SEM, _ANY], out_specs=_HBM,
        out_shape=pltpu.HBM(buf.shape, buf.dtype), input_output_aliases={0: 0},
        compiler_params=pltpu.CompilerParams(has_side_effects=_DATAFLOW),
    )(buf, send_sems, recv_sems, after)


def _sum_slots(buf, *, name):
    _, R, L = buf.shape

    def body(buf_ref, sum_ref):
        acc = buf_ref[0]
        for d in range(1, N_DEV):
            acc = acc + buf_ref[d]
        sum_ref[...] = acc

    vmem = pl.BlockSpec(memory_space=pltpu.VMEM)
    return pl.pallas_call(body, name=name, in_specs=[vmem], out_specs=vmem,
                          out_shape=jax.ShapeDtypeStruct((R, L), F32),
                          compiler_params=pltpu.CompilerParams(vmem_limit_bytes=VMEM_LIMIT))(buf)


def _adamw_math(w, g, m, v):
    m = ADAM_B1 * m + (1.0 - ADAM_B1) * g
    v = ADAM_B2 * v + (1.0 - ADAM_B2) * (g * g)
    m_hat = m / (1.0 - ADAM_B1 ** ADAM_STEP)
    v_hat = v / (1.0 - ADAM_B2 ** ADAM_STEP)
    delta = -ADAM_LR * (m_hat / (jnp.sqrt(v_hat) + ADAM_EPS) + ADAM_WD * w)
    return delta, m, v


def _adamw_shard(w, m, v, g8, from_sibling, from_chips, pos, *, name):
    R, C = w.shape
    tr = _div(R, 256, 16)

    def body(pos_ref, w_ref, m_ref, v_ref, g_ref, s_ref, r_ref, go_ref, d_ref, mo_ref, vo_ref):
        g = g_ref[...].astype(F32) + s_ref[...].astype(F32)
        for k in range(3):
            g = g + r_ref[k].astype(F32)
        delta, m_, v_ = _adamw_math(w_ref[...], g, m_ref[...], v_ref[...])
        go_ref[...] = g
        d_ref[...] = delta
        mo_ref[...] = m_
        vo_ref[...] = v_

    blk = pl.BlockSpec((tr, C), lambda i, pos_ref: (i, 0))
    grid_spec = pltpu.PrefetchScalarGridSpec(
        num_scalar_prefetch=1, grid=(R // tr,),
        in_specs=[blk, blk, blk,
                  pl.BlockSpec((None, tr, C), lambda i, pos_ref: (pos_ref[0], i, 0)),
                  pl.BlockSpec((None, tr, C), lambda i, pos_ref: (pos_ref[1], i, 0)),
                  pl.BlockSpec((3, tr, C), lambda i, pos_ref: (0, i, 0))],
        out_specs=[blk] * 4)
    out = jax.ShapeDtypeStruct((R, C), F32)
    return pl.pallas_call(
        body, name=name, grid_spec=grid_spec, out_shape=[out] * 4,
        compiler_params=_cparams("parallel"),
    )(pos, w, m, v, g8, from_sibling, from_chips)


def _adamw_small(w, g, m, v, *, name):
    R, L = w.shape

    def body(w_ref, g_ref, m_ref, v_ref, d_ref, mo_ref, vo_ref):
        delta, m_, v_ = _adamw_math(w_ref[...], g_ref[...], m_ref[...], v_ref[...])
        d_ref[...] = delta
        mo_ref[...] = m_
        vo_ref[...] = v_

    vmem = pl.BlockSpec(memory_space=pltpu.VMEM)
    out = jax.ShapeDtypeStruct((R, L), F32)
    return pl.pallas_call(body, name=name, in_specs=[vmem] * 4, out_specs=[vmem] * 3, out_shape=[out] * 3)(w, g, m, v)


_TILE = 8 * LANES


def _pack(pieces):
    rows = []
    for p in pieces:
        flat = p.reshape(-1).astype(F32)
        padded = -(-flat.shape[0] // _TILE) * _TILE
        rows.append(jnp.pad(flat, (0, padded - flat.shape[0])).reshape(-1, LANES))
    return jnp.concatenate(rows, axis=0)


def _unpack(packed, like):
    out, r = [], 0
    for p in like:
        size = int(np.prod(p.shape)) if p.shape else 1
        nrows = -(-size // _TILE) * 8
        out.append(packed[r:r + nrows].reshape(-1)[:size].reshape(p.shape))
        r += nrows
    return out


_BIG = ("w_in", "w_a_out", "w_b_out", "w_o", "w_gate", "w_up", "w_down")
_TRANSPOSED = ("w_in", "w_gate", "w_up")
_COL_SHARDED = ("w_a_out", "w_b_out")
_GATHER_GROUPS = (("w_in",), ("w_a_out", "w_b_out", "w_o"), ("w_gate",), ("w_up",), ("w_down",))
_START_AFTER_WAIT = {0: (1, 2), 1: (3,), 2: (4,)}
_SMALL = ("norm_mix", "sgu_v_gain", "sgu_w_s", "sgu_b_s", "attn_sink", "rel_bias", "norm_ffn", "norm_final")
_ORDER = ("w_in", "norm_mix", "sgu_v_gain", "sgu_w_s", "sgu_b_s", "w_a_out", "attn_sink", "rel_bias", "w_b_out", "w_o",
          "norm_ffn", "w_gate", "w_up", "w_down", "norm_final")


def _shard(name, a):
    return jnp.swapaxes(a, 1, 2)[0] if name in _TRANSPOSED else a[0]


def _unshard(name, a):
    return jnp.swapaxes(a[None], 1, 2) if name in _TRANSPOSED else a[None]


def _whole(name, gathered):
    _, r, c = gathered.shape
    return gathered if name in _COL_SHARDED else gathered.reshape(N_DEV * r, c)


def _blocks(name, grad):
    if name in _COL_SHARDED:
        return grad
    r, c = grad.shape
    return grad.reshape(N_DEV, r // N_DEV, c)


def kernel(x, w_in, norm_mix, sgu_v_gain, sgu_w_s, sgu_b_s, w_a_out, attn_sink, rel_bias, w_b_out, w_o, norm_ffn, w_gate, w_up, w_down, norm_final, loss_target, m_w_in, m_norm_mix, m_sgu_v_gain, m_sgu_w_s, m_sgu_b_s, m_w_a_out, m_attn_sink, m_rel_bias, m_w_b_out, m_w_o, m_norm_ffn, m_w_gate, m_w_up, m_w_down, m_norm_final, v_w_in, v_norm_mix, v_sgu_v_gain, v_sgu_w_s, v_sgu_b_s, v_w_a_out, v_attn_sink, v_rel_bias, v_w_b_out, v_w_o, v_norm_ffn, v_w_gate, v_w_up, v_w_down, v_norm_final):
    w = dict(w_in=w_in, norm_mix=norm_mix, sgu_v_gain=sgu_v_gain, sgu_w_s=sgu_w_s, sgu_b_s=sgu_b_s, w_a_out=w_a_out,
             attn_sink=attn_sink, rel_bias=rel_bias, w_b_out=w_b_out, w_o=w_o, norm_ffn=norm_ffn, w_gate=w_gate,
             w_up=w_up, w_down=w_down, norm_final=norm_final)
    m = dict(w_in=m_w_in, norm_mix=m_norm_mix, sgu_v_gain=m_sgu_v_gain, sgu_w_s=m_sgu_w_s, sgu_b_s=m_sgu_b_s,
             w_a_out=m_w_a_out, attn_sink=m_attn_sink, rel_bias=m_rel_bias, w_b_out=m_w_b_out, w_o=m_w_o,
             norm_ffn=m_norm_ffn, w_gate=m_w_gate, w_up=m_w_up, w_down=m_w_down, norm_final=m_norm_final)
    v = dict(w_in=v_w_in, norm_mix=v_norm_mix, sgu_v_gain=v_sgu_v_gain, sgu_w_s=v_sgu_w_s, sgu_b_s=v_sgu_b_s,
             w_a_out=v_w_a_out, attn_sink=v_attn_sink, rel_bias=v_rel_bias, w_b_out=v_w_b_out, w_o=v_w_o,
             norm_ffn=v_norm_ffn, w_gate=v_w_gate, w_up=v_w_up, w_down=v_w_down, norm_final=v_norm_final)
    xc, yc, cc = _position()
    pos = jnp.stack([_slot(xc, yc, cc), 2 * xc + yc, cc]).astype(jnp.int32)

    in_flight, full, slots = {}, {}, {}

    def start_gather(groups, after):
        names = [n for gi in groups for n in _GATHER_GROUPS[gi]]
        flights, token = _ag_start([slots[n] for n in names], [[names.index(n) for n in _GATHER_GROUPS[gi]] for gi in groups],
                                   name="ag_start_%d" % groups[0], after=after)
        in_flight.update(zip(groups, flights))
        return token

    def weight(name, after):
        if name not in full:
            gi = next(i for i, grp in enumerate(_GATHER_GROUPS) if name in grp)
            send_sems, recv_sems, lands = in_flight[gi]
            lands, token = _ag_wait(send_sems, recv_sems, lands, after, name="ag_wait_%d" % gi)
            started = start_gather(_START_AFTER_WAIT[gi], token) if gi in _START_AFTER_WAIT else None
            gathered = _ag_forward(lands, name="ag_forward_%d" % gi, after=started)
            full.update({n: _whole(n, g) for n, g in zip(_GATHER_GROUPS[gi], gathered)})
        return full[name]

    for n in _GATHER_GROUPS[0]:
        slots[n] = _own_slot(_shard(n, w[n]), pos, name="own_slot_" + n)
    first_started = start_gather((0,), None)
    for grp in _GATHER_GROUPS[1:]:
        for n in grp:
            slots[n] = _own_slot(_shard(n, w[n]), pos, name="own_slot_" + n, after=first_started)

    to_sibling, reducing = [], {}

    def emit(names, grads):
        g8 = [_blocks(n, g) for n, g in zip(names, grads)]
        send_sems, recv_sems, g8, lands, token = _copies_start(_sibling_copies, 4, g8, name="rs_sibling_start_" + names[0])
        to_sibling.append((names, send_sems, recv_sems, g8, lands))
        return token

    def flush(after):
        names, send_sems, recv_sems, g8, lands = to_sibling.pop()
        g8, from_sibling = _copies_wait(_sibling_copies, send_sems, recv_sems, g8, lands, after,
                                        name="rs_sibling_wait_" + names[0])
        sums4 = [_chip_sums(g, s, pos, name="chip_sums_" + n) for n, g, s in zip(names, g8, from_sibling)]
        send_sems, recv_sems, sums4, lands, token = _copies_start(_chip_copies, 3, sums4, name="rs_chips_start_" + names[0])
        reducing[names] = (g8, from_sibling, send_sems, recv_sems, sums4, lands)
        return token

    loss, grad_x, small_grads_local = _local_step(
        x[0], loss_target[0], weight, emit, flush, norm_mix, sgu_v_gain, sgu_w_s[0], sgu_b_s[0], attn_sink, rel_bias,
        norm_ffn, norm_final[None], early=[slots[n] for grp in _GATHER_GROUPS[1:] for n in grp])

    out_g, out_d, out_m, out_v = {}, {}, {}, {}
    small_like = [w[n] for n in _SMALL]
    small_w = _pack(small_like)
    packed = _pack([small_grads_local[n] for n in _SMALL] + [loss[0, 0]])
    mine = lax.dynamic_update_index_in_dim(jnp.zeros((N_DEV,) + packed.shape, F32), packed, pos[0], 0)
    small_send, small_recv, everyone, after = _small_exchange_start(mine, name="small_exchange_start")
    for gi, (names, (g8, from_sibling, send_sems, recv_sems, sums4, lands)) in enumerate(reducing.items()):
        if gi == len(reducing) - 1:
            everyone = _small_exchange_wait(small_send, small_recv, everyone, after, name="small_exchange_wait")
            summed = _sum_slots(everyone, name="small_sum")
            after = summed
        _, from_chips = _copies_wait(_chip_copies, send_sems, recv_sems, sums4, lands, after,
                                     name="rs_chips_wait_" + names[0])
        for i, n in enumerate(names):
            g, d, m_, v_ = _adamw_shard(_shard(n, w[n]), _shard(n, m[n]), _shard(n, v[n]), g8[i], from_sibling[i],
                                        from_chips[i], pos, name="adamw_" + n)
            out_g[n], out_d[n], out_m[n], out_v[n] = (_unshard(n, o) for o in (g, d, m_, v_))
            after = d
    *small_grads, loss_sum = _unpack(summed, small_like + [jax.ShapeDtypeStruct((), F32)])
    d_s, m_s, v_s = _adamw_small(small_w, summed[:small_w.shape[0]], _pack([m[n] for n in _SMALL]),
                                 _pack([v[n] for n in _SMALL]), name="adamw_small")
    for n, g, d, m_, v_ in zip(_SMALL, small_grads, _unpack(d_s, small_like), _unpack(m_s, small_like), _unpack(v_s, small_like)):
        out_g[n], out_d[n], out_m[n], out_v[n] = g, d, m_, v_

    return (loss_sum, grad_x[None], *[out_g[n] for n in _ORDER], *[out_d[n] for n in _ORDER],
            *[out_m[n] for n in _ORDER], *[out_v[n] for n in _ORDER])
```

```python
import functools
import math

import numpy as np
import jax
import jax.numpy as jnp
from jax import lax
from jax.experimental import pallas as pl
from jax.experimental.pallas import tpu as pltpu

F32 = jnp.float32
BF16 = jnp.bfloat16

EPS = 1e-6
NEG = -1e30
HEAD_DIM = 128
BLOCK = 128
N_KV_HEADS = 2
KV_WIDTH = N_KV_HEADS * HEAD_DIM
REL_BUCKETS = 32
REL_MAX_DIST = 128

ADAM_LR = 0.001
ADAM_B1 = 0.9
ADAM_B2 = 0.999
ADAM_EPS = 1e-08
ADAM_WD = 0.01
ADAM_STEP = 10

N_DEV = 8
LANES = 128
VMEM_LIMIT = 56 * 1024 * 1024
MESH = pl.DeviceIdType.MESH


def _cparams(*sem):
    return pltpu.CompilerParams(dimension_semantics=sem, vmem_limit_bytes=VMEM_LIMIT)


def _div(n, target, mult=LANES):
    best = None
    for d in range(mult, min(n, target) + 1, mult):
        if n % d == 0:
            best = d
    assert best is not None, (n, target, mult)
    return best


_ANY = pl.BlockSpec(memory_space=pl.ANY)


def _ordered_after(body, n_inputs, in_specs, args, after):
    if after is None:
        return body, in_specs, args
    extra = tuple(after) if isinstance(after, (tuple, list)) else (after,)

    def wrapped(*refs):
        return body(*refs[:n_inputs], *refs[n_inputs + len(extra):])

    return wrapped, list(in_specs) + [_ANY] * len(extra), tuple(args) + extra


def _bucket_map():
    nb = REL_BUCKETS // 2
    qi = np.arange(BLOCK)[:, None]
    kj = np.arange(3 * BLOCK)[None, :]
    rel = kj - BLOCK - qi
    ret = np.where(rel > 0, nb, 0)
    n = np.abs(rel)
    max_exact = nb // 2
    nf = np.maximum(n, 1).astype(np.float32)
    large = max_exact + (np.log(nf / np.float32(max_exact)) / np.float32(math.log(REL_MAX_DIST / max_exact))
                         * np.float32(nb - max_exact)).astype(np.int32)
    large = np.minimum(large, nb - 1)
    return (ret + np.where(n < max_exact, n, large)).astype(np.int32)


_GELU_C = math.sqrt(2.0 / math.pi)
_GELU_A = 0.044715


def _gelu(x):
    t = jnp.tanh(_GELU_C * (x + _GELU_A * (x * x * x)))
    return 0.5 * x * (1.0 + t)


def _gelu_and_grad(x):
    x2 = x * x
    t = jnp.tanh(_GELU_C * (x + _GELU_A * (x2 * x)))
    g = 0.5 * x * (1.0 + t)
    dg = 0.5 * (1.0 + t) + 0.5 * x * (1.0 - t * t) * (_GELU_C * (1.0 + 3.0 * _GELU_A * x2))
    return g, dg


def _sigmoid(x):
    return 1.0 / (1.0 + jnp.exp(-x))


def _mm(a, b, *, name, ta=False, tb=False, add=None, out_dtype=F32, bm=1024, bn=1024, bk=None, after=None,
        row_blocks=None, into=None):
    if ta:
        K, M = a.shape
    else:
        M, K = a.shape
    N = b.shape[0] if tb else b.shape[1]
    assert (b.shape[1] if tb else b.shape[0]) == K
    bm = _div(M, bm)
    bn = _div(N, bn)
    bk = K if bk is None else _div(K, bk)
    nk = K // bk
    i0, ni = (0, M // bm) if row_blocks is None else row_blocks
    a_spec = (pl.BlockSpec((bk, bm), lambda i, j, k: (k, i + i0)) if ta
              else pl.BlockSpec((bm, bk), lambda i, j, k: (i + i0, k)))
    b_spec = pl.BlockSpec((bn, bk), lambda i, j, k: (j, k)) if tb else pl.BlockSpec((bk, bn), lambda i, j, k: (k, j))
    o_spec = pl.BlockSpec((bm, bn), lambda i, j, k: (i + i0, j))
    dims = (((0 if ta else 1,), (1 if tb else 0,)), ((), ()))
    has_add = add is not None

    def body(*refs):
        if has_add:
            a_ref, b_ref, add_ref, o_ref, *scratch = refs
        else:
            a_ref, b_ref, o_ref, *scratch = refs
            add_ref = None
        p = lax.dot_general(a_ref[...].astype(BF16), b_ref[...].astype(BF16), dims, preferred_element_type=F32)
        if nk == 1:
            if has_add:
                p = p + add_ref[...]
            o_ref[...] = p.astype(out_dtype)
        else:
            acc = scratch[0]
            k = pl.program_id(2)

            @pl.when(k == 0)
            def _():
                acc[...] = p

            @pl.when(k > 0)
            def _():
                acc[...] += p

            @pl.when(k == nk - 1)
            def _():
                r = acc[...]
                if has_add:
                    r = r + add_ref[...]
                o_ref[...] = r.astype(out_dtype)

    in_specs = [a_spec, b_spec] + ([o_spec] if has_add else [])
    args = (a, b) + ((add,) if has_add else ())
    aliases = {}
    if into is not None:
        body, in_specs, args = _ordered_after(body, len(args), in_specs, args, into)
        aliases = {len(args) - 1: 0}
    body, in_specs, args = _ordered_after(body, len(args), in_specs, args, after)
    return pl.pallas_call(
        body, name=name, grid=(ni, N // bn, nk),
        in_specs=in_specs, out_specs=o_spec,
        out_shape=jax.ShapeDtypeStruct((M, N), out_dtype),
        input_output_aliases=aliases,
        scratch_shapes=[pltpu.VMEM((bm, bn), F32)] if nk > 1 else [],
        compiler_params=_cparams("parallel", "parallel", "arbitrary"),
    )(*args)


def _mm_resid_rms(a, b, resid, gain, *, name, bm=512):
    M, K = a.shape
    N = b.shape[1]
    bm = _div(M, bm)

    def body(a_ref, b_ref, r_ref, g_ref, x_ref, h_ref):
        x = r_ref[...] + jnp.dot(a_ref[...], b_ref[...], preferred_element_type=F32)
        x_ref[...] = x
        r = lax.rsqrt(jnp.mean(x * x, axis=-1, keepdims=True) + EPS)
        h_ref[...] = ((x * r) * g_ref[...]).astype(BF16)

    row = pl.BlockSpec((bm, N), lambda i: (i, 0))
    return pl.pallas_call(
        body, name=name, grid=(M // bm,),
        in_specs=[pl.BlockSpec((bm, K), lambda i: (i, 0)), pl.BlockSpec((K, N), lambda i: (0, 0)), row,
                  pl.BlockSpec((1, N), lambda i: (0, 0))],
        out_specs=[row, row], out_shape=[jax.ShapeDtypeStruct((M, N), F32), jax.ShapeDtypeStruct((M, N), BF16)],
        compiler_params=_cparams("parallel"),
    )(a, b, resid, gain)


def _mm_sum2(a1, b1, a2, b2, *, name, bm=1024, bn=512, bk=2816, after=None):
    M, K = a1.shape
    N = b1.shape[1]
    bm, bn, bk = _div(M, bm), _div(N, bn), _div(K, bk)
    nk = K // bk

    def body(a1_ref, b1_ref, a2_ref, b2_ref, o_ref, acc):
        p = (jnp.dot(a1_ref[...], b1_ref[...], preferred_element_type=F32)
             + jnp.dot(a2_ref[...], b2_ref[...], preferred_element_type=F32))
        k = pl.program_id(2)

        @pl.when(k == 0)
        def _():
            acc[...] = p

        @pl.when(k > 0)
        def _():
            acc[...] += p

        @pl.when(k == nk - 1)
        def _():
            o_ref[...] = acc[...]

    a_spec = pl.BlockSpec((bm, bk), lambda i, j, k: (i, k))
    b_spec = pl.BlockSpec((bk, bn), lambda i, j, k: (k, j))
    body, in_specs, args = _ordered_after(body, 4, [a_spec, b_spec, a_spec, b_spec], (a1, b1, a2, b2), after)
    return pl.pallas_call(
        body, name=name, grid=(M // bm, N // bn, nk),
        in_specs=in_specs, out_specs=pl.BlockSpec((bm, bn), lambda i, j, k: (i, j)),
        out_shape=jax.ShapeDtypeStruct((M, N), F32),
        scratch_shapes=[pltpu.VMEM((bm, bn), F32)],
        compiler_params=_cparams("parallel", "parallel", "arbitrary"),
    )(*args)


def _blocks_per_tile(c):
    nb = 1
    while (nb * c) % LANES or (nb * c < 1024 and nb < N_DEV):
        nb *= 2
    assert nb <= N_DEV and (nb * c) % LANES == 0, c
    return nb


def _mm_w8(a, w8, *, name, bm=1024, out_dtype=F32):
    M, K = a.shape
    _, _, c = w8.shape
    nb = _blocks_per_tile(c)
    bm = _div(M, bm)

    def body(a_ref, w_ref, o_ref):
        a_ = a_ref[...]
        for t in range(nb):
            o_ref[:, t * c:(t + 1) * c] = jnp.dot(a_, w_ref[t], preferred_element_type=F32).astype(out_dtype)

    return pl.pallas_call(
        body, name=name, grid=(M // bm, N_DEV // nb),
        in_specs=[pl.BlockSpec((bm, K), lambda i, j: (i, 0)), pl.BlockSpec((nb, K, c), lambda i, j: (j, 0, 0))],
        out_specs=pl.BlockSpec((bm, nb * c), lambda i, j: (i, j)),
        out_shape=jax.ShapeDtypeStruct((M, N_DEV * c), out_dtype),
        compiler_params=_cparams("parallel", "parallel"),
    )(a, w8)


def _mm_w8t(dy, w8, *, name, add=None, out_dtype=F32, bm=1024, bn=1024, after=None, lead=None):
    M = dy.shape[-2]
    _, K, c = w8.shape
    nb = _blocks_per_tile(c)
    nk = N_DEV // nb
    bm, bn = _div(M, bm), _div(K, bn)
    has_add = add is not None
    dims = (((1,), (1,)), ((), ()))

    def body(*refs):
        if has_add:
            dy_ref, w_ref, add_ref, o_ref, acc = refs
        else:
            dy_ref, w_ref, o_ref, acc = refs
        p = lax.dot_general(dy_ref[:, 0:c], w_ref[0], dims, preferred_element_type=F32)
        for t in range(1, nb):
            p = p + lax.dot_general(dy_ref[:, t * c:(t + 1) * c], w_ref[t], dims, preferred_element_type=F32)
        k = pl.program_id(2)

        @pl.when(k == 0)
        def _():
            acc[...] = p

        @pl.when(k > 0)
        def _():
            acc[...] += p

        @pl.when(k == nk - 1)
        def _():
            r = acc[...]
            if has_add:
                r = r + add_ref[...]
            o_ref[...] = r.astype(out_dtype)

    o_spec = pl.BlockSpec((bm, bn), lambda i, j, k: (i, j))
    dy_spec = (pl.BlockSpec((bm, nb * c), lambda i, j, k: (i, k)) if lead is None
               else pl.BlockSpec((None, bm, nb * c), lambda i, j, k: (lead, i, k)))
    in_specs = [dy_spec, pl.BlockSpec((nb, bn, c), lambda i, j, k: (k, j, 0))]
    in_specs += [o_spec] if has_add else []
    args = (dy, w8) + ((add,) if has_add else ())
    body, in_specs, args = _ordered_after(body, len(args), in_specs, args, after)
    return pl.pallas_call(
        body, name=name, grid=(M // bm, K // bn, nk),
        in_specs=in_specs, out_specs=o_spec,
        out_shape=jax.ShapeDtypeStruct((M, K), out_dtype),
        scratch_shapes=[pltpu.VMEM((bm, bn), F32)],
        compiler_params=_cparams("parallel", "parallel", "arbitrary"),
    )(*args)


def _mm_gw8(x, dy, c, *, name, bk=1024, lead=None):
    T, K = x.shape
    nb = _blocks_per_tile(c)
    bk = _div(K, bk)
    dims = (((0,), (0,)), ((), ()))

    def body(x_ref, dy_ref, o_ref):
        x_ = x_ref[...]
        for t in range(nb):
            o_ref[t] = lax.dot_general(x_, dy_ref[:, t * c:(t + 1) * c], dims, preferred_element_type=F32).astype(BF16)

    dy_spec = (pl.BlockSpec((T, nb * c), lambda i, j: (0, j)) if lead is None
               else pl.BlockSpec((None, T, nb * c), lambda i, j: (lead, 0, j)))
    return pl.pallas_call(
        body, name=name, grid=(K // bk, N_DEV // nb),
        in_specs=[pl.BlockSpec((T, bk), lambda i, j: (0, i)), dy_spec],
        out_specs=pl.BlockSpec((nb, bk, c), lambda i, j: (j, i, 0)),
        out_shape=jax.ShapeDtypeStruct((N_DEV, K, c), BF16),
        compiler_params=_cparams("parallel", "parallel"),
    )(x, dy)


def _rms_fwd(x, g, *, name, after=None):
    T, D = x.shape
    tm = _div(T, 256, 8)

    def body(x_ref, g_ref, h_ref):
        xf = x_ref[...]
        r = lax.rsqrt(jnp.mean(xf * xf, axis=-1, keepdims=True) + EPS)
        h_ref[...] = ((xf * r) * g_ref[...]).astype(BF16)

    in_specs = [pl.BlockSpec((tm, D), lambda i: (i, 0)), pl.BlockSpec((1, D), lambda i: (0, 0))]
    body, in_specs, args = _ordered_after(body, 2, in_specs, (x, g), after)
    return pl.pallas_call(
        body, name=name, grid=(T // tm,),
        in_specs=in_specs,
        out_specs=pl.BlockSpec((tm, D), lambda i: (i, 0)),
        out_shape=jax.ShapeDtypeStruct((T, D), BF16),
        compiler_params=_cparams("parallel"),
    )(*args)


def _rms_bwd(x, g, dh, dres, *, name, want_bf16, after=None):
    T, D = x.shape
    tm = _div(T, 256, 8)

    def body(x_ref, g_ref, dh_ref, dres_ref, dx_ref, *rest):
        if want_bf16:
            dxb_ref, dg_ref = rest
        else:
            (dg_ref,) = rest
        xf = x_ref[...]
        r = lax.rsqrt(jnp.mean(xf * xf, axis=-1, keepdims=True) + EPS)
        xhat = xf * r
        dh_ = dh_ref[...]
        dy = dh_ * g_ref[...]
        dx = dres_ref[...].astype(F32) + r * (dy - xhat * jnp.mean(dy * xhat, axis=-1, keepdims=True))
        dx_ref[...] = dx
        if want_bf16:
            dxb_ref[...] = dx.astype(BF16)
        part = jnp.sum(dh_ * xhat, axis=0, keepdims=True)

        @pl.when(pl.program_id(0) == 0)
        def _():
            dg_ref[...] = part

        @pl.when(pl.program_id(0) > 0)
        def _():
            dg_ref[...] += part

    row = pl.BlockSpec((tm, D), lambda i: (i, 0))
    vec = pl.BlockSpec((1, D), lambda i: (0, 0))
    out_specs = [row] + ([row] if want_bf16 else []) + [vec]
    out_shape = ([jax.ShapeDtypeStruct((T, D), F32)] + ([jax.ShapeDtypeStruct((T, D), BF16)] if want_bf16 else [])
                 + [jax.ShapeDtypeStruct((1, D), F32)])
    body, in_specs, args = _ordered_after(body, 4, [row, vec, row, row], (x, g, dh, dres), after)
    return pl.pallas_call(
        body, name=name, grid=(T // tm,),
        in_specs=in_specs, out_specs=out_specs, out_shape=out_shape,
        compiler_params=_cparams("arbitrary"),
    )(*args)


def _loss_head(x, g, target, *, name):
    T, D = x.shape
    tm = _div(T, 256, 16)

    def body(x_ref, g_ref, t_ref, loss_ref, dxb_ref, dg_ref):
        xf = x_ref[...]
        r = lax.rsqrt(jnp.mean(xf * xf, axis=-1, keepdims=True) + EPS)
        xhat = xf * r
        gain = g_ref[...]
        err = xhat * gain - t_ref[...]
        lpart = 0.5 * jnp.sum(jnp.mean(err * err, axis=-1, keepdims=True), axis=0, keepdims=True)
        dh_ = err * (1.0 / D)
        dy = dh_ * gain
        dx = r * (dy - xhat * jnp.mean(dy * xhat, axis=-1, keepdims=True))
        dxb_ref[...] = dx.astype(BF16)
        part = jnp.sum(dh_ * xhat, axis=0, keepdims=True)

        @pl.when(pl.program_id(0) == 0)
        def _():
            dg_ref[...] = part
            loss_ref[...] = jnp.broadcast_to(lpart, loss_ref.shape)

        @pl.when(pl.program_id(0) > 0)
        def _():
            dg_ref[...] += part
            loss_ref[...] += jnp.broadcast_to(lpart, loss_ref.shape)

    row = pl.BlockSpec((tm, D), lambda i: (i, 0))
    vec = pl.BlockSpec((1, D), lambda i: (0, 0))
    return pl.pallas_call(
        body, name=name, grid=(T // tm,),
        in_specs=[row, vec, row],
        out_specs=[pl.BlockSpec((8, LANES), lambda i: (0, 0)), row, vec],
        out_shape=[jax.ShapeDtypeStruct((8, LANES), F32), jax.ShapeDtypeStruct((T, D), BF16), jax.ShapeDtypeStruct((1, D), F32)],
        compiler_params=_cparams("arbitrary"),
    )(x, g, target)


def _gate_cols(D):
    off_a = 3 * D // 2 + 2 * KV_WIDTH
    off_b = off_a + D
    cw = math.gcd(math.gcd(off_a, off_b), math.gcd(D, 512))
    return cw, off_a // cw, off_b // cw


def _merge_fwd(z, ya, yb, *, name):
    T, D = ya.shape
    cw, ba, bb = _gate_cols(D)
    tm = _div(T, 512, 8)

    def body(ga_ref, gb_ref, ya_ref, yb_ref, m_ref):
        m_ref[...] = (_sigmoid(ga_ref[...].astype(F32)) * ya_ref[...]
                      + _sigmoid(gb_ref[...].astype(F32)) * yb_ref[...]).astype(BF16)

    blk = pl.BlockSpec((tm, cw), lambda i, j: (i, j))
    return pl.pallas_call(
        body, name=name, grid=(T // tm, D // cw),
        in_specs=[pl.BlockSpec((tm, cw), lambda i, j: (i, ba + j)), pl.BlockSpec((tm, cw), lambda i, j: (i, bb + j)), blk, blk],
        out_specs=blk, out_shape=jax.ShapeDtypeStruct((T, D), BF16),
        compiler_params=_cparams("parallel", "parallel"),
    )(z, z, ya, yb)


def _merge_bwd(z, ya, yb, dm, *, name, after=None):
    T, D = ya.shape
    cw, ba, bb = _gate_cols(D)
    nj = D // cw
    assert bb == ba + nj
    tm = _div(T, 512, 8)

    def body(g_ref, ya_ref, yb_ref, dm_ref, dy_ref, dz_ref):
        sig = _sigmoid(g_ref[...].astype(F32))
        dm_ = dm_ref[...]
        y = jnp.where(pl.program_id(1) == 0, ya_ref[...], yb_ref[...])
        dy_ref[...] = (dm_ * sig).astype(BF16)
        dz_ref[...] = (dm_ * y * (sig * (1.0 - sig))).astype(BF16)

    in_specs = [pl.BlockSpec((tm, cw), lambda i, s, j: (i, ba + s * nj + j)),
                pl.BlockSpec((tm, cw), lambda i, s, j: (i, j * (1 - s))),
                pl.BlockSpec((tm, cw), lambda i, s, j: (i, j * s)),
                pl.BlockSpec((tm, cw), lambda i, s, j: (i, j))]
    body, in_specs, args = _ordered_after(body, 4, in_specs, (z, ya, yb, dm), after)
    return pl.pallas_call(
        body, name=name, grid=(T // tm, 2, nj),
        in_specs=in_specs,
        out_specs=[pl.BlockSpec((None, tm, cw), lambda i, s, j: (s, i, j)),
                   pl.BlockSpec((tm, cw), lambda i, s, j: (i, ba + s * nj + j))],
        out_shape=[jax.ShapeDtypeStruct((2, T, D), BF16), jax.ShapeDtypeStruct(z.shape, BF16)],
        compiler_params=_cparams("parallel", "arbitrary", "arbitrary"),
    )(*args)


def _swiglu_mm_fwd(h, wu_t, gate, *, name, bm=1024, bn=512):
    T, D = h.shape
    F = wu_t.shape[0]
    bm, bn = _div(T, bm), _div(F, bn)

    rc = _div(bm, 256, 16)

    def body(h_ref, wu_ref, gin_ref, g_ref, u_ref, act_ref):
        w = wu_ref[...]
        for r in range(0, bm, rc):
            rows = slice(r, r + rc)
            u = lax.dot_general(h_ref[rows, :], w, (((1,), (1,)), ((), ())), preferred_element_type=F32)
            g = gin_ref[rows, :]
            g_ref[rows, :] = g.astype(BF16)
            u_ref[rows, :] = u.astype(BF16)
            act_ref[rows, :] = (g * _sigmoid(g) * u).astype(BF16)

    o_spec = pl.BlockSpec((bm, bn), lambda i, j: (i, j))
    return pl.pallas_call(
        body, name=name, grid=(T // bm, F // bn),
        in_specs=[pl.BlockSpec((bm, D), lambda i, j: (i, 0)), pl.BlockSpec((bn, D), lambda i, j: (j, 0)), o_spec],
        out_specs=[o_spec] * 3, out_shape=[jax.ShapeDtypeStruct((T, F), BF16)] * 3,
        compiler_params=_cparams("parallel", "parallel"),
    )(h, wu_t, gate)


def _swiglu_mm_bwd(dx, w_down, gate, up, *, name, bm=2048, bn=512, after=None):
    T, D = dx.shape
    F = w_down.shape[0]
    bm, bn = _div(T, bm), _div(F, bn)
    dims = (((1,), (1,)), ((), ()))

    rc = _div(bm, 256, 16)

    def body(dx_ref, w_ref, g_ref, u_ref, dg_ref, du_ref):
        w = w_ref[...]
        for r in range(0, bm, rc):
            rows = slice(r, r + rc)
            d = lax.dot_general(dx_ref[rows, :], w, dims, preferred_element_type=F32)
            g = g_ref[rows, :].astype(F32)
            s = _sigmoid(g)
            silu = g * s
            dg_ref[rows, :] = (d * u_ref[rows, :].astype(F32) * (s + silu * (1.0 - s))).astype(BF16)
            du_ref[rows, :] = (d * silu).astype(BF16)

    o_spec = pl.BlockSpec((bm, bn), lambda i, j: (i, j))
    in_specs = [pl.BlockSpec((bm, D), lambda i, j: (i, 0)), pl.BlockSpec((bn, D), lambda i, j: (j, 0)), o_spec, o_spec]
    body, in_specs, args = _ordered_after(body, 4, in_specs, (dx, w_down, gate, up), after)
    out = jax.ShapeDtypeStruct((T, F), BF16)
    return pl.pallas_call(
        body, name=name, grid=(T // bm, F // bn), in_specs=in_specs, out_specs=[o_spec, o_spec], out_shape=[out, out],
        compiler_params=_cparams("parallel", "parallel"),
    )(*args)


def _sgu_fwd(z, gain, ws_b, bs_t, *, name):
    T = z.shape[0]
    SW = gain.shape[1]
    G = SW // BLOCK

    def body(zu_ref, zv_ref, gain_ref, ws_ref, bs_ref, a_ref):
        u = _gelu(zu_ref[...].astype(F32))
        vg = _gelu(zv_ref[...].astype(F32))
        r = lax.rsqrt(jnp.mean(vg * vg, axis=-1, keepdims=True) + EPS)
        vn = ((vg * r) * gain_ref[...]).astype(BF16)
        for g in range(G):
            sl = slice(g * BLOCK, (g + 1) * BLOCK)
            mixed = jnp.dot(ws_ref[g], vn[:, sl], preferred_element_type=F32) + bs_ref[:, g:g + 1]
            a_ref[:, sl] = (u[:, sl] * mixed).astype(BF16)

    return pl.pallas_call(
        body, name=name, grid=(T // BLOCK,),
        in_specs=[pl.BlockSpec((BLOCK, SW), lambda c: (c, 0)), pl.BlockSpec((BLOCK, SW), lambda c: (c, 1)),
                  pl.BlockSpec((1, SW), lambda c: (0, 0)), pl.BlockSpec((G, BLOCK, BLOCK), lambda c: (0, 0, 0)),
                  pl.BlockSpec((BLOCK, G), lambda c: (0, 0))],
        out_specs=pl.BlockSpec((BLOCK, SW), lambda c: (c, 0)),
        out_shape=jax.ShapeDtypeStruct((T, SW), BF16),
        compiler_params=_cparams("parallel"),
    )(z, z, gain, ws_b, bs_t)


def _sgu_bwd(z, gain, ws_b, bs_t, da, dz, *, name):
    T = z.shape[0]
    SW = gain.shape[1]
    G = SW // BLOCK

    def body(zu_ref, zv_ref, gain_ref, ws_ref, bs_ref, da_ref, dz_in_ref, dz_ref, dws_ref, dbs_ref, dgain_ref, dvn_ref):
        first = pl.program_id(0) == 0

        @pl.when(first)
        def _():
            dws_ref[...] = jnp.zeros_like(dws_ref)
            dbs_ref[...] = jnp.zeros_like(dbs_ref)
            dgain_ref[...] = jnp.zeros_like(dgain_ref)

        u, du = _gelu_and_grad(zu_ref[...].astype(F32))
        vg, dvg = _gelu_and_grad(zv_ref[...].astype(F32))
        r = lax.rsqrt(jnp.mean(vg * vg, axis=-1, keepdims=True) + EPS)
        xhat = vg * r
        gain_ = gain_ref[...]
        vn = (xhat * gain_).astype(BF16)
        da_ = da_ref[...]
        for g in range(G):
            sl = slice(g * BLOCK, (g + 1) * BLOCK)
            w = ws_ref[g]
            mixed = jnp.dot(w, vn[:, sl], preferred_element_type=F32) + bs_ref[:, g:g + 1]
            dmix = da_[:, sl] * u[:, sl]
            dz_ref[:, sl] = (da_[:, sl] * mixed * du[:, sl]).astype(BF16)
            dmb = dmix.astype(BF16)
            dws_ref[g] += lax.dot_general(dmb, vn[:, sl], (((1,), (1,)), ((), ())), preferred_element_type=F32)
            dbs_ref[:, g:g + 1] += jnp.sum(dmix, axis=-1, keepdims=True)
            dvn_ref[:, sl] = lax.dot_general(w, dmb, (((0,), (0,)), ((), ())), preferred_element_type=F32)
        dvn = dvn_ref[...]
        dgain_ref[...] += jnp.sum(dvn * xhat, axis=0, keepdims=True)
        dy = dvn * gain_
        dv_ = r * (dy - xhat * jnp.mean(dy * xhat, axis=-1, keepdims=True))
        dz_ref[:, SW:] = (dv_ * dvg).astype(BF16)

    row = pl.BlockSpec((BLOCK, SW), lambda c: (c, 0))
    return pl.pallas_call(
        body, name=name, grid=(T // BLOCK,),
        in_specs=[row, pl.BlockSpec((BLOCK, SW), lambda c: (c, 1)),
                  pl.BlockSpec((1, SW), lambda c: (0, 0)), pl.BlockSpec((G, BLOCK, BLOCK), lambda c: (0, 0, 0)),
                  pl.BlockSpec((BLOCK, G), lambda c: (0, 0)), row, _ANY],
        out_specs=[pl.BlockSpec((BLOCK, 2 * SW), lambda c: (c, 0)), pl.BlockSpec((G, BLOCK, BLOCK), lambda c: (0, 0, 0)),
                   pl.BlockSpec((BLOCK, G), lambda c: (0, 0)), pl.BlockSpec((1, SW), lambda c: (0, 0))],
        out_shape=[jax.ShapeDtypeStruct(dz.shape, dz.dtype),
                   jax.ShapeDtypeStruct((G, BLOCK, BLOCK), F32), jax.ShapeDtypeStruct((BLOCK, G), F32),
                   jax.ShapeDtypeStruct((1, SW), F32)],
        input_output_aliases={6: 0},
        scratch_shapes=[pltpu.VMEM((BLOCK, SW), F32)],
        compiler_params=_cparams("arbitrary"),
    )(z, z, gain, ws_b, bs_t, da, dz)


def _bias_table(rel_bias, bmap, *, name):
    H = rel_bias.shape[1]

    def body(rb_ref, bmap_ref, o_ref):
        bm_ = bmap_ref[...]
        for h in range(H):
            acc = jnp.zeros(bm_.shape, F32)
            for b in range(REL_BUCKETS):
                acc = jnp.where(bm_ == b, rb_ref[b, h], acc)
            o_ref[h] = acc

    return pl.pallas_call(
        body, name=name,
        in_specs=[pl.BlockSpec(memory_space=pltpu.SMEM), pl.BlockSpec(memory_space=pltpu.VMEM)],
        out_specs=pl.BlockSpec(memory_space=pltpu.VMEM),
        out_shape=jax.ShapeDtypeStruct((H, BLOCK, 3 * BLOCK), F32),
    )(rel_bias, bmap)


def _attn_probs(q_ref, kb, bias_ref, sink_ref, s_ref, n, T, group):
    H = s_ref.shape[0]
    for h in range(H):
        kv = h // group
        qh = q_ref[:, h * HEAD_DIM:(h + 1) * HEAD_DIM].astype(BF16)
        s_ref[h] = lax.dot_general(qh, kb[:, kv * HEAD_DIM:(kv + 1) * HEAD_DIM], (((1,), (1,)), ((), ())),
                                   preferred_element_type=F32)
    row = lax.broadcasted_iota(jnp.int32, (BLOCK, 3 * BLOCK), 0)
    col = lax.broadcasted_iota(jnp.int32, (BLOCK, 3 * BLOCK), 1)
    key_pos = n * BLOCK + col - BLOCK
    valid = (jnp.abs(col - BLOCK - row) <= BLOCK) & (key_pos >= 0) & (key_pos < T)
    s = s_ref[...] * (HEAD_DIM ** -0.5) + bias_ref[...]
    s = jnp.where(valid[None], s, NEG)
    sink = sink_ref[...]
    m = jnp.maximum(jnp.max(s, axis=-1, keepdims=True), sink)
    e = jnp.exp(s - m)
    es = jnp.exp(sink - m)
    inv = 1.0 / (jnp.sum(e, axis=-1, keepdims=True) + es)
    return e * inv, es * inv


def _attn_fwd(z, kpad, vpad, bias, sink, *, name):
    T = z.shape[0]
    H = bias.shape[0]
    AW = H * HEAD_DIM
    group = H // N_KV_HEADS

    def body(q_ref, k_ref, v_ref, bias_ref, sink_ref, o_ref, s_ref, p_ref):
        n = pl.program_id(0)
        start = pl.multiple_of(n * BLOCK, BLOCK)
        kb = k_ref[pl.ds(start, 3 * BLOCK), :]
        vb = v_ref[pl.ds(start, 3 * BLOCK), :]
        p, _ = _attn_probs(q_ref, kb, bias_ref, sink_ref, s_ref, n, T, group)
        p_ref[...] = p.astype(BF16)
        for h in range(H):
            kv = h // group
            o = jnp.dot(p_ref[h], vb[:, kv * HEAD_DIM:(kv + 1) * HEAD_DIM], preferred_element_type=F32)
            o_ref[:, h * HEAD_DIM:(h + 1) * HEAD_DIM] = o.astype(BF16)

    full_kv = pl.BlockSpec((T + 2 * BLOCK, KV_WIDTH), lambda n: (0, 0))
    return pl.pallas_call(
        body, name=name, grid=(T // BLOCK,),
        in_specs=[pl.BlockSpec((BLOCK, AW), lambda n: (n, 2)), full_kv, full_kv,
                  pl.BlockSpec((H, BLOCK, 3 * BLOCK), lambda n: (0, 0, 0)), pl.BlockSpec((H, 1, 1), lambda n: (0, 0, 0))],
        out_specs=pl.BlockSpec((BLOCK, AW), lambda n: (n, 0)),
        out_shape=jax.ShapeDtypeStruct((T, AW), BF16),
        scratch_shapes=[pltpu.VMEM((H, BLOCK, 3 * BLOCK), F32), pltpu.VMEM((H, BLOCK, 3 * BLOCK), BF16)],
        compiler_params=_cparams("parallel"),
    )(z, kpad, vpad, bias, sink)


def _attn_bwd(z, kpad, vpad, bias, sink, do, dz, *, name):
    T = z.shape[0]
    H = bias.shape[0]
    AW = H * HEAD_DIM
    group = H // N_KV_HEADS
    scale = HEAD_DIM ** -0.5

    def body(q_ref, k_ref, v_ref, bias_ref, sink_ref, do_ref, dz_in_ref, dq_ref, dk_ref, dv_ref, dbias_ref, dsink_ref,
             s_ref, dp_ref, p_ref, ds_ref):
        n = pl.program_id(0)

        @pl.when(n == 0)
        def _():
            dk_ref[...] = jnp.zeros_like(dk_ref)
            dv_ref[...] = jnp.zeros_like(dv_ref)
            dbias_ref[...] = jnp.zeros_like(dbias_ref)
            dsink_ref[...] = jnp.zeros_like(dsink_ref)

        start = pl.multiple_of(n * BLOCK, BLOCK)
        kb = k_ref[pl.ds(start, 3 * BLOCK), :]
        vb = v_ref[pl.ds(start, 3 * BLOCK), :]
        p, p_sink = _attn_probs(q_ref, kb, bias_ref, sink_ref, s_ref, n, T, group)
        s_ref[...] = p
        p_ref[...] = p.astype(BF16)
        for h in range(H):
            kv = h // group
            dp_ref[h] = lax.dot_general(do_ref[:, h * HEAD_DIM:(h + 1) * HEAD_DIM], vb[:, kv * HEAD_DIM:(kv + 1) * HEAD_DIM],
                                        (((1,), (1,)), ((), ())), preferred_element_type=F32)
        p = s_ref[...]
        dp = dp_ref[...]
        delta = jnp.sum(p * dp, axis=-1, keepdims=True)
        ds = p * (dp - delta)
        dbias_ref[...] += ds
        dsink_ref[...] += -(p_sink * delta)
        ds_ref[...] = ds.astype(BF16)
        for kv in range(N_KV_HEADS):
            ksl = slice(kv * HEAD_DIM, (kv + 1) * HEAD_DIM)
            dk_acc = jnp.zeros((3 * BLOCK, HEAD_DIM), F32)
            dv_acc = jnp.zeros((3 * BLOCK, HEAD_DIM), F32)
            for gi in range(group):
                h = kv * group + gi
                hsl = slice(h * HEAD_DIM, (h + 1) * HEAD_DIM)
                dsb = ds_ref[h]
                dq = jnp.dot(dsb, kb[:, ksl], preferred_element_type=F32) * scale
                dq_ref[:, hsl] = dq.astype(BF16)
                dk_acc = dk_acc + lax.dot_general(dsb, q_ref[:, hsl].astype(BF16), (((0,), (0,)), ((), ())),
                                                  preferred_element_type=F32)
                dv_acc = dv_acc + lax.dot_general(p_ref[h], do_ref[:, hsl], (((0,), (0,)), ((), ())),
                                                  preferred_element_type=F32)
            dk_ref[pl.ds(start, 3 * BLOCK), ksl] += dk_acc * scale
            dv_ref[pl.ds(start, 3 * BLOCK), ksl] += dv_acc

    full_kv = pl.BlockSpec((T + 2 * BLOCK, KV_WIDTH), lambda n: (0, 0))
    bias_spec = pl.BlockSpec((H, BLOCK, 3 * BLOCK), lambda n: (0, 0, 0))
    row = pl.BlockSpec((BLOCK, AW), lambda n: (n, 0))
    q_cols = pl.BlockSpec((BLOCK, AW), lambda n: (n, 2))
    band = (H, BLOCK, 3 * BLOCK)
    return pl.pallas_call(
        body, name=name, grid=(T // BLOCK,),
        in_specs=[q_cols, full_kv, full_kv, bias_spec, pl.BlockSpec((H, 1, 1), lambda n: (0, 0, 0)), row, _ANY],
        out_specs=[q_cols, full_kv, full_kv, bias_spec, pl.BlockSpec((H, BLOCK, 1), lambda n: (0, 0, 0))],
        out_shape=[jax.ShapeDtypeStruct(dz.shape, dz.dtype),
                   jax.ShapeDtypeStruct((T + 2 * BLOCK, KV_WIDTH), F32), jax.ShapeDtypeStruct((T + 2 * BLOCK, KV_WIDTH), F32),
                   jax.ShapeDtypeStruct(band, F32), jax.ShapeDtypeStruct((H, BLOCK, 1), F32)],
        input_output_aliases={6: 0},
        scratch_shapes=[pltpu.VMEM(band, F32), pltpu.VMEM(band, F32), pltpu.VMEM(band, BF16), pltpu.VMEM(band, BF16)],
        compiler_params=_cparams("arbitrary"),
    )(z, kpad, vpad, bias, sink, do, dz)


def _dkv_into(dkp, dvp, dz, *, name):
    T = dz.shape[0]
    D = (dz.shape[1] - 2 * KV_WIDTH) * 2 // 7
    col = (D + D // 2) // (2 * KV_WIDTH)
    assert col * 2 * KV_WIDTH == D + D // 2

    def body(dk_ref, dv_ref, dz_in_ref, o_ref):
        o_ref[:, :KV_WIDTH] = dk_ref[...].astype(BF16)
        o_ref[:, KV_WIDTH:] = dv_ref[...].astype(BF16)

    kv = pl.BlockSpec((BLOCK, KV_WIDTH), lambda n: (n + 1, 0))
    return pl.pallas_call(
        body, name=name, grid=(T // BLOCK,),
        in_specs=[kv, kv, _ANY], out_specs=pl.BlockSpec((BLOCK, 2 * KV_WIDTH), lambda n: (n, col)),
        out_shape=jax.ShapeDtypeStruct(dz.shape, dz.dtype), input_output_aliases={2: 0},
        compiler_params=_cparams("parallel"),
    )(dkp, dvp, dz)


def _kv_pad(z, *, name):
    T = z.shape[0]
    D = (z.shape[1] - 2 * KV_WIDTH) * 2 // 7
    kcol = (D + D // 2) // KV_WIDTH
    nb = T // BLOCK

    def body(k_ref, v_ref, ko_ref, vo_ref):
        b = pl.program_id(0)
        inside = (b >= 1) & (b <= nb)
        ko_ref[...] = jnp.where(inside, k_ref[...].astype(F32), 0.0).astype(BF16)
        vo_ref[...] = jnp.where(inside, v_ref[...].astype(F32), 0.0).astype(BF16)

    out = jax.ShapeDtypeStruct((T + 2 * BLOCK, KV_WIDTH), BF16)
    o_spec = pl.BlockSpec((BLOCK, KV_WIDTH), lambda b: (b, 0))
    return pl.pallas_call(
        body, name=name, grid=(nb + 2,),
        in_specs=[pl.BlockSpec((BLOCK, KV_WIDTH), lambda b: (jnp.clip(b - 1, 0, nb - 1), kcol)),
                  pl.BlockSpec((BLOCK, KV_WIDTH), lambda b: (jnp.clip(b - 1, 0, nb - 1), kcol + 1))],
        out_specs=[o_spec, o_spec], out_shape=[out, out],
        compiler_params=_cparams("parallel"),
    )(z, z)


def _attn_small_grads(dbias, dsink_rows, bmap, after, *, name):
    H = dbias.shape[0]

    def body(dbias_ref, dsink_ref, bmap_ref, drel_ref, ds_ref):
        bm_ = bmap_ref[...]
        for h in range(H):
            d = dbias_ref[h]
            for b in range(REL_BUCKETS):
                drel_ref[b, h] = jnp.sum(jnp.where(bm_ == b, d, 0.0))
            ds_ref[0, h] = jnp.sum(dsink_ref[h])

    vmem = pl.BlockSpec(memory_space=pltpu.VMEM)
    smem = pl.BlockSpec(memory_space=pltpu.SMEM)
    body, in_specs, args = _ordered_after(body, 3, [vmem, vmem, vmem], (dbias, dsink_rows, bmap), after)
    return pl.pallas_call(
        body, name=name, in_specs=in_specs, out_specs=[smem, smem],
        out_shape=[jax.ShapeDtypeStruct((REL_BUCKETS, H), F32), jax.ShapeDtypeStruct((1, H), F32)],
    )(*args)


def _local_step(x, target, weight, emit, flush, share, norm_mix, v_gain, w_s, b_s, sink, rel_bias, norm_ffn, norm_final,
                early=()):
    T, D = x.shape
    ws_b = w_s.astype(BF16)
    bs_t = b_s.T
    bmap = jnp.asarray(_bucket_map())
    sink = sink.reshape(-1, 1, 1)

    bias = _bias_table(rel_bias, bmap, name="bias_table")
    h = _rms_fwd(x, norm_mix, name="rms_mix", after=[bias, *early])
    w_in = weight("w_in", h)
    z = _mm(h, w_in, tb=True, out_dtype=BF16, name="mm_z", bm=2048, bn=768)
    a = _sgu_fwd(z, v_gain, ws_b, bs_t, name="sgu_fwd")
    w_a = weight("w_a_out", a)
    ya = _mm_w8(a, w_a, name="mm_ya", bm=2048, out_dtype=BF16)
    kpad, vpad = _kv_pad(z, name="kv_pad")
    o = _attn_fwd(z, kpad, vpad, bias, sink, name="attn_fwd")
    w_b = weight("w_b_out", o)
    yb = _mm_w8(o, w_b, name="mm_yb", bm=2048, out_dtype=BF16)
    m = _merge_fwd(z, ya, yb, name="merge_fwd")
    w_o = weight("w_o", m)
    x1, h2 = _mm_resid_rms(m, w_o, x, norm_ffn, name="mm_x1_rms")
    w_gate = weight("w_gate", h2)
    gate = _mm(h2, w_gate, tb=True, name="mm_gate", bm=2048, bn=512)
    w_up = weight("w_up", gate)
    gate, up, act = _swiglu_mm_fwd(h2, w_up, gate, name="mm_up_swiglu")
    w_down = weight("w_down", act)
    x2 = _mm(act, w_down, name="mm_x2", add=x1, bm=1024, bn=512)
    loss, dx2b, g_norm_final = _loss_head(x2, norm_final, target, name="loss_head")

    g_w_down = _mm(act, dx2b, ta=True, out_dtype=BF16, name="mm_gwdown", bm=512, bn=2048)
    tok = emit(("w_down",), (g_w_down,))
    dgate, dup = _swiglu_mm_bwd(dx2b, w_down, gate, up, name="mm_dact_swiglu", after=tok)
    tok = flush(dgate)
    g_w_gate = _mm(dgate, h2, ta=True, out_dtype=BF16, name="mm_gwgate", bm=512, bn=2048, after=tok)
    g_w_up = _mm(dup, h2, ta=True, out_dtype=BF16, name="mm_gwup", bm=512, bn=2048)
    tok = emit(("w_gate", "w_up"), (g_w_gate, g_w_up))
    dh2 = _mm_sum2(dgate, w_gate, dup, w_up, name="mm_dh2", after=tok)
    tok = flush(dh2)
    dx1, dx1b, g_norm_ffn = _rms_bwd(x1, norm_ffn, dh2, dx2b, name="rms_ffn_bwd", want_bf16=True, after=tok)

    g_w_o = _mm(m, dx1b, ta=True, out_dtype=BF16, name="mm_gwo", bm=2048, bn=512)
    tok = emit(("w_o",), (g_w_o,))
    dm = _mm(dx1b, w_o, tb=True, name="mm_dm", bm=2048, bn=512, after=tok)
    tok = flush(dm)
    dy, dz = _merge_bwd(z, ya, yb, dm, name="merge_bwd", after=tok)
    g_w_a = _mm_gw8(a, dy, w_a.shape[2], name="mm_gwa", lead=0)
    g_w_b = _mm_gw8(o, dy, w_b.shape[2], name="mm_gwb", lead=1)
    tok = emit(("w_a_out", "w_b_out"), (g_w_a, g_w_b))
    da = _mm_w8t(dy, w_a, name="mm_da", bm=2048, bn=512, after=tok, lead=0)
    tok = flush(da)
    do = _mm_w8t(dy, w_b, out_dtype=BF16, name="mm_do", bm=2048, bn=512, after=tok, lead=1)
    dz, g_w_s, g_b_s_t, g_v_gain = _sgu_bwd(z, v_gain, ws_b, bs_t, da, dz, name="sgu_bwd")
    dz, dkp, dvp, dbias, dsink_rows = _attn_bwd(z, kpad, vpad, bias, sink, do, dz, name="attn_bwd")
    dz = _dkv_into(dkp, dvp, dz, name="dkv_into_dz")
    g_rel_bias, g_sink = _attn_small_grads(dbias, dsink_rows, bmap, None, name="attn_small_grads")
    tok = share(dict(sgu_v_gain=g_v_gain, sgu_w_s=g_w_s, sgu_b_s=g_b_s_t.T, attn_sink=g_sink, rel_bias=g_rel_bias,
                     norm_ffn=g_norm_ffn, norm_final=g_norm_final, loss=loss[0, 0]))
    g_w_in = _mm(dz, h, ta=True, out_dtype=BF16, name="mm_gwin", bm=768, bn=2048, after=tok)
    tok = emit(("w_in",), (g_w_in,))
    half = dict(bm=T // 2, bn=256)
    dh = _mm(dz, w_in, name="mm_dh_top", row_blocks=(0, 1), after=tok, **half)
    tok = flush(dh)
    dh = _mm(dz, w_in, name="mm_dh_bottom", row_blocks=(1, 1), into=dh, after=tok, **half)
    grad_x, g_norm_mix = _rms_bwd(x, norm_mix, dh, dx1, name="rms_mix_bwd", want_bf16=False)
    return grad_x, g_norm_mix


def _position():
    return lax.axis_index("x"), lax.axis_index("y"), lax.axis_index("c")


def _other_chips(x, y):
    return [(1 - x, y), (x, 1 - y), (1 - x, 1 - y)]


def _slot(px, py, pc):
    return 4 * px + 2 * py + pc


_HBM = pl.BlockSpec(memory_space=pltpu.HBM)
_SEM = pl.BlockSpec(memory_space=pltpu.SEMAPHORE)
_DATAFLOW = pltpu.SideEffectType.DATAFLOW_SIDE_EFFECTING


def _in_hbm(a):
    return pltpu.with_memory_space_constraint(a, pltpu.HBM)


def _own_slot(shard, pos, *, name, after=None):
    R, C = shard.shape
    tr = _div(R, 256, 16)

    def body(pos_ref, w_ref, o_ref):
        o_ref[...] = w_ref[...].astype(BF16)

    body, in_specs, args = _ordered_after(body, 2, [pl.BlockSpec((tr, C), lambda i, pos_ref: (i, 0))], (pos, shard), after)
    grid_spec = pltpu.PrefetchScalarGridSpec(
        num_scalar_prefetch=1, grid=(R // tr,), in_specs=in_specs,
        out_specs=pl.BlockSpec((None, tr, C), lambda i, pos_ref: (pos_ref[0], i, 0)))
    return pl.pallas_call(
        body, name=name, grid_spec=grid_spec,
        out_shape=jax.ShapeDtypeStruct((N_DEV, R, C), BF16),
        compiler_params=_cparams("parallel"),
    )(*args)


def _ag_copies(w, land_ref, send_sems, recv_sems):
    x, y, c = _position()
    mine = land_ref.at[_slot(x, y, c)]
    targets = [(px, py, c) for px, py in _other_chips(x, y)] + [(x, y, 1 - c)]
    return [pltpu.make_async_remote_copy(src_ref=mine, dst_ref=mine, send_sem=send_sems.at[4 * w + k],
                                         recv_sem=recv_sems.at[4 * w + k], device_id=to, device_id_type=MESH)
            for k, to in enumerate(targets)]


def _ag_start(buffers, groups, *, name, after=None):
    lands = [buffers[i] for g in groups for i in g]
    n, ng = len(lands), len(groups)
    sizes = [len(g) for g in groups]

    def body(*refs):
        land_refs = refs[:n]
        sems = refs[n:n + 2 * ng]
        token = refs[-1]
        i = 0
        for g in range(ng):
            for w in range(sizes[g]):
                for cp in _ag_copies(w, land_refs[i], sems[2 * g], sems[2 * g + 1]):
                    cp.start()
                i += 1
        token[...] = jnp.zeros_like(token)

    sem_shapes = [pltpu.SemaphoreType.DMA((4 * k,)) for k in sizes for _ in range(2)]
    body, in_specs, args = _ordered_after(body, n, [_HBM] * n, tuple(_in_hbm(a) for a in lands), after)
    outs = pl.pallas_call(
        body, name=name,
        in_specs=in_specs,
        out_specs=tuple([_SEM] * (2 * ng) + [_HBM] * n + [pl.BlockSpec(memory_space=pltpu.VMEM)]),
        out_shape=tuple(sem_shapes + [pltpu.HBM(a.shape, a.dtype) for a in lands] + [jax.ShapeDtypeStruct((8, LANES), F32)]),
        input_output_aliases={i: 2 * ng + i for i in range(n)},
        compiler_params=pltpu.CompilerParams(has_side_effects=_DATAFLOW),
    )(*args)
    sems, thru = outs[:2 * ng], outs[2 * ng:2 * ng + n]
    result, i = [], 0
    for g in range(ng):
        k = sizes[g]
        result.append((sems[2 * g], sems[2 * g + 1], list(thru[i:i + k])))
        i += k
    return result, outs[-1]


def _ag_wait(send_sems, recv_sems, lands, after, *, name):
    n = len(lands)

    def body(*refs):
        land_refs = refs[:n]
        send_ref, recv_ref = refs[n], refs[n + 1]
        token = refs[-1]
        for w in range(n):
            for cp in _ag_copies(w, land_refs[w], send_ref, recv_ref):
                cp.wait_send()
                cp.wait_recv()
        token[...] = jnp.zeros_like(token)

    outs = pl.pallas_call(
        body, name=name,
        in_specs=[_HBM] * n + [_SEM, _SEM, _ANY],
        out_specs=tuple([_HBM] * n + [pl.BlockSpec(memory_space=pltpu.VMEM)]),
        out_shape=tuple([pltpu.HBM(a.shape, a.dtype) for a in lands] + [jax.ShapeDtypeStruct((8, LANES), F32)]),
        input_output_aliases={i: i for i in range(n)},
        compiler_params=pltpu.CompilerParams(has_side_effects=_DATAFLOW),
    )(*lands, send_sems, recv_sems, after)
    return list(outs[:n]), outs[n]


def _ag_forward(lands, *, name, after=None):
    n = len(lands)

    def body(*refs):
        in_refs, out_refs = refs[:n], refs[n:2 * n]
        send_sems, recv_sems = refs[2 * n:]
        x, y, c = _position()
        copies = []
        for w in range(n):
            for k, (px, py) in enumerate(_other_chips(x, y)):
                cp = pltpu.make_async_remote_copy(
                    src_ref=in_refs[w].at[_slot(px, py, c)], dst_ref=out_refs[w].at[_slot(px, py, c)],
                    send_sem=send_sems.at[3 * w + k], recv_sem=recv_sems.at[3 * w + k],
                    device_id=(x, y, 1 - c), device_id_type=MESH)
                cp.start()
                copies.append(cp)
        for cp in copies:
            cp.wait()

    body, in_specs, args = _ordered_after(body, n, [_ANY] * n, tuple(lands), after)
    return pl.pallas_call(
        body, name=name,
        in_specs=in_specs, out_specs=[_ANY] * n,
        out_shape=[jax.ShapeDtypeStruct(a.shape, a.dtype) for a in lands],
        input_output_aliases={i: i for i in range(n)},
        scratch_shapes=[pltpu.SemaphoreType.DMA((3 * n,)), pltpu.SemaphoreType.DMA((3 * n,))],
    )(*args)


def _sibling_copies(w, g8_ref, land_ref, send_sems, recv_sems):
    x, y, c = _position()
    return [pltpu.make_async_remote_copy(src_ref=g8_ref.at[2 * p + (1 - c)], dst_ref=land_ref.at[p],
                                         send_sem=send_sems.at[4 * w + p], recv_sem=recv_sems.at[4 * w + p],
                                         device_id=(x, y, 1 - c), device_id_type=MESH)
            for p in range(4)]


def _chip_copies(w, sums_ref, land_ref, send_sems, recv_sems):
    x, y, c = _position()
    return [pltpu.make_async_remote_copy(src_ref=sums_ref.at[2 * px + py], dst_ref=land_ref.at[k],
                                         send_sem=send_sems.at[3 * w + k], recv_sem=recv_sems.at[3 * w + k],
                                         device_id=(px, py, c), device_id_type=MESH)
            for k, (px, py) in enumerate(_other_chips(x, y))]


def _copies_start(copies, per_weight, srcs, *, name):
    n = len(srcs)
    lands = [lax.empty((per_weight,) + s.shape[1:], s.dtype) for s in srcs]

    def body(*refs):
        src_refs, land_refs = refs[:n], refs[n:2 * n]
        send_sems, recv_sems = refs[2 * n], refs[2 * n + 1]
        token = refs[-1]
        for w in range(n):
            for cp in copies(w, src_refs[w], land_refs[w], send_sems, recv_sems):
                cp.start()
        token[...] = jnp.zeros_like(token)

    outs = pl.pallas_call(
        body, name=name,
        in_specs=[_HBM] * (2 * n),
        out_specs=tuple([_SEM, _SEM] + [_HBM] * (2 * n) + [pl.BlockSpec(memory_space=pltpu.VMEM)]),
        out_shape=tuple([pltpu.SemaphoreType.DMA((per_weight * n,)), pltpu.SemaphoreType.DMA((per_weight * n,))]
                        + [pltpu.HBM(a.shape, a.dtype) for a in srcs + lands] + [jax.ShapeDtypeStruct((8, LANES), F32)]),
        input_output_aliases={i: 2 + i for i in range(2 * n)},
        compiler_params=pltpu.CompilerParams(has_side_effects=_DATAFLOW),
    )(*[_in_hbm(a) for a in srcs + lands])
    return outs[0], outs[1], list(outs[2:2 + n]), list(outs[2 + n:2 + 2 * n]), outs[-1]


def _copies_wait(copies, send_sems, recv_sems, srcs, lands, after, *, name):
    n = len(srcs)

    def body(*refs):
        src_refs, land_refs = refs[:n], refs[n:2 * n]
        send_ref, recv_ref = refs[2 * n], refs[2 * n + 1]
        for w in range(n):
            for cp in copies(w, src_refs[w], land_refs[w], send_ref, recv_ref):
                cp.wait_send()
                cp.wait_recv()

    outs = pl.pallas_call(
        body, name=name,
        in_specs=[_HBM] * (2 * n) + [_SEM, _SEM, _ANY],
        out_specs=tuple([_HBM] * (2 * n)),
        out_shape=tuple(pltpu.HBM(a.shape, a.dtype) for a in srcs + lands),
        input_output_aliases={i: i for i in range(2 * n)},
        compiler_params=pltpu.CompilerParams(has_side_effects=_DATAFLOW),
    )(*srcs, *lands, send_sems, recv_sems, after)
    return list(outs[:n]), list(outs[n:])


def _chip_sums(g8, from_sibling, pos, *, name):
    _, R, C = g8.shape
    tr = _div(R, 512, 16)

    def body(pos_ref, g_ref, s_ref, o_ref):
        o_ref[...] = (g_ref[...].astype(F32) + s_ref[...].astype(F32)).astype(BF16)

    def chip(k, pos_ref):
        return jnp.where(k >= pos_ref[1], k + 1, k)

    grid_spec = pltpu.PrefetchScalarGridSpec(
        num_scalar_prefetch=1, grid=(3, R // tr),
        in_specs=[pl.BlockSpec((None, tr, C), lambda k, i, pos_ref: (2 * chip(k, pos_ref) + pos_ref[2], i, 0)),
                  pl.BlockSpec((None, tr, C), lambda k, i, pos_ref: (chip(k, pos_ref), i, 0))],
        out_specs=pl.BlockSpec((None, tr, C), lambda k, i, pos_ref: (chip(k, pos_ref), i, 0)))
    return pl.pallas_call(
        body, name=name, grid_spec=grid_spec,
        out_shape=jax.ShapeDtypeStruct((4, R, C), BF16),
        compiler_params=_cparams("parallel", "parallel"),
    )(pos, g8, from_sibling)


def _small_all_reduce(packed, after, *, name):
    R, L = packed.shape

    def body(x_ref, sum_ref, gath_ref, send_sems, recv_sems, local_sem):
        x, y, c = _position()
        me, sibling = (x, y, c), (x, y, 1 - c)
        chips = _other_chips(x, y)

        def rows(px, py, pc):
            return gath_ref.at[pl.ds(_slot(px, py, pc) * R, R), :]

        def copy(k, block, to, src=None):
            return pltpu.make_async_remote_copy(
                src_ref=rows(*block) if src is None else src, dst_ref=rows(*block),
                send_sem=send_sems.at[k], recv_sem=recv_sems.at[k], device_id=to, device_id_type=MESH)

        mine = pltpu.make_async_copy(x_ref, rows(*me), local_sem)
        mine.start()
        first = [copy(0, me, sibling, src=x_ref)]
        first += [copy(1 + j, me, (*chip, c), src=x_ref) for j, chip in enumerate(chips)]
        for cp in first:
            cp.start()
        passed = [copy(4 + j, (*chip, c), sibling) for j, chip in enumerate(chips)]
        for j, chip in enumerate(chips):
            copy(1 + j, (*chip, c), me).wait_recv()
            passed[j].start()
        copy(0, sibling, me).wait_recv()
        for j, chip in enumerate(chips):
            copy(4 + j, (*chip, 1 - c), me).wait_recv()
        for cp in first + passed:
            cp.wait_send()
        mine.wait()
        acc = gath_ref[0:R, :]
        for d in range(1, N_DEV):
            acc = acc + gath_ref[d * R:(d + 1) * R, :]
        sum_ref[...] = acc

    vmem = pl.BlockSpec(memory_space=pltpu.VMEM)
    body, in_specs, args = _ordered_after(body, 1, [vmem], (packed,), after)
    return pl.pallas_call(
        body, name=name, in_specs=in_specs, out_specs=vmem,
        out_shape=jax.ShapeDtypeStruct((R, L), F32),
        scratch_shapes=[pltpu.VMEM((N_DEV * R, L), F32), pltpu.SemaphoreType.DMA((7,)), pltpu.SemaphoreType.DMA((7,)),
                        pltpu.SemaphoreType.DMA],
        compiler_params=pltpu.CompilerParams(vmem_limit_bytes=VMEM_LIMIT),
    )(*args)


def _peer_copies(land_ref, send_sems, recv_sems):
    x, y, c = _position()
    mine = land_ref.at[_slot(x, y, c)]
    copies = []
    for k in range(1, N_DEV):
        to = (1 - x if k & 4 else x, 1 - y if k & 2 else y, 1 - c if k & 1 else c)
        copies.append(pltpu.make_async_remote_copy(src_ref=mine, dst_ref=mine, send_sem=send_sems.at[k - 1],
                                                   recv_sem=recv_sems.at[k - 1], device_id=to, device_id_type=MESH))
    return copies


def _small_exchange_start(buf, *, name):
    def body(buf_ref, send_sems, recv_sems, thru_ref, token):
        for cp in _peer_copies(buf_ref, send_sems, recv_sems):
            cp.start()
        token[...] = jnp.zeros_like(token)

    return pl.pallas_call(
        body, name=name, in_specs=[_HBM],
        out_specs=(_SEM, _SEM, _HBM, pl.BlockSpec(memory_space=pltpu.VMEM)),
        out_shape=(pltpu.SemaphoreType.DMA((N_DEV - 1,)), pltpu.SemaphoreType.DMA((N_DEV - 1,)),
                   pltpu.HBM(buf.shape, buf.dtype), jax.ShapeDtypeStruct((8, LANES), F32)),
        input_output_aliases={0: 2},
        compiler_params=pltpu.CompilerParams(has_side_effects=_DATAFLOW),
    )(_in_hbm(buf))


def _small_exchange_wait(send_sems, recv_sems, buf, after, *, name):
    def body(buf_ref, send_ref, recv_ref, after_ref, out_ref):
        for cp in _peer_copies(buf_ref, send_ref, recv_ref):
            cp.wait_send()
            cp.wait_recv()

    return pl.pallas_call(
        body, name=name, in_specs=[_HBM, _SEM, _SEM, _ANY], out_specs=_HBM,
        out_shape=pltpu.HBM(buf.shape, buf.dtype), input_output_aliases={0: 0},
        compiler_params=pltpu.CompilerParams(has_side_effects=_DATAFLOW),
    )(buf, send_sems, recv_sems, after)


def _sum_slots(buf, *, name):
    _, R, L = buf.shape

    def body(buf_ref, sum_ref):
        acc = buf_ref[0]
        for d in range(1, N_DEV):
            acc = acc + buf_ref[d]
        sum_ref[...] = acc

    vmem = pl.BlockSpec(memory_space=pltpu.VMEM)
    return pl.pallas_call(body, name=name, in_specs=[vmem], out_specs=vmem,
                          out_shape=jax.ShapeDtypeStruct((R, L), F32),
                          compiler_params=pltpu.CompilerParams(vmem_limit_bytes=VMEM_LIMIT))(buf)


def _adamw_math(w, g, m, v):
    m = ADAM_B1 * m + (1.0 - ADAM_B1) * g
    v = ADAM_B2 * v + (1.0 - ADAM_B2) * (g * g)
    m_hat = m / (1.0 - ADAM_B1 ** ADAM_STEP)
    v_hat = v / (1.0 - ADAM_B2 ** ADAM_STEP)
    delta = -ADAM_LR * (m_hat / (jnp.sqrt(v_hat) + ADAM_EPS) + ADAM_WD * w)
    return delta, m, v


def _adamw_shard(w, m, v, g8, from_sibling, from_chips, pos, *, name):
    R, C = w.shape
    tr = _div(R, 256, 16)

    def body(pos_ref, w_ref, m_ref, v_ref, g_ref, s_ref, r_ref, go_ref, d_ref, mo_ref, vo_ref):
        g = g_ref[...].astype(F32) + s_ref[...].astype(F32)
        for k in range(3):
            g = g + r_ref[k].astype(F32)
        delta, m_, v_ = _adamw_math(w_ref[...], g, m_ref[...], v_ref[...])
        go_ref[...] = g
        d_ref[...] = delta
        mo_ref[...] = m_
        vo_ref[...] = v_

    blk = pl.BlockSpec((tr, C), lambda i, pos_ref: (i, 0))
    grid_spec = pltpu.PrefetchScalarGridSpec(
        num_scalar_prefetch=1, grid=(R // tr,),
        in_specs=[blk, blk, blk,
                  pl.BlockSpec((None, tr, C), lambda i, pos_ref: (pos_ref[0], i, 0)),
                  pl.BlockSpec((None, tr, C), lambda i, pos_ref: (pos_ref[1], i, 0)),
                  pl.BlockSpec((3, tr, C), lambda i, pos_ref: (0, i, 0))],
        out_specs=[blk] * 4)
    out = jax.ShapeDtypeStruct((R, C), F32)
    return pl.pallas_call(
        body, name=name, grid_spec=grid_spec, out_shape=[out] * 4,
        compiler_params=_cparams("parallel"),
    )(pos, w, m, v, g8, from_sibling, from_chips)


def _adamw_small(w, g, m, v, *, name):
    R, L = w.shape

    def body(w_ref, g_ref, m_ref, v_ref, d_ref, mo_ref, vo_ref):
        delta, m_, v_ = _adamw_math(w_ref[...], g_ref[...], m_ref[...], v_ref[...])
        d_ref[...] = delta
        mo_ref[...] = m_
        vo_ref[...] = v_

    vmem = pl.BlockSpec(memory_space=pltpu.VMEM)
    out = jax.ShapeDtypeStruct((R, L), F32)
    return pl.pallas_call(body, name=name, in_specs=[vmem] * 4, out_specs=[vmem] * 3, out_shape=[out] * 3)(w, g, m, v)


_TILE = 8 * LANES


def _pack(pieces):
    rows = []
    for p in pieces:
        flat = p.reshape(-1).astype(F32)
        padded = -(-flat.shape[0] // _TILE) * _TILE
        rows.append(jnp.pad(flat, (0, padded - flat.shape[0])).reshape(-1, LANES))
    return jnp.concatenate(rows, axis=0)


def _unpack(packed, like):
    out, r = [], 0
    for p in like:
        size = int(np.prod(p.shape)) if p.shape else 1
        nrows = -(-size // _TILE) * 8
        out.append(packed[r:r + nrows].reshape(-1)[:size].reshape(p.shape))
        r += nrows
    return out


_BIG = ("w_in", "w_a_out", "w_b_out", "w_o", "w_gate", "w_up", "w_down")
_TRANSPOSED = ("w_in", "w_gate", "w_up")
_COL_SHARDED = ("w_a_out", "w_b_out")
_GATHER_GROUPS = (("w_in",), ("w_a_out", "w_b_out", "w_o"), ("w_gate",), ("w_up",), ("w_down",))
_START_AFTER_WAIT = {0: (1, 2), 1: (3,), 2: (4,)}
_SMALL = ("norm_mix", "sgu_v_gain", "sgu_w_s", "sgu_b_s", "attn_sink", "rel_bias", "norm_ffn", "norm_final")
_ORDER = ("w_in", "norm_mix", "sgu_v_gain", "sgu_w_s", "sgu_b_s", "w_a_out", "attn_sink", "rel_bias", "w_b_out", "w_o",
          "norm_ffn", "w_gate", "w_up", "w_down", "norm_final")


def _shard(name, a):
    return jnp.swapaxes(a, 1, 2)[0] if name in _TRANSPOSED else a[0]


def _unshard(name, a):
    return jnp.swapaxes(a[None], 1, 2) if name in _TRANSPOSED else a[None]


def _whole(name, gathered):
    _, r, c = gathered.shape
    return gathered if name in _COL_SHARDED else gathered.reshape(N_DEV * r, c)


def _blocks(name, grad):
    if name in _COL_SHARDED:
        return grad
    r, c = grad.shape
    return grad.reshape(N_DEV, r // N_DEV, c)


def kernel(x, w_in, norm_mix, sgu_v_gain, sgu_w_s, sgu_b_s, w_a_out, attn_sink, rel_bias, w_b_out, w_o, norm_ffn, w_gate, w_up, w_down, norm_final, loss_target, m_w_in, m_norm_mix, m_sgu_v_gain, m_sgu_w_s, m_sgu_b_s, m_w_a_out, m_attn_sink, m_rel_bias, m_w_b_out, m_w_o, m_norm_ffn, m_w_gate, m_w_up, m_w_down, m_norm_final, v_w_in, v_norm_mix, v_sgu_v_gain, v_sgu_w_s, v_sgu_b_s, v_w_a_out, v_attn_sink, v_rel_bias, v_w_b_out, v_w_o, v_norm_ffn, v_w_gate, v_w_up, v_w_down, v_norm_final):
    w = dict(w_in=w_in, norm_mix=norm_mix, sgu_v_gain=sgu_v_gain, sgu_w_s=sgu_w_s, sgu_b_s=sgu_b_s, w_a_out=w_a_out,
             attn_sink=attn_sink, rel_bias=rel_bias, w_b_out=w_b_out, w_o=w_o, norm_ffn=norm_ffn, w_gate=w_gate,
             w_up=w_up, w_down=w_down, norm_final=norm_final)
    m = dict(w_in=m_w_in, norm_mix=m_norm_mix, sgu_v_gain=m_sgu_v_gain, sgu_w_s=m_sgu_w_s, sgu_b_s=m_sgu_b_s,
             w_a_out=m_w_a_out, attn_sink=m_attn_sink, rel_bias=m_rel_bias, w_b_out=m_w_b_out, w_o=m_w_o,
             norm_ffn=m_norm_ffn, w_gate=m_w_gate, w_up=m_w_up, w_down=m_w_down, norm_final=m_norm_final)
    v = dict(w_in=v_w_in, norm_mix=v_norm_mix, sgu_v_gain=v_sgu_v_gain, sgu_w_s=v_sgu_w_s, sgu_b_s=v_sgu_b_s,
             w_a_out=v_w_a_out, attn_sink=v_attn_sink, rel_bias=v_rel_bias, w_b_out=v_w_b_out, w_o=v_w_o,
             norm_ffn=v_norm_ffn, w_gate=v_w_gate, w_up=v_w_up, w_down=v_w_down, norm_final=v_norm_final)
    xc, yc, cc = _position()
    pos = jnp.stack([_slot(xc, yc, cc), 2 * xc + yc, cc]).astype(jnp.int32)

    in_flight, full, slots = {}, {}, {}

    def start_gather(groups, after):
        names = [n for gi in groups for n in _GATHER_GROUPS[gi]]
        flights, token = _ag_start([slots[n] for n in names], [[names.index(n) for n in _GATHER_GROUPS[gi]] for gi in groups],
                                   name="ag_start_%d" % groups[0], after=after)
        in_flight.update(zip(groups, flights))
        return token

    def weight(name, after):
        if name not in full:
            gi = next(i for i, grp in enumerate(_GATHER_GROUPS) if name in grp)
            send_sems, recv_sems, lands = in_flight[gi]
            lands, token = _ag_wait(send_sems, recv_sems, lands, after, name="ag_wait_%d" % gi)
            started = start_gather(_START_AFTER_WAIT[gi], token) if gi in _START_AFTER_WAIT else None
            gathered = _ag_forward(lands, name="ag_forward_%d" % gi, after=started)
            full.update({n: _whole(n, g) for n, g in zip(_GATHER_GROUPS[gi], gathered)})
        return full[name]

    for n in _GATHER_GROUPS[0]:
        slots[n] = _own_slot(_shard(n, w[n]), pos, name="own_slot_" + n)
    first_started = start_gather((0,), None)
    for grp in _GATHER_GROUPS[1:]:
        for n in grp:
            slots[n] = _own_slot(_shard(n, w[n]), pos, name="own_slot_" + n, after=first_started)

    to_sibling, reducing = [], {}

    def emit(names, grads):
        g8 = [_blocks(n, g) for n, g in zip(names, grads)]
        send_sems, recv_sems, g8, lands, token = _copies_start(_sibling_copies, 4, g8, name="rs_sibling_start_" + names[0])
        to_sibling.append((names, send_sems, recv_sems, g8, lands))
        return token

    def flush(after):
        names, send_sems, recv_sems, g8, lands = to_sibling.pop()
        g8, from_sibling = _copies_wait(_sibling_copies, send_sems, recv_sems, g8, lands, after,
                                        name="rs_sibling_wait_" + names[0])
        sums4 = [_chip_sums(g, s, pos, name="chip_sums_" + n) for n, g, s in zip(names, g8, from_sibling)]
        send_sems, recv_sems, sums4, lands, token = _copies_start(_chip_copies, 3, sums4, name="rs_chips_start_" + names[0])
        reducing[names] = (g8, from_sibling, send_sems, recv_sems, sums4, lands)
        return token

    small_like = [w[n] for n in _SMALL]
    shared = {}

    def share(small):
        pieces = [jnp.zeros_like(w[n]) if n == "norm_mix" else small[n] for n in _SMALL] + [small["loss"]]
        packed = _pack(pieces)
        mine = lax.dynamic_update_index_in_dim(jnp.zeros((N_DEV,) + packed.shape, F32), packed, pos[0], 0)
        shared["send"], shared["recv"], shared["buf"], token = _small_exchange_start(mine, name="small_exchange_start")
        return token

    grad_x, g_norm_mix = _local_step(
        x[0], loss_target[0], weight, emit, flush, share, norm_mix, sgu_v_gain, sgu_w_s[0], sgu_b_s[0], attn_sink,
        rel_bias, norm_ffn, norm_final[None], early=[slots[n] for grp in _GATHER_GROUPS[1:] for n in grp])

    out_g, out_d, out_m, out_v = {}, {}, {}, {}
    small_w = _pack(small_like)
    after = grad_x
    for gi, (names, (g8, from_sibling, send_sems, recv_sems, sums4, lands)) in enumerate(reducing.items()):
        if gi == len(reducing) - 1:
            everyone = _small_exchange_wait(shared["send"], shared["recv"], shared["buf"], after, name="small_exchange_wait")
            early_sum = _sum_slots(everyone, name="small_sum")
            late = _pack([g_norm_mix])
            late_sum = _small_all_reduce(late, early_sum, name="norm_mix_all_reduce")
            summed = jnp.concatenate([late_sum, early_sum[late.shape[0]:]], axis=0)
            after = summed
        _, from_chips = _copies_wait(_chip_copies, send_sems, recv_sems, sums4, lands, after,
                                     name="rs_chips_wait_" + names[0])
        for i, n in enumerate(names):
            g, d, m_, v_ = _adamw_shard(_shard(n, w[n]), _shard(n, m[n]), _shard(n, v[n]), g8[i], from_sibling[i],
                                        from_chips[i], pos, name="adamw_" + n)
            out_g[n], out_d[n], out_m[n], out_v[n] = (_unshard(n, o) for o in (g, d, m_, v_))
            after = d
    *small_grads, loss_sum = _unpack(summed, small_like + [jax.ShapeDtypeStruct((), F32)])
    d_s, m_s, v_s = _adamw_small(small_w, summed[:small_w.shape[0]], _pack([m[n] for n in _SMALL]),
                                 _pack([v[n] for n in _SMALL]), name="adamw_small")
    for n, g, d, m_, v_ in zip(_SMALL, small_grads, _unpack(d_s, small_like), _unpack(m_s, small_like), _unpack(v_s, small_like)):
        out_g[n], out_d[n], out_m[n], out_v[n] = g, d, m_, v_

    return (loss_sum, grad_x[None], *[out_g[n] for n in _ORDER], *[out_d[n] for n in _ORDER],
            *[out_m[n] for n in _ORDER], *[out_v[n] for n in _ORDER])
```

```python
import functools
import math

import numpy as np
import jax
import jax.numpy as jnp
from jax import lax
from jax.experimental import pallas as pl
from jax.experimental.pallas import tpu as pltpu

F32 = jnp.float32
BF16 = jnp.bfloat16

EPS = 1e-6
NEG = -1e30
HEAD_DIM = 128
BLOCK = 128
N_KV_HEADS = 2
KV_WIDTH = N_KV_HEADS * HEAD_DIM
REL_BUCKETS = 32
REL_MAX_DIST = 128

ADAM_LR = 0.001
ADAM_B1 = 0.9
ADAM_B2 = 0.999
ADAM_EPS = 1e-08
ADAM_WD = 0.01
ADAM_STEP = 10

N_DEV = 8
LANES = 128
VMEM_LIMIT = 56 * 1024 * 1024
MESH = pl.DeviceIdType.MESH


def _cparams(*sem):
    return pltpu.CompilerParams(dimension_semantics=sem, vmem_limit_bytes=VMEM_LIMIT)


def _div(n, target, mult=LANES):
    best = None
    for d in range(mult, min(n, target) + 1, mult):
        if n % d == 0:
            best = d
    assert best is not None, (n, target, mult)
    return best


_ANY = pl.BlockSpec(memory_space=pl.ANY)


def _ordered_after(body, n_inputs, in_specs, args, after):
    if after is None:
        return body, in_specs, args
    extra = tuple(after) if isinstance(after, (tuple, list)) else (after,)

    def wrapped(*refs):
        return body(*refs[:n_inputs], *refs[n_inputs + len(extra):])

    return wrapped, list(in_specs) + [_ANY] * len(extra), tuple(args) + extra


def _bucket_map():
    nb = REL_BUCKETS // 2
    qi = np.arange(BLOCK)[:, None]
    kj = np.arange(3 * BLOCK)[None, :]
    rel = kj - BLOCK - qi
    ret = np.where(rel > 0, nb, 0)
    n = np.abs(rel)
    max_exact = nb // 2
    nf = np.maximum(n, 1).astype(np.float32)
    large = max_exact + (np.log(nf / np.float32(max_exact)) / np.float32(math.log(REL_MAX_DIST / max_exact))
                         * np.float32(nb - max_exact)).astype(np.int32)
    large = np.minimum(large, nb - 1)
    return (ret + np.where(n < max_exact, n, large)).astype(np.int32)


_GELU_C = math.sqrt(2.0 / math.pi)
_GELU_A = 0.044715


def _gelu(x):
    t = jnp.tanh(_GELU_C * (x + _GELU_A * (x * x * x)))
    return 0.5 * x * (1.0 + t)


def _gelu_and_grad(x):
    x2 = x * x
    t = jnp.tanh(_GELU_C * (x + _GELU_A * (x2 * x)))
    g = 0.5 * x * (1.0 + t)
    dg = 0.5 * (1.0 + t) + 0.5 * x * (1.0 - t * t) * (_GELU_C * (1.0 + 3.0 * _GELU_A * x2))
    return g, dg


def _sigmoid(x):
    return 1.0 / (1.0 + jnp.exp(-x))


def _mm(a, b, *, name, ta=False, tb=False, add=None, out_dtype=F32, bm=1024, bn=1024, bk=None, after=None,
        row_blocks=None, into=None):
    if ta:
        K, M = a.shape
    else:
        M, K = a.shape
    N = b.shape[0] if tb else b.shape[1]
    assert (b.shape[1] if tb else b.shape[0]) == K
    bm = _div(M, bm)
    bn = _div(N, bn)
    bk = K if bk is None else _div(K, bk)
    nk = K // bk
    i0, ni = (0, M // bm) if row_blocks is None else row_blocks
    a_spec = (pl.BlockSpec((bk, bm), lambda i, j, k: (k, i + i0)) if ta
              else pl.BlockSpec((bm, bk), lambda i, j, k: (i + i0, k)))
    b_spec = pl.BlockSpec((bn, bk), lambda i, j, k: (j, k)) if tb else pl.BlockSpec((bk, bn), lambda i, j, k: (k, j))
    o_spec = pl.BlockSpec((bm, bn), lambda i, j, k: (i + i0, j))
    dims = (((0 if ta else 1,), (1 if tb else 0,)), ((), ()))
    has_add = add is not None

    def body(*refs):
        if has_add:
            a_ref, b_ref, add_ref, o_ref, *scratch = refs
        else:
            a_ref, b_ref, o_ref, *scratch = refs
            add_ref = None
        p = lax.dot_general(a_ref[...].astype(BF16), b_ref[...].astype(BF16), dims, preferred_element_type=F32)
        if nk == 1:
            if has_add:
                p = p + add_ref[...]
            o_ref[...] = p.astype(out_dtype)
        else:
            acc = scratch[0]
            k = pl.program_id(2)

            @pl.when(k == 0)
            def _():
                acc[...] = p

            @pl.when(k > 0)
            def _():
                acc[...] += p

            @pl.when(k == nk - 1)
            def _():
                r = acc[...]
                if has_add:
                    r = r + add_ref[...]
                o_ref[...] = r.astype(out_dtype)

    in_specs = [a_spec, b_spec] + ([o_spec] if has_add else [])
    args = (a, b) + ((add,) if has_add else ())
    aliases = {}
    if into is not None:
        body, in_specs, args = _ordered_after(body, len(args), in_specs, args, into)
        aliases = {len(args) - 1: 0}
    body, in_specs, args = _ordered_after(body, len(args), in_specs, args, after)
    return pl.pallas_call(
        body, name=name, grid=(ni, N // bn, nk),
        in_specs=in_specs, out_specs=o_spec,
        out_shape=jax.ShapeDtypeStruct((M, N), out_dtype),
        input_output_aliases=aliases,
        scratch_shapes=[pltpu.VMEM((bm, bn), F32)] if nk > 1 else [],
        compiler_params=_cparams("parallel", "parallel", "arbitrary"),
    )(*args)


def _mm_resid_rms(a, b, resid, gain, *, name, bm=512):
    M, K = a.shape
    N = b.shape[1]
    bm = _div(M, bm)

    def body(a_ref, b_ref, r_ref, g_ref, x_ref, h_ref):
        x = r_ref[...] + jnp.dot(a_ref[...], b_ref[...], preferred_element_type=F32)
        x_ref[...] = x
        r = lax.rsqrt(jnp.mean(x * x, axis=-1, keepdims=True) + EPS)
        h_ref[...] = ((x * r) * g_ref[...]).astype(BF16)

    row = pl.BlockSpec((bm, N), lambda i: (i, 0))
    return pl.pallas_call(
        body, name=name, grid=(M // bm,),
        in_specs=[pl.BlockSpec((bm, K), lambda i: (i, 0)), pl.BlockSpec((K, N), lambda i: (0, 0)), row,
                  pl.BlockSpec((1, N), lambda i: (0, 0))],
        out_specs=[row, row], out_shape=[jax.ShapeDtypeStruct((M, N), F32), jax.ShapeDtypeStruct((M, N), BF16)],
        compiler_params=_cparams("parallel"),
    )(a, b, resid, gain)


def _mm_sum2(a1, b1, a2, b2, *, name, bm=1024, bn=512, bk=2816, after=None):
    M, K = a1.shape
    N = b1.shape[1]
    bm, bn, bk = _div(M, bm), _div(N, bn), _div(K, bk)
    nk = K // bk

    def body(a1_ref, b1_ref, a2_ref, b2_ref, o_ref, acc):
        p = (jnp.dot(a1_ref[...], b1_ref[...], preferred_element_type=F32)
             + jnp.dot(a2_ref[...], b2_ref[...], preferred_element_type=F32))
        k = pl.program_id(2)

        @pl.when(k == 0)
        def _():
            acc[...] = p

        @pl.when(k > 0)
        def _():
            acc[...] += p

        @pl.when(k == nk - 1)
        def _():
            o_ref[...] = acc[...]

    a_spec = pl.BlockSpec((bm, bk), lambda i, j, k: (i, k))
    b_spec = pl.BlockSpec((bk, bn), lambda i, j, k: (k, j))
    body, in_specs, args = _ordered_after(body, 4, [a_spec, b_spec, a_spec, b_spec], (a1, b1, a2, b2), after)
    return pl.pallas_call(
        body, name=name, grid=(M // bm, N // bn, nk),
        in_specs=in_specs, out_specs=pl.BlockSpec((bm, bn), lambda i, j, k: (i, j)),
        out_shape=jax.ShapeDtypeStruct((M, N), F32),
        scratch_shapes=[pltpu.VMEM((bm, bn), F32)],
        compiler_params=_cparams("parallel", "parallel", "arbitrary"),
    )(*args)


def _blocks_per_tile(c):
    nb = 1
    while (nb * c) % LANES or (nb * c < 1024 and nb < N_DEV):
        nb *= 2
    assert nb <= N_DEV and (nb * c) % LANES == 0, c
    return nb


def _mm_w8(a, w8, *, name, bm=1024, out_dtype=F32):
    M, K = a.shape
    _, _, c = w8.shape
    nb = _blocks_per_tile(c)
    bm = _div(M, bm)

    def body(a_ref, w_ref, o_ref):
        a_ = a_ref[...]
        for t in range(nb):
            o_ref[:, t * c:(t + 1) * c] = jnp.dot(a_, w_ref[t], preferred_element_type=F32).astype(out_dtype)

    return pl.pallas_call(
        body, name=name, grid=(M // bm, N_DEV // nb),
        in_specs=[pl.BlockSpec((bm, K), lambda i, j: (i, 0)), pl.BlockSpec((nb, K, c), lambda i, j: (j, 0, 0))],
        out_specs=pl.BlockSpec((bm, nb * c), lambda i, j: (i, j)),
        out_shape=jax.ShapeDtypeStruct((M, N_DEV * c), out_dtype),
        compiler_params=_cparams("parallel", "parallel"),
    )(a, w8)


def _mm_w8t(dy, w8, *, name, add=None, out_dtype=F32, bm=1024, bn=1024, after=None, lead=None):
    M = dy.shape[-2]
    _, K, c = w8.shape
    nb = _blocks_per_tile(c)
    nk = N_DEV // nb
    bm, bn = _div(M, bm), _div(K, bn)
    has_add = add is not None
    dims = (((1,), (1,)), ((), ()))

    def body(*refs):
        if has_add:
            dy_ref, w_ref, add_ref, o_ref, acc = refs
        else:
            dy_ref, w_ref, o_ref, acc = refs
        p = lax.dot_general(dy_ref[:, 0:c], w_ref[0], dims, preferred_element_type=F32)
        for t in range(1, nb):
            p = p + lax.dot_general(dy_ref[:, t * c:(t + 1) * c], w_ref[t], dims, preferred_element_type=F32)
        k = pl.program_id(2)

        @pl.when(k == 0)
        def _():
            acc[...] = p

        @pl.when(k > 0)
        def _():
            acc[...] += p

        @pl.when(k == nk - 1)
        def _():
            r = acc[...]
            if has_add:
                r = r + add_ref[...]
            o_ref[...] = r.astype(out_dtype)

    o_spec = pl.BlockSpec((bm, bn), lambda i, j, k: (i, j))
    dy_spec = (pl.BlockSpec((bm, nb * c), lambda i, j, k: (i, k)) if lead is None
               else pl.BlockSpec((None, bm, nb * c), lambda i, j, k: (lead, i, k)))
    in_specs = [dy_spec, pl.BlockSpec((nb, bn, c), lambda i, j, k: (k, j, 0))]
    in_specs += [o_spec] if has_add else []
    args = (dy, w8) + ((add,) if has_add else ())
    body, in_specs, args = _ordered_after(body, len(args), in_specs, args, after)
    return pl.pallas_call(
        body, name=name, grid=(M // bm, K // bn, nk),
        in_specs=in_specs, out_specs=o_spec,
        out_shape=jax.ShapeDtypeStruct((M, K), out_dtype),
        scratch_shapes=[pltpu.VMEM((bm, bn), F32)],
        compiler_params=_cparams("parallel", "parallel", "arbitrary"),
    )(*args)


def _mm_gw8(x, dy, c, *, name, bk=1024, lead=None):
    T, K = x.shape
    nb = _blocks_per_tile(c)
    bk = _div(K, bk)
    dims = (((0,), (0,)), ((), ()))

    def body(x_ref, dy_ref, o_ref):
        x_ = x_ref[...]
        for t in range(nb):
            o_ref[t] = lax.dot_general(x_, dy_ref[:, t * c:(t + 1) * c], dims, preferred_element_type=F32).astype(BF16)

    dy_spec = (pl.BlockSpec((T, nb * c), lambda i, j: (0, j)) if lead is None
               else pl.BlockSpec((None, T, nb * c), lambda i, j: (lead, 0, j)))
    return pl.pallas_call(
        body, name=name, grid=(K // bk, N_DEV // nb),
        in_specs=[pl.BlockSpec((T, bk), lambda i, j: (0, i)), dy_spec],
        out_specs=pl.BlockSpec((nb, bk, c), lambda i, j: (j, i, 0)),
        out_shape=jax.ShapeDtypeStruct((N_DEV, K, c), BF16),
        compiler_params=_cparams("parallel", "parallel"),
    )(x, dy)


def _rms_fwd(x, g, *, name, after=None):
    T, D = x.shape
    tm = _div(T, 256, 8)

    def body(x_ref, g_ref, h_ref):
        xf = x_ref[...]
        r = lax.rsqrt(jnp.mean(xf * xf, axis=-1, keepdims=True) + EPS)
        h_ref[...] = ((xf * r) * g_ref[...]).astype(BF16)

    in_specs = [pl.BlockSpec((tm, D), lambda i: (i, 0)), pl.BlockSpec((1, D), lambda i: (0, 0))]
    body, in_specs, args = _ordered_after(body, 2, in_specs, (x, g), after)
    return pl.pallas_call(
        body, name=name, grid=(T // tm,),
        in_specs=in_specs,
        out_specs=pl.BlockSpec((tm, D), lambda i: (i, 0)),
        out_shape=jax.ShapeDtypeStruct((T, D), BF16),
        compiler_params=_cparams("parallel"),
    )(*args)


def _rms_bwd(x, g, dh, dres, *, name, want_bf16, after=None):
    T, D = x.shape
    tm = _div(T, 256, 8)

    def body(x_ref, g_ref, dh_ref, dres_ref, dx_ref, *rest):
        if want_bf16:
            dxb_ref, dg_ref = rest
        else:
            (dg_ref,) = rest
        xf = x_ref[...]
        r = lax.rsqrt(jnp.mean(xf * xf, axis=-1, keepdims=True) + EPS)
        xhat = xf * r
        dh_ = dh_ref[...]
        dy = dh_ * g_ref[...]
        dx = dres_ref[...].astype(F32) + r * (dy - xhat * jnp.mean(dy * xhat, axis=-1, keepdims=True))
        dx_ref[...] = dx
        if want_bf16:
            dxb_ref[...] = dx.astype(BF16)
        part = jnp.sum(dh_ * xhat, axis=0, keepdims=True)

        @pl.when(pl.program_id(0) == 0)
        def _():
            dg_ref[...] = part

        @pl.when(pl.program_id(0) > 0)
        def _():
            dg_ref[...] += part

    row = pl.BlockSpec((tm, D), lambda i: (i, 0))
    vec = pl.BlockSpec((1, D), lambda i: (0, 0))
    out_specs = [row] + ([row] if want_bf16 else []) + [vec]
    out_shape = ([jax.ShapeDtypeStruct((T, D), F32)] + ([jax.ShapeDtypeStruct((T, D), BF16)] if want_bf16 else [])
                 + [jax.ShapeDtypeStruct((1, D), F32)])
    body, in_specs, args = _ordered_after(body, 4, [row, vec, row, row], (x, g, dh, dres), after)
    return pl.pallas_call(
        body, name=name, grid=(T // tm,),
        in_specs=in_specs, out_specs=out_specs, out_shape=out_shape,
        compiler_params=_cparams("arbitrary"),
    )(*args)


def _loss_head(x, g, target, *, name):
    T, D = x.shape
    tm = _div(T, 256, 16)

    def body(x_ref, g_ref, t_ref, loss_ref, dxb_ref, dg_ref):
        xf = x_ref[...]
        r = lax.rsqrt(jnp.mean(xf * xf, axis=-1, keepdims=True) + EPS)
        xhat = xf * r
        gain = g_ref[...]
        err = xhat * gain - t_ref[...]
        lpart = 0.5 * jnp.sum(jnp.mean(err * err, axis=-1, keepdims=True), axis=0, keepdims=True)
        dh_ = err * (1.0 / D)
        dy = dh_ * gain
        dx = r * (dy - xhat * jnp.mean(dy * xhat, axis=-1, keepdims=True))
        dxb_ref[...] = dx.astype(BF16)
        part = jnp.sum(dh_ * xhat, axis=0, keepdims=True)

        @pl.when(pl.program_id(0) == 0)
        def _():
            dg_ref[...] = part
            loss_ref[...] = jnp.broadcast_to(lpart, loss_ref.shape)

        @pl.when(pl.program_id(0) > 0)
        def _():
            dg_ref[...] += part
            loss_ref[...] += jnp.broadcast_to(lpart, loss_ref.shape)

    row = pl.BlockSpec((tm, D), lambda i: (i, 0))
    vec = pl.BlockSpec((1, D), lambda i: (0, 0))
    return pl.pallas_call(
        body, name=name, grid=(T // tm,),
        in_specs=[row, vec, row],
        out_specs=[pl.BlockSpec((8, LANES), lambda i: (0, 0)), row, vec],
        out_shape=[jax.ShapeDtypeStruct((8, LANES), F32), jax.ShapeDtypeStruct((T, D), BF16), jax.ShapeDtypeStruct((1, D), F32)],
        compiler_params=_cparams("arbitrary"),
    )(x, g, target)


def _gate_cols(D):
    off_a = 3 * D // 2 + 2 * KV_WIDTH
    off_b = off_a + D
    cw = math.gcd(math.gcd(off_a, off_b), math.gcd(D, 512))
    return cw, off_a // cw, off_b // cw


def _merge_fwd(z, ya, yb, *, name):
    T, D = ya.shape
    cw, ba, bb = _gate_cols(D)
    tm = _div(T, 512, 8)

    def body(ga_ref, gb_ref, ya_ref, yb_ref, m_ref):
        m_ref[...] = (_sigmoid(ga_ref[...].astype(F32)) * ya_ref[...]
                      + _sigmoid(gb_ref[...].astype(F32)) * yb_ref[...]).astype(BF16)

    blk = pl.BlockSpec((tm, cw), lambda i, j: (i, j))
    return pl.pallas_call(
        body, name=name, grid=(T // tm, D // cw),
        in_specs=[pl.BlockSpec((tm, cw), lambda i, j: (i, ba + j)), pl.BlockSpec((tm, cw), lambda i, j: (i, bb + j)), blk, blk],
        out_specs=blk, out_shape=jax.ShapeDtypeStruct((T, D), BF16),
        compiler_params=_cparams("parallel", "parallel"),
    )(z, z, ya, yb)


def _merge_bwd(z, ya, yb, dm, *, name, after=None):
    T, D = ya.shape
    cw, ba, bb = _gate_cols(D)
    nj = D // cw
    assert bb == ba + nj
    tm = _div(T, 512, 8)

    def body(g_ref, ya_ref, yb_ref, dm_ref, dy_ref, dz_ref):
        sig = _sigmoid(g_ref[...].astype(F32))
        dm_ = dm_ref[...]
        y = jnp.where(pl.program_id(1) == 0, ya_ref[...], yb_ref[...])
        dy_ref[...] = (dm_ * sig).astype(BF16)
        dz_ref[...] = (dm_ * y * (sig * (1.0 - sig))).astype(BF16)

    in_specs = [pl.BlockSpec((tm, cw), lambda i, s, j: (i, ba + s * nj + j)),
                pl.BlockSpec((tm, cw), lambda i, s, j: (i, j * (1 - s))),
                pl.BlockSpec((tm, cw), lambda i, s, j: (i, j * s)),
                pl.BlockSpec((tm, cw), lambda i, s, j: (i, j))]
    body, in_specs, args = _ordered_after(body, 4, in_specs, (z, ya, yb, dm), after)
    return pl.pallas_call(
        body, name=name, grid=(T // tm, 2, nj),
        in_specs=in_specs,
        out_specs=[pl.BlockSpec((None, tm, cw), lambda i, s, j: (s, i, j)),
                   pl.BlockSpec((tm, cw), lambda i, s, j: (i, ba + s * nj + j))],
        out_shape=[jax.ShapeDtypeStruct((2, T, D), BF16), jax.ShapeDtypeStruct(z.shape, BF16)],
        compiler_params=_cparams("parallel", "arbitrary", "arbitrary"),
    )(*args)


def _swiglu_mm_fwd(h, wu_t, gate, *, name, bm=1024, bn=512):
    T, D = h.shape
    F = wu_t.shape[0]
    bm, bn = _div(T, bm), _div(F, bn)

    rc = _div(bm, 256, 16)

    def body(h_ref, wu_ref, gin_ref, g_ref, u_ref, act_ref):
        w = wu_ref[...]
        for r in range(0, bm, rc):
            rows = slice(r, r + rc)
            u = lax.dot_general(h_ref[rows, :], w, (((1,), (1,)), ((), ())), preferred_element_type=F32)
            g = gin_ref[rows, :]
            g_ref[rows, :] = g.astype(BF16)
            u_ref[rows, :] = u.astype(BF16)
            act_ref[rows, :] = (g * _sigmoid(g) * u).astype(BF16)

    o_spec = pl.BlockSpec((bm, bn), lambda i, j: (i, j))
    return pl.pallas_call(
        body, name=name, grid=(T // bm, F // bn),
        in_specs=[pl.BlockSpec((bm, D), lambda i, j: (i, 0)), pl.BlockSpec((bn, D), lambda i, j: (j, 0)), o_spec],
        out_specs=[o_spec] * 3, out_shape=[jax.ShapeDtypeStruct((T, F), BF16)] * 3,
        compiler_params=_cparams("parallel", "parallel"),
    )(h, wu_t, gate)


def _swiglu_mm_bwd(dx, w_down, gate, up, *, name, bm=2048, bn=512, after=None):
    T, D = dx.shape
    F = w_down.shape[0]
    bm, bn = _div(T, bm), _div(F, bn)
    dims = (((1,), (1,)), ((), ()))

    rc = _div(bm, 256, 16)

    def body(dx_ref, w_ref, g_ref, u_ref, dg_ref, du_ref):
        w = w_ref[...]
        for r in range(0, bm, rc):
            rows = slice(r, r + rc)
            d = lax.dot_general(dx_ref[rows, :], w, dims, preferred_element_type=F32)
            g = g_ref[rows, :].astype(F32)
            s = _sigmoid(g)
            silu = g * s
            dg_ref[rows, :] = (d * u_ref[rows, :].astype(F32) * (s + silu * (1.0 - s))).astype(BF16)
            du_ref[rows, :] = (d * silu).astype(BF16)

    o_spec = pl.BlockSpec((bm, bn), lambda i, j: (i, j))
    in_specs = [pl.BlockSpec((bm, D), lambda i, j: (i, 0)), pl.BlockSpec((bn, D), lambda i, j: (j, 0)), o_spec, o_spec]
    body, in_specs, args = _ordered_after(body, 4, in_specs, (dx, w_down, gate, up), after)
    out = jax.ShapeDtypeStruct((T, F), BF16)
    return pl.pallas_call(
        body, name=name, grid=(T // bm, F // bn), in_specs=in_specs, out_specs=[o_spec, o_spec], out_shape=[out, out],
        compiler_params=_cparams("parallel", "parallel"),
    )(*args)


def _sgu_fwd(z, gain, ws_b, bs_t, *, name):
    T = z.shape[0]
    SW = gain.shape[1]
    G = SW // BLOCK

    def body(zu_ref, zv_ref, gain_ref, ws_ref, bs_ref, a_ref):
        u = _gelu(zu_ref[...].astype(F32))
        vg = _gelu(zv_ref[...].astype(F32))
        r = lax.rsqrt(jnp.mean(vg * vg, axis=-1, keepdims=True) + EPS)
        vn = ((vg * r) * gain_ref[...]).astype(BF16)
        for g in range(G):
            sl = slice(g * BLOCK, (g + 1) * BLOCK)
            mixed = jnp.dot(ws_ref[g], vn[:, sl], preferred_element_type=F32) + bs_ref[:, g:g + 1]
            a_ref[:, sl] = (u[:, sl] * mixed).astype(BF16)

    return pl.pallas_call(
        body, name=name, grid=(T // BLOCK,),
        in_specs=[pl.BlockSpec((BLOCK, SW), lambda c: (c, 0)), pl.BlockSpec((BLOCK, SW), lambda c: (c, 1)),
                  pl.BlockSpec((1, SW), lambda c: (0, 0)), pl.BlockSpec((G, BLOCK, BLOCK), lambda c: (0, 0, 0)),
                  pl.BlockSpec((BLOCK, G), lambda c: (0, 0))],
        out_specs=pl.BlockSpec((BLOCK, SW), lambda c: (c, 0)),
        out_shape=jax.ShapeDtypeStruct((T, SW), BF16),
        compiler_params=_cparams("parallel"),
    )(z, z, gain, ws_b, bs_t)


def _sgu_bwd(z, gain, ws_b, bs_t, da, dz, *, name):
    T = z.shape[0]
    SW = gain.shape[1]
    G = SW // BLOCK

    def body(zu_ref, zv_ref, gain_ref, ws_ref, bs_ref, da_ref, dz_in_ref, dz_ref, dws_ref, dbs_ref, dgain_ref, dvn_ref):
        first = pl.program_id(0) == 0

        @pl.when(first)
        def _():
            dws_ref[...] = jnp.zeros_like(dws_ref)
            dbs_ref[...] = jnp.zeros_like(dbs_ref)
            dgain_ref[...] = jnp.zeros_like(dgain_ref)

        u, du = _gelu_and_grad(zu_ref[...].astype(F32))
        vg, dvg = _gelu_and_grad(zv_ref[...].astype(F32))
        r = lax.rsqrt(jnp.mean(vg * vg, axis=-1, keepdims=True) + EPS)
        xhat = vg * r
        gain_ = gain_ref[...]
        vn = (xhat * gain_).astype(BF16)
        da_ = da_ref[...]
        for g in range(G):
            sl = slice(g * BLOCK, (g + 1) * BLOCK)
            w = ws_ref[g]
            mixed = jnp.dot(w, vn[:, sl], preferred_element_type=F32) + bs_ref[:, g:g + 1]
            dmix = da_[:, sl] * u[:, sl]
            dz_ref[:, sl] = (da_[:, sl] * mixed * du[:, sl]).astype(BF16)
            dmb = dmix.astype(BF16)
            dws_ref[g] += lax.dot_general(dmb, vn[:, sl], (((1,), (1,)), ((), ())), preferred_element_type=F32)
            dbs_ref[:, g:g + 1] += jnp.sum(dmix, axis=-1, keepdims=True)
            dvn_ref[:, sl] = lax.dot_general(w, dmb, (((0,), (0,)), ((), ())), preferred_element_type=F32)
        dvn = dvn_ref[...]
        dgain_ref[...] += jnp.sum(dvn * xhat, axis=0, keepdims=True)
        dy = dvn * gain_
        dv_ = r * (dy - xhat * jnp.mean(dy * xhat, axis=-1, keepdims=True))
        dz_ref[:, SW:] = (dv_ * dvg).astype(BF16)

    row = pl.BlockSpec((BLOCK, SW), lambda c: (c, 0))
    return pl.pallas_call(
        body, name=name, grid=(T // BLOCK,),
        in_specs=[row, pl.BlockSpec((BLOCK, SW), lambda c: (c, 1)),
                  pl.BlockSpec((1, SW), lambda c: (0, 0)), pl.BlockSpec((G, BLOCK, BLOCK), lambda c: (0, 0, 0)),
                  pl.BlockSpec((BLOCK, G), lambda c: (0, 0)), row, _ANY],
        out_specs=[pl.BlockSpec((BLOCK, 2 * SW), lambda c: (c, 0)), pl.BlockSpec((G, BLOCK, BLOCK), lambda c: (0, 0, 0)),
                   pl.BlockSpec((BLOCK, G), lambda c: (0, 0)), pl.BlockSpec((1, SW), lambda c: (0, 0))],
        out_shape=[jax.ShapeDtypeStruct(dz.shape, dz.dtype),
                   jax.ShapeDtypeStruct((G, BLOCK, BLOCK), F32), jax.ShapeDtypeStruct((BLOCK, G), F32),
                   jax.ShapeDtypeStruct((1, SW), F32)],
        input_output_aliases={6: 0},
        scratch_shapes=[pltpu.VMEM((BLOCK, SW), F32)],
        compiler_params=_cparams("arbitrary"),
    )(z, z, gain, ws_b, bs_t, da, dz)


def _bias_table(rel_bias, bmap, *, name):
    H = rel_bias.shape[1]

    def body(rb_ref, bmap_ref, o_ref):
        bm_ = bmap_ref[...]
        for h in range(H):
            acc = jnp.zeros(bm_.shape, F32)
            for b in range(REL_BUCKETS):
                acc = jnp.where(bm_ == b, rb_ref[b, h], acc)
            o_ref[h] = acc

    return pl.pallas_call(
        body, name=name,
        in_specs=[pl.BlockSpec(memory_space=pltpu.SMEM), pl.BlockSpec(memory_space=pltpu.VMEM)],
        out_specs=pl.BlockSpec(memory_space=pltpu.VMEM),
        out_shape=jax.ShapeDtypeStruct((H, BLOCK, 3 * BLOCK), F32),
    )(rel_bias, bmap)


def _attn_probs(q_ref, kb, bias_ref, sink_ref, s_ref, n, T, group):
    H = s_ref.shape[0]
    for h in range(H):
        kv = h // group
        qh = q_ref[:, h * HEAD_DIM:(h + 1) * HEAD_DIM].astype(BF16)
        s_ref[h] = lax.dot_general(qh, kb[:, kv * HEAD_DIM:(kv + 1) * HEAD_DIM], (((1,), (1,)), ((), ())),
                                   preferred_element_type=F32)
    row = lax.broadcasted_iota(jnp.int32, (BLOCK, 3 * BLOCK), 0)
    col = lax.broadcasted_iota(jnp.int32, (BLOCK, 3 * BLOCK), 1)
    key_pos = n * BLOCK + col - BLOCK
    valid = (jnp.abs(col - BLOCK - row) <= BLOCK) & (key_pos >= 0) & (key_pos < T)
    s = s_ref[...] * (HEAD_DIM ** -0.5) + bias_ref[...]
    s = jnp.where(valid[None], s, NEG)
    sink = sink_ref[...]
    m = jnp.maximum(jnp.max(s, axis=-1, keepdims=True), sink)
    e = jnp.exp(s - m)
    es = jnp.exp(sink - m)
    inv = 1.0 / (jnp.sum(e, axis=-1, keepdims=True) + es)
    return e * inv, es * inv


def _attn_fwd(z, kpad, vpad, bias, sink, *, name):
    T = z.shape[0]
    H = bias.shape[0]
    AW = H * HEAD_DIM
    group = H // N_KV_HEADS

    def body(q_ref, k_ref, v_ref, bias_ref, sink_ref, o_ref, s_ref, p_ref):
        n = pl.program_id(0)
        start = pl.multiple_of(n * BLOCK, BLOCK)
        kb = k_ref[pl.ds(start, 3 * BLOCK), :]
        vb = v_ref[pl.ds(start, 3 * BLOCK), :]
        p, _ = _attn_probs(q_ref, kb, bias_ref, sink_ref, s_ref, n, T, group)
        p_ref[...] = p.astype(BF16)
        for h in range(H):
            kv = h // group
            o = jnp.dot(p_ref[h], vb[:, kv * HEAD_DIM:(kv + 1) * HEAD_DIM], preferred_element_type=F32)
            o_ref[:, h * HEAD_DIM:(h + 1) * HEAD_DIM] = o.astype(BF16)

    full_kv = pl.BlockSpec((T + 2 * BLOCK, KV_WIDTH), lambda n: (0, 0))
    return pl.pallas_call(
        body, name=name, grid=(T // BLOCK,),
        in_specs=[pl.BlockSpec((BLOCK, AW), lambda n: (n, 2)), full_kv, full_kv,
                  pl.BlockSpec((H, BLOCK, 3 * BLOCK), lambda n: (0, 0, 0)), pl.BlockSpec((H, 1, 1), lambda n: (0, 0, 0))],
        out_specs=pl.BlockSpec((BLOCK, AW), lambda n: (n, 0)),
        out_shape=jax.ShapeDtypeStruct((T, AW), BF16),
        scratch_shapes=[pltpu.VMEM((H, BLOCK, 3 * BLOCK), F32), pltpu.VMEM((H, BLOCK, 3 * BLOCK), BF16)],
        compiler_params=_cparams("parallel"),
    )(z, kpad, vpad, bias, sink)


def _attn_bwd(z, kpad, vpad, bias, sink, do, dz, *, name):
    T = z.shape[0]
    H = bias.shape[0]
    AW = H * HEAD_DIM
    group = H // N_KV_HEADS
    scale = HEAD_DIM ** -0.5

    def body(q_ref, k_ref, v_ref, bias_ref, sink_ref, do_ref, dz_in_ref, dq_ref, dk_ref, dv_ref, dbias_ref, dsink_ref,
             s_ref, dp_ref, p_ref, ds_ref):
        n = pl.program_id(0)

        @pl.when(n == 0)
        def _():
            dk_ref[...] = jnp.zeros_like(dk_ref)
            dv_ref[...] = jnp.zeros_like(dv_ref)
            dbias_ref[...] = jnp.zeros_like(dbias_ref)
            dsink_ref[...] = jnp.zeros_like(dsink_ref)

        start = pl.multiple_of(n * BLOCK, BLOCK)
        kb = k_ref[pl.ds(start, 3 * BLOCK), :]
        vb = v_ref[pl.ds(start, 3 * BLOCK), :]
        p, p_sink = _attn_probs(q_ref, kb, bias_ref, sink_ref, s_ref, n, T, group)
        s_ref[...] = p
        p_ref[...] = p.astype(BF16)
        for h in range(H):
            kv = h // group
            dp_ref[h] = lax.dot_general(do_ref[:, h * HEAD_DIM:(h + 1) * HEAD_DIM], vb[:, kv * HEAD_DIM:(kv + 1) * HEAD_DIM],
                                        (((1,), (1,)), ((), ())), preferred_element_type=F32)
        p = s_ref[...]
        dp = dp_ref[...]
        delta = jnp.sum(p * dp, axis=-1, keepdims=True)
        ds = p * (dp - delta)
        dbias_ref[...] += ds
        dsink_ref[...] += -(p_sink * delta)
        ds_ref[...] = ds.astype(BF16)
        for kv in range(N_KV_HEADS):
            ksl = slice(kv * HEAD_DIM, (kv + 1) * HEAD_DIM)
            dk_acc = jnp.zeros((3 * BLOCK, HEAD_DIM), F32)
            dv_acc = jnp.zeros((3 * BLOCK, HEAD_DIM), F32)
            for gi in range(group):
                h = kv * group + gi
                hsl = slice(h * HEAD_DIM, (h + 1) * HEAD_DIM)
                dsb = ds_ref[h]
                dq = jnp.dot(dsb, kb[:, ksl], preferred_element_type=F32) * scale
                dq_ref[:, hsl] = dq.astype(BF16)
                dk_acc = dk_acc + lax.dot_general(dsb, q_ref[:, hsl].astype(BF16), (((0,), (0,)), ((), ())),
                                                  preferred_element_type=F32)
                dv_acc = dv_acc + lax.dot_general(p_ref[h], do_ref[:, hsl], (((0,), (0,)), ((), ())),
                                                  preferred_element_type=F32)
            dk_ref[pl.ds(start, 3 * BLOCK), ksl] += dk_acc * scale
            dv_ref[pl.ds(start, 3 * BLOCK), ksl] += dv_acc

    full_kv = pl.BlockSpec((T + 2 * BLOCK, KV_WIDTH), lambda n: (0, 0))
    bias_spec = pl.BlockSpec((H, BLOCK, 3 * BLOCK), lambda n: (0, 0, 0))
    row = pl.BlockSpec((BLOCK, AW), lambda n: (n, 0))
    q_cols = pl.BlockSpec((BLOCK, AW), lambda n: (n, 2))
    band = (H, BLOCK, 3 * BLOCK)
    return pl.pallas_call(
        body, name=name, grid=(T // BLOCK,),
        in_specs=[q_cols, full_kv, full_kv, bias_spec, pl.BlockSpec((H, 1, 1), lambda n: (0, 0, 0)), row, _ANY],
        out_specs=[q_cols, full_kv, full_kv, bias_spec, pl.BlockSpec((H, BLOCK, 1), lambda n: (0, 0, 0))],
        out_shape=[jax.ShapeDtypeStruct(dz.shape, dz.dtype),
                   jax.ShapeDtypeStruct((T + 2 * BLOCK, KV_WIDTH), F32), jax.ShapeDtypeStruct((T + 2 * BLOCK, KV_WIDTH), F32),
                   jax.ShapeDtypeStruct(band, F32), jax.ShapeDtypeStruct((H, BLOCK, 1), F32)],
        input_output_aliases={6: 0},
        scratch_shapes=[pltpu.VMEM(band, F32), pltpu.VMEM(band, F32), pltpu.VMEM(band, BF16), pltpu.VMEM(band, BF16)],
        compiler_params=_cparams("arbitrary"),
    )(z, kpad, vpad, bias, sink, do, dz)


def _dkv_into(dkp, dvp, dz, *, name):
    T = dz.shape[0]
    D = (dz.shape[1] - 2 * KV_WIDTH) * 2 // 7
    col = (D + D // 2) // (2 * KV_WIDTH)
    assert col * 2 * KV_WIDTH == D + D // 2

    def body(dk_ref, dv_ref, dz_in_ref, o_ref):
        o_ref[:, :KV_WIDTH] = dk_ref[...].astype(BF16)
        o_ref[:, KV_WIDTH:] = dv_ref[...].astype(BF16)

    kv = pl.BlockSpec((BLOCK, KV_WIDTH), lambda n: (n + 1, 0))
    return pl.pallas_call(
        body, name=name, grid=(T // BLOCK,),
        in_specs=[kv, kv, _ANY], out_specs=pl.BlockSpec((BLOCK, 2 * KV_WIDTH), lambda n: (n, col)),
        out_shape=jax.ShapeDtypeStruct(dz.shape, dz.dtype), input_output_aliases={2: 0},
        compiler_params=_cparams("parallel"),
    )(dkp, dvp, dz)


def _kv_pad(z, *, name):
    T = z.shape[0]
    D = (z.shape[1] - 2 * KV_WIDTH) * 2 // 7
    kcol = (D + D // 2) // KV_WIDTH
    nb = T // BLOCK

    def body(k_ref, v_ref, ko_ref, vo_ref):
        b = pl.program_id(0)
        inside = (b >= 1) & (b <= nb)
        ko_ref[...] = jnp.where(inside, k_ref[...].astype(F32), 0.0).astype(BF16)
        vo_ref[...] = jnp.where(inside, v_ref[...].astype(F32), 0.0).astype(BF16)

    out = jax.ShapeDtypeStruct((T + 2 * BLOCK, KV_WIDTH), BF16)
    o_spec = pl.BlockSpec((BLOCK, KV_WIDTH), lambda b: (b, 0))
    return pl.pallas_call(
        body, name=name, grid=(nb + 2,),
        in_specs=[pl.BlockSpec((BLOCK, KV_WIDTH), lambda b: (jnp.clip(b - 1, 0, nb - 1), kcol)),
                  pl.BlockSpec((BLOCK, KV_WIDTH), lambda b: (jnp.clip(b - 1, 0, nb - 1), kcol + 1))],
        out_specs=[o_spec, o_spec], out_shape=[out, out],
        compiler_params=_cparams("parallel"),
    )(z, z)


def _attn_small_grads(dbias, dsink_rows, bmap, after, *, name):
    H = dbias.shape[0]

    def body(dbias_ref, dsink_ref, bmap_ref, drel_ref, ds_ref):
        bm_ = bmap_ref[...]
        for h in range(H):
            d = dbias_ref[h]
            for b in range(REL_BUCKETS):
                drel_ref[b, h] = jnp.sum(jnp.where(bm_ == b, d, 0.0))
            ds_ref[0, h] = jnp.sum(dsink_ref[h])

    vmem = pl.BlockSpec(memory_space=pltpu.VMEM)
    smem = pl.BlockSpec(memory_space=pltpu.SMEM)
    body, in_specs, args = _ordered_after(body, 3, [vmem, vmem, vmem], (dbias, dsink_rows, bmap), after)
    return pl.pallas_call(
        body, name=name, in_specs=in_specs, out_specs=[smem, smem],
        out_shape=[jax.ShapeDtypeStruct((REL_BUCKETS, H), F32), jax.ShapeDtypeStruct((1, H), F32)],
    )(*args)


def _local_step(x, target, weight, emit, flush, share, norm_mix, v_gain, w_s, b_s, sink, rel_bias, norm_ffn, norm_final,
                early=()):
    T, D = x.shape
    ws_b = w_s.astype(BF16)
    bs_t = b_s.T
    bmap = jnp.asarray(_bucket_map())
    sink = sink.reshape(-1, 1, 1)

    bias = _bias_table(rel_bias, bmap, name="bias_table")
    h = _rms_fwd(x, norm_mix, name="rms_mix", after=[bias, *early])
    w_in = weight("w_in", h)
    z = _mm(h, w_in, tb=True, out_dtype=BF16, name="mm_z", bm=2048, bn=768)
    a = _sgu_fwd(z, v_gain, ws_b, bs_t, name="sgu_fwd")
    w_a = weight("w_a_out", a)
    ya = _mm_w8(a, w_a, name="mm_ya", bm=2048, out_dtype=BF16)
    kpad, vpad = _kv_pad(z, name="kv_pad")
    o = _attn_fwd(z, kpad, vpad, bias, sink, name="attn_fwd")
    w_b = weight("w_b_out", o)
    yb = _mm_w8(o, w_b, name="mm_yb", bm=2048, out_dtype=BF16)
    m = _merge_fwd(z, ya, yb, name="merge_fwd")
    w_o = weight("w_o", m)
    x1, h2 = _mm_resid_rms(m, w_o, x, norm_ffn, name="mm_x1_rms")
    w_gate = weight("w_gate", h2)
    gate = _mm(h2, w_gate, tb=True, name="mm_gate", bm=2048, bn=512)
    w_up = weight("w_up", gate)
    gate, up, act = _swiglu_mm_fwd(h2, w_up, gate, name="mm_up_swiglu")
    w_down = weight("w_down", act)
    x2 = _mm(act, w_down, name="mm_x2", add=x1, bm=1024, bn=512)
    loss, dx2b, g_norm_final = _loss_head(x2, norm_final, target, name="loss_head")

    g_w_down = _mm(act, dx2b, ta=True, out_dtype=BF16, name="mm_gwdown", bm=512, bn=2048)
    tok = emit(("w_down",), (g_w_down,))
    dgate, dup = _swiglu_mm_bwd(dx2b, w_down, gate, up, name="mm_dact_swiglu", after=tok)
    tok = flush(dgate)
    g_w_gate = _mm(dgate, h2, ta=True, out_dtype=BF16, name="mm_gwgate", bm=512, bn=2048, after=tok)
    g_w_up = _mm(dup, h2, ta=True, out_dtype=BF16, name="mm_gwup", bm=512, bn=2048)
    tok = emit(("w_gate", "w_up"), (g_w_gate, g_w_up))
    dh2 = _mm_sum2(dgate, w_gate, dup, w_up, name="mm_dh2", after=tok)
    tok = flush(dh2)
    dx1, dx1b, g_norm_ffn = _rms_bwd(x1, norm_ffn, dh2, dx2b, name="rms_ffn_bwd", want_bf16=True, after=tok)

    g_w_o = _mm(m, dx1b, ta=True, out_dtype=BF16, name="mm_gwo", bm=2048, bn=512)
    tok = emit(("w_o",), (g_w_o,))
    dm = _mm(dx1b, w_o, tb=True, name="mm_dm", bm=2048, bn=512, after=tok)
    tok = flush(dm)
    dy, dz = _merge_bwd(z, ya, yb, dm, name="merge_bwd", after=tok)
    g_w_a = _mm_gw8(a, dy, w_a.shape[2], name="mm_gwa", lead=0)
    g_w_b = _mm_gw8(o, dy, w_b.shape[2], name="mm_gwb", lead=1)
    tok = emit(("w_a_out", "w_b_out"), (g_w_a, g_w_b))
    da = _mm_w8t(dy, w_a, name="mm_da", bm=2048, bn=512, after=tok, lead=0)
    tok = flush(da)
    do = _mm_w8t(dy, w_b, out_dtype=BF16, name="mm_do", bm=2048, bn=512, after=tok, lead=1)
    dz, g_w_s, g_b_s_t, g_v_gain = _sgu_bwd(z, v_gain, ws_b, bs_t, da, dz, name="sgu_bwd")
    dz, dkp, dvp, dbias, dsink_rows = _attn_bwd(z, kpad, vpad, bias, sink, do, dz, name="attn_bwd")
    dz = _dkv_into(dkp, dvp, dz, name="dkv_into_dz")
    g_rel_bias, g_sink = _attn_small_grads(dbias, dsink_rows, bmap, None, name="attn_small_grads")
    tok = share(dict(sgu_v_gain=g_v_gain, sgu_w_s=g_w_s, sgu_b_s=g_b_s_t.T, attn_sink=g_sink, rel_bias=g_rel_bias,
                     norm_ffn=g_norm_ffn, norm_final=g_norm_final, loss=loss[0, 0]))
    g_w_in = _mm(dz, h, ta=True, out_dtype=BF16, name="mm_gwin", bm=768, bn=2048, after=tok)
    tok = emit(("w_in",), (g_w_in,))
    half = dict(bm=T // 2, bn=256)
    dh = _mm(dz, w_in, name="mm_dh_top", row_blocks=(0, 1), after=tok, **half)
    tok = flush(dh)
    dh = _mm(dz, w_in, name="mm_dh_bottom", row_blocks=(1, 1), into=dh, after=tok, **half)
    grad_x, g_norm_mix = _rms_bwd(x, norm_mix, dh, dx1, name="rms_mix_bwd", want_bf16=False)
    return grad_x, g_norm_mix


def _position():
    return lax.axis_index("x"), lax.axis_index("y"), lax.axis_index("c")


def _other_chips(x, y):
    return [(1 - x, y), (x, 1 - y), (1 - x, 1 - y)]


def _slot(px, py, pc):
    return 4 * px + 2 * py + pc


_HBM = pl.BlockSpec(memory_space=pltpu.HBM)
_SEM = pl.BlockSpec(memory_space=pltpu.SEMAPHORE)
_DATAFLOW = pltpu.SideEffectType.DATAFLOW_SIDE_EFFECTING


def _in_hbm(a):
    return pltpu.with_memory_space_constraint(a, pltpu.HBM)


def _own_slot(shard, pos, *, name, after=None):
    R, C = shard.shape
    tr = _div(R, 256, 16)

    def body(pos_ref, w_ref, o_ref):
        o_ref[...] = w_ref[...].astype(BF16)

    body, in_specs, args = _ordered_after(body, 2, [pl.BlockSpec((tr, C), lambda i, pos_ref: (i, 0))], (pos, shard), after)
    grid_spec = pltpu.PrefetchScalarGridSpec(
        num_scalar_prefetch=1, grid=(R // tr,), in_specs=in_specs,
        out_specs=pl.BlockSpec((None, tr, C), lambda i, pos_ref: (pos_ref[0], i, 0)))
    return pl.pallas_call(
        body, name=name, grid_spec=grid_spec,
        out_shape=jax.ShapeDtypeStruct((N_DEV, R, C), BF16),
        compiler_params=_cparams("parallel"),
    )(*args)


def _ag_copies(w, land_ref, send_sems, recv_sems):
    x, y, c = _position()
    mine = land_ref.at[_slot(x, y, c)]
    targets = [(px, py, c) for px, py in _other_chips(x, y)] + [(x, y, 1 - c)]
    return [pltpu.make_async_remote_copy(src_ref=mine, dst_ref=mine, send_sem=send_sems.at[4 * w + k],
                                         recv_sem=recv_sems.at[4 * w + k], device_id=to, device_id_type=MESH)
            for k, to in enumerate(targets)]


def _ag_start(buffers, groups, *, name, after=None):
    lands = [buffers[i] for g in groups for i in g]
    n, ng = len(lands), len(groups)
    sizes = [len(g) for g in groups]

    def body(*refs):
        land_refs = refs[:n]
        sems = refs[n:n + 2 * ng]
        token = refs[-1]
        i = 0
        for g in range(ng):
            for w in range(sizes[g]):
                for cp in _ag_copies(w, land_refs[i], sems[2 * g], sems[2 * g + 1]):
                    cp.start()
                i += 1
        token[...] = jnp.zeros_like(token)

    sem_shapes = [pltpu.SemaphoreType.DMA((4 * k,)) for k in sizes for _ in range(2)]
    body, in_specs, args = _ordered_after(body, n, [_HBM] * n, tuple(_in_hbm(a) for a in lands), after)
    outs = pl.pallas_call(
        body, name=name,
        in_specs=in_specs,
        out_specs=tuple([_SEM] * (2 * ng) + [_HBM] * n + [pl.BlockSpec(memory_space=pltpu.VMEM)]),
        out_shape=tuple(sem_shapes + [pltpu.HBM(a.shape, a.dtype) for a in lands] + [jax.ShapeDtypeStruct((8, LANES), F32)]),
        input_output_aliases={i: 2 * ng + i for i in range(n)},
        compiler_params=pltpu.CompilerParams(has_side_effects=_DATAFLOW),
    )(*args)
    sems, thru = outs[:2 * ng], outs[2 * ng:2 * ng + n]
    result, i = [], 0
    for g in range(ng):
        k = sizes[g]
        result.append((sems[2 * g], sems[2 * g + 1], list(thru[i:i + k])))
        i += k
    return result, outs[-1]


def _ag_wait(send_sems, recv_sems, lands, after, *, name):
    n = len(lands)

    def body(*refs):
        land_refs = refs[:n]
        send_ref, recv_ref = refs[n], refs[n + 1]
        token = refs[-1]
        for w in range(n):
            for cp in _ag_copies(w, land_refs[w], send_ref, recv_ref):
                cp.wait_send()
                cp.wait_recv()
        token[...] = jnp.zeros_like(token)

    outs = pl.pallas_call(
        body, name=name,
        in_specs=[_HBM] * n + [_SEM, _SEM, _ANY],
        out_specs=tuple([_HBM] * n + [pl.BlockSpec(memory_space=pltpu.VMEM)]),
        out_shape=tuple([pltpu.HBM(a.shape, a.dtype) for a in lands] + [jax.ShapeDtypeStruct((8, LANES), F32)]),
        input_output_aliases={i: i for i in range(n)},
        compiler_params=pltpu.CompilerParams(has_side_effects=_DATAFLOW),
    )(*lands, send_sems, recv_sems, after)
    return list(outs[:n]), outs[n]


def _ag_forward(lands, *, name, after=None):
    n = len(lands)

    def body(*refs):
        in_refs, out_refs = refs[:n], refs[n:2 * n]
        send_sems, recv_sems = refs[2 * n:]
        x, y, c = _position()
        copies = []
        for w in range(n):
            for k, (px, py) in enumerate(_other_chips(x, y)):
                cp = pltpu.make_async_remote_copy(
                    src_ref=in_refs[w].at[_slot(px, py, c)], dst_ref=out_refs[w].at[_slot(px, py, c)],
                    send_sem=send_sems.at[3 * w + k], recv_sem=recv_sems.at[3 * w + k],
                    device_id=(x, y, 1 - c), device_id_type=MESH)
                cp.start()
                copies.append(cp)
        for cp in copies:
            cp.wait()

    body, in_specs, args = _ordered_after(body, n, [_ANY] * n, tuple(lands), after)
    return pl.pallas_call(
        body, name=name,
        in_specs=in_specs, out_specs=[_ANY] * n,
        out_shape=[jax.ShapeDtypeStruct(a.shape, a.dtype) for a in lands],
        input_output_aliases={i: i for i in range(n)},
        scratch_shapes=[pltpu.SemaphoreType.DMA((3 * n,)), pltpu.SemaphoreType.DMA((3 * n,))],
    )(*args)


def _sibling_copies(w, g8_ref, land_ref, send_sems, recv_sems):
    x, y, c = _position()
    return [pltpu.make_async_remote_copy(src_ref=g8_ref.at[2 * p + (1 - c)], dst_ref=land_ref.at[p],
                                         send_sem=send_sems.at[4 * w + p], recv_sem=recv_sems.at[4 * w + p],
                                         device_id=(x, y, 1 - c), device_id_type=MESH)
            for p in range(4)]


def _chip_copies(w, sums_ref, land_ref, send_sems, recv_sems):
    x, y, c = _position()
    return [pltpu.make_async_remote_copy(src_ref=sums_ref.at[2 * px + py], dst_ref=land_ref.at[k],
                                         send_sem=send_sems.at[3 * w + k], recv_sem=recv_sems.at[3 * w + k],
                                         device_id=(px, py, c), device_id_type=MESH)
            for k, (px, py) in enumerate(_other_chips(x, y))]


def _copies_start(copies, per_weight, srcs, *, name):
    n = len(srcs)
    lands = [lax.empty((per_weight,) + s.shape[1:], s.dtype) for s in srcs]

    def body(*refs):
        src_refs, land_refs = refs[:n], refs[n:2 * n]
        send_sems, recv_sems = refs[2 * n], refs[2 * n + 1]
        token = refs[-1]
        for w in range(n):
            for cp in copies(w, src_refs[w], land_refs[w], send_sems, recv_sems):
                cp.start()
        token[...] = jnp.zeros_like(token)

    outs = pl.pallas_call(
        body, name=name,
        in_specs=[_HBM] * (2 * n),
        out_specs=tuple([_SEM, _SEM] + [_HBM] * (2 * n) + [pl.BlockSpec(memory_space=pltpu.VMEM)]),
        out_shape=tuple([pltpu.SemaphoreType.DMA((per_weight * n,)), pltpu.SemaphoreType.DMA((per_weight * n,))]
                        + [pltpu.HBM(a.shape, a.dtype) for a in srcs + lands] + [jax.ShapeDtypeStruct((8, LANES), F32)]),
        input_output_aliases={i: 2 + i for i in range(2 * n)},
        compiler_params=pltpu.CompilerParams(has_side_effects=_DATAFLOW),
    )(*[_in_hbm(a) for a in srcs + lands])
    return outs[0], outs[1], list(outs[2:2 + n]), list(outs[2 + n:2 + 2 * n]), outs[-1]


def _copies_wait(copies, send_sems, recv_sems, srcs, lands, after, *, name):
    n = len(srcs)

    def body(*refs):
        src_refs, land_refs = refs[:n], refs[n:2 * n]
        send_ref, recv_ref = refs[2 * n], refs[2 * n + 1]
        for w in range(n):
            for cp in copies(w, src_refs[w], land_refs[w], send_ref, recv_ref):
                cp.wait_send()
                cp.wait_recv()

    outs = pl.pallas_call(
        body, name=name,
        in_specs=[_HBM] * (2 * n) + [_SEM, _SEM, _ANY],
        out_specs=tuple([_HBM] * (2 * n)),
        out_shape=tuple(pltpu.HBM(a.shape, a.dtype) for a in srcs + lands),
        input_output_aliases={i: i for i in range(2 * n)},
        compiler_params=pltpu.CompilerParams(has_side_effects=_DATAFLOW),
    )(*srcs, *lands, send_sems, recv_sems, after)
    return list(outs[:n]), list(outs[n:])


def _chip_sums(g8, from_sibling, pos, *, name):
    _, R, C = g8.shape
    tr = _div(R, 512, 16)

    def body(pos_ref, g_ref, s_ref, o_ref):
        o_ref[...] = (g_ref[...].astype(F32) + s_ref[...].astype(F32)).astype(BF16)

    def chip(k, pos_ref):
        return jnp.where(k >= pos_ref[1], k + 1, k)

    grid_spec = pltpu.PrefetchScalarGridSpec(
        num_scalar_prefetch=1, grid=(3, R // tr),
        in_specs=[pl.BlockSpec((None, tr, C), lambda k, i, pos_ref: (2 * chip(k, pos_ref) + pos_ref[2], i, 0)),
                  pl.BlockSpec((None, tr, C), lambda k, i, pos_ref: (chip(k, pos_ref), i, 0))],
        out_specs=pl.BlockSpec((None, tr, C), lambda k, i, pos_ref: (chip(k, pos_ref), i, 0)))
    return pl.pallas_call(
        body, name=name, grid_spec=grid_spec,
        out_shape=jax.ShapeDtypeStruct((4, R, C), BF16),
        compiler_params=_cparams("parallel", "parallel"),
    )(pos, g8, from_sibling)


def _peer_copies(land_ref, send_sems, recv_sems):
    x, y, c = _position()
    mine = land_ref.at[_slot(x, y, c)]
    copies = []
    for k in range(1, N_DEV):
        to = (1 - x if k & 4 else x, 1 - y if k & 2 else y, 1 - c if k & 1 else c)
        copies.append(pltpu.make_async_remote_copy(src_ref=mine, dst_ref=mine, send_sem=send_sems.at[k - 1],
                                                   recv_sem=recv_sems.at[k - 1], device_id=to, device_id_type=MESH))
    return copies


def _small_exchange_start(buf, *, name):
    def body(buf_ref, send_sems, recv_sems, thru_ref, token):
        for cp in _peer_copies(buf_ref, send_sems, recv_sems):
            cp.start()
        token[...] = jnp.zeros_like(token)

    return pl.pallas_call(
        body, name=name, in_specs=[_HBM],
        out_specs=(_SEM, _SEM, _HBM, pl.BlockSpec(memory_space=pltpu.VMEM)),
        out_shape=(pltpu.SemaphoreType.DMA((N_DEV - 1,)), pltpu.SemaphoreType.DMA((N_DEV - 1,)),
                   pltpu.HBM(buf.shape, buf.dtype), jax.ShapeDtypeStruct((8, LANES), F32)),
        input_output_aliases={0: 2},
        compiler_params=pltpu.CompilerParams(has_side_effects=_DATAFLOW),
    )(_in_hbm(buf))


def _small_exchange_wait(send_sems, recv_sems, buf, after, *, name):
    def body(buf_ref, send_ref, recv_ref, after_ref, out_ref):
        for cp in _peer_copies(buf_ref, send_ref, recv_ref):
            cp.wait_send()
            cp.wait_recv()

    return pl.pallas_call(
        body, name=name, in_specs=[_HBM, _SEM, _SEM, _ANY], out_specs=_HBM,
        out_shape=pltpu.HBM(buf.shape, buf.dtype), input_output_aliases={0: 0},
        compiler_params=pltpu.CompilerParams(has_side_effects=_DATAFLOW),
    )(buf, send_sems, recv_sems, after)


def _sum_slots(buf, *, name):
    _, R, L = buf.shape

    def body(buf_ref, sum_ref):
        acc = buf_ref[0]
        for d in range(1, N_DEV):
            acc = acc + buf_ref[d]
        sum_ref[...] = acc

    vmem = pl.BlockSpec(memory_space=pltpu.VMEM)
    return pl.pallas_call(body, name=name, in_specs=[vmem], out_specs=vmem,
                          out_shape=jax.ShapeDtypeStruct((R, L), F32),
                          compiler_params=pltpu.CompilerParams(vmem_limit_bytes=VMEM_LIMIT))(buf)


def _adamw_math(w, g, m, v):
    m = ADAM_B1 * m + (1.0 - ADAM_B1) * g
    v = ADAM_B2 * v + (1.0 - ADAM_B2) * (g * g)
    m_hat = m / (1.0 - ADAM_B1 ** ADAM_STEP)
    v_hat = v / (1.0 - ADAM_B2 ** ADAM_STEP)
    delta = -ADAM_LR * (m_hat / (jnp.sqrt(v_hat) + ADAM_EPS) + ADAM_WD * w)
    return delta, m, v


def _adamw_shard(w, m, v, g8, from_sibling, from_chips, pos, *, name):
    R, C = w.shape
    tr = _div(R, 256, 16)

    def body(pos_ref, w_ref, m_ref, v_ref, g_ref, s_ref, r_ref, go_ref, d_ref, mo_ref, vo_ref):
        g = g_ref[...].astype(F32) + s_ref[...].astype(F32)
        for k in range(3):
            g = g + r_ref[k].astype(F32)
        delta, m_, v_ = _adamw_math(w_ref[...], g, m_ref[...], v_ref[...])
        go_ref[...] = g
        d_ref[...] = delta
        mo_ref[...] = m_
        vo_ref[...] = v_

    blk = pl.BlockSpec((tr, C), lambda i, pos_ref: (i, 0))
    grid_spec = pltpu.PrefetchScalarGridSpec(
        num_scalar_prefetch=1, grid=(R // tr,),
        in_specs=[blk, blk, blk,
                  pl.BlockSpec((None, tr, C), lambda i, pos_ref: (pos_ref[0], i, 0)),
                  pl.BlockSpec((None, tr, C), lambda i, pos_ref: (pos_ref[1], i, 0)),
                  pl.BlockSpec((3, tr, C), lambda i, pos_ref: (0, i, 0))],
        out_specs=[blk] * 4)
    out = jax.ShapeDtypeStruct((R, C), F32)
    return pl.pallas_call(
        body, name=name, grid_spec=grid_spec, out_shape=[out] * 4,
        compiler_params=_cparams("parallel"),
    )(pos, w, m, v, g8, from_sibling, from_chips)


def _adamw_small(w, g, m, v, *, name):
    R, L = w.shape

    def body(w_ref, g_ref, m_ref, v_ref, d_ref, mo_ref, vo_ref):
        delta, m_, v_ = _adamw_math(w_ref[...], g_ref[...], m_ref[...], v_ref[...])
        d_ref[...] = delta
        mo_ref[...] = m_
        vo_ref[...] = v_

    vmem = pl.BlockSpec(memory_space=pltpu.VMEM)
    out = jax.ShapeDtypeStruct((R, L), F32)
    return pl.pallas_call(body, name=name, in_specs=[vmem] * 4, out_specs=[vmem] * 3, out_shape=[out] * 3)(w, g, m, v)


_TILE = 8 * LANES


def _pack(pieces):
    rows = []
    for p in pieces:
        flat = p.reshape(-1).astype(F32)
        padded = -(-flat.shape[0] // _TILE) * _TILE
        rows.append(jnp.pad(flat, (0, padded - flat.shape[0])).reshape(-1, LANES))
    return jnp.concatenate(rows, axis=0)


def _unpack(packed, like):
    out, r = [], 0
    for p in like:
        size = int(np.prod(p.shape)) if p.shape else 1
        nrows = -(-size // _TILE) * 8
        out.append(packed[r:r + nrows].reshape(-1)[:size].reshape(p.shape))
        r += nrows
    return out


_BIG = ("w_in", "w_a_out", "w_b_out", "w_o", "w_gate", "w_up", "w_down")
_TRANSPOSED = ("w_in", "w_gate", "w_up")
_COL_SHARDED = ("w_a_out", "w_b_out")
_GATHER_GROUPS = (("w_in",), ("w_a_out", "w_b_out", "w_o"), ("w_gate",), ("w_up",), ("w_down",))
_START_AFTER_WAIT = {0: (1, 2), 1: (3,), 2: (4,)}
_SMALL = ("norm_mix", "sgu_v_gain", "sgu_w_s", "sgu_b_s", "attn_sink", "rel_bias", "norm_ffn", "norm_final")
_ORDER = ("w_in", "norm_mix", "sgu_v_gain", "sgu_w_s", "sgu_b_s", "w_a_out", "attn_sink", "rel_bias", "w_b_out", "w_o",
          "norm_ffn", "w_gate", "w_up", "w_down", "norm_final")


def _shard(name, a):
    return jnp.swapaxes(a, 1, 2)[0] if name in _TRANSPOSED else a[0]


def _unshard(name, a):
    return jnp.swapaxes(a[None], 1, 2) if name in _TRANSPOSED else a[None]


def _whole(name, gathered):
    _, r, c = gathered.shape
    return gathered if name in _COL_SHARDED else gathered.reshape(N_DEV * r, c)


def _blocks(name, grad):
    if name in _COL_SHARDED:
        return grad
    r, c = grad.shape
    return grad.reshape(N_DEV, r // N_DEV, c)


def kernel(x, w_in, norm_mix, sgu_v_gain, sgu_w_s, sgu_b_s, w_a_out, attn_sink, rel_bias, w_b_out, w_o, norm_ffn, w_gate, w_up, w_down, norm_final, loss_target, m_w_in, m_norm_mix, m_sgu_v_gain, m_sgu_w_s, m_sgu_b_s, m_w_a_out, m_attn_sink, m_rel_bias, m_w_b_out, m_w_o, m_norm_ffn, m_w_gate, m_w_up, m_w_down, m_norm_final, v_w_in, v_norm_mix, v_sgu_v_gain, v_sgu_w_s, v_sgu_b_s, v_w_a_out, v_attn_sink, v_rel_bias, v_w_b_out, v_w_o, v_norm_ffn, v_w_gate, v_w_up, v_w_down, v_norm_final):
    w = dict(w_in=w_in, norm_mix=norm_mix, sgu_v_gain=sgu_v_gain, sgu_w_s=sgu_w_s, sgu_b_s=sgu_b_s, w_a_out=w_a_out,
             attn_sink=attn_sink, rel_bias=rel_bias, w_b_out=w_b_out, w_o=w_o, norm_ffn=norm_ffn, w_gate=w_gate,
             w_up=w_up, w_down=w_down, norm_final=norm_final)
    m = dict(w_in=m_w_in, norm_mix=m_norm_mix, sgu_v_gain=m_sgu_v_gain, sgu_w_s=m_sgu_w_s, sgu_b_s=m_sgu_b_s,
             w_a_out=m_w_a_out, attn_sink=m_attn_sink, rel_bias=m_rel_bias, w_b_out=m_w_b_out, w_o=m_w_o,
             norm_ffn=m_norm_ffn, w_gate=m_w_gate, w_up=m_w_up, w_down=m_w_down, norm_final=m_norm_final)
    v = dict(w_in=v_w_in, norm_mix=v_norm_mix, sgu_v_gain=v_sgu_v_gain, sgu_w_s=v_sgu_w_s, sgu_b_s=v_sgu_b_s,
             w_a_out=v_w_a_out, attn_sink=v_attn_sink, rel_bias=v_rel_bias, w_b_out=v_w_b_out, w_o=v_w_o,
             norm_ffn=v_norm_ffn, w_gate=v_w_gate, w_up=v_w_up, w_down=v_w_down, norm_final=v_norm_final)
    xc, yc, cc = _position()
    pos = jnp.stack([_slot(xc, yc, cc), 2 * xc + yc, cc]).astype(jnp.int32)

    in_flight, full, slots = {}, {}, {}

    def start_gather(groups, after):
        names = [n for gi in groups for n in _GATHER_GROUPS[gi]]
        flights, token = _ag_start([slots[n] for n in names], [[names.index(n) for n in _GATHER_GROUPS[gi]] for gi in groups],
                                   name="ag_start_%d" % groups[0], after=after)
        in_flight.update(zip(groups, flights))
        return token

    def weight(name, after):
        if name not in full:
            gi = next(i for i, grp in enumerate(_GATHER_GROUPS) if name in grp)
            send_sems, recv_sems, lands = in_flight[gi]
            lands, token = _ag_wait(send_sems, recv_sems, lands, after, name="ag_wait_%d" % gi)
            started = start_gather(_START_AFTER_WAIT[gi], token) if gi in _START_AFTER_WAIT else None
            gathered = _ag_forward(lands, name="ag_forward_%d" % gi, after=started)
            full.update({n: _whole(n, g) for n, g in zip(_GATHER_GROUPS[gi], gathered)})
        return full[name]

    for n in _GATHER_GROUPS[0]:
        slots[n] = _own_slot(_shard(n, w[n]), pos, name="own_slot_" + n)
    first_started = start_gather((0,), None)
    for grp in _GATHER_GROUPS[1:]:
        for n in grp:
            slots[n] = _own_slot(_shard(n, w[n]), pos, name="own_slot_" + n, after=first_started)

    to_sibling, reducing = [], {}

    def emit(names, grads):
        g8 = [_blocks(n, g) for n, g in zip(names, grads)]
        send_sems, recv_sems, g8, lands, token = _copies_start(_sibling_copies, 4, g8, name="rs_sibling_start_" + names[0])
        to_sibling.append((names, send_sems, recv_sems, g8, lands))
        return token

    def flush(after):
        names, send_sems, recv_sems, g8, lands = to_sibling.pop()
        g8, from_sibling = _copies_wait(_sibling_copies, send_sems, recv_sems, g8, lands, after,
                                        name="rs_sibling_wait_" + names[0])
        sums4 = [_chip_sums(g, s, pos, name="chip_sums_" + n) for n, g, s in zip(names, g8, from_sibling)]
        send_sems, recv_sems, sums4, lands, token = _copies_start(_chip_copies, 3, sums4, name="rs_chips_start_" + names[0])
        reducing[names] = (g8, from_sibling, send_sems, recv_sems, sums4, lands)
        return token

    small_like = [w[n] for n in _SMALL]
    shared = {}

    def share(small):
        pieces = [jnp.zeros_like(w[n]) if n == "norm_mix" else small[n] for n in _SMALL] + [small["loss"]]
        packed = _pack(pieces)
        mine = lax.dynamic_update_index_in_dim(jnp.zeros((N_DEV,) + packed.shape, F32), packed, pos[0], 0)
        shared["send"], shared["recv"], shared["buf"], token = _small_exchange_start(mine, name="small_exchange_start")
        return token

    grad_x, g_norm_mix = _local_step(
        x[0], loss_target[0], weight, emit, flush, share, norm_mix, sgu_v_gain, sgu_w_s[0], sgu_b_s[0], attn_sink,
        rel_bias, norm_ffn, norm_final[None], early=[slots[n] for grp in _GATHER_GROUPS[1:] for n in grp])

    out_g, out_d, out_m, out_v = {}, {}, {}, {}
    small_w = _pack(small_like)
    late = _pack([g_norm_mix])
    late_mine = lax.dynamic_update_index_in_dim(jnp.zeros((N_DEV,) + late.shape, F32), late, pos[0], 0)
    late_send, late_recv, late_buf, after = _small_exchange_start(late_mine, name="norm_mix_exchange_start")
    for gi, (names, (g8, from_sibling, send_sems, recv_sems, sums4, lands)) in enumerate(reducing.items()):
        if gi == len(reducing) - 1:
            everyone = _small_exchange_wait(shared["send"], shared["recv"], shared["buf"], after, name="small_exchange_wait")
            early_sum = _sum_slots(everyone, name="small_sum")
            late_all = _small_exchange_wait(late_send, late_recv, late_buf, early_sum, name="norm_mix_exchange_wait")
            late_sum = _sum_slots(late_all, name="norm_mix_sum")
            summed = jnp.concatenate([late_sum, early_sum[late.shape[0]:]], axis=0)
            after = summed
        _, from_chips = _copies_wait(_chip_copies, send_sems, recv_sems, sums4, lands, after,
                                     name="rs_chips_wait_" + names[0])
        for i, n in enumerate(names):
            g, d, m_, v_ = _adamw_shard(_shard(n, w[n]), _shard(n, m[n]), _shard(n, v[n]), g8[i], from_sibling[i],
                                        from_chips[i], pos, name="adamw_" + n)
            out_g[n], out_d[n], out_m[n], out_v[n] = (_unshard(n, o) for o in (g, d, m_, v_))
            after = d
    *small_grads, loss_sum = _unpack(summed, small_like + [jax.ShapeDtypeStruct((), F32)])
    d_s, m_s, v_s = _adamw_small(small_w, summed[:small_w.shape[0]], _pack([m[n] for n in _SMALL]),
                                 _pack([v[n] for n in _SMALL]), name="adamw_small")
    for n, g, d, m_, v_ in zip(_SMALL, small_grads, _unpack(d_s, small_like), _unpack(m_s, small_like), _unpack(v_s, small_like)):
        out_g[n], out_d[n], out_m[n], out_v[n] = g, d, m_, v_

    return (loss_sum, grad_x[None], *[out_g[n] for n in _ORDER], *[out_d[n] for n in _ORDER],
            *[out_m[n] for n in _ORDER], *[out_v[n] for n in _ORDER])
```
